```python
import math
import jax, jax.numpy as jnp
from jax import lax
import numpy as np

D_MODEL = 1024
BATCH = 8
SEQ = 8192
DEPTH = 1

HEAD_DIM = 64
SB_HEADS = D_MODEL // (2 * HEAD_DIM)
SWA_HEADS = D_MODEL // (2 * HEAD_DIM)
SWA_KV_HEADS = max(1, SWA_HEADS // 4)
SB_WIDTH = SB_HEADS * HEAD_DIM
SWA_WIDTH = SWA_HEADS * HEAD_DIM
SWA_KV_WIDTH = SWA_KV_HEADS * HEAD_DIM
MIX_WIDTH = SB_WIDTH + SWA_WIDTH
IN_COLS = 3 * SB_WIDTH + SWA_WIDTH + 2 * SWA_KV_WIDTH
BLOCK = 128
WINDOW = 128
REL_BUCKETS = 32
REL_MAX_DIST = 128
D_FF = -((-8 * D_MODEL) // (3 * 256)) * 256
ALPHA = (2 * DEPTH) ** 0.25
BETA_INIT = (8 * DEPTH) ** -0.25
LN_EPS = 1e-5
RMS_EPS = 1e-6

kernel_name = "stickbreak_swa_sink_hybrid_deepnorm"


def layer_norm(x, g, b):
    xf = x.astype(jnp.float32)
    mu = jnp.mean(xf, axis=-1, keepdims=True)
    var = jnp.mean(jnp.square(xf - mu), axis=-1, keepdims=True)
    return ((xf - mu) * lax.rsqrt(var + LN_EPS)).astype(x.dtype) * g + b


def rms_norm(x, g):
    xf = x.astype(jnp.float32)
    y = xf * lax.rsqrt(jnp.mean(jnp.square(xf), axis=-1, keepdims=True) + RMS_EPS)
    return y.astype(x.dtype) * g


def t5_causal_bucket(distance):
    exact = REL_BUCKETS // 2
    d = jnp.maximum(distance, 0)
    d_f = jnp.maximum(d, 1).astype(jnp.float32)
    large = exact + (jnp.log(d_f / exact) / math.log(REL_MAX_DIST / exact)
                     * (REL_BUCKETS - exact)).astype(jnp.int32)
    large = jnp.minimum(large, REL_BUCKETS - 1)
    return jnp.where(d < exact, d, large)


def stick_breaking_attention(q, k, v):
    B, S, H, Dh = q.shape
    nb = S // BLOCK
    scale = Dh ** -0.5
    qb = q.reshape(B, nb, BLOCK, H, Dh).transpose(1, 0, 3, 2, 4)
    key_pos = jnp.arange(S)

    def one_block(args):
        q_blk, i = args
        z = jnp.einsum('bhqd,bkhd->bhqk', q_blk, k).astype(jnp.float32) * scale
        q_pos = i * BLOCK + jnp.arange(BLOCK)
        causal = key_pos[None, :] < q_pos[:, None]
        log_beta = jax.nn.log_sigmoid(z)
        log_1m_beta = jnp.where(causal, jax.nn.log_sigmoid(-z), 0.0)
        suffix = lax.cumsum(log_1m_beta, axis=3, reverse=True) - log_1m_beta
        attn = jnp.where(causal, jnp.exp(log_beta + suffix), 0.0)
        return jnp.einsum('bhqk,bkhd->bqhd', attn.astype(v.dtype), v)

    out = lax.map(one_block, (qb, jnp.arange(nb)))
    return out.transpose(1, 0, 2, 3, 4).reshape(B, S, H * Dh).astype(v.dtype)


def sliding_window_attention(q, k, v, sinks, rel_bias):
    B, S, H, Dh = q.shape
    KVH = k.shape[2]
    G = H // KVH
    nb = S // BLOCK
    scale = Dh ** -0.5
    qb = q.reshape(B, nb, BLOCK, KVH, G, Dh)

    def band(t):
        tb = t.reshape(B, nb, BLOCK, KVH, Dh)
        prev = jnp.pad(tb[:, :-1], ((0, 0), (1, 0), (0, 0), (0, 0), (0, 0)))
        return jnp.concatenate([prev, tb], axis=2)

    kb, vb = band(k), band(v)
    logits = jnp.einsum('bnqhgd,bnchd->bnhgqc', qb, kb).astype(jnp.float32) * scale

    qi = jnp.arange(BLOCK)[:, None]
    cj = jnp.arange(2 * BLOCK)[None, :]
    dist = qi + BLOCK - cj
    key_abs = jnp.arange(nb)[:, None, None] * BLOCK - BLOCK + cj[None]
    valid = (dist >= 0)[None] & (dist < WINDOW)[None] & (key_abs >= 0)
    bias = rel_bias.astype(jnp.float32)[t5_causal_bucket(dist)]
    bias = bias.transpose(2, 0, 1).reshape(KVH, G, BLOCK, 2 * BLOCK)
    logits = jnp.where(valid[None, :, None, None], logits + bias, -jnp.inf)

    sink = sinks.astype(jnp.float32).reshape(1, 1, KVH, G, 1, 1)
    m = jnp.maximum(jnp.max(logits, axis=-1, keepdims=True), sink)
    p = jnp.exp(logits - m)
    denom = jnp.sum(p, axis=-1, keepdims=True) + jnp.exp(sink - m)
    o = jnp.einsum('bnhgqc,bnchd->bnqhgd', (p / denom).astype(v.dtype), vb)
    return o.reshape(B, S, H * Dh).astype(v.dtype)


def hybrid_mixer(h, w_in, sb_norm_g, swa_norm_g, sinks, rel_bias, w_out):
    B, S, _ = h.shape
    proj = h @ w_in
    o1 = SB_WIDTH
    o2 = o1 + SB_WIDTH
    o3 = o2 + SB_WIDTH
    o4 = o3 + SWA_WIDTH
    o5 = o4 + SWA_KV_WIDTH
    q_sb, k_sb, v_sb, q_sw, k_sw, v_sw = jnp.split(proj, [o1, o2, o3, o4, o5], axis=-1)
    sb_out = stick_breaking_attention(
        q_sb.reshape(B, S, SB_HEADS, HEAD_DIM),
        k_sb.reshape(B, S, SB_HEADS, HEAD_DIM),
        v_sb.reshape(B, S, SB_HEADS, HEAD_DIM))
    swa_out = sliding_window_attention(
        q_sw.reshape(B, S, SWA_HEADS, HEAD_DIM),
        k_sw.reshape(B, S, SWA_KV_HEADS, HEAD_DIM),
        v_sw.reshape(B, S, SWA_KV_HEADS, HEAD_DIM),
        sinks, rel_bias)
    merged = jnp.concatenate([rms_norm(sb_out, sb_norm_g),
                              rms_norm(swa_out, swa_norm_g)], axis=-1)
    return merged @ w_out


def swiglu_ffn(h, w_gate_up, w_down):
    gate, up = jnp.split(h @ w_gate_up, 2, axis=-1)
    return (jax.nn.silu(gate) * up) @ w_down


def _fwd_setup_inputs(seed: int = 0) -> dict:
    key = jax.random.key(seed)
    ks = jax.random.split(key, 16)
    f32 = jnp.float32
    n = lambda k, shape, s: jax.random.normal(k, shape, f32) * s
    return {
        "x": n(ks[0], (BATCH, SEQ, D_MODEL), 1.0),
        "ln_in_g": 1.0 + n(ks[1], (D_MODEL,), 0.02),
        "ln_in_b": n(ks[2], (D_MODEL,), 0.02),
        "w_in": n(ks[3], (DEPTH, D_MODEL, IN_COLS), D_MODEL ** -0.5),
        "sb_norm_g": 1.0 + n(ks[4], (DEPTH, SB_WIDTH), 0.02),
        "swa_norm_g": 1.0 + n(ks[5], (DEPTH, SWA_WIDTH), 0.02),
        "sinks": n(ks[6], (DEPTH, SWA_HEADS), 0.5),
        "rel_bias": n(ks[7], (REL_BUCKETS, SWA_HEADS), 0.5),
        "w_out": n(ks[8], (DEPTH, MIX_WIDTH, D_MODEL), MIX_WIDTH ** -0.5 * BETA_INIT),
        "ln1_g": 1.0 + n(ks[9], (DEPTH, D_MODEL), 0.02),
        "ln1_b": n(ks[10], (DEPTH, D_MODEL), 0.02),
        "w_gate_up": n(ks[11], (DEPTH, D_MODEL, 2 * D_FF), D_MODEL ** -0.5),
        "w_down": n(ks[12], (DEPTH, D_FF, D_MODEL), D_FF ** -0.5 * BETA_INIT),
        "ln2_g": 1.0 + n(ks[13], (DEPTH, D_MODEL), 0.02),
        "ln2_b": n(ks[14], (DEPTH, D_MODEL), 0.02),
    }


def _fwd_reference(x, ln_in_g, ln_in_b, w_in, sb_norm_g, swa_norm_g, sinks, rel_bias,
              w_out, ln1_g, ln1_b, w_gate_up, w_down, ln2_g, ln2_b):
    h = layer_norm(x, ln_in_g, ln_in_b)
    for l in range(DEPTH):
        mix = hybrid_mixer(h, w_in[l], sb_norm_g[l], swa_norm_g[l], sinks[l], rel_bias, w_out[l])
        h = layer_norm(ALPHA * h + mix, ln1_g[l], ln1_b[l])
        ffn = swiglu_ffn(h, w_gate_up[l], w_down[l])
        h = layer_norm(ALPHA * h + ffn, ln2_g[l], ln2_b[l])
    return h


import jax as _jax
import jax.numpy as _jnp

TWIN_FORMAT = 'train_step'
FWD_PARAMS = ['x', 'ln_in_g', 'ln_in_b', 'w_in', 'sb_norm_g', 'swa_norm_g', 'sinks', 'rel_bias', 'w_out', 'ln1_g', 'ln1_b', 'w_gate_up', 'w_down', 'ln2_g', 'ln2_b']
TWIN_WEIGHTS = ['ln_in_g', 'ln_in_b', 'w_in', 'sb_norm_g', 'swa_norm_g', 'sinks', 'rel_bias', 'w_out', 'ln1_g', 'ln1_b', 'w_gate_up', 'w_down', 'ln2_g', 'ln2_b']
TWIN_DIFF_INPUT = 'x'
TWIN_INPUTS = ['x', 'ln_in_g', 'ln_in_b', 'w_in', 'sb_norm_g', 'swa_norm_g', 'sinks', 'rel_bias', 'w_out', 'ln1_g', 'ln1_b', 'w_gate_up', 'w_down', 'ln2_g', 'ln2_b', 'loss_target', 'm_ln_in_g', 'm_ln_in_b', 'm_w_in', 'm_sb_norm_g', 'm_swa_norm_g', 'm_sinks', 'm_rel_bias', 'm_w_out', 'm_ln1_g', 'm_ln1_b', 'm_w_gate_up', 'm_w_down', 'm_ln2_g', 'm_ln2_b', 'v_ln_in_g', 'v_ln_in_b', 'v_w_in', 'v_sb_norm_g', 'v_swa_norm_g', 'v_sinks', 'v_rel_bias', 'v_w_out', 'v_ln1_g', 'v_ln1_b', 'v_w_gate_up', 'v_w_down', 'v_ln2_g', 'v_ln2_b']
TWIN_OUTPUTS = ['loss', 'grad_x', 'grad_ln_in_g', 'grad_ln_in_b', 'grad_w_in', 'grad_sb_norm_g', 'grad_swa_norm_g', 'grad_sinks', 'grad_rel_bias', 'grad_w_out', 'grad_ln1_g', 'grad_ln1_b', 'grad_w_gate_up', 'grad_w_down', 'grad_ln2_g', 'grad_ln2_b', 'delta_ln_in_g', 'delta_ln_in_b', 'delta_w_in', 'delta_sb_norm_g', 'delta_swa_norm_g', 'delta_sinks', 'delta_rel_bias', 'delta_w_out', 'delta_ln1_g', 'delta_ln1_b', 'delta_w_gate_up', 'delta_w_down', 'delta_ln2_g', 'delta_ln2_b', 'new_m_ln_in_g', 'new_m_ln_in_b', 'new_m_w_in', 'new_m_sb_norm_g', 'new_m_swa_norm_g', 'new_m_sinks', 'new_m_rel_bias', 'new_m_w_out', 'new_m_ln1_g', 'new_m_ln1_b', 'new_m_w_gate_up', 'new_m_w_down', 'new_m_ln2_g', 'new_m_ln2_b', 'new_v_ln_in_g', 'new_v_ln_in_b', 'new_v_w_in', 'new_v_sb_norm_g', 'new_v_swa_norm_g', 'new_v_sinks', 'new_v_rel_bias', 'new_v_w_out', 'new_v_ln1_g', 'new_v_ln1_b', 'new_v_w_gate_up', 'new_v_w_down', 'new_v_ln2_g', 'new_v_ln2_b']
TWIN_LEAF_KINDS = {'loss': 'loss', 'grad_x': 'grad_x', 'grad_ln_in_g': 'grad_w', 'grad_ln_in_b': 'grad_w', 'grad_w_in': 'grad_w', 'grad_sb_norm_g': 'grad_w', 'grad_swa_norm_g': 'grad_w', 'grad_sinks': 'grad_w', 'grad_rel_bias': 'grad_w', 'grad_w_out': 'grad_w', 'grad_ln1_g': 'grad_w', 'grad_ln1_b': 'grad_w', 'grad_w_gate_up': 'grad_w', 'grad_w_down': 'grad_w', 'grad_ln2_g': 'grad_w', 'grad_ln2_b': 'grad_w', 'delta_ln_in_g': 'delta_w', 'delta_ln_in_b': 'delta_w', 'delta_w_in': 'delta_w', 'delta_sb_norm_g': 'delta_w', 'delta_swa_norm_g': 'delta_w', 'delta_sinks': 'delta_w', 'delta_rel_bias': 'delta_w', 'delta_w_out': 'delta_w', 'delta_ln1_g': 'delta_w', 'delta_ln1_b': 'delta_w', 'delta_w_gate_up': 'delta_w', 'delta_w_down': 'delta_w', 'delta_ln2_g': 'delta_w', 'delta_ln2_b': 'delta_w', 'new_m_ln_in_g': 'new_m', 'new_m_ln_in_b': 'new_m', 'new_m_w_in': 'new_m', 'new_m_sb_norm_g': 'new_m', 'new_m_swa_norm_g': 'new_m', 'new_m_sinks': 'new_m', 'new_m_rel_bias': 'new_m', 'new_m_w_out': 'new_m', 'new_m_ln1_g': 'new_m', 'new_m_ln1_b': 'new_m', 'new_m_w_gate_up': 'new_m', 'new_m_w_down': 'new_m', 'new_m_ln2_g': 'new_m', 'new_m_ln2_b': 'new_m', 'new_v_ln_in_g': 'new_v', 'new_v_ln_in_b': 'new_v', 'new_v_w_in': 'new_v', 'new_v_sb_norm_g': 'new_v', 'new_v_swa_norm_g': 'new_v', 'new_v_sinks': 'new_v', 'new_v_rel_bias': 'new_v', 'new_v_w_out': 'new_v', 'new_v_ln1_g': 'new_v', 'new_v_ln1_b': 'new_v', 'new_v_w_gate_up': 'new_v', 'new_v_w_down': 'new_v', 'new_v_ln2_g': 'new_v', 'new_v_ln2_b': 'new_v'}


def _forward(args):
    return _fwd_reference(*[args[k] for k in FWD_PARAMS])


def _output_shape():
    def fwd():
        inp = _fwd_setup_inputs(0)
        return _fwd_reference(*[inp[k] for k in FWD_PARAMS])
    out = _jax.eval_shape(fwd)
    return out.shape, out.dtype

N_MICROBATCH = 1
ADAM_LR = 0.001
ADAM_B1 = 0.9
ADAM_B2 = 0.999
ADAM_EPS = 1e-08
ADAM_WD = 0.01
ADAM_STEP = 10
PER_EXAMPLE_BATCH_AXIS = {'x': 0, 'loss_target': 0}
SHARED_INPUTS = []
_WEIGHT_DTYPES = {'ln_in_g': _jnp.float32, 'ln_in_b': _jnp.float32, 'w_in': _jnp.float32, 'sb_norm_g': _jnp.float32, 'swa_norm_g': _jnp.float32, 'sinks': _jnp.float32, 'rel_bias': _jnp.float32, 'w_out': _jnp.float32, 'ln1_g': _jnp.float32, 'ln1_b': _jnp.float32, 'w_gate_up': _jnp.float32, 'w_down': _jnp.float32, 'ln2_g': _jnp.float32, 'ln2_b': _jnp.float32}
MOMENT_SCALE = {'ln_in_g': 1.510195e+00, 'ln_in_b': 2.155728e+00, 'w_in': 1.138501e-01, 'sb_norm_g': 1.147342e-01, 'swa_norm_g': 1.203575e-01, 'sinks': 4.529059e-02, 'rel_bias': 1.811578e-01, 'w_out': 2.012763e-01, 'ln1_g': 2.092933e+00, 'ln1_b': 9.849614e-01, 'w_gate_up': 4.394504e-02, 'w_down': 1.207391e-01, 'ln2_g': 6.402535e+01, 'ln2_b': 2.441889e+00}


def _to_microbatches(a, axis):
    t = _jnp.moveaxis(a, axis, 0)
    t = t.reshape((N_MICROBATCH, t.shape[0] // N_MICROBATCH) + t.shape[1:])
    return _jnp.moveaxis(t, 1, axis + 1)


def setup_inputs(seed: int = 0) -> dict:
    inp = _fwd_setup_inputs(seed)
    key = _jax.random.fold_in(_jax.random.key(seed), 7919)
    shape, _ = _output_shape()
    out = dict(inp)
    out["loss_target"] = _jax.random.normal(_jax.random.fold_in(key, 0), shape, _jnp.float32)
    for i, name in enumerate(TWIN_WEIGHTS):
        w = inp[name].astype(_jnp.float32)
        if MOMENT_SCALE is None:
            s = _jnp.sqrt(_jnp.mean(_jnp.square(w)) + 1e-30)
        else:
            s = MOMENT_SCALE[name]
        km, kv = _jax.random.split(_jax.random.fold_in(key, i + 1))
        out[name] = w
        out["m_" + name] = s * _jax.random.normal(km, w.shape, _jnp.float32)
        out["v_" + name] = (s * s) * _jax.random.uniform(kv, w.shape, _jnp.float32, 0.5, 1.5)
    if N_MICROBATCH > 1:
        for name, axis in PER_EXAMPLE_BATCH_AXIS.items():
            out[name] = _to_microbatches(out[name], axis)
    return {'x': out['x'], 'ln_in_g': out['ln_in_g'], 'ln_in_b': out['ln_in_b'], 'w_in': out['w_in'], 'sb_norm_g': out['sb_norm_g'], 'swa_norm_g': out['swa_norm_g'], 'sinks': out['sinks'], 'rel_bias': out['rel_bias'], 'w_out': out['w_out'], 'ln1_g': out['ln1_g'], 'ln1_b': out['ln1_b'], 'w_gate_up': out['w_gate_up'], 'w_down': out['w_down'], 'ln2_g': out['ln2_g'], 'ln2_b': out['ln2_b'], 'loss_target': out['loss_target'], 'm_ln_in_g': out['m_ln_in_g'], 'm_ln_in_b': out['m_ln_in_b'], 'm_w_in': out['m_w_in'], 'm_sb_norm_g': out['m_sb_norm_g'], 'm_swa_norm_g': out['m_swa_norm_g'], 'm_sinks': out['m_sinks'], 'm_rel_bias': out['m_rel_bias'], 'm_w_out': out['m_w_out'], 'm_ln1_g': out['m_ln1_g'], 'm_ln1_b': out['m_ln1_b'], 'm_w_gate_up': out['m_w_gate_up'], 'm_w_down': out['m_w_down'], 'm_ln2_g': out['m_ln2_g'], 'm_ln2_b': out['m_ln2_b'], 'v_ln_in_g': out['v_ln_in_g'], 'v_ln_in_b': out['v_ln_in_b'], 'v_w_in': out['v_w_in'], 'v_sb_norm_g': out['v_sb_norm_g'], 'v_swa_norm_g': out['v_swa_norm_g'], 'v_sinks': out['v_sinks'], 'v_rel_bias': out['v_rel_bias'], 'v_w_out': out['v_w_out'], 'v_ln1_g': out['v_ln1_g'], 'v_ln1_b': out['v_ln1_b'], 'v_w_gate_up': out['v_w_gate_up'], 'v_w_down': out['v_w_down'], 'v_ln2_g': out['v_ln2_g'], 'v_ln2_b': out['v_ln2_b']}


def _loss(weights, diff, rest, loss_target):
    with _jax.named_scope("forward"):
        args = {**rest, TWIN_DIFF_INPUT: diff, **{k: w.astype(_WEIGHT_DTYPES[k]) for k, w in weights.items()}}
        y = _forward(args)
    with _jax.named_scope("loss_head"):
        err = _jnp.square(y.astype(_jnp.float32) - loss_target)
        return 0.5 * _jnp.sum(_jnp.mean(err, axis=-1)) if err.ndim else 0.5 * err


def _adamw(w, g, m, v):
    m = ADAM_B1 * m + (1.0 - ADAM_B1) * g
    v = ADAM_B2 * v + (1.0 - ADAM_B2) * _jnp.square(g)
    m_hat = m / (1.0 - ADAM_B1 ** ADAM_STEP)
    v_hat = v / (1.0 - ADAM_B2 ** ADAM_STEP)
    delta = -ADAM_LR * (m_hat / (_jnp.sqrt(v_hat) + ADAM_EPS) + ADAM_WD * w)
    return delta, m, v


def reference(x, ln_in_g, ln_in_b, w_in, sb_norm_g, swa_norm_g, sinks, rel_bias, w_out, ln1_g, ln1_b, w_gate_up, w_down, ln2_g, ln2_b, loss_target, m_ln_in_g, m_ln_in_b, m_w_in, m_sb_norm_g, m_swa_norm_g, m_sinks, m_rel_bias, m_w_out, m_ln1_g, m_ln1_b, m_w_gate_up, m_w_down, m_ln2_g, m_ln2_b, v_ln_in_g, v_ln_in_b, v_w_in, v_sb_norm_g, v_swa_norm_g, v_sinks, v_rel_bias, v_w_out, v_ln1_g, v_ln1_b, v_w_gate_up, v_w_down, v_ln2_g, v_ln2_b):
    given = dict(x=x, ln_in_g=ln_in_g, ln_in_b=ln_in_b, w_in=w_in, sb_norm_g=sb_norm_g, swa_norm_g=swa_norm_g, sinks=sinks, rel_bias=rel_bias, w_out=w_out, ln1_g=ln1_g, ln1_b=ln1_b, w_gate_up=w_gate_up, w_down=w_down, ln2_g=ln2_g, ln2_b=ln2_b, loss_target=loss_target, m_ln_in_g=m_ln_in_g, m_ln_in_b=m_ln_in_b, m_w_in=m_w_in, m_sb_norm_g=m_sb_norm_g, m_swa_norm_g=m_swa_norm_g, m_sinks=m_sinks, m_rel_bias=m_rel_bias, m_w_out=m_w_out, m_ln1_g=m_ln1_g, m_ln1_b=m_ln1_b, m_w_gate_up=m_w_gate_up, m_w_down=m_w_down, m_ln2_g=m_ln2_g, m_ln2_b=m_ln2_b, v_ln_in_g=v_ln_in_g, v_ln_in_b=v_ln_in_b, v_w_in=v_w_in, v_sb_norm_g=v_sb_norm_g, v_swa_norm_g=v_swa_norm_g, v_sinks=v_sinks, v_rel_bias=v_rel_bias, v_w_out=v_w_out, v_ln1_g=v_ln1_g, v_ln1_b=v_ln1_b, v_w_gate_up=v_w_gate_up, v_w_down=v_w_down, v_ln2_g=v_ln2_g, v_ln2_b=v_ln2_b)
    weights = {n: given[n] for n in TWIN_WEIGHTS}
    shared = {n: given[n] for n in SHARED_INPUTS}
    per_example = {n: given[n] for n in ['x']}
    grad_fn = _jax.value_and_grad(_loss, argnums=(0, 1))

    def one_microbatch(ex, loss_target):
        ex = dict(ex)
        diff = ex.pop(TWIN_DIFF_INPUT)
        return grad_fn(weights, diff, {**shared, **ex}, loss_target)

    if N_MICROBATCH == 1:
        loss, (grad_w, grad_x) = one_microbatch(per_example, given["loss_target"])
    else:
        def body(carry, xs):
            loss_sum, grad_sum = carry
            l_k, (gw_k, gx_k) = one_microbatch(xs[0], xs[1])
            with _jax.named_scope("update"):
                return (loss_sum + l_k, _jax.tree.map(_jnp.add, grad_sum, gw_k)), gx_k

        init = (_jnp.zeros((), _jnp.float32), _jax.tree.map(_jnp.zeros_like, weights))
        (loss, grad_w), grad_x = _jax.lax.scan(body, init, (per_example, given["loss_target"]))
    with _jax.named_scope("update"):
        delta_w, new_m, new_v = {}, {}, {}
        for n in TWIN_WEIGHTS:
            delta_w[n], new_m[n], new_v[n] = _adamw(weights[n], grad_w[n], given["m_" + n], given["v_" + n])
    return (loss, grad_x, *[grad_w[n] for n in TWIN_WEIGHTS], *[delta_w[n] for n in TWIN_WEIGHTS],
            *[new_m[n] for n in TWIN_WEIGHTS], *[new_v[n] for n in TWIN_WEIGHTS])
```

```python
import functools
import math

import numpy as np
import jax
import jax.numpy as jnp
from jax import lax
from jax.experimental import pallas as pl
from jax.experimental.pallas import tpu as pltpu

F32 = jnp.float32
_MXU = jnp.bfloat16

D_MODEL = 1024
HEAD_DIM = 64
SB_HEADS = 8
SWA_HEADS = 8
SWA_KV_HEADS = 2
SWA_GROUP = SWA_HEADS // SWA_KV_HEADS
SB_WIDTH = SB_HEADS * HEAD_DIM
SWA_WIDTH = SWA_HEADS * HEAD_DIM
SWA_KV_WIDTH = SWA_KV_HEADS * HEAD_DIM
IN_COLS = 3 * SB_WIDTH + SWA_WIDTH + 2 * SWA_KV_WIDTH
BLOCK = 128
REL_BUCKETS = 32
REL_MAX_DIST = 128
D_FF = 2816
FF_CHUNK = D_FF // 2
ALPHA = 2.0 ** 0.25
LN_EPS = 1e-5
RMS_EPS = 1e-6
SCALE = HEAD_DIM ** -0.5

ADAM_LR = 0.001
ADAM_B1 = 0.9
ADAM_B2 = 0.999
ADAM_EPS = 1e-08
ADAM_WD = 0.01
ADAM_STEP = 10

N_CHIPS = 4
SLAB_ROWS = (D_MODEL * IN_COLS // N_CHIPS + D_MODEL * D_MODEL // N_CHIPS
             + D_MODEL * 2 * D_FF // N_CHIPS + D_FF * D_MODEL // N_CHIPS) // D_MODEL
HALF_ROWS = SLAB_ROWS // 2
SMALL_ROWS = 16

MESH = pl.DeviceIdType.MESH


def _sds(shape, dtype):
    return jax.ShapeDtypeStruct(shape, dtype)


def _cp(sem=None, vmem_mb=48):
    kw = dict(vmem_limit_bytes=vmem_mb * 1024 * 1024)
    if sem is not None:
        kw["dimension_semantics"] = sem
    return pltpu.CompilerParams(**kw)


def _dot(a, b):
    return jnp.dot(a, b, preferred_element_type=F32)


def _dot_nt(a, b):
    return lax.dot_general(a, b, (((1,), (1,)), ((), ())), preferred_element_type=F32)


def _dot_tn(a, b):
    return lax.dot_general(a, b, (((0,), (0,)), ((), ())), preferred_element_type=F32)


def _ln_hat(x):
    mu = jnp.mean(x, axis=-1, keepdims=True)
    xc = x - mu
    var = jnp.mean(xc * xc, axis=-1, keepdims=True)
    rstd = lax.rsqrt(var + LN_EPS)
    return xc * rstd, rstd


def _ln_bwd(xhat, rstd, dy, g):
    dxh = dy * g
    m1 = jnp.mean(dxh, axis=-1, keepdims=True)
    m2 = jnp.mean(dxh * xhat, axis=-1, keepdims=True)
    return rstd * (dxh - m1 - xhat * m2)


def _colsum(x):
    return jnp.sum(x, axis=0, keepdims=True)


def _split2(x):
    hi = x.astype(_MXU)
    lo = (x - hi.astype(F32)).astype(_MXU)
    return hi, lo


def _rows(tm, n):
    return pl.BlockSpec((tm, n), lambda i: (i, 0))


def _fixed(*shape):
    nd = len(shape)
    return pl.BlockSpec(shape, lambda i: (0,) * nd)


def _ln_in_proj(x, g, b, w):
    S = x.shape[0]
    N = w.shape[1]
    tm = min(S, 512)

    def body(x_ref, g_ref, b_ref, w_ref, h_ref, hb_ref, p_ref):
        xhat, _ = _ln_hat(x_ref[...])
        h = xhat * g_ref[...] + b_ref[...]
        h_ref[...] = h
        hb = h.astype(_MXU)
        hb_ref[...] = hb
        p_ref[...] = _dot(hb, w_ref[...]).astype(p_ref.dtype)

    return pl.pallas_call(
        body, name="ln_in_proj", grid=(S // tm,),
        in_specs=[_rows(tm, D_MODEL), _fixed(1, D_MODEL), _fixed(1, D_MODEL), _fixed(D_MODEL, N)],
        out_specs=[_rows(tm, D_MODEL), _rows(tm, D_MODEL), _rows(tm, N)],
        out_shape=[_sds((S, D_MODEL), F32), _sds((S, D_MODEL), _MXU), _sds((S, N), _MXU)],
        compiler_params=_cp(("parallel",)),
    )(x, g, b, w)


def _rms(x, g):
    r = lax.rsqrt(jnp.mean(x * x, axis=-1, keepdims=True) + RMS_EPS)
    return x * r * g, r


def _mix_out(sb, sw, gsb, gsw, w_out, h0):
    S = sb.shape[0]
    tm = min(S, 512)

    def body(sb_ref, sw_ref, gsb_ref, gsw_ref, w_ref, h0_ref, pre_ref, mg_ref):
        ysb, _ = _rms(sb_ref[...], gsb_ref[...])
        ysw, _ = _rms(sw_ref[...], gsw_ref[...])
        ysb = ysb.astype(_MXU)
        ysw = ysw.astype(_MXU)
        mg_ref[:, :SB_WIDTH] = ysb
        mg_ref[:, SB_WIDTH:] = ysw
        mix = _dot(ysb, w_ref[:SB_WIDTH, :]) + _dot(ysw, w_ref[SB_WIDTH:, :])
        pre_ref[...] = ALPHA * h0_ref[...] + mix

    return pl.pallas_call(
        body, name="mix_out", grid=(S // tm,),
        in_specs=[_rows(tm, SB_WIDTH), _rows(tm, SWA_WIDTH), _fixed(1, SB_WIDTH), _fixed(1, SWA_WIDTH),
                  _fixed(D_MODEL, D_MODEL), _rows(tm, D_MODEL)],
        out_specs=[_rows(tm, D_MODEL), _rows(tm, D_MODEL)],
        out_shape=[_sds((S, D_MODEL), F32), _sds((S, D_MODEL), _MXU)],
        compiler_params=_cp(("parallel",)),
    )(sb, sw, gsb, gsw, w_out, h0)


def _sigmoid(x):
    return 1.0 / (1.0 + jnp.exp(-x))


def _ffn_up(pre1, g1, b1, wgu):
    S = pre1.shape[0]
    tm = min(S, 512)

    def body(p_ref, g_ref, b_ref, wg_ref, wu_ref, h1_ref, gate_ref, up_ref, a_ref):
        xhat, _ = _ln_hat(p_ref[...])
        h1 = (xhat * g_ref[...] + b_ref[...]).astype(_MXU)
        h1_ref[...] = h1
        gate = _dot(h1, wg_ref[0])
        up = _dot(h1, wu_ref[0])
        gate_ref[...] = gate.astype(gate_ref.dtype)
        up_ref[...] = up.astype(up_ref.dtype)
        a_ref[...] = (gate * _sigmoid(gate) * up).astype(a_ref.dtype)

    chunk = lambda i, j: (i, j)
    return pl.pallas_call(
        body, name="ffn_up", grid=(S // tm, 2),
        in_specs=[pl.BlockSpec((tm, D_MODEL), lambda i, j: (i, 0)),
                  pl.BlockSpec((1, D_MODEL), lambda i, j: (0, 0)),
                  pl.BlockSpec((1, D_MODEL), lambda i, j: (0, 0)),
                  pl.BlockSpec((1, D_MODEL, FF_CHUNK), lambda i, j: (j, 0, 0)),
                  pl.BlockSpec((1, D_MODEL, FF_CHUNK), lambda i, j: (j + 2, 0, 0))],
        out_specs=[pl.BlockSpec((tm, D_MODEL), lambda i, j: (i, 0)),
                   pl.BlockSpec((tm, FF_CHUNK), chunk), pl.BlockSpec((tm, FF_CHUNK), chunk),
                   pl.BlockSpec((tm, FF_CHUNK), chunk)],
        out_shape=[_sds((S, D_MODEL), _MXU), _sds((S, D_FF), _MXU), _sds((S, D_FF), _MXU), _sds((S, D_FF), _MXU)],
        compiler_params=_cp(("parallel", "arbitrary")),
    )(pre1, g1, b1, wgu, wgu)


def _ffn_down_loss(a, w_down, pre1, g1, b1, g2, b2, tgt):
    S = a.shape[0]
    tm = min(S, 512)

    def body(a_ref, w_ref, p_ref, g1_ref, b1_ref, g2_ref, b2_ref, t_ref, d_ref, db_ref, dg2_ref, db2_ref, err_ref):
        @pl.when(pl.program_id(0) == 0)
        def _():
            dg2_ref[...] = jnp.zeros_like(dg2_ref)
            db2_ref[...] = jnp.zeros_like(db2_ref)
            err_ref[...] = jnp.zeros_like(err_ref)

        xhat1, _ = _ln_hat(p_ref[...])
        h1 = xhat1 * g1_ref[...] + b1_ref[...]
        pre2 = ALPHA * h1 + _dot(a_ref[...], w_ref[...])
        xhat2, rstd2 = _ln_hat(pre2)
        err = xhat2 * g2_ref[...] + b2_ref[...] - t_ref[...]
        dh2 = err * (1.0 / D_MODEL)
        dp2 = _ln_bwd(xhat2, rstd2, dh2, g2_ref[...])
        d_ref[...] = dp2
        db_ref[...] = dp2.astype(db_ref.dtype)
        dg2_ref[...] += _colsum(dh2 * xhat2)
        db2_ref[...] += _colsum(dh2)
        err_ref[...] += _colsum(err * err)

    vec = _fixed(1, D_MODEL)
    return pl.pallas_call(
        body, name="ffn_down_loss", grid=(S // tm,),
        in_specs=[_rows(tm, D_FF), _fixed(D_FF, D_MODEL), _rows(tm, D_MODEL), vec, vec, vec, vec, _rows(tm, D_MODEL)],
        out_specs=[_rows(tm, D_MODEL), _rows(tm, D_MODEL), vec, vec, vec],
        out_shape=[_sds((S, D_MODEL), F32), _sds((S, D_MODEL), _MXU), _sds((1, D_MODEL), F32), _sds((1, D_MODEL), F32),
                   _sds((1, D_MODEL), F32)],
        compiler_params=_cp(("arbitrary",)),
    )(a, w_down, pre1, g1, b1, g2, b2, tgt)


def _ffn_down_bwd(dp2b, w_down, gate, up):
    S = dp2b.shape[0]
    tm = min(S, 512)

    def body(d_ref, w_ref, g_ref, u_ref, dg_ref, du_ref):
        da = _dot_nt(d_ref[...], w_ref[...])
        g = g_ref[...].astype(F32)
        u = u_ref[...].astype(F32)
        sg = _sigmoid(g)
        du_ref[...] = (da * g * sg).astype(du_ref.dtype)
        dg_ref[...] = (da * u * (sg * (1.0 + g * (1.0 - sg)))).astype(dg_ref.dtype)

    chunk = pl.BlockSpec((tm, FF_CHUNK), lambda i, j: (i, j))
    return pl.pallas_call(
        body, name="ffn_down_bwd", grid=(S // tm, 2),
        in_specs=[pl.BlockSpec((tm, D_MODEL), lambda i, j: (i, 0)),
                  pl.BlockSpec((FF_CHUNK, D_MODEL), lambda i, j: (j, 0)), chunk, chunk],
        out_specs=[chunk, chunk],
        out_shape=[_sds((S, D_FF), _MXU), _sds((S, D_FF), _MXU)],
        compiler_params=_cp(("parallel", "arbitrary")),
    )(dp2b, w_down, gate, up)


def _ffn_up_bwd(dgate, dup, wgu, dp2, pre1, g1):
    S = dgate.shape[0]
    tm = min(S, 256)

    def body(dg_ref, du_ref, w_ref, d2_ref, p_ref, g_ref, d1_ref, d1b_ref, dg1_ref, db1_ref):
        @pl.when(pl.program_id(0) == 0)
        def _():
            dg1_ref[...] = jnp.zeros_like(dg1_ref)
            db1_ref[...] = jnp.zeros_like(db1_ref)

        dh1 = ALPHA * d2_ref[...]
        for j in range(2):
            cols = slice(j * FF_CHUNK, (j + 1) * FF_CHUNK)
            dh1 += _dot_nt(dg_ref[:, cols], w_ref[j])
            dh1 += _dot_nt(du_ref[:, cols], w_ref[j + 2])
        xhat, rstd = _ln_hat(p_ref[...])
        dp1 = _ln_bwd(xhat, rstd, dh1, g_ref[...])
        d1_ref[...] = dp1
        d1b_ref[...] = dp1.astype(d1b_ref.dtype)
        dg1_ref[...] += _colsum(dh1 * xhat)
        db1_ref[...] += _colsum(dh1)

    vec = _fixed(1, D_MODEL)
    return pl.pallas_call(
        body, name="ffn_up_bwd", grid=(S // tm,),
        in_specs=[_rows(tm, D_FF), _rows(tm, D_FF), _fixed(4, D_MODEL, FF_CHUNK), _rows(tm, D_MODEL),
                  _rows(tm, D_MODEL), vec],
        out_specs=[_rows(tm, D_MODEL), _rows(tm, D_MODEL), vec, vec],
        out_shape=[_sds((S, D_MODEL), F32), _sds((S, D_MODEL), _MXU), _sds((1, D_MODEL), F32), _sds((1, D_MODEL), F32)],
        compiler_params=_cp(("arbitrary",), vmem_mb=56),
    )(dgate, dup, wgu, dp2, pre1, g1)


def _rms_bwd(x, g, dy):
    n = x.shape[-1]
    r = lax.rsqrt(jnp.mean(x * x, axis=-1, keepdims=True) + RMS_EPS)
    u = dy * g
    dx = r * u - x * (r * r * r) * (jnp.sum(u * x, axis=-1, keepdims=True) * (1.0 / n))
    return dx, _colsum(dy * x * r)


def _mix_bwd(dp1b, w_out, sb, sw, gsb, gsw):
    S = sb.shape[0]
    tm = min(S, 512)

    def body(d_ref, w_ref, sb_ref, sw_ref, gsb_ref, gsw_ref, dsb_ref, dsw_ref, dgsb_ref, dgsw_ref):
        @pl.when(pl.program_id(0) == 0)
        def _():
            dgsb_ref[...] = jnp.zeros_like(dgsb_ref)
            dgsw_ref[...] = jnp.zeros_like(dgsw_ref)

        dm = _dot_nt(d_ref[...], w_ref[...])
        dsb, dgsb = _rms_bwd(sb_ref[...], gsb_ref[...], dm[:, :SB_WIDTH])
        dsw, dgsw = _rms_bwd(sw_ref[...], gsw_ref[...], dm[:, SB_WIDTH:])
        dsb_ref[...] = dsb.astype(dsb_ref.dtype)
        dsw_ref[...] = dsw.astype(dsw_ref.dtype)
        dgsb_ref[...] += dgsb
        dgsw_ref[...] += dgsw

    return pl.pallas_call(
        body, name="mix_bwd", grid=(S // tm,),
        in_specs=[_rows(tm, D_MODEL), _fixed(D_MODEL, D_MODEL), _rows(tm, SB_WIDTH), _rows(tm, SWA_WIDTH),
                  _fixed(1, SB_WIDTH), _fixed(1, SWA_WIDTH)],
        out_specs=[_rows(tm, SB_WIDTH), _rows(tm, SWA_WIDTH), _fixed(1, SB_WIDTH), _fixed(1, SWA_WIDTH)],
        out_shape=[_sds((S, SB_WIDTH), _MXU), _sds((S, SWA_WIDTH), _MXU), _sds((1, SB_WIDTH), F32),
                   _sds((1, SWA_WIDTH), F32)],
        compiler_params=_cp(("arbitrary",)),
    )(dp1b, w_out, sb, sw, gsb, gsw)


def _in_proj_bwd(dproj, w_in, dp1, x, g):
    S = x.shape[0]
    N = dproj.shape[1]
    tm = min(S, 512)

    def body(dpj_ref, w_ref, d1_ref, x_ref, g_ref, gx_ref, dg_ref, db_ref):
        @pl.when(pl.program_id(0) == 0)
        def _():
            dg_ref[...] = jnp.zeros_like(dg_ref)
            db_ref[...] = jnp.zeros_like(db_ref)

        dh0 = _dot_nt(dpj_ref[...], w_ref[...]) + ALPHA * d1_ref[...]
        xhat, rstd = _ln_hat(x_ref[...])
        gx_ref[...] = _ln_bwd(xhat, rstd, dh0, g_ref[...])
        dg_ref[...] += _colsum(dh0 * xhat)
        db_ref[...] += _colsum(dh0)

    vec = _fixed(1, D_MODEL)
    return pl.pallas_call(
        body, name="in_proj_bwd", grid=(S // tm,),
        in_specs=[_rows(tm, N), _fixed(D_MODEL, N), _rows(tm, D_MODEL), _rows(tm, D_MODEL), vec],
        out_specs=[_rows(tm, D_MODEL), vec, vec],
        out_shape=[_sds((S, D_MODEL), F32), _sds((1, D_MODEL), F32), _sds((1, D_MODEL), F32)],
        compiler_params=_cp(("arbitrary",)),
    )(dproj, w_in, dp1, x, g)


def _matmul_tn(a, b, name, tk, tn):
    T, K = a.shape
    N = b.shape[1]
    tt = min(T, 512)

    def body(a_ref, b_ref, o_ref):
        @pl.when(pl.program_id(2) == 0)
        def _():
            o_ref[...] = jnp.zeros_like(o_ref)

        o_ref[...] += _dot_tn(a_ref[...], b_ref[...])

    return pl.pallas_call(
        body, name=name, grid=(K // tk, N // tn, T // tt),
        in_specs=[pl.BlockSpec((tt, tk), lambda k, n, t: (t, k)), pl.BlockSpec((tt, tn), lambda k, n, t: (t, n))],
        out_specs=pl.BlockSpec((tk, tn), lambda k, n, t: (k, n)),
        out_shape=_sds((K, N), F32),
        compiler_params=_cp(("parallel", "parallel", "arbitrary")),
    )(a, b)


def _sb_scores(kk, qt, rsum, upper, causal):
    zt = _dot(kk, qt)
    e = jnp.exp(-jnp.abs(zt))
    lb = jnp.minimum(zt, 0.0) - jnp.log(1.0 + e)
    l1m = lb - zt
    if causal is not None:
        l1m = jnp.where(causal, l1m, 0.0)
    hi, lo = _split2(l1m)
    suf = rsum + _dot(upper, hi) + _dot(upper, lo)
    a = jnp.exp(lb + suf)
    if causal is not None:
        a = jnp.where(causal, a, 0.0)
    return lb, l1m, a


def _tri_masks(t):
    r = lax.broadcasted_iota(jnp.int32, (t, t), 0)
    c = lax.broadcasted_iota(jnp.int32, (t, t), 1)
    return r, c


def _sb_fwd(qT, kb, vTb):
    Hh, _, S = qT.shape
    nk, T = kb.shape[1], kb.shape[2]
    nq = S // T

    def body(qT_ref, k_ref, vT_ref, oT_ref, rs_ref):
        i = pl.program_id(1)
        qt = (qT_ref[0].astype(F32) * SCALE).astype(_MXU)
        r, c = _tri_masks(T)
        upper = (c > r).astype(_MXU)
        causal = r < c

        def blk(j, carry, mask):
            rsum, acc = carry
            rs_ref[0, 0, j] = jnp.broadcast_to(rsum, (8, T))
            _, l1m, a = _sb_scores(k_ref[0, j], qt, rsum, upper, mask)
            acc = acc + _dot(vT_ref[0, j], a.astype(_MXU))
            return rsum + _colsum(l1m), acc

        carry = blk(i, (jnp.zeros((1, T), F32), jnp.zeros((HEAD_DIM, T), F32)), causal)
        carry = lax.fori_loop(0, i, lambda s, cr: blk(i - 1 - s, cr, None), carry)
        oT_ref[0] = carry[1]

    return pl.pallas_call(
        body, name="sb_fwd", grid=(Hh, nq),
        in_specs=[pl.BlockSpec((1, HEAD_DIM, T), lambda h, i: (h, 0, i)),
                  pl.BlockSpec((1, nk, T, HEAD_DIM), lambda h, i: (h, 0, 0, 0)),
                  pl.BlockSpec((1, nk, HEAD_DIM, T), lambda h, i: (h, 0, 0, 0))],
        out_specs=[pl.BlockSpec((1, HEAD_DIM, T), lambda h, i: (h, 0, i)),
                   pl.BlockSpec((1, 1, nk, 8, T), lambda h, i: (h, i, 0, 0, 0))],
        out_shape=[_sds((Hh, HEAD_DIM, S), F32), _sds((Hh, nq, nk, 8, T), F32)],
        compiler_params=_cp(("parallel", "arbitrary")),
    )(qT, kb, vTb)


def _sb_bwd(q, qT, kb, kTb, vb, do, doT, rsave):
    Hh, _, S = qT.shape
    nk, T = kb.shape[1], kb.shape[2]
    nq = S // T

    def body(q_ref, qT_ref, k_ref, kT_ref, v_ref, do_ref, doT_ref, rs_ref, dqT_ref, dk_ref, dv_ref):
        i = pl.program_id(1)

        @pl.when(i == 0)
        def _():
            dk_ref[...] = jnp.zeros_like(dk_ref)
            dv_ref[...] = jnp.zeros_like(dv_ref)

        qs = (q_ref[0].astype(F32) * SCALE).astype(_MXU)
        qt = (qT_ref[0].astype(F32) * SCALE).astype(_MXU)
        dout = do_ref[0]
        doutT = doT_ref[0]
        r, c = _tri_masks(T)
        upper = (c > r).astype(_MXU)
        lower = (c < r).astype(_MXU)
        causal = r < c

        def blk(j, carry, mask):
            psum, dq = carry
            rsum = rs_ref[0, 0, j][0:1, :]
            lb, _, a = _sb_scores(k_ref[0, j], qt, rsum, upper, mask)
            e_t = _dot(v_ref[0, j], doutT) * a
            hi, lo = _split2(e_t)
            p_t = psum + _dot(lower, hi) + _dot(lower, lo)
            sig = jnp.exp(lb)
            dz = e_t * (1.0 - sig) - p_t * sig
            if mask is not None:
                dz = jnp.where(mask, dz, 0.0)
            dzb = dz.astype(_MXU)
            dq = dq + _dot(kT_ref[0, j], dzb)
            dk_ref[0, j] += _dot(dzb, qs)
            dv_ref[0, j] += _dot(a.astype(_MXU), dout)
            return psum + _colsum(e_t), dq

        carry = (jnp.zeros((1, T), F32), jnp.zeros((HEAD_DIM, T), F32))
        carry = lax.fori_loop(0, i, lambda s, cr: blk(s, cr, None), carry)
        carry = blk(i, carry, causal)
        dqT_ref[0] = carry[1] * SCALE

    rowblk = pl.BlockSpec((1, T, HEAD_DIM), lambda h, i: (h, i, 0))
    colblk = pl.BlockSpec((1, HEAD_DIM, T), lambda h, i: (h, 0, i))
    kblk = pl.BlockSpec((1, nk, T, HEAD_DIM), lambda h, i: (h, 0, 0, 0))
    kTblk = pl.BlockSpec((1, nk, HEAD_DIM, T), lambda h, i: (h, 0, 0, 0))
    return pl.pallas_call(
        body, name="sb_bwd", grid=(Hh, nq),
        in_specs=[rowblk, colblk, kblk, kTblk, kblk, rowblk, colblk,
                  pl.BlockSpec((1, 1, nk, 8, T), lambda h, i: (h, i, 0, 0, 0))],
        out_specs=[colblk, kblk, kblk],
        out_shape=[_sds((Hh, HEAD_DIM, S), F32), _sds((Hh, nk, T, HEAD_DIM), F32), _sds((Hh, nk, T, HEAD_DIM), F32)],
        compiler_params=_cp(("parallel", "arbitrary")),
    )(q, qT, kb, kTb, vb, do, doT, rsave)


def _bucket_table():
    qi = np.arange(BLOCK)[:, None]
    cj = np.arange(2 * BLOCK)[None, :]
    dist = qi + BLOCK - cj
    exact = REL_BUCKETS // 2
    d = np.maximum(dist, 0)
    d_f = np.maximum(d, 1).astype(np.float32)
    large = exact + (np.log(d_f / np.float32(exact)) / np.float32(math.log(REL_MAX_DIST / exact))
                     * np.float32(REL_BUCKETS - exact)).astype(np.int32)
    large = np.minimum(large, REL_BUCKETS - 1)
    return np.where(d < exact, d, large).astype(np.int32)


def _swa_bias(rel_bias, bucket):
    def body(rb_ref, bk_ref, o_ref):
        bk = bk_ref[...]
        for h in range(SWA_HEADS):
            t = jnp.zeros((BLOCK, 2 * BLOCK), F32)
            for b in range(REL_BUCKETS):
                t = jnp.where(bk == b, rb_ref[b, h], t)
            o_ref[h] = t

    return pl.pallas_call(
        body, name="swa_bias",
        in_specs=[pl.BlockSpec(memory_space=pltpu.SMEM), pl.BlockSpec(memory_space=pltpu.VMEM)],
        out_specs=pl.BlockSpec(memory_space=pltpu.VMEM),
        out_shape=_sds((SWA_HEADS, BLOCK, 2 * BLOCK), F32),
    )(rel_bias, bucket)


def _swa_probs(q_ref, kp_ref, kc_ref, bias_ref, sink_ref, i):
    qs = (q_ref[0].astype(F32) * SCALE).astype(_MXU)
    bias = bias_ref[0]
    lp = _dot_nt(qs, kp_ref[0]) + bias[:, :BLOCK]
    lc = _dot_nt(qs, kc_ref[0]) + bias[:, BLOCK:]
    r, c = _tri_masks(BLOCK)
    lp = jnp.where(jnp.logical_and(c > r, i > 0), lp, -jnp.inf)
    lc = jnp.where(c <= r, lc, -jnp.inf)
    sink = sink_ref[0][:, :1]
    m = jnp.maximum(jnp.maximum(jnp.max(lp, axis=1, keepdims=True), jnp.max(lc, axis=1, keepdims=True)), sink)
    pp = jnp.exp(lp - m)
    pc = jnp.exp(lc - m)
    ps = jnp.exp(sink - m)
    denom = jnp.sum(pp, axis=1, keepdims=True) + jnp.sum(pc, axis=1, keepdims=True) + ps
    return qs, pp / denom, pc / denom, ps / denom


def _swa_fwd(q, k, v, bias, sink):
    S = q.shape[1]
    nb = S // BLOCK

    def body(q_ref, kp_ref, kc_ref, vp_ref, vc_ref, bias_ref, sink_ref, o_ref):
        i = pl.program_id(1)
        _, wp, wc, _ = _swa_probs(q_ref, kp_ref, kc_ref, bias_ref, sink_ref, i)
        o_ref[0] = _dot(wp.astype(_MXU), vp_ref[0]) + _dot(wc.astype(_MXU), vc_ref[0])

    prev = pl.BlockSpec((1, BLOCK, HEAD_DIM), lambda h, i: (h // SWA_GROUP, jnp.maximum(i - 1, 0), 0))
    cur = pl.BlockSpec((1, BLOCK, HEAD_DIM), lambda h, i: (h // SWA_GROUP, i, 0))
    qblk = pl.BlockSpec((1, BLOCK, HEAD_DIM), lambda h, i: (h, i, 0))
    return pl.pallas_call(
        body, name="swa_fwd", grid=(SWA_HEADS, nb),
        in_specs=[qblk, prev, cur, prev, cur,
                  pl.BlockSpec((1, BLOCK, 2 * BLOCK), lambda h, i: (h, 0, 0)),
                  pl.BlockSpec((1, 1, BLOCK), lambda h, i: (h, 0, 0))],
        out_specs=qblk,
        out_shape=_sds((SWA_HEADS, S, HEAD_DIM), F32),
        compiler_params=_cp(("parallel", "parallel")),
    )(q, k, k, v, v, bias, sink)


def _swa_bwd(q, k, v, bias, sink, do):
    S = q.shape[1]
    nb = S // BLOCK

    def body(q_ref, kp_ref, kc_ref, vp_ref, vc_ref, bias_ref, sink_ref, do_ref, dq_ref, dk_ref, dv_ref, dbias_ref,
             dsink_ref):
        g = pl.program_id(1)
        i = pl.program_id(2)

        @pl.when(jnp.logical_and(g == 0, i == 0))
        def _():
            dk_ref[...] = jnp.zeros_like(dk_ref)
            dv_ref[...] = jnp.zeros_like(dv_ref)

        @pl.when(i == 0)
        def _():
            dbias_ref[...] = jnp.zeros_like(dbias_ref)
            dsink_ref[...] = jnp.zeros_like(dsink_ref)

        qs, wp, wc, ws = _swa_probs(q_ref, kp_ref, kc_ref, bias_ref, sink_ref, i)
        dout = do_ref[0]
        dwp = _dot_nt(dout, vp_ref[0])
        dwc = _dot_nt(dout, vc_ref[0])
        delta = jnp.sum(wp * dwp, axis=1, keepdims=True) + jnp.sum(wc * dwc, axis=1, keepdims=True)
        dlp = wp * (dwp - delta)
        dlc = wc * (dwc - delta)
        dbias_ref[0, :, :BLOCK] += dlp
        dbias_ref[0, :, BLOCK:] += dlc
        dsink_ref[0] += jnp.broadcast_to(-ws * delta, (BLOCK, BLOCK))
        dlpb = dlp.astype(_MXU)
        dlcb = dlc.astype(_MXU)
        dq_ref[0] = (_dot(dlpb, kp_ref[0]) + _dot(dlcb, kc_ref[0])) * SCALE
        dk_ref[0, i] += _dot_tn(dlcb, qs)
        dv_ref[0, i] += _dot_tn(wc.astype(_MXU), dout)

        @pl.when(i > 0)
        def _():
            dk_ref[0, i - 1] += _dot_tn(dlpb, qs)
            dv_ref[0, i - 1] += _dot_tn(wp.astype(_MXU), dout)

    hq = lambda kv, g, i: kv * SWA_GROUP + g
    prev = pl.BlockSpec((1, BLOCK, HEAD_DIM), lambda kv, g, i: (kv, jnp.maximum(i - 1, 0), 0))
    cur = pl.BlockSpec((1, BLOCK, HEAD_DIM), lambda kv, g, i: (kv, i, 0))
    qblk = pl.BlockSpec((1, BLOCK, HEAD_DIM), lambda kv, g, i: (hq(kv, g, i), i, 0))
    kvacc = pl.BlockSpec((1, nb, BLOCK, HEAD_DIM), lambda kv, g, i: (kv, 0, 0, 0))
    return pl.pallas_call(
        body, name="swa_bwd", grid=(SWA_KV_HEADS, SWA_GROUP, nb),
        in_specs=[qblk, prev, cur, prev, cur,
                  pl.BlockSpec((1, BLOCK, 2 * BLOCK), lambda kv, g, i: (hq(kv, g, i), 0, 0)),
                  pl.BlockSpec((1, 1, BLOCK), lambda kv, g, i: (hq(kv, g, i), 0, 0)), qblk],
        out_specs=[qblk, kvacc, kvacc,
                   pl.BlockSpec((1, BLOCK, 2 * BLOCK), lambda kv, g, i: (hq(kv, g, i), 0, 0)),
                   pl.BlockSpec((1, BLOCK, BLOCK), lambda kv, g, i: (hq(kv, g, i), 0, 0))],
        out_shape=[_sds((SWA_HEADS, S, HEAD_DIM), F32), _sds((SWA_KV_HEADS, nb, BLOCK, HEAD_DIM), F32),
                   _sds((SWA_KV_HEADS, nb, BLOCK, HEAD_DIM), F32), _sds((SWA_HEADS, BLOCK, 2 * BLOCK), F32),
                   _sds((SWA_HEADS, BLOCK, BLOCK), F32)],
        compiler_params=_cp(("arbitrary", "arbitrary", "arbitrary")),
    )(q, k, k, v, v, bias, sink, do)


def _swa_small_grads(dbias, dsink, bucket):
    rows = REL_BUCKETS + 8

    def total(x):
        return jnp.sum(jnp.sum(x, axis=1, keepdims=True), axis=0, keepdims=True)

    def body(db_ref, ds_ref, bk_ref, o_ref):
        bk = bk_ref[...]
        r = lax.broadcasted_iota(jnp.int32, (rows, BLOCK), 0)
        c = lax.broadcasted_iota(jnp.int32, (rows, BLOCK), 1)
        out = jnp.zeros((rows, BLOCK), F32)
        for h in range(SWA_HEADS):
            db = db_ref[h]
            for b in range(REL_BUCKETS):
                s = total(jnp.where(bk == b, db, 0.0))
                out = jnp.where(jnp.logical_and(r == b, c == h), s, out)
            s = jnp.sum(ds_ref[h][:, :1], axis=0, keepdims=True)
            out = jnp.where(jnp.logical_and(r == REL_BUCKETS, c == h), s, out)
        o_ref[...] = out

    vm = pl.BlockSpec(memory_space=pltpu.VMEM)
    return pl.pallas_call(body, name="swa_small_grads", in_specs=[vm, vm, vm], out_specs=vm,
                          out_shape=_sds((rows, BLOCK), F32))(dbias, dsink, bucket)


def _tile_rows(n):
    for t in (512, 368, 256, 184, 128, 64, 32, 16, 8):
        if n % t == 0:
            return t
    return n


def _cast_rows(x, dtype, name):
    R, C = x.shape
    tr = _tile_rows(R)

    def body(x_ref, o_ref):
        o_ref[...] = x_ref[...].astype(o_ref.dtype)

    return pl.pallas_call(body, name=name, grid=(R // tr,), in_specs=[_rows(tr, C)], out_specs=_rows(tr, C),
                          out_shape=_sds((R, C), dtype), compiler_params=_cp(("parallel",)))(x)


def _pair_sum(g, recv, name):
    n, R, C = g.shape
    tr = _tile_rows(R)

    def body(a_ref, b_ref, o_ref):
        o_ref[...] = (a_ref[...] + b_ref[...]).astype(o_ref.dtype)

    blk = pl.BlockSpec((1, tr, C), lambda j, i: (j, i, 0))
    return pl.pallas_call(body, name=name, grid=(n, R // tr), in_specs=[blk, blk], out_specs=blk,
                          out_shape=_sds((n, R, C), _MXU),
                          compiler_params=_cp(("parallel", "parallel")))(g, recv)


def _chip_sum(parts, name):
    n, R, C = parts.shape
    tr = _tile_rows(R)

    def body(p_ref, o_ref):
        acc = p_ref[0].astype(F32)
        for j in range(1, n):
            acc = acc + p_ref[j].astype(F32)
        o_ref[...] = acc

    return pl.pallas_call(body, name=name, grid=(R // tr,),
                          in_specs=[pl.BlockSpec((n, tr, C), lambda i: (0, i, 0))], out_specs=_rows(tr, C),
                          out_shape=_sds((R, C), F32), compiler_params=_cp(("parallel",)))(parts)


def _adamw_math(w, g, m, v):
    m = ADAM_B1 * m + (1.0 - ADAM_B1) * g
    v = ADAM_B2 * v + (1.0 - ADAM_B2) * (g * g)
    m_hat = m / (1.0 - ADAM_B1 ** ADAM_STEP)
    v_hat = v / (1.0 - ADAM_B2 ** ADAM_STEP)
    delta = -ADAM_LR * (m_hat / (jnp.sqrt(v_hat) + ADAM_EPS) + ADAM_WD * w)
    return delta, m, v


def _adamw(w, g, m, v, name):
    R, C = w.shape
    tr = _tile_rows(R)

    def body(w_ref, g_ref, m_ref, v_ref, d_ref, nm_ref, nv_ref):
        d, nm, nv = _adamw_math(w_ref[...], g_ref[...], m_ref[...], v_ref[...])
        d_ref[...] = d
        nm_ref[...] = nm
        nv_ref[...] = nv

    blk = _rows(tr, C)
    return pl.pallas_call(body, name=name, grid=(R // tr,), in_specs=[blk] * 4, out_specs=[blk] * 3,
                          out_shape=[_sds((R, C), F32)] * 3, compiler_params=_cp(("parallel",)))(w, g, m, v)


def _place():
    x, y, c = lax.axis_index("x"), lax.axis_index("y"), lax.axis_index("c")
    chips = [(1 - x, y), (x, 1 - y), (1 - x, 1 - y)]
    return x, y, c, chips


def _gather_weights(slab):
    rows, cols = slab.shape
    half = rows // 2

    def body(in_ref, out_ref, send_sems, recv_sems, local_sem):
        x, y, c, chips = _place()
        sibling = (x, y, 1 - c)

        def part(chip, hc):
            return out_ref.at[2 * chip[0] + chip[1], pl.ds(pl.multiple_of(hc * half, 16), half), :]

        def copy(k, chip, hc, to, src=None):
            return pltpu.make_async_remote_copy(
                src_ref=part(chip, hc) if src is None else src, dst_ref=part(chip, hc),
                send_sem=send_sems.at[k], recv_sem=recv_sems.at[k], device_id=to, device_id_type=MESH)

        mine = pltpu.make_async_copy(in_ref, out_ref.at[2 * x + y], local_sem)
        mine.start()
        my_half = in_ref.at[pl.ds(pl.multiple_of(c * half, 16), half), :]
        first = [copy(j, (x, y), c, (*chip, c), src=my_half) for j, chip in enumerate(chips)]
        for cp in first:
            cp.start()
        passed = [copy(3 + j, chip, c, sibling) for j, chip in enumerate(chips)]
        for j, chip in enumerate(chips):
            copy(j, chip, c, (x, y, c)).wait_recv()
            passed[j].start()
        for j, chip in enumerate(chips):
            copy(3 + j, chip, 1 - c, (x, y, c)).wait_recv()
        for cp in first + passed:
            cp.wait_send()
        mine.wait()

    any_spec = pl.BlockSpec(memory_space=pl.ANY)
    return pl.pallas_call(
        body, name="gather_weights", in_specs=[any_spec], out_specs=any_spec,
        out_shape=_sds((N_CHIPS, rows, cols), slab.dtype),
        scratch_shapes=[pltpu.SemaphoreType.DMA((6,)), pltpu.SemaphoreType.DMA((6,)), pltpu.SemaphoreType.DMA],
    )(slab)


def _swap_halves(g):
    n, rows, cols = g.shape
    half = rows // 2

    def body(g_ref, out_ref, send_sem, recv_sem):
        x, y, c, _ = _place()
        theirs = g_ref.at[:, pl.ds(pl.multiple_of((1 - c) * half, 8), half), :]
        cp = pltpu.make_async_remote_copy(src_ref=theirs, dst_ref=out_ref, send_sem=send_sem, recv_sem=recv_sem,
                                          device_id=(x, y, 1 - c), device_id_type=MESH)
        cp.start()
        cp.wait()

    any_spec = pl.BlockSpec(memory_space=pl.ANY)
    return pl.pallas_call(
        body, name="swap_halves", in_specs=[any_spec], out_specs=any_spec, out_shape=_sds((n, half, cols), g.dtype),
        scratch_shapes=[pltpu.SemaphoreType.DMA, pltpu.SemaphoreType.DMA],
    )(g)


def _scatter_partials(p):
    n, half, cols = p.shape

    def body(p_ref, out_ref, send_sems, recv_sems, local_sem):
        x, y, c, chips = _place()
        me = 2 * x + y
        mine = pltpu.make_async_copy(p_ref.at[me], out_ref.at[me], local_sem)
        mine.start()
        sends = [pltpu.make_async_remote_copy(
            src_ref=p_ref.at[2 * chip[0] + chip[1]], dst_ref=out_ref.at[me], send_sem=send_sems.at[j],
            recv_sem=recv_sems.at[j], device_id=(*chip, c), device_id_type=MESH) for j, chip in enumerate(chips)]
        for cp in sends:
            cp.start()
        for j, chip in enumerate(chips):
            pltpu.make_async_remote_copy(
                src_ref=p_ref.at[me], dst_ref=out_ref.at[2 * chip[0] + chip[1]], send_sem=send_sems.at[j],
                recv_sem=recv_sems.at[j], device_id=(*chip, c), device_id_type=MESH).wait_recv()
        for cp in sends:
            cp.wait_send()
        mine.wait()

    any_spec = pl.BlockSpec(memory_space=pl.ANY)
    return pl.pallas_call(
        body, name="scatter_partials", in_specs=[any_spec], out_specs=any_spec, out_shape=_sds((n, half, cols), p.dtype),
        scratch_shapes=[pltpu.SemaphoreType.DMA((3,)), pltpu.SemaphoreType.DMA((3,)), pltpu.SemaphoreType.DMA],
    )(p)


def _join_halves(f):
    half, cols = f.shape

    def body(f_ref, out_ref, send_sem, recv_sem, local_sem):
        x, y, c, _ = _place()
        mine_rows = out_ref.at[pl.ds(pl.multiple_of(c * half, 8), half), :]
        mine = pltpu.make_async_copy(f_ref, mine_rows, local_sem)
        mine.start()
        cp = pltpu.make_async_remote_copy(src_ref=f_ref, dst_ref=mine_rows, send_sem=send_sem, recv_sem=recv_sem,
                                          device_id=(x, y, 1 - c), device_id_type=MESH)
        cp.start()
        their_rows = out_ref.at[pl.ds(pl.multiple_of((1 - c) * half, 8), half), :]
        pltpu.make_async_remote_copy(src_ref=f_ref, dst_ref=their_rows, send_sem=send_sem, recv_sem=recv_sem,
                                     device_id=(x, y, 1 - c), device_id_type=MESH).wait_recv()
        cp.wait_send()
        mine.wait()

    any_spec = pl.BlockSpec(memory_space=pl.ANY)
    return pl.pallas_call(
        body, name="join_halves", in_specs=[any_spec], out_specs=any_spec, out_shape=_sds((2 * half, cols), f.dtype),
        scratch_shapes=[pltpu.SemaphoreType.DMA, pltpu.SemaphoreType.DMA, pltpu.SemaphoreType.DMA],
    )(f)


def _allreduce_small(block):
    m_per, n = block.shape

    def body(x_ref, sum_ref, loss_ref, all_ref, send_sems, recv_sems, local_sem):
        x, y, c, chips = _place()
        me, sibling = (x, y, c), (x, y, 1 - c)

        def rows(px, py, pc):
            return all_ref.at[pl.ds(pl.multiple_of((4 * px + 2 * py + pc) * m_per, 8), m_per), :]

        def copy(k, blk, to, src=None):
            return pltpu.make_async_remote_copy(
                src_ref=rows(*blk) if src is None else src, dst_ref=rows(*blk), send_sem=send_sems.at[k],
                recv_sem=recv_sems.at[k], device_id=to, device_id_type=MESH)

        mine = pltpu.make_async_copy(x_ref, rows(*me), local_sem)
        mine.start()
        first = [copy(0, me, sibling, src=x_ref)]
        first += [copy(1 + j, me, (*chip, c), src=x_ref) for j, chip in enumerate(chips)]
        for cp in first:
            cp.start()
        passed = [copy(4 + j, (*chip, c), sibling) for j, chip in enumerate(chips)]
        for j, chip in enumerate(chips):
            copy(1 + j, (*chip, c), me).wait_recv()
            passed[j].start()
        copy(0, sibling, me).wait_recv()
        for j, chip in enumerate(chips):
            copy(4 + j, (*chip, 1 - c), me).wait_recv()
        for cp in first + passed:
            cp.wait_send()
        mine.wait()

        acc = all_ref[0:m_per, :]
        for d in range(1, 8):
            acc = acc + all_ref[d * m_per:(d + 1) * m_per, :]
        sum_ref[...] = acc
        tot = jnp.sum(acc[8:9, :], axis=1, keepdims=True) * (0.5 / D_MODEL)
        loss_ref[...] = jnp.broadcast_to(tot, loss_ref.shape)

    vm = pl.BlockSpec(memory_space=pltpu.VMEM)
    return pl.pallas_call(
        body, name="allreduce_small", in_specs=[vm], out_specs=[vm, vm],
        out_shape=[_sds((m_per, n), F32), _sds((8, 128), F32)],
        scratch_shapes=[pltpu.VMEM((8 * m_per, n), F32), pltpu.SemaphoreType.DMA((7,)), pltpu.SemaphoreType.DMA((7,)),
                        pltpu.SemaphoreType.DMA],
    )(block)


def _slab_sections():
    return [D_MODEL * IN_COLS // N_CHIPS // D_MODEL, D_MODEL // N_CHIPS, 2 * D_FF // N_CHIPS, D_FF // N_CHIPS]


def _to_slab(w_in_s, w_out_s, w_gu_s, w_down_s):
    return jnp.concatenate([w_in_s.reshape(-1, D_MODEL), w_out_s, w_gu_s.reshape(-1, D_MODEL), w_down_s], axis=0)


def _from_slab(slab):
    lead = slab.shape[:-2]
    r0, r1, r2, r3 = _slab_sections()
    o1, o2, o3 = r0, r0 + r1, r0 + r1 + r2
    return (slab[..., :o1, :].reshape(*lead, D_MODEL, IN_COLS // N_CHIPS), slab[..., o1:o2, :],
            slab[..., o2:o3, :].reshape(*lead, D_MODEL, FF_CHUNK), slab[..., o3:, :])


def _heads_rows(x, nh):
    S = x.shape[0]
    return x.reshape(S, nh, HEAD_DIM).transpose(1, 0, 2)


def _heads_cols(x, nh):
    S = x.shape[0]
    return x.reshape(S, nh, HEAD_DIM).transpose(1, 2, 0)


def _key_blocks(x, nh, t):
    S = x.shape[0]
    return x.reshape(S // t, t, nh, HEAD_DIM).transpose(2, 0, 1, 3)


def _key_blocks_t(x, nh, t):
    S = x.shape[0]
    return x.reshape(S // t, t, nh, HEAD_DIM).transpose(2, 0, 3, 1)


def _pad_row(v):
    v = v.reshape(1, -1)
    return jnp.pad(v, ((0, 0), (0, D_MODEL - v.shape[1])))


def _pack_small(ln_in_g, ln_in_b, sb_g, swa_g, sinks, rel_bias, ln1_g, ln1_b, ln2_g, ln2_b, extra):
    rows = [_pad_row(ln_in_g), _pad_row(ln_in_b), jnp.concatenate([sb_g.reshape(1, -1), swa_g.reshape(1, -1)], axis=1),
            _pad_row(jnp.concatenate([rel_bias.reshape(1, -1), sinks.reshape(1, -1)], axis=1)),
            _pad_row(ln1_g), _pad_row(ln1_b), _pad_row(ln2_g), _pad_row(ln2_b), _pad_row(extra)]
    rows.append(jnp.zeros((SMALL_ROWS - len(rows), D_MODEL), F32))
    return jnp.concatenate(rows, axis=0)


def _unpack_small(blk):
    nrb = REL_BUCKETS * SWA_HEADS
    return (blk[0], blk[1], blk[2:3, :SB_WIDTH], blk[2:3, SB_WIDTH:], blk[3:4, nrb:nrb + SWA_HEADS],
            blk[3, :nrb].reshape(REL_BUCKETS, SWA_HEADS), blk[4:5], blk[5:6], blk[6:7], blk[7:8])


def kernel(x, ln_in_g, ln_in_b, w_in, sb_norm_g, swa_norm_g, sinks, rel_bias, w_out, ln1_g, ln1_b, w_gate_up, w_down, ln2_g, ln2_b, loss_target, m_ln_in_g, m_ln_in_b, m_w_in, m_sb_norm_g, m_swa_norm_g, m_sinks, m_rel_bias, m_w_out, m_ln1_g, m_ln1_b, m_w_gate_up, m_w_down, m_ln2_g, m_ln2_b, v_ln_in_g, v_ln_in_b, v_w_in, v_sb_norm_g, v_swa_norm_g, v_sinks, v_rel_bias, v_w_out, v_ln1_g, v_ln1_b, v_w_gate_up, v_w_down, v_ln2_g, v_ln2_b):
    S = x.shape[1]
    x2 = x.reshape(S, D_MODEL)
    tgt = loss_target.reshape(S, D_MODEL)
    T = min(S, 256)
    bucket = jnp.asarray(_bucket_table())
    row = lambda v: v.reshape(1, -1)

    slab = _cast_rows(_to_slab(w_in[0], w_out[0], w_gate_up[0], w_down[0]), _MXU, "cast_weights")
    w_in_sh, w_out_sh, w_gu_sh, w_down_sh = _from_slab(_gather_weights(slab))
    w_in_f = jnp.concatenate([w_in_sh[j] for j in range(N_CHIPS)], axis=1)
    w_out_f = w_out_sh.reshape(D_MODEL, D_MODEL)
    w_down_f = w_down_sh.reshape(D_FF, D_MODEL)

    h0, h0b, proj = _ln_in_proj(x2, row(ln_in_g), row(ln_in_b), w_in_f)
    o1, o2, o3, o4, o5 = SB_WIDTH, 2 * SB_WIDTH, 3 * SB_WIDTH, 3 * SB_WIDTH + SWA_WIDTH, 3 * SB_WIDTH + SWA_WIDTH + SWA_KV_WIDTH
    q_sb, k_sb, v_sb = proj[:, :o1], proj[:, o1:o2], proj[:, o2:o3]
    q_sw, k_sw, v_sw = proj[:, o3:o4], proj[:, o4:o5], proj[:, o5:]
    qT_sb = _heads_cols(q_sb, SB_HEADS)
    kb_sb = _key_blocks(k_sb, SB_HEADS, T)
    oT_sb, rsave = _sb_fwd(qT_sb, kb_sb, _key_blocks_t(v_sb, SB_HEADS, T))
    sb_out = oT_sb.transpose(2, 0, 1).reshape(S, SB_WIDTH)

    bias = _swa_bias(rel_bias, bucket)
    sink_rows = jnp.broadcast_to(sinks.reshape(SWA_HEADS, 1, 1), (SWA_HEADS, 1, BLOCK))
    qh_sw, kh_sw, vh_sw = _heads_rows(q_sw, SWA_HEADS), _heads_rows(k_sw, SWA_KV_HEADS), _heads_rows(v_sw, SWA_KV_HEADS)
    swa_out = _swa_fwd(qh_sw, kh_sw, vh_sw, bias, sink_rows).transpose(1, 0, 2).reshape(S, SWA_WIDTH)

    pre1, merged = _mix_out(sb_out, swa_out, sb_norm_g, swa_norm_g, w_out_f, h0)
    h1b, gate, up, act = _ffn_up(pre1, ln1_g, ln1_b, w_gu_sh)
    dp2, dp2b, dg2, db2, errsum = _ffn_down_loss(act, w_down_f, pre1, ln1_g, ln1_b, ln2_g, ln2_b, tgt)

    g_w_down = _matmul_tn(act, dp2b, "grad_w_down", FF_CHUNK, D_MODEL)
    dgate, dup = _ffn_down_bwd(dp2b, w_down_f, gate, up)
    g_w_gate = _matmul_tn(h1b, dgate, "grad_w_gate", D_MODEL, FF_CHUNK)
    g_w_up = _matmul_tn(h1b, dup, "grad_w_up", D_MODEL, FF_CHUNK)
    dp1, dp1b, dg1, db1 = _ffn_up_bwd(dgate, dup, w_gu_sh, dp2, pre1, ln1_g)
    g_w_out = _matmul_tn(merged, dp1b, "grad_w_out", D_MODEL, D_MODEL)
    dsb, dsw, dgsb, dgsw = _mix_bwd(dp1b, w_out_f, sb_out, swa_out, sb_norm_g, swa_norm_g)

    dqh_sw, dkh_sw, dvh_sw, dbias, dsink = _swa_bwd(qh_sw, kh_sw, vh_sw, bias, sink_rows, _heads_rows(dsw, SWA_HEADS))
    swa_small = _swa_small_grads(dbias, dsink, bucket)
    dqT_sb, dk_sb, dv_sb = _sb_bwd(_heads_rows(q_sb, SB_HEADS), qT_sb, kb_sb, _key_blocks_t(k_sb, SB_HEADS, T),
                                   _key_blocks(v_sb, SB_HEADS, T), _heads_rows(dsb, SB_HEADS),
                                   _heads_cols(dsb, SB_HEADS), rsave)
    tok = lambda t, nh: t.reshape(nh, S, HEAD_DIM).transpose(1, 0, 2).reshape(S, nh * HEAD_DIM)
    dproj = jnp.concatenate([dqT_sb.transpose(2, 0, 1).reshape(S, SB_WIDTH), tok(dk_sb, SB_HEADS), tok(dv_sb, SB_HEADS),
                             tok(dqh_sw, SWA_HEADS), tok(dkh_sw, SWA_KV_HEADS), tok(dvh_sw, SWA_KV_HEADS)],
                            axis=1).astype(_MXU)
    g_w_in = _matmul_tn(h0b, dproj, "grad_w_in", D_MODEL, IN_COLS // 2)
    grad_x, dg_in, db_in = _in_proj_bwd(dproj, w_in_f, dp1, x2, row(ln_in_g))

    cin, cff = IN_COLS // N_CHIPS, D_FF // N_CHIPS
    g_slab = jnp.stack([_to_slab(g_w_in[:, j * cin:(j + 1) * cin], g_w_out[j * (D_MODEL // N_CHIPS):(j + 1) * (D_MODEL // N_CHIPS)],
                                 (g_w_gate if j < 2 else g_w_up)[:, (j % 2) * FF_CHUNK:(j % 2 + 1) * FF_CHUNK],
                                 g_w_down[j * cff:(j + 1) * cff]) for j in range(N_CHIPS)])
    c = lax.axis_index("c")
    own = lax.dynamic_slice_in_dim(g_slab, c * HALF_ROWS, HALF_ROWS, axis=1)
    partial = _pair_sum(own, _swap_halves(g_slab), "pair_sum")
    g_shard = _join_halves(_chip_sum(_scatter_partials(partial), "chip_sum"))
    gs_in, gs_out, gs_gu, gs_down = _from_slab(g_shard)

    nrb = REL_BUCKETS * SWA_HEADS
    small = _pack_small(dg_in, db_in, dgsb, dgsw, swa_small[REL_BUCKETS, :SWA_HEADS],
                        swa_small[:REL_BUCKETS, :SWA_HEADS], dg1, db1, dg2, db2, errsum)
    g_small, loss_tile = _allreduce_small(small)
    loss = loss_tile[0, 0]

    big = []
    for name, w, g, m, v in (("adamw_w_in", w_in, gs_in, m_w_in, v_w_in), ("adamw_w_out", w_out, gs_out, m_w_out, v_w_out),
                             ("adamw_w_gate_up", w_gate_up, gs_gu, m_w_gate_up, v_w_gate_up),
                             ("adamw_w_down", w_down, gs_down, m_w_down, v_w_down)):
        d, nm, nv = _adamw(w[0], g, m[0], v[0], name)
        big.append((g[None], d[None], nm[None], nv[None]))
    zero = jnp.zeros((1,), F32)
    w_small = _pack_small(ln_in_g, ln_in_b, sb_norm_g, swa_norm_g, sinks, rel_bias, ln1_g, ln1_b, ln2_g, ln2_b, zero)
    m_small = _pack_small(m_ln_in_g, m_ln_in_b, m_sb_norm_g, m_swa_norm_g, m_sinks, m_rel_bias, m_ln1_g, m_ln1_b,
                          m_ln2_g, m_ln2_b, zero)
    v_small = _pack_small(v_ln_in_g, v_ln_in_b, v_sb_norm_g, v_swa_norm_g, v_sinks, v_rel_bias, v_ln1_g, v_ln1_b,
                          v_ln2_g, v_ln2_b, zero)
    small_out = [_unpack_small(t) for t in (g_small,) + tuple(_adamw(w_small, g_small, m_small, v_small, "adamw_small"))]

    def kind(k):
        s = small_out[k]
        return [s[0], s[1], big[0][k], s[2], s[3], s[4], s[5], big[1][k], s[6], s[7], big[2][k], big[3][k], s[8], s[9]]

    return (loss, grad_x.reshape(1, S, D_MODEL), *kind(0), *kind(1), *kind(2), *kind(3))
```

```python
import functools
import math

import numpy as np
import jax
import jax.numpy as jnp
from jax import lax
from jax.experimental import pallas as pl
from jax.experimental.pallas import tpu as pltpu

F32 = jnp.float32
_MXU = jnp.bfloat16

D_MODEL = 1024
HEAD_DIM = 64
SB_HEADS = 8
SWA_HEADS = 8
SWA_KV_HEADS = 2
SWA_GROUP = SWA_HEADS // SWA_KV_HEADS
SB_WIDTH = SB_HEADS * HEAD_DIM
SWA_WIDTH = SWA_HEADS * HEAD_DIM
SWA_KV_WIDTH = SWA_KV_HEADS * HEAD_DIM
IN_COLS = 3 * SB_WIDTH + SWA_WIDTH + 2 * SWA_KV_WIDTH
BLOCK = 128
REL_BUCKETS = 32
REL_MAX_DIST = 128
D_FF = 2816
FF_CHUNK = D_FF // 2
ALPHA = 2.0 ** 0.25
LN_EPS = 1e-5
RMS_EPS = 1e-6
SCALE = HEAD_DIM ** -0.5
SB_GROUP_FWD = 8
SB_GROUP_BWD = 4
SWA_SUB = 8

ADAM_LR = 0.001
ADAM_B1 = 0.9
ADAM_B2 = 0.999
ADAM_EPS = 1e-08
ADAM_WD = 0.01
ADAM_STEP = 10

N_CHIPS = 4
SLAB_ROWS = (D_MODEL * IN_COLS // N_CHIPS + D_MODEL * D_MODEL // N_CHIPS
             + D_MODEL * 2 * D_FF // N_CHIPS + D_FF * D_MODEL // N_CHIPS) // D_MODEL
HALF_ROWS = SLAB_ROWS // 2
SMALL_ROWS = 16

MESH = pl.DeviceIdType.MESH


def _sds(shape, dtype):
    return jax.ShapeDtypeStruct(shape, dtype)


def _cp(sem=None, vmem_mb=48):
    kw = dict(vmem_limit_bytes=vmem_mb * 1024 * 1024)
    if sem is not None:
        kw["dimension_semantics"] = sem
    return pltpu.CompilerParams(**kw)


def _dot(a, b):
    return jnp.dot(a, b, preferred_element_type=F32)


def _dot_nt(a, b):
    return lax.dot_general(a, b, (((1,), (1,)), ((), ())), preferred_element_type=F32)


def _dot_tn(a, b):
    return lax.dot_general(a, b, (((0,), (0,)), ((), ())), preferred_element_type=F32)


def _ln_hat(x):
    mu = jnp.mean(x, axis=-1, keepdims=True)
    xc = x - mu
    var = jnp.mean(xc * xc, axis=-1, keepdims=True)
    rstd = lax.rsqrt(var + LN_EPS)
    return xc * rstd, rstd


def _ln_bwd(xhat, rstd, dy, g):
    dxh = dy * g
    m1 = jnp.mean(dxh, axis=-1, keepdims=True)
    m2 = jnp.mean(dxh * xhat, axis=-1, keepdims=True)
    return rstd * (dxh - m1 - xhat * m2)


def _colsum(x):
    return jnp.sum(x, axis=0, keepdims=True)


def _split2(x):
    hi = x.astype(_MXU)
    lo = (x - hi.astype(F32)).astype(_MXU)
    return hi, lo


def _rows(tm, n):
    return pl.BlockSpec((tm, n), lambda i: (i, 0))


def _fixed(*shape):
    nd = len(shape)
    return pl.BlockSpec(shape, lambda i: (0,) * nd)


def _ln_in_proj(x, g, b, w):
    S = x.shape[0]
    N = w.shape[1]
    tm = min(S, 512)

    def body(x_ref, g_ref, b_ref, w_ref, h_ref, hb_ref, p_ref):
        xhat, _ = _ln_hat(x_ref[...])
        h = xhat * g_ref[...] + b_ref[...]
        h_ref[...] = h
        hb = h.astype(_MXU)
        hb_ref[...] = hb
        p_ref[...] = _dot(hb, w_ref[...]).astype(p_ref.dtype)

    return pl.pallas_call(
        body, name="ln_in_proj", grid=(S // tm,),
        in_specs=[_rows(tm, D_MODEL), _fixed(1, D_MODEL), _fixed(1, D_MODEL), _fixed(D_MODEL, N)],
        out_specs=[_rows(tm, D_MODEL), _rows(tm, D_MODEL), _rows(tm, N)],
        out_shape=[_sds((S, D_MODEL), F32), _sds((S, D_MODEL), _MXU), _sds((S, N), _MXU)],
        compiler_params=_cp(("parallel",)),
    )(x, g, b, w)


def _rms(x, g):
    r = lax.rsqrt(jnp.mean(x * x, axis=-1, keepdims=True) + RMS_EPS)
    return x * r * g, r


def _mix_out(sb, sw, gsb, gsw, w_out, h0):
    S = sb.shape[0]
    tm = min(S, 512)

    def body(sb_ref, sw_ref, gsb_ref, gsw_ref, w_ref, h0_ref, pre_ref, mg_ref):
        ysb, _ = _rms(sb_ref[...], gsb_ref[...])
        ysw, _ = _rms(sw_ref[...], gsw_ref[...])
        ysb = ysb.astype(_MXU)
        ysw = ysw.astype(_MXU)
        mg_ref[:, :SB_WIDTH] = ysb
        mg_ref[:, SB_WIDTH:] = ysw
        mix = _dot(ysb, w_ref[:SB_WIDTH, :]) + _dot(ysw, w_ref[SB_WIDTH:, :])
        pre_ref[...] = ALPHA * h0_ref[...] + mix

    return pl.pallas_call(
        body, name="mix_out", grid=(S // tm,),
        in_specs=[_rows(tm, SB_WIDTH), _rows(tm, SWA_WIDTH), _fixed(1, SB_WIDTH), _fixed(1, SWA_WIDTH),
                  _fixed(D_MODEL, D_MODEL), _rows(tm, D_MODEL)],
        out_specs=[_rows(tm, D_MODEL), _rows(tm, D_MODEL)],
        out_shape=[_sds((S, D_MODEL), F32), _sds((S, D_MODEL), _MXU)],
        compiler_params=_cp(("parallel",)),
    )(sb, sw, gsb, gsw, w_out, h0)


def _sigmoid(x):
    return 1.0 / (1.0 + jnp.exp(-x))


def _ffn_up(pre1, g1, b1, wgu):
    S = pre1.shape[0]
    tm = min(S, 512)

    def body(p_ref, g_ref, b_ref, wg_ref, wu_ref, h1_ref, gate_ref, up_ref, a_ref):
        xhat, _ = _ln_hat(p_ref[...])
        h1 = (xhat * g_ref[...] + b_ref[...]).astype(_MXU)
        h1_ref[...] = h1
        gate = _dot(h1, wg_ref[0])
        up = _dot(h1, wu_ref[0])
        gate_ref[...] = gate.astype(gate_ref.dtype)
        up_ref[...] = up.astype(up_ref.dtype)
        a_ref[...] = (gate * _sigmoid(gate) * up).astype(a_ref.dtype)

    chunk = lambda i, j: (i, j)
    return pl.pallas_call(
        body, name="ffn_up", grid=(S // tm, 2),
        in_specs=[pl.BlockSpec((tm, D_MODEL), lambda i, j: (i, 0)),
                  pl.BlockSpec((1, D_MODEL), lambda i, j: (0, 0)),
                  pl.BlockSpec((1, D_MODEL), lambda i, j: (0, 0)),
                  pl.BlockSpec((1, D_MODEL, FF_CHUNK), lambda i, j: (j, 0, 0)),
                  pl.BlockSpec((1, D_MODEL, FF_CHUNK), lambda i, j: (j + 2, 0, 0))],
        out_specs=[pl.BlockSpec((tm, D_MODEL), lambda i, j: (i, 0)),
                   pl.BlockSpec((tm, FF_CHUNK), chunk), pl.BlockSpec((tm, FF_CHUNK), chunk),
                   pl.BlockSpec((tm, FF_CHUNK), chunk)],
        out_shape=[_sds((S, D_MODEL), _MXU), _sds((S, D_FF), _MXU), _sds((S, D_FF), _MXU), _sds((S, D_FF), _MXU)],
        compiler_params=_cp(("parallel", "arbitrary")),
    )(pre1, g1, b1, wgu, wgu)


def _ffn_down_loss(a, w_down, pre1, g1, b1, g2, b2, tgt):
    S = a.shape[0]
    tm = min(S, 512)

    def body(a_ref, w_ref, p_ref, g1_ref, b1_ref, g2_ref, b2_ref, t_ref, d_ref, db_ref, dg2_ref, db2_ref, err_ref):
        @pl.when(pl.program_id(0) == 0)
        def _():
            dg2_ref[...] = jnp.zeros_like(dg2_ref)
            db2_ref[...] = jnp.zeros_like(db2_ref)
            err_ref[...] = jnp.zeros_like(err_ref)

        xhat1, _ = _ln_hat(p_ref[...])
        h1 = xhat1 * g1_ref[...] + b1_ref[...]
        pre2 = ALPHA * h1 + _dot(a_ref[...], w_ref[...])
        xhat2, rstd2 = _ln_hat(pre2)
        err = xhat2 * g2_ref[...] + b2_ref[...] - t_ref[...]
        dh2 = err * (1.0 / D_MODEL)
        dp2 = _ln_bwd(xhat2, rstd2, dh2, g2_ref[...])
        d_ref[...] = dp2
        db_ref[...] = dp2.astype(db_ref.dtype)
        dg2_ref[...] += _colsum(dh2 * xhat2)
        db2_ref[...] += _colsum(dh2)
        err_ref[...] += _colsum(err * err)

    vec = _fixed(1, D_MODEL)
    return pl.pallas_call(
        body, name="ffn_down_loss", grid=(S // tm,),
        in_specs=[_rows(tm, D_FF), _fixed(D_FF, D_MODEL), _rows(tm, D_MODEL), vec, vec, vec, vec, _rows(tm, D_MODEL)],
        out_specs=[_rows(tm, D_MODEL), _rows(tm, D_MODEL), vec, vec, vec],
        out_shape=[_sds((S, D_MODEL), F32), _sds((S, D_MODEL), _MXU), _sds((1, D_MODEL), F32), _sds((1, D_MODEL), F32),
                   _sds((1, D_MODEL), F32)],
        compiler_params=_cp(("arbitrary",)),
    )(a, w_down, pre1, g1, b1, g2, b2, tgt)


def _ffn_down_bwd(dp2b, w_down, gate, up):
    S = dp2b.shape[0]
    tm = min(S, 512)

    def body(d_ref, w_ref, g_ref, u_ref, dg_ref, du_ref):
        da = _dot_nt(d_ref[...], w_ref[...])
        g = g_ref[...].astype(F32)
        u = u_ref[...].astype(F32)
        sg = _sigmoid(g)
        du_ref[...] = (da * g * sg).astype(du_ref.dtype)
        dg_ref[...] = (da * u * (sg * (1.0 + g * (1.0 - sg)))).astype(dg_ref.dtype)

    chunk = pl.BlockSpec((tm, FF_CHUNK), lambda i, j: (i, j))
    return pl.pallas_call(
        body, name="ffn_down_bwd", grid=(S // tm, 2),
        in_specs=[pl.BlockSpec((tm, D_MODEL), lambda i, j: (i, 0)),
                  pl.BlockSpec((FF_CHUNK, D_MODEL), lambda i, j: (j, 0)), chunk, chunk],
        out_specs=[chunk, chunk],
        out_shape=[_sds((S, D_FF), _MXU), _sds((S, D_FF), _MXU)],
        compiler_params=_cp(("parallel", "arbitrary")),
    )(dp2b, w_down, gate, up)


def _ffn_up_bwd(dgate, dup, wgu, dp2, pre1, g1):
    S = dgate.shape[0]
    tm = min(S, 256)

    def body(dg_ref, du_ref, w_ref, d2_ref, p_ref, g_ref, d1_ref, d1b_ref, dg1_ref, db1_ref):
        @pl.when(pl.program_id(0) == 0)
        def _():
            dg1_ref[...] = jnp.zeros_like(dg1_ref)
            db1_ref[...] = jnp.zeros_like(db1_ref)

        dh1 = ALPHA * d2_ref[...]
        for j in range(2):
            cols = slice(j * FF_CHUNK, (j + 1) * FF_CHUNK)
            dh1 += _dot_nt(dg_ref[:, cols], w_ref[j])
            dh1 += _dot_nt(du_ref[:, cols], w_ref[j + 2])
        xhat, rstd = _ln_hat(p_ref[...])
        dp1 = _ln_bwd(xhat, rstd, dh1, g_ref[...])
        d1_ref[...] = dp1
        d1b_ref[...] = dp1.astype(d1b_ref.dtype)
        dg1_ref[...] += _colsum(dh1 * xhat)
        db1_ref[...] += _colsum(dh1)

    vec = _fixed(1, D_MODEL)
    return pl.pallas_call(
        body, name="ffn_up_bwd", grid=(S // tm,),
        in_specs=[_rows(tm, D_FF), _rows(tm, D_FF), _fixed(4, D_MODEL, FF_CHUNK), _rows(tm, D_MODEL),
                  _rows(tm, D_MODEL), vec],
        out_specs=[_rows(tm, D_MODEL), _rows(tm, D_MODEL), vec, vec],
        out_shape=[_sds((S, D_MODEL), F32), _sds((S, D_MODEL), _MXU), _sds((1, D_MODEL), F32), _sds((1, D_MODEL), F32)],
        compiler_params=_cp(("arbitrary",), vmem_mb=56),
    )(dgate, dup, wgu, dp2, pre1, g1)


def _rms_bwd(x, g, dy):
    n = x.shape[-1]
    r = lax.rsqrt(jnp.mean(x * x, axis=-1, keepdims=True) + RMS_EPS)
    u = dy * g
    dx = r * u - x * (r * r * r) * (jnp.sum(u * x, axis=-1, keepdims=True) * (1.0 / n))
    return dx, _colsum(dy * x * r)


def _mix_bwd(dp1b, w_out, sb, sw, gsb, gsw):
    S = sb.shape[0]
    tm = min(S, 512)

    def body(d_ref, w_ref, sb_ref, sw_ref, gsb_ref, gsw_ref, dsb_ref, dsw_ref, dgsb_ref, dgsw_ref):
        @pl.when(pl.program_id(0) == 0)
        def _():
            dgsb_ref[...] = jnp.zeros_like(dgsb_ref)
            dgsw_ref[...] = jnp.zeros_like(dgsw_ref)

        dm = _dot_nt(d_ref[...], w_ref[...])
        dsb, dgsb = _rms_bwd(sb_ref[...], gsb_ref[...], dm[:, :SB_WIDTH])
        dsw, dgsw = _rms_bwd(sw_ref[...], gsw_ref[...], dm[:, SB_WIDTH:])
        dsb_ref[...] = dsb.astype(dsb_ref.dtype)
        dsw_ref[...] = dsw.astype(dsw_ref.dtype)
        dgsb_ref[...] += dgsb
        dgsw_ref[...] += dgsw

    return pl.pallas_call(
        body, name="mix_bwd", grid=(S // tm,),
        in_specs=[_rows(tm, D_MODEL), _fixed(D_MODEL, D_MODEL), _rows(tm, SB_WIDTH), _rows(tm, SWA_WIDTH),
                  _fixed(1, SB_WIDTH), _fixed(1, SWA_WIDTH)],
        out_specs=[_rows(tm, SB_WIDTH), _rows(tm, SWA_WIDTH), _fixed(1, SB_WIDTH), _fixed(1, SWA_WIDTH)],
        out_shape=[_sds((S, SB_WIDTH), _MXU), _sds((S, SWA_WIDTH), _MXU), _sds((1, SB_WIDTH), F32),
                   _sds((1, SWA_WIDTH), F32)],
        compiler_params=_cp(("arbitrary",)),
    )(dp1b, w_out, sb, sw, gsb, gsw)


def _in_proj_bwd(dproj, w_in, dp1, x, g):
    S = x.shape[0]
    N = dproj.shape[1]
    tm = min(S, 512)

    def body(dpj_ref, w_ref, d1_ref, x_ref, g_ref, gx_ref, dg_ref, db_ref):
        @pl.when(pl.program_id(0) == 0)
        def _():
            dg_ref[...] = jnp.zeros_like(dg_ref)
            db_ref[...] = jnp.zeros_like(db_ref)

        dh0 = _dot_nt(dpj_ref[...], w_ref[...]) + ALPHA * d1_ref[...]
        xhat, rstd = _ln_hat(x_ref[...])
        gx_ref[...] = _ln_bwd(xhat, rstd, dh0, g_ref[...])
        dg_ref[...] += _colsum(dh0 * xhat)
        db_ref[...] += _colsum(dh0)

    vec = _fixed(1, D_MODEL)
    return pl.pallas_call(
        body, name="in_proj_bwd", grid=(S // tm,),
        in_specs=[_rows(tm, N), _fixed(D_MODEL, N), _rows(tm, D_MODEL), _rows(tm, D_MODEL), vec],
        out_specs=[_rows(tm, D_MODEL), vec, vec],
        out_shape=[_sds((S, D_MODEL), F32), _sds((1, D_MODEL), F32), _sds((1, D_MODEL), F32)],
        compiler_params=_cp(("arbitrary",)),
    )(dproj, w_in, dp1, x, g)


def _matmul_tn(a, b, name, tk, tn):
    T, K = a.shape
    N = b.shape[1]
    tt = min(T, 512)

    def body(a_ref, b_ref, o_ref):
        @pl.when(pl.program_id(2) == 0)
        def _():
            o_ref[...] = jnp.zeros_like(o_ref)

        o_ref[...] += _dot_tn(a_ref[...], b_ref[...])

    return pl.pallas_call(
        body, name=name, grid=(K // tk, N // tn, T // tt),
        in_specs=[pl.BlockSpec((tt, tk), lambda k, n, t: (t, k)), pl.BlockSpec((tt, tn), lambda k, n, t: (t, n))],
        out_specs=pl.BlockSpec((tk, tn), lambda k, n, t: (k, n)),
        out_shape=_sds((K, N), F32),
        compiler_params=_cp(("parallel", "parallel", "arbitrary")),
    )(a, b)


def _sb_logs(zt, causal):
    e = jnp.exp(-jnp.abs(zt))
    lb = jnp.minimum(zt, 0.0) - jnp.log(1.0 + e)
    l1m = lb - zt
    if causal is not None:
        l1m = jnp.where(causal, l1m, 0.0)
    return lb, l1m


def _sb_weights(lb, suf, causal):
    a = jnp.exp(lb + suf)
    if causal is not None:
        a = jnp.where(causal, a, 0.0)
    return a


def _tri_masks(t):
    r = lax.broadcasted_iota(jnp.int32, (t, t), 0)
    c = lax.broadcasted_iota(jnp.int32, (t, t), 1)
    return r, c


def _sb_fwd(qT, kb, vTb):
    Hh, _, S = qT.shape
    nk, T = kb.shape[1], kb.shape[2]
    nq = S // T
    G = SB_GROUP_FWD

    def body(qT_ref, k_ref, vT_ref, oT_ref, rs_ref):
        i = pl.program_id(1)
        qts = [(qT_ref[g].astype(F32) * SCALE).astype(_MXU) for g in range(G)]
        r, c = _tri_masks(T)
        upper = (c > r).astype(_MXU)
        causal = r < c

        def blk(j, carry, mask):
            hs = range(G)
            for g in hs:
                rs_ref[g, 0, j] = jnp.broadcast_to(carry[g][0], (8, T))
            zs = [_dot(k_ref[g, j], qts[g]) for g in hs]
            lbs, l1ms = zip(*[_sb_logs(zs[g], mask) for g in hs])
            splits = [_split2(l1ms[g]) for g in hs]
            cums = [_dot(upper, splits[g][0]) + _dot(upper, splits[g][1]) for g in hs]
            avs = [_sb_weights(lbs[g], carry[g][0] + cums[g], mask).astype(_MXU) for g in hs]
            accs = [carry[g][1] + _dot(vT_ref[g, j], avs[g]) for g in hs]
            return tuple((carry[g][0] + _colsum(l1ms[g]), accs[g]) for g in hs)

        init = tuple((jnp.zeros((1, T), F32), jnp.zeros((HEAD_DIM, T), F32)) for _ in range(G))
        carry = blk(i, init, causal)
        carry = lax.fori_loop(0, i, lambda s, cr: blk(i - 1 - s, cr, None), carry)
        for g in range(G):
            oT_ref[g] = carry[g][1]

    return pl.pallas_call(
        body, name="sb_fwd", grid=(Hh // G, nq),
        in_specs=[pl.BlockSpec((G, HEAD_DIM, T), lambda h, i: (h, 0, i)),
                  pl.BlockSpec((G, nk, T, HEAD_DIM), lambda h, i: (h, 0, 0, 0), pipeline_mode=pl.Buffered(1)),
                  pl.BlockSpec((G, nk, HEAD_DIM, T), lambda h, i: (h, 0, 0, 0), pipeline_mode=pl.Buffered(1))],
        out_specs=[pl.BlockSpec((G, HEAD_DIM, T), lambda h, i: (h, 0, i)),
                   pl.BlockSpec((G, 1, nk, 8, T), lambda h, i: (h, i, 0, 0, 0))],
        out_shape=[_sds((Hh, HEAD_DIM, S), F32), _sds((Hh, nq, nk, 8, T), F32)],
        compiler_params=_cp(("parallel", "arbitrary")),
    )(qT, kb, vTb)


def _sb_bwd(qT, kb, kTb, vb, doT, rsave):
    Hh, _, S = qT.shape
    nk, T = kb.shape[1], kb.shape[2]
    nq = S // T
    G = SB_GROUP_BWD

    def body(qT_ref, k_ref, kT_ref, v_ref, doT_ref, rs_ref, dqT_ref, dk_ref, dv_ref):
        i = pl.program_id(1)

        @pl.when(i == 0)
        def _():
            dk_ref[...] = jnp.zeros_like(dk_ref)
            dv_ref[...] = jnp.zeros_like(dv_ref)

        qts = [(qT_ref[g].astype(F32) * SCALE).astype(_MXU) for g in range(G)]
        douts = [doT_ref[g] for g in range(G)]
        r, c = _tri_masks(T)
        upper = (c > r).astype(_MXU)
        lower = (c < r).astype(_MXU)
        causal = r < c

        def blk(j, carry, mask):
            hs = range(G)
            zs = [_dot(k_ref[g, j], qts[g]) for g in hs]
            das = [_dot(v_ref[g, j], douts[g]) for g in hs]
            lbs, l1ms = zip(*[_sb_logs(zs[g], mask) for g in hs])
            splits = [_split2(l1ms[g]) for g in hs]
            cums = [_dot(upper, splits[g][0]) + _dot(upper, splits[g][1]) for g in hs]
            avs = [_sb_weights(lbs[g], rs_ref[g, 0, j][0:1, :] + cums[g], mask) for g in hs]
            ets = [das[g] * avs[g] for g in hs]
            esplits = [_split2(ets[g]) for g in hs]
            ecums = [_dot(lower, esplits[g][0]) + _dot(lower, esplits[g][1]) for g in hs]
            dzs = []
            for g in hs:
                sig = jnp.exp(lbs[g])
                dz = ets[g] * (1.0 - sig) - (carry[g][0] + ecums[g]) * sig
                if mask is not None:
                    dz = jnp.where(mask, dz, 0.0)
                dzs.append(dz.astype(_MXU))
            dqs = [carry[g][1] + _dot(kT_ref[g, j], dzs[g]) for g in hs]
            for g in hs:
                dk_ref[g, j] += _dot_nt(qts[g], dzs[g])
            for g in hs:
                dv_ref[g, j] += _dot_nt(douts[g], avs[g].astype(_MXU))
            return tuple((carry[g][0] + _colsum(ets[g]), dqs[g]) for g in hs)

        carry = tuple((jnp.zeros((1, T), F32), jnp.zeros((HEAD_DIM, T), F32)) for _ in range(G))
        carry = lax.fori_loop(0, i, lambda s, cr: blk(s, cr, None), carry)
        carry = blk(i, carry, causal)
        for g in range(G):
            dqT_ref[g] = carry[g][1] * SCALE

    colblk = pl.BlockSpec((G, HEAD_DIM, T), lambda h, i: (h, 0, i))
    once = pl.Buffered(1)
    kblk = pl.BlockSpec((G, nk, T, HEAD_DIM), lambda h, i: (h, 0, 0, 0), pipeline_mode=once)
    kTblk = pl.BlockSpec((G, nk, HEAD_DIM, T), lambda h, i: (h, 0, 0, 0), pipeline_mode=once)
    return pl.pallas_call(
        body, name="sb_bwd", grid=(Hh // G, nq),
        in_specs=[colblk, kblk, kTblk, kblk, colblk,
                  pl.BlockSpec((G, 1, nk, 8, T), lambda h, i: (h, i, 0, 0, 0))],
        out_specs=[colblk, kTblk, kTblk],
        out_shape=[_sds((Hh, HEAD_DIM, S), F32), _sds((Hh, nk, HEAD_DIM, T), F32), _sds((Hh, nk, HEAD_DIM, T), F32)],
        compiler_params=_cp(("parallel", "arbitrary"), vmem_mb=56),
    )(qT, kb, kTb, vb, doT, rsave)


def _bucket_table():
    qi = np.arange(BLOCK)[:, None]
    cj = np.arange(2 * BLOCK)[None, :]
    dist = qi + BLOCK - cj
    exact = REL_BUCKETS // 2
    d = np.maximum(dist, 0)
    d_f = np.maximum(d, 1).astype(np.float32)
    large = exact + (np.log(d_f / np.float32(exact)) / np.float32(math.log(REL_MAX_DIST / exact))
                     * np.float32(REL_BUCKETS - exact)).astype(np.int32)
    large = np.minimum(large, REL_BUCKETS - 1)
    return np.where(d < exact, d, large).astype(np.int32)


def _swa_bias(rel_bias, bucket):
    def body(rb_ref, bk_ref, o_ref):
        bk = bk_ref[...]
        for h in range(SWA_HEADS):
            t = jnp.zeros((BLOCK, 2 * BLOCK), F32)
            for b in range(REL_BUCKETS):
                t = jnp.where(bk == b, rb_ref[b, h], t)
            o_ref[h] = t

    return pl.pallas_call(
        body, name="swa_bias",
        in_specs=[pl.BlockSpec(memory_space=pltpu.SMEM), pl.BlockSpec(memory_space=pltpu.VMEM)],
        out_specs=pl.BlockSpec(memory_space=pltpu.VMEM),
        out_shape=_sds((SWA_HEADS, BLOCK, 2 * BLOCK), F32),
    )(rel_bias, bucket)


def _swa_logits(q, kp, kc):
    qs = (q.astype(F32) * SCALE).astype(_MXU)
    return qs, _dot_nt(qs, kp), _dot_nt(qs, kc)


def _swa_softmax(lp, lc, bias, sink, live_prev):
    r, c = _tri_masks(BLOCK)
    in_window = c > r if live_prev is None else jnp.logical_and(c > r, live_prev)
    lp = jnp.where(in_window, lp + bias[:, :BLOCK], -jnp.inf)
    lc = jnp.where(c <= r, lc + bias[:, BLOCK:], -jnp.inf)
    m = jnp.maximum(jnp.maximum(jnp.max(lp, axis=1, keepdims=True), jnp.max(lc, axis=1, keepdims=True)), sink)
    pp = jnp.exp(lp - m)
    pc = jnp.exp(lc - m)
    ps = jnp.exp(sink - m)
    denom = jnp.sum(pp, axis=1, keepdims=True) + jnp.sum(pc, axis=1, keepdims=True) + ps
    return pp / denom, pc / denom, ps / denom


def _swa_sub(nb):
    return min(SWA_SUB, nb)


def _swa_keys(b, prev_ref, cur_ref, i):
    cur = cur_ref[0, b * BLOCK:(b + 1) * BLOCK, :]
    if b == 0:
        return prev_ref[0], cur, i > 0
    return cur_ref[0, (b - 1) * BLOCK:b * BLOCK, :], cur, None


def _swa_fwd(q, k, v, bias, sink):
    S = q.shape[1]
    nb = S // BLOCK
    ns = _swa_sub(nb)
    R = ns * BLOCK

    def body(q_ref, kp_ref, kc_ref, vp_ref, vc_ref, bias_ref, sink_ref, o_ref):
        i = pl.program_id(1)
        bias = bias_ref[0]
        sink = sink_ref[0][:, :1]
        subs = range(ns)
        rows = [slice(b * BLOCK, (b + 1) * BLOCK) for b in subs]
        keys = [_swa_keys(b, kp_ref, kc_ref, i) for b in subs]
        vals = [_swa_keys(b, vp_ref, vc_ref, i) for b in subs]
        logits = [_swa_logits(q_ref[0, rows[b], :], keys[b][0], keys[b][1]) for b in subs]
        ws = [_swa_softmax(logits[b][1], logits[b][2], bias, sink, keys[b][2]) for b in subs]
        for b in subs:
            o_ref[0, rows[b], :] = _dot(ws[b][0].astype(_MXU), vals[b][0]) + _dot(ws[b][1].astype(_MXU), vals[b][1])

    prev = pl.BlockSpec((1, BLOCK, HEAD_DIM), lambda h, i: (h // SWA_GROUP, jnp.maximum(i * ns - 1, 0), 0))
    cur = pl.BlockSpec((1, R, HEAD_DIM), lambda h, i: (h // SWA_GROUP, i, 0))
    qblk = pl.BlockSpec((1, R, HEAD_DIM), lambda h, i: (h, i, 0))
    return pl.pallas_call(
        body, name="swa_fwd", grid=(SWA_HEADS, nb // ns),
        in_specs=[qblk, prev, cur, prev, cur,
                  pl.BlockSpec((1, BLOCK, 2 * BLOCK), lambda h, i: (h, 0, 0)),
                  pl.BlockSpec((1, 1, BLOCK), lambda h, i: (h, 0, 0))],
        out_specs=qblk,
        out_shape=_sds((SWA_HEADS, S, HEAD_DIM), F32),
        compiler_params=_cp(("parallel", "parallel")),
    )(q, k, k, v, v, bias, sink)


def _swa_bwd(q, k, v, bias, sink, do):
    S = q.shape[1]
    nb = S // BLOCK
    ns = _swa_sub(nb)
    R = ns * BLOCK

    def body(q_ref, kp_ref, kc_ref, vp_ref, vc_ref, bias_ref, sink_ref, do_ref, dq_ref, dk_ref, dv_ref, dbias_ref,
             dsink_ref):
        g = pl.program_id(1)
        i = pl.program_id(2)

        @pl.when(jnp.logical_and(g == 0, i == 0))
        def _():
            dk_ref[...] = jnp.zeros_like(dk_ref)
            dv_ref[...] = jnp.zeros_like(dv_ref)

        @pl.when(i == 0)
        def _():
            dbias_ref[...] = jnp.zeros_like(dbias_ref)
            dsink_ref[...] = jnp.zeros_like(dsink_ref)

        bias = bias_ref[0]
        sink = sink_ref[0][:, :1]
        subs = range(ns)
        rows = [slice(b * BLOCK, (b + 1) * BLOCK) for b in subs]
        keys = [_swa_keys(b, kp_ref, kc_ref, i) for b in subs]
        vals = [_swa_keys(b, vp_ref, vc_ref, i) for b in subs]
        douts = [do_ref[0, rows[b], :] for b in subs]
        logits = [_swa_logits(q_ref[0, rows[b], :], keys[b][0], keys[b][1]) for b in subs]
        dws = [(_dot_nt(douts[b], vals[b][0]), _dot_nt(douts[b], vals[b][1])) for b in subs]
        dbp = jnp.zeros((BLOCK, BLOCK), F32)
        dbc = jnp.zeros((BLOCK, BLOCK), F32)
        dsk = jnp.zeros((BLOCK, 1), F32)
        wts, dls = [], []
        for b in subs:
            wp, wc, ws = _swa_softmax(logits[b][1], logits[b][2], bias, sink, keys[b][2])
            dwp, dwc = dws[b]
            delta = jnp.sum(wp * dwp, axis=1, keepdims=True) + jnp.sum(wc * dwc, axis=1, keepdims=True)
            dlp = wp * (dwp - delta)
            dlc = wc * (dwc - delta)
            dbp += dlp
            dbc += dlc
            dsk -= ws * delta
            wts.append((wp.astype(_MXU), wc.astype(_MXU)))
            dls.append((dlp.astype(_MXU), dlc.astype(_MXU)))
        for b in subs:
            dq_ref[0, rows[b], :] = (_dot(dls[b][0], keys[b][0]) + _dot(dls[b][1], keys[b][1])) * SCALE
        for b in subs:
            qs = logits[b][0]
            blk = i * ns + b
            dk_ref[0, blk] += _dot_tn(dls[b][1], qs)
            dv_ref[0, blk] += _dot_tn(wts[b][1], douts[b])
            if b == 0:
                @pl.when(i > 0)
                def _():
                    dk_ref[0, blk - 1] += _dot_tn(dls[0][0], qs)
                    dv_ref[0, blk - 1] += _dot_tn(wts[0][0], douts[0])
            else:
                dk_ref[0, blk - 1] += _dot_tn(dls[b][0], qs)
                dv_ref[0, blk - 1] += _dot_tn(wts[b][0], douts[b])
        dbias_ref[0, :, :BLOCK] += dbp
        dbias_ref[0, :, BLOCK:] += dbc
        dsink_ref[0] += jnp.broadcast_to(dsk, (BLOCK, BLOCK))

    hq = lambda kv, g, i: kv * SWA_GROUP + g
    prev = pl.BlockSpec((1, BLOCK, HEAD_DIM), lambda kv, g, i: (kv, jnp.maximum(i * ns - 1, 0), 0))
    cur = pl.BlockSpec((1, R, HEAD_DIM), lambda kv, g, i: (kv, i, 0))
    qblk = pl.BlockSpec((1, R, HEAD_DIM), lambda kv, g, i: (hq(kv, g, i), i, 0))
    kvacc = pl.BlockSpec((1, nb, BLOCK, HEAD_DIM), lambda kv, g, i: (kv, 0, 0, 0))
    return pl.pallas_call(
        body, name="swa_bwd", grid=(SWA_KV_HEADS, SWA_GROUP, nb // ns),
        in_specs=[qblk, prev, cur, prev, cur,
                  pl.BlockSpec((1, BLOCK, 2 * BLOCK), lambda kv, g, i: (hq(kv, g, i), 0, 0)),
                  pl.BlockSpec((1, 1, BLOCK), lambda kv, g, i: (hq(kv, g, i), 0, 0)), qblk],
        out_specs=[qblk, kvacc, kvacc,
                   pl.BlockSpec((1, BLOCK, 2 * BLOCK), lambda kv, g, i: (hq(kv, g, i), 0, 0)),
                   pl.BlockSpec((1, BLOCK, BLOCK), lambda kv, g, i: (hq(kv, g, i), 0, 0))],
        out_shape=[_sds((SWA_HEADS, S, HEAD_DIM), F32), _sds((SWA_KV_HEADS, nb, BLOCK, HEAD_DIM), F32),
                   _sds((SWA_KV_HEADS, nb, BLOCK, HEAD_DIM), F32), _sds((SWA_HEADS, BLOCK, 2 * BLOCK), F32),
                   _sds((SWA_HEADS, BLOCK, BLOCK), F32)],
        compiler_params=_cp(("arbitrary", "arbitrary", "arbitrary")),
    )(q, k, k, v, v, bias, sink, do)


def _swa_small_grads(dbias, dsink, bucket):
    rows = REL_BUCKETS + 8

    def total(x):
        return jnp.sum(jnp.sum(x, axis=1, keepdims=True), axis=0, keepdims=True)

    def body(db_ref, ds_ref, bk_ref, o_ref):
        bk = bk_ref[...]
        r = lax.broadcasted_iota(jnp.int32, (rows, BLOCK), 0)
        c = lax.broadcasted_iota(jnp.int32, (rows, BLOCK), 1)
        out = jnp.zeros((rows, BLOCK), F32)
        for h in range(SWA_HEADS):
            db = db_ref[h]
            for b in range(REL_BUCKETS):
                s = total(jnp.where(bk == b, db, 0.0))
                out = jnp.where(jnp.logical_and(r == b, c == h), s, out)
            s = jnp.sum(ds_ref[h][:, :1], axis=0, keepdims=True)
            out = jnp.where(jnp.logical_and(r == REL_BUCKETS, c == h), s, out)
        o_ref[...] = out

    vm = pl.BlockSpec(memory_space=pltpu.VMEM)
    return pl.pallas_call(body, name="swa_small_grads", in_specs=[vm, vm, vm], out_specs=vm,
                          out_shape=_sds((rows, BLOCK), F32))(dbias, dsink, bucket)


def _tile_rows(n):
    for t in (512, 368, 256, 184, 128, 64, 32, 16, 8):
        if n % t == 0:
            return t
    return n


def _cast_rows(x, dtype, name):
    R, C = x.shape
    tr = _tile_rows(R)

    def body(x_ref, o_ref):
        o_ref[...] = x_ref[...].astype(o_ref.dtype)

    return pl.pallas_call(body, name=name, grid=(R // tr,), in_specs=[_rows(tr, C)], out_specs=_rows(tr, C),
                          out_shape=_sds((R, C), dtype), compiler_params=_cp(("parallel",)))(x)


def _pair_sum(g, recv, name):
    n, R, C = g.shape
    tr = _tile_rows(R)

    def body(a_ref, b_ref, o_ref):
        o_ref[...] = (a_ref[...] + b_ref[...]).astype(o_ref.dtype)

    blk = pl.BlockSpec((1, tr, C), lambda j, i: (j, i, 0))
    return pl.pallas_call(body, name=name, grid=(n, R // tr), in_specs=[blk, blk], out_specs=blk,
                          out_shape=_sds((n, R, C), _MXU),
                          compiler_params=_cp(("parallel", "parallel")))(g, recv)


def _chip_sum(parts, name):
    n, R, C = parts.shape
    tr = _tile_rows(R)

    def body(p_ref, o_ref):
        acc = p_ref[0].astype(F32)
        for j in range(1, n):
            acc = acc + p_ref[j].astype(F32)
        o_ref[...] = acc

    return pl.pallas_call(body, name=name, grid=(R // tr,),
                          in_specs=[pl.BlockSpec((n, tr, C), lambda i: (0, i, 0))], out_specs=_rows(tr, C),
                          out_shape=_sds((R, C), F32), compiler_params=_cp(("parallel",)))(parts)


def _adamw_math(w, g, m, v):
    m = ADAM_B1 * m + (1.0 - ADAM_B1) * g
    v = ADAM_B2 * v + (1.0 - ADAM_B2) * (g * g)
    m_hat = m / (1.0 - ADAM_B1 ** ADAM_STEP)
    v_hat = v / (1.0 - ADAM_B2 ** ADAM_STEP)
    delta = -ADAM_LR * (m_hat / (jnp.sqrt(v_hat) + ADAM_EPS) + ADAM_WD * w)
    return delta, m, v


def _adamw(w, g, m, v, name):
    R, C = w.shape
    tr = _tile_rows(R)

    def body(w_ref, g_ref, m_ref, v_ref, d_ref, nm_ref, nv_ref):
        d, nm, nv = _adamw_math(w_ref[...], g_ref[...], m_ref[...], v_ref[...])
        d_ref[...] = d
        nm_ref[...] = nm
        nv_ref[...] = nv

    blk = _rows(tr, C)
    return pl.pallas_call(body, name=name, grid=(R // tr,), in_specs=[blk] * 4, out_specs=[blk] * 3,
                          out_shape=[_sds((R, C), F32)] * 3, compiler_params=_cp(("parallel",)))(w, g, m, v)


def _place():
    x, y, c = lax.axis_index("x"), lax.axis_index("y"), lax.axis_index("c")
    chips = [(1 - x, y), (x, 1 - y), (1 - x, 1 - y)]
    return x, y, c, chips


def _gather_weights(slab):
    rows, cols = slab.shape
    half = rows // 2

    def body(in_ref, out_ref, send_sems, recv_sems, local_sem):
        x, y, c, chips = _place()
        sibling = (x, y, 1 - c)

        def part(chip, hc):
            return out_ref.at[2 * chip[0] + chip[1], pl.ds(pl.multiple_of(hc * half, 16), half), :]

        def copy(k, chip, hc, to, src=None):
            return pltpu.make_async_remote_copy(
                src_ref=part(chip, hc) if src is None else src, dst_ref=part(chip, hc),
                send_sem=send_sems.at[k], recv_sem=recv_sems.at[k], device_id=to, device_id_type=MESH)

        mine = pltpu.make_async_copy(in_ref, out_ref.at[2 * x + y], local_sem)
        mine.start()
        my_half = in_ref.at[pl.ds(pl.multiple_of(c * half, 16), half), :]
        first = [copy(j, (x, y), c, (*chip, c), src=my_half) for j, chip in enumerate(chips)]
        for cp in first:
            cp.start()
        passed = [copy(3 + j, chip, c, sibling) for j, chip in enumerate(chips)]
        for j, chip in enumerate(chips):
            copy(j, chip, c, (x, y, c)).wait_recv()
            passed[j].start()
        for j, chip in enumerate(chips):
            copy(3 + j, chip, 1 - c, (x, y, c)).wait_recv()
        for cp in first + passed:
            cp.wait_send()
        mine.wait()

    any_spec = pl.BlockSpec(memory_space=pl.ANY)
    return pl.pallas_call(
        body, name="gather_weights", in_specs=[any_spec], out_specs=any_spec,
        out_shape=_sds((N_CHIPS, rows, cols), slab.dtype),
        scratch_shapes=[pltpu.SemaphoreType.DMA((6,)), pltpu.SemaphoreType.DMA((6,)), pltpu.SemaphoreType.DMA],
    )(slab)


def _swap_halves(g):
    n, rows, cols = g.shape
    half = rows // 2

    def body(g_ref, out_ref, send_sem, recv_sem):
        x, y, c, _ = _place()
        theirs = g_ref.at[:, pl.ds(pl.multiple_of((1 - c) * half, 8), half), :]
        cp = pltpu.make_async_remote_copy(src_ref=theirs, dst_ref=out_ref, send_sem=send_sem, recv_sem=recv_sem,
                                          device_id=(x, y, 1 - c), device_id_type=MESH)
        cp.start()
        cp.wait()

    any_spec = pl.BlockSpec(memory_space=pl.ANY)
    return pl.pallas_call(
        body, name="swap_halves", in_specs=[any_spec], out_specs=any_spec, out_shape=_sds((n, half, cols), g.dtype),
        scratch_shapes=[pltpu.SemaphoreType.DMA, pltpu.SemaphoreType.DMA],
    )(g)


def _scatter_partials(p):
    n, half, cols = p.shape

    def body(p_ref, out_ref, send_sems, recv_sems, local_sem):
        x, y, c, chips = _place()
        me = 2 * x + y
        mine = pltpu.make_async_copy(p_ref.at[me], out_ref.at[me], local_sem)
        mine.start()
        sends = [pltpu.make_async_remote_copy(
            src_ref=p_ref.at[2 * chip[0] + chip[1]], dst_ref=out_ref.at[me], send_sem=send_sems.at[j],
            recv_sem=recv_sems.at[j], device_id=(*chip, c), device_id_type=MESH) for j, chip in enumerate(chips)]
        for cp in sends:
            cp.start()
        for j, chip in enumerate(chips):
            pltpu.make_async_remote_copy(
                src_ref=p_ref.at[me], dst_ref=out_ref.at[2 * chip[0] + chip[1]], send_sem=send_sems.at[j],
                recv_sem=recv_sems.at[j], device_id=(*chip, c), device_id_type=MESH).wait_recv()
        for cp in sends:
            cp.wait_send()
        mine.wait()

    any_spec = pl.BlockSpec(memory_space=pl.ANY)
    return pl.pallas_call(
        body, name="scatter_partials", in_specs=[any_spec], out_specs=any_spec, out_shape=_sds((n, half, cols), p.dtype),
        scratch_shapes=[pltpu.SemaphoreType.DMA((3,)), pltpu.SemaphoreType.DMA((3,)), pltpu.SemaphoreType.DMA],
    )(p)


def _join_halves(f):
    half, cols = f.shape

    def body(f_ref, out_ref, send_sem, recv_sem, local_sem):
        x, y, c, _ = _place()
        mine_rows = out_ref.at[pl.ds(pl.multiple_of(c * half, 8), half), :]
        mine = pltpu.make_async_copy(f_ref, mine_rows, local_sem)
        mine.start()
        cp = pltpu.make_async_remote_copy(src_ref=f_ref, dst_ref=mine_rows, send_sem=send_sem, recv_sem=recv_sem,
                                          device_id=(x, y, 1 - c), device_id_type=MESH)
        cp.start()
        their_rows = out_ref.at[pl.ds(pl.multiple_of((1 - c) * half, 8), half), :]
        pltpu.make_async_remote_copy(src_ref=f_ref, dst_ref=their_rows, send_sem=send_sem, recv_sem=recv_sem,
                                     device_id=(x, y, 1 - c), device_id_type=MESH).wait_recv()
        cp.wait_send()
        mine.wait()

    any_spec = pl.BlockSpec(memory_space=pl.ANY)
    return pl.pallas_call(
        body, name="join_halves", in_specs=[any_spec], out_specs=any_spec, out_shape=_sds((2 * half, cols), f.dtype),
        scratch_shapes=[pltpu.SemaphoreType.DMA, pltpu.SemaphoreType.DMA, pltpu.SemaphoreType.DMA],
    )(f)


def _allreduce_small(block):
    m_per, n = block.shape

    def body(x_ref, sum_ref, loss_ref, all_ref, send_sems, recv_sems, local_sem):
        x, y, c, chips = _place()
        me, sibling = (x, y, c), (x, y, 1 - c)

        def rows(px, py, pc):
            return all_ref.at[pl.ds(pl.multiple_of((4 * px + 2 * py + pc) * m_per, 8), m_per), :]

        def copy(k, blk, to, src=None):
            return pltpu.make_async_remote_copy(
                src_ref=rows(*blk) if src is None else src, dst_ref=rows(*blk), send_sem=send_sems.at[k],
                recv_sem=recv_sems.at[k], device_id=to, device_id_type=MESH)

        mine = pltpu.make_async_copy(x_ref, rows(*me), local_sem)
        mine.start()
        first = [copy(0, me, sibling, src=x_ref)]
        first += [copy(1 + j, me, (*chip, c), src=x_ref) for j, chip in enumerate(chips)]
        for cp in first:
            cp.start()
        passed = [copy(4 + j, (*chip, c), sibling) for j, chip in enumerate(chips)]
        for j, chip in enumerate(chips):
            copy(1 + j, (*chip, c), me).wait_recv()
            passed[j].start()
        copy(0, sibling, me).wait_recv()
        for j, chip in enumerate(chips):
            copy(4 + j, (*chip, 1 - c), me).wait_recv()
        for cp in first + passed:
            cp.wait_send()
        mine.wait()

        acc = all_ref[0:m_per, :]
        for d in range(1, 8):
            acc = acc + all_ref[d * m_per:(d + 1) * m_per, :]
        sum_ref[...] = acc
        tot = jnp.sum(acc[8:9, :], axis=1, keepdims=True) * (0.5 / D_MODEL)
        loss_ref[...] = jnp.broadcast_to(tot, loss_ref.shape)

    vm = pl.BlockSpec(memory_space=pltpu.VMEM)
    return pl.pallas_call(
        body, name="allreduce_small", in_specs=[vm], out_specs=[vm, vm],
        out_shape=[_sds((m_per, n), F32), _sds((8, 128), F32)],
        scratch_shapes=[pltpu.VMEM((8 * m_per, n), F32), pltpu.SemaphoreType.DMA((7,)), pltpu.SemaphoreType.DMA((7,)),
                        pltpu.SemaphoreType.DMA],
    )(block)


def _slab_sections():
    return [D_MODEL * IN_COLS // N_CHIPS // D_MODEL, D_MODEL // N_CHIPS, 2 * D_FF // N_CHIPS, D_FF // N_CHIPS]


def _to_slab(w_in_s, w_out_s, w_gu_s, w_down_s):
    return jnp.concatenate([w_in_s.reshape(-1, D_MODEL), w_out_s, w_gu_s.reshape(-1, D_MODEL), w_down_s], axis=0)


def _from_slab(slab):
    lead = slab.shape[:-2]
    r0, r1, r2, r3 = _slab_sections()
    o1, o2, o3 = r0, r0 + r1, r0 + r1 + r2
    return (slab[..., :o1, :].reshape(*lead, D_MODEL, IN_COLS // N_CHIPS), slab[..., o1:o2, :],
            slab[..., o2:o3, :].reshape(*lead, D_MODEL, FF_CHUNK), slab[..., o3:, :])


def _heads_rows(x, nh):
    S = x.shape[0]
    return x.reshape(S, nh, HEAD_DIM).transpose(1, 0, 2)


def _heads_cols(x, nh):
    S = x.shape[0]
    return x.reshape(S, nh, HEAD_DIM).transpose(1, 2, 0)


def _key_blocks(x, nh, t):
    S = x.shape[0]
    return x.reshape(S // t, t, nh, HEAD_DIM).transpose(2, 0, 1, 3)


def _key_blocks_t(x, nh, t):
    S = x.shape[0]
    return x.reshape(S // t, t, nh, HEAD_DIM).transpose(2, 0, 3, 1)


def _pad_row(v):
    v = v.reshape(1, -1)
    return jnp.pad(v, ((0, 0), (0, D_MODEL - v.shape[1])))


def _pack_small(ln_in_g, ln_in_b, sb_g, swa_g, sinks, rel_bias, ln1_g, ln1_b, ln2_g, ln2_b, extra):
    rows = [_pad_row(ln_in_g), _pad_row(ln_in_b), jnp.concatenate([sb_g.reshape(1, -1), swa_g.reshape(1, -1)], axis=1),
            _pad_row(jnp.concatenate([rel_bias.reshape(1, -1), sinks.reshape(1, -1)], axis=1)),
            _pad_row(ln1_g), _pad_row(ln1_b), _pad_row(ln2_g), _pad_row(ln2_b), _pad_row(extra)]
    rows.append(jnp.zeros((SMALL_ROWS - len(rows), D_MODEL), F32))
    return jnp.concatenate(rows, axis=0)


def _unpack_small(blk):
    nrb = REL_BUCKETS * SWA_HEADS
    return (blk[0], blk[1], blk[2:3, :SB_WIDTH], blk[2:3, SB_WIDTH:], blk[3:4, nrb:nrb + SWA_HEADS],
            blk[3, :nrb].reshape(REL_BUCKETS, SWA_HEADS), blk[4:5], blk[5:6], blk[6:7], blk[7:8])


def kernel(x, ln_in_g, ln_in_b, w_in, sb_norm_g, swa_norm_g, sinks, rel_bias, w_out, ln1_g, ln1_b, w_gate_up, w_down, ln2_g, ln2_b, loss_target, m_ln_in_g, m_ln_in_b, m_w_in, m_sb_norm_g, m_swa_norm_g, m_sinks, m_rel_bias, m_w_out, m_ln1_g, m_ln1_b, m_w_gate_up, m_w_down, m_ln2_g, m_ln2_b, v_ln_in_g, v_ln_in_b, v_w_in, v_sb_norm_g, v_swa_norm_g, v_sinks, v_rel_bias, v_w_out, v_ln1_g, v_ln1_b, v_w_gate_up, v_w_down, v_ln2_g, v_ln2_b):
    S = x.shape[1]
    x2 = x.reshape(S, D_MODEL)
    tgt = loss_target.reshape(S, D_MODEL)
    T = min(S, 256)
    bucket = jnp.asarray(_bucket_table())
    row = lambda v: v.reshape(1, -1)

    slab = _cast_rows(_to_slab(w_in[0], w_out[0], w_gate_up[0], w_down[0]), _MXU, "cast_weights")
    w_in_sh, w_out_sh, w_gu_sh, w_down_sh = _from_slab(_gather_weights(slab))
    w_in_f = jnp.concatenate([w_in_sh[j] for j in range(N_CHIPS)], axis=1)
    w_out_f = w_out_sh.reshape(D_MODEL, D_MODEL)
    w_down_f = w_down_sh.reshape(D_FF, D_MODEL)

    h0, h0b, proj = _ln_in_proj(x2, row(ln_in_g), row(ln_in_b), w_in_f)
    o1, o2, o3, o4, o5 = SB_WIDTH, 2 * SB_WIDTH, 3 * SB_WIDTH, 3 * SB_WIDTH + SWA_WIDTH, 3 * SB_WIDTH + SWA_WIDTH + SWA_KV_WIDTH
    q_sb, k_sb, v_sb = proj[:, :o1], proj[:, o1:o2], proj[:, o2:o3]
    q_sw, k_sw, v_sw = proj[:, o3:o4], proj[:, o4:o5], proj[:, o5:]
    qT_sb = _heads_cols(q_sb, SB_HEADS)
    kb_sb = _key_blocks(k_sb, SB_HEADS, T)
    oT_sb, rsave = _sb_fwd(qT_sb, kb_sb, _key_blocks_t(v_sb, SB_HEADS, T))
    sb_out = oT_sb.transpose(2, 0, 1).reshape(S, SB_WIDTH)

    bias = _swa_bias(rel_bias, bucket)
    sink_rows = jnp.broadcast_to(sinks.reshape(SWA_HEADS, 1, 1), (SWA_HEADS, 1, BLOCK))
    qh_sw, kh_sw, vh_sw = _heads_rows(q_sw, SWA_HEADS), _heads_rows(k_sw, SWA_KV_HEADS), _heads_rows(v_sw, SWA_KV_HEADS)
    swa_out = _swa_fwd(qh_sw, kh_sw, vh_sw, bias, sink_rows).transpose(1, 0, 2).reshape(S, SWA_WIDTH)

    pre1, merged = _mix_out(sb_out, swa_out, sb_norm_g, swa_norm_g, w_out_f, h0)
    h1b, gate, up, act = _ffn_up(pre1, ln1_g, ln1_b, w_gu_sh)
    dp2, dp2b, dg2, db2, errsum = _ffn_down_loss(act, w_down_f, pre1, ln1_g, ln1_b, ln2_g, ln2_b, tgt)

    g_w_down = _matmul_tn(act, dp2b, "grad_w_down", FF_CHUNK, D_MODEL)
    dgate, dup = _ffn_down_bwd(dp2b, w_down_f, gate, up)
    g_w_gate = _matmul_tn(h1b, dgate, "grad_w_gate", D_MODEL, FF_CHUNK)
    g_w_up = _matmul_tn(h1b, dup, "grad_w_up", D_MODEL, FF_CHUNK)
    dp1, dp1b, dg1, db1 = _ffn_up_bwd(dgate, dup, w_gu_sh, dp2, pre1, ln1_g)
    g_w_out = _matmul_tn(merged, dp1b, "grad_w_out", D_MODEL, D_MODEL)
    dsb, dsw, dgsb, dgsw = _mix_bwd(dp1b, w_out_f, sb_out, swa_out, sb_norm_g, swa_norm_g)

    dqh_sw, dkh_sw, dvh_sw, dbias, dsink = _swa_bwd(qh_sw, kh_sw, vh_sw, bias, sink_rows, _heads_rows(dsw, SWA_HEADS))
    swa_small = _swa_small_grads(dbias, dsink, bucket)
    dqT_sb, dkT_sb, dvT_sb = _sb_bwd(qT_sb, kb_sb, _key_blocks_t(k_sb, SB_HEADS, T), _key_blocks(v_sb, SB_HEADS, T),
                                     _heads_cols(dsb, SB_HEADS), rsave)
    tok = lambda t, nh: t.reshape(nh, S, HEAD_DIM).transpose(1, 0, 2).reshape(S, nh * HEAD_DIM)
    tokT = lambda t: t.transpose(1, 3, 0, 2).reshape(S, SB_WIDTH)
    dproj = jnp.concatenate([dqT_sb.transpose(2, 0, 1).reshape(S, SB_WIDTH), tokT(dkT_sb), tokT(dvT_sb),
                             tok(dqh_sw, SWA_HEADS), tok(dkh_sw, SWA_KV_HEADS), tok(dvh_sw, SWA_KV_HEADS)],
                            axis=1).astype(_MXU)
    g_w_in = _matmul_tn(h0b, dproj, "grad_w_in", D_MODEL, IN_COLS // 2)
    grad_x, dg_in, db_in = _in_proj_bwd(dproj, w_in_f, dp1, x2, row(ln_in_g))

    cin, cff = IN_COLS // N_CHIPS, D_FF // N_CHIPS
    g_slab = jnp.stack([_to_slab(g_w_in[:, j * cin:(j + 1) * cin], g_w_out[j * (D_MODEL // N_CHIPS):(j + 1) * (D_MODEL // N_CHIPS)],
                                 (g_w_gate if j < 2 else g_w_up)[:, (j % 2) * FF_CHUNK:(j % 2 + 1) * FF_CHUNK],
                                 g_w_down[j * cff:(j + 1) * cff]) for j in range(N_CHIPS)])
    c = lax.axis_index("c")
    own = lax.dynamic_slice_in_dim(g_slab, c * HALF_ROWS, HALF_ROWS, axis=1)
    partial = _pair_sum(own, _swap_halves(g_slab), "pair_sum")
    g_shard = _join_halves(_chip_sum(_scatter_partials(partial), "chip_sum"))
    gs_in, gs_out, gs_gu, gs_down = _from_slab(g_shard)

    nrb = REL_BUCKETS * SWA_HEADS
    small = _pack_small(dg_in, db_in, dgsb, dgsw, swa_small[REL_BUCKETS, :SWA_HEADS],
                        swa_small[:REL_BUCKETS, :SWA_HEADS], dg1, db1, dg2, db2, errsum)
    g_small, loss_tile = _allreduce_small(small)
    loss = loss_tile[0, 0]

    big = []
    for name, w, g, m, v in (("adamw_w_in", w_in, gs_in, m_w_in, v_w_in), ("adamw_w_out", w_out, gs_out, m_w_out, v_w_out),
                             ("adamw_w_gate_up", w_gate_up, gs_gu, m_w_gate_up, v_w_gate_up),
                             ("adamw_w_down", w_down, gs_down, m_w_down, v_w_down)):
        d, nm, nv = _adamw(w[0], g, m[0], v[0], name)
        big.append((g[None], d[None], nm[None], nv[None]))
    zero = jnp.zeros((1,), F32)
    w_small = _pack_small(ln_in_g, ln_in_b, sb_norm_g, swa_norm_g, sinks, rel_bias, ln1_g, ln1_b, ln2_g, ln2_b, zero)
    m_small = _pack_small(m_ln_in_g, m_ln_in_b, m_sb_norm_g, m_swa_norm_g, m_sinks, m_rel_bias, m_ln1_g, m_ln1_b,
                          m_ln2_g, m_ln2_b, zero)
    v_small = _pack_small(v_ln_in_g, v_ln_in_b, v_sb_norm_g, v_swa_norm_g, v_sinks, v_rel_bias, v_ln1_g, v_ln1_b,
                          v_ln2_g, v_ln2_b, zero)
    small_out = [_unpack_small(t) for t in (g_small,) + tuple(_adamw(w_small, g_small, m_small, v_small, "adamw_small"))]

    def kind(k):
        s = small_out[k]
        return [s[0], s[1], big[0][k], s[2], s[3], s[4], s[5], big[1][k], s[6], s[7], big[2][k], big[3][k], s[8], s[9]]

    return (loss, grad_x.reshape(1, S, D_MODEL), *kind(0), *kind(1), *kind(2), *kind(3))
```

```python
import functools
import math

import numpy as np
import jax
import jax.numpy as jnp
from jax import lax
from jax.experimental import pallas as pl
from jax.experimental.pallas import tpu as pltpu

F32 = jnp.float32
_MXU = jnp.bfloat16

D_MODEL = 1024
HEAD_DIM = 64
SB_HEADS = 8
SWA_HEADS = 8
SWA_KV_HEADS = 2
SWA_GROUP = SWA_HEADS // SWA_KV_HEADS
SB_WIDTH = SB_HEADS * HEAD_DIM
SWA_WIDTH = SWA_HEADS * HEAD_DIM
SWA_KV_WIDTH = SWA_KV_HEADS * HEAD_DIM
IN_COLS = 3 * SB_WIDTH + SWA_WIDTH + 2 * SWA_KV_WIDTH
BLOCK = 128
REL_BUCKETS = 32
REL_MAX_DIST = 128
D_FF = 2816
FF_CHUNK = D_FF // 2
ALPHA = 2.0 ** 0.25
LN_EPS = 1e-5
RMS_EPS = 1e-6
SCALE = HEAD_DIM ** -0.5
SB_GROUP_FWD = 8
SB_GROUP_BWD = 4
SB_DEAD = -105.0
SWA_SUB = 8

ADAM_LR = 0.001
ADAM_B1 = 0.9
ADAM_B2 = 0.999
ADAM_EPS = 1e-08
ADAM_WD = 0.01
ADAM_STEP = 10

N_CHIPS = 4
SLAB_ROWS = (D_MODEL * IN_COLS // N_CHIPS + D_MODEL * D_MODEL // N_CHIPS
             + D_MODEL * 2 * D_FF // N_CHIPS + D_FF * D_MODEL // N_CHIPS) // D_MODEL
HALF_ROWS = SLAB_ROWS // 2
SMALL_ROWS = 16

MESH = pl.DeviceIdType.MESH


def _sds(shape, dtype):
    return jax.ShapeDtypeStruct(shape, dtype)


def _cp(sem=None, vmem_mb=48):
    kw = dict(vmem_limit_bytes=vmem_mb * 1024 * 1024)
    if sem is not None:
        kw["dimension_semantics"] = sem
    return pltpu.CompilerParams(**kw)


def _dot(a, b):
    return jnp.dot(a, b, preferred_element_type=F32)


def _dot_nt(a, b):
    return lax.dot_general(a, b, (((1,), (1,)), ((), ())), preferred_element_type=F32)


def _dot_tn(a, b):
    return lax.dot_general(a, b, (((0,), (0,)), ((), ())), preferred_element_type=F32)


def _ln_hat(x):
    mu = jnp.mean(x, axis=-1, keepdims=True)
    xc = x - mu
    var = jnp.mean(xc * xc, axis=-1, keepdims=True)
    rstd = lax.rsqrt(var + LN_EPS)
    return xc * rstd, rstd


def _ln_bwd(xhat, rstd, dy, g):
    dxh = dy * g
    m1 = jnp.mean(dxh, axis=-1, keepdims=True)
    m2 = jnp.mean(dxh * xhat, axis=-1, keepdims=True)
    return rstd * (dxh - m1 - xhat * m2)


def _colsum(x):
    return jnp.sum(x, axis=0, keepdims=True)


def _split2(x):
    hi = x.astype(_MXU)
    lo = (x - hi.astype(F32)).astype(_MXU)
    return hi, lo


def _rows(tm, n):
    return pl.BlockSpec((tm, n), lambda i: (i, 0))


def _fixed(*shape):
    nd = len(shape)
    return pl.BlockSpec(shape, lambda i: (0,) * nd)


def _ln_in_proj(x, g, b, w):
    S = x.shape[0]
    N = w.shape[1]
    tm = min(S, 512)

    def body(x_ref, g_ref, b_ref, w_ref, h_ref, hb_ref, p_ref):
        xhat, _ = _ln_hat(x_ref[...])
        h = xhat * g_ref[...] + b_ref[...]
        h_ref[...] = h
        hb = h.astype(_MXU)
        hb_ref[...] = hb
        p_ref[...] = _dot(hb, w_ref[...]).astype(p_ref.dtype)

    return pl.pallas_call(
        body, name="ln_in_proj", grid=(S // tm,),
        in_specs=[_rows(tm, D_MODEL), _fixed(1, D_MODEL), _fixed(1, D_MODEL), _fixed(D_MODEL, N)],
        out_specs=[_rows(tm, D_MODEL), _rows(tm, D_MODEL), _rows(tm, N)],
        out_shape=[_sds((S, D_MODEL), F32), _sds((S, D_MODEL), _MXU), _sds((S, N), _MXU)],
        compiler_params=_cp(("parallel",)),
    )(x, g, b, w)


def _rms(x, g):
    r = lax.rsqrt(jnp.mean(x * x, axis=-1, keepdims=True) + RMS_EPS)
    return x * r * g, r


def _mix_out(sb, sw, gsb, gsw, w_out, h0):
    S = sb.shape[0]
    tm = min(S, 512)

    def body(sb_ref, sw_ref, gsb_ref, gsw_ref, w_ref, h0_ref, pre_ref, mg_ref):
        ysb, _ = _rms(sb_ref[...], gsb_ref[...])
        ysw, _ = _rms(sw_ref[...], gsw_ref[...])
        ysb = ysb.astype(_MXU)
        ysw = ysw.astype(_MXU)
        mg_ref[:, :SB_WIDTH] = ysb
        mg_ref[:, SB_WIDTH:] = ysw
        mix = _dot(ysb, w_ref[:SB_WIDTH, :]) + _dot(ysw, w_ref[SB_WIDTH:, :])
        pre_ref[...] = ALPHA * h0_ref[...] + mix

    return pl.pallas_call(
        body, name="mix_out", grid=(S // tm,),
        in_specs=[_rows(tm, SB_WIDTH), _rows(tm, SWA_WIDTH), _fixed(1, SB_WIDTH), _fixed(1, SWA_WIDTH),
                  _fixed(D_MODEL, D_MODEL), _rows(tm, D_MODEL)],
        out_specs=[_rows(tm, D_MODEL), _rows(tm, D_MODEL)],
        out_shape=[_sds((S, D_MODEL), F32), _sds((S, D_MODEL), _MXU)],
        compiler_params=_cp(("parallel",)),
    )(sb, sw, gsb, gsw, w_out, h0)


def _sigmoid(x):
    return 1.0 / (1.0 + jnp.exp(-x))


def _ffn_up(pre1, g1, b1, wgu):
    S = pre1.shape[0]
    tm = min(S, 512)

    def body(p_ref, g_ref, b_ref, wg_ref, wu_ref, h1_ref, gate_ref, up_ref, a_ref):
        xhat, _ = _ln_hat(p_ref[...])
        h1 = (xhat * g_ref[...] + b_ref[...]).astype(_MXU)
        h1_ref[...] = h1
        gate = _dot(h1, wg_ref[0])
        up = _dot(h1, wu_ref[0])
        gate_ref[...] = gate.astype(gate_ref.dtype)
        up_ref[...] = up.astype(up_ref.dtype)
        a_ref[...] = (gate * _sigmoid(gate) * up).astype(a_ref.dtype)

    chunk = lambda i, j: (i, j)
    return pl.pallas_call(
        body, name="ffn_up", grid=(S // tm, 2),
        in_specs=[pl.BlockSpec((tm, D_MODEL), lambda i, j: (i, 0)),
                  pl.BlockSpec((1, D_MODEL), lambda i, j: (0, 0)),
                  pl.BlockSpec((1, D_MODEL), lambda i, j: (0, 0)),
                  pl.BlockSpec((1, D_MODEL, FF_CHUNK), lambda i, j: (j, 0, 0)),
                  pl.BlockSpec((1, D_MODEL, FF_CHUNK), lambda i, j: (j + 2, 0, 0))],
        out_specs=[pl.BlockSpec((tm, D_MODEL), lambda i, j: (i, 0)),
                   pl.BlockSpec((tm, FF_CHUNK), chunk), pl.BlockSpec((tm, FF_CHUNK), chunk),
                   pl.BlockSpec((tm, FF_CHUNK), chunk)],
        out_shape=[_sds((S, D_MODEL), _MXU), _sds((S, D_FF), _MXU), _sds((S, D_FF), _MXU), _sds((S, D_FF), _MXU)],
        compiler_params=_cp(("parallel", "arbitrary")),
    )(pre1, g1, b1, wgu, wgu)


def _ffn_down_loss(a, w_down, pre1, g1, b1, g2, b2, tgt):
    S = a.shape[0]
    tm = min(S, 512)

    def body(a_ref, w_ref, p_ref, g1_ref, b1_ref, g2_ref, b2_ref, t_ref, d_ref, db_ref, dg2_ref, db2_ref, err_ref):
        @pl.when(pl.program_id(0) == 0)
        def _():
            dg2_ref[...] = jnp.zeros_like(dg2_ref)
            db2_ref[...] = jnp.zeros_like(db2_ref)
            err_ref[...] = jnp.zeros_like(err_ref)

        xhat1, _ = _ln_hat(p_ref[...])
        h1 = xhat1 * g1_ref[...] + b1_ref[...]
        pre2 = ALPHA * h1 + _dot(a_ref[...], w_ref[...])
        xhat2, rstd2 = _ln_hat(pre2)
        err = xhat2 * g2_ref[...] + b2_ref[...] - t_ref[...]
        dh2 = err * (1.0 / D_MODEL)
        dp2 = _ln_bwd(xhat2, rstd2, dh2, g2_ref[...])
        d_ref[...] = dp2
        db_ref[...] = dp2.astype(db_ref.dtype)
        dg2_ref[...] += _colsum(dh2 * xhat2)
        db2_ref[...] += _colsum(dh2)
        err_ref[...] += _colsum(err * err)

    vec = _fixed(1, D_MODEL)
    return pl.pallas_call(
        body, name="ffn_down_loss", grid=(S // tm,),
        in_specs=[_rows(tm, D_FF), _fixed(D_FF, D_MODEL), _rows(tm, D_MODEL), vec, vec, vec, vec, _rows(tm, D_MODEL)],
        out_specs=[_rows(tm, D_MODEL), _rows(tm, D_MODEL), vec, vec, vec],
        out_shape=[_sds((S, D_MODEL), F32), _sds((S, D_MODEL), _MXU), _sds((1, D_MODEL), F32), _sds((1, D_MODEL), F32),
                   _sds((1, D_MODEL), F32)],
        compiler_params=_cp(("arbitrary",)),
    )(a, w_down, pre1, g1, b1, g2, b2, tgt)


def _ffn_down_bwd(dp2b, w_down, gate, up):
    S = dp2b.shape[0]
    tm = min(S, 512)

    def body(d_ref, w_ref, g_ref, u_ref, dg_ref, du_ref):
        da = _dot_nt(d_ref[...], w_ref[...])
        g = g_ref[...].astype(F32)
        u = u_ref[...].astype(F32)
        sg = _sigmoid(g)
        du_ref[...] = (da * g * sg).astype(du_ref.dtype)
        dg_ref[...] = (da * u * (sg * (1.0 + g * (1.0 - sg)))).astype(dg_ref.dtype)

    chunk = pl.BlockSpec((tm, FF_CHUNK), lambda i, j: (i, j))
    return pl.pallas_call(
        body, name="ffn_down_bwd", grid=(S // tm, 2),
        in_specs=[pl.BlockSpec((tm, D_MODEL), lambda i, j: (i, 0)),
                  pl.BlockSpec((FF_CHUNK, D_MODEL), lambda i, j: (j, 0)), chunk, chunk],
        out_specs=[chunk, chunk],
        out_shape=[_sds((S, D_FF), _MXU), _sds((S, D_FF), _MXU)],
        compiler_params=_cp(("parallel", "arbitrary")),
    )(dp2b, w_down, gate, up)


def _ffn_up_bwd(dgate, dup, wgu, dp2, pre1, g1):
    S = dgate.shape[0]
    tm = min(S, 256)

    def body(dg_ref, du_ref, w_ref, d2_ref, p_ref, g_ref, d1_ref, d1b_ref, dg1_ref, db1_ref):
        @pl.when(pl.program_id(0) == 0)
        def _():
            dg1_ref[...] = jnp.zeros_like(dg1_ref)
            db1_ref[...] = jnp.zeros_like(db1_ref)

        dh1 = ALPHA * d2_ref[...]
        for j in range(2):
            cols = slice(j * FF_CHUNK, (j + 1) * FF_CHUNK)
            dh1 += _dot_nt(dg_ref[:, cols], w_ref[j])
            dh1 += _dot_nt(du_ref[:, cols], w_ref[j + 2])
        xhat, rstd = _ln_hat(p_ref[...])
        dp1 = _ln_bwd(xhat, rstd, dh1, g_ref[...])
        d1_ref[...] = dp1
        d1b_ref[...] = dp1.astype(d1b_ref.dtype)
        dg1_ref[...] += _colsum(dh1 * xhat)
        db1_ref[...] += _colsum(dh1)

    vec = _fixed(1, D_MODEL)
    return pl.pallas_call(
        body, name="ffn_up_bwd", grid=(S // tm,),
        in_specs=[_rows(tm, D_FF), _rows(tm, D_FF), _fixed(4, D_MODEL, FF_CHUNK), _rows(tm, D_MODEL),
                  _rows(tm, D_MODEL), vec],
        out_specs=[_rows(tm, D_MODEL), _rows(tm, D_MODEL), vec, vec],
        out_shape=[_sds((S, D_MODEL), F32), _sds((S, D_MODEL), _MXU), _sds((1, D_MODEL), F32), _sds((1, D_MODEL), F32)],
        compiler_params=_cp(("arbitrary",), vmem_mb=56),
    )(dgate, dup, wgu, dp2, pre1, g1)


def _rms_bwd(x, g, dy):
    n = x.shape[-1]
    r = lax.rsqrt(jnp.mean(x * x, axis=-1, keepdims=True) + RMS_EPS)
    u = dy * g
    dx = r * u - x * (r * r * r) * (jnp.sum(u * x, axis=-1, keepdims=True) * (1.0 / n))
    return dx, _colsum(dy * x * r)


def _mix_bwd(dp1b, w_out, sb, sw, gsb, gsw):
    S = sb.shape[0]
    tm = min(S, 512)

    def body(d_ref, w_ref, sb_ref, sw_ref, gsb_ref, gsw_ref, dsb_ref, dsw_ref, dgsb_ref, dgsw_ref):
        @pl.when(pl.program_id(0) == 0)
        def _():
            dgsb_ref[...] = jnp.zeros_like(dgsb_ref)
            dgsw_ref[...] = jnp.zeros_like(dgsw_ref)

        dm = _dot_nt(d_ref[...], w_ref[...])
        dsb, dgsb = _rms_bwd(sb_ref[...], gsb_ref[...], dm[:, :SB_WIDTH])
        dsw, dgsw = _rms_bwd(sw_ref[...], gsw_ref[...], dm[:, SB_WIDTH:])
        dsb_ref[...] = dsb.astype(dsb_ref.dtype)
        dsw_ref[...] = dsw.astype(dsw_ref.dtype)
        dgsb_ref[...] += dgsb
        dgsw_ref[...] += dgsw

    return pl.pallas_call(
        body, name="mix_bwd", grid=(S // tm,),
        in_specs=[_rows(tm, D_MODEL), _fixed(D_MODEL, D_MODEL), _rows(tm, SB_WIDTH), _rows(tm, SWA_WIDTH),
                  _fixed(1, SB_WIDTH), _fixed(1, SWA_WIDTH)],
        out_specs=[_rows(tm, SB_WIDTH), _rows(tm, SWA_WIDTH), _fixed(1, SB_WIDTH), _fixed(1, SWA_WIDTH)],
        out_shape=[_sds((S, SB_WIDTH), _MXU), _sds((S, SWA_WIDTH), _MXU), _sds((1, SB_WIDTH), F32),
                   _sds((1, SWA_WIDTH), F32)],
        compiler_params=_cp(("arbitrary",)),
    )(dp1b, w_out, sb, sw, gsb, gsw)


def _in_proj_bwd(dproj, w_in, dp1, x, g):
    S = x.shape[0]
    N = dproj.shape[1]
    tm = min(S, 512)

    def body(dpj_ref, w_ref, d1_ref, x_ref, g_ref, gx_ref, dg_ref, db_ref):
        @pl.when(pl.program_id(0) == 0)
        def _():
            dg_ref[...] = jnp.zeros_like(dg_ref)
            db_ref[...] = jnp.zeros_like(db_ref)

        dh0 = _dot_nt(dpj_ref[...], w_ref[...]) + ALPHA * d1_ref[...]
        xhat, rstd = _ln_hat(x_ref[...])
        gx_ref[...] = _ln_bwd(xhat, rstd, dh0, g_ref[...])
        dg_ref[...] += _colsum(dh0 * xhat)
        db_ref[...] += _colsum(dh0)

    vec = _fixed(1, D_MODEL)
    return pl.pallas_call(
        body, name="in_proj_bwd", grid=(S // tm,),
        in_specs=[_rows(tm, N), _fixed(D_MODEL, N), _rows(tm, D_MODEL), _rows(tm, D_MODEL), vec],
        out_specs=[_rows(tm, D_MODEL), vec, vec],
        out_shape=[_sds((S, D_MODEL), F32), _sds((1, D_MODEL), F32), _sds((1, D_MODEL), F32)],
        compiler_params=_cp(("arbitrary",)),
    )(dproj, w_in, dp1, x, g)


def _matmul_tn(a, b, name, tk, tn):
    T, K = a.shape
    N = b.shape[1]
    tt = min(T, 512)

    def body(a_ref, b_ref, o_ref):
        @pl.when(pl.program_id(2) == 0)
        def _():
            o_ref[...] = jnp.zeros_like(o_ref)

        o_ref[...] += _dot_tn(a_ref[...], b_ref[...])

    return pl.pallas_call(
        body, name=name, grid=(K // tk, N // tn, T // tt),
        in_specs=[pl.BlockSpec((tt, tk), lambda k, n, t: (t, k)), pl.BlockSpec((tt, tn), lambda k, n, t: (t, n))],
        out_specs=pl.BlockSpec((tk, tn), lambda k, n, t: (k, n)),
        out_shape=_sds((K, N), F32),
        compiler_params=_cp(("parallel", "parallel", "arbitrary")),
    )(a, b)


def _sb_logs(zt, causal):
    e = jnp.exp(-jnp.abs(zt))
    lb = jnp.minimum(zt, 0.0) - jnp.log(1.0 + e)
    l1m = lb - zt
    if causal is not None:
        l1m = jnp.where(causal, l1m, 0.0)
    return lb, l1m


def _sb_weights(lb, suf, causal):
    a = jnp.exp(lb + suf)
    if causal is not None:
        a = jnp.where(causal, a, 0.0)
    return a


def _tri_masks(t):
    r = lax.broadcasted_iota(jnp.int32, (t, t), 0)
    c = lax.broadcasted_iota(jnp.int32, (t, t), 1)
    return r, c


def _sb_fwd(qT, kb, vTb):
    Hh, _, S = qT.shape
    nk, T = kb.shape[1], kb.shape[2]
    nq = S // T
    G = SB_GROUP_FWD

    def body(qT_ref, k_ref, vT_ref, oT_ref, rs_ref):
        i = pl.program_id(1)
        qts = [(qT_ref[g].astype(F32) * SCALE).astype(_MXU) for g in range(G)]
        r, c = _tri_masks(T)
        upper = (c > r).astype(_MXU)
        causal = r < c

        def blk(j, carry, mask):
            hs = range(G)
            for g in hs:
                rs_ref[g, 0, j] = jnp.broadcast_to(carry[g][0], (8, T))
            zs = [_dot(k_ref[g, j], qts[g]) for g in hs]
            lbs, l1ms = zip(*[_sb_logs(zs[g], mask) for g in hs])
            splits = [_split2(l1ms[g]) for g in hs]
            cums = [_dot(upper, splits[g][0]) + _dot(upper, splits[g][1]) for g in hs]
            avs = [_sb_weights(lbs[g], carry[g][0] + cums[g], mask).astype(_MXU) for g in hs]
            accs = [carry[g][1] + _dot(vT_ref[g, j], avs[g]) for g in hs]
            return tuple((carry[g][0] + _colsum(l1ms[g]), accs[g]) for g in hs)

        def go_on(j, carry):
            top = carry[0][0]
            for g in range(1, G):
                top = jnp.maximum(top, carry[g][0])
            return jnp.logical_and(j >= 0, jnp.max(top) >= SB_DEAD)

        init = tuple((jnp.zeros((1, T), F32), jnp.zeros((HEAD_DIM, T), F32)) for _ in range(G))
        carry = blk(i, init, causal)
        j, carry = lax.while_loop(lambda st: go_on(*st), lambda st: (st[0] - 1, blk(st[0], st[1], None)),
                                  (i - 1, carry))

        @pl.when(j >= 0)
        def _():
            for g in range(G):
                rs_ref[g, 0, j] = jnp.broadcast_to(carry[g][0], (8, T))

        for g in range(G):
            oT_ref[g] = carry[g][1]

    return pl.pallas_call(
        body, name="sb_fwd", grid=(Hh // G, nq),
        in_specs=[pl.BlockSpec((G, HEAD_DIM, T), lambda h, i: (h, 0, i)),
                  pl.BlockSpec((G, nk, T, HEAD_DIM), lambda h, i: (h, 0, 0, 0), pipeline_mode=pl.Buffered(1)),
                  pl.BlockSpec((G, nk, HEAD_DIM, T), lambda h, i: (h, 0, 0, 0), pipeline_mode=pl.Buffered(1))],
        out_specs=[pl.BlockSpec((G, HEAD_DIM, T), lambda h, i: (h, 0, i)),
                   pl.BlockSpec((G, 1, nk, 8, T), lambda h, i: (h, i, 0, 0, 0))],
        out_shape=[_sds((Hh, HEAD_DIM, S), F32), _sds((Hh, nq, nk, 8, T), F32)],
        compiler_params=_cp(("parallel", "arbitrary")),
    )(qT, kb, vTb)


def _sb_bwd(qT, kb, kTb, vb, doT, rsave):
    Hh, _, S = qT.shape
    nk, T = kb.shape[1], kb.shape[2]
    nq = S // T
    G = SB_GROUP_BWD

    def body(qT_ref, k_ref, kT_ref, v_ref, doT_ref, rs_ref, dqT_ref, dk_ref, dv_ref):
        i = pl.program_id(1)

        @pl.when(i == 0)
        def _():
            dk_ref[...] = jnp.zeros_like(dk_ref)
            dv_ref[...] = jnp.zeros_like(dv_ref)

        qts = [(qT_ref[g].astype(F32) * SCALE).astype(_MXU) for g in range(G)]
        douts = [doT_ref[g] for g in range(G)]
        r, c = _tri_masks(T)
        upper = (c > r).astype(_MXU)
        lower = (c < r).astype(_MXU)
        causal = r < c

        def blk(j, carry, mask):
            hs = range(G)
            zs = [_dot(k_ref[g, j], qts[g]) for g in hs]
            das = [_dot(v_ref[g, j], douts[g]) for g in hs]
            lbs, l1ms = zip(*[_sb_logs(zs[g], mask) for g in hs])
            splits = [_split2(l1ms[g]) for g in hs]
            cums = [_dot(upper, splits[g][0]) + _dot(upper, splits[g][1]) for g in hs]
            avs = [_sb_weights(lbs[g], rs_ref[g, 0, j][0:1, :] + cums[g], mask) for g in hs]
            ets = [das[g] * avs[g] for g in hs]
            esplits = [_split2(ets[g]) for g in hs]
            ecums = [_dot(lower, esplits[g][0]) + _dot(lower, esplits[g][1]) for g in hs]
            dzs = []
            for g in hs:
                sig = jnp.exp(lbs[g])
                dz = ets[g] * (1.0 - sig) - (carry[g][0] + ecums[g]) * sig
                if mask is not None:
                    dz = jnp.where(mask, dz, 0.0)
                dzs.append(dz.astype(_MXU))
            dqs = [carry[g][1] + _dot(kT_ref[g, j], dzs[g]) for g in hs]
            for g in hs:
                dk_ref[g, j] += _dot_nt(qts[g], dzs[g])
            for g in hs:
                dv_ref[g, j] += _dot_nt(douts[g], avs[g].astype(_MXU))
            return tuple((carry[g][0] + _colsum(ets[g]), dqs[g]) for g in hs)

        def live(j):
            jj = jnp.maximum(j, 0)
            top = rs_ref[0, 0, jj][0:1, :]
            for g in range(1, G):
                top = jnp.maximum(top, rs_ref[g, 0, jj][0:1, :])
            return jnp.logical_and(j >= 0, jnp.max(top) >= SB_DEAD)

        first = lax.while_loop(lambda st: st[1], lambda st: (st[0] - 1, live(st[0] - 2)), (i, live(i - 1)))[0]
        carry = tuple((jnp.zeros((1, T), F32), jnp.zeros((HEAD_DIM, T), F32)) for _ in range(G))
        carry = lax.fori_loop(first, i, lambda s, cr: blk(s, cr, None), carry)
        carry = blk(i, carry, causal)
        for g in range(G):
            dqT_ref[g] = carry[g][1] * SCALE

    colblk = pl.BlockSpec((G, HEAD_DIM, T), lambda h, i: (h, 0, i))
    once = pl.Buffered(1)
    kblk = pl.BlockSpec((G, nk, T, HEAD_DIM), lambda h, i: (h, 0, 0, 0), pipeline_mode=once)
    kTblk = pl.BlockSpec((G, nk, HEAD_DIM, T), lambda h, i: (h, 0, 0, 0), pipeline_mode=once)
    return pl.pallas_call(
        body, name="sb_bwd", grid=(Hh // G, nq),
        in_specs=[colblk, kblk, kTblk, kblk, colblk,
                  pl.BlockSpec((G, 1, nk, 8, T), lambda h, i: (h, i, 0, 0, 0))],
        out_specs=[colblk, kTblk, kTblk],
        out_shape=[_sds((Hh, HEAD_DIM, S), F32), _sds((Hh, nk, HEAD_DIM, T), F32), _sds((Hh, nk, HEAD_DIM, T), F32)],
        compiler_params=_cp(("parallel", "arbitrary"), vmem_mb=56),
    )(qT, kb, kTb, vb, doT, rsave)


def _bucket_table():
    qi = np.arange(BLOCK)[:, None]
    cj = np.arange(2 * BLOCK)[None, :]
    dist = qi + BLOCK - cj
    exact = REL_BUCKETS // 2
    d = np.maximum(dist, 0)
    d_f = np.maximum(d, 1).astype(np.float32)
    large = exact + (np.log(d_f / np.float32(exact)) / np.float32(math.log(REL_MAX_DIST / exact))
                     * np.float32(REL_BUCKETS - exact)).astype(np.int32)
    large = np.minimum(large, REL_BUCKETS - 1)
    return np.where(d < exact, d, large).astype(np.int32)


def _swa_bias(rel_bias, bucket):
    def body(rb_ref, bk_ref, o_ref):
        bk = bk_ref[...]
        for h in range(SWA_HEADS):
            t = jnp.zeros((BLOCK, 2 * BLOCK), F32)
            for b in range(REL_BUCKETS):
                t = jnp.where(bk == b, rb_ref[b, h], t)
            o_ref[h] = t

    return pl.pallas_call(
        body, name="swa_bias",
        in_specs=[pl.BlockSpec(memory_space=pltpu.SMEM), pl.BlockSpec(memory_space=pltpu.VMEM)],
        out_specs=pl.BlockSpec(memory_space=pltpu.VMEM),
        out_shape=_sds((SWA_HEADS, BLOCK, 2 * BLOCK), F32),
    )(rel_bias, bucket)


def _swa_logits(q, kp, kc):
    qs = (q.astype(F32) * SCALE).astype(_MXU)
    return qs, _dot_nt(qs, kp), _dot_nt(qs, kc)


def _swa_softmax(lp, lc, bias, sink, live_prev):
    r, c = _tri_masks(BLOCK)
    in_window = c > r if live_prev is None else jnp.logical_and(c > r, live_prev)
    lp = jnp.where(in_window, lp + bias[:, :BLOCK], -jnp.inf)
    lc = jnp.where(c <= r, lc + bias[:, BLOCK:], -jnp.inf)
    m = jnp.maximum(jnp.maximum(jnp.max(lp, axis=1, keepdims=True), jnp.max(lc, axis=1, keepdims=True)), sink)
    pp = jnp.exp(lp - m)
    pc = jnp.exp(lc - m)
    ps = jnp.exp(sink - m)
    denom = jnp.sum(pp, axis=1, keepdims=True) + jnp.sum(pc, axis=1, keepdims=True) + ps
    return pp / denom, pc / denom, ps / denom


def _swa_sub(nb):
    return min(SWA_SUB, nb)


def _swa_keys(b, prev_ref, cur_ref, i):
    cur = cur_ref[0, b * BLOCK:(b + 1) * BLOCK, :]
    if b == 0:
        return prev_ref[0], cur, i > 0
    return cur_ref[0, (b - 1) * BLOCK:b * BLOCK, :], cur, None


def _swa_fwd(q, k, v, bias, sink):
    S = q.shape[1]
    nb = S // BLOCK
    ns = _swa_sub(nb)
    R = ns * BLOCK

    def body(q_ref, kp_ref, kc_ref, vp_ref, vc_ref, bias_ref, sink_ref, o_ref):
        i = pl.program_id(1)
        bias = bias_ref[0]
        sink = sink_ref[0][:, :1]
        subs = range(ns)
        rows = [slice(b * BLOCK, (b + 1) * BLOCK) for b in subs]
        keys = [_swa_keys(b, kp_ref, kc_ref, i) for b in subs]
        vals = [_swa_keys(b, vp_ref, vc_ref, i) for b in subs]
        logits = [_swa_logits(q_ref[0, rows[b], :], keys[b][0], keys[b][1]) for b in subs]
        ws = [_swa_softmax(logits[b][1], logits[b][2], bias, sink, keys[b][2]) for b in subs]
        for b in subs:
            o_ref[0, rows[b], :] = _dot(ws[b][0].astype(_MXU), vals[b][0]) + _dot(ws[b][1].astype(_MXU), vals[b][1])

    prev = pl.BlockSpec((1, BLOCK, HEAD_DIM), lambda h, i: (h // SWA_GROUP, jnp.maximum(i * ns - 1, 0), 0))
    cur = pl.BlockSpec((1, R, HEAD_DIM), lambda h, i: (h // SWA_GROUP, i, 0))
    qblk = pl.BlockSpec((1, R, HEAD_DIM), lambda h, i: (h, i, 0))
    return pl.pallas_call(
        body, name="swa_fwd", grid=(SWA_HEADS, nb // ns),
        in_specs=[qblk, prev, cur, prev, cur,
                  pl.BlockSpec((1, BLOCK, 2 * BLOCK), lambda h, i: (h, 0, 0)),
                  pl.BlockSpec((1, 1, BLOCK), lambda h, i: (h, 0, 0))],
        out_specs=qblk,
        out_shape=_sds((SWA_HEADS, S, HEAD_DIM), F32),
        compiler_params=_cp(("parallel", "parallel")),
    )(q, k, k, v, v, bias, sink)


def _swa_bwd(q, k, v, bias, sink, do):
    S = q.shape[1]
    nb = S // BLOCK
    ns = _swa_sub(nb)
    R = ns * BLOCK

    def body(q_ref, kp_ref, kc_ref, vp_ref, vc_ref, bias_ref, sink_ref, do_ref, dq_ref, dk_ref, dv_ref, dbias_ref,
             dsink_ref):
        g = pl.program_id(1)
        i = pl.program_id(2)

        @pl.when(jnp.logical_and(g == 0, i == 0))
        def _():
            dk_ref[...] = jnp.zeros_like(dk_ref)
            dv_ref[...] = jnp.zeros_like(dv_ref)

        @pl.when(i == 0)
        def _():
            dbias_ref[...] = jnp.zeros_like(dbias_ref)
            dsink_ref[...] = jnp.zeros_like(dsink_ref)

        bias = bias_ref[0]
        sink = sink_ref[0][:, :1]
        subs = range(ns)
        rows = [slice(b * BLOCK, (b + 1) * BLOCK) for b in subs]
        keys = [_swa_keys(b, kp_ref, kc_ref, i) for b in subs]
        vals = [_swa_keys(b, vp_ref, vc_ref, i) for b in subs]
        douts = [do_ref[0, rows[b], :] for b in subs]
        logits = [_swa_logits(q_ref[0, rows[b], :], keys[b][0], keys[b][1]) for b in subs]
        dws = [(_dot_nt(douts[b], vals[b][0]), _dot_nt(douts[b], vals[b][1])) for b in subs]
        dbp = jnp.zeros((BLOCK, BLOCK), F32)
        dbc = jnp.zeros((BLOCK, BLOCK), F32)
        dsk = jnp.zeros((BLOCK, 1), F32)
        wts, dls = [], []
        for b in subs:
            wp, wc, ws = _swa_softmax(logits[b][1], logits[b][2], bias, sink, keys[b][2])
            dwp, dwc = dws[b]
            delta = jnp.sum(wp * dwp, axis=1, keepdims=True) + jnp.sum(wc * dwc, axis=1, keepdims=True)
            dlp = wp * (dwp - delta)
            dlc = wc * (dwc - delta)
            dbp += dlp
            dbc += dlc
            dsk -= ws * delta
            wts.append((wp.astype(_MXU), wc.astype(_MXU)))
            dls.append((dlp.astype(_MXU), dlc.astype(_MXU)))
        for b in subs:
            dq_ref[0, rows[b], :] = (_dot(dls[b][0], keys[b][0]) + _dot(dls[b][1], keys[b][1])) * SCALE
        for b in subs:
            qs = logits[b][0]
            blk = i * ns + b
            dk_ref[0, blk] += _dot_tn(dls[b][1], qs)
            dv_ref[0, blk] += _dot_tn(wts[b][1], douts[b])
            if b == 0:
                @pl.when(i > 0)
                def _():
                    dk_ref[0, blk - 1] += _dot_tn(dls[0][0], qs)
                    dv_ref[0, blk - 1] += _dot_tn(wts[0][0], douts[0])
            else:
                dk_ref[0, blk - 1] += _dot_tn(dls[b][0], qs)
                dv_ref[0, blk - 1] += _dot_tn(wts[b][0], douts[b])
        dbias_ref[0, :, :BLOCK] += dbp
        dbias_ref[0, :, BLOCK:] += dbc
        dsink_ref[0] += jnp.broadcast_to(dsk, (BLOCK, BLOCK))

    hq = lambda kv, g, i: kv * SWA_GROUP + g
    prev = pl.BlockSpec((1, BLOCK, HEAD_DIM), lambda kv, g, i: (kv, jnp.maximum(i * ns - 1, 0), 0))
    cur = pl.BlockSpec((1, R, HEAD_DIM), lambda kv, g, i: (kv, i, 0))
    qblk = pl.BlockSpec((1, R, HEAD_DIM), lambda kv, g, i: (hq(kv, g, i), i, 0))
    kvacc = pl.BlockSpec((1, nb, BLOCK, HEAD_DIM), lambda kv, g, i: (kv, 0, 0, 0))
    return pl.pallas_call(
        body, name="swa_bwd", grid=(SWA_KV_HEADS, SWA_GROUP, nb // ns),
        in_specs=[qblk, prev, cur, prev, cur,
                  pl.BlockSpec((1, BLOCK, 2 * BLOCK), lambda kv, g, i: (hq(kv, g, i), 0, 0)),
                  pl.BlockSpec((1, 1, BLOCK), lambda kv, g, i: (hq(kv, g, i), 0, 0)), qblk],
        out_specs=[qblk, kvacc, kvacc,
                   pl.BlockSpec((1, BLOCK, 2 * BLOCK), lambda kv, g, i: (hq(kv, g, i), 0, 0)),
                   pl.BlockSpec((1, BLOCK, BLOCK), lambda kv, g, i: (hq(kv, g, i), 0, 0))],
        out_shape=[_sds((SWA_HEADS, S, HEAD_DIM), F32), _sds((SWA_KV_HEADS, nb, BLOCK, HEAD_DIM), F32),
                   _sds((SWA_KV_HEADS, nb, BLOCK, HEAD_DIM), F32), _sds((SWA_HEADS, BLOCK, 2 * BLOCK), F32),
                   _sds((SWA_HEADS, BLOCK, BLOCK), F32)],
        compiler_params=_cp(("arbitrary", "arbitrary", "arbitrary")),
    )(q, k, k, v, v, bias, sink, do)


def _swa_small_grads(dbias, dsink, bucket):
    rows = REL_BUCKETS + 8

    def total(x):
        return jnp.sum(jnp.sum(x, axis=1, keepdims=True), axis=0, keepdims=True)

    def body(db_ref, ds_ref, bk_ref, o_ref):
        bk = bk_ref[...]
        r = lax.broadcasted_iota(jnp.int32, (rows, BLOCK), 0)
        c = lax.broadcasted_iota(jnp.int32, (rows, BLOCK), 1)
        out = jnp.zeros((rows, BLOCK), F32)
        for h in range(SWA_HEADS):
            db = db_ref[h]
            for b in range(REL_BUCKETS):
                s = total(jnp.where(bk == b, db, 0.0))
                out = jnp.where(jnp.logical_and(r == b, c == h), s, out)
            s = jnp.sum(ds_ref[h][:, :1], axis=0, keepdims=True)
            out = jnp.where(jnp.logical_and(r == REL_BUCKETS, c == h), s, out)
        o_ref[...] = out

    vm = pl.BlockSpec(memory_space=pltpu.VMEM)
    return pl.pallas_call(body, name="swa_small_grads", in_specs=[vm, vm, vm], out_specs=vm,
                          out_shape=_sds((rows, BLOCK), F32))(dbias, dsink, bucket)


def _tile_rows(n):
    for t in (512, 368, 256, 184, 128, 64, 32, 16, 8):
        if n % t == 0:
            return t
    return n


def _cast_rows(x, dtype, name):
    R, C = x.shape
    tr = _tile_rows(R)

    def body(x_ref, o_ref):
        o_ref[...] = x_ref[...].astype(o_ref.dtype)

    return pl.pallas_call(body, name=name, grid=(R // tr,), in_specs=[_rows(tr, C)], out_specs=_rows(tr, C),
                          out_shape=_sds((R, C), dtype), compiler_params=_cp(("parallel",)))(x)


def _pair_sum(g, recv, name):
    n, R, C = g.shape
    tr = _tile_rows(R)

    def body(a_ref, b_ref, o_ref):
        o_ref[...] = (a_ref[...] + b_ref[...]).astype(o_ref.dtype)

    blk = pl.BlockSpec((1, tr, C), lambda j, i: (j, i, 0))
    return pl.pallas_call(body, name=name, grid=(n, R // tr), in_specs=[blk, blk], out_specs=blk,
                          out_shape=_sds((n, R, C), _MXU),
                          compiler_params=_cp(("parallel", "parallel")))(g, recv)


def _chip_sum(parts, name):
    n, R, C = parts.shape
    tr = _tile_rows(R)

    def body(p_ref, o_ref):
        acc = p_ref[0].astype(F32)
        for j in range(1, n):
            acc = acc + p_ref[j].astype(F32)
        o_ref[...] = acc

    return pl.pallas_call(body, name=name, grid=(R // tr,),
                          in_specs=[pl.BlockSpec((n, tr, C), lambda i: (0, i, 0))], out_specs=_rows(tr, C),
                          out_shape=_sds((R, C), F32), compiler_params=_cp(("parallel",)))(parts)


def _adamw_math(w, g, m, v):
    m = ADAM_B1 * m + (1.0 - ADAM_B1) * g
    v = ADAM_B2 * v + (1.0 - ADAM_B2) * (g * g)
    m_hat = m / (1.0 - ADAM_B1 ** ADAM_STEP)
    v_hat = v / (1.0 - ADAM_B2 ** ADAM_STEP)
    delta = -ADAM_LR * (m_hat / (jnp.sqrt(v_hat) + ADAM_EPS) + ADAM_WD * w)
    return delta, m, v


def _adamw(w, g, m, v, name):
    R, C = w.shape
    tr = _tile_rows(R)

    def body(w_ref, g_ref, m_ref, v_ref, d_ref, nm_ref, nv_ref):
        d, nm, nv = _adamw_math(w_ref[...], g_ref[...], m_ref[...], v_ref[...])
        d_ref[...] = d
        nm_ref[...] = nm
        nv_ref[...] = nv

    blk = _rows(tr, C)
    return pl.pallas_call(body, name=name, grid=(R // tr,), in_specs=[blk] * 4, out_specs=[blk] * 3,
                          out_shape=[_sds((R, C), F32)] * 3, compiler_params=_cp(("parallel",)))(w, g, m, v)


def _place():
    x, y, c = lax.axis_index("x"), lax.axis_index("y"), lax.axis_index("c")
    chips = [(1 - x, y), (x, 1 - y), (1 - x, 1 - y)]
    return x, y, c, chips


def _gather_weights(slab):
    rows, cols = slab.shape
    half = rows // 2

    def body(in_ref, out_ref, send_sems, recv_sems, local_sem):
        x, y, c, chips = _place()
        sibling = (x, y, 1 - c)

        def part(chip, hc):
            return out_ref.at[2 * chip[0] + chip[1], pl.ds(pl.multiple_of(hc * half, 16), half), :]

        def copy(k, chip, hc, to, src=None):
            return pltpu.make_async_remote_copy(
                src_ref=part(chip, hc) if src is None else src, dst_ref=part(chip, hc),
                send_sem=send_sems.at[k], recv_sem=recv_sems.at[k], device_id=to, device_id_type=MESH)

        mine = pltpu.make_async_copy(in_ref, out_ref.at[2 * x + y], local_sem)
        mine.start()
        my_half = in_ref.at[pl.ds(pl.multiple_of(c * half, 16), half), :]
        first = [copy(j, (x, y), c, (*chip, c), src=my_half) for j, chip in enumerate(chips)]
        for cp in first:
            cp.start()
        passed = [copy(3 + j, chip, c, sibling) for j, chip in enumerate(chips)]
        for j, chip in enumerate(chips):
            copy(j, chip, c, (x, y, c)).wait_recv()
            passed[j].start()
        for j, chip in enumerate(chips):
            copy(3 + j, chip, 1 - c, (x, y, c)).wait_recv()
        for cp in first + passed:
            cp.wait_send()
        mine.wait()

    any_spec = pl.BlockSpec(memory_space=pl.ANY)
    return pl.pallas_call(
        body, name="gather_weights", in_specs=[any_spec], out_specs=any_spec,
        out_shape=_sds((N_CHIPS, rows, cols), slab.dtype),
        scratch_shapes=[pltpu.SemaphoreType.DMA((6,)), pltpu.SemaphoreType.DMA((6,)), pltpu.SemaphoreType.DMA],
    )(slab)


def _swap_halves(g):
    n, rows, cols = g.shape
    half = rows // 2

    def body(g_ref, out_ref, send_sem, recv_sem):
        x, y, c, _ = _place()
        theirs = g_ref.at[:, pl.ds(pl.multiple_of((1 - c) * half, 8), half), :]
        cp = pltpu.make_async_remote_copy(src_ref=theirs, dst_ref=out_ref, send_sem=send_sem, recv_sem=recv_sem,
                                          device_id=(x, y, 1 - c), device_id_type=MESH)
        cp.start()
        cp.wait()

    any_spec = pl.BlockSpec(memory_space=pl.ANY)
    return pl.pallas_call(
        body, name="swap_halves", in_specs=[any_spec], out_specs=any_spec, out_shape=_sds((n, half, cols), g.dtype),
        scratch_shapes=[pltpu.SemaphoreType.DMA, pltpu.SemaphoreType.DMA],
    )(g)


def _scatter_partials(p):
    n, half, cols = p.shape

    def body(p_ref, out_ref, send_sems, recv_sems, local_sem):
        x, y, c, chips = _place()
        me = 2 * x + y
        mine = pltpu.make_async_copy(p_ref.at[me], out_ref.at[me], local_sem)
        mine.start()
        sends = [pltpu.make_async_remote_copy(
            src_ref=p_ref.at[2 * chip[0] + chip[1]], dst_ref=out_ref.at[me], send_sem=send_sems.at[j],
            recv_sem=recv_sems.at[j], device_id=(*chip, c), device_id_type=MESH) for j, chip in enumerate(chips)]
        for cp in sends:
            cp.start()
        for j, chip in enumerate(chips):
            pltpu.make_async_remote_copy(
                src_ref=p_ref.at[me], dst_ref=out_ref.at[2 * chip[0] + chip[1]], send_sem=send_sems.at[j],
                recv_sem=recv_sems.at[j], device_id=(*chip, c), device_id_type=MESH).wait_recv()
        for cp in sends:
            cp.wait_send()
        mine.wait()

    any_spec = pl.BlockSpec(memory_space=pl.ANY)
    return pl.pallas_call(
        body, name="scatter_partials", in_specs=[any_spec], out_specs=any_spec, out_shape=_sds((n, half, cols), p.dtype),
        scratch_shapes=[pltpu.SemaphoreType.DMA((3,)), pltpu.SemaphoreType.DMA((3,)), pltpu.SemaphoreType.DMA],
    )(p)


def _join_halves(f):
    half, cols = f.shape

    def body(f_ref, out_ref, send_sem, recv_sem, local_sem):
        x, y, c, _ = _place()
        mine_rows = out_ref.at[pl.ds(pl.multiple_of(c * half, 8), half), :]
        mine = pltpu.make_async_copy(f_ref, mine_rows, local_sem)
        mine.start()
        cp = pltpu.make_async_remote_copy(src_ref=f_ref, dst_ref=mine_rows, send_sem=send_sem, recv_sem=recv_sem,
                                          device_id=(x, y, 1 - c), device_id_type=MESH)
        cp.start()
        their_rows = out_ref.at[pl.ds(pl.multiple_of((1 - c) * half, 8), half), :]
        pltpu.make_async_remote_copy(src_ref=f_ref, dst_ref=their_rows, send_sem=send_sem, recv_sem=recv_sem,
                                     device_id=(x, y, 1 - c), device_id_type=MESH).wait_recv()
        cp.wait_send()
        mine.wait()

    any_spec = pl.BlockSpec(memory_space=pl.ANY)
    return pl.pallas_call(
        body, name="join_halves", in_specs=[any_spec], out_specs=any_spec, out_shape=_sds((2 * half, cols), f.dtype),
        scratch_shapes=[pltpu.SemaphoreType.DMA, pltpu.SemaphoreType.DMA, pltpu.SemaphoreType.DMA],
    )(f)


def _allreduce_small(block):
    m_per, n = block.shape

    def body(x_ref, sum_ref, loss_ref, all_ref, send_sems, recv_sems, local_sem):
        x, y, c, chips = _place()
        me, sibling = (x, y, c), (x, y, 1 - c)

        def rows(px, py, pc):
            return all_ref.at[pl.ds(pl.multiple_of((4 * px + 2 * py + pc) * m_per, 8), m_per), :]

        def copy(k, blk, to, src=None):
            return pltpu.make_async_remote_copy(
                src_ref=rows(*blk) if src is None else src, dst_ref=rows(*blk), send_sem=send_sems.at[k],
                recv_sem=recv_sems.at[k], device_id=to, device_id_type=MESH)

        mine = pltpu.make_async_copy(x_ref, rows(*me), local_sem)
        mine.start()
        first = [copy(0, me, sibling, src=x_ref)]
        first += [copy(1 + j, me, (*chip, c), src=x_ref) for j, chip in enumerate(chips)]
        for cp in first:
            cp.start()
        passed = [copy(4 + j, (*chip, c), sibling) for j, chip in enumerate(chips)]
        for j, chip in enumerate(chips):
            copy(1 + j, (*chip, c), me).wait_recv()
            passed[j].start()
        copy(0, sibling, me).wait_recv()
        for j, chip in enumerate(chips):
            copy(4 + j, (*chip, 1 - c), me).wait_recv()
        for cp in first + passed:
            cp.wait_send()
        mine.wait()

        acc = all_ref[0:m_per, :]
        for d in range(1, 8):
            acc = acc + all_ref[d * m_per:(d + 1) * m_per, :]
        sum_ref[...] = acc
        tot = jnp.sum(acc[8:9, :], axis=1, keepdims=True) * (0.5 / D_MODEL)
        loss_ref[...] = jnp.broadcast_to(tot, loss_ref.shape)

    vm = pl.BlockSpec(memory_space=pltpu.VMEM)
    return pl.pallas_call(
        body, name="allreduce_small", in_specs=[vm], out_specs=[vm, vm],
        out_shape=[_sds((m_per, n), F32), _sds((8, 128), F32)],
        scratch_shapes=[pltpu.VMEM((8 * m_per, n), F32), pltpu.SemaphoreType.DMA((7,)), pltpu.SemaphoreType.DMA((7,)),
                        pltpu.SemaphoreType.DMA],
    )(block)


def _slab_sections():
    return [D_MODEL * IN_COLS // N_CHIPS // D_MODEL, D_MODEL // N_CHIPS, 2 * D_FF // N_CHIPS, D_FF // N_CHIPS]


def _to_slab(w_in_s, w_out_s, w_gu_s, w_down_s):
    return jnp.concatenate([w_in_s.reshape(-1, D_MODEL), w_out_s, w_gu_s.reshape(-1, D_MODEL), w_down_s], axis=0)


def _from_slab(slab):
    lead = slab.shape[:-2]
    r0, r1, r2, r3 = _slab_sections()
    o1, o2, o3 = r0, r0 + r1, r0 + r1 + r2
    return (slab[..., :o1, :].reshape(*lead, D_MODEL, IN_COLS // N_CHIPS), slab[..., o1:o2, :],
            slab[..., o2:o3, :].reshape(*lead, D_MODEL, FF_CHUNK), slab[..., o3:, :])


def _heads_rows(x, nh):
    S = x.shape[0]
    return x.reshape(S, nh, HEAD_DIM).transpose(1, 0, 2)


def _heads_cols(x, nh):
    S = x.shape[0]
    return x.reshape(S, nh, HEAD_DIM).transpose(1, 2, 0)


def _key_blocks(x, nh, t):
    S = x.shape[0]
    return x.reshape(S // t, t, nh, HEAD_DIM).transpose(2, 0, 1, 3)


def _key_blocks_t(x, nh, t):
    S = x.shape[0]
    return x.reshape(S // t, t, nh, HEAD_DIM).transpose(2, 0, 3, 1)


def _pad_row(v):
    v = v.reshape(1, -1)
    return jnp.pad(v, ((0, 0), (0, D_MODEL - v.shape[1])))


def _pack_small(ln_in_g, ln_in_b, sb_g, swa_g, sinks, rel_bias, ln1_g, ln1_b, ln2_g, ln2_b, extra):
    rows = [_pad_row(ln_in_g), _pad_row(ln_in_b), jnp.concatenate([sb_g.reshape(1, -1), swa_g.reshape(1, -1)], axis=1),
            _pad_row(jnp.concatenate([rel_bias.reshape(1, -1), sinks.reshape(1, -1)], axis=1)),
            _pad_row(ln1_g), _pad_row(ln1_b), _pad_row(ln2_g), _pad_row(ln2_b), _pad_row(extra)]
    rows.append(jnp.zeros((SMALL_ROWS - len(rows), D_MODEL), F32))
    return jnp.concatenate(rows, axis=0)


def _unpack_small(blk):
    nrb = REL_BUCKETS * SWA_HEADS
    return (blk[0], blk[1], blk[2:3, :SB_WIDTH], blk[2:3, SB_WIDTH:], blk[3:4, nrb:nrb + SWA_HEADS],
            blk[3, :nrb].reshape(REL_BUCKETS, SWA_HEADS), blk[4:5], blk[5:6], blk[6:7], blk[7:8])


def kernel(x, ln_in_g, ln_in_b, w_in, sb_norm_g, swa_norm_g, sinks, rel_bias, w_out, ln1_g, ln1_b, w_gate_up, w_down, ln2_g, ln2_b, loss_target, m_ln_in_g, m_ln_in_b, m_w_in, m_sb_norm_g, m_swa_norm_g, m_sinks, m_rel_bias, m_w_out, m_ln1_g, m_ln1_b, m_w_gate_up, m_w_down, m_ln2_g, m_ln2_b, v_ln_in_g, v_ln_in_b, v_w_in, v_sb_norm_g, v_swa_norm_g, v_sinks, v_rel_bias, v_w_out, v_ln1_g, v_ln1_b, v_w_gate_up, v_w_down, v_ln2_g, v_ln2_b):
    S = x.shape[1]
    x2 = x.reshape(S, D_MODEL)
    tgt = loss_target.reshape(S, D_MODEL)
    T = min(S, 256)
    bucket = jnp.asarray(_bucket_table())
    row = lambda v: v.reshape(1, -1)

    slab = _cast_rows(_to_slab(w_in[0], w_out[0], w_gate_up[0], w_down[0]), _MXU, "cast_weights")
    w_in_sh, w_out_sh, w_gu_sh, w_down_sh = _from_slab(_gather_weights(slab))
    w_in_f = jnp.concatenate([w_in_sh[j] for j in range(N_CHIPS)], axis=1)
    w_out_f = w_out_sh.reshape(D_MODEL, D_MODEL)
    w_down_f = w_down_sh.reshape(D_FF, D_MODEL)

    h0, h0b, proj = _ln_in_proj(x2, row(ln_in_g), row(ln_in_b), w_in_f)
    o1, o2, o3, o4, o5 = SB_WIDTH, 2 * SB_WIDTH, 3 * SB_WIDTH, 3 * SB_WIDTH + SWA_WIDTH, 3 * SB_WIDTH + SWA_WIDTH + SWA_KV_WIDTH
    q_sb, k_sb, v_sb = proj[:, :o1], proj[:, o1:o2], proj[:, o2:o3]
    q_sw, k_sw, v_sw = proj[:, o3:o4], proj[:, o4:o5], proj[:, o5:]
    qT_sb = _heads_cols(q_sb, SB_HEADS)
    kb_sb = _key_blocks(k_sb, SB_HEADS, T)
    oT_sb, rsave = _sb_fwd(qT_sb, kb_sb, _key_blocks_t(v_sb, SB_HEADS, T))
    sb_out = oT_sb.transpose(2, 0, 1).reshape(S, SB_WIDTH)

    bias = _swa_bias(rel_bias, bucket)
    sink_rows = jnp.broadcast_to(sinks.reshape(SWA_HEADS, 1, 1), (SWA_HEADS, 1, BLOCK))
    qh_sw, kh_sw, vh_sw = _heads_rows(q_sw, SWA_HEADS), _heads_rows(k_sw, SWA_KV_HEADS), _heads_rows(v_sw, SWA_KV_HEADS)
    swa_out = _swa_fwd(qh_sw, kh_sw, vh_sw, bias, sink_rows).transpose(1, 0, 2).reshape(S, SWA_WIDTH)

    pre1, merged = _mix_out(sb_out, swa_out, sb_norm_g, swa_norm_g, w_out_f, h0)
    h1b, gate, up, act = _ffn_up(pre1, ln1_g, ln1_b, w_gu_sh)
    dp2, dp2b, dg2, db2, errsum = _ffn_down_loss(act, w_down_f, pre1, ln1_g, ln1_b, ln2_g, ln2_b, tgt)

    g_w_down = _matmul_tn(act, dp2b, "grad_w_down", FF_CHUNK, D_MODEL)
    dgate, dup = _ffn_down_bwd(dp2b, w_down_f, gate, up)
    g_w_gate = _matmul_tn(h1b, dgate, "grad_w_gate", D_MODEL, FF_CHUNK)
    g_w_up = _matmul_tn(h1b, dup, "grad_w_up", D_MODEL, FF_CHUNK)
    dp1, dp1b, dg1, db1 = _ffn_up_bwd(dgate, dup, w_gu_sh, dp2, pre1, ln1_g)
    g_w_out = _matmul_tn(merged, dp1b, "grad_w_out", D_MODEL, D_MODEL)
    dsb, dsw, dgsb, dgsw = _mix_bwd(dp1b, w_out_f, sb_out, swa_out, sb_norm_g, swa_norm_g)

    dqh_sw, dkh_sw, dvh_sw, dbias, dsink = _swa_bwd(qh_sw, kh_sw, vh_sw, bias, sink_rows, _heads_rows(dsw, SWA_HEADS))
    swa_small = _swa_small_grads(dbias, dsink, bucket)
    dqT_sb, dkT_sb, dvT_sb = _sb_bwd(qT_sb, kb_sb, _key_blocks_t(k_sb, SB_HEADS, T), _key_blocks(v_sb, SB_HEADS, T),
                                     _heads_cols(dsb, SB_HEADS), rsave)
    tok = lambda t, nh: t.reshape(nh, S, HEAD_DIM).transpose(1, 0, 2).reshape(S, nh * HEAD_DIM)
    tokT = lambda t: t.transpose(1, 3, 0, 2).reshape(S, SB_WIDTH)
    dproj = jnp.concatenate([dqT_sb.transpose(2, 0, 1).reshape(S, SB_WIDTH), tokT(dkT_sb), tokT(dvT_sb),
                             tok(dqh_sw, SWA_HEADS), tok(dkh_sw, SWA_KV_HEADS), tok(dvh_sw, SWA_KV_HEADS)],
                            axis=1).astype(_MXU)
    g_w_in = _matmul_tn(h0b, dproj, "grad_w_in", D_MODEL, IN_COLS // 2)
    grad_x, dg_in, db_in = _in_proj_bwd(dproj, w_in_f, dp1, x2, row(ln_in_g))

    cin, cff = IN_COLS // N_CHIPS, D_FF // N_CHIPS
    g_slab = jnp.stack([_to_slab(g_w_in[:, j * cin:(j + 1) * cin], g_w_out[j * (D_MODEL // N_CHIPS):(j + 1) * (D_MODEL // N_CHIPS)],
                                 (g_w_gate if j < 2 else g_w_up)[:, (j % 2) * FF_CHUNK:(j % 2 + 1) * FF_CHUNK],
                                 g_w_down[j * cff:(j + 1) * cff]) for j in range(N_CHIPS)])
    c = lax.axis_index("c")
    own = lax.dynamic_slice_in_dim(g_slab, c * HALF_ROWS, HALF_ROWS, axis=1)
    partial = _pair_sum(own, _swap_halves(g_slab), "pair_sum")
    g_shard = _join_halves(_chip_sum(_scatter_partials(partial), "chip_sum"))
    gs_in, gs_out, gs_gu, gs_down = _from_slab(g_shard)

    nrb = REL_BUCKETS * SWA_HEADS
    small = _pack_small(dg_in, db_in, dgsb, dgsw, swa_small[REL_BUCKETS, :SWA_HEADS],
                        swa_small[:REL_BUCKETS, :SWA_HEADS], dg1, db1, dg2, db2, errsum)
    g_small, loss_tile = _allreduce_small(small)
    loss = loss_tile[0, 0]

    big = []
    for name, w, g, m, v in (("adamw_w_in", w_in, gs_in, m_w_in, v_w_in), ("adamw_w_out", w_out, gs_out, m_w_out, v_w_out),
                             ("adamw_w_gate_up", w_gate_up, gs_gu, m_w_gate_up, v_w_gate_up),
                             ("adamw_w_down", w_down, gs_down, m_w_down, v_w_down)):
        d, nm, nv = _adamw(w[0], g, m[0], v[0], name)
        big.append((g[None], d[None], nm[None], nv[None]))
    zero = jnp.zeros((1,), F32)
    w_small = _pack_small(ln_in_g, ln_in_b, sb_norm_g, swa_norm_g, sinks, rel_bias, ln1_g, ln1_b, ln2_g, ln2_b, zero)
    m_small = _pack_small(m_ln_in_g, m_ln_in_b, m_sb_norm_g, m_swa_norm_g, m_sinks, m_rel_bias, m_ln1_g, m_ln1_b,
                          m_ln2_g, m_ln2_b, zero)
    v_small = _pack_small(v_ln_in_g, v_ln_in_b, v_sb_norm_g, v_swa_norm_g, v_sinks, v_rel_bias, v_ln1_g, v_ln1_b,
                          v_ln2_g, v_ln2_b, zero)
    small_out = [_unpack_small(t) for t in (g_small,) + tuple(_adamw(w_small, g_small, m_small, v_small, "adamw_small"))]

    def kind(k):
        s = small_out[k]
        return [s[0], s[1], big[0][k], s[2], s[3], s[4], s[5], big[1][k], s[6], s[7], big[2][k], big[3][k], s[8], s[9]]

    return (loss, grad_x.reshape(1, S, D_MODEL), *kind(0), *kind(1), *kind(2), *kind(3))
```

```python
import functools
import math

import numpy as np
import jax
import jax.numpy as jnp
from jax import lax
from jax.experimental import pallas as pl
from jax.experimental.pallas import tpu as pltpu

F32 = jnp.float32
_MXU = jnp.bfloat16

D_MODEL = 1024
HEAD_DIM = 64
SB_HEADS = 8
SWA_HEADS = 8
SWA_KV_HEADS = 2
SWA_GROUP = SWA_HEADS // SWA_KV_HEADS
SB_WIDTH = SB_HEADS * HEAD_DIM
SWA_WIDTH = SWA_HEADS * HEAD_DIM
SWA_KV_WIDTH = SWA_KV_HEADS * HEAD_DIM
IN_COLS = 3 * SB_WIDTH + SWA_WIDTH + 2 * SWA_KV_WIDTH
BLOCK = 128
REL_BUCKETS = 32
REL_MAX_DIST = 128
D_FF = 2816
FF_CHUNK = D_FF // 2
ALPHA = 2.0 ** 0.25
LN_EPS = 1e-5
RMS_EPS = 1e-6
SCALE = HEAD_DIM ** -0.5
SB_GROUP_FWD = 8
SB_GROUP_BWD = 4
SB_DEAD = -105.0
SWA_SUB = 8

ADAM_LR = 0.001
ADAM_B1 = 0.9
ADAM_B2 = 0.999
ADAM_EPS = 1e-08
ADAM_WD = 0.01
ADAM_STEP = 10

N_CHIPS = 4
SMALL_ROWS = 16

MESH = pl.DeviceIdType.MESH


def _sds(shape, dtype):
    return jax.ShapeDtypeStruct(shape, dtype)


def _cp(sem=None, vmem_mb=48):
    kw = dict(vmem_limit_bytes=vmem_mb * 1024 * 1024)
    if sem is not None:
        kw["dimension_semantics"] = sem
    return pltpu.CompilerParams(**kw)


def _dot(a, b):
    return jnp.dot(a, b, preferred_element_type=F32)


def _dot_nt(a, b):
    return lax.dot_general(a, b, (((1,), (1,)), ((), ())), preferred_element_type=F32)


def _dot_tn(a, b):
    return lax.dot_general(a, b, (((0,), (0,)), ((), ())), preferred_element_type=F32)


def _ln_hat(x):
    mu = jnp.mean(x, axis=-1, keepdims=True)
    xc = x - mu
    var = jnp.mean(xc * xc, axis=-1, keepdims=True)
    rstd = lax.rsqrt(var + LN_EPS)
    return xc * rstd, rstd


def _ln_bwd(xhat, rstd, dy, g):
    dxh = dy * g
    m1 = jnp.mean(dxh, axis=-1, keepdims=True)
    m2 = jnp.mean(dxh * xhat, axis=-1, keepdims=True)
    return rstd * (dxh - m1 - xhat * m2)


def _colsum(x):
    return jnp.sum(x, axis=0, keepdims=True)


def _split2(x):
    hi = x.astype(_MXU)
    lo = (x - hi.astype(F32)).astype(_MXU)
    return hi, lo


def _rows(tm, n):
    return pl.BlockSpec((tm, n), lambda i: (i, 0))


def _fixed(*shape):
    nd = len(shape)
    return pl.BlockSpec(shape, lambda i: (0,) * nd)


def _ln_in_proj(x, g, b, w):
    S = x.shape[0]
    N = w.shape[1]
    tm = min(S, 512)

    def body(x_ref, g_ref, b_ref, w_ref, h_ref, hb_ref, p_ref):
        xhat, _ = _ln_hat(x_ref[...])
        h = xhat * g_ref[...] + b_ref[...]
        h_ref[...] = h
        hb = h.astype(_MXU)
        hb_ref[...] = hb
        p_ref[...] = _dot(hb, w_ref[...]).astype(p_ref.dtype)

    return pl.pallas_call(
        body, name="ln_in_proj", grid=(S // tm,),
        in_specs=[_rows(tm, D_MODEL), _fixed(1, D_MODEL), _fixed(1, D_MODEL), _fixed(D_MODEL, N)],
        out_specs=[_rows(tm, D_MODEL), _rows(tm, D_MODEL), _rows(tm, N)],
        out_shape=[_sds((S, D_MODEL), F32), _sds((S, D_MODEL), _MXU), _sds((S, N), _MXU)],
        compiler_params=_cp(("parallel",)),
    )(x, g, b, w)


def _rms(x, g):
    r = lax.rsqrt(jnp.mean(x * x, axis=-1, keepdims=True) + RMS_EPS)
    return x * r * g, r


def _mix_out(sb, sw, gsb, gsw, w_out, h0):
    S = sb.shape[0]
    tm = min(S, 512)

    def body(sb_ref, sw_ref, gsb_ref, gsw_ref, w_ref, h0_ref, pre_ref, mg_ref):
        ysb, _ = _rms(sb_ref[...], gsb_ref[...])
        ysw, _ = _rms(sw_ref[...], gsw_ref[...])
        ysb = ysb.astype(_MXU)
        ysw = ysw.astype(_MXU)
        mg_ref[:, :SB_WIDTH] = ysb
        mg_ref[:, SB_WIDTH:] = ysw
        mix = _dot(ysb, w_ref[:SB_WIDTH, :]) + _dot(ysw, w_ref[SB_WIDTH:, :])
        pre_ref[...] = ALPHA * h0_ref[...] + mix

    return pl.pallas_call(
        body, name="mix_out", grid=(S // tm,),
        in_specs=[_rows(tm, SB_WIDTH), _rows(tm, SWA_WIDTH), _fixed(1, SB_WIDTH), _fixed(1, SWA_WIDTH),
                  _fixed(D_MODEL, D_MODEL), _rows(tm, D_MODEL)],
        out_specs=[_rows(tm, D_MODEL), _rows(tm, D_MODEL)],
        out_shape=[_sds((S, D_MODEL), F32), _sds((S, D_MODEL), _MXU)],
        compiler_params=_cp(("parallel",)),
    )(sb, sw, gsb, gsw, w_out, h0)


def _sigmoid(x):
    return 1.0 / (1.0 + jnp.exp(-x))


def _ffn_up(pre1, g1, b1, wgu):
    S = pre1.shape[0]
    tm = min(S, 512)

    def body(p_ref, g_ref, b_ref, wg_ref, wu_ref, h1_ref, gate_ref, up_ref, a_ref):
        xhat, _ = _ln_hat(p_ref[...])
        h1 = (xhat * g_ref[...] + b_ref[...]).astype(_MXU)
        h1_ref[...] = h1
        gate = _dot(h1, wg_ref[0])
        up = _dot(h1, wu_ref[0])
        gate_ref[...] = gate.astype(gate_ref.dtype)
        up_ref[...] = up.astype(up_ref.dtype)
        a_ref[...] = (gate * _sigmoid(gate) * up).astype(a_ref.dtype)

    chunk = lambda i, j: (i, j)
    return pl.pallas_call(
        body, name="ffn_up", grid=(S // tm, 2),
        in_specs=[pl.BlockSpec((tm, D_MODEL), lambda i, j: (i, 0)),
                  pl.BlockSpec((1, D_MODEL), lambda i, j: (0, 0)),
                  pl.BlockSpec((1, D_MODEL), lambda i, j: (0, 0)),
                  pl.BlockSpec((1, D_MODEL, FF_CHUNK), lambda i, j: (j, 0, 0)),
                  pl.BlockSpec((1, D_MODEL, FF_CHUNK), lambda i, j: (j + 2, 0, 0))],
        out_specs=[pl.BlockSpec((tm, D_MODEL), lambda i, j: (i, 0)),
                   pl.BlockSpec((tm, FF_CHUNK), chunk), pl.BlockSpec((tm, FF_CHUNK), chunk),
                   pl.BlockSpec((tm, FF_CHUNK), chunk)],
        out_shape=[_sds((S, D_MODEL), _MXU), _sds((S, D_FF), _MXU), _sds((S, D_FF), _MXU), _sds((S, D_FF), _MXU)],
        compiler_params=_cp(("parallel", "arbitrary")),
    )(pre1, g1, b1, wgu, wgu)


def _ffn_down_loss(a, w_down, pre1, g1, b1, g2, b2, tgt):
    S = a.shape[0]
    tm = min(S, 512)

    def body(a_ref, w_ref, p_ref, g1_ref, b1_ref, g2_ref, b2_ref, t_ref, d_ref, db_ref, dg2_ref, db2_ref, err_ref):
        @pl.when(pl.program_id(0) == 0)
        def _():
            dg2_ref[...] = jnp.zeros_like(dg2_ref)
            db2_ref[...] = jnp.zeros_like(db2_ref)
            err_ref[...] = jnp.zeros_like(err_ref)

        xhat1, _ = _ln_hat(p_ref[...])
        h1 = xhat1 * g1_ref[...] + b1_ref[...]
        pre2 = ALPHA * h1 + _dot(a_ref[...], w_ref[...])
        xhat2, rstd2 = _ln_hat(pre2)
        err = xhat2 * g2_ref[...] + b2_ref[...] - t_ref[...]
        dh2 = err * (1.0 / D_MODEL)
        dp2 = _ln_bwd(xhat2, rstd2, dh2, g2_ref[...])
        d_ref[...] = dp2
        db_ref[...] = dp2.astype(db_ref.dtype)
        dg2_ref[...] += _colsum(dh2 * xhat2)
        db2_ref[...] += _colsum(dh2)
        err_ref[...] += _colsum(err * err)

    vec = _fixed(1, D_MODEL)
    return pl.pallas_call(
        body, name="ffn_down_loss", grid=(S // tm,),
        in_specs=[_rows(tm, D_FF), _fixed(D_FF, D_MODEL), _rows(tm, D_MODEL), vec, vec, vec, vec, _rows(tm, D_MODEL)],
        out_specs=[_rows(tm, D_MODEL), _rows(tm, D_MODEL), vec, vec, vec],
        out_shape=[_sds((S, D_MODEL), F32), _sds((S, D_MODEL), _MXU), _sds((1, D_MODEL), F32), _sds((1, D_MODEL), F32),
                   _sds((1, D_MODEL), F32)],
        compiler_params=_cp(("arbitrary",)),
    )(a, w_down, pre1, g1, b1, g2, b2, tgt)


def _ffn_down_bwd(dp2b, w_down, gate, up):
    S = dp2b.shape[0]
    tm = min(S, 512)

    def body(d_ref, w_ref, g_ref, u_ref, dg_ref, du_ref):
        da = _dot_nt(d_ref[...], w_ref[...])
        g = g_ref[...].astype(F32)
        u = u_ref[...].astype(F32)
        sg = _sigmoid(g)
        du_ref[...] = (da * g * sg).astype(du_ref.dtype)
        dg_ref[...] = (da * u * (sg * (1.0 + g * (1.0 - sg)))).astype(dg_ref.dtype)

    chunk = pl.BlockSpec((tm, FF_CHUNK), lambda i, j: (i, j))
    return pl.pallas_call(
        body, name="ffn_down_bwd", grid=(S // tm, 2),
        in_specs=[pl.BlockSpec((tm, D_MODEL), lambda i, j: (i, 0)),
                  pl.BlockSpec((FF_CHUNK, D_MODEL), lambda i, j: (j, 0)), chunk, chunk],
        out_specs=[chunk, chunk],
        out_shape=[_sds((S, D_FF), _MXU), _sds((S, D_FF), _MXU)],
        compiler_params=_cp(("parallel", "arbitrary")),
    )(dp2b, w_down, gate, up)


def _ffn_up_bwd(dgate, dup, wgu, dp2, pre1, g1):
    S = dgate.shape[0]
    tm = min(S, 256)

    def body(dg_ref, du_ref, w_ref, d2_ref, p_ref, g_ref, d1_ref, d1b_ref, dg1_ref, db1_ref):
        @pl.when(pl.program_id(0) == 0)
        def _():
            dg1_ref[...] = jnp.zeros_like(dg1_ref)
            db1_ref[...] = jnp.zeros_like(db1_ref)

        dh1 = ALPHA * d2_ref[...]
        for j in range(2):
            cols = slice(j * FF_CHUNK, (j + 1) * FF_CHUNK)
            dh1 += _dot_nt(dg_ref[:, cols], w_ref[j])
            dh1 += _dot_nt(du_ref[:, cols], w_ref[j + 2])
        xhat, rstd = _ln_hat(p_ref[...])
        dp1 = _ln_bwd(xhat, rstd, dh1, g_ref[...])
        d1_ref[...] = dp1
        d1b_ref[...] = dp1.astype(d1b_ref.dtype)
        dg1_ref[...] += _colsum(dh1 * xhat)
        db1_ref[...] += _colsum(dh1)

    vec = _fixed(1, D_MODEL)
    return pl.pallas_call(
        body, name="ffn_up_bwd", grid=(S // tm,),
        in_specs=[_rows(tm, D_FF), _rows(tm, D_FF), _fixed(4, D_MODEL, FF_CHUNK), _rows(tm, D_MODEL),
                  _rows(tm, D_MODEL), vec],
        out_specs=[_rows(tm, D_MODEL), _rows(tm, D_MODEL), vec, vec],
        out_shape=[_sds((S, D_MODEL), F32), _sds((S, D_MODEL), _MXU), _sds((1, D_MODEL), F32), _sds((1, D_MODEL), F32)],
        compiler_params=_cp(("arbitrary",), vmem_mb=56),
    )(dgate, dup, wgu, dp2, pre1, g1)


def _rms_bwd(x, g, dy):
    n = x.shape[-1]
    r = lax.rsqrt(jnp.mean(x * x, axis=-1, keepdims=True) + RMS_EPS)
    u = dy * g
    dx = r * u - x * (r * r * r) * (jnp.sum(u * x, axis=-1, keepdims=True) * (1.0 / n))
    return dx, _colsum(dy * x * r)


def _mix_bwd(dp1b, w_out, sb, sw, gsb, gsw):
    S = sb.shape[0]
    tm = min(S, 512)

    def body(d_ref, w_ref, sb_ref, sw_ref, gsb_ref, gsw_ref, dsb_ref, dsw_ref, dgsb_ref, dgsw_ref):
        @pl.when(pl.program_id(0) == 0)
        def _():
            dgsb_ref[...] = jnp.zeros_like(dgsb_ref)
            dgsw_ref[...] = jnp.zeros_like(dgsw_ref)

        dm = _dot_nt(d_ref[...], w_ref[...])
        dsb, dgsb = _rms_bwd(sb_ref[...], gsb_ref[...], dm[:, :SB_WIDTH])
        dsw, dgsw = _rms_bwd(sw_ref[...], gsw_ref[...], dm[:, SB_WIDTH:])
        dsb_ref[...] = dsb.astype(dsb_ref.dtype)
        dsw_ref[...] = dsw.astype(dsw_ref.dtype)
        dgsb_ref[...] += dgsb
        dgsw_ref[...] += dgsw

    return pl.pallas_call(
        body, name="mix_bwd", grid=(S // tm,),
        in_specs=[_rows(tm, D_MODEL), _fixed(D_MODEL, D_MODEL), _rows(tm, SB_WIDTH), _rows(tm, SWA_WIDTH),
                  _fixed(1, SB_WIDTH), _fixed(1, SWA_WIDTH)],
        out_specs=[_rows(tm, SB_WIDTH), _rows(tm, SWA_WIDTH), _fixed(1, SB_WIDTH), _fixed(1, SWA_WIDTH)],
        out_shape=[_sds((S, SB_WIDTH), _MXU), _sds((S, SWA_WIDTH), _MXU), _sds((1, SB_WIDTH), F32),
                   _sds((1, SWA_WIDTH), F32)],
        compiler_params=_cp(("arbitrary",)),
    )(dp1b, w_out, sb, sw, gsb, gsw)


def _in_proj_bwd(dproj, w_in, dp1, x, g):
    S = x.shape[0]
    N = dproj.shape[1]
    tm = min(S, 512)

    def body(dpj_ref, w_ref, d1_ref, x_ref, g_ref, gx_ref, dg_ref, db_ref):
        @pl.when(pl.program_id(0) == 0)
        def _():
            dg_ref[...] = jnp.zeros_like(dg_ref)
            db_ref[...] = jnp.zeros_like(db_ref)

        dh0 = _dot_nt(dpj_ref[...], w_ref[...]) + ALPHA * d1_ref[...]
        xhat, rstd = _ln_hat(x_ref[...])
        gx_ref[...] = _ln_bwd(xhat, rstd, dh0, g_ref[...])
        dg_ref[...] += _colsum(dh0 * xhat)
        db_ref[...] += _colsum(dh0)

    vec = _fixed(1, D_MODEL)
    return pl.pallas_call(
        body, name="in_proj_bwd", grid=(S // tm,),
        in_specs=[_rows(tm, N), _fixed(D_MODEL, N), _rows(tm, D_MODEL), _rows(tm, D_MODEL), vec],
        out_specs=[_rows(tm, D_MODEL), vec, vec],
        out_shape=[_sds((S, D_MODEL), F32), _sds((1, D_MODEL), F32), _sds((1, D_MODEL), F32)],
        compiler_params=_cp(("arbitrary",)),
    )(dproj, w_in, dp1, x, g)


def _matmul_tn(a, b, name, tk, tn):
    T, K = a.shape
    N = b.shape[1]
    tt = min(T, 512)

    def body(a_ref, b_ref, o_ref):
        @pl.when(pl.program_id(2) == 0)
        def _():
            o_ref[...] = jnp.zeros_like(o_ref)

        o_ref[...] += _dot_tn(a_ref[...], b_ref[...])

    return pl.pallas_call(
        body, name=name, grid=(K // tk, N // tn, T // tt),
        in_specs=[pl.BlockSpec((tt, tk), lambda k, n, t: (t, k)), pl.BlockSpec((tt, tn), lambda k, n, t: (t, n))],
        out_specs=pl.BlockSpec((tk, tn), lambda k, n, t: (k, n)),
        out_shape=_sds((K, N), F32),
        compiler_params=_cp(("parallel", "parallel", "arbitrary")),
    )(a, b)


def _matmul_tn_pair(a, b0, b1, name):
    T, K = a.shape
    tt = min(T, 512)

    def body(a_ref, b0_ref, b1_ref, o_ref):
        n = pl.program_id(0)

        @pl.when(pl.program_id(1) == 0)
        def _():
            o_ref[...] = jnp.zeros_like(o_ref)

        @pl.when(n < 2)
        def _():
            o_ref[0] += _dot_tn(a_ref[...], b0_ref[...])

        @pl.when(n >= 2)
        def _():
            o_ref[0] += _dot_tn(a_ref[...], b1_ref[...])

    return pl.pallas_call(
        body, name=name, grid=(4, T // tt),
        in_specs=[pl.BlockSpec((tt, K), lambda n, t: (t, 0)),
                  pl.BlockSpec((tt, FF_CHUNK), lambda n, t: (t, jnp.minimum(n, 1))),
                  pl.BlockSpec((tt, FF_CHUNK), lambda n, t: (t, jnp.maximum(n - 2, 0)))],
        out_specs=pl.BlockSpec((1, K, FF_CHUNK), lambda n, t: (n, 0, 0)),
        out_shape=_sds((4, K, FF_CHUNK), F32),
        compiler_params=_cp(("parallel", "arbitrary")),
    )(a, b0, b1)


def _sb_logs(zt, causal):
    e = jnp.exp(-jnp.abs(zt))
    lb = jnp.minimum(zt, 0.0) - jnp.log(1.0 + e)
    l1m = lb - zt
    if causal is not None:
        l1m = jnp.where(causal, l1m, 0.0)
    return lb, l1m


def _sb_weights(lb, suf, causal):
    a = jnp.exp(lb + suf)
    if causal is not None:
        a = jnp.where(causal, a, 0.0)
    return a


def _tri_masks(t):
    r = lax.broadcasted_iota(jnp.int32, (t, t), 0)
    c = lax.broadcasted_iota(jnp.int32, (t, t), 1)
    return r, c


def _sb_fwd(qT, kb, vTb):
    Hh, _, S = qT.shape
    nk, T = kb.shape[1], kb.shape[2]
    nq = S // T
    G = SB_GROUP_FWD

    def body(qT_ref, k_ref, vT_ref, oT_ref, rs_ref):
        i = pl.program_id(1)
        qts = [(qT_ref[g].astype(F32) * SCALE).astype(_MXU) for g in range(G)]
        r, c = _tri_masks(T)
        upper = (c > r).astype(_MXU)
        causal = r < c

        def blk(j, carry, mask):
            hs = range(G)
            for g in hs:
                rs_ref[g, 0, j] = jnp.broadcast_to(carry[g][0], (8, T))
            zs = [_dot(k_ref[g, j], qts[g]) for g in hs]
            lbs, l1ms = zip(*[_sb_logs(zs[g], mask) for g in hs])
            splits = [_split2(l1ms[g]) for g in hs]
            cums = [_dot(upper, splits[g][0]) + _dot(upper, splits[g][1]) for g in hs]
            avs = [_sb_weights(lbs[g], carry[g][0] + cums[g], mask).astype(_MXU) for g in hs]
            accs = [carry[g][1] + _dot(vT_ref[g, j], avs[g]) for g in hs]
            return tuple((carry[g][0] + _colsum(l1ms[g]), accs[g]) for g in hs)

        def go_on(j, carry):
            top = carry[0][0]
            for g in range(1, G):
                top = jnp.maximum(top, carry[g][0])
            return jnp.logical_and(j >= 0, jnp.max(top) >= SB_DEAD)

        init = tuple((jnp.zeros((1, T), F32), jnp.zeros((HEAD_DIM, T), F32)) for _ in range(G))
        carry = blk(i, init, causal)
        j, carry = lax.while_loop(lambda st: go_on(*st), lambda st: (st[0] - 1, blk(st[0], st[1], None)),
                                  (i - 1, carry))

        @pl.when(j >= 0)
        def _():
            for g in range(G):
                rs_ref[g, 0, j] = jnp.broadcast_to(carry[g][0], (8, T))

        for g in range(G):
            oT_ref[g] = carry[g][1]

    return pl.pallas_call(
        body, name="sb_fwd", grid=(Hh // G, nq),
        in_specs=[pl.BlockSpec((G, HEAD_DIM, T), lambda h, i: (h, 0, i)),
                  pl.BlockSpec((G, nk, T, HEAD_DIM), lambda h, i: (h, 0, 0, 0), pipeline_mode=pl.Buffered(1)),
                  pl.BlockSpec((G, nk, HEAD_DIM, T), lambda h, i: (h, 0, 0, 0), pipeline_mode=pl.Buffered(1))],
        out_specs=[pl.BlockSpec((G, HEAD_DIM, T), lambda h, i: (h, 0, i)),
                   pl.BlockSpec((G, 1, nk, 8, T), lambda h, i: (h, i, 0, 0, 0))],
        out_shape=[_sds((Hh, HEAD_DIM, S), F32), _sds((Hh, nq, nk, 8, T), F32)],
        compiler_params=_cp(("parallel", "arbitrary")),
    )(qT, kb, vTb)


def _sb_bwd(qT, kb, kTb, vb, doT, rsave):
    Hh, _, S = qT.shape
    nk, T = kb.shape[1], kb.shape[2]
    nq = S // T
    G = SB_GROUP_BWD

    def body(qT_ref, k_ref, kT_ref, v_ref, doT_ref, rs_ref, dqT_ref, dk_ref, dv_ref):
        i = pl.program_id(1)

        @pl.when(i == 0)
        def _():
            dk_ref[...] = jnp.zeros_like(dk_ref)
            dv_ref[...] = jnp.zeros_like(dv_ref)

        qts = [(qT_ref[g].astype(F32) * SCALE).astype(_MXU) for g in range(G)]
        douts = [doT_ref[g] for g in range(G)]
        r, c = _tri_masks(T)
        upper = (c > r).astype(_MXU)
        lower = (c < r).astype(_MXU)
        causal = r < c

        def blk(j, carry, mask):
            hs = range(G)
            zs = [_dot(k_ref[g, j], qts[g]) for g in hs]
            das = [_dot(v_ref[g, j], douts[g]) for g in hs]
            lbs, l1ms = zip(*[_sb_logs(zs[g], mask) for g in hs])
            splits = [_split2(l1ms[g]) for g in hs]
            cums = [_dot(upper, splits[g][0]) + _dot(upper, splits[g][1]) for g in hs]
            avs = [_sb_weights(lbs[g], rs_ref[g, 0, j][0:1, :] + cums[g], mask) for g in hs]
            ets = [das[g] * avs[g] for g in hs]
            esplits = [_split2(ets[g]) for g in hs]
            ecums = [_dot(lower, esplits[g][0]) + _dot(lower, esplits[g][1]) for g in hs]
            dzs = []
            for g in hs:
                sig = jnp.exp(lbs[g])
                dz = ets[g] * (1.0 - sig) - (carry[g][0] + ecums[g]) * sig
                if mask is not None:
                    dz = jnp.where(mask, dz, 0.0)
                dzs.append(dz.astype(_MXU))
            dqs = [carry[g][1] + _dot(kT_ref[g, j], dzs[g]) for g in hs]
            for g in hs:
                dk_ref[g, j] += _dot_nt(qts[g], dzs[g])
            for g in hs:
                dv_ref[g, j] += _dot_nt(douts[g], avs[g].astype(_MXU))
            return tuple((carry[g][0] + _colsum(ets[g]), dqs[g]) for g in hs)

        def live(j):
            jj = jnp.maximum(j, 0)
            top = rs_ref[0, 0, jj][0:1, :]
            for g in range(1, G):
                top = jnp.maximum(top, rs_ref[g, 0, jj][0:1, :])
            return jnp.logical_and(j >= 0, jnp.max(top) >= SB_DEAD)

        first = lax.while_loop(lambda st: st[1], lambda st: (st[0] - 1, live(st[0] - 2)), (i, live(i - 1)))[0]
        carry = tuple((jnp.zeros((1, T), F32), jnp.zeros((HEAD_DIM, T), F32)) for _ in range(G))
        carry = lax.fori_loop(first, i, lambda s, cr: blk(s, cr, None), carry)
        carry = blk(i, carry, causal)
        for g in range(G):
            dqT_ref[g] = carry[g][1] * SCALE

    colblk = pl.BlockSpec((G, HEAD_DIM, T), lambda h, i: (h, 0, i))
    once = pl.Buffered(1)
    kblk = pl.BlockSpec((G, nk, T, HEAD_DIM), lambda h, i: (h, 0, 0, 0), pipeline_mode=once)
    kTblk = pl.BlockSpec((G, nk, HEAD_DIM, T), lambda h, i: (h, 0, 0, 0), pipeline_mode=once)
    return pl.pallas_call(
        body, name="sb_bwd", grid=(Hh // G, nq),
        in_specs=[colblk, kblk, kTblk, kblk, colblk,
                  pl.BlockSpec((G, 1, nk, 8, T), lambda h, i: (h, i, 0, 0, 0))],
        out_specs=[colblk, kTblk, kTblk],
        out_shape=[_sds((Hh, HEAD_DIM, S), F32), _sds((Hh, nk, HEAD_DIM, T), F32), _sds((Hh, nk, HEAD_DIM, T), F32)],
        compiler_params=_cp(("parallel", "arbitrary"), vmem_mb=56),
    )(qT, kb, kTb, vb, doT, rsave)


def _bucket_table():
    qi = np.arange(BLOCK)[:, None]
    cj = np.arange(2 * BLOCK)[None, :]
    dist = qi + BLOCK - cj
    exact = REL_BUCKETS // 2
    d = np.maximum(dist, 0)
    d_f = np.maximum(d, 1).astype(np.float32)
    large = exact + (np.log(d_f / np.float32(exact)) / np.float32(math.log(REL_MAX_DIST / exact))
                     * np.float32(REL_BUCKETS - exact)).astype(np.int32)
    large = np.minimum(large, REL_BUCKETS - 1)
    return np.where(d < exact, d, large).astype(np.int32)


def _swa_bias(rel_bias, bucket):
    def body(rb_ref, bk_ref, o_ref):
        bk = bk_ref[...]
        for h in range(SWA_HEADS):
            t = jnp.zeros((BLOCK, 2 * BLOCK), F32)
            for b in range(REL_BUCKETS):
                t = jnp.where(bk == b, rb_ref[b, h], t)
            o_ref[h] = t

    return pl.pallas_call(
        body, name="swa_bias",
        in_specs=[pl.BlockSpec(memory_space=pltpu.SMEM), pl.BlockSpec(memory_space=pltpu.VMEM)],
        out_specs=pl.BlockSpec(memory_space=pltpu.VMEM),
        out_shape=_sds((SWA_HEADS, BLOCK, 2 * BLOCK), F32),
    )(rel_bias, bucket)


def _swa_logits(q, kp, kc):
    qs = (q.astype(F32) * SCALE).astype(_MXU)
    return qs, _dot_nt(qs, kp), _dot_nt(qs, kc)


def _swa_softmax(lp, lc, bias, sink, live_prev):
    r, c = _tri_masks(BLOCK)
    in_window = c > r if live_prev is None else jnp.logical_and(c > r, live_prev)
    lp = jnp.where(in_window, lp + bias[:, :BLOCK], -jnp.inf)
    lc = jnp.where(c <= r, lc + bias[:, BLOCK:], -jnp.inf)
    m = jnp.maximum(jnp.maximum(jnp.max(lp, axis=1, keepdims=True), jnp.max(lc, axis=1, keepdims=True)), sink)
    pp = jnp.exp(lp - m)
    pc = jnp.exp(lc - m)
    ps = jnp.exp(sink - m)
    denom = jnp.sum(pp, axis=1, keepdims=True) + jnp.sum(pc, axis=1, keepdims=True) + ps
    return pp / denom, pc / denom, ps / denom


def _swa_sub(nb):
    return min(SWA_SUB, nb)


def _swa_keys(b, prev_ref, cur_ref, i):
    cur = cur_ref[0, b * BLOCK:(b + 1) * BLOCK, :]
    if b == 0:
        return prev_ref[0], cur, i > 0
    return cur_ref[0, (b - 1) * BLOCK:b * BLOCK, :], cur, None


def _swa_fwd(q, k, v, bias, sink):
    S = q.shape[1]
    nb = S // BLOCK
    ns = _swa_sub(nb)
    R = ns * BLOCK

    def body(q_ref, kp_ref, kc_ref, vp_ref, vc_ref, bias_ref, sink_ref, o_ref):
        i = pl.program_id(1)
        bias = bias_ref[0]
        sink = sink_ref[0][:, :1]
        subs = range(ns)
        rows = [slice(b * BLOCK, (b + 1) * BLOCK) for b in subs]
        keys = [_swa_keys(b, kp_ref, kc_ref, i) for b in subs]
        vals = [_swa_keys(b, vp_ref, vc_ref, i) for b in subs]
        logits = [_swa_logits(q_ref[0, rows[b], :], keys[b][0], keys[b][1]) for b in subs]
        ws = [_swa_softmax(logits[b][1], logits[b][2], bias, sink, keys[b][2]) for b in subs]
        for b in subs:
            o_ref[0, rows[b], :] = _dot(ws[b][0].astype(_MXU), vals[b][0]) + _dot(ws[b][1].astype(_MXU), vals[b][1])

    prev = pl.BlockSpec((1, BLOCK, HEAD_DIM), lambda h, i: (h // SWA_GROUP, jnp.maximum(i * ns - 1, 0), 0))
    cur = pl.BlockSpec((1, R, HEAD_DIM), lambda h, i: (h // SWA_GROUP, i, 0))
    qblk = pl.BlockSpec((1, R, HEAD_DIM), lambda h, i: (h, i, 0))
    return pl.pallas_call(
        body, name="swa_fwd", grid=(SWA_HEADS, nb // ns),
        in_specs=[qblk, prev, cur, prev, cur,
                  pl.BlockSpec((1, BLOCK, 2 * BLOCK), lambda h, i: (h, 0, 0)),
                  pl.BlockSpec((1, 1, BLOCK), lambda h, i: (h, 0, 0))],
        out_specs=qblk,
        out_shape=_sds((SWA_HEADS, S, HEAD_DIM), F32),
        compiler_params=_cp(("parallel", "parallel")),
    )(q, k, k, v, v, bias, sink)


def _swa_bwd(q, k, v, bias, sink, do):
    S = q.shape[1]
    nb = S // BLOCK
    ns = _swa_sub(nb)
    R = ns * BLOCK

    def body(q_ref, kp_ref, kc_ref, vp_ref, vc_ref, bias_ref, sink_ref, do_ref, dq_ref, dk_ref, dv_ref, dbias_ref,
             dsink_ref):
        g = pl.program_id(1)
        i = pl.program_id(2)

        @pl.when(jnp.logical_and(g == 0, i == 0))
        def _():
            dk_ref[...] = jnp.zeros_like(dk_ref)
            dv_ref[...] = jnp.zeros_like(dv_ref)

        @pl.when(i == 0)
        def _():
            dbias_ref[...] = jnp.zeros_like(dbias_ref)
            dsink_ref[...] = jnp.zeros_like(dsink_ref)

        bias = bias_ref[0]
        sink = sink_ref[0][:, :1]
        subs = range(ns)
        rows = [slice(b * BLOCK, (b + 1) * BLOCK) for b in subs]
        keys = [_swa_keys(b, kp_ref, kc_ref, i) for b in subs]
        vals = [_swa_keys(b, vp_ref, vc_ref, i) for b in subs]
        douts = [do_ref[0, rows[b], :] for b in subs]
        logits = [_swa_logits(q_ref[0, rows[b], :], keys[b][0], keys[b][1]) for b in subs]
        dws = [(_dot_nt(douts[b], vals[b][0]), _dot_nt(douts[b], vals[b][1])) for b in subs]
        dbp = jnp.zeros((BLOCK, BLOCK), F32)
        dbc = jnp.zeros((BLOCK, BLOCK), F32)
        dsk = jnp.zeros((BLOCK, 1), F32)
        wts, dls = [], []
        for b in subs:
            wp, wc, ws = _swa_softmax(logits[b][1], logits[b][2], bias, sink, keys[b][2])
            dwp, dwc = dws[b]
            delta = jnp.sum(wp * dwp, axis=1, keepdims=True) + jnp.sum(wc * dwc, axis=1, keepdims=True)
            dlp = wp * (dwp - delta)
            dlc = wc * (dwc - delta)
            dbp += dlp
            dbc += dlc
            dsk -= ws * delta
            wts.append((wp.astype(_MXU), wc.astype(_MXU)))
            dls.append((dlp.astype(_MXU), dlc.astype(_MXU)))
        for b in subs:
            dq_ref[0, rows[b], :] = (_dot(dls[b][0], keys[b][0]) + _dot(dls[b][1], keys[b][1])) * SCALE
        for b in subs:
            qs = logits[b][0]
            blk = i * ns + b
            dk_ref[0, blk] += _dot_tn(dls[b][1], qs)
            dv_ref[0, blk] += _dot_tn(wts[b][1], douts[b])
            if b == 0:
                @pl.when(i > 0)
                def _():
                    dk_ref[0, blk - 1] += _dot_tn(dls[0][0], qs)
                    dv_ref[0, blk - 1] += _dot_tn(wts[0][0], douts[0])
            else:
                dk_ref[0, blk - 1] += _dot_tn(dls[b][0], qs)
                dv_ref[0, blk - 1] += _dot_tn(wts[b][0], douts[b])
        dbias_ref[0, :, :BLOCK] += dbp
        dbias_ref[0, :, BLOCK:] += dbc
        dsink_ref[0] += jnp.broadcast_to(dsk, (BLOCK, BLOCK))

    hq = lambda kv, g, i: kv * SWA_GROUP + g
    prev = pl.BlockSpec((1, BLOCK, HEAD_DIM), lambda kv, g, i: (kv, jnp.maximum(i * ns - 1, 0), 0))
    cur = pl.BlockSpec((1, R, HEAD_DIM), lambda kv, g, i: (kv, i, 0))
    qblk = pl.BlockSpec((1, R, HEAD_DIM), lambda kv, g, i: (hq(kv, g, i), i, 0))
    kvacc = pl.BlockSpec((1, nb, BLOCK, HEAD_DIM), lambda kv, g, i: (kv, 0, 0, 0))
    return pl.pallas_call(
        body, name="swa_bwd", grid=(SWA_KV_HEADS, SWA_GROUP, nb // ns),
        in_specs=[qblk, prev, cur, prev, cur,
                  pl.BlockSpec((1, BLOCK, 2 * BLOCK), lambda kv, g, i: (hq(kv, g, i), 0, 0)),
                  pl.BlockSpec((1, 1, BLOCK), lambda kv, g, i: (hq(kv, g, i), 0, 0)), qblk],
        out_specs=[qblk, kvacc, kvacc,
                   pl.BlockSpec((1, BLOCK, 2 * BLOCK), lambda kv, g, i: (hq(kv, g, i), 0, 0)),
                   pl.BlockSpec((1, BLOCK, BLOCK), lambda kv, g, i: (hq(kv, g, i), 0, 0))],
        out_shape=[_sds((SWA_HEADS, S, HEAD_DIM), F32), _sds((SWA_KV_HEADS, nb, BLOCK, HEAD_DIM), F32),
                   _sds((SWA_KV_HEADS, nb, BLOCK, HEAD_DIM), F32), _sds((SWA_HEADS, BLOCK, 2 * BLOCK), F32),
                   _sds((SWA_HEADS, BLOCK, BLOCK), F32)],
        compiler_params=_cp(("arbitrary", "arbitrary", "arbitrary")),
    )(q, k, k, v, v, bias, sink, do)


def _swa_small_grads(dbias, dsink, bucket):
    rows = REL_BUCKETS + 8

    def total(x):
        return jnp.sum(jnp.sum(x, axis=1, keepdims=True), axis=0, keepdims=True)

    def body(db_ref, ds_ref, bk_ref, o_ref):
        bk = bk_ref[...]
        r = lax.broadcasted_iota(jnp.int32, (rows, BLOCK), 0)
        c = lax.broadcasted_iota(jnp.int32, (rows, BLOCK), 1)
        out = jnp.zeros((rows, BLOCK), F32)
        for h in range(SWA_HEADS):
            db = db_ref[h]
            for b in range(REL_BUCKETS):
                s = total(jnp.where(bk == b, db, 0.0))
                out = jnp.where(jnp.logical_and(r == b, c == h), s, out)
            s = jnp.sum(ds_ref[h][:, :1], axis=0, keepdims=True)
            out = jnp.where(jnp.logical_and(r == REL_BUCKETS, c == h), s, out)
        o_ref[...] = out

    vm = pl.BlockSpec(memory_space=pltpu.VMEM)
    return pl.pallas_call(body, name="swa_small_grads", in_specs=[vm, vm, vm], out_specs=vm,
                          out_shape=_sds((rows, BLOCK), F32))(dbias, dsink, bucket)


def _tile_rows(n):
    for t in (512, 352, 256, 176, 128, 64, 32, 16, 8):
        if n % t == 0:
            return t
    return n


def _cast_rows(x, dtype, name):
    R, C = x.shape
    tr = _tile_rows(R)

    def body(x_ref, o_ref):
        o_ref[...] = x_ref[...].astype(o_ref.dtype)

    return pl.pallas_call(body, name=name, grid=(R // tr,), in_specs=[_rows(tr, C)], out_specs=_rows(tr, C),
                          out_shape=_sds((R, C), dtype), compiler_params=_cp(("parallel",)))(x)


def _pair_sum(g, recv, c, name):
    n, half, C = recv.shape
    tr = _tile_rows(half)

    def body(c_ref, a_ref, b_ref, o_ref):
        o_ref[...] = (a_ref[0] + b_ref[...]).astype(o_ref.dtype)

    return pl.pallas_call(
        body, name=name,
        grid_spec=pltpu.PrefetchScalarGridSpec(
            num_scalar_prefetch=1, grid=(n, half // tr),
            in_specs=[pl.BlockSpec((1, 1, tr, C), lambda j, i, c_ref: (j, c_ref[0], i, 0)),
                      pl.BlockSpec((1, tr, C), lambda j, i, c_ref: (j, i, 0))],
            out_specs=pl.BlockSpec((1, tr, C), lambda j, i, c_ref: (j, i, 0))),
        out_shape=_sds((n, half, C), _MXU),
        compiler_params=_cp(("parallel", "parallel")))(c.reshape(1), g.reshape(n, 2, half, C), recv)


def _chip_sum(parts, name):
    n, R, C = parts.shape
    tr = _tile_rows(R)

    def body(p_ref, o_ref):
        acc = p_ref[0].astype(F32)
        for j in range(1, n):
            acc = acc + p_ref[j].astype(F32)
        o_ref[...] = acc

    return pl.pallas_call(body, name=name, grid=(R // tr,),
                          in_specs=[pl.BlockSpec((n, tr, C), lambda i: (0, i, 0))], out_specs=_rows(tr, C),
                          out_shape=_sds((R, C), F32), compiler_params=_cp(("parallel",)))(parts)


def _adamw_math(w, g, m, v):
    m = ADAM_B1 * m + (1.0 - ADAM_B1) * g
    v = ADAM_B2 * v + (1.0 - ADAM_B2) * (g * g)
    m_hat = m / (1.0 - ADAM_B1 ** ADAM_STEP)
    v_hat = v / (1.0 - ADAM_B2 ** ADAM_STEP)
    delta = -ADAM_LR * (m_hat / (jnp.sqrt(v_hat) + ADAM_EPS) + ADAM_WD * w)
    return delta, m, v


def _adamw(w, g, m, v, name):
    R, C = w.shape
    tr = _tile_rows(R)

    def body(w_ref, g_ref, m_ref, v_ref, d_ref, nm_ref, nv_ref):
        d, nm, nv = _adamw_math(w_ref[...], g_ref[...], m_ref[...], v_ref[...])
        d_ref[...] = d
        nm_ref[...] = nm
        nv_ref[...] = nv

    blk = _rows(tr, C)
    return pl.pallas_call(body, name=name, grid=(R // tr,), in_specs=[blk] * 4, out_specs=[blk] * 3,
                          out_shape=[_sds((R, C), F32)] * 3, compiler_params=_cp(("parallel",)))(w, g, m, v)


def _place():
    x, y, c = lax.axis_index("x"), lax.axis_index("y"), lax.axis_index("c")
    chips = [(1 - x, y), (x, 1 - y), (1 - x, 1 - y)]
    return x, y, c, chips


def _gather_weights(shards):
    nw = len(shards)

    def body(*refs):
        in_refs, out_refs = refs[:nw], refs[nw:2 * nw]
        send_sems, recv_sems, local_sems = refs[2 * nw:]
        x, y, c, chips = _place()
        me, sibling = (x, y, c), (x, y, 1 - c)

        def copy(w, k, chip, hc, to, src=None):
            part = out_refs[w].at[2 * chip[0] + chip[1], hc]
            return pltpu.make_async_remote_copy(
                src_ref=part if src is None else src, dst_ref=part, send_sem=send_sems.at[w, k],
                recv_sem=recv_sems.at[w, k], device_id=to, device_id_type=MESH)

        ws = range(nw)
        mines = [pltpu.make_async_copy(in_refs[w], out_refs[w].at[2 * x + y], local_sems.at[w]) for w in ws]
        for cp in mines:
            cp.start()
        first = [copy(w, j, (x, y), c, (*chip, c), src=in_refs[w].at[c]) for w in ws for j, chip in enumerate(chips)]
        for cp in first:
            cp.start()
        passed = []
        for w in ws:
            for j, chip in enumerate(chips):
                copy(w, j, chip, c, me).wait_recv()
                passed.append(copy(w, 3 + j, chip, c, sibling))
                passed[-1].start()
        for w in ws:
            for j, chip in enumerate(chips):
                copy(w, 3 + j, chip, 1 - c, me).wait_recv()
        for cp in first + passed:
            cp.wait_send()
        for cp in mines:
            cp.wait()

    any_spec = pl.BlockSpec(memory_space=pl.ANY)
    halves = [(s.shape[0] // 2, s.shape[1]) for s in shards]
    outs = pl.pallas_call(
        body, name="gather_weights", in_specs=[any_spec] * nw, out_specs=[any_spec] * nw,
        out_shape=[_sds((N_CHIPS, 2, h, cols), s.dtype) for s, (h, cols) in zip(shards, halves)],
        scratch_shapes=[pltpu.SemaphoreType.DMA((nw, 6)), pltpu.SemaphoreType.DMA((nw, 6)),
                        pltpu.SemaphoreType.DMA((nw,))],
    )(*[s.reshape(2, h, cols) for s, (h, cols) in zip(shards, halves)])
    return [o.reshape(N_CHIPS, 2 * h, cols) for o, (h, cols) in zip(outs, halves)]


def _swap_halves(grads):
    nw = len(grads)

    def body(*refs):
        g_refs, out_refs = refs[:nw], refs[nw:2 * nw]
        send_sems, recv_sems = refs[2 * nw:]
        x, y, c, _ = _place()
        cps = []
        for w in range(nw):
            half = out_refs[w].shape[1]
            theirs = g_refs[w].at[:, pl.ds(pl.multiple_of((1 - c) * half, 8), half), :]
            cps.append(pltpu.make_async_remote_copy(
                src_ref=theirs, dst_ref=out_refs[w], send_sem=send_sems.at[w], recv_sem=recv_sems.at[w],
                device_id=(x, y, 1 - c), device_id_type=MESH))
        for cp in cps:
            cp.start()
        for cp in cps:
            cp.wait()

    any_spec = pl.BlockSpec(memory_space=pl.ANY)
    return pl.pallas_call(
        body, name="swap_halves", in_specs=[any_spec] * nw, out_specs=[any_spec] * nw,
        out_shape=[_sds((g.shape[0], g.shape[1] // 2, g.shape[2]), g.dtype) for g in grads],
        scratch_shapes=[pltpu.SemaphoreType.DMA((nw,)), pltpu.SemaphoreType.DMA((nw,))],
    )(*grads)


def _scatter_partials(parts):
    nw = len(parts)

    def body(*refs):
        p_refs, out_refs = refs[:nw], refs[nw:2 * nw]
        send_sems, recv_sems, local_sems = refs[2 * nw:]
        x, y, c, chips = _place()
        me = 2 * x + y
        ws = range(nw)
        mines = [pltpu.make_async_copy(p_refs[w].at[me], out_refs[w].at[me], local_sems.at[w]) for w in ws]
        for cp in mines:
            cp.start()

        def copy(w, j, chip, src_chip, dst_chip):
            return pltpu.make_async_remote_copy(
                src_ref=p_refs[w].at[src_chip], dst_ref=out_refs[w].at[dst_chip], send_sem=send_sems.at[w, j],
                recv_sem=recv_sems.at[w, j], device_id=(*chip, c), device_id_type=MESH)

        sends = [copy(w, j, chip, 2 * chip[0] + chip[1], me) for w in ws for j, chip in enumerate(chips)]
        for cp in sends:
            cp.start()
        for w in ws:
            for j, chip in enumerate(chips):
                copy(w, j, chip, me, 2 * chip[0] + chip[1]).wait_recv()
        for cp in sends:
            cp.wait_send()
        for cp in mines:
            cp.wait()

    any_spec = pl.BlockSpec(memory_space=pl.ANY)
    return pl.pallas_call(
        body, name="scatter_partials", in_specs=[any_spec] * nw, out_specs=[any_spec] * nw,
        out_shape=[_sds(p.shape, p.dtype) for p in parts],
        scratch_shapes=[pltpu.SemaphoreType.DMA((nw, 3)), pltpu.SemaphoreType.DMA((nw, 3)),
                        pltpu.SemaphoreType.DMA((nw,))],
    )(*parts)


def _join_halves(sums):
    nw = len(sums)

    def body(*refs):
        f_refs, out_refs = refs[:nw], refs[nw:2 * nw]
        send_sems, recv_sems, local_sems = refs[2 * nw:]
        x, y, c, _ = _place()
        ws = range(nw)
        mines = [pltpu.make_async_copy(f_refs[w], out_refs[w].at[c], local_sems.at[w]) for w in ws]
        for cp in mines:
            cp.start()

        def copy(w, half_index):
            return pltpu.make_async_remote_copy(
                src_ref=f_refs[w], dst_ref=out_refs[w].at[half_index], send_sem=send_sems.at[w],
                recv_sem=recv_sems.at[w], device_id=(x, y, 1 - c), device_id_type=MESH)

        sends = [copy(w, c) for w in ws]
        for cp in sends:
            cp.start()
        for w in ws:
            copy(w, 1 - c).wait_recv()
        for cp in sends:
            cp.wait_send()
        for cp in mines:
            cp.wait()

    any_spec = pl.BlockSpec(memory_space=pl.ANY)
    outs = pl.pallas_call(
        body, name="join_halves", in_specs=[any_spec] * nw, out_specs=[any_spec] * nw,
        out_shape=[_sds((2,) + f.shape, f.dtype) for f in sums],
        scratch_shapes=[pltpu.SemaphoreType.DMA((nw,)), pltpu.SemaphoreType.DMA((nw,)), pltpu.SemaphoreType.DMA((nw,))],
    )(*sums)
    return [o.reshape(2 * f.shape[0], f.shape[1]) for o, f in zip(outs, sums)]


def _allreduce_small(block):
    m_per, n = block.shape

    def body(x_ref, sum_ref, loss_ref, all_ref, send_sems, recv_sems, local_sem):
        x, y, c, chips = _place()
        me, sibling = (x, y, c), (x, y, 1 - c)

        def rows(px, py, pc):
            return all_ref.at[pl.ds(pl.multiple_of((4 * px + 2 * py + pc) * m_per, 8), m_per), :]

        def copy(k, blk, to, src=None):
            return pltpu.make_async_remote_copy(
                src_ref=rows(*blk) if src is None else src, dst_ref=rows(*blk), send_sem=send_sems.at[k],
                recv_sem=recv_sems.at[k], device_id=to, device_id_type=MESH)

        mine = pltpu.make_async_copy(x_ref, rows(*me), local_sem)
        mine.start()
        first = [copy(0, me, sibling, src=x_ref)]
        first += [copy(1 + j, me, (*chip, c), src=x_ref) for j, chip in enumerate(chips)]
        for cp in first:
            cp.start()
        passed = [copy(4 + j, (*chip, c), sibling) for j, chip in enumerate(chips)]
        for j, chip in enumerate(chips):
            copy(1 + j, (*chip, c), me).wait_recv()
            passed[j].start()
        copy(0, sibling, me).wait_recv()
        for j, chip in enumerate(chips):
            copy(4 + j, (*chip, 1 - c), me).wait_recv()
        for cp in first + passed:
            cp.wait_send()
        mine.wait()

        acc = all_ref[0:m_per, :]
        for d in range(1, 8):
            acc = acc + all_ref[d * m_per:(d + 1) * m_per, :]
        sum_ref[...] = acc
        tot = jnp.sum(acc[8:9, :], axis=1, keepdims=True) * (0.5 / D_MODEL)
        loss_ref[...] = jnp.broadcast_to(tot, loss_ref.shape)

    vm = pl.BlockSpec(memory_space=pltpu.VMEM)
    return pl.pallas_call(
        body, name="allreduce_small", in_specs=[vm], out_specs=[vm, vm],
        out_shape=[_sds((m_per, n), F32), _sds((8, 128), F32)],
        scratch_shapes=[pltpu.VMEM((8 * m_per, n), F32), pltpu.SemaphoreType.DMA((7,)), pltpu.SemaphoreType.DMA((7,)),
                        pltpu.SemaphoreType.DMA],
    )(block)


def _heads_rows(x, nh):
    S = x.shape[0]
    return x.reshape(S, nh, HEAD_DIM).transpose(1, 0, 2)


def _heads_cols(x, nh):
    S = x.shape[0]
    return x.reshape(S, nh, HEAD_DIM).transpose(1, 2, 0)


def _key_blocks(x, nh, t):
    S = x.shape[0]
    return x.reshape(S // t, t, nh, HEAD_DIM).transpose(2, 0, 1, 3)


def _key_blocks_t(x, nh, t):
    S = x.shape[0]
    return x.reshape(S // t, t, nh, HEAD_DIM).transpose(2, 0, 3, 1)


def _pad_row(v):
    v = v.reshape(1, -1)
    return jnp.pad(v, ((0, 0), (0, D_MODEL - v.shape[1])))


def _pack_small(ln_in_g, ln_in_b, sb_g, swa_g, sinks, rel_bias, ln1_g, ln1_b, ln2_g, ln2_b, extra):
    rows = [_pad_row(ln_in_g), _pad_row(ln_in_b), jnp.concatenate([sb_g.reshape(1, -1), swa_g.reshape(1, -1)], axis=1),
            _pad_row(jnp.concatenate([rel_bias.reshape(1, -1), sinks.reshape(1, -1)], axis=1)),
            _pad_row(ln1_g), _pad_row(ln1_b), _pad_row(ln2_g), _pad_row(ln2_b), _pad_row(extra)]
    rows.append(jnp.zeros((SMALL_ROWS - len(rows), D_MODEL), F32))
    return jnp.concatenate(rows, axis=0)


def _unpack_small(blk):
    nrb = REL_BUCKETS * SWA_HEADS
    return (blk[0], blk[1], blk[2:3, :SB_WIDTH], blk[2:3, SB_WIDTH:], blk[3:4, nrb:nrb + SWA_HEADS],
            blk[3, :nrb].reshape(REL_BUCKETS, SWA_HEADS), blk[4:5], blk[5:6], blk[6:7], blk[7:8])


def kernel(x, ln_in_g, ln_in_b, w_in, sb_norm_g, swa_norm_g, sinks, rel_bias, w_out, ln1_g, ln1_b, w_gate_up, w_down, ln2_g, ln2_b, loss_target, m_ln_in_g, m_ln_in_b, m_w_in, m_sb_norm_g, m_swa_norm_g, m_sinks, m_rel_bias, m_w_out, m_ln1_g, m_ln1_b, m_w_gate_up, m_w_down, m_ln2_g, m_ln2_b, v_ln_in_g, v_ln_in_b, v_w_in, v_sb_norm_g, v_swa_norm_g, v_sinks, v_rel_bias, v_w_out, v_ln1_g, v_ln1_b, v_w_gate_up, v_w_down, v_ln2_g, v_ln2_b):
    S = x.shape[1]
    x2 = x.reshape(S, D_MODEL)
    tgt = loss_target.reshape(S, D_MODEL)
    T = min(S, 256)
    bucket = jnp.asarray(_bucket_table())
    row = lambda v: v.reshape(1, -1)

    shards = [_cast_rows(w[0], _MXU, "cast_" + n) for n, w in (("w_in", w_in), ("w_out", w_out), ("w_gate_up", w_gate_up), ("w_down", w_down))]
    w_in_sh, w_out_sh, w_gu_sh, w_down_sh = _gather_weights(shards)
    w_in_f = jnp.concatenate([w_in_sh[j] for j in range(N_CHIPS)], axis=1)
    w_out_f = w_out_sh.reshape(D_MODEL, D_MODEL)
    w_down_f = w_down_sh.reshape(D_FF, D_MODEL)

    h0, h0b, proj = _ln_in_proj(x2, row(ln_in_g), row(ln_in_b), w_in_f)
    o1, o2, o3, o4, o5 = SB_WIDTH, 2 * SB_WIDTH, 3 * SB_WIDTH, 3 * SB_WIDTH + SWA_WIDTH, 3 * SB_WIDTH + SWA_WIDTH + SWA_KV_WIDTH
    q_sb, k_sb, v_sb = proj[:, :o1], proj[:, o1:o2], proj[:, o2:o3]
    q_sw, k_sw, v_sw = proj[:, o3:o4], proj[:, o4:o5], proj[:, o5:]
    qT_sb = _heads_cols(q_sb, SB_HEADS)
    kb_sb = _key_blocks(k_sb, SB_HEADS, T)
    oT_sb, rsave = _sb_fwd(qT_sb, kb_sb, _key_blocks_t(v_sb, SB_HEADS, T))
    sb_out = oT_sb.transpose(2, 0, 1).reshape(S, SB_WIDTH)

    bias = _swa_bias(rel_bias, bucket)
    sink_rows = jnp.broadcast_to(sinks.reshape(SWA_HEADS, 1, 1), (SWA_HEADS, 1, BLOCK))
    qh_sw, kh_sw, vh_sw = _heads_rows(q_sw, SWA_HEADS), _heads_rows(k_sw, SWA_KV_HEADS), _heads_rows(v_sw, SWA_KV_HEADS)
    swa_out = _swa_fwd(qh_sw, kh_sw, vh_sw, bias, sink_rows).transpose(1, 0, 2).reshape(S, SWA_WIDTH)

    pre1, merged = _mix_out(sb_out, swa_out, sb_norm_g, swa_norm_g, w_out_f, h0)
    h1b, gate, up, act = _ffn_up(pre1, ln1_g, ln1_b, w_gu_sh)
    dp2, dp2b, dg2, db2, errsum = _ffn_down_loss(act, w_down_f, pre1, ln1_g, ln1_b, ln2_g, ln2_b, tgt)

    g_w_down = _matmul_tn(act, dp2b, "grad_w_down", FF_CHUNK, D_MODEL)
    dgate, dup = _ffn_down_bwd(dp2b, w_down_f, gate, up)
    g_w_gu = _matmul_tn_pair(h1b, dgate, dup, "grad_w_gate_up")
    dp1, dp1b, dg1, db1 = _ffn_up_bwd(dgate, dup, w_gu_sh, dp2, pre1, ln1_g)
    g_w_out = _matmul_tn(merged, dp1b, "grad_w_out", D_MODEL, D_MODEL)
    dsb, dsw, dgsb, dgsw = _mix_bwd(dp1b, w_out_f, sb_out, swa_out, sb_norm_g, swa_norm_g)

    dqh_sw, dkh_sw, dvh_sw, dbias, dsink = _swa_bwd(qh_sw, kh_sw, vh_sw, bias, sink_rows, _heads_rows(dsw, SWA_HEADS))
    swa_small = _swa_small_grads(dbias, dsink, bucket)
    dqT_sb, dkT_sb, dvT_sb = _sb_bwd(qT_sb, kb_sb, _key_blocks_t(k_sb, SB_HEADS, T), _key_blocks(v_sb, SB_HEADS, T),
                                     _heads_cols(dsb, SB_HEADS), rsave)
    tok = lambda t, nh: t.reshape(nh, S, HEAD_DIM).transpose(1, 0, 2).reshape(S, nh * HEAD_DIM)
    tokT = lambda t: t.transpose(1, 3, 0, 2).reshape(S, SB_WIDTH)
    dproj = jnp.concatenate([dqT_sb.transpose(2, 0, 1).reshape(S, SB_WIDTH), tokT(dkT_sb), tokT(dvT_sb),
                             tok(dqh_sw, SWA_HEADS), tok(dkh_sw, SWA_KV_HEADS), tok(dvh_sw, SWA_KV_HEADS)],
                            axis=1).astype(_MXU)
    g_w_in = _matmul_tn(h0b, dproj, "grad_w_in", D_MODEL, IN_COLS // 2)
    grad_x, dg_in, db_in = _in_proj_bwd(dproj, w_in_f, dp1, x2, row(ln_in_g))

    cin = IN_COLS // N_CHIPS
    grads = [jnp.stack([g_w_in[:, j * cin:(j + 1) * cin] for j in range(N_CHIPS)]),
             g_w_out.reshape(N_CHIPS, D_MODEL // N_CHIPS, D_MODEL), g_w_gu,
             g_w_down.reshape(N_CHIPS, D_FF // N_CHIPS, D_MODEL)]
    names = ("w_in", "w_out", "w_gate_up", "w_down")
    c = lax.axis_index("c").astype(jnp.int32)
    partials = [_pair_sum(g, r, c, "pair_sum_" + n) for g, r, n in zip(grads, _swap_halves(grads), names)]
    sums = [_chip_sum(p, "chip_sum_" + n) for p, n in zip(_scatter_partials(partials), names)]
    gs_in, gs_out, gs_gu, gs_down = _join_halves(sums)

    nrb = REL_BUCKETS * SWA_HEADS
    small = _pack_small(dg_in, db_in, dgsb, dgsw, swa_small[REL_BUCKETS, :SWA_HEADS],
                        swa_small[:REL_BUCKETS, :SWA_HEADS], dg1, db1, dg2, db2, errsum)
    g_small, loss_tile = _allreduce_small(small)
    loss = loss_tile[0, 0]

    big = []
    for name, w, g, m, v in (("adamw_w_in", w_in, gs_in, m_w_in, v_w_in), ("adamw_w_out", w_out, gs_out, m_w_out, v_w_out),
                             ("adamw_w_gate_up", w_gate_up, gs_gu, m_w_gate_up, v_w_gate_up),
                             ("adamw_w_down", w_down, gs_down, m_w_down, v_w_down)):
        d, nm, nv = _adamw(w[0], g, m[0], v[0], name)
        big.append((g[None], d[None], nm[None], nv[None]))
    zero = jnp.zeros((1,), F32)
    w_small = _pack_small(ln_in_g, ln_in_b, sb_norm_g, swa_norm_g, sinks, rel_bias, ln1_g, ln1_b, ln2_g, ln2_b, zero)
    m_small = _pack_small(m_ln_in_g, m_ln_in_b, m_sb_norm_g, m_swa_norm_g, m_sinks, m_rel_bias, m_ln1_g, m_ln1_b,
                          m_ln2_g, m_ln2_b, zero)
    v_small = _pack_small(v_ln_in_g, v_ln_in_b, v_sb_norm_g, v_swa_norm_g, v_sinks, v_rel_bias, v_ln1_g, v_ln1_b,
                          v_ln2_g, v_ln2_b, zero)
    small_out = [_unpack_small(t) for t in (g_small,) + tuple(_adamw(w_small, g_small, m_small, v_small, "adamw_small"))]

    def kind(k):
        s = small_out[k]
        return [s[0], s[1], big[0][k], s[2], s[3], s[4], s[5], big[1][k], s[6], s[7], big[2][k], big[3][k], s[8], s[9]]

    return (loss, grad_x.reshape(1, S, D_MODEL), *kind(0), *kind(1), *kind(2), *kind(3))
```

```python
import functools
import math

import numpy as np
import jax
import jax.numpy as jnp
from jax import lax
from jax.experimental import pallas as pl
from jax.experimental.pallas import tpu as pltpu

F32 = jnp.float32
_MXU = jnp.bfloat16

D_MODEL = 1024
HEAD_DIM = 64
SB_HEADS = 8
SWA_HEADS = 8
SWA_KV_HEADS = 2
SWA_GROUP = SWA_HEADS // SWA_KV_HEADS
SB_WIDTH = SB_HEADS * HEAD_DIM
SWA_WIDTH = SWA_HEADS * HEAD_DIM
SWA_KV_WIDTH = SWA_KV_HEADS * HEAD_DIM
IN_COLS = 3 * SB_WIDTH + SWA_WIDTH + 2 * SWA_KV_WIDTH
BLOCK = 128
REL_BUCKETS = 32
REL_MAX_DIST = 128
D_FF = 2816
FF_CHUNK = D_FF // 2
ALPHA = 2.0 ** 0.25
LN_EPS = 1e-5
RMS_EPS = 1e-6
SCALE = HEAD_DIM ** -0.5
SB_GROUP_FWD = 8
SB_GROUP_BWD = 4
SB_DEAD = -105.0
SWA_SUB = 8

ADAM_LR = 0.001
ADAM_B1 = 0.9
ADAM_B2 = 0.999
ADAM_EPS = 1e-08
ADAM_WD = 0.01
ADAM_STEP = 10

N_CHIPS = 4
SMALL_ROWS = 16

MESH = pl.DeviceIdType.MESH


def _sds(shape, dtype):
    return jax.ShapeDtypeStruct(shape, dtype)


def _cp(sem=None, vmem_mb=48):
    kw = dict(vmem_limit_bytes=vmem_mb * 1024 * 1024)
    if sem is not None:
        kw["dimension_semantics"] = sem
    return pltpu.CompilerParams(**kw)


def _dot(a, b):
    return jnp.dot(a, b, preferred_element_type=F32)


def _dot_nt(a, b):
    return lax.dot_general(a, b, (((1,), (1,)), ((), ())), preferred_element_type=F32)


def _dot_tn(a, b):
    return lax.dot_general(a, b, (((0,), (0,)), ((), ())), preferred_element_type=F32)


def _ln_hat(x):
    mu = jnp.mean(x, axis=-1, keepdims=True)
    xc = x - mu
    var = jnp.mean(xc * xc, axis=-1, keepdims=True)
    rstd = lax.rsqrt(var + LN_EPS)
    return xc * rstd, rstd


def _ln_bwd(xhat, rstd, dy, g):
    dxh = dy * g
    m1 = jnp.mean(dxh, axis=-1, keepdims=True)
    m2 = jnp.mean(dxh * xhat, axis=-1, keepdims=True)
    return rstd * (dxh - m1 - xhat * m2)


def _colsum(x):
    return jnp.sum(x, axis=0, keepdims=True)


def _split2(x):
    hi = x.astype(_MXU)
    lo = (x - hi.astype(F32)).astype(_MXU)
    return hi, lo


def _rows(tm, n):
    return pl.BlockSpec((tm, n), lambda i: (i, 0))


def _fixed(*shape):
    nd = len(shape)
    return pl.BlockSpec(shape, lambda i: (0,) * nd)


def _ln_in_proj(x, g, b, w):
    S = x.shape[0]
    N = w.shape[1]
    tm = min(S, 512)

    def body(x_ref, g_ref, b_ref, w_ref, h_ref, hb_ref, p_ref):
        xhat, _ = _ln_hat(x_ref[...])
        h = xhat * g_ref[...] + b_ref[...]
        h_ref[...] = h
        hb = h.astype(_MXU)
        hb_ref[...] = hb
        p_ref[...] = _dot(hb, w_ref[...]).astype(p_ref.dtype)

    return pl.pallas_call(
        body, name="ln_in_proj", grid=(S // tm,),
        in_specs=[_rows(tm, D_MODEL), _fixed(1, D_MODEL), _fixed(1, D_MODEL), _fixed(D_MODEL, N)],
        out_specs=[_rows(tm, D_MODEL), _rows(tm, D_MODEL), _rows(tm, N)],
        out_shape=[_sds((S, D_MODEL), F32), _sds((S, D_MODEL), _MXU), _sds((S, N), _MXU)],
        compiler_params=_cp(("parallel",)),
    )(x, g, b, w)


def _rms(x, g):
    r = lax.rsqrt(jnp.mean(x * x, axis=-1, keepdims=True) + RMS_EPS)
    return x * r * g, r


def _mix_out(sb, sw, gsb, gsw, w_out, h0):
    S = sb.shape[0]
    tm = min(S, 512)

    def body(sb_ref, sw_ref, gsb_ref, gsw_ref, w_ref, h0_ref, pre_ref, mg_ref):
        ysb, _ = _rms(sb_ref[...], gsb_ref[...])
        ysw, _ = _rms(sw_ref[...], gsw_ref[...])
        ysb = ysb.astype(_MXU)
        ysw = ysw.astype(_MXU)
        mg_ref[:, :SB_WIDTH] = ysb
        mg_ref[:, SB_WIDTH:] = ysw
        mix = _dot(ysb, w_ref[:SB_WIDTH, :]) + _dot(ysw, w_ref[SB_WIDTH:, :])
        pre_ref[...] = ALPHA * h0_ref[...] + mix

    return pl.pallas_call(
        body, name="mix_out", grid=(S // tm,),
        in_specs=[_rows(tm, SB_WIDTH), _rows(tm, SWA_WIDTH), _fixed(1, SB_WIDTH), _fixed(1, SWA_WIDTH),
                  _fixed(D_MODEL, D_MODEL), _rows(tm, D_MODEL)],
        out_specs=[_rows(tm, D_MODEL), _rows(tm, D_MODEL)],
        out_shape=[_sds((S, D_MODEL), F32), _sds((S, D_MODEL), _MXU)],
        compiler_params=_cp(("parallel",)),
    )(sb, sw, gsb, gsw, w_out, h0)


def _sigmoid(x):
    return 1.0 / (1.0 + jnp.exp(-x))


def _ffn_up(pre1, g1, b1, wgu):
    S = pre1.shape[0]
    tm = min(S, 512)

    def body(p_ref, g_ref, b_ref, wg_ref, wu_ref, h1_ref, gate_ref, up_ref, a_ref):
        xhat, _ = _ln_hat(p_ref[...])
        h1 = (xhat * g_ref[...] + b_ref[...]).astype(_MXU)
        h1_ref[...] = h1
        gate = _dot(h1, wg_ref[0])
        up = _dot(h1, wu_ref[0])
        gate_ref[...] = gate.astype(gate_ref.dtype)
        up_ref[...] = up.astype(up_ref.dtype)
        a_ref[...] = (gate * _sigmoid(gate) * up).astype(a_ref.dtype)

    chunk = lambda i, j: (i, j)
    return pl.pallas_call(
        body, name="ffn_up", grid=(S // tm, 2),
        in_specs=[pl.BlockSpec((tm, D_MODEL), lambda i, j: (i, 0)),
                  pl.BlockSpec((1, D_MODEL), lambda i, j: (0, 0)),
                  pl.BlockSpec((1, D_MODEL), lambda i, j: (0, 0)),
                  pl.BlockSpec((1, D_MODEL, FF_CHUNK), lambda i, j: (j, 0, 0)),
                  pl.BlockSpec((1, D_MODEL, FF_CHUNK), lambda i, j: (j + 2, 0, 0))],
        out_specs=[pl.BlockSpec((tm, D_MODEL), lambda i, j: (i, 0)),
                   pl.BlockSpec((tm, FF_CHUNK), chunk), pl.BlockSpec((tm, FF_CHUNK), chunk),
                   pl.BlockSpec((tm, FF_CHUNK), chunk)],
        out_shape=[_sds((S, D_MODEL), _MXU), _sds((S, D_FF), _MXU), _sds((S, D_FF), _MXU), _sds((S, D_FF), _MXU)],
        compiler_params=_cp(("parallel", "arbitrary")),
    )(pre1, g1, b1, wgu, wgu)


def _ffn_down_loss(a, w_down, pre1, g1, b1, g2, b2, tgt):
    S = a.shape[0]
    tm = min(S, 512)

    def body(a_ref, w_ref, p_ref, g1_ref, b1_ref, g2_ref, b2_ref, t_ref, d_ref, db_ref, dg2_ref, db2_ref, err_ref):
        @pl.when(pl.program_id(0) == 0)
        def _():
            dg2_ref[...] = jnp.zeros_like(dg2_ref)
            db2_ref[...] = jnp.zeros_like(db2_ref)
            err_ref[...] = jnp.zeros_like(err_ref)

        xhat1, _ = _ln_hat(p_ref[...])
        h1 = xhat1 * g1_ref[...] + b1_ref[...]
        pre2 = ALPHA * h1 + _dot(a_ref[...], w_ref[...])
        xhat2, rstd2 = _ln_hat(pre2)
        err = xhat2 * g2_ref[...] + b2_ref[...] - t_ref[...]
        dh2 = err * (1.0 / D_MODEL)
        dp2 = _ln_bwd(xhat2, rstd2, dh2, g2_ref[...])
        d_ref[...] = dp2
        db_ref[...] = dp2.astype(db_ref.dtype)
        dg2_ref[...] += _colsum(dh2 * xhat2)
        db2_ref[...] += _colsum(dh2)
        err_ref[...] += _colsum(err * err)

    vec = _fixed(1, D_MODEL)
    return pl.pallas_call(
        body, name="ffn_down_loss", grid=(S // tm,),
        in_specs=[_rows(tm, D_FF), _fixed(D_FF, D_MODEL), _rows(tm, D_MODEL), vec, vec, vec, vec, _rows(tm, D_MODEL)],
        out_specs=[_rows(tm, D_MODEL), _rows(tm, D_MODEL), vec, vec, vec],
        out_shape=[_sds((S, D_MODEL), F32), _sds((S, D_MODEL), _MXU), _sds((1, D_MODEL), F32), _sds((1, D_MODEL), F32),
                   _sds((1, D_MODEL), F32)],
        compiler_params=_cp(("arbitrary",)),
    )(a, w_down, pre1, g1, b1, g2, b2, tgt)


def _ffn_down_bwd(dp2b, w_down, gate, up):
    S = dp2b.shape[0]
    tm = min(S, 512)

    def body(d_ref, w_ref, g_ref, u_ref, dg_ref, du_ref):
        da = _dot_nt(d_ref[...], w_ref[...])
        g = g_ref[...].astype(F32)
        u = u_ref[...].astype(F32)
        sg = _sigmoid(g)
        du_ref[...] = (da * g * sg).astype(du_ref.dtype)
        dg_ref[...] = (da * u * (sg * (1.0 + g * (1.0 - sg)))).astype(dg_ref.dtype)

    chunk = pl.BlockSpec((tm, FF_CHUNK), lambda i, j: (i, j))
    return pl.pallas_call(
        body, name="ffn_down_bwd", grid=(S // tm, 2),
        in_specs=[pl.BlockSpec((tm, D_MODEL), lambda i, j: (i, 0)),
                  pl.BlockSpec((FF_CHUNK, D_MODEL), lambda i, j: (j, 0)), chunk, chunk],
        out_specs=[chunk, chunk],
        out_shape=[_sds((S, D_FF), _MXU), _sds((S, D_FF), _MXU)],
        compiler_params=_cp(("parallel", "arbitrary")),
    )(dp2b, w_down, gate, up)


def _ffn_up_bwd(dgate, dup, wgu, dp2, pre1, g1):
    S = dgate.shape[0]
    tm = min(S, 256)

    def body(dg_ref, du_ref, w_ref, d2_ref, p_ref, g_ref, d1_ref, d1b_ref, dg1_ref, db1_ref):
        @pl.when(pl.program_id(0) == 0)
        def _():
            dg1_ref[...] = jnp.zeros_like(dg1_ref)
            db1_ref[...] = jnp.zeros_like(db1_ref)

        dh1 = ALPHA * d2_ref[...]
        for j in range(2):
            cols = slice(j * FF_CHUNK, (j + 1) * FF_CHUNK)
            dh1 += _dot_nt(dg_ref[:, cols], w_ref[j])
            dh1 += _dot_nt(du_ref[:, cols], w_ref[j + 2])
        xhat, rstd = _ln_hat(p_ref[...])
        dp1 = _ln_bwd(xhat, rstd, dh1, g_ref[...])
        d1_ref[...] = dp1
        d1b_ref[...] = dp1.astype(d1b_ref.dtype)
        dg1_ref[...] += _colsum(dh1 * xhat)
        db1_ref[...] += _colsum(dh1)

    vec = _fixed(1, D_MODEL)
    return pl.pallas_call(
        body, name="ffn_up_bwd", grid=(S // tm,),
        in_specs=[_rows(tm, D_FF), _rows(tm, D_FF), _fixed(4, D_MODEL, FF_CHUNK), _rows(tm, D_MODEL),
                  _rows(tm, D_MODEL), vec],
        out_specs=[_rows(tm, D_MODEL), _rows(tm, D_MODEL), vec, vec],
        out_shape=[_sds((S, D_MODEL), F32), _sds((S, D_MODEL), _MXU), _sds((1, D_MODEL), F32), _sds((1, D_MODEL), F32)],
        compiler_params=_cp(("arbitrary",), vmem_mb=56),
    )(dgate, dup, wgu, dp2, pre1, g1)


def _rms_bwd(x, g, dy):
    n = x.shape[-1]
    r = lax.rsqrt(jnp.mean(x * x, axis=-1, keepdims=True) + RMS_EPS)
    u = dy * g
    dx = r * u - x * (r * r * r) * (jnp.sum(u * x, axis=-1, keepdims=True) * (1.0 / n))
    return dx, _colsum(dy * x * r)


def _mix_bwd(dp1b, w_out, sb, sw, gsb, gsw):
    S = sb.shape[0]
    tm = min(S, 512)

    def body(d_ref, w_ref, sb_ref, sw_ref, gsb_ref, gsw_ref, dsb_ref, dsw_ref, dgsb_ref, dgsw_ref):
        @pl.when(pl.program_id(0) == 0)
        def _():
            dgsb_ref[...] = jnp.zeros_like(dgsb_ref)
            dgsw_ref[...] = jnp.zeros_like(dgsw_ref)

        dm = _dot_nt(d_ref[...], w_ref[...])
        dsb, dgsb = _rms_bwd(sb_ref[...], gsb_ref[...], dm[:, :SB_WIDTH])
        dsw, dgsw = _rms_bwd(sw_ref[...], gsw_ref[...], dm[:, SB_WIDTH:])
        dsb_ref[...] = dsb.astype(dsb_ref.dtype)
        dsw_ref[...] = dsw.astype(dsw_ref.dtype)
        dgsb_ref[...] += dgsb
        dgsw_ref[...] += dgsw

    return pl.pallas_call(
        body, name="mix_bwd", grid=(S // tm,),
        in_specs=[_rows(tm, D_MODEL), _fixed(D_MODEL, D_MODEL), _rows(tm, SB_WIDTH), _rows(tm, SWA_WIDTH),
                  _fixed(1, SB_WIDTH), _fixed(1, SWA_WIDTH)],
        out_specs=[_rows(tm, SB_WIDTH), _rows(tm, SWA_WIDTH), _fixed(1, SB_WIDTH), _fixed(1, SWA_WIDTH)],
        out_shape=[_sds((S, SB_WIDTH), _MXU), _sds((S, SWA_WIDTH), _MXU), _sds((1, SB_WIDTH), F32),
                   _sds((1, SWA_WIDTH), F32)],
        compiler_params=_cp(("arbitrary",)),
    )(dp1b, w_out, sb, sw, gsb, gsw)


def _in_proj_bwd(dproj, w_in, dp1, x, g):
    S = x.shape[0]
    N = dproj.shape[1]
    tm = min(S, 512)

    def body(dpj_ref, w_ref, d1_ref, x_ref, g_ref, gx_ref, dg_ref, db_ref):
        @pl.when(pl.program_id(0) == 0)
        def _():
            dg_ref[...] = jnp.zeros_like(dg_ref)
            db_ref[...] = jnp.zeros_like(db_ref)

        dh0 = _dot_nt(dpj_ref[...], w_ref[...]) + ALPHA * d1_ref[...]
        xhat, rstd = _ln_hat(x_ref[...])
        gx_ref[...] = _ln_bwd(xhat, rstd, dh0, g_ref[...])
        dg_ref[...] += _colsum(dh0 * xhat)
        db_ref[...] += _colsum(dh0)

    vec = _fixed(1, D_MODEL)
    return pl.pallas_call(
        body, name="in_proj_bwd", grid=(S // tm,),
        in_specs=[_rows(tm, N), _fixed(D_MODEL, N), _rows(tm, D_MODEL), _rows(tm, D_MODEL), vec],
        out_specs=[_rows(tm, D_MODEL), vec, vec],
        out_shape=[_sds((S, D_MODEL), F32), _sds((1, D_MODEL), F32), _sds((1, D_MODEL), F32)],
        compiler_params=_cp(("arbitrary",)),
    )(dproj, w_in, dp1, x, g)


def _matmul_tn(a, b, name, tk, tn):
    T, K = a.shape
    N = b.shape[1]
    tt = min(T, 512)

    def body(a_ref, b_ref, o_ref):
        @pl.when(pl.program_id(2) == 0)
        def _():
            o_ref[...] = jnp.zeros_like(o_ref)

        o_ref[...] += _dot_tn(a_ref[...], b_ref[...])

    return pl.pallas_call(
        body, name=name, grid=(K // tk, N // tn, T // tt),
        in_specs=[pl.BlockSpec((tt, tk), lambda k, n, t: (t, k)), pl.BlockSpec((tt, tn), lambda k, n, t: (t, n))],
        out_specs=pl.BlockSpec((tk, tn), lambda k, n, t: (k, n)),
        out_shape=_sds((K, N), F32),
        compiler_params=_cp(("parallel", "parallel", "arbitrary")),
    )(a, b)


def _matmul_tn_pair(a, b0, b1, name):
    T, K = a.shape
    tt = min(T, 512)

    def body(a_ref, b0_ref, b1_ref, o_ref):
        n = pl.program_id(0)

        @pl.when(pl.program_id(1) == 0)
        def _():
            o_ref[...] = jnp.zeros_like(o_ref)

        @pl.when(n < 2)
        def _():
            o_ref[0] += _dot_tn(a_ref[...], b0_ref[...])

        @pl.when(n >= 2)
        def _():
            o_ref[0] += _dot_tn(a_ref[...], b1_ref[...])

    return pl.pallas_call(
        body, name=name, grid=(4, T // tt),
        in_specs=[pl.BlockSpec((tt, K), lambda n, t: (t, 0)),
                  pl.BlockSpec((tt, FF_CHUNK), lambda n, t: (t, jnp.minimum(n, 1))),
                  pl.BlockSpec((tt, FF_CHUNK), lambda n, t: (t, jnp.maximum(n - 2, 0)))],
        out_specs=pl.BlockSpec((1, K, FF_CHUNK), lambda n, t: (n, 0, 0)),
        out_shape=_sds((4, K, FF_CHUNK), F32),
        compiler_params=_cp(("parallel", "arbitrary")),
    )(a, b0, b1)


def _sb_logs(zt, causal):
    e = jnp.exp(-jnp.abs(zt))
    lb = jnp.minimum(zt, 0.0) - jnp.log(1.0 + e)
    l1m = lb - zt
    if causal is not None:
        l1m = jnp.where(causal, l1m, 0.0)
    return lb, l1m


def _sb_weights(lb, suf, causal):
    a = jnp.exp(lb + suf)
    if causal is not None:
        a = jnp.where(causal, a, 0.0)
    return a


def _tri_masks(t):
    r = lax.broadcasted_iota(jnp.int32, (t, t), 0)
    c = lax.broadcasted_iota(jnp.int32, (t, t), 1)
    return r, c


def _sb_fwd(qT, kb, vTb):
    Hh, _, S = qT.shape
    nk, T = kb.shape[1], kb.shape[2]
    nq = S // T
    G = SB_GROUP_FWD

    def body(qT_ref, k_ref, vT_ref, oT_ref, rs_ref):
        i = pl.program_id(1)
        qts = [(qT_ref[g].astype(F32) * SCALE).astype(_MXU) for g in range(G)]
        r, c = _tri_masks(T)
        upper = (c > r).astype(_MXU)
        causal = r < c

        def blk(j, carry, mask):
            hs = range(G)
            for g in hs:
                rs_ref[g, 0, j] = jnp.broadcast_to(carry[g][0], (8, T))
            zs = [_dot(k_ref[g, j], qts[g]) for g in hs]
            lbs, l1ms = zip(*[_sb_logs(zs[g], mask) for g in hs])
            splits = [_split2(l1ms[g]) for g in hs]
            cums = [_dot(upper, splits[g][0]) + _dot(upper, splits[g][1]) for g in hs]
            avs = [_sb_weights(lbs[g], carry[g][0] + cums[g], mask).astype(_MXU) for g in hs]
            accs = [carry[g][1] + _dot(vT_ref[g, j], avs[g]) for g in hs]
            return tuple((carry[g][0] + _colsum(l1ms[g]), accs[g]) for g in hs)

        def go_on(j, carry):
            top = carry[0][0]
            for g in range(1, G):
                top = jnp.maximum(top, carry[g][0])
            return jnp.logical_and(j >= 0, jnp.max(top) >= SB_DEAD)

        init = tuple((jnp.zeros((1, T), F32), jnp.zeros((HEAD_DIM, T), F32)) for _ in range(G))
        carry = blk(i, init, causal)
        j, carry = lax.while_loop(lambda st: go_on(*st), lambda st: (st[0] - 1, blk(st[0], st[1], None)),
                                  (i - 1, carry))

        @pl.when(j >= 0)
        def _():
            for g in range(G):
                rs_ref[g, 0, j] = jnp.broadcast_to(carry[g][0], (8, T))

        for g in range(G):
            oT_ref[g] = carry[g][1]

    return pl.pallas_call(
        body, name="sb_fwd", grid=(Hh // G, nq),
        in_specs=[pl.BlockSpec((G, HEAD_DIM, T), lambda h, i: (h, 0, i)),
                  pl.BlockSpec((G, nk, T, HEAD_DIM), lambda h, i: (h, 0, 0, 0), pipeline_mode=pl.Buffered(1)),
                  pl.BlockSpec((G, nk, HEAD_DIM, T), lambda h, i: (h, 0, 0, 0), pipeline_mode=pl.Buffered(1))],
        out_specs=[pl.BlockSpec((G, HEAD_DIM, T), lambda h, i: (h, 0, i)),
                   pl.BlockSpec((G, 1, nk, 8, T), lambda h, i: (h, i, 0, 0, 0))],
        out_shape=[_sds((Hh, HEAD_DIM, S), F32), _sds((Hh, nq, nk, 8, T), F32)],
        compiler_params=_cp(("parallel", "arbitrary")),
    )(qT, kb, vTb)


def _sb_bwd(qT, kb, kTb, vb, doT, rsave):
    Hh, _, S = qT.shape
    nk, T = kb.shape[1], kb.shape[2]
    nq = S // T
    G = SB_GROUP_BWD

    def body(qT_ref, k_ref, kT_ref, v_ref, doT_ref, rs_ref, dqT_ref, dk_ref, dv_ref):
        i = pl.program_id(1)

        @pl.when(i == 0)
        def _():
            dk_ref[...] = jnp.zeros_like(dk_ref)
            dv_ref[...] = jnp.zeros_like(dv_ref)

        qts = [(qT_ref[g].astype(F32) * SCALE).astype(_MXU) for g in range(G)]
        douts = [doT_ref[g] for g in range(G)]
        r, c = _tri_masks(T)
        upper = (c > r).astype(_MXU)
        lower = (c < r).astype(_MXU)
        causal = r < c

        def blk(j, carry, mask):
            hs = range(G)
            zs = [_dot(k_ref[g, j], qts[g]) for g in hs]
            das = [_dot(v_ref[g, j], douts[g]) for g in hs]
            lbs, l1ms = zip(*[_sb_logs(zs[g], mask) for g in hs])
            splits = [_split2(l1ms[g]) for g in hs]
            cums = [_dot(upper, splits[g][0]) + _dot(upper, splits[g][1]) for g in hs]
            avs = [_sb_weights(lbs[g], rs_ref[g, 0, j][0:1, :] + cums[g], mask) for g in hs]
            ets = [das[g] * avs[g] for g in hs]
            esplits = [_split2(ets[g]) for g in hs]
            ecums = [_dot(lower, esplits[g][0]) + _dot(lower, esplits[g][1]) for g in hs]
            dzs = []
            for g in hs:
                sig = jnp.exp(lbs[g])
                dz = ets[g] * (1.0 - sig) - (carry[g][0] + ecums[g]) * sig
                if mask is not None:
                    dz = jnp.where(mask, dz, 0.0)
                dzs.append(dz.astype(_MXU))
            dqs = [carry[g][1] + _dot(kT_ref[g, j], dzs[g]) for g in hs]
            for g in hs:
                dk_ref[g, j] += _dot_nt(qts[g], dzs[g])
            for g in hs:
                dv_ref[g, j] += _dot_nt(douts[g], avs[g].astype(_MXU))
            return tuple((carry[g][0] + _colsum(ets[g]), dqs[g]) for g in hs)

        def live(j):
            jj = jnp.maximum(j, 0)
            top = rs_ref[0, 0, jj][0:1, :]
            for g in range(1, G):
                top = jnp.maximum(top, rs_ref[g, 0, jj][0:1, :])
            return jnp.logical_and(j >= 0, jnp.max(top) >= SB_DEAD)

        first = lax.while_loop(lambda st: st[1], lambda st: (st[0] - 1, live(st[0] - 2)), (i, live(i - 1)))[0]
        carry = tuple((jnp.zeros((1, T), F32), jnp.zeros((HEAD_DIM, T), F32)) for _ in range(G))
        carry = lax.fori_loop(first, i, lambda s, cr: blk(s, cr, None), carry)
        carry = blk(i, carry, causal)
        for g in range(G):
            dqT_ref[g] = carry[g][1] * SCALE

    colblk = pl.BlockSpec((G, HEAD_DIM, T), lambda h, i: (h, 0, i))
    once = pl.Buffered(1)
    kblk = pl.BlockSpec((G, nk, T, HEAD_DIM), lambda h, i: (h, 0, 0, 0), pipeline_mode=once)
    kTblk = pl.BlockSpec((G, nk, HEAD_DIM, T), lambda h, i: (h, 0, 0, 0), pipeline_mode=once)
    return pl.pallas_call(
        body, name="sb_bwd", grid=(Hh // G, nq),
        in_specs=[colblk, kblk, kTblk, kblk, colblk,
                  pl.BlockSpec((G, 1, nk, 8, T), lambda h, i: (h, i, 0, 0, 0))],
        out_specs=[colblk, kTblk, kTblk],
        out_shape=[_sds((Hh, HEAD_DIM, S), F32), _sds((Hh, nk, HEAD_DIM, T), F32), _sds((Hh, nk, HEAD_DIM, T), F32)],
        compiler_params=_cp(("parallel", "arbitrary"), vmem_mb=56),
    )(qT, kb, kTb, vb, doT, rsave)


def _bucket_table():
    qi = np.arange(BLOCK)[:, None]
    cj = np.arange(2 * BLOCK)[None, :]
    dist = qi + BLOCK - cj
    exact = REL_BUCKETS // 2
    d = np.maximum(dist, 0)
    d_f = np.maximum(d, 1).astype(np.float32)
    large = exact + (np.log(d_f / np.float32(exact)) / np.float32(math.log(REL_MAX_DIST / exact))
                     * np.float32(REL_BUCKETS - exact)).astype(np.int32)
    large = np.minimum(large, REL_BUCKETS - 1)
    return np.where(d < exact, d, large).astype(np.int32)


def _swa_bias(rel_bias, bucket):
    def body(rb_ref, bk_ref, o_ref):
        bk = bk_ref[...]
        for h in range(SWA_HEADS):
            t = jnp.zeros((BLOCK, 2 * BLOCK), F32)
            for b in range(REL_BUCKETS):
                t = jnp.where(bk == b, rb_ref[b, h], t)
            o_ref[h] = t

    return pl.pallas_call(
        body, name="swa_bias",
        in_specs=[pl.BlockSpec(memory_space=pltpu.SMEM), pl.BlockSpec(memory_space=pltpu.VMEM)],
        out_specs=pl.BlockSpec(memory_space=pltpu.VMEM),
        out_shape=_sds((SWA_HEADS, BLOCK, 2 * BLOCK), F32),
    )(rel_bias, bucket)


def _swa_logits(q, kp, kc):
    qs = (q.astype(F32) * SCALE).astype(_MXU)
    return qs, _dot_nt(qs, kp), _dot_nt(qs, kc)


def _swa_softmax(lp, lc, bias, sink, live_prev):
    r, c = _tri_masks(BLOCK)
    in_window = c > r if live_prev is None else jnp.logical_and(c > r, live_prev)
    lp = jnp.where(in_window, lp + bias[:, :BLOCK], -jnp.inf)
    lc = jnp.where(c <= r, lc + bias[:, BLOCK:], -jnp.inf)
    m = jnp.maximum(jnp.maximum(jnp.max(lp, axis=1, keepdims=True), jnp.max(lc, axis=1, keepdims=True)), sink)
    pp = jnp.exp(lp - m)
    pc = jnp.exp(lc - m)
    ps = jnp.exp(sink - m)
    denom = jnp.sum(pp, axis=1, keepdims=True) + jnp.sum(pc, axis=1, keepdims=True) + ps
    return pp / denom, pc / denom, ps / denom


def _swa_sub(nb):
    return min(SWA_SUB, nb)


def _swa_keys(b, prev_ref, cur_ref, i):
    cur = cur_ref[0, b * BLOCK:(b + 1) * BLOCK, :]
    if b == 0:
        return prev_ref[0], cur, i > 0
    return cur_ref[0, (b - 1) * BLOCK:b * BLOCK, :], cur, None


def _swa_fwd(q, k, v, bias, sink):
    S = q.shape[1]
    nb = S // BLOCK
    ns = _swa_sub(nb)
    R = ns * BLOCK

    def body(q_ref, kp_ref, kc_ref, vp_ref, vc_ref, bias_ref, sink_ref, o_ref):
        i = pl.program_id(1)
        bias = bias_ref[0]
        sink = sink_ref[0][:, :1]
        subs = range(ns)
        rows = [slice(b * BLOCK, (b + 1) * BLOCK) for b in subs]
        keys = [_swa_keys(b, kp_ref, kc_ref, i) for b in subs]
        vals = [_swa_keys(b, vp_ref, vc_ref, i) for b in subs]
        logits = [_swa_logits(q_ref[0, rows[b], :], keys[b][0], keys[b][1]) for b in subs]
        ws = [_swa_softmax(logits[b][1], logits[b][2], bias, sink, keys[b][2]) for b in subs]
        for b in subs:
            o_ref[0, rows[b], :] = _dot(ws[b][0].astype(_MXU), vals[b][0]) + _dot(ws[b][1].astype(_MXU), vals[b][1])

    prev = pl.BlockSpec((1, BLOCK, HEAD_DIM), lambda h, i: (h // SWA_GROUP, jnp.maximum(i * ns - 1, 0), 0))
    cur = pl.BlockSpec((1, R, HEAD_DIM), lambda h, i: (h // SWA_GROUP, i, 0))
    qblk = pl.BlockSpec((1, R, HEAD_DIM), lambda h, i: (h, i, 0))
    return pl.pallas_call(
        body, name="swa_fwd", grid=(SWA_HEADS, nb // ns),
        in_specs=[qblk, prev, cur, prev, cur,
                  pl.BlockSpec((1, BLOCK, 2 * BLOCK), lambda h, i: (h, 0, 0)),
                  pl.BlockSpec((1, 1, BLOCK), lambda h, i: (h, 0, 0))],
        out_specs=qblk,
        out_shape=_sds((SWA_HEADS, S, HEAD_DIM), F32),
        compiler_params=_cp(("parallel", "parallel")),
    )(q, k, k, v, v, bias, sink)


def _swa_bwd(q, k, v, bias, sink, do):
    S = q.shape[1]
    nb = S // BLOCK
    ns = _swa_sub(nb)
    R = ns * BLOCK

    def body(q_ref, kp_ref, kc_ref, vp_ref, vc_ref, bias_ref, sink_ref, do_ref, dq_ref, dk_ref, dv_ref, dbias_ref,
             dsink_ref):
        g = pl.program_id(1)
        i = pl.program_id(2)

        @pl.when(jnp.logical_and(g == 0, i == 0))
        def _():
            dk_ref[...] = jnp.zeros_like(dk_ref)
            dv_ref[...] = jnp.zeros_like(dv_ref)

        @pl.when(i == 0)
        def _():
            dbias_ref[...] = jnp.zeros_like(dbias_ref)
            dsink_ref[...] = jnp.zeros_like(dsink_ref)

        bias = bias_ref[0]
        sink = sink_ref[0][:, :1]
        subs = range(ns)
        rows = [slice(b * BLOCK, (b + 1) * BLOCK) for b in subs]
        keys = [_swa_keys(b, kp_ref, kc_ref, i) for b in subs]
        vals = [_swa_keys(b, vp_ref, vc_ref, i) for b in subs]
        douts = [do_ref[0, rows[b], :] for b in subs]
        logits = [_swa_logits(q_ref[0, rows[b], :], keys[b][0], keys[b][1]) for b in subs]
        dws = [(_dot_nt(douts[b], vals[b][0]), _dot_nt(douts[b], vals[b][1])) for b in subs]
        dbp = jnp.zeros((BLOCK, BLOCK), F32)
        dbc = jnp.zeros((BLOCK, BLOCK), F32)
        dsk = jnp.zeros((BLOCK, 1), F32)
        wts, dls = [], []
        for b in subs:
            wp, wc, ws = _swa_softmax(logits[b][1], logits[b][2], bias, sink, keys[b][2])
            dwp, dwc = dws[b]
            delta = jnp.sum(wp * dwp, axis=1, keepdims=True) + jnp.sum(wc * dwc, axis=1, keepdims=True)
            dlp = wp * (dwp - delta)
            dlc = wc * (dwc - delta)
            dbp += dlp
            dbc += dlc
            dsk -= ws * delta
            wts.append((wp.astype(_MXU), wc.astype(_MXU)))
            dls.append((dlp.astype(_MXU), dlc.astype(_MXU)))
        for b in subs:
            dq_ref[0, rows[b], :] = (_dot(dls[b][0], keys[b][0]) + _dot(dls[b][1], keys[b][1])) * SCALE
        for b in subs:
            qs = logits[b][0]
            blk = i * ns + b
            dk_ref[0, blk] += _dot_tn(dls[b][1], qs)
            dv_ref[0, blk] += _dot_tn(wts[b][1], douts[b])
            if b == 0:
                @pl.when(i > 0)
                def _():
                    dk_ref[0, blk - 1] += _dot_tn(dls[0][0], qs)
                    dv_ref[0, blk - 1] += _dot_tn(wts[0][0], douts[0])
            else:
                dk_ref[0, blk - 1] += _dot_tn(dls[b][0], qs)
                dv_ref[0, blk - 1] += _dot_tn(wts[b][0], douts[b])
        dbias_ref[0, :, :BLOCK] += dbp
        dbias_ref[0, :, BLOCK:] += dbc
        dsink_ref[0] += jnp.broadcast_to(dsk, (BLOCK, BLOCK))

    hq = lambda kv, g, i: kv * SWA_GROUP + g
    prev = pl.BlockSpec((1, BLOCK, HEAD_DIM), lambda kv, g, i: (kv, jnp.maximum(i * ns - 1, 0), 0))
    cur = pl.BlockSpec((1, R, HEAD_DIM), lambda kv, g, i: (kv, i, 0))
    qblk = pl.BlockSpec((1, R, HEAD_DIM), lambda kv, g, i: (hq(kv, g, i), i, 0))
    kvacc = pl.BlockSpec((1, nb, BLOCK, HEAD_DIM), lambda kv, g, i: (kv, 0, 0, 0))
    return pl.pallas_call(
        body, name="swa_bwd", grid=(SWA_KV_HEADS, SWA_GROUP, nb // ns),
        in_specs=[qblk, prev, cur, prev, cur,
                  pl.BlockSpec((1, BLOCK, 2 * BLOCK), lambda kv, g, i: (hq(kv, g, i), 0, 0)),
                  pl.BlockSpec((1, 1, BLOCK), lambda kv, g, i: (hq(kv, g, i), 0, 0)), qblk],
        out_specs=[qblk, kvacc, kvacc,
                   pl.BlockSpec((1, BLOCK, 2 * BLOCK), lambda kv, g, i: (hq(kv, g, i), 0, 0)),
                   pl.BlockSpec((1, BLOCK, BLOCK), lambda kv, g, i: (hq(kv, g, i), 0, 0))],
        out_shape=[_sds((SWA_HEADS, S, HEAD_DIM), F32), _sds((SWA_KV_HEADS, nb, BLOCK, HEAD_DIM), F32),
                   _sds((SWA_KV_HEADS, nb, BLOCK, HEAD_DIM), F32), _sds((SWA_HEADS, BLOCK, 2 * BLOCK), F32),
                   _sds((SWA_HEADS, BLOCK, BLOCK), F32)],
        compiler_params=_cp(("arbitrary", "arbitrary", "arbitrary")),
    )(q, k, k, v, v, bias, sink, do)


def _swa_small_grads(dbias, dsink, bucket):
    rows = REL_BUCKETS + 8

    def total(x):
        return jnp.sum(jnp.sum(x, axis=1, keepdims=True), axis=0, keepdims=True)

    def body(db_ref, ds_ref, bk_ref, o_ref):
        bk = bk_ref[...]
        r = lax.broadcasted_iota(jnp.int32, (rows, BLOCK), 0)
        c = lax.broadcasted_iota(jnp.int32, (rows, BLOCK), 1)
        out = jnp.zeros((rows, BLOCK), F32)
        for h in range(SWA_HEADS):
            db = db_ref[h]
            for b in range(REL_BUCKETS):
                s = total(jnp.where(bk == b, db, 0.0))
                out = jnp.where(jnp.logical_and(r == b, c == h), s, out)
            s = jnp.sum(ds_ref[h][:, :1], axis=0, keepdims=True)
            out = jnp.where(jnp.logical_and(r == REL_BUCKETS, c == h), s, out)
        o_ref[...] = out

    vm = pl.BlockSpec(memory_space=pltpu.VMEM)
    return pl.pallas_call(body, name="swa_small_grads", in_specs=[vm, vm, vm], out_specs=vm,
                          out_shape=_sds((rows, BLOCK), F32))(dbias, dsink, bucket)


def _tile_rows(n):
    for t in (512, 352, 256, 176, 128, 64, 32, 16, 8):
        if n % t == 0:
            return t
    return n


def _cast_rows(x, dtype, name):
    R, C = x.shape
    tr = _tile_rows(R)

    def body(x_ref, o_ref):
        o_ref[...] = x_ref[...].astype(o_ref.dtype)

    return pl.pallas_call(body, name=name, grid=(R // tr,), in_specs=[_rows(tr, C)], out_specs=_rows(tr, C),
                          out_shape=_sds((R, C), dtype), compiler_params=_cp(("parallel",)))(x)


def _pair_sum(g, recv, c, name):
    n, half, C = recv.shape
    tr = _tile_rows(half)

    def body(c_ref, a_ref, b_ref, o_ref):
        o_ref[...] = (a_ref[0] + b_ref[...]).astype(o_ref.dtype)

    return pl.pallas_call(
        body, name=name,
        grid_spec=pltpu.PrefetchScalarGridSpec(
            num_scalar_prefetch=1, grid=(n, half // tr),
            in_specs=[pl.BlockSpec((1, 1, tr, C), lambda j, i, c_ref: (j, c_ref[0], i, 0)),
                      pl.BlockSpec((1, tr, C), lambda j, i, c_ref: (j, i, 0))],
            out_specs=pl.BlockSpec((1, tr, C), lambda j, i, c_ref: (j, i, 0))),
        out_shape=_sds((n, half, C), _MXU),
        compiler_params=_cp(("parallel", "parallel")))(c.reshape(1), g.reshape(n, 2, half, C), recv)


def _chip_sum(own, recv, me, name):
    n, R, C = recv.shape
    tr = _tile_rows(R)

    def body(me_ref, own_ref, recv_ref, o_ref):
        acc = None
        for j in range(n):
            term = jnp.where(me_ref[0] == j, own_ref[0], recv_ref[j]).astype(F32)
            acc = term if acc is None else acc + term
        o_ref[...] = acc

    return pl.pallas_call(
        body, name=name,
        grid_spec=pltpu.PrefetchScalarGridSpec(
            num_scalar_prefetch=1, grid=(R // tr,),
            in_specs=[pl.BlockSpec((1, tr, C), lambda i, me_ref: (me_ref[0], i, 0)),
                      pl.BlockSpec((n, tr, C), lambda i, me_ref: (0, i, 0))],
            out_specs=pl.BlockSpec((tr, C), lambda i, me_ref: (i, 0))),
        out_shape=_sds((R, C), F32), compiler_params=_cp(("parallel",)))(me.reshape(1), own, recv)


def _adamw_math(w, g, m, v):
    m = ADAM_B1 * m + (1.0 - ADAM_B1) * g
    v = ADAM_B2 * v + (1.0 - ADAM_B2) * (g * g)
    m_hat = m / (1.0 - ADAM_B1 ** ADAM_STEP)
    v_hat = v / (1.0 - ADAM_B2 ** ADAM_STEP)
    delta = -ADAM_LR * (m_hat / (jnp.sqrt(v_hat) + ADAM_EPS) + ADAM_WD * w)
    return delta, m, v


def _adamw(w, g, m, v, name):
    R, C = w.shape
    tr = _tile_rows(R)

    def body(w_ref, g_ref, m_ref, v_ref, d_ref, nm_ref, nv_ref):
        d, nm, nv = _adamw_math(w_ref[...], g_ref[...], m_ref[...], v_ref[...])
        d_ref[...] = d
        nm_ref[...] = nm
        nv_ref[...] = nv

    blk = _rows(tr, C)
    return pl.pallas_call(body, name=name, grid=(R // tr,), in_specs=[blk] * 4, out_specs=[blk] * 3,
                          out_shape=[_sds((R, C), F32)] * 3, compiler_params=_cp(("parallel",)))(w, g, m, v)


def _place():
    x, y, c = lax.axis_index("x"), lax.axis_index("y"), lax.axis_index("c")
    chips = [(1 - x, y), (x, 1 - y), (1 - x, 1 - y)]
    return x, y, c, chips


def _gather_weights(shards):
    nw = len(shards)

    def body(*refs):
        in_refs, out_refs = refs[:nw], refs[nw:2 * nw]
        send_sems, recv_sems = refs[2 * nw:]
        x, y, c, chips = _place()
        me, sibling = (x, y, c), (x, y, 1 - c)

        def copy(w, k, chip, hc, to, src=None):
            part = out_refs[w].at[2 * chip[0] + chip[1], hc]
            return pltpu.make_async_remote_copy(
                src_ref=part if src is None else src, dst_ref=part, send_sem=send_sems.at[w, k],
                recv_sem=recv_sems.at[w, k], device_id=to, device_id_type=MESH)

        ws = range(nw)
        first = [copy(w, j, (x, y), c, (*chip, c), src=in_refs[w].at[c]) for w in ws for j, chip in enumerate(chips)]
        for cp in first:
            cp.start()
        passed = []
        for w in ws:
            for j, chip in enumerate(chips):
                copy(w, j, chip, c, me).wait_recv()
                passed.append(copy(w, 3 + j, chip, c, sibling))
                passed[-1].start()
        for w in ws:
            for j, chip in enumerate(chips):
                copy(w, 3 + j, chip, 1 - c, me).wait_recv()
        for cp in first + passed:
            cp.wait_send()

    any_spec = pl.BlockSpec(memory_space=pl.ANY)
    halves = [(s.shape[0] // 2, s.shape[1]) for s in shards]
    outs = pl.pallas_call(
        body, name="gather_weights", in_specs=[any_spec] * nw, out_specs=[any_spec] * nw,
        out_shape=[_sds((N_CHIPS, 2, h, cols), s.dtype) for s, (h, cols) in zip(shards, halves)],
        scratch_shapes=[pltpu.SemaphoreType.DMA((nw, 6)), pltpu.SemaphoreType.DMA((nw, 6))],
    )(*[s.reshape(2, h, cols) for s, (h, cols) in zip(shards, halves)])
    me = 2 * lax.axis_index("x") + lax.axis_index("y")
    return [lax.dynamic_update_slice_in_dim(o.reshape(N_CHIPS, 2 * h, cols), s[None], me, axis=0)
            for o, s, (h, cols) in zip(outs, shards, halves)]


def _swap_halves(grads):
    nw = len(grads)

    def body(*refs):
        g_refs, out_refs = refs[:nw], refs[nw:2 * nw]
        send_sems, recv_sems = refs[2 * nw:]
        x, y, c, _ = _place()
        cps = []
        for w in range(nw):
            half = out_refs[w].shape[1]
            theirs = g_refs[w].at[:, pl.ds(pl.multiple_of((1 - c) * half, 8), half), :]
            cps.append(pltpu.make_async_remote_copy(
                src_ref=theirs, dst_ref=out_refs[w], send_sem=send_sems.at[w], recv_sem=recv_sems.at[w],
                device_id=(x, y, 1 - c), device_id_type=MESH))
        for cp in cps:
            cp.start()
        for cp in cps:
            cp.wait()

    any_spec = pl.BlockSpec(memory_space=pl.ANY)
    return pl.pallas_call(
        body, name="swap_halves", in_specs=[any_spec] * nw, out_specs=[any_spec] * nw,
        out_shape=[_sds((g.shape[0], g.shape[1] // 2, g.shape[2]), g.dtype) for g in grads],
        scratch_shapes=[pltpu.SemaphoreType.DMA((nw,)), pltpu.SemaphoreType.DMA((nw,))],
    )(*grads)


def _scatter_partials(parts):
    nw = len(parts)

    def body(*refs):
        p_refs, out_refs = refs[:nw], refs[nw:2 * nw]
        send_sems, recv_sems = refs[2 * nw:]
        x, y, c, chips = _place()
        me = 2 * x + y
        ws = range(nw)

        def copy(w, j, chip, src_chip, dst_chip):
            return pltpu.make_async_remote_copy(
                src_ref=p_refs[w].at[src_chip], dst_ref=out_refs[w].at[dst_chip], send_sem=send_sems.at[w, j],
                recv_sem=recv_sems.at[w, j], device_id=(*chip, c), device_id_type=MESH)

        sends = [copy(w, j, chip, 2 * chip[0] + chip[1], me) for w in ws for j, chip in enumerate(chips)]
        for cp in sends:
            cp.start()
        for w in ws:
            for j, chip in enumerate(chips):
                copy(w, j, chip, me, 2 * chip[0] + chip[1]).wait_recv()
        for cp in sends:
            cp.wait_send()

    any_spec = pl.BlockSpec(memory_space=pl.ANY)
    return pl.pallas_call(
        body, name="scatter_partials", in_specs=[any_spec] * nw, out_specs=[any_spec] * nw,
        out_shape=[_sds(p.shape, p.dtype) for p in parts],
        scratch_shapes=[pltpu.SemaphoreType.DMA((nw, 3)), pltpu.SemaphoreType.DMA((nw, 3))],
    )(*parts)


def _join_halves(sums):
    nw = len(sums)

    def body(*refs):
        f_refs, out_refs = refs[:nw], refs[nw:2 * nw]
        send_sems, recv_sems = refs[2 * nw:]
        x, y, c, _ = _place()
        ws = range(nw)

        def copy(w, half_index):
            return pltpu.make_async_remote_copy(
                src_ref=f_refs[w], dst_ref=out_refs[w].at[half_index], send_sem=send_sems.at[w],
                recv_sem=recv_sems.at[w], device_id=(x, y, 1 - c), device_id_type=MESH)

        sends = [copy(w, c) for w in ws]
        for cp in sends:
            cp.start()
        for w in ws:
            copy(w, 1 - c).wait_recv()
        for cp in sends:
            cp.wait_send()

    any_spec = pl.BlockSpec(memory_space=pl.ANY)
    outs = pl.pallas_call(
        body, name="join_halves", in_specs=[any_spec] * nw, out_specs=[any_spec] * nw,
        out_shape=[_sds((2,) + f.shape, f.dtype) for f in sums],
        scratch_shapes=[pltpu.SemaphoreType.DMA((nw,)), pltpu.SemaphoreType.DMA((nw,))],
    )(*sums)
    c = lax.axis_index("c")
    return [lax.dynamic_update_slice_in_dim(o, f[None], c, axis=0).reshape(2 * f.shape[0], f.shape[1])
            for o, f in zip(outs, sums)]


def _allreduce_small(block):
    m_per, n = block.shape

    def body(x_ref, sum_ref, loss_ref, all_ref, send_sems, recv_sems, local_sem):
        x, y, c, chips = _place()
        me, sibling = (x, y, c), (x, y, 1 - c)

        def rows(px, py, pc):
            return all_ref.at[pl.ds(pl.multiple_of((4 * px + 2 * py + pc) * m_per, 8), m_per), :]

        def copy(k, blk, to, src=None):
            return pltpu.make_async_remote_copy(
                src_ref=rows(*blk) if src is None else src, dst_ref=rows(*blk), send_sem=send_sems.at[k],
                recv_sem=recv_sems.at[k], device_id=to, device_id_type=MESH)

        mine = pltpu.make_async_copy(x_ref, rows(*me), local_sem)
        mine.start()
        first = [copy(0, me, sibling, src=x_ref)]
        first += [copy(1 + j, me, (*chip, c), src=x_ref) for j, chip in enumerate(chips)]
        for cp in first:
            cp.start()
        passed = [copy(4 + j, (*chip, c), sibling) for j, chip in enumerate(chips)]
        for j, chip in enumerate(chips):
            copy(1 + j, (*chip, c), me).wait_recv()
            passed[j].start()
        copy(0, sibling, me).wait_recv()
        for j, chip in enumerate(chips):
            copy(4 + j, (*chip, 1 - c), me).wait_recv()
        for cp in first + passed:
            cp.wait_send()
        mine.wait()

        acc = all_ref[0:m_per, :]
        for d in range(1, 8):
            acc = acc + all_ref[d * m_per:(d + 1) * m_per, :]
        sum_ref[...] = acc
        tot = jnp.sum(acc[8:9, :], axis=1, keepdims=True) * (0.5 / D_MODEL)
        loss_ref[...] = jnp.broadcast_to(tot, loss_ref.shape)

    vm = pl.BlockSpec(memory_space=pltpu.VMEM)
    return pl.pallas_call(
        body, name="allreduce_small", in_specs=[vm], out_specs=[vm, vm],
        out_shape=[_sds((m_per, n), F32), _sds((8, 128), F32)],
        scratch_shapes=[pltpu.VMEM((8 * m_per, n), F32), pltpu.SemaphoreType.DMA((7,)), pltpu.SemaphoreType.DMA((7,)),
                        pltpu.SemaphoreType.DMA],
    )(block)


def _heads_rows(x, nh):
    S = x.shape[0]
    return x.reshape(S, nh, HEAD_DIM).transpose(1, 0, 2)


def _heads_cols(x, nh):
    S = x.shape[0]
    return x.reshape(S, nh, HEAD_DIM).transpose(1, 2, 0)


def _key_blocks(x, nh, t):
    S = x.shape[0]
    return x.reshape(S // t, t, nh, HEAD_DIM).transpose(2, 0, 1, 3)


def _key_blocks_t(x, nh, t):
    S = x.shape[0]
    return x.reshape(S // t, t, nh, HEAD_DIM).transpose(2, 0, 3, 1)


def _pad_row(v):
    v = v.reshape(1, -1)
    return jnp.pad(v, ((0, 0), (0, D_MODEL - v.shape[1])))


def _pack_small(ln_in_g, ln_in_b, sb_g, swa_g, sinks, rel_bias, ln1_g, ln1_b, ln2_g, ln2_b, extra):
    rows = [_pad_row(ln_in_g), _pad_row(ln_in_b), jnp.concatenate([sb_g.reshape(1, -1), swa_g.reshape(1, -1)], axis=1),
            _pad_row(jnp.concatenate([rel_bias.reshape(1, -1), sinks.reshape(1, -1)], axis=1)),
            _pad_row(ln1_g), _pad_row(ln1_b), _pad_row(ln2_g), _pad_row(ln2_b), _pad_row(extra)]
    rows.append(jnp.zeros((SMALL_ROWS - len(rows), D_MODEL), F32))
    return jnp.concatenate(rows, axis=0)


def _unpack_small(blk):
    nrb = REL_BUCKETS * SWA_HEADS
    return (blk[0], blk[1], blk[2:3, :SB_WIDTH], blk[2:3, SB_WIDTH:], blk[3:4, nrb:nrb + SWA_HEADS],
            blk[3, :nrb].reshape(REL_BUCKETS, SWA_HEADS), blk[4:5], blk[5:6], blk[6:7], blk[7:8])


def kernel(x, ln_in_g, ln_in_b, w_in, sb_norm_g, swa_norm_g, sinks, rel_bias, w_out, ln1_g, ln1_b, w_gate_up, w_down, ln2_g, ln2_b, loss_target, m_ln_in_g, m_ln_in_b, m_w_in, m_sb_norm_g, m_swa_norm_g, m_sinks, m_rel_bias, m_w_out, m_ln1_g, m_ln1_b, m_w_gate_up, m_w_down, m_ln2_g, m_ln2_b, v_ln_in_g, v_ln_in_b, v_w_in, v_sb_norm_g, v_swa_norm_g, v_sinks, v_rel_bias, v_w_out, v_ln1_g, v_ln1_b, v_w_gate_up, v_w_down, v_ln2_g, v_ln2_b):
    S = x.shape[1]
    x2 = x.reshape(S, D_MODEL)
    tgt = loss_target.reshape(S, D_MODEL)
    T = min(S, 256)
    bucket = jnp.asarray(_bucket_table())
    row = lambda v: v.reshape(1, -1)

    shards = [_cast_rows(w[0], _MXU, "cast_" + n) for n, w in (("w_in", w_in), ("w_out", w_out), ("w_gate_up", w_gate_up), ("w_down", w_down))]
    w_in_sh, w_out_sh, w_gu_sh, w_down_sh = _gather_weights(shards)
    w_in_f = jnp.concatenate([w_in_sh[j] for j in range(N_CHIPS)], axis=1)
    w_out_f = w_out_sh.reshape(D_MODEL, D_MODEL)
    w_down_f = w_down_sh.reshape(D_FF, D_MODEL)

    h0, h0b, proj = _ln_in_proj(x2, row(ln_in_g), row(ln_in_b), w_in_f)
    o1, o2, o3, o4, o5 = SB_WIDTH, 2 * SB_WIDTH, 3 * SB_WIDTH, 3 * SB_WIDTH + SWA_WIDTH, 3 * SB_WIDTH + SWA_WIDTH + SWA_KV_WIDTH
    q_sb, k_sb, v_sb = proj[:, :o1], proj[:, o1:o2], proj[:, o2:o3]
    q_sw, k_sw, v_sw = proj[:, o3:o4], proj[:, o4:o5], proj[:, o5:]
    qT_sb = _heads_cols(q_sb, SB_HEADS)
    kb_sb = _key_blocks(k_sb, SB_HEADS, T)
    oT_sb, rsave = _sb_fwd(qT_sb, kb_sb, _key_blocks_t(v_sb, SB_HEADS, T))
    sb_out = oT_sb.transpose(2, 0, 1).reshape(S, SB_WIDTH)

    bias = _swa_bias(rel_bias, bucket)
    sink_rows = jnp.broadcast_to(sinks.reshape(SWA_HEADS, 1, 1), (SWA_HEADS, 1, BLOCK))
    qh_sw, kh_sw, vh_sw = _heads_rows(q_sw, SWA_HEADS), _heads_rows(k_sw, SWA_KV_HEADS), _heads_rows(v_sw, SWA_KV_HEADS)
    swa_out = _swa_fwd(qh_sw, kh_sw, vh_sw, bias, sink_rows).transpose(1, 0, 2).reshape(S, SWA_WIDTH)

    pre1, merged = _mix_out(sb_out, swa_out, sb_norm_g, swa_norm_g, w_out_f, h0)
    h1b, gate, up, act = _ffn_up(pre1, ln1_g, ln1_b, w_gu_sh)
    dp2, dp2b, dg2, db2, errsum = _ffn_down_loss(act, w_down_f, pre1, ln1_g, ln1_b, ln2_g, ln2_b, tgt)

    g_w_down = _matmul_tn(act, dp2b, "grad_w_down", FF_CHUNK, D_MODEL)
    dgate, dup = _ffn_down_bwd(dp2b, w_down_f, gate, up)
    g_w_gu = _matmul_tn_pair(h1b, dgate, dup, "grad_w_gate_up")
    dp1, dp1b, dg1, db1 = _ffn_up_bwd(dgate, dup, w_gu_sh, dp2, pre1, ln1_g)
    g_w_out = _matmul_tn(merged, dp1b, "grad_w_out", D_MODEL, D_MODEL)
    dsb, dsw, dgsb, dgsw = _mix_bwd(dp1b, w_out_f, sb_out, swa_out, sb_norm_g, swa_norm_g)

    dqh_sw, dkh_sw, dvh_sw, dbias, dsink = _swa_bwd(qh_sw, kh_sw, vh_sw, bias, sink_rows, _heads_rows(dsw, SWA_HEADS))
    swa_small = _swa_small_grads(dbias, dsink, bucket)
    dqT_sb, dkT_sb, dvT_sb = _sb_bwd(qT_sb, kb_sb, _key_blocks_t(k_sb, SB_HEADS, T), _key_blocks(v_sb, SB_HEADS, T),
                                     _heads_cols(dsb, SB_HEADS), rsave)
    tok = lambda t, nh: t.reshape(nh, S, HEAD_DIM).transpose(1, 0, 2).reshape(S, nh * HEAD_DIM)
    tokT = lambda t: t.transpose(1, 3, 0, 2).reshape(S, SB_WIDTH)
    dproj = jnp.concatenate([dqT_sb.transpose(2, 0, 1).reshape(S, SB_WIDTH), tokT(dkT_sb), tokT(dvT_sb),
                             tok(dqh_sw, SWA_HEADS), tok(dkh_sw, SWA_KV_HEADS), tok(dvh_sw, SWA_KV_HEADS)],
                            axis=1).astype(_MXU)
    g_w_in = _matmul_tn(h0b, dproj, "grad_w_in", D_MODEL, IN_COLS // 2)
    grad_x, dg_in, db_in = _in_proj_bwd(dproj, w_in_f, dp1, x2, row(ln_in_g))

    cin = IN_COLS // N_CHIPS
    grads = [jnp.stack([g_w_in[:, j * cin:(j + 1) * cin] for j in range(N_CHIPS)]),
             g_w_out.reshape(N_CHIPS, D_MODEL // N_CHIPS, D_MODEL), g_w_gu,
             g_w_down.reshape(N_CHIPS, D_FF // N_CHIPS, D_MODEL)]
    names = ("w_in", "w_out", "w_gate_up", "w_down")
    c = lax.axis_index("c").astype(jnp.int32)
    partials = [_pair_sum(g, r, c, "pair_sum_" + n) for g, r, n in zip(grads, _swap_halves(grads), names)]
    me = (2 * lax.axis_index("x") + lax.axis_index("y")).astype(jnp.int32)
    sums = [_chip_sum(p, r, me, "chip_sum_" + n) for p, r, n in zip(partials, _scatter_partials(partials), names)]
    gs_in, gs_out, gs_gu, gs_down = _join_halves(sums)

    nrb = REL_BUCKETS * SWA_HEADS
    small = _pack_small(dg_in, db_in, dgsb, dgsw, swa_small[REL_BUCKETS, :SWA_HEADS],
                        swa_small[:REL_BUCKETS, :SWA_HEADS], dg1, db1, dg2, db2, errsum)
    g_small, loss_tile = _allreduce_small(small)
    loss = loss_tile[0, 0]

    big = []
    for name, w, g, m, v in (("adamw_w_in", w_in, gs_in, m_w_in, v_w_in), ("adamw_w_out", w_out, gs_out, m_w_out, v_w_out),
                             ("adamw_w_gate_up", w_gate_up, gs_gu, m_w_gate_up, v_w_gate_up),
                             ("adamw_w_down", w_down, gs_down, m_w_down, v_w_down)):
        d, nm, nv = _adamw(w[0], g, m[0], v[0], name)
        big.append((g[None], d[None], nm[None], nv[None]))
    zero = jnp.zeros((1,), F32)
    w_small = _pack_small(ln_in_g, ln_in_b, sb_norm_g, swa_norm_g, sinks, rel_bias, ln1_g, ln1_b, ln2_g, ln2_b, zero)
    m_small = _pack_small(m_ln_in_g, m_ln_in_b, m_sb_norm_g, m_swa_norm_g, m_sinks, m_rel_bias, m_ln1_g, m_ln1_b,
                          m_ln2_g, m_ln2_b, zero)
    v_small = _pack_small(v_ln_in_g, v_ln_in_b, v_sb_norm_g, v_swa_norm_g, v_sinks, v_rel_bias, v_ln1_g, v_ln1_b,
                          v_ln2_g, v_ln2_b, zero)
    small_out = [_unpack_small(t) for t in (g_small,) + tuple(_adamw(w_small, g_small, m_small, v_small, "adamw_small"))]

    def kind(k):
        s = small_out[k]
        return [s[0], s[1], big[0][k], s[2], s[3], s[4], s[5], big[1][k], s[6], s[7], big[2][k], big[3][k], s[8], s[9]]

    return (loss, grad_x.reshape(1, S, D_MODEL), *kind(0), *kind(1), *kind(2), *kind(3))
```

```python
import functools
import math

import numpy as np
import jax
import jax.numpy as jnp
from jax import lax
from jax.experimental import pallas as pl
from jax.experimental.pallas import tpu as pltpu

F32 = jnp.float32
_MXU = jnp.bfloat16

D_MODEL = 1024
HEAD_DIM = 64
SB_HEADS = 8
SWA_HEADS = 8
SWA_KV_HEADS = 2
SWA_GROUP = SWA_HEADS // SWA_KV_HEADS
SB_WIDTH = SB_HEADS * HEAD_DIM
SWA_WIDTH = SWA_HEADS * HEAD_DIM
SWA_KV_WIDTH = SWA_KV_HEADS * HEAD_DIM
IN_COLS = 3 * SB_WIDTH + SWA_WIDTH + 2 * SWA_KV_WIDTH
BLOCK = 128
REL_BUCKETS = 32
REL_MAX_DIST = 128
D_FF = 2816
FF_CHUNK = D_FF // 2
ALPHA = 2.0 ** 0.25
LN_EPS = 1e-5
RMS_EPS = 1e-6
SCALE = HEAD_DIM ** -0.5
SB_TILE = 256
SB_GROUP_FWD = 8
SB_GROUP_BWD = 4
SB_DEAD = -105.0
SWA_SUB = 8

ADAM_LR = 0.001
ADAM_B1 = 0.9
ADAM_B2 = 0.999
ADAM_EPS = 1e-08
ADAM_WD = 0.01
ADAM_STEP = 10

N_CHIPS = 4
SMALL_ROWS = 16

MESH = pl.DeviceIdType.MESH


def _sds(shape, dtype):
    return jax.ShapeDtypeStruct(shape, dtype)


def _cp(sem=None, vmem_mb=48):
    kw = dict(vmem_limit_bytes=vmem_mb * 1024 * 1024)
    if sem is not None:
        kw["dimension_semantics"] = sem
    return pltpu.CompilerParams(**kw)


def _dot(a, b):
    return jnp.dot(a, b, preferred_element_type=F32)


def _dot_nt(a, b):
    return lax.dot_general(a, b, (((1,), (1,)), ((), ())), preferred_element_type=F32)


def _dot_tn(a, b):
    return lax.dot_general(a, b, (((0,), (0,)), ((), ())), preferred_element_type=F32)


def _ln_hat(x):
    mu = jnp.mean(x, axis=-1, keepdims=True)
    xc = x - mu
    var = jnp.mean(xc * xc, axis=-1, keepdims=True)
    rstd = lax.rsqrt(var + LN_EPS)
    return xc * rstd, rstd


def _ln_bwd(xhat, rstd, dy, g):
    dxh = dy * g
    m1 = jnp.mean(dxh, axis=-1, keepdims=True)
    m2 = jnp.mean(dxh * xhat, axis=-1, keepdims=True)
    return rstd * (dxh - m1 - xhat * m2)


def _colsum(x):
    return jnp.sum(x, axis=0, keepdims=True)


def _split2(x):
    hi = x.astype(_MXU)
    lo = (x - hi.astype(F32)).astype(_MXU)
    return hi, lo


def _rows(tm, n):
    return pl.BlockSpec((tm, n), lambda i: (i, 0))


def _fixed(*shape):
    nd = len(shape)
    return pl.BlockSpec(shape, lambda i: (0,) * nd)


def _ln_in_proj(x, g, b, w):
    S = x.shape[0]
    N = w.shape[1]
    tm = min(S, 512)

    def body(x_ref, g_ref, b_ref, w_ref, h_ref, hb_ref, p_ref):
        xhat, _ = _ln_hat(x_ref[...])
        h = xhat * g_ref[...] + b_ref[...]
        h_ref[...] = h
        hb = h.astype(_MXU)
        hb_ref[...] = hb
        p_ref[...] = _dot(hb, w_ref[...]).astype(p_ref.dtype)

    return pl.pallas_call(
        body, name="ln_in_proj", grid=(S // tm,),
        in_specs=[_rows(tm, D_MODEL), _fixed(1, D_MODEL), _fixed(1, D_MODEL), _fixed(D_MODEL, N)],
        out_specs=[_rows(tm, D_MODEL), _rows(tm, D_MODEL), _rows(tm, N)],
        out_shape=[_sds((S, D_MODEL), F32), _sds((S, D_MODEL), _MXU), _sds((S, N), _MXU)],
        compiler_params=_cp(("parallel",)),
    )(x, g, b, w)


def _rms(x, g):
    r = lax.rsqrt(jnp.mean(x * x, axis=-1, keepdims=True) + RMS_EPS)
    return x * r * g, r


def _mix_out(sb, sw, gsb, gsw, w_out, h0):
    S = sb.shape[0]
    tm = min(S, 512)

    def body(sb_ref, sw_ref, gsb_ref, gsw_ref, w_ref, h0_ref, pre_ref, mg_ref):
        ysb, _ = _rms(sb_ref[...], gsb_ref[...])
        ysw, _ = _rms(sw_ref[...], gsw_ref[...])
        ysb = ysb.astype(_MXU)
        ysw = ysw.astype(_MXU)
        mg_ref[:, :SB_WIDTH] = ysb
        mg_ref[:, SB_WIDTH:] = ysw
        mix = _dot(ysb, w_ref[:SB_WIDTH, :]) + _dot(ysw, w_ref[SB_WIDTH:, :])
        pre_ref[...] = ALPHA * h0_ref[...] + mix

    return pl.pallas_call(
        body, name="mix_out", grid=(S // tm,),
        in_specs=[_rows(tm, SB_WIDTH), _rows(tm, SWA_WIDTH), _fixed(1, SB_WIDTH), _fixed(1, SWA_WIDTH),
                  _fixed(D_MODEL, D_MODEL), _rows(tm, D_MODEL)],
        out_specs=[_rows(tm, D_MODEL), _rows(tm, D_MODEL)],
        out_shape=[_sds((S, D_MODEL), F32), _sds((S, D_MODEL), _MXU)],
        compiler_params=_cp(("parallel",)),
    )(sb, sw, gsb, gsw, w_out, h0)


def _sigmoid(x):
    return 1.0 / (1.0 + jnp.exp(-x))


def _ffn_up(pre1, g1, b1, wgu):
    S = pre1.shape[0]
    tm = min(S, 512)

    def body(p_ref, g_ref, b_ref, wg_ref, wu_ref, h1_ref, gate_ref, up_ref, a_ref):
        xhat, _ = _ln_hat(p_ref[...])
        h1 = (xhat * g_ref[...] + b_ref[...]).astype(_MXU)
        h1_ref[...] = h1
        gate = _dot(h1, wg_ref[0])
        up = _dot(h1, wu_ref[0])
        gate_ref[...] = gate.astype(gate_ref.dtype)
        up_ref[...] = up.astype(up_ref.dtype)
        a_ref[...] = (gate * _sigmoid(gate) * up).astype(a_ref.dtype)

    chunk = lambda i, j: (i, j)
    return pl.pallas_call(
        body, name="ffn_up", grid=(S // tm, 2),
        in_specs=[pl.BlockSpec((tm, D_MODEL), lambda i, j: (i, 0)),
                  pl.BlockSpec((1, D_MODEL), lambda i, j: (0, 0)),
                  pl.BlockSpec((1, D_MODEL), lambda i, j: (0, 0)),
                  pl.BlockSpec((1, D_MODEL, FF_CHUNK), lambda i, j: (j, 0, 0)),
                  pl.BlockSpec((1, D_MODEL, FF_CHUNK), lambda i, j: (j + 2, 0, 0))],
        out_specs=[pl.BlockSpec((tm, D_MODEL), lambda i, j: (i, 0)),
                   pl.BlockSpec((tm, FF_CHUNK), chunk), pl.BlockSpec((tm, FF_CHUNK), chunk),
                   pl.BlockSpec((tm, FF_CHUNK), chunk)],
        out_shape=[_sds((S, D_MODEL), _MXU), _sds((S, D_FF), _MXU), _sds((S, D_FF), _MXU), _sds((S, D_FF), _MXU)],
        compiler_params=_cp(("parallel", "arbitrary")),
    )(pre1, g1, b1, wgu, wgu)


def _ffn_down_loss(a, w_down, pre1, g1, b1, g2, b2, tgt):
    S = a.shape[0]
    tm = min(S, 512)

    def body(a_ref, w_ref, p_ref, g1_ref, b1_ref, g2_ref, b2_ref, t_ref, d_ref, db_ref, dg2_ref, db2_ref, err_ref):
        @pl.when(pl.program_id(0) == 0)
        def _():
            dg2_ref[...] = jnp.zeros_like(dg2_ref)
            db2_ref[...] = jnp.zeros_like(db2_ref)
            err_ref[...] = jnp.zeros_like(err_ref)

        xhat1, _ = _ln_hat(p_ref[...])
        h1 = xhat1 * g1_ref[...] + b1_ref[...]
        pre2 = ALPHA * h1 + _dot(a_ref[...], w_ref[...])
        xhat2, rstd2 = _ln_hat(pre2)
        err = xhat2 * g2_ref[...] + b2_ref[...] - t_ref[...]
        dh2 = err * (1.0 / D_MODEL)
        dp2 = _ln_bwd(xhat2, rstd2, dh2, g2_ref[...])
        d_ref[...] = dp2
        db_ref[...] = dp2.astype(db_ref.dtype)
        dg2_ref[...] += _colsum(dh2 * xhat2)
        db2_ref[...] += _colsum(dh2)
        err_ref[...] += _colsum(err * err)

    vec = _fixed(1, D_MODEL)
    return pl.pallas_call(
        body, name="ffn_down_loss", grid=(S // tm,),
        in_specs=[_rows(tm, D_FF), _fixed(D_FF, D_MODEL), _rows(tm, D_MODEL), vec, vec, vec, vec, _rows(tm, D_MODEL)],
        out_specs=[_rows(tm, D_MODEL), _rows(tm, D_MODEL), vec, vec, vec],
        out_shape=[_sds((S, D_MODEL), F32), _sds((S, D_MODEL), _MXU), _sds((1, D_MODEL), F32), _sds((1, D_MODEL), F32),
                   _sds((1, D_MODEL), F32)],
        compiler_params=_cp(("arbitrary",)),
    )(a, w_down, pre1, g1, b1, g2, b2, tgt)


def _ffn_down_bwd(dp2b, w_down, gate, up):
    S = dp2b.shape[0]
    tm = min(S, 512)

    def body(d_ref, w_ref, g_ref, u_ref, dg_ref, du_ref):
        da = _dot_nt(d_ref[...], w_ref[...])
        g = g_ref[...].astype(F32)
        u = u_ref[...].astype(F32)
        sg = _sigmoid(g)
        du_ref[...] = (da * g * sg).astype(du_ref.dtype)
        dg_ref[...] = (da * u * (sg * (1.0 + g * (1.0 - sg)))).astype(dg_ref.dtype)

    chunk = pl.BlockSpec((tm, FF_CHUNK), lambda i, j: (i, j))
    return pl.pallas_call(
        body, name="ffn_down_bwd", grid=(S // tm, 2),
        in_specs=[pl.BlockSpec((tm, D_MODEL), lambda i, j: (i, 0)),
                  pl.BlockSpec((FF_CHUNK, D_MODEL), lambda i, j: (j, 0)), chunk, chunk],
        out_specs=[chunk, chunk],
        out_shape=[_sds((S, D_FF), _MXU), _sds((S, D_FF), _MXU)],
        compiler_params=_cp(("parallel", "arbitrary")),
    )(dp2b, w_down, gate, up)


def _ffn_up_bwd(dgate, dup, wgu, dp2, pre1, g1):
    S = dgate.shape[0]
    tm = min(S, 256)

    def body(dg_ref, du_ref, w_ref, d2_ref, p_ref, g_ref, d1_ref, d1b_ref, dg1_ref, db1_ref):
        @pl.when(pl.program_id(0) == 0)
        def _():
            dg1_ref[...] = jnp.zeros_like(dg1_ref)
            db1_ref[...] = jnp.zeros_like(db1_ref)

        dh1 = ALPHA * d2_ref[...]
        for j in range(2):
            cols = slice(j * FF_CHUNK, (j + 1) * FF_CHUNK)
            dh1 += _dot_nt(dg_ref[:, cols], w_ref[j])
            dh1 += _dot_nt(du_ref[:, cols], w_ref[j + 2])
        xhat, rstd = _ln_hat(p_ref[...])
        dp1 = _ln_bwd(xhat, rstd, dh1, g_ref[...])
        d1_ref[...] = dp1
        d1b_ref[...] = dp1.astype(d1b_ref.dtype)
        dg1_ref[...] += _colsum(dh1 * xhat)
        db1_ref[...] += _colsum(dh1)

    vec = _fixed(1, D_MODEL)
    return pl.pallas_call(
        body, name="ffn_up_bwd", grid=(S // tm,),
        in_specs=[_rows(tm, D_FF), _rows(tm, D_FF), _fixed(4, D_MODEL, FF_CHUNK), _rows(tm, D_MODEL),
                  _rows(tm, D_MODEL), vec],
        out_specs=[_rows(tm, D_MODEL), _rows(tm, D_MODEL), vec, vec],
        out_shape=[_sds((S, D_MODEL), F32), _sds((S, D_MODEL), _MXU), _sds((1, D_MODEL), F32), _sds((1, D_MODEL), F32)],
        compiler_params=_cp(("arbitrary",), vmem_mb=56),
    )(dgate, dup, wgu, dp2, pre1, g1)


def _rms_bwd(x, g, dy):
    n = x.shape[-1]
    r = lax.rsqrt(jnp.mean(x * x, axis=-1, keepdims=True) + RMS_EPS)
    u = dy * g
    dx = r * u - x * (r * r * r) * (jnp.sum(u * x, axis=-1, keepdims=True) * (1.0 / n))
    return dx, _colsum(dy * x * r)


def _mix_bwd(dp1b, w_out, sb, sw, gsb, gsw):
    S = sb.shape[0]
    tm = min(S, 512)

    def body(d_ref, w_ref, sb_ref, sw_ref, gsb_ref, gsw_ref, dsb_ref, dsw_ref, dgsb_ref, dgsw_ref):
        @pl.when(pl.program_id(0) == 0)
        def _():
            dgsb_ref[...] = jnp.zeros_like(dgsb_ref)
            dgsw_ref[...] = jnp.zeros_like(dgsw_ref)

        dm = _dot_nt(d_ref[...], w_ref[...])
        dsb, dgsb = _rms_bwd(sb_ref[...], gsb_ref[...], dm[:, :SB_WIDTH])
        dsw, dgsw = _rms_bwd(sw_ref[...], gsw_ref[...], dm[:, SB_WIDTH:])
        dsb_ref[...] = dsb.astype(dsb_ref.dtype)
        dsw_ref[...] = dsw.astype(dsw_ref.dtype)
        dgsb_ref[...] += dgsb
        dgsw_ref[...] += dgsw

    return pl.pallas_call(
        body, name="mix_bwd", grid=(S // tm,),
        in_specs=[_rows(tm, D_MODEL), _fixed(D_MODEL, D_MODEL), _rows(tm, SB_WIDTH), _rows(tm, SWA_WIDTH),
                  _fixed(1, SB_WIDTH), _fixed(1, SWA_WIDTH)],
        out_specs=[_rows(tm, SB_WIDTH), _rows(tm, SWA_WIDTH), _fixed(1, SB_WIDTH), _fixed(1, SWA_WIDTH)],
        out_shape=[_sds((S, SB_WIDTH), _MXU), _sds((S, SWA_WIDTH), _MXU), _sds((1, SB_WIDTH), F32),
                   _sds((1, SWA_WIDTH), F32)],
        compiler_params=_cp(("arbitrary",)),
    )(dp1b, w_out, sb, sw, gsb, gsw)


def _in_proj_bwd(dproj, w_in, dp1, x, g):
    S = x.shape[0]
    N = dproj.shape[1]
    tm = min(S, 512)

    def body(dpj_ref, w_ref, d1_ref, x_ref, g_ref, gx_ref, dg_ref, db_ref):
        @pl.when(pl.program_id(0) == 0)
        def _():
            dg_ref[...] = jnp.zeros_like(dg_ref)
            db_ref[...] = jnp.zeros_like(db_ref)

        dh0 = _dot_nt(dpj_ref[...], w_ref[...]) + ALPHA * d1_ref[...]
        xhat, rstd = _ln_hat(x_ref[...])
        gx_ref[...] = _ln_bwd(xhat, rstd, dh0, g_ref[...])
        dg_ref[...] += _colsum(dh0 * xhat)
        db_ref[...] += _colsum(dh0)

    vec = _fixed(1, D_MODEL)
    return pl.pallas_call(
        body, name="in_proj_bwd", grid=(S // tm,),
        in_specs=[_rows(tm, N), _fixed(D_MODEL, N), _rows(tm, D_MODEL), _rows(tm, D_MODEL), vec],
        out_specs=[_rows(tm, D_MODEL), vec, vec],
        out_shape=[_sds((S, D_MODEL), F32), _sds((1, D_MODEL), F32), _sds((1, D_MODEL), F32)],
        compiler_params=_cp(("arbitrary",)),
    )(dproj, w_in, dp1, x, g)


def _matmul_tn(a, b, name, tk, tn):
    T, K = a.shape
    N = b.shape[1]
    tt = min(T, 512)

    def body(a_ref, b_ref, o_ref):
        @pl.when(pl.program_id(2) == 0)
        def _():
            o_ref[...] = jnp.zeros_like(o_ref)

        o_ref[...] += _dot_tn(a_ref[...], b_ref[...])

    return pl.pallas_call(
        body, name=name, grid=(K // tk, N // tn, T // tt),
        in_specs=[pl.BlockSpec((tt, tk), lambda k, n, t: (t, k)), pl.BlockSpec((tt, tn), lambda k, n, t: (t, n))],
        out_specs=pl.BlockSpec((tk, tn), lambda k, n, t: (k, n)),
        out_shape=_sds((K, N), F32),
        compiler_params=_cp(("parallel", "parallel", "arbitrary")),
    )(a, b)


def _matmul_tn_pair(a, b0, b1, name):
    T, K = a.shape
    tt = min(T, 512)

    def body(a_ref, b0_ref, b1_ref, o_ref):
        n = pl.program_id(0)

        @pl.when(pl.program_id(1) == 0)
        def _():
            o_ref[...] = jnp.zeros_like(o_ref)

        @pl.when(n < 2)
        def _():
            o_ref[0] += _dot_tn(a_ref[...], b0_ref[...])

        @pl.when(n >= 2)
        def _():
            o_ref[0] += _dot_tn(a_ref[...], b1_ref[...])

    return pl.pallas_call(
        body, name=name, grid=(4, T // tt),
        in_specs=[pl.BlockSpec((tt, K), lambda n, t: (t, 0)),
                  pl.BlockSpec((tt, FF_CHUNK), lambda n, t: (t, jnp.minimum(n, 1))),
                  pl.BlockSpec((tt, FF_CHUNK), lambda n, t: (t, jnp.maximum(n - 2, 0)))],
        out_specs=pl.BlockSpec((1, K, FF_CHUNK), lambda n, t: (n, 0, 0)),
        out_shape=_sds((4, K, FF_CHUNK), F32),
        compiler_params=_cp(("parallel", "arbitrary")),
    )(a, b0, b1)


def _sb_logs(zt, causal):
    e = jnp.exp(-jnp.abs(zt))
    lb = jnp.minimum(zt, 0.0) - jnp.log(1.0 + e)
    l1m = lb - zt
    if causal is not None:
        l1m = jnp.where(causal, l1m, 0.0)
    return lb, l1m


def _sb_weights(lb, suf, causal):
    a = jnp.exp(lb + suf)
    if causal is not None:
        a = jnp.where(causal, a, 0.0)
    return a


def _tri_masks(t):
    r = lax.broadcasted_iota(jnp.int32, (t, t), 0)
    c = lax.broadcasted_iota(jnp.int32, (t, t), 1)
    return r, c


def _sb_fwd(qT, kb, vTb):
    Hh, _, S = qT.shape
    nk, T = kb.shape[1], kb.shape[2]
    nq = S // T
    G = SB_GROUP_FWD

    def body(qT_ref, k_ref, vT_ref, oT_ref, rs_ref):
        i = pl.program_id(1)
        qts = [(qT_ref[g].astype(F32) * SCALE).astype(_MXU) for g in range(G)]
        r, c = _tri_masks(T)
        upper = (c > r).astype(_MXU)
        causal = r < c

        def blk(j, carry, mask):
            hs = range(G)
            for g in hs:
                rs_ref[g, 0, j] = jnp.broadcast_to(carry[g][0], (8, T))
            zs = [_dot(k_ref[g, j], qts[g]) for g in hs]
            lbs, l1ms = zip(*[_sb_logs(zs[g], mask) for g in hs])
            splits = [_split2(l1ms[g]) for g in hs]
            cums = [_dot(upper, splits[g][0]) + _dot(upper, splits[g][1]) for g in hs]
            avs = [_sb_weights(lbs[g], carry[g][0] + cums[g], mask).astype(_MXU) for g in hs]
            accs = [carry[g][1] + _dot(vT_ref[g, j], avs[g]) for g in hs]
            return tuple((carry[g][0] + _colsum(l1ms[g]), accs[g]) for g in hs)

        def go_on(j, carry):
            top = carry[0][0]
            for g in range(1, G):
                top = jnp.maximum(top, carry[g][0])
            return jnp.logical_and(j >= 0, jnp.max(top) >= SB_DEAD)

        init = tuple((jnp.zeros((1, T), F32), jnp.zeros((HEAD_DIM, T), F32)) for _ in range(G))
        carry = blk(i, init, causal)
        j, carry = lax.while_loop(lambda st: go_on(*st), lambda st: (st[0] - 1, blk(st[0], st[1], None)),
                                  (i - 1, carry))

        @pl.when(j >= 0)
        def _():
            for g in range(G):
                rs_ref[g, 0, j] = jnp.broadcast_to(carry[g][0], (8, T))

        for g in range(G):
            oT_ref[g] = carry[g][1]

    return pl.pallas_call(
        body, name="sb_fwd", grid=(Hh // G, nq),
        in_specs=[pl.BlockSpec((G, HEAD_DIM, T), lambda h, i: (h, 0, i)),
                  pl.BlockSpec((G, nk, T, HEAD_DIM), lambda h, i: (h, 0, 0, 0), pipeline_mode=pl.Buffered(1)),
                  pl.BlockSpec((G, nk, HEAD_DIM, T), lambda h, i: (h, 0, 0, 0), pipeline_mode=pl.Buffered(1))],
        out_specs=[pl.BlockSpec((G, HEAD_DIM, T), lambda h, i: (h, 0, i)),
                   pl.BlockSpec((G, 1, nk, 8, T), lambda h, i: (h, i, 0, 0, 0))],
        out_shape=[_sds((Hh, HEAD_DIM, S), F32), _sds((Hh, nq, nk, 8, T), F32)],
        compiler_params=_cp(("parallel", "arbitrary")),
    )(qT, kb, vTb)


def _sb_bwd(qT, kb, kTb, vb, doT, rsave):
    Hh, _, S = qT.shape
    nk, T = kb.shape[1], kb.shape[2]
    nq = S // T
    G = SB_GROUP_BWD

    def body(qT_ref, k_ref, kT_ref, v_ref, doT_ref, rs_ref, dqT_ref, dk_ref, dv_ref):
        i = pl.program_id(1)

        @pl.when(i == 0)
        def _():
            dk_ref[...] = jnp.zeros_like(dk_ref)
            dv_ref[...] = jnp.zeros_like(dv_ref)

        qts = [(qT_ref[g].astype(F32) * SCALE).astype(_MXU) for g in range(G)]
        douts = [doT_ref[g] for g in range(G)]
        r, c = _tri_masks(T)
        upper = (c > r).astype(_MXU)
        lower = (c < r).astype(_MXU)
        causal = r < c

        def blk(j, carry, mask):
            hs = range(G)
            zs = [_dot(k_ref[g, j], qts[g]) for g in hs]
            das = [_dot(v_ref[g, j], douts[g]) for g in hs]
            lbs, l1ms = zip(*[_sb_logs(zs[g], mask) for g in hs])
            splits = [_split2(l1ms[g]) for g in hs]
            cums = [_dot(upper, splits[g][0]) + _dot(upper, splits[g][1]) for g in hs]
            avs = [_sb_weights(lbs[g], rs_ref[g, 0, j][0:1, :] + cums[g], mask) for g in hs]
            ets = [das[g] * avs[g] for g in hs]
            esplits = [_split2(ets[g]) for g in hs]
            ecums = [_dot(lower, esplits[g][0]) + _dot(lower, esplits[g][1]) for g in hs]
            dzs = []
            for g in hs:
                sig = jnp.exp(lbs[g])
                dz = ets[g] * (1.0 - sig) - (carry[g][0] + ecums[g]) * sig
                if mask is not None:
                    dz = jnp.where(mask, dz, 0.0)
                dzs.append(dz.astype(_MXU))
            dqs = [carry[g][1] + _dot(kT_ref[g, j], dzs[g]) for g in hs]
            for g in hs:
                dk_ref[g, j] += _dot_nt(qts[g], dzs[g])
            for g in hs:
                dv_ref[g, j] += _dot_nt(douts[g], avs[g].astype(_MXU))
            return tuple((carry[g][0] + _colsum(ets[g]), dqs[g]) for g in hs)

        def live(j):
            jj = jnp.maximum(j, 0)
            top = rs_ref[0, 0, jj][0:1, :]
            for g in range(1, G):
                top = jnp.maximum(top, rs_ref[g, 0, jj][0:1, :])
            return jnp.logical_and(j >= 0, jnp.max(top) >= SB_DEAD)

        first = lax.while_loop(lambda st: st[1], lambda st: (st[0] - 1, live(st[0] - 2)), (i, live(i - 1)))[0]
        carry = tuple((jnp.zeros((1, T), F32), jnp.zeros((HEAD_DIM, T), F32)) for _ in range(G))
        carry = lax.fori_loop(first, i, lambda s, cr: blk(s, cr, None), carry)
        carry = blk(i, carry, causal)
        for g in range(G):
            dqT_ref[g] = carry[g][1] * SCALE

    colblk = pl.BlockSpec((G, HEAD_DIM, T), lambda h, i: (h, 0, i))
    once = pl.Buffered(1)
    kblk = pl.BlockSpec((G, nk, T, HEAD_DIM), lambda h, i: (h, 0, 0, 0), pipeline_mode=once)
    kTblk = pl.BlockSpec((G, nk, HEAD_DIM, T), lambda h, i: (h, 0, 0, 0), pipeline_mode=once)
    return pl.pallas_call(
        body, name="sb_bwd", grid=(Hh // G, nq),
        in_specs=[colblk, kblk, kTblk, kblk, colblk,
                  pl.BlockSpec((G, 1, nk, 8, T), lambda h, i: (h, i, 0, 0, 0))],
        out_specs=[colblk, kTblk, kTblk],
        out_shape=[_sds((Hh, HEAD_DIM, S), F32), _sds((Hh, nk, HEAD_DIM, T), F32), _sds((Hh, nk, HEAD_DIM, T), F32)],
        compiler_params=_cp(("parallel", "arbitrary"), vmem_mb=56),
    )(qT, kb, kTb, vb, doT, rsave)


def _bucket_table():
    qi = np.arange(BLOCK)[:, None]
    cj = np.arange(2 * BLOCK)[None, :]
    dist = qi + BLOCK - cj
    exact = REL_BUCKETS // 2
    d = np.maximum(dist, 0)
    d_f = np.maximum(d, 1).astype(np.float32)
    large = exact + (np.log(d_f / np.float32(exact)) / np.float32(math.log(REL_MAX_DIST / exact))
                     * np.float32(REL_BUCKETS - exact)).astype(np.int32)
    large = np.minimum(large, REL_BUCKETS - 1)
    return np.where(d < exact, d, large).astype(np.int32)


def _swa_bias(rel_bias, bucket):
    def body(rb_ref, bk_ref, o_ref):
        bk = bk_ref[...]
        for h in range(SWA_HEADS):
            t = jnp.zeros((2 * BLOCK, BLOCK), F32)
            for b in range(REL_BUCKETS):
                t = jnp.where(bk == b, rb_ref[b, h], t)
            o_ref[h] = t

    return pl.pallas_call(
        body, name="swa_bias",
        in_specs=[pl.BlockSpec(memory_space=pltpu.SMEM), pl.BlockSpec(memory_space=pltpu.VMEM)],
        out_specs=pl.BlockSpec(memory_space=pltpu.VMEM),
        out_shape=_sds((SWA_HEADS, 2 * BLOCK, BLOCK), F32),
    )(rel_bias, bucket)


def _swa_logits(q, kp, kc):
    qs = (q.astype(F32) * SCALE).astype(_MXU)
    return qs, _dot_nt(kp, qs), _dot_nt(kc, qs)


def _swa_softmax(lp, lc, bias, sink, live_prev):
    r, c = _tri_masks(BLOCK)
    in_window = r > c if live_prev is None else jnp.logical_and(r > c, live_prev)
    lp = jnp.where(in_window, lp + bias[:BLOCK, :], -jnp.inf)
    lc = jnp.where(r <= c, lc + bias[BLOCK:, :], -jnp.inf)
    m = jnp.maximum(jnp.maximum(jnp.max(lp, axis=0, keepdims=True), jnp.max(lc, axis=0, keepdims=True)), sink)
    pp = jnp.exp(lp - m)
    pc = jnp.exp(lc - m)
    ps = jnp.exp(sink - m)
    denom = _colsum(pp) + _colsum(pc) + ps
    return pp / denom, pc / denom, ps / denom


def _swa_sub(nb):
    return min(SWA_SUB, nb)


def _swa_keys(b, prev_ref, cur_ref, i):
    cur = cur_ref[0, b * BLOCK:(b + 1) * BLOCK, :]
    if b == 0:
        return prev_ref[0], cur, i > 0
    return cur_ref[0, (b - 1) * BLOCK:b * BLOCK, :], cur, None


def _swa_keys_t(b, prev_ref, cur_ref):
    cur = cur_ref[0, :, b * BLOCK:(b + 1) * BLOCK]
    return (prev_ref[0] if b == 0 else cur_ref[0, :, (b - 1) * BLOCK:b * BLOCK]), cur


def _swa_fwd(q, k, vT, bias, sink):
    S = q.shape[1]
    nb = S // BLOCK
    ns = _swa_sub(nb)
    R = ns * BLOCK

    def body(q_ref, kp_ref, kc_ref, vp_ref, vc_ref, bias_ref, sink_ref, o_ref):
        i = pl.program_id(1)
        bias = bias_ref[0]
        sink = sink_ref[0][:, :1]
        subs = range(ns)
        keys = [_swa_keys(b, kp_ref, kc_ref, i) for b in subs]
        vals = [_swa_keys_t(b, vp_ref, vc_ref) for b in subs]
        logits = [_swa_logits(q_ref[0, b * BLOCK:(b + 1) * BLOCK, :], keys[b][0], keys[b][1]) for b in subs]
        ws = [_swa_softmax(logits[b][1], logits[b][2], bias, sink, keys[b][2]) for b in subs]
        for b in subs:
            o_ref[0, :, b * BLOCK:(b + 1) * BLOCK] = (_dot(vals[b][0], ws[b][0].astype(_MXU))
                                                      + _dot(vals[b][1], ws[b][1].astype(_MXU)))

    prev = pl.BlockSpec((1, BLOCK, HEAD_DIM), lambda h, i: (h // SWA_GROUP, jnp.maximum(i * ns - 1, 0), 0))
    cur = pl.BlockSpec((1, R, HEAD_DIM), lambda h, i: (h // SWA_GROUP, i, 0))
    prev_t = pl.BlockSpec((1, HEAD_DIM, BLOCK), lambda h, i: (h // SWA_GROUP, 0, jnp.maximum(i * ns - 1, 0)))
    cur_t = pl.BlockSpec((1, HEAD_DIM, R), lambda h, i: (h // SWA_GROUP, 0, i))
    return pl.pallas_call(
        body, name="swa_fwd", grid=(SWA_HEADS, nb // ns),
        in_specs=[pl.BlockSpec((1, R, HEAD_DIM), lambda h, i: (h, i, 0)), prev, cur, prev_t, cur_t,
                  pl.BlockSpec((1, 2 * BLOCK, BLOCK), lambda h, i: (h, 0, 0)),
                  pl.BlockSpec((1, 1, BLOCK), lambda h, i: (h, 0, 0))],
        out_specs=pl.BlockSpec((1, HEAD_DIM, R), lambda h, i: (h, 0, i)),
        out_shape=_sds((SWA_HEADS, HEAD_DIM, S), F32),
        compiler_params=_cp(("parallel", "parallel")),
    )(q, k, k, vT, vT, bias, sink)


def _swa_bwd(q, k, kT, v, bias, sink, do):
    S = q.shape[1]
    nb = S // BLOCK
    ns = _swa_sub(nb)
    R = ns * BLOCK

    def body(q_ref, kp_ref, kc_ref, ktp_ref, ktc_ref, vp_ref, vc_ref, bias_ref, sink_ref, do_ref, dq_ref, dk_ref, dv_ref,
             dbias_ref, dsink_ref):
        g = pl.program_id(1)
        i = pl.program_id(2)

        @pl.when(jnp.logical_and(g == 0, i == 0))
        def _():
            dk_ref[...] = jnp.zeros_like(dk_ref)
            dv_ref[...] = jnp.zeros_like(dv_ref)

        @pl.when(i == 0)
        def _():
            dbias_ref[...] = jnp.zeros_like(dbias_ref)
            dsink_ref[...] = jnp.zeros_like(dsink_ref)

        bias = bias_ref[0]
        sink = sink_ref[0][:, :1]
        subs = range(ns)
        rows = [slice(b * BLOCK, (b + 1) * BLOCK) for b in subs]
        keys = [_swa_keys(b, kp_ref, kc_ref, i) for b in subs]
        keys_t = [_swa_keys_t(b, ktp_ref, ktc_ref) for b in subs]
        vals = [_swa_keys(b, vp_ref, vc_ref, i) for b in subs]
        douts = [do_ref[0, rows[b], :] for b in subs]
        logits = [_swa_logits(q_ref[0, rows[b], :], keys[b][0], keys[b][1]) for b in subs]
        dws = [(_dot_nt(vals[b][0], douts[b]), _dot_nt(vals[b][1], douts[b])) for b in subs]
        dbp = jnp.zeros((BLOCK, BLOCK), F32)
        dbc = jnp.zeros((BLOCK, BLOCK), F32)
        dsk = jnp.zeros((1, BLOCK), F32)
        wts, dls = [], []
        for b in subs:
            wp, wc, ws = _swa_softmax(logits[b][1], logits[b][2], bias, sink, keys[b][2])
            dwp, dwc = dws[b]
            delta = _colsum(wp * dwp) + _colsum(wc * dwc)
            dlp = wp * (dwp - delta)
            dlc = wc * (dwc - delta)
            dbp += dlp
            dbc += dlc
            dsk -= ws * delta
            wts.append((wp.astype(_MXU), wc.astype(_MXU)))
            dls.append((dlp.astype(_MXU), dlc.astype(_MXU)))
        for b in subs:
            dq_ref[0, :, rows[b]] = (_dot(keys_t[b][0], dls[b][0]) + _dot(keys_t[b][1], dls[b][1])) * SCALE
        for b in subs:
            qs = logits[b][0]
            blk = i * ns + b
            dk_ref[0, blk] += _dot(dls[b][1], qs)
            dv_ref[0, blk] += _dot(wts[b][1], douts[b])
            if b == 0:
                @pl.when(i > 0)
                def _():
                    dk_ref[0, blk - 1] += _dot(dls[0][0], qs)
                    dv_ref[0, blk - 1] += _dot(wts[0][0], douts[0])
            else:
                dk_ref[0, blk - 1] += _dot(dls[b][0], qs)
                dv_ref[0, blk - 1] += _dot(wts[b][0], douts[b])
        dbias_ref[0, :BLOCK, :] += dbp
        dbias_ref[0, BLOCK:, :] += dbc
        dsink_ref[0] += jnp.broadcast_to(dsk, (8, BLOCK))

    hq = lambda kv, g, i: kv * SWA_GROUP + g
    prev = pl.BlockSpec((1, BLOCK, HEAD_DIM), lambda kv, g, i: (kv, jnp.maximum(i * ns - 1, 0), 0))
    cur = pl.BlockSpec((1, R, HEAD_DIM), lambda kv, g, i: (kv, i, 0))
    prev_t = pl.BlockSpec((1, HEAD_DIM, BLOCK), lambda kv, g, i: (kv, 0, jnp.maximum(i * ns - 1, 0)))
    cur_t = pl.BlockSpec((1, HEAD_DIM, R), lambda kv, g, i: (kv, 0, i))
    qblk = pl.BlockSpec((1, R, HEAD_DIM), lambda kv, g, i: (hq(kv, g, i), i, 0))
    qblk_t = pl.BlockSpec((1, HEAD_DIM, R), lambda kv, g, i: (hq(kv, g, i), 0, i))
    kvacc = pl.BlockSpec((1, nb, BLOCK, HEAD_DIM), lambda kv, g, i: (kv, 0, 0, 0))
    return pl.pallas_call(
        body, name="swa_bwd", grid=(SWA_KV_HEADS, SWA_GROUP, nb // ns),
        in_specs=[qblk, prev, cur, prev_t, cur_t, prev, cur,
                  pl.BlockSpec((1, 2 * BLOCK, BLOCK), lambda kv, g, i: (hq(kv, g, i), 0, 0)),
                  pl.BlockSpec((1, 1, BLOCK), lambda kv, g, i: (hq(kv, g, i), 0, 0)), qblk],
        out_specs=[qblk_t, kvacc, kvacc,
                   pl.BlockSpec((1, 2 * BLOCK, BLOCK), lambda kv, g, i: (hq(kv, g, i), 0, 0)),
                   pl.BlockSpec((1, 8, BLOCK), lambda kv, g, i: (hq(kv, g, i), 0, 0))],
        out_shape=[_sds((SWA_HEADS, HEAD_DIM, S), F32), _sds((SWA_KV_HEADS, nb, BLOCK, HEAD_DIM), F32),
                   _sds((SWA_KV_HEADS, nb, BLOCK, HEAD_DIM), F32), _sds((SWA_HEADS, 2 * BLOCK, BLOCK), F32),
                   _sds((SWA_HEADS, 8, BLOCK), F32)],
        compiler_params=_cp(("arbitrary", "arbitrary", "arbitrary")),
    )(q, k, k, kT, kT, v, v, bias, sink, do)


def _swa_small_grads(dbias, dsink, bucket):
    rows = REL_BUCKETS + 8

    def total(x):
        return jnp.sum(jnp.sum(x, axis=1, keepdims=True), axis=0, keepdims=True)

    def body(db_ref, ds_ref, bk_ref, o_ref):
        bk = bk_ref[...]
        r = lax.broadcasted_iota(jnp.int32, (rows, BLOCK), 0)
        c = lax.broadcasted_iota(jnp.int32, (rows, BLOCK), 1)
        out = jnp.zeros((rows, BLOCK), F32)
        for h in range(SWA_HEADS):
            db = db_ref[h]
            for b in range(REL_BUCKETS):
                s = total(jnp.where(bk == b, db, 0.0))
                out = jnp.where(jnp.logical_and(r == b, c == h), s, out)
            s = jnp.sum(ds_ref[h][0:1, :], axis=1, keepdims=True)
            out = jnp.where(jnp.logical_and(r == REL_BUCKETS, c == h), s, out)
        o_ref[...] = out

    vm = pl.BlockSpec(memory_space=pltpu.VMEM)
    return pl.pallas_call(body, name="swa_small_grads", in_specs=[vm, vm, vm], out_specs=vm,
                          out_shape=_sds((rows, BLOCK), F32))(dbias, dsink, bucket)


def _tile_rows(n):
    for t in (512, 352, 256, 176, 128, 64, 32, 16, 8):
        if n % t == 0:
            return t
    return n


def _cast_rows(x, dtype, name):
    R, C = x.shape
    tr = _tile_rows(R)

    def body(x_ref, o_ref):
        o_ref[...] = x_ref[...].astype(o_ref.dtype)

    return pl.pallas_call(body, name=name, grid=(R // tr,), in_specs=[_rows(tr, C)], out_specs=_rows(tr, C),
                          out_shape=_sds((R, C), dtype), compiler_params=_cp(("parallel",)))(x)


def _pair_sum(g, recv, c, name):
    n, half, C = recv.shape
    tr = _tile_rows(half)

    def body(c_ref, a_ref, b_ref, o_ref):
        o_ref[...] = (a_ref[0] + b_ref[...]).astype(o_ref.dtype)

    return pl.pallas_call(
        body, name=name,
        grid_spec=pltpu.PrefetchScalarGridSpec(
            num_scalar_prefetch=1, grid=(n, half // tr),
            in_specs=[pl.BlockSpec((1, 1, tr, C), lambda j, i, c_ref: (j, c_ref[0], i, 0)),
                      pl.BlockSpec((1, tr, C), lambda j, i, c_ref: (j, i, 0))],
            out_specs=pl.BlockSpec((1, tr, C), lambda j, i, c_ref: (j, i, 0))),
        out_shape=_sds((n, half, C), _MXU),
        compiler_params=_cp(("parallel", "parallel")))(c.reshape(1), g.reshape(n, 2, half, C), recv)


def _chip_sum(own, recv, me, name):
    n, R, C = recv.shape
    tr = _tile_rows(R)

    def body(me_ref, own_ref, recv_ref, o_ref):
        acc = None
        for j in range(n):
            term = jnp.where(me_ref[0] == j, own_ref[0], recv_ref[j]).astype(F32)
            acc = term if acc is None else acc + term
        o_ref[...] = acc

    return pl.pallas_call(
        body, name=name,
        grid_spec=pltpu.PrefetchScalarGridSpec(
            num_scalar_prefetch=1, grid=(R // tr,),
            in_specs=[pl.BlockSpec((1, tr, C), lambda i, me_ref: (me_ref[0], i, 0)),
                      pl.BlockSpec((n, tr, C), lambda i, me_ref: (0, i, 0))],
            out_specs=pl.BlockSpec((tr, C), lambda i, me_ref: (i, 0))),
        out_shape=_sds((R, C), F32), compiler_params=_cp(("parallel",)))(me.reshape(1), own, recv)


def _adamw_math(w, g, m, v):
    m = ADAM_B1 * m + (1.0 - ADAM_B1) * g
    v = ADAM_B2 * v + (1.0 - ADAM_B2) * (g * g)
    m_hat = m / (1.0 - ADAM_B1 ** ADAM_STEP)
    v_hat = v / (1.0 - ADAM_B2 ** ADAM_STEP)
    delta = -ADAM_LR * (m_hat / (jnp.sqrt(v_hat) + ADAM_EPS) + ADAM_WD * w)
    return delta, m, v


def _adamw(w, g, m, v, name):
    R, C = w.shape
    tr = _tile_rows(R)

    def body(w_ref, g_ref, m_ref, v_ref, d_ref, nm_ref, nv_ref):
        d, nm, nv = _adamw_math(w_ref[...], g_ref[...], m_ref[...], v_ref[...])
        d_ref[...] = d
        nm_ref[...] = nm
        nv_ref[...] = nv

    blk = _rows(tr, C)
    return pl.pallas_call(body, name=name, grid=(R // tr,), in_specs=[blk] * 4, out_specs=[blk] * 3,
                          out_shape=[_sds((R, C), F32)] * 3, compiler_params=_cp(("parallel",)))(w, g, m, v)


def _place():
    x, y, c = lax.axis_index("x"), lax.axis_index("y"), lax.axis_index("c")
    chips = [(1 - x, y), (x, 1 - y), (1 - x, 1 - y)]
    return x, y, c, chips


def _gather_weights(shards):
    nw = len(shards)

    def body(*refs):
        in_refs, out_refs = refs[:nw], refs[nw:2 * nw]
        send_sems, recv_sems = refs[2 * nw:]
        x, y, c, chips = _place()
        me, sibling = (x, y, c), (x, y, 1 - c)

        def copy(w, k, chip, hc, to, src=None):
            part = out_refs[w].at[2 * chip[0] + chip[1], hc]
            return pltpu.make_async_remote_copy(
                src_ref=part if src is None else src, dst_ref=part, send_sem=send_sems.at[w, k],
                recv_sem=recv_sems.at[w, k], device_id=to, device_id_type=MESH)

        ws = range(nw)
        first = [copy(w, j, (x, y), c, (*chip, c), src=in_refs[w].at[c]) for w in ws for j, chip in enumerate(chips)]
        for cp in first:
            cp.start()
        passed = []
        for w in ws:
            for j, chip in enumerate(chips):
                copy(w, j, chip, c, me).wait_recv()
                passed.append(copy(w, 3 + j, chip, c, sibling))
                passed[-1].start()
        for w in ws:
            for j, chip in enumerate(chips):
                copy(w, 3 + j, chip, 1 - c, me).wait_recv()
        for cp in first + passed:
            cp.wait_send()

    any_spec = pl.BlockSpec(memory_space=pl.ANY)
    halves = [(s.shape[0] // 2, s.shape[1]) for s in shards]
    outs = pl.pallas_call(
        body, name="gather_weights", in_specs=[any_spec] * nw, out_specs=[any_spec] * nw,
        out_shape=[_sds((N_CHIPS, 2, h, cols), s.dtype) for s, (h, cols) in zip(shards, halves)],
        scratch_shapes=[pltpu.SemaphoreType.DMA((nw, 6)), pltpu.SemaphoreType.DMA((nw, 6))],
    )(*[s.reshape(2, h, cols) for s, (h, cols) in zip(shards, halves)])
    me = 2 * lax.axis_index("x") + lax.axis_index("y")
    return [lax.dynamic_update_slice_in_dim(o.reshape(N_CHIPS, 2 * h, cols), s[None], me, axis=0)
            for o, s, (h, cols) in zip(outs, shards, halves)]


def _swap_halves(grads):
    nw = len(grads)

    def body(*refs):
        g_refs, out_refs = refs[:nw], refs[nw:2 * nw]
        send_sems, recv_sems = refs[2 * nw:]
        x, y, c, _ = _place()
        cps = []
        for w in range(nw):
            half = out_refs[w].shape[1]
            theirs = g_refs[w].at[:, pl.ds(pl.multiple_of((1 - c) * half, 8), half), :]
            cps.append(pltpu.make_async_remote_copy(
                src_ref=theirs, dst_ref=out_refs[w], send_sem=send_sems.at[w], recv_sem=recv_sems.at[w],
                device_id=(x, y, 1 - c), device_id_type=MESH))
        for cp in cps:
            cp.start()
        for cp in cps:
            cp.wait()

    any_spec = pl.BlockSpec(memory_space=pl.ANY)
    return pl.pallas_call(
        body, name="swap_halves", in_specs=[any_spec] * nw, out_specs=[any_spec] * nw,
        out_shape=[_sds((g.shape[0], g.shape[1] // 2, g.shape[2]), g.dtype) for g in grads],
        scratch_shapes=[pltpu.SemaphoreType.DMA((nw,)), pltpu.SemaphoreType.DMA((nw,))],
    )(*grads)


def _scatter_partials(parts):
    nw = len(parts)

    def body(*refs):
        p_refs, out_refs = refs[:nw], refs[nw:2 * nw]
        send_sems, recv_sems = refs[2 * nw:]
        x, y, c, chips = _place()
        me = 2 * x + y
        ws = range(nw)

        def copy(w, j, chip, src_chip, dst_chip):
            return pltpu.make_async_remote_copy(
                src_ref=p_refs[w].at[src_chip], dst_ref=out_refs[w].at[dst_chip], send_sem=send_sems.at[w, j],
                recv_sem=recv_sems.at[w, j], device_id=(*chip, c), device_id_type=MESH)

        sends = [copy(w, j, chip, 2 * chip[0] + chip[1], me) for w in ws for j, chip in enumerate(chips)]
        for cp in sends:
            cp.start()
        for w in ws:
            for j, chip in enumerate(chips):
                copy(w, j, chip, me, 2 * chip[0] + chip[1]).wait_recv()
        for cp in sends:
            cp.wait_send()

    any_spec = pl.BlockSpec(memory_space=pl.ANY)
    return pl.pallas_call(
        body, name="scatter_partials", in_specs=[any_spec] * nw, out_specs=[any_spec] * nw,
        out_shape=[_sds(p.shape, p.dtype) for p in parts],
        scratch_shapes=[pltpu.SemaphoreType.DMA((nw, 3)), pltpu.SemaphoreType.DMA((nw, 3))],
    )(*parts)


def _join_halves(sums):
    nw = len(sums)

    def body(*refs):
        f_refs, out_refs = refs[:nw], refs[nw:2 * nw]
        send_sems, recv_sems = refs[2 * nw:]
        x, y, c, _ = _place()
        ws = range(nw)

        def copy(w, half_index):
            return pltpu.make_async_remote_copy(
                src_ref=f_refs[w], dst_ref=out_refs[w].at[half_index], send_sem=send_sems.at[w],
                recv_sem=recv_sems.at[w], device_id=(x, y, 1 - c), device_id_type=MESH)

        sends = [copy(w, c) for w in ws]
        for cp in sends:
            cp.start()
        for w in ws:
            copy(w, 1 - c).wait_recv()
        for cp in sends:
            cp.wait_send()

    any_spec = pl.BlockSpec(memory_space=pl.ANY)
    outs = pl.pallas_call(
        body, name="join_halves", in_specs=[any_spec] * nw, out_specs=[any_spec] * nw,
        out_shape=[_sds((2,) + f.shape, f.dtype) for f in sums],
        scratch_shapes=[pltpu.SemaphoreType.DMA((nw,)), pltpu.SemaphoreType.DMA((nw,))],
    )(*sums)
    c = lax.axis_index("c")
    return [lax.dynamic_update_slice_in_dim(o, f[None], c, axis=0).reshape(2 * f.shape[0], f.shape[1])
            for o, f in zip(outs, sums)]


def _allreduce_small(block):
    m_per, n = block.shape

    def body(x_ref, sum_ref, loss_ref, all_ref, send_sems, recv_sems, local_sem):
        x, y, c, chips = _place()
        me, sibling = (x, y, c), (x, y, 1 - c)

        def rows(px, py, pc):
            return all_ref.at[pl.ds(pl.multiple_of((4 * px + 2 * py + pc) * m_per, 8), m_per), :]

        def copy(k, blk, to, src=None):
            return pltpu.make_async_remote_copy(
                src_ref=rows(*blk) if src is None else src, dst_ref=rows(*blk), send_sem=send_sems.at[k],
                recv_sem=recv_sems.at[k], device_id=to, device_id_type=MESH)

        mine = pltpu.make_async_copy(x_ref, rows(*me), local_sem)
        mine.start()
        first = [copy(0, me, sibling, src=x_ref)]
        first += [copy(1 + j, me, (*chip, c), src=x_ref) for j, chip in enumerate(chips)]
        for cp in first:
            cp.start()
        passed = [copy(4 + j, (*chip, c), sibling) for j, chip in enumerate(chips)]
        for j, chip in enumerate(chips):
            copy(1 + j, (*chip, c), me).wait_recv()
            passed[j].start()
        copy(0, sibling, me).wait_recv()
        for j, chip in enumerate(chips):
            copy(4 + j, (*chip, 1 - c), me).wait_recv()
        for cp in first + passed:
            cp.wait_send()
        mine.wait()

        acc = all_ref[0:m_per, :]
        for d in range(1, 8):
            acc = acc + all_ref[d * m_per:(d + 1) * m_per, :]
        sum_ref[...] = acc
        tot = jnp.sum(acc[8:9, :], axis=1, keepdims=True) * (0.5 / D_MODEL)
        loss_ref[...] = jnp.broadcast_to(tot, loss_ref.shape)

    vm = pl.BlockSpec(memory_space=pltpu.VMEM)
    return pl.pallas_call(
        body, name="allreduce_small", in_specs=[vm], out_specs=[vm, vm],
        out_shape=[_sds((m_per, n), F32), _sds((8, 128), F32)],
        scratch_shapes=[pltpu.VMEM((8 * m_per, n), F32), pltpu.SemaphoreType.DMA((7,)), pltpu.SemaphoreType.DMA((7,)),
                        pltpu.SemaphoreType.DMA],
    )(block)


def _heads_rows(x, nh):
    S = x.shape[0]
    return x.reshape(S, nh, HEAD_DIM).transpose(1, 0, 2)


def _heads_cols(x, nh):
    S = x.shape[0]
    return x.reshape(S, nh, HEAD_DIM).transpose(1, 2, 0)


def _key_blocks(x, nh, t):
    S = x.shape[0]
    return x.reshape(S // t, t, nh, HEAD_DIM).transpose(2, 0, 1, 3)


def _key_blocks_t(x, nh, t):
    S = x.shape[0]
    return x.reshape(S // t, t, nh, HEAD_DIM).transpose(2, 0, 3, 1)


def _pad_row(v):
    v = v.reshape(1, -1)
    return jnp.pad(v, ((0, 0), (0, D_MODEL - v.shape[1])))


def _pack_small(ln_in_g, ln_in_b, sb_g, swa_g, sinks, rel_bias, ln1_g, ln1_b, ln2_g, ln2_b, extra):
    rows = [_pad_row(ln_in_g), _pad_row(ln_in_b), jnp.concatenate([sb_g.reshape(1, -1), swa_g.reshape(1, -1)], axis=1),
            _pad_row(jnp.concatenate([rel_bias.reshape(1, -1), sinks.reshape(1, -1)], axis=1)),
            _pad_row(ln1_g), _pad_row(ln1_b), _pad_row(ln2_g), _pad_row(ln2_b), _pad_row(extra)]
    rows.append(jnp.zeros((SMALL_ROWS - len(rows), D_MODEL), F32))
    return jnp.concatenate(rows, axis=0)


def _unpack_small(blk):
    nrb = REL_BUCKETS * SWA_HEADS
    return (blk[0], blk[1], blk[2:3, :SB_WIDTH], blk[2:3, SB_WIDTH:], blk[3:4, nrb:nrb + SWA_HEADS],
            blk[3, :nrb].reshape(REL_BUCKETS, SWA_HEADS), blk[4:5], blk[5:6], blk[6:7], blk[7:8])


def kernel(x, ln_in_g, ln_in_b, w_in, sb_norm_g, swa_norm_g, sinks, rel_bias, w_out, ln1_g, ln1_b, w_gate_up, w_down, ln2_g, ln2_b, loss_target, m_ln_in_g, m_ln_in_b, m_w_in, m_sb_norm_g, m_swa_norm_g, m_sinks, m_rel_bias, m_w_out, m_ln1_g, m_ln1_b, m_w_gate_up, m_w_down, m_ln2_g, m_ln2_b, v_ln_in_g, v_ln_in_b, v_w_in, v_sb_norm_g, v_swa_norm_g, v_sinks, v_rel_bias, v_w_out, v_ln1_g, v_ln1_b, v_w_gate_up, v_w_down, v_ln2_g, v_ln2_b):
    S = x.shape[1]
    x2 = x.reshape(S, D_MODEL)
    tgt = loss_target.reshape(S, D_MODEL)
    T = min(S, SB_TILE)
    bucket = jnp.asarray(_bucket_table().T)
    row = lambda v: v.reshape(1, -1)

    shards = [_cast_rows(w[0], _MXU, "cast_" + n) for n, w in (("w_in", w_in), ("w_out", w_out), ("w_gate_up", w_gate_up), ("w_down", w_down))]
    w_in_sh, w_out_sh, w_gu_sh, w_down_sh = _gather_weights(shards)
    w_in_f = jnp.concatenate([w_in_sh[j] for j in range(N_CHIPS)], axis=1)
    w_out_f = w_out_sh.reshape(D_MODEL, D_MODEL)
    w_down_f = w_down_sh.reshape(D_FF, D_MODEL)

    h0, h0b, proj = _ln_in_proj(x2, row(ln_in_g), row(ln_in_b), w_in_f)
    o1, o2, o3, o4, o5 = SB_WIDTH, 2 * SB_WIDTH, 3 * SB_WIDTH, 3 * SB_WIDTH + SWA_WIDTH, 3 * SB_WIDTH + SWA_WIDTH + SWA_KV_WIDTH
    q_sb, k_sb, v_sb = proj[:, :o1], proj[:, o1:o2], proj[:, o2:o3]
    q_sw, k_sw, v_sw = proj[:, o3:o4], proj[:, o4:o5], proj[:, o5:]
    qT_sb = _heads_cols(q_sb, SB_HEADS)
    kb_sb = _key_blocks(k_sb, SB_HEADS, T)
    oT_sb, rsave = _sb_fwd(qT_sb, kb_sb, _key_blocks_t(v_sb, SB_HEADS, T))
    sb_out = oT_sb.transpose(2, 0, 1).reshape(S, SB_WIDTH)

    bias = _swa_bias(rel_bias, bucket)
    sink_rows = jnp.broadcast_to(sinks.reshape(SWA_HEADS, 1, 1), (SWA_HEADS, 1, BLOCK))
    qh_sw, kh_sw, vh_sw = _heads_rows(q_sw, SWA_HEADS), _heads_rows(k_sw, SWA_KV_HEADS), _heads_rows(v_sw, SWA_KV_HEADS)
    oT_sw = _swa_fwd(qh_sw, kh_sw, _heads_cols(v_sw, SWA_KV_HEADS), bias, sink_rows)
    swa_out = oT_sw.transpose(2, 0, 1).reshape(S, SWA_WIDTH)

    pre1, merged = _mix_out(sb_out, swa_out, sb_norm_g, swa_norm_g, w_out_f, h0)
    h1b, gate, up, act = _ffn_up(pre1, ln1_g, ln1_b, w_gu_sh)
    dp2, dp2b, dg2, db2, errsum = _ffn_down_loss(act, w_down_f, pre1, ln1_g, ln1_b, ln2_g, ln2_b, tgt)

    g_w_down = _matmul_tn(act, dp2b, "grad_w_down", FF_CHUNK, D_MODEL)
    dgate, dup = _ffn_down_bwd(dp2b, w_down_f, gate, up)
    g_w_gu = _matmul_tn_pair(h1b, dgate, dup, "grad_w_gate_up")
    dp1, dp1b, dg1, db1 = _ffn_up_bwd(dgate, dup, w_gu_sh, dp2, pre1, ln1_g)
    g_w_out = _matmul_tn(merged, dp1b, "grad_w_out", D_MODEL, D_MODEL)
    dsb, dsw, dgsb, dgsw = _mix_bwd(dp1b, w_out_f, sb_out, swa_out, sb_norm_g, swa_norm_g)

    dqT_sw, dkh_sw, dvh_sw, dbias, dsink = _swa_bwd(qh_sw, kh_sw, _heads_cols(k_sw, SWA_KV_HEADS), vh_sw, bias, sink_rows,
                                                    _heads_rows(dsw, SWA_HEADS))
    swa_small = _swa_small_grads(dbias, dsink, bucket)
    dqT_sb, dkT_sb, dvT_sb = _sb_bwd(qT_sb, kb_sb, _key_blocks_t(k_sb, SB_HEADS, T), _key_blocks(v_sb, SB_HEADS, T),
                                     _heads_cols(dsb, SB_HEADS), rsave)
    tok = lambda t, nh: t.reshape(nh, S, HEAD_DIM).transpose(1, 0, 2).reshape(S, nh * HEAD_DIM)
    tokT = lambda t: t.transpose(1, 3, 0, 2).reshape(S, SB_WIDTH)
    dproj = jnp.concatenate([dqT_sb.transpose(2, 0, 1).reshape(S, SB_WIDTH), tokT(dkT_sb), tokT(dvT_sb),
                             dqT_sw.transpose(2, 0, 1).reshape(S, SWA_WIDTH), tok(dkh_sw, SWA_KV_HEADS), tok(dvh_sw, SWA_KV_HEADS)],
                            axis=1).astype(_MXU)
    g_w_in = _matmul_tn(h0b, dproj, "grad_w_in", D_MODEL, IN_COLS // 2)
    grad_x, dg_in, db_in = _in_proj_bwd(dproj, w_in_f, dp1, x2, row(ln_in_g))

    cin = IN_COLS // N_CHIPS
    grads = [jnp.stack([g_w_in[:, j * cin:(j + 1) * cin] for j in range(N_CHIPS)]),
             g_w_out.reshape(N_CHIPS, D_MODEL // N_CHIPS, D_MODEL), g_w_gu,
             g_w_down.reshape(N_CHIPS, D_FF // N_CHIPS, D_MODEL)]
    names = ("w_in", "w_out", "w_gate_up", "w_down")
    c = lax.axis_index("c").astype(jnp.int32)
    partials = [_pair_sum(g, r, c, "pair_sum_" + n) for g, r, n in zip(grads, _swap_halves(grads), names)]
    me = (2 * lax.axis_index("x") + lax.axis_index("y")).astype(jnp.int32)
    sums = [_chip_sum(p, r, me, "chip_sum_" + n) for p, r, n in zip(partials, _scatter_partials(partials), names)]
    gs_in, gs_out, gs_gu, gs_down = _join_halves(sums)

    nrb = REL_BUCKETS * SWA_HEADS
    small = _pack_small(dg_in, db_in, dgsb, dgsw, swa_small[REL_BUCKETS, :SWA_HEADS],
                        swa_small[:REL_BUCKETS, :SWA_HEADS], dg1, db1, dg2, db2, errsum)
    g_small, loss_tile = _allreduce_small(small)
    loss = loss_tile[0, 0]

    big = []
    for name, w, g, m, v in (("adamw_w_in", w_in, gs_in, m_w_in, v_w_in), ("adamw_w_out", w_out, gs_out, m_w_out, v_w_out),
                             ("adamw_w_gate_up", w_gate_up, gs_gu, m_w_gate_up, v_w_gate_up),
                             ("adamw_w_down", w_down, gs_down, m_w_down, v_w_down)):
        d, nm, nv = _adamw(w[0], g, m[0], v[0], name)
        big.append((g[None], d[None], nm[None], nv[None]))
    zero = jnp.zeros((1,), F32)
    w_small = _pack_small(ln_in_g, ln_in_b, sb_norm_g, swa_norm_g, sinks, rel_bias, ln1_g, ln1_b, ln2_g, ln2_b, zero)
    m_small = _pack_small(m_ln_in_g, m_ln_in_b, m_sb_norm_g, m_swa_norm_g, m_sinks, m_rel_bias, m_ln1_g, m_ln1_b,
                          m_ln2_g, m_ln2_b, zero)
    v_small = _pack_small(v_ln_in_g, v_ln_in_b, v_sb_norm_g, v_swa_norm_g, v_sinks, v_rel_bias, v_ln1_g, v_ln1_b,
                          v_ln2_g, v_ln2_b, zero)
    small_out = [_unpack_small(t) for t in (g_small,) + tuple(_adamw(w_small, g_small, m_small, v_small, "adamw_small"))]

    def kind(k):
        s = small_out[k]
        return [s[0], s[1], big[0][k], s[2], s[3], s[4], s[5], big[1][k], s[6], s[7], big[2][k], big[3][k], s[8], s[9]]

    return (loss, grad_x.reshape(1, S, D_MODEL), *kind(0), *kind(1), *kind(2), *kind(3))
```

```python
import functools
import math

import numpy as np
import jax
import jax.numpy as jnp
from jax import lax
from jax.experimental import pallas as pl
from jax.experimental.pallas import tpu as pltpu

F32 = jnp.float32
_MXU = jnp.bfloat16

D_MODEL = 1024
HEAD_DIM = 64
SB_HEADS = 8
SWA_HEADS = 8
SWA_KV_HEADS = 2
SWA_GROUP = SWA_HEADS // SWA_KV_HEADS
SB_WIDTH = SB_HEADS * HEAD_DIM
SWA_WIDTH = SWA_HEADS * HEAD_DIM
SWA_KV_WIDTH = SWA_KV_HEADS * HEAD_DIM
IN_COLS = 3 * SB_WIDTH + SWA_WIDTH + 2 * SWA_KV_WIDTH
BLOCK = 128
REL_BUCKETS = 32
REL_MAX_DIST = 128
D_FF = 2816
FF_CHUNK = D_FF // 2
ALPHA = 2.0 ** 0.25
LN_EPS = 1e-5
RMS_EPS = 1e-6
SCALE = HEAD_DIM ** -0.5
SB_TILE = 256
SB_GROUP_FWD = 8
SB_GROUP_BWD = 4
SB_DEAD = -105.0
SWA_SUB = 8

ADAM_LR = 0.001
ADAM_B1 = 0.9
ADAM_B2 = 0.999
ADAM_EPS = 1e-08
ADAM_WD = 0.01
ADAM_STEP = 10

N_CHIPS = 4
SMALL_ROWS = 16

MESH = pl.DeviceIdType.MESH


def _sds(shape, dtype):
    return jax.ShapeDtypeStruct(shape, dtype)


def _cp(sem=None, vmem_mb=48):
    kw = dict(vmem_limit_bytes=vmem_mb * 1024 * 1024)
    if sem is not None:
        kw["dimension_semantics"] = sem
    return pltpu.CompilerParams(**kw)


def _dot(a, b):
    return jnp.dot(a, b, preferred_element_type=F32)


def _dot_nt(a, b):
    return lax.dot_general(a, b, (((1,), (1,)), ((), ())), preferred_element_type=F32)


def _dot_tn(a, b):
    return lax.dot_general(a, b, (((0,), (0,)), ((), ())), preferred_element_type=F32)


def _ln_hat(x):
    mu = jnp.mean(x, axis=-1, keepdims=True)
    xc = x - mu
    var = jnp.mean(xc * xc, axis=-1, keepdims=True)
    rstd = lax.rsqrt(var + LN_EPS)
    return xc * rstd, rstd


def _ln_bwd(xhat, rstd, dy, g):
    dxh = dy * g
    m1 = jnp.mean(dxh, axis=-1, keepdims=True)
    m2 = jnp.mean(dxh * xhat, axis=-1, keepdims=True)
    return rstd * (dxh - m1 - xhat * m2)


def _colsum(x):
    return jnp.sum(x, axis=0, keepdims=True)


def _split2(x):
    hi = x.astype(_MXU)
    lo = (x - hi.astype(F32)).astype(_MXU)
    return hi, lo


def _rows(tm, n):
    return pl.BlockSpec((tm, n), lambda i: (i, 0))


def _fixed(*shape):
    nd = len(shape)
    return pl.BlockSpec(shape, lambda i: (0,) * nd)


def _ln_in_proj(x, g, b, w):
    S = x.shape[0]
    N = w.shape[1]
    tm = min(S, 512)

    def body(x_ref, g_ref, b_ref, w_ref, h_ref, hb_ref, p_ref):
        xhat, _ = _ln_hat(x_ref[...])
        h = xhat * g_ref[...] + b_ref[...]
        h_ref[...] = h
        hb = h.astype(_MXU)
        hb_ref[...] = hb
        p_ref[...] = _dot(hb, w_ref[...]).astype(p_ref.dtype)

    return pl.pallas_call(
        body, name="ln_in_proj", grid=(S // tm,),
        in_specs=[_rows(tm, D_MODEL), _fixed(1, D_MODEL), _fixed(1, D_MODEL), _fixed(D_MODEL, N)],
        out_specs=[_rows(tm, D_MODEL), _rows(tm, D_MODEL), _rows(tm, N)],
        out_shape=[_sds((S, D_MODEL), F32), _sds((S, D_MODEL), _MXU), _sds((S, N), _MXU)],
        compiler_params=_cp(("parallel",)),
    )(x, g, b, w)


def _rms(x, g):
    r = lax.rsqrt(jnp.mean(x * x, axis=-1, keepdims=True) + RMS_EPS)
    return x * r * g, r


def _mix_out(sb, sw, gsb, gsw, w_out, h0):
    S = sb.shape[0]
    tm = min(S, 512)

    def body(sb_ref, sw_ref, gsb_ref, gsw_ref, w_ref, h0_ref, pre_ref, mg_ref):
        ysb, _ = _rms(sb_ref[...], gsb_ref[...])
        ysw, _ = _rms(sw_ref[...], gsw_ref[...])
        ysb = ysb.astype(_MXU)
        ysw = ysw.astype(_MXU)
        mg_ref[:, :SB_WIDTH] = ysb
        mg_ref[:, SB_WIDTH:] = ysw
        mix = _dot(ysb, w_ref[:SB_WIDTH, :]) + _dot(ysw, w_ref[SB_WIDTH:, :])
        pre_ref[...] = ALPHA * h0_ref[...] + mix

    return pl.pallas_call(
        body, name="mix_out", grid=(S // tm,),
        in_specs=[_rows(tm, SB_WIDTH), _rows(tm, SWA_WIDTH), _fixed(1, SB_WIDTH), _fixed(1, SWA_WIDTH),
                  _fixed(D_MODEL, D_MODEL), _rows(tm, D_MODEL)],
        out_specs=[_rows(tm, D_MODEL), _rows(tm, D_MODEL)],
        out_shape=[_sds((S, D_MODEL), F32), _sds((S, D_MODEL), _MXU)],
        compiler_params=_cp(("parallel",)),
    )(sb, sw, gsb, gsw, w_out, h0)


def _sigmoid(x):
    return 1.0 / (1.0 + jnp.exp(-x))


def _ffn_up(pre1, g1, b1, wgu):
    S = pre1.shape[0]
    tm = min(S, 512)

    def body(p_ref, g_ref, b_ref, wg_ref, wu_ref, h1_ref, gate_ref, up_ref, a_ref):
        xhat, _ = _ln_hat(p_ref[...])
        h1 = (xhat * g_ref[...] + b_ref[...]).astype(_MXU)
        h1_ref[...] = h1
        gate = _dot(h1, wg_ref[0])
        up = _dot(h1, wu_ref[0])
        gate_ref[...] = gate.astype(gate_ref.dtype)
        up_ref[...] = up.astype(up_ref.dtype)
        a_ref[...] = (gate * _sigmoid(gate) * up).astype(a_ref.dtype)

    chunk = lambda i, j: (i, j)
    return pl.pallas_call(
        body, name="ffn_up", grid=(S // tm, 2),
        in_specs=[pl.BlockSpec((tm, D_MODEL), lambda i, j: (i, 0)),
                  pl.BlockSpec((1, D_MODEL), lambda i, j: (0, 0)),
                  pl.BlockSpec((1, D_MODEL), lambda i, j: (0, 0)),
                  pl.BlockSpec((1, D_MODEL, FF_CHUNK), lambda i, j: (j, 0, 0)),
                  pl.BlockSpec((1, D_MODEL, FF_CHUNK), lambda i, j: (j + 2, 0, 0))],
        out_specs=[pl.BlockSpec((tm, D_MODEL), lambda i, j: (i, 0)),
                   pl.BlockSpec((tm, FF_CHUNK), chunk), pl.BlockSpec((tm, FF_CHUNK), chunk),
                   pl.BlockSpec((tm, FF_CHUNK), chunk)],
        out_shape=[_sds((S, D_MODEL), _MXU), _sds((S, D_FF), _MXU), _sds((S, D_FF), _MXU), _sds((S, D_FF), _MXU)],
        compiler_params=_cp(("parallel", "arbitrary")),
    )(pre1, g1, b1, wgu, wgu)


def _ffn_down_loss(a, w_down, pre1, g1, b1, g2, b2, tgt):
    S = a.shape[0]
    tm = min(S, 512)

    def body(a_ref, w_ref, p_ref, g1_ref, b1_ref, g2_ref, b2_ref, t_ref, d_ref, db_ref, dg2_ref, db2_ref, err_ref):
        @pl.when(pl.program_id(0) == 0)
        def _():
            dg2_ref[...] = jnp.zeros_like(dg2_ref)
            db2_ref[...] = jnp.zeros_like(db2_ref)
            err_ref[...] = jnp.zeros_like(err_ref)

        xhat1, _ = _ln_hat(p_ref[...])
        h1 = xhat1 * g1_ref[...] + b1_ref[...]
        pre2 = ALPHA * h1 + _dot(a_ref[...], w_ref[...])
        xhat2, rstd2 = _ln_hat(pre2)
        err = xhat2 * g2_ref[...] + b2_ref[...] - t_ref[...]
        dh2 = err * (1.0 / D_MODEL)
        dp2 = _ln_bwd(xhat2, rstd2, dh2, g2_ref[...])
        d_ref[...] = dp2
        db_ref[...] = dp2.astype(db_ref.dtype)
        dg2_ref[...] += _colsum(dh2 * xhat2)
        db2_ref[...] += _colsum(dh2)
        err_ref[...] += _colsum(err * err)

    vec = _fixed(1, D_MODEL)
    return pl.pallas_call(
        body, name="ffn_down_loss", grid=(S // tm,),
        in_specs=[_rows(tm, D_FF), _fixed(D_FF, D_MODEL), _rows(tm, D_MODEL), vec, vec, vec, vec, _rows(tm, D_MODEL)],
        out_specs=[_rows(tm, D_MODEL), _rows(tm, D_MODEL), vec, vec, vec],
        out_shape=[_sds((S, D_MODEL), F32), _sds((S, D_MODEL), _MXU), _sds((1, D_MODEL), F32), _sds((1, D_MODEL), F32),
                   _sds((1, D_MODEL), F32)],
        compiler_params=_cp(("arbitrary",)),
    )(a, w_down, pre1, g1, b1, g2, b2, tgt)


def _ffn_down_bwd(dp2b, w_down, gate, up):
    S = dp2b.shape[0]
    tm = min(S, 512)

    def body(d_ref, w_ref, g_ref, u_ref, dg_ref, du_ref):
        da = _dot_nt(d_ref[...], w_ref[...])
        g = g_ref[...].astype(F32)
        u = u_ref[...].astype(F32)
        sg = _sigmoid(g)
        du_ref[...] = (da * g * sg).astype(du_ref.dtype)
        dg_ref[...] = (da * u * (sg * (1.0 + g * (1.0 - sg)))).astype(dg_ref.dtype)

    chunk = pl.BlockSpec((tm, FF_CHUNK), lambda i, j: (i, j))
    return pl.pallas_call(
        body, name="ffn_down_bwd", grid=(S // tm, 2),
        in_specs=[pl.BlockSpec((tm, D_MODEL), lambda i, j: (i, 0)),
                  pl.BlockSpec((FF_CHUNK, D_MODEL), lambda i, j: (j, 0)), chunk, chunk],
        out_specs=[chunk, chunk],
        out_shape=[_sds((S, D_FF), _MXU), _sds((S, D_FF), _MXU)],
        compiler_params=_cp(("parallel", "arbitrary")),
    )(dp2b, w_down, gate, up)


def _ffn_up_bwd(dgate, dup, wgu, dp2, pre1, g1):
    S = dgate.shape[0]
    tm = min(S, 256)

    def body(dg_ref, du_ref, w_ref, d2_ref, p_ref, g_ref, d1_ref, d1b_ref, dg1_ref, db1_ref):
        @pl.when(pl.program_id(0) == 0)
        def _():
            dg1_ref[...] = jnp.zeros_like(dg1_ref)
            db1_ref[...] = jnp.zeros_like(db1_ref)

        dh1 = ALPHA * d2_ref[...]
        for j in range(2):
            cols = slice(j * FF_CHUNK, (j + 1) * FF_CHUNK)
            dh1 += _dot_nt(dg_ref[:, cols], w_ref[j])
            dh1 += _dot_nt(du_ref[:, cols], w_ref[j + 2])
        xhat, rstd = _ln_hat(p_ref[...])
        dp1 = _ln_bwd(xhat, rstd, dh1, g_ref[...])
        d1_ref[...] = dp1
        d1b_ref[...] = dp1.astype(d1b_ref.dtype)
        dg1_ref[...] += _colsum(dh1 * xhat)
        db1_ref[...] += _colsum(dh1)

    vec = _fixed(1, D_MODEL)
    return pl.pallas_call(
        body, name="ffn_up_bwd", grid=(S // tm,),
        in_specs=[_rows(tm, D_FF), _rows(tm, D_FF), _fixed(4, D_MODEL, FF_CHUNK), _rows(tm, D_MODEL),
                  _rows(tm, D_MODEL), vec],
        out_specs=[_rows(tm, D_MODEL), _rows(tm, D_MODEL), vec, vec],
        out_shape=[_sds((S, D_MODEL), F32), _sds((S, D_MODEL), _MXU), _sds((1, D_MODEL), F32), _sds((1, D_MODEL), F32)],
        compiler_params=_cp(("arbitrary",), vmem_mb=56),
    )(dgate, dup, wgu, dp2, pre1, g1)


def _rms_bwd(x, g, dy):
    n = x.shape[-1]
    r = lax.rsqrt(jnp.mean(x * x, axis=-1, keepdims=True) + RMS_EPS)
    u = dy * g
    dx = r * u - x * (r * r * r) * (jnp.sum(u * x, axis=-1, keepdims=True) * (1.0 / n))
    return dx, _colsum(dy * x * r)


def _mix_bwd(dp1b, w_out, sb, sw, gsb, gsw):
    S = sb.shape[0]
    tm = min(S, 512)

    def body(d_ref, w_ref, sb_ref, sw_ref, gsb_ref, gsw_ref, dsb_ref, dsw_ref, dgsb_ref, dgsw_ref):
        @pl.when(pl.program_id(0) == 0)
        def _():
            dgsb_ref[...] = jnp.zeros_like(dgsb_ref)
            dgsw_ref[...] = jnp.zeros_like(dgsw_ref)

        dm = _dot_nt(d_ref[...], w_ref[...])
        dsb, dgsb = _rms_bwd(sb_ref[...], gsb_ref[...], dm[:, :SB_WIDTH])
        dsw, dgsw = _rms_bwd(sw_ref[...], gsw_ref[...], dm[:, SB_WIDTH:])
        dsb_ref[...] = dsb.astype(dsb_ref.dtype)
        dsw_ref[...] = dsw.astype(dsw_ref.dtype)
        dgsb_ref[...] += dgsb
        dgsw_ref[...] += dgsw

    return pl.pallas_call(
        body, name="mix_bwd", grid=(S // tm,),
        in_specs=[_rows(tm, D_MODEL), _fixed(D_MODEL, D_MODEL), _rows(tm, SB_WIDTH), _rows(tm, SWA_WIDTH),
                  _fixed(1, SB_WIDTH), _fixed(1, SWA_WIDTH)],
        out_specs=[_rows(tm, SB_WIDTH), _rows(tm, SWA_WIDTH), _fixed(1, SB_WIDTH), _fixed(1, SWA_WIDTH)],
        out_shape=[_sds((S, SB_WIDTH), _MXU), _sds((S, SWA_WIDTH), _MXU), _sds((1, SB_WIDTH), F32),
                   _sds((1, SWA_WIDTH), F32)],
        compiler_params=_cp(("arbitrary",)),
    )(dp1b, w_out, sb, sw, gsb, gsw)


def _in_proj_bwd(dproj, w_in, dp1, x, g):
    S = x.shape[0]
    N = dproj.shape[1]
    tm = min(S, 512)

    def body(dpj_ref, w_ref, d1_ref, x_ref, g_ref, gx_ref, dg_ref, db_ref):
        @pl.when(pl.program_id(0) == 0)
        def _():
            dg_ref[...] = jnp.zeros_like(dg_ref)
            db_ref[...] = jnp.zeros_like(db_ref)

        dh0 = _dot_nt(dpj_ref[...], w_ref[...]) + ALPHA * d1_ref[...]
        xhat, rstd = _ln_hat(x_ref[...])
        gx_ref[...] = _ln_bwd(xhat, rstd, dh0, g_ref[...])
        dg_ref[...] += _colsum(dh0 * xhat)
        db_ref[...] += _colsum(dh0)

    vec = _fixed(1, D_MODEL)
    return pl.pallas_call(
        body, name="in_proj_bwd", grid=(S // tm,),
        in_specs=[_rows(tm, N), _fixed(D_MODEL, N), _rows(tm, D_MODEL), _rows(tm, D_MODEL), vec],
        out_specs=[_rows(tm, D_MODEL), vec, vec],
        out_shape=[_sds((S, D_MODEL), F32), _sds((1, D_MODEL), F32), _sds((1, D_MODEL), F32)],
        compiler_params=_cp(("arbitrary",)),
    )(dproj, w_in, dp1, x, g)


def _matmul_tn(a, b, name, tk, tn):
    T, K = a.shape
    N = b.shape[1]
    tt = min(T, 512)

    def body(a_ref, b_ref, o_ref):
        @pl.when(pl.program_id(2) == 0)
        def _():
            o_ref[...] = jnp.zeros_like(o_ref)

        o_ref[...] += _dot_tn(a_ref[...], b_ref[...])

    return pl.pallas_call(
        body, name=name, grid=(K // tk, N // tn, T // tt),
        in_specs=[pl.BlockSpec((tt, tk), lambda k, n, t: (t, k)), pl.BlockSpec((tt, tn), lambda k, n, t: (t, n))],
        out_specs=pl.BlockSpec((tk, tn), lambda k, n, t: (k, n)),
        out_shape=_sds((K, N), F32),
        compiler_params=_cp(("parallel", "parallel", "arbitrary")),
    )(a, b)


def _place():
    x, y, c = lax.axis_index("x"), lax.axis_index("y"), lax.axis_index("c")
    chips = [(1 - x, y), (x, 1 - y), (1 - x, 1 - y)]
    return x, y, c, chips


class _Gather:
    def __init__(self, in_refs, out_refs, send_sems, recv_sems):
        self.in_refs, self.out_refs, self.send_sems, self.recv_sems = in_refs, out_refs, send_sems, recv_sems
        self.x, self.y, self.c, self.chips = _place()

    def _copy(self, w, k, chip, hc, to, src=None):
        part = self.out_refs[w].at[2 * chip[0] + chip[1], hc]
        return pltpu.make_async_remote_copy(
            src_ref=part if src is None else src, dst_ref=part, send_sem=self.send_sems.at[w, k],
            recv_sem=self.recv_sems.at[w, k], device_id=to, device_id_type=MESH)

    def _first(self):
        x, y, c = self.x, self.y, self.c
        return [self._copy(w, j, (x, y), c, (*chip, c), src=self.in_refs[w].at[c])
                for w in range(len(self.in_refs)) for j, chip in enumerate(self.chips)]

    def start(self):
        for cp in self._first():
            cp.start()

    def finish(self):
        x, y, c = self.x, self.y, self.c
        me, sibling = (x, y, c), (x, y, 1 - c)
        ws = range(len(self.in_refs))
        passed = []
        for w in ws:
            for j, chip in enumerate(self.chips):
                self._copy(w, j, chip, c, me).wait_recv()
                passed.append(self._copy(w, 3 + j, chip, c, sibling))
                passed[-1].start()
        for w in ws:
            for j, chip in enumerate(self.chips):
                self._copy(w, 3 + j, chip, 1 - c, me).wait_recv()
        for cp in self._first() + passed:
            cp.wait_send()


def _gather_io(shards):
    halves = [(s.shape[0] // 2, s.shape[1]) for s in shards]
    ins = [s.reshape(2, h, cols) for s, (h, cols) in zip(shards, halves)]
    outs = [_sds((N_CHIPS, 2, h, cols), s.dtype) for s, (h, cols) in zip(shards, halves)]
    sems = [pltpu.SemaphoreType.DMA((len(shards), 6)), pltpu.SemaphoreType.DMA((len(shards), 6))]
    return ins, outs, sems


def _gather_assemble(outs, shards):
    me = 2 * lax.axis_index("x") + lax.axis_index("y")
    return [lax.dynamic_update_slice_in_dim(o.reshape((N_CHIPS,) + s.shape), s[None], me, axis=0)
            for o, s in zip(outs, shards)]


class _Scatter:
    def __init__(self, p_refs, out_refs, send_sems, recv_sems):
        self.p_refs, self.out_refs, self.send_sems, self.recv_sems = p_refs, out_refs, send_sems, recv_sems
        self.x, self.y, self.c, self.chips = _place()
        self.me = 2 * self.x + self.y

    def _copy(self, w, j, chip, src_chip, dst_chip):
        return pltpu.make_async_remote_copy(
            src_ref=self.p_refs[w].at[src_chip], dst_ref=self.out_refs[w].at[dst_chip], send_sem=self.send_sems.at[w, j],
            recv_sem=self.recv_sems.at[w, j], device_id=(*chip, self.c), device_id_type=MESH)

    def _sends(self):
        return [self._copy(w, j, chip, 2 * chip[0] + chip[1], self.me)
                for w in range(len(self.p_refs)) for j, chip in enumerate(self.chips)]

    def start(self):
        for cp in self._sends():
            cp.start()

    def finish(self):
        for w in range(len(self.p_refs)):
            for j, chip in enumerate(self.chips):
                self._copy(w, j, chip, self.me, 2 * chip[0] + chip[1]).wait_recv()
        for cp in self._sends():
            cp.wait_send()


def _scatter_io(parts):
    sems = [pltpu.SemaphoreType.DMA((len(parts), 3)), pltpu.SemaphoreType.DMA((len(parts), 3))]
    return list(parts), [_sds(p.shape, p.dtype) for p in parts], sems


def _matmul_tn_pair(a, b0, b1, name):
    T, K = a.shape
    tt = min(T, 512)

    def body(a_ref, b0_ref, b1_ref, o_ref):
        n = pl.program_id(0)

        @pl.when(pl.program_id(1) == 0)
        def _():
            o_ref[...] = jnp.zeros_like(o_ref)

        @pl.when(n < 2)
        def _():
            o_ref[0] += _dot_tn(a_ref[...], b0_ref[...])

        @pl.when(n >= 2)
        def _():
            o_ref[0] += _dot_tn(a_ref[...], b1_ref[...])

    return pl.pallas_call(
        body, name=name, grid=(4, T // tt),
        in_specs=[pl.BlockSpec((tt, K), lambda n, t: (t, 0)),
                  pl.BlockSpec((tt, FF_CHUNK), lambda n, t: (t, jnp.minimum(n, 1))),
                  pl.BlockSpec((tt, FF_CHUNK), lambda n, t: (t, jnp.maximum(n - 2, 0)))],
        out_specs=pl.BlockSpec((1, K, FF_CHUNK), lambda n, t: (n, 0, 0)),
        out_shape=_sds((4, K, FF_CHUNK), F32),
        compiler_params=_cp(("parallel", "arbitrary")),
    )(a, b0, b1)


def _sb_logs(zt, causal):
    e = jnp.exp(-jnp.abs(zt))
    lb = jnp.minimum(zt, 0.0) - jnp.log(1.0 + e)
    l1m = lb - zt
    if causal is not None:
        l1m = jnp.where(causal, l1m, 0.0)
    return lb, l1m


def _sb_weights(lb, suf, causal):
    a = jnp.exp(lb + suf)
    if causal is not None:
        a = jnp.where(causal, a, 0.0)
    return a


def _tri_masks(t):
    r = lax.broadcasted_iota(jnp.int32, (t, t), 0)
    c = lax.broadcasted_iota(jnp.int32, (t, t), 1)
    return r, c


def _sb_fwd(qT, kb, vTb, shards):
    Hh, _, S = qT.shape
    nk, T = kb.shape[1], kb.shape[2]
    nq = S // T
    G = SB_GROUP_FWD
    nw = len(shards)
    g_ins, g_outs, g_sems = _gather_io(shards)

    def body(qT_ref, k_ref, vT_ref, *rest):
        oT_ref, rs_ref = rest[nw:nw + 2]
        gather = _Gather(rest[:nw], rest[nw + 2:2 * nw + 2], *rest[2 * nw + 2:])
        i = pl.program_id(1)
        first_step = jnp.logical_and(pl.program_id(0) == 0, i == 0)
        last_step = jnp.logical_and(pl.program_id(0) == pl.num_programs(0) - 1, i == pl.num_programs(1) - 1)

        @pl.when(first_step)
        def _():
            gather.start()

        qts = [(qT_ref[g].astype(F32) * SCALE).astype(_MXU) for g in range(G)]
        r, c = _tri_masks(T)
        upper = (c > r).astype(_MXU)
        causal = r < c

        def blk(j, carry, mask):
            hs = range(G)
            for g in hs:
                rs_ref[g, 0, j] = jnp.broadcast_to(carry[g][0], (8, T))
            zs = [_dot(k_ref[g, j], qts[g]) for g in hs]
            lbs, l1ms = zip(*[_sb_logs(zs[g], mask) for g in hs])
            splits = [_split2(l1ms[g]) for g in hs]
            cums = [_dot(upper, splits[g][0]) + _dot(upper, splits[g][1]) for g in hs]
            avs = [_sb_weights(lbs[g], carry[g][0] + cums[g], mask).astype(_MXU) for g in hs]
            accs = [carry[g][1] + _dot(vT_ref[g, j], avs[g]) for g in hs]
            return tuple((carry[g][0] + _colsum(l1ms[g]), accs[g]) for g in hs)

        def go_on(j, carry):
            top = carry[0][0]
            for g in range(1, G):
                top = jnp.maximum(top, carry[g][0])
            return jnp.logical_and(j >= 0, jnp.max(top) >= SB_DEAD)

        init = tuple((jnp.zeros((1, T), F32), jnp.zeros((HEAD_DIM, T), F32)) for _ in range(G))
        carry = blk(i, init, causal)
        j, carry = lax.while_loop(lambda st: go_on(*st), lambda st: (st[0] - 1, blk(st[0], st[1], None)),
                                  (i - 1, carry))

        @pl.when(j >= 0)
        def _():
            for g in range(G):
                rs_ref[g, 0, j] = jnp.broadcast_to(carry[g][0], (8, T))

        for g in range(G):
            oT_ref[g] = carry[g][1]

        @pl.when(last_step)
        def _():
            gather.finish()

    any_spec = pl.BlockSpec(memory_space=pl.ANY)
    res = pl.pallas_call(
        body, name="sb_fwd", grid=(Hh // G, nq),
        in_specs=[pl.BlockSpec((G, HEAD_DIM, T), lambda h, i: (h, 0, i)),
                  pl.BlockSpec((G, nk, T, HEAD_DIM), lambda h, i: (h, 0, 0, 0), pipeline_mode=pl.Buffered(1)),
                  pl.BlockSpec((G, nk, HEAD_DIM, T), lambda h, i: (h, 0, 0, 0), pipeline_mode=pl.Buffered(1))]
                 + [any_spec] * nw,
        out_specs=[pl.BlockSpec((G, HEAD_DIM, T), lambda h, i: (h, 0, i)),
                   pl.BlockSpec((G, 1, nk, 8, T), lambda h, i: (h, i, 0, 0, 0))] + [any_spec] * nw,
        out_shape=[_sds((Hh, HEAD_DIM, S), F32), _sds((Hh, nq, nk, 8, T), F32)] + g_outs,
        scratch_shapes=g_sems,
        compiler_params=_cp(("arbitrary", "arbitrary")),
    )(qT, kb, vTb, *g_ins)
    return res[0], res[1], _gather_assemble(res[2:], shards)


def _sb_bwd(qT, kb, kTb, vb, doT, rsave, parts):
    Hh, _, S = qT.shape
    nk, T = kb.shape[1], kb.shape[2]
    nq = S // T
    G = SB_GROUP_BWD
    nw = len(parts)
    s_ins, s_outs, s_sems = _scatter_io(parts)

    def body(qT_ref, k_ref, kT_ref, v_ref, doT_ref, rs_ref, *rest):
        dqT_ref, dk_ref, dv_ref = rest[nw:nw + 3]
        scatter = _Scatter(rest[:nw], rest[nw + 3:2 * nw + 3], *rest[2 * nw + 3:])
        i = pl.program_id(1)
        first_step = jnp.logical_and(pl.program_id(0) == 0, i == 0)
        last_step = jnp.logical_and(pl.program_id(0) == pl.num_programs(0) - 1, i == pl.num_programs(1) - 1)

        @pl.when(first_step)
        def _():
            scatter.start()

        @pl.when(i == 0)
        def _():
            dk_ref[...] = jnp.zeros_like(dk_ref)
            dv_ref[...] = jnp.zeros_like(dv_ref)

        qts = [(qT_ref[g].astype(F32) * SCALE).astype(_MXU) for g in range(G)]
        douts = [doT_ref[g] for g in range(G)]
        r, c = _tri_masks(T)
        upper = (c > r).astype(_MXU)
        lower = (c < r).astype(_MXU)
        causal = r < c

        def blk(j, carry, mask):
            hs = range(G)
            zs = [_dot(k_ref[g, j], qts[g]) for g in hs]
            das = [_dot(v_ref[g, j], douts[g]) for g in hs]
            lbs, l1ms = zip(*[_sb_logs(zs[g], mask) for g in hs])
            splits = [_split2(l1ms[g]) for g in hs]
            cums = [_dot(upper, splits[g][0]) + _dot(upper, splits[g][1]) for g in hs]
            avs = [_sb_weights(lbs[g], rs_ref[g, 0, j][0:1, :] + cums[g], mask) for g in hs]
            ets = [das[g] * avs[g] for g in hs]
            esplits = [_split2(ets[g]) for g in hs]
            ecums = [_dot(lower, esplits[g][0]) + _dot(lower, esplits[g][1]) for g in hs]
            dzs = []
            for g in hs:
                sig = jnp.exp(lbs[g])
                dz = ets[g] * (1.0 - sig) - (carry[g][0] + ecums[g]) * sig
                if mask is not None:
                    dz = jnp.where(mask, dz, 0.0)
                dzs.append(dz.astype(_MXU))
            dqs = [carry[g][1] + _dot(kT_ref[g, j], dzs[g]) for g in hs]
            for g in hs:
                dk_ref[g, j] += _dot_nt(qts[g], dzs[g])
            for g in hs:
                dv_ref[g, j] += _dot_nt(douts[g], avs[g].astype(_MXU))
            return tuple((carry[g][0] + _colsum(ets[g]), dqs[g]) for g in hs)

        def live(j):
            jj = jnp.maximum(j, 0)
            top = rs_ref[0, 0, jj][0:1, :]
            for g in range(1, G):
                top = jnp.maximum(top, rs_ref[g, 0, jj][0:1, :])
            return jnp.logical_and(j >= 0, jnp.max(top) >= SB_DEAD)

        first = lax.while_loop(lambda st: st[1], lambda st: (st[0] - 1, live(st[0] - 2)), (i, live(i - 1)))[0]
        carry = tuple((jnp.zeros((1, T), F32), jnp.zeros((HEAD_DIM, T), F32)) for _ in range(G))
        carry = lax.fori_loop(first, i, lambda s, cr: blk(s, cr, None), carry)
        carry = blk(i, carry, causal)
        for g in range(G):
            dqT_ref[g] = carry[g][1] * SCALE

        @pl.when(last_step)
        def _():
            scatter.finish()

    colblk = pl.BlockSpec((G, HEAD_DIM, T), lambda h, i: (h, 0, i))
    once = pl.Buffered(1)
    kblk = pl.BlockSpec((G, nk, T, HEAD_DIM), lambda h, i: (h, 0, 0, 0), pipeline_mode=once)
    kTblk = pl.BlockSpec((G, nk, HEAD_DIM, T), lambda h, i: (h, 0, 0, 0), pipeline_mode=once)
    any_spec = pl.BlockSpec(memory_space=pl.ANY)
    res = pl.pallas_call(
        body, name="sb_bwd", grid=(Hh // G, nq),
        in_specs=[colblk, kblk, kTblk, kblk, colblk,
                  pl.BlockSpec((G, 1, nk, 8, T), lambda h, i: (h, i, 0, 0, 0))] + [any_spec] * nw,
        out_specs=[colblk, kTblk, kTblk] + [any_spec] * nw,
        out_shape=[_sds((Hh, HEAD_DIM, S), F32), _sds((Hh, nk, HEAD_DIM, T), F32), _sds((Hh, nk, HEAD_DIM, T), F32)]
                  + s_outs,
        scratch_shapes=s_sems,
        compiler_params=_cp(("arbitrary", "arbitrary"), vmem_mb=56),
    )(qT, kb, kTb, vb, doT, rsave, *s_ins)
    return res[0], res[1], res[2], list(res[3:])


def _bucket_table():
    qi = np.arange(BLOCK)[:, None]
    cj = np.arange(2 * BLOCK)[None, :]
    dist = qi + BLOCK - cj
    exact = REL_BUCKETS // 2
    d = np.maximum(dist, 0)
    d_f = np.maximum(d, 1).astype(np.float32)
    large = exact + (np.log(d_f / np.float32(exact)) / np.float32(math.log(REL_MAX_DIST / exact))
                     * np.float32(REL_BUCKETS - exact)).astype(np.int32)
    large = np.minimum(large, REL_BUCKETS - 1)
    return np.where(d < exact, d, large).astype(np.int32)


def _swa_bias(rel_bias, bucket):
    def body(rb_ref, bk_ref, o_ref):
        bk = bk_ref[...]
        for h in range(SWA_HEADS):
            t = jnp.zeros((2 * BLOCK, BLOCK), F32)
            for b in range(REL_BUCKETS):
                t = jnp.where(bk == b, rb_ref[b, h], t)
            o_ref[h] = t

    return pl.pallas_call(
        body, name="swa_bias",
        in_specs=[pl.BlockSpec(memory_space=pltpu.SMEM), pl.BlockSpec(memory_space=pltpu.VMEM)],
        out_specs=pl.BlockSpec(memory_space=pltpu.VMEM),
        out_shape=_sds((SWA_HEADS, 2 * BLOCK, BLOCK), F32),
    )(rel_bias, bucket)


def _swa_logits(q, kp, kc):
    qs = (q.astype(F32) * SCALE).astype(_MXU)
    return qs, _dot_nt(kp, qs), _dot_nt(kc, qs)


def _swa_softmax(lp, lc, bias, sink, live_prev):
    r, c = _tri_masks(BLOCK)
    in_window = r > c if live_prev is None else jnp.logical_and(r > c, live_prev)
    lp = jnp.where(in_window, lp + bias[:BLOCK, :], -jnp.inf)
    lc = jnp.where(r <= c, lc + bias[BLOCK:, :], -jnp.inf)
    m = jnp.maximum(jnp.maximum(jnp.max(lp, axis=0, keepdims=True), jnp.max(lc, axis=0, keepdims=True)), sink)
    pp = jnp.exp(lp - m)
    pc = jnp.exp(lc - m)
    ps = jnp.exp(sink - m)
    denom = _colsum(pp) + _colsum(pc) + ps
    return pp / denom, pc / denom, ps / denom


def _swa_sub(nb):
    return min(SWA_SUB, nb)


def _swa_keys(b, prev_ref, cur_ref, i):
    cur = cur_ref[0, b * BLOCK:(b + 1) * BLOCK, :]
    if b == 0:
        return prev_ref[0], cur, i > 0
    return cur_ref[0, (b - 1) * BLOCK:b * BLOCK, :], cur, None


def _swa_keys_t(b, prev_ref, cur_ref):
    cur = cur_ref[0, :, b * BLOCK:(b + 1) * BLOCK]
    return (prev_ref[0] if b == 0 else cur_ref[0, :, (b - 1) * BLOCK:b * BLOCK]), cur


def _swa_fwd(q, k, vT, bias, sink):
    S = q.shape[1]
    nb = S // BLOCK
    ns = _swa_sub(nb)
    R = ns * BLOCK

    def body(q_ref, kp_ref, kc_ref, vp_ref, vc_ref, bias_ref, sink_ref, o_ref):
        i = pl.program_id(1)
        bias = bias_ref[0]
        sink = sink_ref[0][:, :1]
        subs = range(ns)
        keys = [_swa_keys(b, kp_ref, kc_ref, i) for b in subs]
        vals = [_swa_keys_t(b, vp_ref, vc_ref) for b in subs]
        logits = [_swa_logits(q_ref[0, b * BLOCK:(b + 1) * BLOCK, :], keys[b][0], keys[b][1]) for b in subs]
        ws = [_swa_softmax(logits[b][1], logits[b][2], bias, sink, keys[b][2]) for b in subs]
        for b in subs:
            o_ref[0, :, b * BLOCK:(b + 1) * BLOCK] = (_dot(vals[b][0], ws[b][0].astype(_MXU))
                                                      + _dot(vals[b][1], ws[b][1].astype(_MXU)))

    prev = pl.BlockSpec((1, BLOCK, HEAD_DIM), lambda h, i: (h // SWA_GROUP, jnp.maximum(i * ns - 1, 0), 0))
    cur = pl.BlockSpec((1, R, HEAD_DIM), lambda h, i: (h // SWA_GROUP, i, 0))
    prev_t = pl.BlockSpec((1, HEAD_DIM, BLOCK), lambda h, i: (h // SWA_GROUP, 0, jnp.maximum(i * ns - 1, 0)))
    cur_t = pl.BlockSpec((1, HEAD_DIM, R), lambda h, i: (h // SWA_GROUP, 0, i))
    return pl.pallas_call(
        body, name="swa_fwd", grid=(SWA_HEADS, nb // ns),
        in_specs=[pl.BlockSpec((1, R, HEAD_DIM), lambda h, i: (h, i, 0)), prev, cur, prev_t, cur_t,
                  pl.BlockSpec((1, 2 * BLOCK, BLOCK), lambda h, i: (h, 0, 0)),
                  pl.BlockSpec((1, 1, BLOCK), lambda h, i: (h, 0, 0))],
        out_specs=pl.BlockSpec((1, HEAD_DIM, R), lambda h, i: (h, 0, i)),
        out_shape=_sds((SWA_HEADS, HEAD_DIM, S), F32),
        compiler_params=_cp(("parallel", "parallel")),
    )(q, k, k, vT, vT, bias, sink)


def _swa_bwd(q, k, kT, v, bias, sink, do):
    S = q.shape[1]
    nb = S // BLOCK
    ns = _swa_sub(nb)
    R = ns * BLOCK

    def body(q_ref, kp_ref, kc_ref, ktp_ref, ktc_ref, vp_ref, vc_ref, bias_ref, sink_ref, do_ref, dq_ref, dk_ref, dv_ref,
             dbias_ref, dsink_ref):
        g = pl.program_id(1)
        i = pl.program_id(2)

        @pl.when(jnp.logical_and(g == 0, i == 0))
        def _():
            dk_ref[...] = jnp.zeros_like(dk_ref)
            dv_ref[...] = jnp.zeros_like(dv_ref)

        @pl.when(i == 0)
        def _():
            dbias_ref[...] = jnp.zeros_like(dbias_ref)
            dsink_ref[...] = jnp.zeros_like(dsink_ref)

        bias = bias_ref[0]
        sink = sink_ref[0][:, :1]
        subs = range(ns)
        rows = [slice(b * BLOCK, (b + 1) * BLOCK) for b in subs]
        keys = [_swa_keys(b, kp_ref, kc_ref, i) for b in subs]
        keys_t = [_swa_keys_t(b, ktp_ref, ktc_ref) for b in subs]
        vals = [_swa_keys(b, vp_ref, vc_ref, i) for b in subs]
        douts = [do_ref[0, rows[b], :] for b in subs]
        logits = [_swa_logits(q_ref[0, rows[b], :], keys[b][0], keys[b][1]) for b in subs]
        dws = [(_dot_nt(vals[b][0], douts[b]), _dot_nt(vals[b][1], douts[b])) for b in subs]
        dbp = jnp.zeros((BLOCK, BLOCK), F32)
        dbc = jnp.zeros((BLOCK, BLOCK), F32)
        dsk = jnp.zeros((1, BLOCK), F32)
        wts, dls = [], []
        for b in subs:
            wp, wc, ws = _swa_softmax(logits[b][1], logits[b][2], bias, sink, keys[b][2])
            dwp, dwc = dws[b]
            delta = _colsum(wp * dwp) + _colsum(wc * dwc)
            dlp = wp * (dwp - delta)
            dlc = wc * (dwc - delta)
            dbp += dlp
            dbc += dlc
            dsk -= ws * delta
            wts.append((wp.astype(_MXU), wc.astype(_MXU)))
            dls.append((dlp.astype(_MXU), dlc.astype(_MXU)))
        for b in subs:
            dq_ref[0, :, rows[b]] = (_dot(keys_t[b][0], dls[b][0]) + _dot(keys_t[b][1], dls[b][1])) * SCALE
        for b in subs:
            qs = logits[b][0]
            blk = i * ns + b
            dk_ref[0, blk] += _dot(dls[b][1], qs)
            dv_ref[0, blk] += _dot(wts[b][1], douts[b])
            if b == 0:
                @pl.when(i > 0)
                def _():
                    dk_ref[0, blk - 1] += _dot(dls[0][0], qs)
                    dv_ref[0, blk - 1] += _dot(wts[0][0], douts[0])
            else:
                dk_ref[0, blk - 1] += _dot(dls[b][0], qs)
                dv_ref[0, blk - 1] += _dot(wts[b][0], douts[b])
        dbias_ref[0, :BLOCK, :] += dbp
        dbias_ref[0, BLOCK:, :] += dbc
        dsink_ref[0] += jnp.broadcast_to(dsk, (8, BLOCK))

    hq = lambda kv, g, i: kv * SWA_GROUP + g
    prev = pl.BlockSpec((1, BLOCK, HEAD_DIM), lambda kv, g, i: (kv, jnp.maximum(i * ns - 1, 0), 0))
    cur = pl.BlockSpec((1, R, HEAD_DIM), lambda kv, g, i: (kv, i, 0))
    prev_t = pl.BlockSpec((1, HEAD_DIM, BLOCK), lambda kv, g, i: (kv, 0, jnp.maximum(i * ns - 1, 0)))
    cur_t = pl.BlockSpec((1, HEAD_DIM, R), lambda kv, g, i: (kv, 0, i))
    qblk = pl.BlockSpec((1, R, HEAD_DIM), lambda kv, g, i: (hq(kv, g, i), i, 0))
    qblk_t = pl.BlockSpec((1, HEAD_DIM, R), lambda kv, g, i: (hq(kv, g, i), 0, i))
    kvacc = pl.BlockSpec((1, nb, BLOCK, HEAD_DIM), lambda kv, g, i: (kv, 0, 0, 0))
    return pl.pallas_call(
        body, name="swa_bwd", grid=(SWA_KV_HEADS, SWA_GROUP, nb // ns),
        in_specs=[qblk, prev, cur, prev_t, cur_t, prev, cur,
                  pl.BlockSpec((1, 2 * BLOCK, BLOCK), lambda kv, g, i: (hq(kv, g, i), 0, 0)),
                  pl.BlockSpec((1, 1, BLOCK), lambda kv, g, i: (hq(kv, g, i), 0, 0)), qblk],
        out_specs=[qblk_t, kvacc, kvacc,
                   pl.BlockSpec((1, 2 * BLOCK, BLOCK), lambda kv, g, i: (hq(kv, g, i), 0, 0)),
                   pl.BlockSpec((1, 8, BLOCK), lambda kv, g, i: (hq(kv, g, i), 0, 0))],
        out_shape=[_sds((SWA_HEADS, HEAD_DIM, S), F32), _sds((SWA_KV_HEADS, nb, BLOCK, HEAD_DIM), F32),
                   _sds((SWA_KV_HEADS, nb, BLOCK, HEAD_DIM), F32), _sds((SWA_HEADS, 2 * BLOCK, BLOCK), F32),
                   _sds((SWA_HEADS, 8, BLOCK), F32)],
        compiler_params=_cp(("arbitrary", "arbitrary", "arbitrary")),
    )(q, k, k, kT, kT, v, v, bias, sink, do)


def _swa_small_grads(dbias, dsink, bucket):
    rows = REL_BUCKETS + 8

    def total(x):
        return jnp.sum(jnp.sum(x, axis=1, keepdims=True), axis=0, keepdims=True)

    def body(db_ref, ds_ref, bk_ref, o_ref):
        bk = bk_ref[...]
        r = lax.broadcasted_iota(jnp.int32, (rows, BLOCK), 0)
        c = lax.broadcasted_iota(jnp.int32, (rows, BLOCK), 1)
        out = jnp.zeros((rows, BLOCK), F32)
        for h in range(SWA_HEADS):
            db = db_ref[h]
            for b in range(REL_BUCKETS):
                s = total(jnp.where(bk == b, db, 0.0))
                out = jnp.where(jnp.logical_and(r == b, c == h), s, out)
            s = jnp.sum(ds_ref[h][0:1, :], axis=1, keepdims=True)
            out = jnp.where(jnp.logical_and(r == REL_BUCKETS, c == h), s, out)
        o_ref[...] = out

    vm = pl.BlockSpec(memory_space=pltpu.VMEM)
    return pl.pallas_call(body, name="swa_small_grads", in_specs=[vm, vm, vm], out_specs=vm,
                          out_shape=_sds((rows, BLOCK), F32))(dbias, dsink, bucket)


def _tile_rows(n):
    for t in (512, 352, 256, 176, 128, 64, 32, 16, 8):
        if n % t == 0:
            return t
    return n


def _cast_rows(x, dtype, name):
    R, C = x.shape
    tr = _tile_rows(R)

    def body(x_ref, o_ref):
        o_ref[...] = x_ref[...].astype(o_ref.dtype)

    return pl.pallas_call(body, name=name, grid=(R // tr,), in_specs=[_rows(tr, C)], out_specs=_rows(tr, C),
                          out_shape=_sds((R, C), dtype), compiler_params=_cp(("parallel",)))(x)


def _pair_sum(g, recv, c, name):
    n, half, C = recv.shape
    tr = _tile_rows(half)

    def body(c_ref, a_ref, b_ref, o_ref):
        o_ref[...] = (a_ref[0] + b_ref[...]).astype(o_ref.dtype)

    return pl.pallas_call(
        body, name=name,
        grid_spec=pltpu.PrefetchScalarGridSpec(
            num_scalar_prefetch=1, grid=(n, half // tr),
            in_specs=[pl.BlockSpec((1, 1, tr, C), lambda j, i, c_ref: (j, c_ref[0], i, 0)),
                      pl.BlockSpec((1, tr, C), lambda j, i, c_ref: (j, i, 0))],
            out_specs=pl.BlockSpec((1, tr, C), lambda j, i, c_ref: (j, i, 0))),
        out_shape=_sds((n, half, C), _MXU),
        compiler_params=_cp(("parallel", "parallel")))(c.reshape(1), g.reshape(n, 2, half, C), recv)


def _chip_sum(own, recv, me, name):
    n, R, C = recv.shape
    tr = _tile_rows(R)

    def body(me_ref, own_ref, recv_ref, o_ref):
        acc = None
        for j in range(n):
            term = jnp.where(me_ref[0] == j, own_ref[0], recv_ref[j]).astype(F32)
            acc = term if acc is None else acc + term
        o_ref[...] = acc

    return pl.pallas_call(
        body, name=name,
        grid_spec=pltpu.PrefetchScalarGridSpec(
            num_scalar_prefetch=1, grid=(R // tr,),
            in_specs=[pl.BlockSpec((1, tr, C), lambda i, me_ref: (me_ref[0], i, 0)),
                      pl.BlockSpec((n, tr, C), lambda i, me_ref: (0, i, 0))],
            out_specs=pl.BlockSpec((tr, C), lambda i, me_ref: (i, 0))),
        out_shape=_sds((R, C), F32), compiler_params=_cp(("parallel",)))(me.reshape(1), own, recv)


def _adamw_math(w, g, m, v):
    m = ADAM_B1 * m + (1.0 - ADAM_B1) * g
    v = ADAM_B2 * v + (1.0 - ADAM_B2) * (g * g)
    m_hat = m / (1.0 - ADAM_B1 ** ADAM_STEP)
    v_hat = v / (1.0 - ADAM_B2 ** ADAM_STEP)
    delta = -ADAM_LR * (m_hat / (jnp.sqrt(v_hat) + ADAM_EPS) + ADAM_WD * w)
    return delta, m, v


def _adamw(w, g, m, v, name):
    R, C = w.shape
    tr = _tile_rows(R)

    def body(w_ref, g_ref, m_ref, v_ref, d_ref, nm_ref, nv_ref):
        d, nm, nv = _adamw_math(w_ref[...], g_ref[...], m_ref[...], v_ref[...])
        d_ref[...] = d
        nm_ref[...] = nm
        nv_ref[...] = nv

    blk = _rows(tr, C)
    return pl.pallas_call(body, name=name, grid=(R // tr,), in_specs=[blk] * 4, out_specs=[blk] * 3,
                          out_shape=[_sds((R, C), F32)] * 3, compiler_params=_cp(("parallel",)))(w, g, m, v)


def _gather_weights(shards):
    nw = len(shards)
    ins, outs, sems = _gather_io(shards)

    def body(*refs):
        ex = _Gather(refs[:nw], refs[nw:2 * nw], *refs[2 * nw:])
        ex.start()
        ex.finish()

    any_spec = pl.BlockSpec(memory_space=pl.ANY)
    got = pl.pallas_call(body, name="gather_weights", in_specs=[any_spec] * nw, out_specs=[any_spec] * nw,
                         out_shape=outs, scratch_shapes=sems)(*ins)
    return _gather_assemble(got, shards)


def _swap_halves(grads, name):
    nw = len(grads)

    def body(*refs):
        g_refs, out_refs = refs[:nw], refs[nw:2 * nw]
        send_sems, recv_sems = refs[2 * nw:]
        x, y, c, _ = _place()
        cps = []
        for w in range(nw):
            half = out_refs[w].shape[1]
            theirs = g_refs[w].at[:, pl.ds(pl.multiple_of((1 - c) * half, 8), half), :]
            cps.append(pltpu.make_async_remote_copy(
                src_ref=theirs, dst_ref=out_refs[w], send_sem=send_sems.at[w], recv_sem=recv_sems.at[w],
                device_id=(x, y, 1 - c), device_id_type=MESH))
        for cp in cps:
            cp.start()
        for cp in cps:
            cp.wait()

    any_spec = pl.BlockSpec(memory_space=pl.ANY)
    return pl.pallas_call(
        body, name=name, in_specs=[any_spec] * nw, out_specs=[any_spec] * nw,
        out_shape=[_sds((g.shape[0], g.shape[1] // 2, g.shape[2]), g.dtype) for g in grads],
        scratch_shapes=[pltpu.SemaphoreType.DMA((nw,)), pltpu.SemaphoreType.DMA((nw,))],
    )(*grads)


def _scatter_partials(parts):
    nw = len(parts)
    ins, outs, sems = _scatter_io(parts)

    def body(*refs):
        ex = _Scatter(refs[:nw], refs[nw:2 * nw], *refs[2 * nw:])
        ex.start()
        ex.finish()

    any_spec = pl.BlockSpec(memory_space=pl.ANY)
    return pl.pallas_call(body, name="scatter_partials", in_specs=[any_spec] * nw, out_specs=[any_spec] * nw,
                          out_shape=outs, scratch_shapes=sems)(*ins)


def _join_halves(sums):
    nw = len(sums)

    def body(*refs):
        f_refs, out_refs = refs[:nw], refs[nw:2 * nw]
        send_sems, recv_sems = refs[2 * nw:]
        x, y, c, _ = _place()
        ws = range(nw)

        def copy(w, half_index):
            return pltpu.make_async_remote_copy(
                src_ref=f_refs[w], dst_ref=out_refs[w].at[half_index], send_sem=send_sems.at[w],
                recv_sem=recv_sems.at[w], device_id=(x, y, 1 - c), device_id_type=MESH)

        sends = [copy(w, c) for w in ws]
        for cp in sends:
            cp.start()
        for w in ws:
            copy(w, 1 - c).wait_recv()
        for cp in sends:
            cp.wait_send()

    any_spec = pl.BlockSpec(memory_space=pl.ANY)
    outs = pl.pallas_call(
        body, name="join_halves", in_specs=[any_spec] * nw, out_specs=[any_spec] * nw,
        out_shape=[_sds((2,) + f.shape, f.dtype) for f in sums],
        scratch_shapes=[pltpu.SemaphoreType.DMA((nw,)), pltpu.SemaphoreType.DMA((nw,))],
    )(*sums)
    c = lax.axis_index("c")
    return [lax.dynamic_update_slice_in_dim(o, f[None], c, axis=0).reshape(2 * f.shape[0], f.shape[1])
            for o, f in zip(outs, sums)]


def _allreduce_small(block):
    m_per, n = block.shape

    def body(x_ref, sum_ref, loss_ref, all_ref, send_sems, recv_sems, local_sem):
        x, y, c, chips = _place()
        me, sibling = (x, y, c), (x, y, 1 - c)

        def rows(px, py, pc):
            return all_ref.at[pl.ds(pl.multiple_of((4 * px + 2 * py + pc) * m_per, 8), m_per), :]

        def copy(k, blk, to, src=None):
            return pltpu.make_async_remote_copy(
                src_ref=rows(*blk) if src is None else src, dst_ref=rows(*blk), send_sem=send_sems.at[k],
                recv_sem=recv_sems.at[k], device_id=to, device_id_type=MESH)

        mine = pltpu.make_async_copy(x_ref, rows(*me), local_sem)
        mine.start()
        first = [copy(0, me, sibling, src=x_ref)]
        first += [copy(1 + j, me, (*chip, c), src=x_ref) for j, chip in enumerate(chips)]
        for cp in first:
            cp.start()
        passed = [copy(4 + j, (*chip, c), sibling) for j, chip in enumerate(chips)]
        for j, chip in enumerate(chips):
            copy(1 + j, (*chip, c), me).wait_recv()
            passed[j].start()
        copy(0, sibling, me).wait_recv()
        for j, chip in enumerate(chips):
            copy(4 + j, (*chip, 1 - c), me).wait_recv()
        for cp in first + passed:
            cp.wait_send()
        mine.wait()

        acc = all_ref[0:m_per, :]
        for d in range(1, 8):
            acc = acc + all_ref[d * m_per:(d + 1) * m_per, :]
        sum_ref[...] = acc
        tot = jnp.sum(acc[8:9, :], axis=1, keepdims=True) * (0.5 / D_MODEL)
        loss_ref[...] = jnp.broadcast_to(tot, loss_ref.shape)

    vm = pl.BlockSpec(memory_space=pltpu.VMEM)
    return pl.pallas_call(
        body, name="allreduce_small", in_specs=[vm], out_specs=[vm, vm],
        out_shape=[_sds((m_per, n), F32), _sds((8, 128), F32)],
        scratch_shapes=[pltpu.VMEM((8 * m_per, n), F32), pltpu.SemaphoreType.DMA((7,)), pltpu.SemaphoreType.DMA((7,)),
                        pltpu.SemaphoreType.DMA],
    )(block)


def _heads_rows(x, nh):
    S = x.shape[0]
    return x.reshape(S, nh, HEAD_DIM).transpose(1, 0, 2)


def _heads_cols(x, nh):
    S = x.shape[0]
    return x.reshape(S, nh, HEAD_DIM).transpose(1, 2, 0)


def _key_blocks(x, nh, t):
    S = x.shape[0]
    return x.reshape(S // t, t, nh, HEAD_DIM).transpose(2, 0, 1, 3)


def _key_blocks_t(x, nh, t):
    S = x.shape[0]
    return x.reshape(S // t, t, nh, HEAD_DIM).transpose(2, 0, 3, 1)


def _pad_row(v):
    v = v.reshape(1, -1)
    return jnp.pad(v, ((0, 0), (0, D_MODEL - v.shape[1])))


def _pack_small(ln_in_g, ln_in_b, sb_g, swa_g, sinks, rel_bias, ln1_g, ln1_b, ln2_g, ln2_b, extra):
    rows = [_pad_row(ln_in_g), _pad_row(ln_in_b), jnp.concatenate([sb_g.reshape(1, -1), swa_g.reshape(1, -1)], axis=1),
            _pad_row(jnp.concatenate([rel_bias.reshape(1, -1), sinks.reshape(1, -1)], axis=1)),
            _pad_row(ln1_g), _pad_row(ln1_b), _pad_row(ln2_g), _pad_row(ln2_b), _pad_row(extra)]
    rows.append(jnp.zeros((SMALL_ROWS - len(rows), D_MODEL), F32))
    return jnp.concatenate(rows, axis=0)


def _unpack_small(blk):
    nrb = REL_BUCKETS * SWA_HEADS
    return (blk[0], blk[1], blk[2:3, :SB_WIDTH], blk[2:3, SB_WIDTH:], blk[3:4, nrb:nrb + SWA_HEADS],
            blk[3, :nrb].reshape(REL_BUCKETS, SWA_HEADS), blk[4:5], blk[5:6], blk[6:7], blk[7:8])


def kernel(x, ln_in_g, ln_in_b, w_in, sb_norm_g, swa_norm_g, sinks, rel_bias, w_out, ln1_g, ln1_b, w_gate_up, w_down, ln2_g, ln2_b, loss_target, m_ln_in_g, m_ln_in_b, m_w_in, m_sb_norm_g, m_swa_norm_g, m_sinks, m_rel_bias, m_w_out, m_ln1_g, m_ln1_b, m_w_gate_up, m_w_down, m_ln2_g, m_ln2_b, v_ln_in_g, v_ln_in_b, v_w_in, v_sb_norm_g, v_swa_norm_g, v_sinks, v_rel_bias, v_w_out, v_ln1_g, v_ln1_b, v_w_gate_up, v_w_down, v_ln2_g, v_ln2_b):
    S = x.shape[1]
    x2 = x.reshape(S, D_MODEL)
    tgt = loss_target.reshape(S, D_MODEL)
    T = min(S, SB_TILE)
    bucket = jnp.asarray(_bucket_table().T)
    row = lambda v: v.reshape(1, -1)

    shards = [_cast_rows(w[0], _MXU, "cast_" + n) for n, w in (("w_in", w_in), ("w_out", w_out), ("w_gate_up", w_gate_up), ("w_down", w_down))]
    (w_in_sh,) = _gather_weights(shards[:1])
    w_in_f = jnp.concatenate([w_in_sh[j] for j in range(N_CHIPS)], axis=1)

    h0, h0b, proj = _ln_in_proj(x2, row(ln_in_g), row(ln_in_b), w_in_f)
    o1, o2, o3, o4, o5 = SB_WIDTH, 2 * SB_WIDTH, 3 * SB_WIDTH, 3 * SB_WIDTH + SWA_WIDTH, 3 * SB_WIDTH + SWA_WIDTH + SWA_KV_WIDTH
    q_sb, k_sb, v_sb = proj[:, :o1], proj[:, o1:o2], proj[:, o2:o3]
    q_sw, k_sw, v_sw = proj[:, o3:o4], proj[:, o4:o5], proj[:, o5:]
    qT_sb = _heads_cols(q_sb, SB_HEADS)
    kb_sb = _key_blocks(k_sb, SB_HEADS, T)
    oT_sb, rsave, (w_out_sh, w_gu_sh, w_down_sh) = _sb_fwd(qT_sb, kb_sb, _key_blocks_t(v_sb, SB_HEADS, T), shards[1:])
    w_out_f = w_out_sh.reshape(D_MODEL, D_MODEL)
    w_down_f = w_down_sh.reshape(D_FF, D_MODEL)
    sb_out = oT_sb.transpose(2, 0, 1).reshape(S, SB_WIDTH)

    bias = _swa_bias(rel_bias, bucket)
    sink_rows = jnp.broadcast_to(sinks.reshape(SWA_HEADS, 1, 1), (SWA_HEADS, 1, BLOCK))
    qh_sw, kh_sw, vh_sw = _heads_rows(q_sw, SWA_HEADS), _heads_rows(k_sw, SWA_KV_HEADS), _heads_rows(v_sw, SWA_KV_HEADS)
    oT_sw = _swa_fwd(qh_sw, kh_sw, _heads_cols(v_sw, SWA_KV_HEADS), bias, sink_rows)
    swa_out = oT_sw.transpose(2, 0, 1).reshape(S, SWA_WIDTH)

    pre1, merged = _mix_out(sb_out, swa_out, sb_norm_g, swa_norm_g, w_out_f, h0)
    h1b, gate, up, act = _ffn_up(pre1, ln1_g, ln1_b, w_gu_sh)
    dp2, dp2b, dg2, db2, errsum = _ffn_down_loss(act, w_down_f, pre1, ln1_g, ln1_b, ln2_g, ln2_b, tgt)

    g_w_down = _matmul_tn(act, dp2b, "grad_w_down", FF_CHUNK, D_MODEL)
    dgate, dup = _ffn_down_bwd(dp2b, w_down_f, gate, up)
    g_w_gu = _matmul_tn_pair(h1b, dgate, dup, "grad_w_gate_up")
    dp1, dp1b, dg1, db1 = _ffn_up_bwd(dgate, dup, w_gu_sh, dp2, pre1, ln1_g)
    g_w_out = _matmul_tn(merged, dp1b, "grad_w_out", D_MODEL, D_MODEL)
    dsb, dsw, dgsb, dgsw = _mix_bwd(dp1b, w_out_f, sb_out, swa_out, sb_norm_g, swa_norm_g)

    dqT_sw, dkh_sw, dvh_sw, dbias, dsink = _swa_bwd(qh_sw, kh_sw, _heads_cols(k_sw, SWA_KV_HEADS), vh_sw, bias, sink_rows,
                                                    _heads_rows(dsw, SWA_HEADS))
    swa_small = _swa_small_grads(dbias, dsink, bucket)
    c = lax.axis_index("c").astype(jnp.int32)
    me = (2 * lax.axis_index("x") + lax.axis_index("y")).astype(jnp.int32)
    grads_a = [g_w_out.reshape(N_CHIPS, D_MODEL // N_CHIPS, D_MODEL), g_w_gu, g_w_down.reshape(N_CHIPS, D_FF // N_CHIPS, D_MODEL)]
    names_a = ("w_out", "w_gate_up", "w_down")
    partials_a = [_pair_sum(g, r, c, "pair_sum_" + n) for g, r, n in zip(grads_a, _swap_halves(grads_a, "swap_halves_ffn"), names_a)]
    dqT_sb, dkT_sb, dvT_sb, recv_a = _sb_bwd(qT_sb, kb_sb, _key_blocks_t(k_sb, SB_HEADS, T), _key_blocks(v_sb, SB_HEADS, T),
                                             _heads_cols(dsb, SB_HEADS), rsave, partials_a)
    tok = lambda t, nh: t.reshape(nh, S, HEAD_DIM).transpose(1, 0, 2).reshape(S, nh * HEAD_DIM)
    tokT = lambda t: t.transpose(1, 3, 0, 2).reshape(S, SB_WIDTH)
    dproj = jnp.concatenate([dqT_sb.transpose(2, 0, 1).reshape(S, SB_WIDTH), tokT(dkT_sb), tokT(dvT_sb),
                             dqT_sw.transpose(2, 0, 1).reshape(S, SWA_WIDTH), tok(dkh_sw, SWA_KV_HEADS), tok(dvh_sw, SWA_KV_HEADS)],
                            axis=1).astype(_MXU)
    g_w_in = _matmul_tn(h0b, dproj, "grad_w_in", D_MODEL, IN_COLS // 2)
    grad_x, dg_in, db_in = _in_proj_bwd(dproj, w_in_f, dp1, x2, row(ln_in_g))

    cin = IN_COLS // N_CHIPS
    grads_b = [jnp.stack([g_w_in[:, j * cin:(j + 1) * cin] for j in range(N_CHIPS)])]
    partials_b = [_pair_sum(grads_b[0], _swap_halves(grads_b, "swap_halves_in")[0], c, "pair_sum_w_in")]
    recv_b = _scatter_partials(partials_b)
    names = ("w_in",) + names_a
    sums = [_chip_sum(p, r, me, "chip_sum_" + n) for p, r, n in zip(partials_b + partials_a, list(recv_b) + list(recv_a), names)]
    gs_in, gs_out, gs_gu, gs_down = _join_halves(sums)

    nrb = REL_BUCKETS * SWA_HEADS
    small = _pack_small(dg_in, db_in, dgsb, dgsw, swa_small[REL_BUCKETS, :SWA_HEADS],
                        swa_small[:REL_BUCKETS, :SWA_HEADS], dg1, db1, dg2, db2, errsum)
    g_small, loss_tile = _allreduce_small(small)
    loss = loss_tile[0, 0]

    big = []
    for name, w, g, m, v in (("adamw_w_in", w_in, gs_in, m_w_in, v_w_in), ("adamw_w_out", w_out, gs_out, m_w_out, v_w_out),
                             ("adamw_w_gate_up", w_gate_up, gs_gu, m_w_gate_up, v_w_gate_up),
                             ("adamw_w_down", w_down, gs_down, m_w_down, v_w_down)):
        d, nm, nv = _adamw(w[0], g, m[0], v[0], name)
        big.append((g[None], d[None], nm[None], nv[None]))
    zero = jnp.zeros((1,), F32)
    w_small = _pack_small(ln_in_g, ln_in_b, sb_norm_g, swa_norm_g, sinks, rel_bias, ln1_g, ln1_b, ln2_g, ln2_b, zero)
    m_small = _pack_small(m_ln_in_g, m_ln_in_b, m_sb_norm_g, m_swa_norm_g, m_sinks, m_rel_bias, m_ln1_g, m_ln1_b,
                          m_ln2_g, m_ln2_b, zero)
    v_small = _pack_small(v_ln_in_g, v_ln_in_b, v_sb_norm_g, v_swa_norm_g, v_sinks, v_rel_bias, v_ln1_g, v_ln1_b,
                          v_ln2_g, v_ln2_b, zero)
    small_out = [_unpack_small(t) for t in (g_small,) + tuple(_adamw(w_small, g_small, m_small, v_small, "adamw_small"))]

    def kind(k):
        s = small_out[k]
        return [s[0], s[1], big[0][k], s[2], s[3], s[4], s[5], big[1][k], s[6], s[7], big[2][k], big[3][k], s[8], s[9]]

    return (loss, grad_x.reshape(1, S, D_MODEL), *kind(0), *kind(1), *kind(2), *kind(3))
```

```python
import functools
import math

import numpy as np
import jax
import jax.numpy as jnp
from jax import lax
from jax.experimental import pallas as pl
from jax.experimental.pallas import tpu as pltpu

F32 = jnp.float32
_MXU = jnp.bfloat16

D_MODEL = 1024
HEAD_DIM = 64
SB_HEADS = 8
SWA_HEADS = 8
SWA_KV_HEADS = 2
SWA_GROUP = SWA_HEADS // SWA_KV_HEADS
SB_WIDTH = SB_HEADS * HEAD_DIM
SWA_WIDTH = SWA_HEADS * HEAD_DIM
SWA_KV_WIDTH = SWA_KV_HEADS * HEAD_DIM
IN_COLS = 3 * SB_WIDTH + SWA_WIDTH + 2 * SWA_KV_WIDTH
BLOCK = 128
REL_BUCKETS = 32
REL_MAX_DIST = 128
D_FF = 2816
FF_CHUNK = D_FF // 2
ALPHA = 2.0 ** 0.25
LN_EPS = 1e-5
RMS_EPS = 1e-6
SCALE = HEAD_DIM ** -0.5
SB_TILE = 256
SB_GROUP_FWD = 8
SB_GROUP_BWD = 4
SB_DEAD = -105.0
SWA_SUB = 8

ADAM_LR = 0.001
ADAM_B1 = 0.9
ADAM_B2 = 0.999
ADAM_EPS = 1e-08
ADAM_WD = 0.01
ADAM_STEP = 10

N_CHIPS = 4
SMALL_ROWS = 16

MESH = pl.DeviceIdType.MESH


def _sds(shape, dtype):
    return jax.ShapeDtypeStruct(shape, dtype)


def _cp(sem=None, vmem_mb=48):
    kw = dict(vmem_limit_bytes=vmem_mb * 1024 * 1024)
    if sem is not None:
        kw["dimension_semantics"] = sem
    return pltpu.CompilerParams(**kw)


def _dot(a, b):
    return jnp.dot(a, b, preferred_element_type=F32)


def _dot_nt(a, b):
    return lax.dot_general(a, b, (((1,), (1,)), ((), ())), preferred_element_type=F32)


def _dot_tn(a, b):
    return lax.dot_general(a, b, (((0,), (0,)), ((), ())), preferred_element_type=F32)


def _ln_hat(x):
    mu = jnp.mean(x, axis=-1, keepdims=True)
    xc = x - mu
    var = jnp.mean(xc * xc, axis=-1, keepdims=True)
    rstd = lax.rsqrt(var + LN_EPS)
    return xc * rstd, rstd


def _ln_bwd(xhat, rstd, dy, g):
    dxh = dy * g
    m1 = jnp.mean(dxh, axis=-1, keepdims=True)
    m2 = jnp.mean(dxh * xhat, axis=-1, keepdims=True)
    return rstd * (dxh - m1 - xhat * m2)


def _colsum(x):
    return jnp.sum(x, axis=0, keepdims=True)


def _split2(x):
    hi = x.astype(_MXU)
    lo = (x - hi.astype(F32)).astype(_MXU)
    return hi, lo


def _rows(tm, n):
    return pl.BlockSpec((tm, n), lambda i: (i, 0))


def _fixed(*shape):
    nd = len(shape)
    return pl.BlockSpec(shape, lambda i: (0,) * nd)


def _ln_in_proj(x, g, b, w):
    S = x.shape[0]
    N = w.shape[1]
    tm = min(S, 512)

    def body(x_ref, g_ref, b_ref, w_ref, h_ref, hb_ref, p_ref):
        xhat, _ = _ln_hat(x_ref[...])
        h = xhat * g_ref[...] + b_ref[...]
        h_ref[...] = h
        hb = h.astype(_MXU)
        hb_ref[...] = hb
        p_ref[...] = _dot(hb, w_ref[...]).astype(p_ref.dtype)

    return pl.pallas_call(
        body, name="ln_in_proj", grid=(S // tm,),
        in_specs=[_rows(tm, D_MODEL), _fixed(1, D_MODEL), _fixed(1, D_MODEL), _fixed(D_MODEL, N)],
        out_specs=[_rows(tm, D_MODEL), _rows(tm, D_MODEL), _rows(tm, N)],
        out_shape=[_sds((S, D_MODEL), F32), _sds((S, D_MODEL), _MXU), _sds((S, N), _MXU)],
        compiler_params=_cp(("parallel",)),
    )(x, g, b, w)


def _rms(x, g):
    r = lax.rsqrt(jnp.mean(x * x, axis=-1, keepdims=True) + RMS_EPS)
    return x * r * g, r


def _mix_out(sb, sw, gsb, gsw, w_out, h0):
    S = sb.shape[0]
    tm = min(S, 512)

    def body(sb_ref, sw_ref, gsb_ref, gsw_ref, w_ref, h0_ref, pre_ref, mg_ref):
        ysb, _ = _rms(sb_ref[...], gsb_ref[...])
        ysw, _ = _rms(sw_ref[...], gsw_ref[...])
        ysb = ysb.astype(_MXU)
        ysw = ysw.astype(_MXU)
        mg_ref[:, :SB_WIDTH] = ysb
        mg_ref[:, SB_WIDTH:] = ysw
        mix = _dot(ysb, w_ref[:SB_WIDTH, :]) + _dot(ysw, w_ref[SB_WIDTH:, :])
        pre_ref[...] = ALPHA * h0_ref[...] + mix

    return pl.pallas_call(
        body, name="mix_out", grid=(S // tm,),
        in_specs=[_rows(tm, SB_WIDTH), _rows(tm, SWA_WIDTH), _fixed(1, SB_WIDTH), _fixed(1, SWA_WIDTH),
                  _fixed(D_MODEL, D_MODEL), _rows(tm, D_MODEL)],
        out_specs=[_rows(tm, D_MODEL), _rows(tm, D_MODEL)],
        out_shape=[_sds((S, D_MODEL), F32), _sds((S, D_MODEL), _MXU)],
        compiler_params=_cp(("parallel",)),
    )(sb, sw, gsb, gsw, w_out, h0)


def _sigmoid(x):
    return 1.0 / (1.0 + jnp.exp(-x))


def _ffn_up(pre1, g1, b1, wgu):
    S = pre1.shape[0]
    tm = min(S, 512)

    def body(p_ref, g_ref, b_ref, wg_ref, wu_ref, gate_ref, up_ref, a_ref):
        xhat, _ = _ln_hat(p_ref[...])
        h1 = (xhat * g_ref[...] + b_ref[...]).astype(_MXU)
        gate = _dot(h1, wg_ref[0])
        up = _dot(h1, wu_ref[0])
        gate_ref[...] = gate.astype(gate_ref.dtype)
        up_ref[...] = up.astype(up_ref.dtype)
        a_ref[...] = (gate * _sigmoid(gate) * up).astype(a_ref.dtype)

    chunk = lambda j, i: (i, j)
    return pl.pallas_call(
        body, name="ffn_up", grid=(2, S // tm),
        in_specs=[pl.BlockSpec((tm, D_MODEL), lambda j, i: (i, 0)),
                  pl.BlockSpec((1, D_MODEL), lambda j, i: (0, 0)),
                  pl.BlockSpec((1, D_MODEL), lambda j, i: (0, 0)),
                  pl.BlockSpec((1, D_MODEL, FF_CHUNK), lambda j, i: (j, 0, 0)),
                  pl.BlockSpec((1, D_MODEL, FF_CHUNK), lambda j, i: (j + 2, 0, 0))],
        out_specs=[pl.BlockSpec((tm, FF_CHUNK), chunk), pl.BlockSpec((tm, FF_CHUNK), chunk),
                   pl.BlockSpec((tm, FF_CHUNK), chunk)],
        out_shape=[_sds((S, D_FF), _MXU), _sds((S, D_FF), _MXU), _sds((S, D_FF), _MXU)],
        compiler_params=_cp(("arbitrary", "arbitrary")),
    )(pre1, g1, b1, wgu, wgu)


def _ffn_down_loss(a, w_down, pre1, g1, b1, g2, b2, tgt):
    S = a.shape[0]
    tm = min(S, 512)

    def body(a_ref, w_ref, p_ref, g1_ref, b1_ref, g2_ref, b2_ref, t_ref, d_ref, db_ref, dg2_ref, db2_ref, err_ref, h1_ref):
        @pl.when(pl.program_id(0) == 0)
        def _():
            dg2_ref[...] = jnp.zeros_like(dg2_ref)
            db2_ref[...] = jnp.zeros_like(db2_ref)
            err_ref[...] = jnp.zeros_like(err_ref)

        xhat1, _ = _ln_hat(p_ref[...])
        h1 = xhat1 * g1_ref[...] + b1_ref[...]
        h1_ref[...] = h1.astype(h1_ref.dtype)
        pre2 = ALPHA * h1 + _dot(a_ref[...], w_ref[...])
        xhat2, rstd2 = _ln_hat(pre2)
        err = xhat2 * g2_ref[...] + b2_ref[...] - t_ref[...]
        dh2 = err * (1.0 / D_MODEL)
        dp2 = _ln_bwd(xhat2, rstd2, dh2, g2_ref[...])
        d_ref[...] = dp2
        db_ref[...] = dp2.astype(db_ref.dtype)
        dg2_ref[...] += _colsum(dh2 * xhat2)
        db2_ref[...] += _colsum(dh2)
        err_ref[...] += _colsum(err * err)

    vec = _fixed(1, D_MODEL)
    return pl.pallas_call(
        body, name="ffn_down_loss", grid=(S // tm,),
        in_specs=[_rows(tm, D_FF), _fixed(D_FF, D_MODEL), _rows(tm, D_MODEL), vec, vec, vec, vec, _rows(tm, D_MODEL)],
        out_specs=[_rows(tm, D_MODEL), _rows(tm, D_MODEL), vec, vec, vec, _rows(tm, D_MODEL)],
        out_shape=[_sds((S, D_MODEL), F32), _sds((S, D_MODEL), _MXU), _sds((1, D_MODEL), F32), _sds((1, D_MODEL), F32),
                   _sds((1, D_MODEL), F32), _sds((S, D_MODEL), _MXU)],
        compiler_params=_cp(("arbitrary",)),
    )(a, w_down, pre1, g1, b1, g2, b2, tgt)


def _ffn_down_bwd(dp2b, w_down, gate, up):
    S = dp2b.shape[0]
    tm = min(S, 512)

    def body(d_ref, w_ref, g_ref, u_ref, dg_ref, du_ref):
        da = _dot_nt(d_ref[...], w_ref[...])
        g = g_ref[...].astype(F32)
        u = u_ref[...].astype(F32)
        sg = _sigmoid(g)
        du_ref[...] = (da * g * sg).astype(du_ref.dtype)
        dg_ref[...] = (da * u * (sg * (1.0 + g * (1.0 - sg)))).astype(dg_ref.dtype)

    chunk = pl.BlockSpec((tm, FF_CHUNK), lambda j, i: (i, j))
    return pl.pallas_call(
        body, name="ffn_down_bwd", grid=(2, S // tm),
        in_specs=[pl.BlockSpec((tm, D_MODEL), lambda j, i: (i, 0)),
                  pl.BlockSpec((FF_CHUNK, D_MODEL), lambda j, i: (j, 0)), chunk, chunk],
        out_specs=[chunk, chunk],
        out_shape=[_sds((S, D_FF), _MXU), _sds((S, D_FF), _MXU)],
        compiler_params=_cp(("arbitrary", "arbitrary")),
    )(dp2b, w_down, gate, up)


def _ffn_up_bwd(dgate, dup, wgu, dp2, pre1, g1):
    S = dgate.shape[0]
    tm = min(S, 256)

    def body(dg_ref, du_ref, w_ref, d2_ref, p_ref, g_ref, d1_ref, d1b_ref, dg1_ref, db1_ref):
        @pl.when(pl.program_id(0) == 0)
        def _():
            dg1_ref[...] = jnp.zeros_like(dg1_ref)
            db1_ref[...] = jnp.zeros_like(db1_ref)

        dh1 = ALPHA * d2_ref[...]
        for j in range(2):
            cols = slice(j * FF_CHUNK, (j + 1) * FF_CHUNK)
            dh1 += _dot_nt(dg_ref[:, cols], w_ref[j])
            dh1 += _dot_nt(du_ref[:, cols], w_ref[j + 2])
        xhat, rstd = _ln_hat(p_ref[...])
        dp1 = _ln_bwd(xhat, rstd, dh1, g_ref[...])
        d1_ref[...] = dp1
        d1b_ref[...] = dp1.astype(d1b_ref.dtype)
        dg1_ref[...] += _colsum(dh1 * xhat)
        db1_ref[...] += _colsum(dh1)

    vec = _fixed(1, D_MODEL)
    return pl.pallas_call(
        body, name="ffn_up_bwd", grid=(S // tm,),
        in_specs=[_rows(tm, D_FF), _rows(tm, D_FF), _fixed(4, D_MODEL, FF_CHUNK), _rows(tm, D_MODEL),
                  _rows(tm, D_MODEL), vec],
        out_specs=[_rows(tm, D_MODEL), _rows(tm, D_MODEL), vec, vec],
        out_shape=[_sds((S, D_MODEL), F32), _sds((S, D_MODEL), _MXU), _sds((1, D_MODEL), F32), _sds((1, D_MODEL), F32)],
        compiler_params=_cp(("arbitrary",), vmem_mb=56),
    )(dgate, dup, wgu, dp2, pre1, g1)


def _rms_bwd(x, g, dy):
    n = x.shape[-1]
    r = lax.rsqrt(jnp.mean(x * x, axis=-1, keepdims=True) + RMS_EPS)
    u = dy * g
    dx = r * u - x * (r * r * r) * (jnp.sum(u * x, axis=-1, keepdims=True) * (1.0 / n))
    return dx, _colsum(dy * x * r)


def _mix_bwd(dp1b, w_out, sb, sw, gsb, gsw):
    S = sb.shape[0]
    tm = min(S, 512)

    def body(d_ref, w_ref, sb_ref, sw_ref, gsb_ref, gsw_ref, dsb_ref, dsw_ref, dgsb_ref, dgsw_ref):
        @pl.when(pl.program_id(0) == 0)
        def _():
            dgsb_ref[...] = jnp.zeros_like(dgsb_ref)
            dgsw_ref[...] = jnp.zeros_like(dgsw_ref)

        dm = _dot_nt(d_ref[...], w_ref[...])
        dsb, dgsb = _rms_bwd(sb_ref[...], gsb_ref[...], dm[:, :SB_WIDTH])
        dsw, dgsw = _rms_bwd(sw_ref[...], gsw_ref[...], dm[:, SB_WIDTH:])
        dsb_ref[...] = dsb.astype(dsb_ref.dtype)
        dsw_ref[...] = dsw.astype(dsw_ref.dtype)
        dgsb_ref[...] += dgsb
        dgsw_ref[...] += dgsw

    return pl.pallas_call(
        body, name="mix_bwd", grid=(S // tm,),
        in_specs=[_rows(tm, D_MODEL), _fixed(D_MODEL, D_MODEL), _rows(tm, SB_WIDTH), _rows(tm, SWA_WIDTH),
                  _fixed(1, SB_WIDTH), _fixed(1, SWA_WIDTH)],
        out_specs=[_rows(tm, SB_WIDTH), _rows(tm, SWA_WIDTH), _fixed(1, SB_WIDTH), _fixed(1, SWA_WIDTH)],
        out_shape=[_sds((S, SB_WIDTH), _MXU), _sds((S, SWA_WIDTH), _MXU), _sds((1, SB_WIDTH), F32),
                   _sds((1, SWA_WIDTH), F32)],
        compiler_params=_cp(("arbitrary",)),
    )(dp1b, w_out, sb, sw, gsb, gsw)


def _in_proj_bwd(dproj, w_in, dp1, x, g):
    S = x.shape[0]
    N = dproj.shape[1]
    tm = min(S, 512)

    def body(dpj_ref, w_ref, d1_ref, x_ref, g_ref, gx_ref, dg_ref, db_ref):
        @pl.when(pl.program_id(0) == 0)
        def _():
            dg_ref[...] = jnp.zeros_like(dg_ref)
            db_ref[...] = jnp.zeros_like(db_ref)

        dh0 = _dot_nt(dpj_ref[...], w_ref[...]) + ALPHA * d1_ref[...]
        xhat, rstd = _ln_hat(x_ref[...])
        gx_ref[...] = _ln_bwd(xhat, rstd, dh0, g_ref[...])
        dg_ref[...] += _colsum(dh0 * xhat)
        db_ref[...] += _colsum(dh0)

    vec = _fixed(1, D_MODEL)
    return pl.pallas_call(
        body, name="in_proj_bwd", grid=(S // tm,),
        in_specs=[_rows(tm, N), _fixed(D_MODEL, N), _rows(tm, D_MODEL), _rows(tm, D_MODEL), vec],
        out_specs=[_rows(tm, D_MODEL), vec, vec],
        out_shape=[_sds((S, D_MODEL), F32), _sds((1, D_MODEL), F32), _sds((1, D_MODEL), F32)],
        compiler_params=_cp(("arbitrary",)),
    )(dproj, w_in, dp1, x, g)


def _matmul_tn(a, b, name, tk, tn):
    T, K = a.shape
    N = b.shape[1]
    tt = min(T, 512)

    def body(a_ref, b_ref, o_ref):
        @pl.when(pl.program_id(2) == 0)
        def _():
            o_ref[...] = jnp.zeros_like(o_ref)

        o_ref[...] += _dot_tn(a_ref[...], b_ref[...])

    return pl.pallas_call(
        body, name=name, grid=(K // tk, N // tn, T // tt),
        in_specs=[pl.BlockSpec((tt, tk), lambda k, n, t: (t, k)), pl.BlockSpec((tt, tn), lambda k, n, t: (t, n))],
        out_specs=pl.BlockSpec((tk, tn), lambda k, n, t: (k, n)),
        out_shape=_sds((K, N), F32),
        compiler_params=_cp(("parallel", "parallel", "arbitrary")),
    )(a, b)


def _place():
    x, y, c = lax.axis_index("x"), lax.axis_index("y"), lax.axis_index("c")
    chips = [(1 - x, y), (x, 1 - y), (1 - x, 1 - y)]
    return x, y, c, chips


class _Gather:
    def __init__(self, in_refs, out_refs, send_sems, recv_sems):
        self.in_refs, self.out_refs, self.send_sems, self.recv_sems = in_refs, out_refs, send_sems, recv_sems
        self.x, self.y, self.c, self.chips = _place()

    def _copy(self, w, k, chip, hc, to, src=None):
        part = self.out_refs[w].at[2 * chip[0] + chip[1], hc]
        return pltpu.make_async_remote_copy(
            src_ref=part if src is None else src, dst_ref=part, send_sem=self.send_sems.at[w, k],
            recv_sem=self.recv_sems.at[w, k], device_id=to, device_id_type=MESH)

    def _first(self):
        x, y, c = self.x, self.y, self.c
        return [self._copy(w, j, (x, y), c, (*chip, c), src=self.in_refs[w].at[c])
                for w in range(len(self.in_refs)) for j, chip in enumerate(self.chips)]

    def start(self):
        for cp in self._first():
            cp.start()

    def finish(self):
        x, y, c = self.x, self.y, self.c
        me, sibling = (x, y, c), (x, y, 1 - c)
        ws = range(len(self.in_refs))
        passed = []
        for w in ws:
            for j, chip in enumerate(self.chips):
                self._copy(w, j, chip, c, me).wait_recv()
                passed.append(self._copy(w, 3 + j, chip, c, sibling))
                passed[-1].start()
        for w in ws:
            for j, chip in enumerate(self.chips):
                self._copy(w, 3 + j, chip, 1 - c, me).wait_recv()
        for cp in self._first() + passed:
            cp.wait_send()


def _gather_io(shards):
    halves = [(s.shape[0] // 2, s.shape[1]) for s in shards]
    ins = [s.reshape(2, h, cols) for s, (h, cols) in zip(shards, halves)]
    outs = [_sds((N_CHIPS, 2, h, cols), s.dtype) for s, (h, cols) in zip(shards, halves)]
    sems = [pltpu.SemaphoreType.DMA((len(shards), 6)), pltpu.SemaphoreType.DMA((len(shards), 6))]
    return ins, outs, sems


def _gather_assemble(outs, shards):
    me = 2 * lax.axis_index("x") + lax.axis_index("y")
    return [lax.dynamic_update_slice_in_dim(o.reshape((N_CHIPS,) + s.shape), s[None], me, axis=0)
            for o, s in zip(outs, shards)]


class _Scatter:
    def __init__(self, p_refs, out_refs, send_sems, recv_sems):
        self.p_refs, self.out_refs, self.send_sems, self.recv_sems = p_refs, out_refs, send_sems, recv_sems
        self.x, self.y, self.c, self.chips = _place()
        self.me = 2 * self.x + self.y

    def _copy(self, w, j, chip, src_chip, dst_chip):
        return pltpu.make_async_remote_copy(
            src_ref=self.p_refs[w].at[src_chip], dst_ref=self.out_refs[w].at[dst_chip], send_sem=self.send_sems.at[w, j],
            recv_sem=self.recv_sems.at[w, j], device_id=(*chip, self.c), device_id_type=MESH)

    def _sends(self):
        return [self._copy(w, j, chip, 2 * chip[0] + chip[1], self.me)
                for w in range(len(self.p_refs)) for j, chip in enumerate(self.chips)]

    def start(self):
        for cp in self._sends():
            cp.start()

    def finish(self):
        for w in range(len(self.p_refs)):
            for j, chip in enumerate(self.chips):
                self._copy(w, j, chip, self.me, 2 * chip[0] + chip[1]).wait_recv()
        for cp in self._sends():
            cp.wait_send()


def _scatter_io(parts):
    sems = [pltpu.SemaphoreType.DMA((len(parts), 3)), pltpu.SemaphoreType.DMA((len(parts), 3))]
    return list(parts), [_sds(p.shape, p.dtype) for p in parts], sems


def _matmul_tn_pair(a, b0, b1, name):
    T, K = a.shape
    tt = min(T, 512)

    def body(a_ref, b0_ref, b1_ref, o_ref):
        n = pl.program_id(0)

        @pl.when(pl.program_id(1) == 0)
        def _():
            o_ref[...] = jnp.zeros_like(o_ref)

        @pl.when(n < 2)
        def _():
            o_ref[0] += _dot_tn(a_ref[...], b0_ref[...])

        @pl.when(n >= 2)
        def _():
            o_ref[0] += _dot_tn(a_ref[...], b1_ref[...])

    return pl.pallas_call(
        body, name=name, grid=(4, T // tt),
        in_specs=[pl.BlockSpec((tt, K), lambda n, t: (t, 0)),
                  pl.BlockSpec((tt, FF_CHUNK), lambda n, t: (t, jnp.minimum(n, 1))),
                  pl.BlockSpec((tt, FF_CHUNK), lambda n, t: (t, jnp.maximum(n - 2, 0)))],
        out_specs=pl.BlockSpec((1, K, FF_CHUNK), lambda n, t: (n, 0, 0)),
        out_shape=_sds((4, K, FF_CHUNK), F32),
        compiler_params=_cp(("parallel", "arbitrary")),
    )(a, b0, b1)


def _sb_logs(zt, causal):
    e = jnp.exp(-jnp.abs(zt))
    lb = jnp.minimum(zt, 0.0) - jnp.log(1.0 + e)
    l1m = lb - zt
    if causal is not None:
        l1m = jnp.where(causal, l1m, 0.0)
    return lb, l1m


def _sb_weights(lb, suf, causal):
    a = jnp.exp(lb + suf)
    if causal is not None:
        a = jnp.where(causal, a, 0.0)
    return a


def _tri_masks(t):
    r = lax.broadcasted_iota(jnp.int32, (t, t), 0)
    c = lax.broadcasted_iota(jnp.int32, (t, t), 1)
    return r, c


def _sb_fwd(qT, kb, vTb, shards):
    Hh, _, S = qT.shape
    nk, T = kb.shape[1], kb.shape[2]
    nq = S // T
    G = SB_GROUP_FWD
    nw = len(shards)
    g_ins, g_outs, g_sems = _gather_io(shards)

    def body(qT_ref, k_ref, vT_ref, *rest):
        o_ref, rs_ref = rest[nw:nw + 2]
        gather = _Gather(rest[:nw], rest[nw + 2:2 * nw + 2], *rest[2 * nw + 2:])
        i = pl.program_id(1)
        first_step = jnp.logical_and(pl.program_id(0) == 0, i == 0)
        last_step = jnp.logical_and(pl.program_id(0) == pl.num_programs(0) - 1, i == pl.num_programs(1) - 1)

        @pl.when(first_step)
        def _():
            gather.start()

        qts = [(qT_ref[g].astype(F32) * SCALE).astype(_MXU) for g in range(G)]
        r, c = _tri_masks(T)
        upper = (c > r).astype(_MXU)
        causal = r < c

        def blk(j, carry, mask):
            hs = range(G)
            for g in hs:
                rs_ref[g, 0, j] = jnp.broadcast_to(carry[g][0], (8, T))
            zs = [_dot(k_ref[g, j], qts[g]) for g in hs]
            lbs, l1ms = zip(*[_sb_logs(zs[g], mask) for g in hs])
            splits = [_split2(l1ms[g]) for g in hs]
            cums = [_dot(upper, splits[g][0]) + _dot(upper, splits[g][1]) for g in hs]
            avs = [_sb_weights(lbs[g], carry[g][0] + cums[g], mask).astype(_MXU) for g in hs]
            accs = [carry[g][1] + _dot(vT_ref[g, j], avs[g]) for g in hs]
            return tuple((carry[g][0] + _colsum(l1ms[g]), accs[g]) for g in hs)

        def go_on(j, carry):
            top = carry[0][0]
            for g in range(1, G):
                top = jnp.maximum(top, carry[g][0])
            return jnp.logical_and(j >= 0, jnp.max(top) >= SB_DEAD)

        init = tuple((jnp.zeros((1, T), F32), jnp.zeros((HEAD_DIM, T), F32)) for _ in range(G))
        carry = blk(i, init, causal)
        j, carry = lax.while_loop(lambda st: go_on(*st), lambda st: (st[0] - 1, blk(st[0], st[1], None)),
                                  (i - 1, carry))

        @pl.when(j >= 0)
        def _():
            for g in range(G):
                rs_ref[g, 0, j] = jnp.broadcast_to(carry[g][0], (8, T))

        o_ref[...] = jnp.concatenate([carry[g][1] for g in range(G)], axis=0).T

        @pl.when(last_step)
        def _():
            gather.finish()

    any_spec = pl.BlockSpec(memory_space=pl.ANY)
    res = pl.pallas_call(
        body, name="sb_fwd", grid=(Hh // G, nq),
        in_specs=[pl.BlockSpec((G, HEAD_DIM, T), lambda h, i: (h, 0, i)),
                  pl.BlockSpec((G, nk, T, HEAD_DIM), lambda h, i: (h, 0, 0, 0), pipeline_mode=pl.Buffered(1)),
                  pl.BlockSpec((G, nk, HEAD_DIM, T), lambda h, i: (h, 0, 0, 0), pipeline_mode=pl.Buffered(1))]
                 + [any_spec] * nw,
        out_specs=[pl.BlockSpec((T, G * HEAD_DIM), lambda h, i: (i, h)),
                   pl.BlockSpec((G, 1, nk, 8, T), lambda h, i: (h, i, 0, 0, 0))] + [any_spec] * nw,
        out_shape=[_sds((S, Hh * HEAD_DIM), F32), _sds((Hh, nq, nk, 8, T), F32)] + g_outs,
        scratch_shapes=g_sems,
        compiler_params=_cp(("arbitrary", "arbitrary")),
    )(qT, kb, vTb, *g_ins)
    return res[0], res[1], _gather_assemble(res[2:], shards)


def _sb_bwd(qT, kb, kTb, vb, doT, rsave, parts):
    Hh, _, S = qT.shape
    nk, T = kb.shape[1], kb.shape[2]
    nq = S // T
    G = SB_GROUP_BWD
    nw = len(parts)
    s_ins, s_outs, s_sems = _scatter_io(parts)

    def body(qT_ref, k_ref, kT_ref, v_ref, doT_ref, rs_ref, *rest):
        dq_ref, dk_out_ref, dv_out_ref = rest[nw:nw + 3]
        dk_ref, dv_ref = rest[2 * nw + 3:2 * nw + 5]
        scatter = _Scatter(rest[:nw], rest[nw + 3:2 * nw + 3], *rest[2 * nw + 5:])
        i = pl.program_id(1)
        first_step = jnp.logical_and(pl.program_id(0) == 0, i == 0)
        last_step = jnp.logical_and(pl.program_id(0) == pl.num_programs(0) - 1, i == pl.num_programs(1) - 1)

        @pl.when(first_step)
        def _():
            scatter.start()

        @pl.when(i == 0)
        def _():
            dk_ref[...] = jnp.zeros_like(dk_ref)
            dv_ref[...] = jnp.zeros_like(dv_ref)

        qts = [(qT_ref[g].astype(F32) * SCALE).astype(_MXU) for g in range(G)]
        douts = [doT_ref[g] for g in range(G)]
        r, c = _tri_masks(T)
        upper = (c > r).astype(_MXU)
        lower = (c < r).astype(_MXU)
        causal = r < c

        def blk(j, carry, mask):
            hs = range(G)
            zs = [_dot(k_ref[g, j], qts[g]) for g in hs]
            das = [_dot(v_ref[g, j], douts[g]) for g in hs]
            lbs, l1ms = zip(*[_sb_logs(zs[g], mask) for g in hs])
            splits = [_split2(l1ms[g]) for g in hs]
            cums = [_dot(upper, splits[g][0]) + _dot(upper, splits[g][1]) for g in hs]
            avs = [_sb_weights(lbs[g], rs_ref[g, 0, j][0:1, :] + cums[g], mask) for g in hs]
            ets = [das[g] * avs[g] for g in hs]
            esplits = [_split2(ets[g]) for g in hs]
            ecums = [_dot(lower, esplits[g][0]) + _dot(lower, esplits[g][1]) for g in hs]
            dzs = []
            for g in hs:
                sig = jnp.exp(lbs[g])
                dz = ets[g] * (1.0 - sig) - (carry[g][0] + ecums[g]) * sig
                if mask is not None:
                    dz = jnp.where(mask, dz, 0.0)
                dzs.append(dz.astype(_MXU))
            dqs = [carry[g][1] + _dot(kT_ref[g, j], dzs[g]) for g in hs]
            for g in hs:
                dk_ref[j, g * HEAD_DIM:(g + 1) * HEAD_DIM, :] += _dot_nt(qts[g], dzs[g])
            for g in hs:
                dv_ref[j, g * HEAD_DIM:(g + 1) * HEAD_DIM, :] += _dot_nt(douts[g], avs[g].astype(_MXU))
            return tuple((carry[g][0] + _colsum(ets[g]), dqs[g]) for g in hs)

        def live(j):
            jj = jnp.maximum(j, 0)
            top = rs_ref[0, 0, jj][0:1, :]
            for g in range(1, G):
                top = jnp.maximum(top, rs_ref[g, 0, jj][0:1, :])
            return jnp.logical_and(j >= 0, jnp.max(top) >= SB_DEAD)

        first = lax.while_loop(lambda st: st[1], lambda st: (st[0] - 1, live(st[0] - 2)), (i, live(i - 1)))[0]
        carry = tuple((jnp.zeros((1, T), F32), jnp.zeros((HEAD_DIM, T), F32)) for _ in range(G))
        carry = lax.fori_loop(first, i, lambda s, cr: blk(s, cr, None), carry)
        carry = blk(i, carry, causal)
        dq_ref[...] = (jnp.concatenate([carry[g][1] for g in range(G)], axis=0) * SCALE).T.astype(dq_ref.dtype)

        @pl.when(i == pl.num_programs(1) - 1)
        def _():
            def flush(j, _):
                rows = pl.ds(pl.multiple_of(j * T, T), T)
                dk_out_ref[rows, :] = dk_ref[j].T.astype(dk_out_ref.dtype)
                dv_out_ref[rows, :] = dv_ref[j].T.astype(dv_out_ref.dtype)
                return 0
            lax.fori_loop(0, nk, flush, 0)

        @pl.when(last_step)
        def _():
            scatter.finish()

    colblk = pl.BlockSpec((G, HEAD_DIM, T), lambda h, i: (h, 0, i))
    once = pl.Buffered(1)
    kblk = pl.BlockSpec((G, nk, T, HEAD_DIM), lambda h, i: (h, 0, 0, 0), pipeline_mode=once)
    kTblk = pl.BlockSpec((G, nk, HEAD_DIM, T), lambda h, i: (h, 0, 0, 0), pipeline_mode=once)
    any_spec = pl.BlockSpec(memory_space=pl.ANY)
    res = pl.pallas_call(
        body, name="sb_bwd", grid=(Hh // G, nq),
        in_specs=[colblk, kblk, kTblk, kblk, colblk,
                  pl.BlockSpec((G, 1, nk, 8, T), lambda h, i: (h, i, 0, 0, 0))] + [any_spec] * nw,
        out_specs=[pl.BlockSpec((T, G * HEAD_DIM), lambda h, i: (i, h)),
                   pl.BlockSpec((S, G * HEAD_DIM), lambda h, i: (0, h), pipeline_mode=once),
                   pl.BlockSpec((S, G * HEAD_DIM), lambda h, i: (0, h), pipeline_mode=once)] + [any_spec] * nw,
        out_shape=[_sds((S, Hh * HEAD_DIM), _MXU)] * 3 + s_outs,
        scratch_shapes=[pltpu.VMEM((nk, G * HEAD_DIM, T), F32), pltpu.VMEM((nk, G * HEAD_DIM, T), F32)] + s_sems,
        compiler_params=_cp(("arbitrary", "arbitrary"), vmem_mb=60),
    )(qT, kb, kTb, vb, doT, rsave, *s_ins)
    return res[0], res[1], res[2], list(res[3:])


def _bucket_table():
    qi = np.arange(BLOCK)[:, None]
    cj = np.arange(2 * BLOCK)[None, :]
    dist = qi + BLOCK - cj
    exact = REL_BUCKETS // 2
    d = np.maximum(dist, 0)
    d_f = np.maximum(d, 1).astype(np.float32)
    large = exact + (np.log(d_f / np.float32(exact)) / np.float32(math.log(REL_MAX_DIST / exact))
                     * np.float32(REL_BUCKETS - exact)).astype(np.int32)
    large = np.minimum(large, REL_BUCKETS - 1)
    return np.where(d < exact, d, large).astype(np.int32)


def _swa_bias(rel_bias, bucket):
    def body(rb_ref, bk_ref, o_ref):
        bk = bk_ref[...]
        for h in range(SWA_HEADS):
            t = jnp.zeros((2 * BLOCK, BLOCK), F32)
            for b in range(REL_BUCKETS):
                t = jnp.where(bk == b, rb_ref[b, h], t)
            o_ref[h] = t

    return pl.pallas_call(
        body, name="swa_bias",
        in_specs=[pl.BlockSpec(memory_space=pltpu.SMEM), pl.BlockSpec(memory_space=pltpu.VMEM)],
        out_specs=pl.BlockSpec(memory_space=pltpu.VMEM),
        out_shape=_sds((SWA_HEADS, 2 * BLOCK, BLOCK), F32),
    )(rel_bias, bucket)


def _swa_logits(q, kp, kc):
    qs = (q.astype(F32) * SCALE).astype(_MXU)
    return qs, _dot_nt(kp, qs), _dot_nt(kc, qs)


def _swa_softmax(lp, lc, bias, sink, live_prev):
    r, c = _tri_masks(BLOCK)
    in_window = r > c if live_prev is None else jnp.logical_and(r > c, live_prev)
    lp = jnp.where(in_window, lp + bias[:BLOCK, :], -jnp.inf)
    lc = jnp.where(r <= c, lc + bias[BLOCK:, :], -jnp.inf)
    m = jnp.maximum(jnp.maximum(jnp.max(lp, axis=0, keepdims=True), jnp.max(lc, axis=0, keepdims=True)), sink)
    pp = jnp.exp(lp - m)
    pc = jnp.exp(lc - m)
    ps = jnp.exp(sink - m)
    denom = _colsum(pp) + _colsum(pc) + ps
    return pp / denom, pc / denom, ps / denom


def _swa_sub(nb):
    return min(SWA_SUB, nb)


def _swa_keys(b, prev_ref, cur_ref, i):
    cur = cur_ref[0, b * BLOCK:(b + 1) * BLOCK, :]
    if b == 0:
        return prev_ref[0], cur, i > 0
    return cur_ref[0, (b - 1) * BLOCK:b * BLOCK, :], cur, None


def _swa_keys_t(b, prev_ref, cur_ref):
    cur = cur_ref[0, :, b * BLOCK:(b + 1) * BLOCK]
    return (prev_ref[0] if b == 0 else cur_ref[0, :, (b - 1) * BLOCK:b * BLOCK]), cur


def _swa_fwd(q, k, vT, bias, sink):
    S = q.shape[1]
    nb = S // BLOCK
    ns = _swa_sub(nb)
    R = ns * BLOCK

    def body(q_ref, kp_ref, kc_ref, vp_ref, vc_ref, bias_ref, sink_ref, o_ref):
        i = pl.program_id(1)
        bias = bias_ref[0]
        sink = sink_ref[0][:, :1]
        subs = range(ns)
        keys = [_swa_keys(b, kp_ref, kc_ref, i) for b in subs]
        vals = [_swa_keys_t(b, vp_ref, vc_ref) for b in subs]
        logits = [_swa_logits(q_ref[0, b * BLOCK:(b + 1) * BLOCK, :], keys[b][0], keys[b][1]) for b in subs]
        ws = [_swa_softmax(logits[b][1], logits[b][2], bias, sink, keys[b][2]) for b in subs]
        for b in subs:
            o_ref[0, :, b * BLOCK:(b + 1) * BLOCK] = (_dot(vals[b][0], ws[b][0].astype(_MXU))
                                                      + _dot(vals[b][1], ws[b][1].astype(_MXU)))

    prev = pl.BlockSpec((1, BLOCK, HEAD_DIM), lambda h, i: (h // SWA_GROUP, jnp.maximum(i * ns - 1, 0), 0))
    cur = pl.BlockSpec((1, R, HEAD_DIM), lambda h, i: (h // SWA_GROUP, i, 0))
    prev_t = pl.BlockSpec((1, HEAD_DIM, BLOCK), lambda h, i: (h // SWA_GROUP, 0, jnp.maximum(i * ns - 1, 0)))
    cur_t = pl.BlockSpec((1, HEAD_DIM, R), lambda h, i: (h // SWA_GROUP, 0, i))
    return pl.pallas_call(
        body, name="swa_fwd", grid=(SWA_HEADS, nb // ns),
        in_specs=[pl.BlockSpec((1, R, HEAD_DIM), lambda h, i: (h, i, 0)), prev, cur, prev_t, cur_t,
                  pl.BlockSpec((1, 2 * BLOCK, BLOCK), lambda h, i: (h, 0, 0)),
                  pl.BlockSpec((1, 1, BLOCK), lambda h, i: (h, 0, 0))],
        out_specs=pl.BlockSpec((1, HEAD_DIM, R), lambda h, i: (h, 0, i)),
        out_shape=_sds((SWA_HEADS, HEAD_DIM, S), F32),
        compiler_params=_cp(("parallel", "parallel")),
    )(q, k, k, vT, vT, bias, sink)


def _swa_bwd(q, k, kT, v, bias, sink, do):
    S = q.shape[1]
    nb = S // BLOCK
    ns = _swa_sub(nb)
    R = ns * BLOCK

    def body(q_ref, kp_ref, kc_ref, ktp_ref, ktc_ref, vp_ref, vc_ref, bias_ref, sink_ref, do_ref, dq_ref, dk_ref, dv_ref,
             dbias_ref, dsink_ref):
        g = pl.program_id(1)
        i = pl.program_id(2)

        @pl.when(jnp.logical_and(g == 0, i == 0))
        def _():
            dk_ref[...] = jnp.zeros_like(dk_ref)
            dv_ref[...] = jnp.zeros_like(dv_ref)

        @pl.when(i == 0)
        def _():
            dbias_ref[...] = jnp.zeros_like(dbias_ref)
            dsink_ref[...] = jnp.zeros_like(dsink_ref)

        bias = bias_ref[0]
        sink = sink_ref[0][:, :1]
        subs = range(ns)
        rows = [slice(b * BLOCK, (b + 1) * BLOCK) for b in subs]
        keys = [_swa_keys(b, kp_ref, kc_ref, i) for b in subs]
        keys_t = [_swa_keys_t(b, ktp_ref, ktc_ref) for b in subs]
        vals = [_swa_keys(b, vp_ref, vc_ref, i) for b in subs]
        douts = [do_ref[0, rows[b], :] for b in subs]
        logits = [_swa_logits(q_ref[0, rows[b], :], keys[b][0], keys[b][1]) for b in subs]
        dws = [(_dot_nt(vals[b][0], douts[b]), _dot_nt(vals[b][1], douts[b])) for b in subs]
        dbp = jnp.zeros((BLOCK, BLOCK), F32)
        dbc = jnp.zeros((BLOCK, BLOCK), F32)
        dsk = jnp.zeros((1, BLOCK), F32)
        wts, dls = [], []
        for b in subs:
            wp, wc, ws = _swa_softmax(logits[b][1], logits[b][2], bias, sink, keys[b][2])
            dwp, dwc = dws[b]
            delta = _colsum(wp * dwp) + _colsum(wc * dwc)
            dlp = wp * (dwp - delta)
            dlc = wc * (dwc - delta)
            dbp += dlp
            dbc += dlc
            dsk -= ws * delta
            wts.append((wp.astype(_MXU), wc.astype(_MXU)))
            dls.append((dlp.astype(_MXU), dlc.astype(_MXU)))
        for b in subs:
            dq_ref[0, :, rows[b]] = (_dot(keys_t[b][0], dls[b][0]) + _dot(keys_t[b][1], dls[b][1])) * SCALE
        for b in subs:
            qs = logits[b][0]
            blk = i * ns + b
            dk_ref[0, blk] += _dot(dls[b][1], qs)
            dv_ref[0, blk] += _dot(wts[b][1], douts[b])
            if b == 0:
                @pl.when(i > 0)
                def _():
                    dk_ref[0, blk - 1] += _dot(dls[0][0], qs)
                    dv_ref[0, blk - 1] += _dot(wts[0][0], douts[0])
            else:
                dk_ref[0, blk - 1] += _dot(dls[b][0], qs)
                dv_ref[0, blk - 1] += _dot(wts[b][0], douts[b])
        dbias_ref[0, :BLOCK, :] += dbp
        dbias_ref[0, BLOCK:, :] += dbc
        dsink_ref[0] += jnp.broadcast_to(dsk, (8, BLOCK))

    hq = lambda kv, g, i: kv * SWA_GROUP + g
    prev = pl.BlockSpec((1, BLOCK, HEAD_DIM), lambda kv, g, i: (kv, jnp.maximum(i * ns - 1, 0), 0))
    cur = pl.BlockSpec((1, R, HEAD_DIM), lambda kv, g, i: (kv, i, 0))
    prev_t = pl.BlockSpec((1, HEAD_DIM, BLOCK), lambda kv, g, i: (kv, 0, jnp.maximum(i * ns - 1, 0)))
    cur_t = pl.BlockSpec((1, HEAD_DIM, R), lambda kv, g, i: (kv, 0, i))
    qblk = pl.BlockSpec((1, R, HEAD_DIM), lambda kv, g, i: (hq(kv, g, i), i, 0))
    qblk_t = pl.BlockSpec((1, HEAD_DIM, R), lambda kv, g, i: (hq(kv, g, i), 0, i))
    kvacc = pl.BlockSpec((1, nb, BLOCK, HEAD_DIM), lambda kv, g, i: (kv, 0, 0, 0))
    return pl.pallas_call(
        body, name="swa_bwd", grid=(SWA_KV_HEADS, SWA_GROUP, nb // ns),
        in_specs=[qblk, prev, cur, prev_t, cur_t, prev, cur,
                  pl.BlockSpec((1, 2 * BLOCK, BLOCK), lambda kv, g, i: (hq(kv, g, i), 0, 0)),
                  pl.BlockSpec((1, 1, BLOCK), lambda kv, g, i: (hq(kv, g, i), 0, 0)), qblk],
        out_specs=[qblk_t, kvacc, kvacc,
                   pl.BlockSpec((1, 2 * BLOCK, BLOCK), lambda kv, g, i: (hq(kv, g, i), 0, 0)),
                   pl.BlockSpec((1, 8, BLOCK), lambda kv, g, i: (hq(kv, g, i), 0, 0))],
        out_shape=[_sds((SWA_HEADS, HEAD_DIM, S), F32), _sds((SWA_KV_HEADS, nb, BLOCK, HEAD_DIM), F32),
                   _sds((SWA_KV_HEADS, nb, BLOCK, HEAD_DIM), F32), _sds((SWA_HEADS, 2 * BLOCK, BLOCK), F32),
                   _sds((SWA_HEADS, 8, BLOCK), F32)],
        compiler_params=_cp(("arbitrary", "arbitrary", "arbitrary")),
    )(q, k, k, kT, kT, v, v, bias, sink, do)


def _swa_small_grads(dbias, dsink, bucket):
    rows = REL_BUCKETS + 8

    def total(x):
        return jnp.sum(jnp.sum(x, axis=1, keepdims=True), axis=0, keepdims=True)

    def body(db_ref, ds_ref, bk_ref, o_ref):
        bk = bk_ref[...]
        r = lax.broadcasted_iota(jnp.int32, (rows, BLOCK), 0)
        c = lax.broadcasted_iota(jnp.int32, (rows, BLOCK), 1)
        out = jnp.zeros((rows, BLOCK), F32)
        for h in range(SWA_HEADS):
            db = db_ref[h]
            for b in range(REL_BUCKETS):
                s = total(jnp.where(bk == b, db, 0.0))
                out = jnp.where(jnp.logical_and(r == b, c == h), s, out)
            s = jnp.sum(ds_ref[h][0:1, :], axis=1, keepdims=True)
            out = jnp.where(jnp.logical_and(r == REL_BUCKETS, c == h), s, out)
        o_ref[...] = out

    vm = pl.BlockSpec(memory_space=pltpu.VMEM)
    return pl.pallas_call(body, name="swa_small_grads", in_specs=[vm, vm, vm], out_specs=vm,
                          out_shape=_sds((rows, BLOCK), F32))(dbias, dsink, bucket)


def _tile_rows(n):
    for t in (512, 352, 256, 176, 128, 64, 32, 16, 8):
        if n % t == 0:
            return t
    return n


def _cast_rows(x, dtype, name):
    R, C = x.shape
    tr = _tile_rows(R)

    def body(x_ref, o_ref):
        o_ref[...] = x_ref[...].astype(o_ref.dtype)

    return pl.pallas_call(body, name=name, grid=(R // tr,), in_specs=[_rows(tr, C)], out_specs=_rows(tr, C),
                          out_shape=_sds((R, C), dtype), compiler_params=_cp(("parallel",)))(x)


def _pair_sum(g, recv, c, name):
    n, half, C = recv.shape
    tr = _tile_rows(half)

    def body(c_ref, a_ref, b_ref, o_ref):
        o_ref[...] = (a_ref[0] + b_ref[...]).astype(o_ref.dtype)

    return pl.pallas_call(
        body, name=name,
        grid_spec=pltpu.PrefetchScalarGridSpec(
            num_scalar_prefetch=1, grid=(n, half // tr),
            in_specs=[pl.BlockSpec((1, 1, tr, C), lambda j, i, c_ref: (j, c_ref[0], i, 0)),
                      pl.BlockSpec((1, tr, C), lambda j, i, c_ref: (j, i, 0))],
            out_specs=pl.BlockSpec((1, tr, C), lambda j, i, c_ref: (j, i, 0))),
        out_shape=_sds((n, half, C), _MXU),
        compiler_params=_cp(("parallel", "parallel")))(c.reshape(1), g.reshape(n, 2, half, C), recv)


def _chip_sum(own, recv, me, name):
    n, R, C = recv.shape
    tr = _tile_rows(R)

    def body(me_ref, own_ref, recv_ref, o_ref):
        acc = None
        for j in range(n):
            term = jnp.where(me_ref[0] == j, own_ref[0], recv_ref[j]).astype(F32)
            acc = term if acc is None else acc + term
        o_ref[...] = acc

    return pl.pallas_call(
        body, name=name,
        grid_spec=pltpu.PrefetchScalarGridSpec(
            num_scalar_prefetch=1, grid=(R // tr,),
            in_specs=[pl.BlockSpec((1, tr, C), lambda i, me_ref: (me_ref[0], i, 0)),
                      pl.BlockSpec((n, tr, C), lambda i, me_ref: (0, i, 0))],
            out_specs=pl.BlockSpec((tr, C), lambda i, me_ref: (i, 0))),
        out_shape=_sds((R, C), F32), compiler_params=_cp(("parallel",)))(me.reshape(1), own, recv)


def _adamw_math(w, g, m, v):
    m = ADAM_B1 * m + (1.0 - ADAM_B1) * g
    v = ADAM_B2 * v + (1.0 - ADAM_B2) * (g * g)
    m_hat = m / (1.0 - ADAM_B1 ** ADAM_STEP)
    v_hat = v / (1.0 - ADAM_B2 ** ADAM_STEP)
    delta = -ADAM_LR * (m_hat / (jnp.sqrt(v_hat) + ADAM_EPS) + ADAM_WD * w)
    return delta, m, v


def _adamw(w, g, m, v, name):
    R, C = w.shape
    tr = _tile_rows(R)

    def body(w_ref, g_ref, m_ref, v_ref, d_ref, nm_ref, nv_ref):
        d, nm, nv = _adamw_math(w_ref[...], g_ref[...], m_ref[...], v_ref[...])
        d_ref[...] = d
        nm_ref[...] = nm
        nv_ref[...] = nv

    blk = _rows(tr, C)
    return pl.pallas_call(body, name=name, grid=(R // tr,), in_specs=[blk] * 4, out_specs=[blk] * 3,
                          out_shape=[_sds((R, C), F32)] * 3, compiler_params=_cp(("parallel",)))(w, g, m, v)


def _gather_weights(shards):
    nw = len(shards)
    ins, outs, sems = _gather_io(shards)

    def body(*refs):
        ex = _Gather(refs[:nw], refs[nw:2 * nw], *refs[2 * nw:])
        ex.start()
        ex.finish()

    any_spec = pl.BlockSpec(memory_space=pl.ANY)
    got = pl.pallas_call(body, name="gather_weights", in_specs=[any_spec] * nw, out_specs=[any_spec] * nw,
                         out_shape=outs, scratch_shapes=sems)(*ins)
    return _gather_assemble(got, shards)


def _swap_halves(grads, name):
    nw = len(grads)

    def body(*refs):
        g_refs, out_refs = refs[:nw], refs[nw:2 * nw]
        send_sems, recv_sems = refs[2 * nw:]
        x, y, c, _ = _place()
        cps = []
        for w in range(nw):
            half = out_refs[w].shape[1]
            theirs = g_refs[w].at[:, pl.ds(pl.multiple_of((1 - c) * half, 8), half), :]
            cps.append(pltpu.make_async_remote_copy(
                src_ref=theirs, dst_ref=out_refs[w], send_sem=send_sems.at[w], recv_sem=recv_sems.at[w],
                device_id=(x, y, 1 - c), device_id_type=MESH))
        for cp in cps:
            cp.start()
        for cp in cps:
            cp.wait()

    any_spec = pl.BlockSpec(memory_space=pl.ANY)
    return pl.pallas_call(
        body, name=name, in_specs=[any_spec] * nw, out_specs=[any_spec] * nw,
        out_shape=[_sds((g.shape[0], g.shape[1] // 2, g.shape[2]), g.dtype) for g in grads],
        scratch_shapes=[pltpu.SemaphoreType.DMA((nw,)), pltpu.SemaphoreType.DMA((nw,))],
    )(*grads)


def _scatter_partials(parts):
    nw = len(parts)
    ins, outs, sems = _scatter_io(parts)

    def body(*refs):
        ex = _Scatter(refs[:nw], refs[nw:2 * nw], *refs[2 * nw:])
        ex.start()
        ex.finish()

    any_spec = pl.BlockSpec(memory_space=pl.ANY)
    return pl.pallas_call(body, name="scatter_partials", in_specs=[any_spec] * nw, out_specs=[any_spec] * nw,
                          out_shape=outs, scratch_shapes=sems)(*ins)


def _join_halves(sums):
    nw = len(sums)

    def body(*refs):
        f_refs, out_refs = refs[:nw], refs[nw:2 * nw]
        send_sems, recv_sems = refs[2 * nw:]
        x, y, c, _ = _place()
        ws = range(nw)

        def copy(w, half_index):
            return pltpu.make_async_remote_copy(
                src_ref=f_refs[w], dst_ref=out_refs[w].at[half_index], send_sem=send_sems.at[w],
                recv_sem=recv_sems.at[w], device_id=(x, y, 1 - c), device_id_type=MESH)

        sends = [copy(w, c) for w in ws]
        for cp in sends:
            cp.start()
        for w in ws:
            copy(w, 1 - c).wait_recv()
        for cp in sends:
            cp.wait_send()

    any_spec = pl.BlockSpec(memory_space=pl.ANY)
    outs = pl.pallas_call(
        body, name="join_halves", in_specs=[any_spec] * nw, out_specs=[any_spec] * nw,
        out_shape=[_sds((2,) + f.shape, f.dtype) for f in sums],
        scratch_shapes=[pltpu.SemaphoreType.DMA((nw,)), pltpu.SemaphoreType.DMA((nw,))],
    )(*sums)
    c = lax.axis_index("c")
    return [lax.dynamic_update_slice_in_dim(o, f[None], c, axis=0).reshape(2 * f.shape[0], f.shape[1])
            for o, f in zip(outs, sums)]


def _allreduce_small(block):
    m_per, n = block.shape

    def body(x_ref, sum_ref, loss_ref, all_ref, send_sems, recv_sems, local_sem):
        x, y, c, chips = _place()
        me, sibling = (x, y, c), (x, y, 1 - c)

        def rows(px, py, pc):
            return all_ref.at[pl.ds(pl.multiple_of((4 * px + 2 * py + pc) * m_per, 8), m_per), :]

        def copy(k, blk, to, src=None):
            return pltpu.make_async_remote_copy(
                src_ref=rows(*blk) if src is None else src, dst_ref=rows(*blk), send_sem=send_sems.at[k],
                recv_sem=recv_sems.at[k], device_id=to, device_id_type=MESH)

        mine = pltpu.make_async_copy(x_ref, rows(*me), local_sem)
        mine.start()
        first = [copy(0, me, sibling, src=x_ref)]
        first += [copy(1 + j, me, (*chip, c), src=x_ref) for j, chip in enumerate(chips)]
        for cp in first:
            cp.start()
        passed = [copy(4 + j, (*chip, c), sibling) for j, chip in enumerate(chips)]
        for j, chip in enumerate(chips):
            copy(1 + j, (*chip, c), me).wait_recv()
            passed[j].start()
        copy(0, sibling, me).wait_recv()
        for j, chip in enumerate(chips):
            copy(4 + j, (*chip, 1 - c), me).wait_recv()
        for cp in first + passed:
            cp.wait_send()
        mine.wait()

        acc = all_ref[0:m_per, :]
        for d in range(1, 8):
            acc = acc + all_ref[d * m_per:(d + 1) * m_per, :]
        sum_ref[...] = acc
        tot = jnp.sum(acc[8:9, :], axis=1, keepdims=True) * (0.5 / D_MODEL)
        loss_ref[...] = jnp.broadcast_to(tot, loss_ref.shape)

    vm = pl.BlockSpec(memory_space=pltpu.VMEM)
    return pl.pallas_call(
        body, name="allreduce_small", in_specs=[vm], out_specs=[vm, vm],
        out_shape=[_sds((m_per, n), F32), _sds((8, 128), F32)],
        scratch_shapes=[pltpu.VMEM((8 * m_per, n), F32), pltpu.SemaphoreType.DMA((7,)), pltpu.SemaphoreType.DMA((7,)),
                        pltpu.SemaphoreType.DMA],
    )(block)


def _heads_rows(x, nh):
    S = x.shape[0]
    return x.reshape(S, nh, HEAD_DIM).transpose(1, 0, 2)


def _heads_cols(x, nh):
    S = x.shape[0]
    return x.reshape(S, nh, HEAD_DIM).transpose(1, 2, 0)


def _key_blocks(x, nh, t):
    S = x.shape[0]
    return x.reshape(S // t, t, nh, HEAD_DIM).transpose(2, 0, 1, 3)


def _key_blocks_t(x, nh, t):
    S = x.shape[0]
    return x.reshape(S // t, t, nh, HEAD_DIM).transpose(2, 0, 3, 1)


def _pad_row(v):
    v = v.reshape(1, -1)
    return jnp.pad(v, ((0, 0), (0, D_MODEL - v.shape[1])))


def _pack_small(ln_in_g, ln_in_b, sb_g, swa_g, sinks, rel_bias, ln1_g, ln1_b, ln2_g, ln2_b, extra):
    rows = [_pad_row(ln_in_g), _pad_row(ln_in_b), jnp.concatenate([sb_g.reshape(1, -1), swa_g.reshape(1, -1)], axis=1),
            _pad_row(jnp.concatenate([rel_bias.reshape(1, -1), sinks.reshape(1, -1)], axis=1)),
            _pad_row(ln1_g), _pad_row(ln1_b), _pad_row(ln2_g), _pad_row(ln2_b), _pad_row(extra)]
    rows.append(jnp.zeros((SMALL_ROWS - len(rows), D_MODEL), F32))
    return jnp.concatenate(rows, axis=0)


def _unpack_small(blk):
    nrb = REL_BUCKETS * SWA_HEADS
    return (blk[0], blk[1], blk[2:3, :SB_WIDTH], blk[2:3, SB_WIDTH:], blk[3:4, nrb:nrb + SWA_HEADS],
            blk[3, :nrb].reshape(REL_BUCKETS, SWA_HEADS), blk[4:5], blk[5:6], blk[6:7], blk[7:8])


def kernel(x, ln_in_g, ln_in_b, w_in, sb_norm_g, swa_norm_g, sinks, rel_bias, w_out, ln1_g, ln1_b, w_gate_up, w_down, ln2_g, ln2_b, loss_target, m_ln_in_g, m_ln_in_b, m_w_in, m_sb_norm_g, m_swa_norm_g, m_sinks, m_rel_bias, m_w_out, m_ln1_g, m_ln1_b, m_w_gate_up, m_w_down, m_ln2_g, m_ln2_b, v_ln_in_g, v_ln_in_b, v_w_in, v_sb_norm_g, v_swa_norm_g, v_sinks, v_rel_bias, v_w_out, v_ln1_g, v_ln1_b, v_w_gate_up, v_w_down, v_ln2_g, v_ln2_b):
    S = x.shape[1]
    x2 = x.reshape(S, D_MODEL)
    tgt = loss_target.reshape(S, D_MODEL)
    T = min(S, SB_TILE)
    bucket = jnp.asarray(_bucket_table().T)
    row = lambda v: v.reshape(1, -1)

    shards = [_cast_rows(w[0], _MXU, "cast_" + n) for n, w in (("w_in", w_in), ("w_out", w_out), ("w_gate_up", w_gate_up), ("w_down", w_down))]
    (w_in_sh,) = _gather_weights(shards[:1])
    w_in_f = jnp.concatenate([w_in_sh[j] for j in range(N_CHIPS)], axis=1)

    h0, h0b, proj = _ln_in_proj(x2, row(ln_in_g), row(ln_in_b), w_in_f)
    o1, o2, o3, o4, o5 = SB_WIDTH, 2 * SB_WIDTH, 3 * SB_WIDTH, 3 * SB_WIDTH + SWA_WIDTH, 3 * SB_WIDTH + SWA_WIDTH + SWA_KV_WIDTH
    q_sb, k_sb, v_sb = proj[:, :o1], proj[:, o1:o2], proj[:, o2:o3]
    q_sw, k_sw, v_sw = proj[:, o3:o4], proj[:, o4:o5], proj[:, o5:]
    qT_sb = _heads_cols(q_sb, SB_HEADS)
    kb_sb = _key_blocks(k_sb, SB_HEADS, T)
    sb_out, rsave, (w_out_sh, w_gu_sh, w_down_sh) = _sb_fwd(qT_sb, kb_sb, _key_blocks_t(v_sb, SB_HEADS, T), shards[1:])
    w_out_f = w_out_sh.reshape(D_MODEL, D_MODEL)
    w_down_f = w_down_sh.reshape(D_FF, D_MODEL)

    bias = _swa_bias(rel_bias, bucket)
    sink_rows = jnp.broadcast_to(sinks.reshape(SWA_HEADS, 1, 1), (SWA_HEADS, 1, BLOCK))
    qh_sw, kh_sw, vh_sw = _heads_rows(q_sw, SWA_HEADS), _heads_rows(k_sw, SWA_KV_HEADS), _heads_rows(v_sw, SWA_KV_HEADS)
    oT_sw = _swa_fwd(qh_sw, kh_sw, _heads_cols(v_sw, SWA_KV_HEADS), bias, sink_rows)
    swa_out = oT_sw.transpose(2, 0, 1).reshape(S, SWA_WIDTH)

    pre1, merged = _mix_out(sb_out, swa_out, sb_norm_g, swa_norm_g, w_out_f, h0)
    gate, up, act = _ffn_up(pre1, ln1_g, ln1_b, w_gu_sh)
    dp2, dp2b, dg2, db2, errsum, h1b = _ffn_down_loss(act, w_down_f, pre1, ln1_g, ln1_b, ln2_g, ln2_b, tgt)

    g_w_down = _matmul_tn(act, dp2b, "grad_w_down", FF_CHUNK, D_MODEL)
    dgate, dup = _ffn_down_bwd(dp2b, w_down_f, gate, up)
    g_w_gu = _matmul_tn_pair(h1b, dgate, dup, "grad_w_gate_up")
    dp1, dp1b, dg1, db1 = _ffn_up_bwd(dgate, dup, w_gu_sh, dp2, pre1, ln1_g)
    g_w_out = _matmul_tn(merged, dp1b, "grad_w_out", D_MODEL, D_MODEL)
    dsb, dsw, dgsb, dgsw = _mix_bwd(dp1b, w_out_f, sb_out, swa_out, sb_norm_g, swa_norm_g)

    dqT_sw, dkh_sw, dvh_sw, dbias, dsink = _swa_bwd(qh_sw, kh_sw, _heads_cols(k_sw, SWA_KV_HEADS), vh_sw, bias, sink_rows,
                                                    _heads_rows(dsw, SWA_HEADS))
    swa_small = _swa_small_grads(dbias, dsink, bucket)
    c = lax.axis_index("c").astype(jnp.int32)
    me = (2 * lax.axis_index("x") + lax.axis_index("y")).astype(jnp.int32)
    grads_a = [g_w_out.reshape(N_CHIPS, D_MODEL // N_CHIPS, D_MODEL), g_w_gu, g_w_down.reshape(N_CHIPS, D_FF // N_CHIPS, D_MODEL)]
    names_a = ("w_out", "w_gate_up", "w_down")
    partials_a = [_pair_sum(g, r, c, "pair_sum_" + n) for g, r, n in zip(grads_a, _swap_halves(grads_a, "swap_halves_ffn"), names_a)]
    dq_sb, dk_sb, dv_sb, recv_a = _sb_bwd(qT_sb, kb_sb, _key_blocks_t(k_sb, SB_HEADS, T), _key_blocks(v_sb, SB_HEADS, T),
                                             _heads_cols(dsb, SB_HEADS), rsave, partials_a)
    tok = lambda t, nh: t.reshape(nh, S, HEAD_DIM).transpose(1, 0, 2).reshape(S, nh * HEAD_DIM)
    dproj = jnp.concatenate([dq_sb, dk_sb, dv_sb,
                             dqT_sw.transpose(2, 0, 1).reshape(S, SWA_WIDTH), tok(dkh_sw, SWA_KV_HEADS), tok(dvh_sw, SWA_KV_HEADS)],
                            axis=1).astype(_MXU)
    g_w_in = _matmul_tn(h0b, dproj, "grad_w_in", D_MODEL, IN_COLS // 2)
    grad_x, dg_in, db_in = _in_proj_bwd(dproj, w_in_f, dp1, x2, row(ln_in_g))

    cin = IN_COLS // N_CHIPS
    grads_b = [jnp.stack([g_w_in[:, j * cin:(j + 1) * cin] for j in range(N_CHIPS)])]
    partials_b = [_pair_sum(grads_b[0], _swap_halves(grads_b, "swap_halves_in")[0], c, "pair_sum_w_in")]
    recv_b = _scatter_partials(partials_b)
    names = ("w_in",) + names_a
    sums = [_chip_sum(p, r, me, "chip_sum_" + n) for p, r, n in zip(partials_b + partials_a, list(recv_b) + list(recv_a), names)]
    gs_in, gs_out, gs_gu, gs_down = _join_halves(sums)

    nrb = REL_BUCKETS * SWA_HEADS
    small = _pack_small(dg_in, db_in, dgsb, dgsw, swa_small[REL_BUCKETS, :SWA_HEADS],
                        swa_small[:REL_BUCKETS, :SWA_HEADS], dg1, db1, dg2, db2, errsum)
    g_small, loss_tile = _allreduce_small(small)
    loss = loss_tile[0, 0]

    big = []
    for name, w, g, m, v in (("adamw_w_in", w_in, gs_in, m_w_in, v_w_in), ("adamw_w_out", w_out, gs_out, m_w_out, v_w_out),
                             ("adamw_w_gate_up", w_gate_up, gs_gu, m_w_gate_up, v_w_gate_up),
                             ("adamw_w_down", w_down, gs_down, m_w_down, v_w_down)):
        d, nm, nv = _adamw(w[0], g, m[0], v[0], name)
        big.append((g[None], d[None], nm[None], nv[None]))
    zero = jnp.zeros((1,), F32)
    w_small = _pack_small(ln_in_g, ln_in_b, sb_norm_g, swa_norm_g, sinks, rel_bias, ln1_g, ln1_b, ln2_g, ln2_b, zero)
    m_small = _pack_small(m_ln_in_g, m_ln_in_b, m_sb_norm_g, m_swa_norm_g, m_sinks, m_rel_bias, m_ln1_g, m_ln1_b,
                          m_ln2_g, m_ln2_b, zero)
    v_small = _pack_small(v_ln_in_g, v_ln_in_b, v_sb_norm_g, v_swa_norm_g, v_sinks, v_rel_bias, v_ln1_g, v_ln1_b,
                          v_ln2_g, v_ln2_b, zero)
    small_out = [_unpack_small(t) for t in (g_small,) + tuple(_adamw(w_small, g_small, m_small, v_small, "adamw_small"))]

    def kind(k):
        s = small_out[k]
        return [s[0], s[1], big[0][k], s[2], s[3], s[4], s[5], big[1][k], s[6], s[7], big[2][k], big[3][k], s[8], s[9]]

    return (loss, grad_x.reshape(1, S, D_MODEL), *kind(0), *kind(1), *kind(2), *kind(3))
```

```python
import functools
import math

import numpy as np
import jax
import jax.numpy as jnp
from jax import lax
from jax.experimental import pallas as pl
from jax.experimental.pallas import tpu as pltpu

F32 = jnp.float32
_MXU = jnp.bfloat16

D_MODEL = 1024
HEAD_DIM = 64
SB_HEADS = 8
SWA_HEADS = 8
SWA_KV_HEADS = 2
SWA_GROUP = SWA_HEADS // SWA_KV_HEADS
SB_WIDTH = SB_HEADS * HEAD_DIM
SWA_WIDTH = SWA_HEADS * HEAD_DIM
SWA_KV_WIDTH = SWA_KV_HEADS * HEAD_DIM
IN_COLS = 3 * SB_WIDTH + SWA_WIDTH + 2 * SWA_KV_WIDTH
BLOCK = 128
REL_BUCKETS = 32
REL_MAX_DIST = 128
D_FF = 2816
FF_CHUNK = D_FF // 2
ALPHA = 2.0 ** 0.25
LN_EPS = 1e-5
RMS_EPS = 1e-6
SCALE = HEAD_DIM ** -0.5
SB_TILE = 256
SB_GROUP_FWD = 8
SB_GROUP_BWD = 4
SB_DEAD = -105.0
SWA_SUB = 8

ADAM_LR = 0.001
ADAM_B1 = 0.9
ADAM_B2 = 0.999
ADAM_EPS = 1e-08
ADAM_WD = 0.01
ADAM_STEP = 10

N_CHIPS = 4
SMALL_ROWS = 16

MESH = pl.DeviceIdType.MESH


def _sds(shape, dtype):
    return jax.ShapeDtypeStruct(shape, dtype)


def _cp(sem=None, vmem_mb=48):
    kw = dict(vmem_limit_bytes=vmem_mb * 1024 * 1024)
    if sem is not None:
        kw["dimension_semantics"] = sem
    return pltpu.CompilerParams(**kw)


def _dot(a, b):
    return jnp.dot(a, b, preferred_element_type=F32)


def _dot_nt(a, b):
    return lax.dot_general(a, b, (((1,), (1,)), ((), ())), preferred_element_type=F32)


def _dot_tn(a, b):
    return lax.dot_general(a, b, (((0,), (0,)), ((), ())), preferred_element_type=F32)


def _ln_hat(x):
    mu = jnp.mean(x, axis=-1, keepdims=True)
    xc = x - mu
    var = jnp.mean(xc * xc, axis=-1, keepdims=True)
    rstd = lax.rsqrt(var + LN_EPS)
    return xc * rstd, rstd


def _ln_bwd(xhat, rstd, dy, g):
    dxh = dy * g
    m1 = jnp.mean(dxh, axis=-1, keepdims=True)
    m2 = jnp.mean(dxh * xhat, axis=-1, keepdims=True)
    return rstd * (dxh - m1 - xhat * m2)


def _colsum(x):
    return jnp.sum(x, axis=0, keepdims=True)


def _split2(x):
    hi = x.astype(_MXU)
    lo = (x - hi.astype(F32)).astype(_MXU)
    return hi, lo


def _rows(tm, n):
    return pl.BlockSpec((tm, n), lambda i: (i, 0))


def _fixed(*shape):
    nd = len(shape)
    return pl.BlockSpec(shape, lambda i: (0,) * nd)


def _ln_in_proj(x, g, b, w):
    S = x.shape[0]
    N = w.shape[1]
    tm = min(S, 512)

    def body(x_ref, g_ref, b_ref, w_ref, h_ref, hb_ref, p_ref):
        xhat, _ = _ln_hat(x_ref[...])
        h = xhat * g_ref[...] + b_ref[...]
        h_ref[...] = h
        hb = h.astype(_MXU)
        hb_ref[...] = hb
        p_ref[...] = _dot(hb, w_ref[...]).astype(p_ref.dtype)

    return pl.pallas_call(
        body, name="ln_in_proj", grid=(S // tm,),
        in_specs=[_rows(tm, D_MODEL), _fixed(1, D_MODEL), _fixed(1, D_MODEL), _fixed(D_MODEL, N)],
        out_specs=[_rows(tm, D_MODEL), _rows(tm, D_MODEL), _rows(tm, N)],
        out_shape=[_sds((S, D_MODEL), F32), _sds((S, D_MODEL), _MXU), _sds((S, N), _MXU)],
        compiler_params=_cp(("parallel",)),
    )(x, g, b, w)


def _rms(x, g):
    r = lax.rsqrt(jnp.mean(x * x, axis=-1, keepdims=True) + RMS_EPS)
    return x * r * g, r


def _mix_out(sb, sw, gsb, gsw, w_out, h0, g1, b1):
    S = sb.shape[0]
    tm = min(S, 512)

    def body(sb_ref, sw_ref, gsb_ref, gsw_ref, w_ref, h0_ref, g1_ref, b1_ref, pre_ref, mg_ref, h1_ref):
        ysb, _ = _rms(sb_ref[...], gsb_ref[...])
        ysw, _ = _rms(sw_ref[...], gsw_ref[...])
        ysb = ysb.astype(_MXU)
        ysw = ysw.astype(_MXU)
        mg_ref[:, :SB_WIDTH] = ysb
        mg_ref[:, SB_WIDTH:] = ysw
        mix = _dot(ysb, w_ref[:SB_WIDTH, :]) + _dot(ysw, w_ref[SB_WIDTH:, :])
        pre1 = ALPHA * h0_ref[...] + mix
        pre_ref[...] = pre1
        xhat, _ = _ln_hat(pre1)
        h1_ref[...] = (xhat * g1_ref[...] + b1_ref[...]).astype(h1_ref.dtype)

    vec = _fixed(1, D_MODEL)
    return pl.pallas_call(
        body, name="mix_out", grid=(S // tm,),
        in_specs=[_rows(tm, SB_WIDTH), _rows(tm, SWA_WIDTH), _fixed(1, SB_WIDTH), _fixed(1, SWA_WIDTH),
                  _fixed(D_MODEL, D_MODEL), _rows(tm, D_MODEL), vec, vec],
        out_specs=[_rows(tm, D_MODEL), _rows(tm, D_MODEL), _rows(tm, D_MODEL)],
        out_shape=[_sds((S, D_MODEL), F32), _sds((S, D_MODEL), _MXU), _sds((S, D_MODEL), _MXU)],
        compiler_params=_cp(("parallel",)),
    )(sb, sw, gsb, gsw, w_out, h0, g1, b1)


def _sigmoid(x):
    return 1.0 / (1.0 + jnp.exp(-x))


def _ffn_up(h1b, wgu):
    S = h1b.shape[0]
    tm = min(S, 512)

    def body(h_ref, wg_ref, wu_ref, a_ref, s1_ref, s2_ref):
        h1 = h_ref[...]
        gate = _dot(h1, wg_ref[0])
        up = _dot(h1, wu_ref[0])
        sg = _sigmoid(gate)
        silu = gate * sg
        a_ref[...] = (silu * up).astype(a_ref.dtype)
        s1_ref[...] = silu.astype(s1_ref.dtype)
        s2_ref[...] = (up * (sg * (1.0 + gate * (1.0 - sg)))).astype(s2_ref.dtype)

    chunk = pl.BlockSpec((tm, FF_CHUNK), lambda j, i: (i, j))
    return pl.pallas_call(
        body, name="ffn_up", grid=(2, S // tm),
        in_specs=[pl.BlockSpec((tm, D_MODEL), lambda j, i: (i, 0)),
                  pl.BlockSpec((1, D_MODEL, FF_CHUNK), lambda j, i: (j, 0, 0)),
                  pl.BlockSpec((1, D_MODEL, FF_CHUNK), lambda j, i: (j + 2, 0, 0))],
        out_specs=[chunk, chunk, chunk],
        out_shape=[_sds((S, D_FF), _MXU)] * 3,
        compiler_params=_cp(("arbitrary", "arbitrary")),
    )(h1b, wgu, wgu)


def _ffn_down_loss(a, w_down, pre1, g1, b1, g2, b2, tgt):
    S = a.shape[0]
    tm = min(S, 512)

    def body(a_ref, w_ref, p_ref, g1_ref, b1_ref, g2_ref, b2_ref, t_ref, d_ref, db_ref, dg2_ref, db2_ref, err_ref):
        @pl.when(pl.program_id(0) == 0)
        def _():
            dg2_ref[...] = jnp.zeros_like(dg2_ref)
            db2_ref[...] = jnp.zeros_like(db2_ref)
            err_ref[...] = jnp.zeros_like(err_ref)

        xhat1, _ = _ln_hat(p_ref[...])
        h1 = xhat1 * g1_ref[...] + b1_ref[...]
        pre2 = ALPHA * h1 + _dot(a_ref[...], w_ref[...])
        xhat2, rstd2 = _ln_hat(pre2)
        err = xhat2 * g2_ref[...] + b2_ref[...] - t_ref[...]
        dh2 = err * (1.0 / D_MODEL)
        dp2 = _ln_bwd(xhat2, rstd2, dh2, g2_ref[...])
        d_ref[...] = dp2
        db_ref[...] = dp2.astype(db_ref.dtype)
        dg2_ref[...] += _colsum(dh2 * xhat2)
        db2_ref[...] += _colsum(dh2)
        err_ref[...] += _colsum(err * err)

    vec = _fixed(1, D_MODEL)
    return pl.pallas_call(
        body, name="ffn_down_loss", grid=(S // tm,),
        in_specs=[_rows(tm, D_FF), _fixed(D_FF, D_MODEL), _rows(tm, D_MODEL), vec, vec, vec, vec, _rows(tm, D_MODEL)],
        out_specs=[_rows(tm, D_MODEL), _rows(tm, D_MODEL), vec, vec, vec],
        out_shape=[_sds((S, D_MODEL), F32), _sds((S, D_MODEL), _MXU), _sds((1, D_MODEL), F32), _sds((1, D_MODEL), F32),
                   _sds((1, D_MODEL), F32)],
        compiler_params=_cp(("arbitrary",)),
    )(a, w_down, pre1, g1, b1, g2, b2, tgt)


def _ffn_down_bwd(dp2b, w_down, s1, s2):
    S = dp2b.shape[0]
    tm = min(S, 512)

    def body(d_ref, w_ref, s1_ref, s2_ref, dg_ref, du_ref):
        da = _dot_nt(d_ref[...], w_ref[...])
        du_ref[...] = (da * s1_ref[...].astype(F32)).astype(du_ref.dtype)
        dg_ref[...] = (da * s2_ref[...].astype(F32)).astype(dg_ref.dtype)

    chunk = pl.BlockSpec((tm, FF_CHUNK), lambda j, i: (i, j))
    return pl.pallas_call(
        body, name="ffn_down_bwd", grid=(2, S // tm),
        in_specs=[pl.BlockSpec((tm, D_MODEL), lambda j, i: (i, 0)),
                  pl.BlockSpec((FF_CHUNK, D_MODEL), lambda j, i: (j, 0)), chunk, chunk],
        out_specs=[chunk, chunk],
        out_shape=[_sds((S, D_FF), _MXU), _sds((S, D_FF), _MXU)],
        compiler_params=_cp(("arbitrary", "arbitrary")),
    )(dp2b, w_down, s1, s2)


def _ffn_up_bwd(dgate, dup, wgu, dp2, pre1, g1):
    S = dgate.shape[0]
    tm = min(S, 256)

    def body(dg_ref, du_ref, w_ref, d2_ref, p_ref, g_ref, d1_ref, d1b_ref, dg1_ref, db1_ref):
        @pl.when(pl.program_id(0) == 0)
        def _():
            dg1_ref[...] = jnp.zeros_like(dg1_ref)
            db1_ref[...] = jnp.zeros_like(db1_ref)

        dh1 = ALPHA * d2_ref[...]
        for j in range(2):
            cols = slice(j * FF_CHUNK, (j + 1) * FF_CHUNK)
            dh1 += _dot_nt(dg_ref[:, cols], w_ref[j])
            dh1 += _dot_nt(du_ref[:, cols], w_ref[j + 2])
        xhat, rstd = _ln_hat(p_ref[...])
        dp1 = _ln_bwd(xhat, rstd, dh1, g_ref[...])
        d1_ref[...] = dp1
        d1b_ref[...] = dp1.astype(d1b_ref.dtype)
        dg1_ref[...] += _colsum(dh1 * xhat)
        db1_ref[...] += _colsum(dh1)

    vec = _fixed(1, D_MODEL)
    return pl.pallas_call(
        body, name="ffn_up_bwd", grid=(S // tm,),
        in_specs=[_rows(tm, D_FF), _rows(tm, D_FF), _fixed(4, D_MODEL, FF_CHUNK), _rows(tm, D_MODEL),
                  _rows(tm, D_MODEL), vec],
        out_specs=[_rows(tm, D_MODEL), _rows(tm, D_MODEL), vec, vec],
        out_shape=[_sds((S, D_MODEL), F32), _sds((S, D_MODEL), _MXU), _sds((1, D_MODEL), F32), _sds((1, D_MODEL), F32)],
        compiler_params=_cp(("arbitrary",), vmem_mb=56),
    )(dgate, dup, wgu, dp2, pre1, g1)


def _rms_bwd(x, g, dy):
    n = x.shape[-1]
    r = lax.rsqrt(jnp.mean(x * x, axis=-1, keepdims=True) + RMS_EPS)
    u = dy * g
    dx = r * u - x * (r * r * r) * (jnp.sum(u * x, axis=-1, keepdims=True) * (1.0 / n))
    return dx, _colsum(dy * x * r)


def _mix_bwd(dp1b, w_out, sb, sw, gsb, gsw):
    S = sb.shape[0]
    tm = min(S, 512)

    def body(d_ref, w_ref, sb_ref, sw_ref, gsb_ref, gsw_ref, dsb_ref, dsw_ref, dgsb_ref, dgsw_ref):
        @pl.when(pl.program_id(0) == 0)
        def _():
            dgsb_ref[...] = jnp.zeros_like(dgsb_ref)
            dgsw_ref[...] = jnp.zeros_like(dgsw_ref)

        dm = _dot_nt(d_ref[...], w_ref[...])
        dsb, dgsb = _rms_bwd(sb_ref[...], gsb_ref[...], dm[:, :SB_WIDTH])
        dsw, dgsw = _rms_bwd(sw_ref[...], gsw_ref[...], dm[:, SB_WIDTH:])
        dsb_ref[...] = dsb.astype(dsb_ref.dtype)
        dsw_ref[...] = dsw.astype(dsw_ref.dtype)
        dgsb_ref[...] += dgsb
        dgsw_ref[...] += dgsw

    return pl.pallas_call(
        body, name="mix_bwd", grid=(S // tm,),
        in_specs=[_rows(tm, D_MODEL), _fixed(D_MODEL, D_MODEL), _rows(tm, SB_WIDTH), _rows(tm, SWA_WIDTH),
                  _fixed(1, SB_WIDTH), _fixed(1, SWA_WIDTH)],
        out_specs=[_rows(tm, SB_WIDTH), _rows(tm, SWA_WIDTH), _fixed(1, SB_WIDTH), _fixed(1, SWA_WIDTH)],
        out_shape=[_sds((S, SB_WIDTH), _MXU), _sds((S, SWA_WIDTH), _MXU), _sds((1, SB_WIDTH), F32),
                   _sds((1, SWA_WIDTH), F32)],
        compiler_params=_cp(("arbitrary",)),
    )(dp1b, w_out, sb, sw, gsb, gsw)


def _in_proj_bwd(dproj, w_in, dp1, x, g, parts):
    S = x.shape[0]
    N = dproj.shape[1]
    tm = min(S, 512)
    nw = len(parts)
    s_ins, s_outs, s_sems = _scatter_io(parts)

    def body(dpj_ref, w_ref, d1_ref, x_ref, g_ref, *rest):
        gx_ref, dg_ref, db_ref = rest[nw:nw + 3]
        scatter = _Scatter(rest[:nw], rest[nw + 3:2 * nw + 3], *rest[2 * nw + 3:])

        @pl.when(pl.program_id(0) == 0)
        def _():
            scatter.start()
            dg_ref[...] = jnp.zeros_like(dg_ref)
            db_ref[...] = jnp.zeros_like(db_ref)

        dh0 = _dot_nt(dpj_ref[...], w_ref[...]) + ALPHA * d1_ref[...]
        xhat, rstd = _ln_hat(x_ref[...])
        gx_ref[...] = _ln_bwd(xhat, rstd, dh0, g_ref[...])
        dg_ref[...] += _colsum(dh0 * xhat)
        db_ref[...] += _colsum(dh0)

        @pl.when(pl.program_id(0) == pl.num_programs(0) - 1)
        def _():
            scatter.finish()

    vec = _fixed(1, D_MODEL)
    any_spec = pl.BlockSpec(memory_space=pl.ANY)
    res = pl.pallas_call(
        body, name="in_proj_bwd", grid=(S // tm,),
        in_specs=[_rows(tm, N), _fixed(D_MODEL, N), _rows(tm, D_MODEL), _rows(tm, D_MODEL), vec] + [any_spec] * nw,
        out_specs=[_rows(tm, D_MODEL), vec, vec] + [any_spec] * nw,
        out_shape=[_sds((S, D_MODEL), F32), _sds((1, D_MODEL), F32), _sds((1, D_MODEL), F32)] + s_outs,
        scratch_shapes=s_sems,
        compiler_params=_cp(("arbitrary",)),
    )(dproj, w_in, dp1, x, g, *s_ins)
    return res[0], res[1], res[2], list(res[3:])


def _matmul_tn(a, b, name, tk, tn):
    T, K = a.shape
    N = b.shape[1]
    tt = min(T, 512)

    def body(a_ref, b_ref, o_ref):
        @pl.when(pl.program_id(2) == 0)
        def _():
            o_ref[...] = jnp.zeros_like(o_ref)

        o_ref[...] += _dot_tn(a_ref[...], b_ref[...])

    return pl.pallas_call(
        body, name=name, grid=(K // tk, N // tn, T // tt),
        in_specs=[pl.BlockSpec((tt, tk), lambda k, n, t: (t, k)), pl.BlockSpec((tt, tn), lambda k, n, t: (t, n))],
        out_specs=pl.BlockSpec((tk, tn), lambda k, n, t: (k, n)),
        out_shape=_sds((K, N), F32),
        compiler_params=_cp(("parallel", "parallel", "arbitrary")),
    )(a, b)


def _place():
    x, y, c = lax.axis_index("x"), lax.axis_index("y"), lax.axis_index("c")
    chips = [(1 - x, y), (x, 1 - y), (1 - x, 1 - y)]
    return x, y, c, chips


class _Gather:
    def __init__(self, in_refs, out_refs, send_sems, recv_sems):
        self.in_refs, self.out_refs, self.send_sems, self.recv_sems = in_refs, out_refs, send_sems, recv_sems
        self.x, self.y, self.c, self.chips = _place()

    def _copy(self, w, k, chip, hc, to, src=None):
        part = self.out_refs[w].at[2 * chip[0] + chip[1], hc]
        return pltpu.make_async_remote_copy(
            src_ref=part if src is None else src, dst_ref=part, send_sem=self.send_sems.at[w, k],
            recv_sem=self.recv_sems.at[w, k], device_id=to, device_id_type=MESH)

    def _first(self):
        x, y, c = self.x, self.y, self.c
        return [self._copy(w, j, (x, y), c, (*chip, c), src=self.in_refs[w].at[c])
                for w in range(len(self.in_refs)) for j, chip in enumerate(self.chips)]

    def start(self):
        for cp in self._first():
            cp.start()

    def finish(self):
        x, y, c = self.x, self.y, self.c
        me, sibling = (x, y, c), (x, y, 1 - c)
        ws = range(len(self.in_refs))
        passed = []
        for w in ws:
            for j, chip in enumerate(self.chips):
                self._copy(w, j, chip, c, me).wait_recv()
                passed.append(self._copy(w, 3 + j, chip, c, sibling))
                passed[-1].start()
        for w in ws:
            for j, chip in enumerate(self.chips):
                self._copy(w, 3 + j, chip, 1 - c, me).wait_recv()
        for cp in self._first() + passed:
            cp.wait_send()


def _gather_io(shards):
    halves = [(s.shape[0] // 2, s.shape[1]) for s in shards]
    ins = [s.reshape(2, h, cols) for s, (h, cols) in zip(shards, halves)]
    outs = [_sds((N_CHIPS, 2, h, cols), s.dtype) for s, (h, cols) in zip(shards, halves)]
    sems = [pltpu.SemaphoreType.DMA((len(shards), 6)), pltpu.SemaphoreType.DMA((len(shards), 6))]
    return ins, outs, sems


def _gather_assemble(outs, shards):
    me = 2 * lax.axis_index("x") + lax.axis_index("y")
    return [lax.dynamic_update_slice_in_dim(o.reshape((N_CHIPS,) + s.shape), s[None], me, axis=0)
            for o, s in zip(outs, shards)]


class _Scatter:
    def __init__(self, p_refs, out_refs, send_sems, recv_sems):
        self.p_refs, self.out_refs, self.send_sems, self.recv_sems = p_refs, out_refs, send_sems, recv_sems
        self.x, self.y, self.c, self.chips = _place()
        self.me = 2 * self.x + self.y

    def _copy(self, w, j, chip, src_chip, dst_chip):
        return pltpu.make_async_remote_copy(
            src_ref=self.p_refs[w].at[src_chip], dst_ref=self.out_refs[w].at[dst_chip], send_sem=self.send_sems.at[w, j],
            recv_sem=self.recv_sems.at[w, j], device_id=(*chip, self.c), device_id_type=MESH)

    def _sends(self):
        return [self._copy(w, j, chip, 2 * chip[0] + chip[1], self.me)
                for w in range(len(self.p_refs)) for j, chip in enumerate(self.chips)]

    def start(self):
        for cp in self._sends():
            cp.start()

    def finish(self):
        for w in range(len(self.p_refs)):
            for j, chip in enumerate(self.chips):
                self._copy(w, j, chip, self.me, 2 * chip[0] + chip[1]).wait_recv()
        for cp in self._sends():
            cp.wait_send()


def _scatter_io(parts):
    sems = [pltpu.SemaphoreType.DMA((len(parts), 3)), pltpu.SemaphoreType.DMA((len(parts), 3))]
    return list(parts), [_sds(p.shape, p.dtype) for p in parts], sems


class _Swap:
    def __init__(self, g_refs, out_refs, send_sems, recv_sems):
        x, y, c, _ = _place()
        self.copies = []
        for w in range(len(g_refs)):
            half = out_refs[w].shape[1]
            theirs = g_refs[w].at[:, pl.ds(pl.multiple_of((1 - c) * half, 8), half), :]
            self.copies.append(pltpu.make_async_remote_copy(
                src_ref=theirs, dst_ref=out_refs[w], send_sem=send_sems.at[w], recv_sem=recv_sems.at[w],
                device_id=(x, y, 1 - c), device_id_type=MESH))

    def start(self):
        for cp in self.copies:
            cp.start()

    def finish(self):
        for cp in self.copies:
            cp.wait()


def _swap_io(grads):
    outs = [_sds((g.shape[0], g.shape[1] // 2, g.shape[2]), g.dtype) for g in grads]
    return list(grads), outs, [pltpu.SemaphoreType.DMA((len(grads),)), pltpu.SemaphoreType.DMA((len(grads),))]


def _matmul_tn_pair(a, b0, b1, name):
    T, K = a.shape
    tt = min(T, 512)

    def body(a_ref, b0_ref, b1_ref, o_ref):
        n = pl.program_id(0)

        @pl.when(pl.program_id(1) == 0)
        def _():
            o_ref[...] = jnp.zeros_like(o_ref)

        @pl.when(n < 2)
        def _():
            o_ref[0] += _dot_tn(a_ref[...], b0_ref[...])

        @pl.when(n >= 2)
        def _():
            o_ref[0] += _dot_tn(a_ref[...], b1_ref[...])

    return pl.pallas_call(
        body, name=name, grid=(4, T // tt),
        in_specs=[pl.BlockSpec((tt, K), lambda n, t: (t, 0)),
                  pl.BlockSpec((tt, FF_CHUNK), lambda n, t: (t, jnp.minimum(n, 1))),
                  pl.BlockSpec((tt, FF_CHUNK), lambda n, t: (t, jnp.maximum(n - 2, 0)))],
        out_specs=pl.BlockSpec((1, K, FF_CHUNK), lambda n, t: (n, 0, 0)),
        out_shape=_sds((4, K, FF_CHUNK), F32),
        compiler_params=_cp(("parallel", "arbitrary")),
    )(a, b0, b1)


def _sb_logs(zt, causal):
    e = jnp.exp(-jnp.abs(zt))
    lb = jnp.minimum(zt, 0.0) - jnp.log(1.0 + e)
    l1m = lb - zt
    if causal is not None:
        l1m = jnp.where(causal, l1m, 0.0)
    return lb, l1m


def _sb_weights(lb, suf, causal):
    a = jnp.exp(lb + suf)
    if causal is not None:
        a = jnp.where(causal, a, 0.0)
    return a


def _tri_masks(t):
    r = lax.broadcasted_iota(jnp.int32, (t, t), 0)
    c = lax.broadcasted_iota(jnp.int32, (t, t), 1)
    return r, c


def _sb_fwd(qT, kb, vTb, shards):
    Hh, _, S = qT.shape
    nk, T = kb.shape[1], kb.shape[2]
    nq = S // T
    G = SB_GROUP_FWD
    nw = len(shards)
    g_ins, g_outs, g_sems = _gather_io(shards)

    def body(qT_ref, k_ref, vT_ref, *rest):
        o_ref, rs_ref = rest[nw:nw + 2]
        gather = _Gather(rest[:nw], rest[nw + 2:2 * nw + 2], *rest[2 * nw + 2:])
        i = pl.program_id(1)
        first_step = jnp.logical_and(pl.program_id(0) == 0, i == 0)
        last_step = jnp.logical_and(pl.program_id(0) == pl.num_programs(0) - 1, i == pl.num_programs(1) - 1)

        @pl.when(first_step)
        def _():
            gather.start()

        qts = [(qT_ref[g].astype(F32) * SCALE).astype(_MXU) for g in range(G)]
        r, c = _tri_masks(T)
        upper = (c > r).astype(_MXU)
        causal = r < c

        def blk(j, carry, mask):
            hs = range(G)
            for g in hs:
                rs_ref[g, 0, j] = jnp.broadcast_to(carry[g][0], (8, T))
            zs = [_dot(k_ref[g, j], qts[g]) for g in hs]
            lbs, l1ms = zip(*[_sb_logs(zs[g], mask) for g in hs])
            splits = [_split2(l1ms[g]) for g in hs]
            cums = [_dot(upper, splits[g][0]) + _dot(upper, splits[g][1]) for g in hs]
            avs = [_sb_weights(lbs[g], carry[g][0] + cums[g], mask).astype(_MXU) for g in hs]
            accs = [carry[g][1] + _dot(vT_ref[g, j], avs[g]) for g in hs]
            return tuple((carry[g][0] + _colsum(l1ms[g]), accs[g]) for g in hs)

        def go_on(j, carry):
            top = carry[0][0]
            for g in range(1, G):
                top = jnp.maximum(top, carry[g][0])
            return jnp.logical_and(j >= 0, jnp.max(top) >= SB_DEAD)

        init = tuple((jnp.zeros((1, T), F32), jnp.zeros((HEAD_DIM, T), F32)) for _ in range(G))
        carry = blk(i, init, causal)
        j, carry = lax.while_loop(lambda st: go_on(*st), lambda st: (st[0] - 1, blk(st[0], st[1], None)),
                                  (i - 1, carry))

        @pl.when(j >= 0)
        def _():
            for g in range(G):
                rs_ref[g, 0, j] = jnp.broadcast_to(carry[g][0], (8, T))

        o_ref[...] = jnp.concatenate([carry[g][1] for g in range(G)], axis=0).T

        @pl.when(last_step)
        def _():
            gather.finish()

    any_spec = pl.BlockSpec(memory_space=pl.ANY)
    res = pl.pallas_call(
        body, name="sb_fwd", grid=(Hh // G, nq),
        in_specs=[pl.BlockSpec((G, HEAD_DIM, T), lambda h, i: (h, 0, i)),
                  pl.BlockSpec((G, nk, T, HEAD_DIM), lambda h, i: (h, 0, 0, 0), pipeline_mode=pl.Buffered(1)),
                  pl.BlockSpec((G, nk, HEAD_DIM, T), lambda h, i: (h, 0, 0, 0), pipeline_mode=pl.Buffered(1))]
                 + [any_spec] * nw,
        out_specs=[pl.BlockSpec((T, G * HEAD_DIM), lambda h, i: (i, h)),
                   pl.BlockSpec((G, 1, nk, 8, T), lambda h, i: (h, i, 0, 0, 0))] + [any_spec] * nw,
        out_shape=[_sds((S, Hh * HEAD_DIM), F32), _sds((Hh, nq, nk, 8, T), F32)] + g_outs,
        scratch_shapes=g_sems,
        compiler_params=_cp(("arbitrary", "arbitrary")),
    )(qT, kb, vTb, *g_ins)
    return res[0], res[1], _gather_assemble(res[2:], shards)


def _sb_bwd(qT, kb, kTb, vb, doT, rsave, parts):
    Hh, _, S = qT.shape
    nk, T = kb.shape[1], kb.shape[2]
    nq = S // T
    G = SB_GROUP_BWD
    nw = len(parts)
    s_ins, s_outs, s_sems = _scatter_io(parts)

    def body(qT_ref, k_ref, kT_ref, v_ref, doT_ref, rs_ref, *rest):
        dq_ref, dk_out_ref, dv_out_ref = rest[nw:nw + 3]
        dk_ref, dv_ref = rest[2 * nw + 3:2 * nw + 5]
        scatter = _Scatter(rest[:nw], rest[nw + 3:2 * nw + 3], *rest[2 * nw + 5:])
        i = pl.program_id(1)
        first_step = jnp.logical_and(pl.program_id(0) == 0, i == 0)
        last_step = jnp.logical_and(pl.program_id(0) == pl.num_programs(0) - 1, i == pl.num_programs(1) - 1)

        @pl.when(first_step)
        def _():
            scatter.start()

        @pl.when(i == 0)
        def _():
            dk_ref[...] = jnp.zeros_like(dk_ref)
            dv_ref[...] = jnp.zeros_like(dv_ref)

        qts = [(qT_ref[g].astype(F32) * SCALE).astype(_MXU) for g in range(G)]
        douts = [doT_ref[g] for g in range(G)]
        r, c = _tri_masks(T)
        upper = (c > r).astype(_MXU)
        lower = (c < r).astype(_MXU)
        causal = r < c

        def blk(j, carry, mask):
            hs = range(G)
            zs = [_dot(k_ref[g, j], qts[g]) for g in hs]
            das = [_dot(v_ref[g, j], douts[g]) for g in hs]
            lbs, l1ms = zip(*[_sb_logs(zs[g], mask) for g in hs])
            splits = [_split2(l1ms[g]) for g in hs]
            cums = [_dot(upper, splits[g][0]) + _dot(upper, splits[g][1]) for g in hs]
            avs = [_sb_weights(lbs[g], rs_ref[g, 0, j][0:1, :] + cums[g], mask) for g in hs]
            ets = [das[g] * avs[g] for g in hs]
            esplits = [_split2(ets[g]) for g in hs]
            ecums = [_dot(lower, esplits[g][0]) + _dot(lower, esplits[g][1]) for g in hs]
            dzs = []
            for g in hs:
                sig = jnp.exp(lbs[g])
                dz = ets[g] * (1.0 - sig) - (carry[g][0] + ecums[g]) * sig
                if mask is not None:
                    dz = jnp.where(mask, dz, 0.0)
                dzs.append(dz.astype(_MXU))
            dqs = [carry[g][1] + _dot(kT_ref[g, j], dzs[g]) for g in hs]
            for g in hs:
                dk_ref[j, g * HEAD_DIM:(g + 1) * HEAD_DIM, :] += _dot_nt(qts[g], dzs[g])
            for g in hs:
                dv_ref[j, g * HEAD_DIM:(g + 1) * HEAD_DIM, :] += _dot_nt(douts[g], avs[g].astype(_MXU))
            return tuple((carry[g][0] + _colsum(ets[g]), dqs[g]) for g in hs)

        def live(j):
            jj = jnp.maximum(j, 0)
            top = rs_ref[0, 0, jj][0:1, :]
            for g in range(1, G):
                top = jnp.maximum(top, rs_ref[g, 0, jj][0:1, :])
            return jnp.logical_and(j >= 0, jnp.max(top) >= SB_DEAD)

        first = lax.while_loop(lambda st: st[1], lambda st: (st[0] - 1, live(st[0] - 2)), (i, live(i - 1)))[0]
        carry = tuple((jnp.zeros((1, T), F32), jnp.zeros((HEAD_DIM, T), F32)) for _ in range(G))
        carry = lax.fori_loop(first, i, lambda s, cr: blk(s, cr, None), carry)
        carry = blk(i, carry, causal)
        dq_ref[...] = (jnp.concatenate([carry[g][1] for g in range(G)], axis=0) * SCALE).T.astype(dq_ref.dtype)

        @pl.when(i == pl.num_programs(1) - 1)
        def _():
            def flush(j, _):
                rows = pl.ds(pl.multiple_of(j * T, T), T)
                dk_out_ref[rows, :] = dk_ref[j].T.astype(dk_out_ref.dtype)
                dv_out_ref[rows, :] = dv_ref[j].T.astype(dv_out_ref.dtype)
                return 0
            lax.fori_loop(0, nk, flush, 0)

        @pl.when(last_step)
        def _():
            scatter.finish()

    colblk = pl.BlockSpec((G, HEAD_DIM, T), lambda h, i: (h, 0, i))
    once = pl.Buffered(1)
    kblk = pl.BlockSpec((G, nk, T, HEAD_DIM), lambda h, i: (h, 0, 0, 0), pipeline_mode=once)
    kTblk = pl.BlockSpec((G, nk, HEAD_DIM, T), lambda h, i: (h, 0, 0, 0), pipeline_mode=once)
    any_spec = pl.BlockSpec(memory_space=pl.ANY)
    res = pl.pallas_call(
        body, name="sb_bwd", grid=(Hh // G, nq),
        in_specs=[colblk, kblk, kTblk, kblk, colblk,
                  pl.BlockSpec((G, 1, nk, 8, T), lambda h, i: (h, i, 0, 0, 0))] + [any_spec] * nw,
        out_specs=[pl.BlockSpec((T, G * HEAD_DIM), lambda h, i: (i, h)),
                   pl.BlockSpec((S, G * HEAD_DIM), lambda h, i: (0, h), pipeline_mode=once),
                   pl.BlockSpec((S, G * HEAD_DIM), lambda h, i: (0, h), pipeline_mode=once)] + [any_spec] * nw,
        out_shape=[_sds((S, Hh * HEAD_DIM), _MXU)] * 3 + s_outs,
        scratch_shapes=[pltpu.VMEM((nk, G * HEAD_DIM, T), F32), pltpu.VMEM((nk, G * HEAD_DIM, T), F32)] + s_sems,
        compiler_params=_cp(("arbitrary", "arbitrary"), vmem_mb=60),
    )(qT, kb, kTb, vb, doT, rsave, *s_ins)
    return res[0], res[1], res[2], list(res[3:])


def _bucket_table():
    qi = np.arange(BLOCK)[:, None]
    cj = np.arange(2 * BLOCK)[None, :]
    dist = qi + BLOCK - cj
    exact = REL_BUCKETS // 2
    d = np.maximum(dist, 0)
    d_f = np.maximum(d, 1).astype(np.float32)
    large = exact + (np.log(d_f / np.float32(exact)) / np.float32(math.log(REL_MAX_DIST / exact))
                     * np.float32(REL_BUCKETS - exact)).astype(np.int32)
    large = np.minimum(large, REL_BUCKETS - 1)
    return np.where(d < exact, d, large).astype(np.int32)


def _swa_bias(rel_bias, bucket):
    def body(rb_ref, bk_ref, o_ref):
        bk = bk_ref[...]
        for h in range(SWA_HEADS):
            t = jnp.zeros((2 * BLOCK, BLOCK), F32)
            for b in range(REL_BUCKETS):
                t = jnp.where(bk == b, rb_ref[b, h], t)
            o_ref[h] = t

    return pl.pallas_call(
        body, name="swa_bias",
        in_specs=[pl.BlockSpec(memory_space=pltpu.SMEM), pl.BlockSpec(memory_space=pltpu.VMEM)],
        out_specs=pl.BlockSpec(memory_space=pltpu.VMEM),
        out_shape=_sds((SWA_HEADS, 2 * BLOCK, BLOCK), F32),
    )(rel_bias, bucket)


def _swa_logits(q, kp, kc):
    qs = (q.astype(F32) * SCALE).astype(_MXU)
    return qs, _dot_nt(kp, qs), _dot_nt(kc, qs)


def _swa_softmax(lp, lc, bias, sink, live_prev):
    r, c = _tri_masks(BLOCK)
    in_window = r > c if live_prev is None else jnp.logical_and(r > c, live_prev)
    lp = jnp.where(in_window, lp + bias[:BLOCK, :], -jnp.inf)
    lc = jnp.where(r <= c, lc + bias[BLOCK:, :], -jnp.inf)
    m = jnp.maximum(jnp.maximum(jnp.max(lp, axis=0, keepdims=True), jnp.max(lc, axis=0, keepdims=True)), sink)
    pp = jnp.exp(lp - m)
    pc = jnp.exp(lc - m)
    ps = jnp.exp(sink - m)
    denom = _colsum(pp) + _colsum(pc) + ps
    return pp / denom, pc / denom, ps / denom


def _swa_sub(nb):
    return min(SWA_SUB, nb)


def _swa_keys(b, prev_ref, cur_ref, i):
    cur = cur_ref[0, b * BLOCK:(b + 1) * BLOCK, :]
    if b == 0:
        return prev_ref[0], cur, i > 0
    return cur_ref[0, (b - 1) * BLOCK:b * BLOCK, :], cur, None


def _swa_keys_t(b, prev_ref, cur_ref):
    cur = cur_ref[0, :, b * BLOCK:(b + 1) * BLOCK]
    return (prev_ref[0] if b == 0 else cur_ref[0, :, (b - 1) * BLOCK:b * BLOCK]), cur


def _swa_fwd(q, k, vT, bias, sink):
    S = q.shape[1]
    nb = S // BLOCK
    ns = _swa_sub(nb)
    R = ns * BLOCK

    def body(q_ref, kp_ref, kc_ref, vp_ref, vc_ref, bias_ref, sink_ref, o_ref):
        i = pl.program_id(1)
        bias = bias_ref[0]
        sink = sink_ref[0][:, :1]
        subs = range(ns)
        keys = [_swa_keys(b, kp_ref, kc_ref, i) for b in subs]
        vals = [_swa_keys_t(b, vp_ref, vc_ref) for b in subs]
        logits = [_swa_logits(q_ref[0, b * BLOCK:(b + 1) * BLOCK, :], keys[b][0], keys[b][1]) for b in subs]
        ws = [_swa_softmax(logits[b][1], logits[b][2], bias, sink, keys[b][2]) for b in subs]
        for b in subs:
            o_ref[0, :, b * BLOCK:(b + 1) * BLOCK] = (_dot(vals[b][0], ws[b][0].astype(_MXU))
                                                      + _dot(vals[b][1], ws[b][1].astype(_MXU)))

    prev = pl.BlockSpec((1, BLOCK, HEAD_DIM), lambda h, i: (h // SWA_GROUP, jnp.maximum(i * ns - 1, 0), 0))
    cur = pl.BlockSpec((1, R, HEAD_DIM), lambda h, i: (h // SWA_GROUP, i, 0))
    prev_t = pl.BlockSpec((1, HEAD_DIM, BLOCK), lambda h, i: (h // SWA_GROUP, 0, jnp.maximum(i * ns - 1, 0)))
    cur_t = pl.BlockSpec((1, HEAD_DIM, R), lambda h, i: (h // SWA_GROUP, 0, i))
    return pl.pallas_call(
        body, name="swa_fwd", grid=(SWA_HEADS, nb // ns),
        in_specs=[pl.BlockSpec((1, R, HEAD_DIM), lambda h, i: (h, i, 0)), prev, cur, prev_t, cur_t,
                  pl.BlockSpec((1, 2 * BLOCK, BLOCK), lambda h, i: (h, 0, 0)),
                  pl.BlockSpec((1, 1, BLOCK), lambda h, i: (h, 0, 0))],
        out_specs=pl.BlockSpec((1, HEAD_DIM, R), lambda h, i: (h, 0, i)),
        out_shape=_sds((SWA_HEADS, HEAD_DIM, S), F32),
        compiler_params=_cp(("parallel", "parallel")),
    )(q, k, k, vT, vT, bias, sink)


def _swa_bwd(q, k, kT, v, bias, sink, do, grads):
    S = q.shape[1]
    nb = S // BLOCK
    ns = _swa_sub(nb)
    R = ns * BLOCK
    nw = len(grads)
    x_ins, x_outs, x_sems = _swap_io(grads)

    def body(q_ref, kp_ref, kc_ref, ktp_ref, ktc_ref, vp_ref, vc_ref, bias_ref, sink_ref, do_ref, *rest):
        dq_ref, dk_ref, dv_ref, dbias_ref, dsink_ref = rest[nw:nw + 5]
        swap = _Swap(rest[:nw], rest[nw + 5:2 * nw + 5], *rest[2 * nw + 5:])
        g = pl.program_id(1)
        i = pl.program_id(2)
        first_step = jnp.logical_and(pl.program_id(0) == 0, jnp.logical_and(g == 0, i == 0))
        last_step = jnp.logical_and(pl.program_id(0) == pl.num_programs(0) - 1,
                                    jnp.logical_and(g == pl.num_programs(1) - 1, i == pl.num_programs(2) - 1))

        @pl.when(first_step)
        def _():
            swap.start()

        @pl.when(jnp.logical_and(g == 0, i == 0))
        def _():
            dk_ref[...] = jnp.zeros_like(dk_ref)
            dv_ref[...] = jnp.zeros_like(dv_ref)

        @pl.when(i == 0)
        def _():
            dbias_ref[...] = jnp.zeros_like(dbias_ref)
            dsink_ref[...] = jnp.zeros_like(dsink_ref)

        bias = bias_ref[0]
        sink = sink_ref[0][:, :1]
        subs = range(ns)
        rows = [slice(b * BLOCK, (b + 1) * BLOCK) for b in subs]
        keys = [_swa_keys(b, kp_ref, kc_ref, i) for b in subs]
        keys_t = [_swa_keys_t(b, ktp_ref, ktc_ref) for b in subs]
        vals = [_swa_keys(b, vp_ref, vc_ref, i) for b in subs]
        douts = [do_ref[0, rows[b], :] for b in subs]
        logits = [_swa_logits(q_ref[0, rows[b], :], keys[b][0], keys[b][1]) for b in subs]
        dws = [(_dot_nt(vals[b][0], douts[b]), _dot_nt(vals[b][1], douts[b])) for b in subs]
        dbp = jnp.zeros((BLOCK, BLOCK), F32)
        dbc = jnp.zeros((BLOCK, BLOCK), F32)
        dsk = jnp.zeros((1, BLOCK), F32)
        wts, dls = [], []
        for b in subs:
            wp, wc, ws = _swa_softmax(logits[b][1], logits[b][2], bias, sink, keys[b][2])
            dwp, dwc = dws[b]
            delta = _colsum(wp * dwp) + _colsum(wc * dwc)
            dlp = wp * (dwp - delta)
            dlc = wc * (dwc - delta)
            dbp += dlp
            dbc += dlc
            dsk -= ws * delta
            wts.append((wp.astype(_MXU), wc.astype(_MXU)))
            dls.append((dlp.astype(_MXU), dlc.astype(_MXU)))
        for b in subs:
            dq_ref[0, :, rows[b]] = (_dot(keys_t[b][0], dls[b][0]) + _dot(keys_t[b][1], dls[b][1])) * SCALE
        for b in subs:
            qs = logits[b][0]
            blk = i * ns + b
            dk_ref[0, blk] += _dot(dls[b][1], qs)
            dv_ref[0, blk] += _dot(wts[b][1], douts[b])
            if b == 0:
                @pl.when(i > 0)
                def _():
                    dk_ref[0, blk - 1] += _dot(dls[0][0], qs)
                    dv_ref[0, blk - 1] += _dot(wts[0][0], douts[0])
            else:
                dk_ref[0, blk - 1] += _dot(dls[b][0], qs)
                dv_ref[0, blk - 1] += _dot(wts[b][0], douts[b])
        dbias_ref[0, :BLOCK, :] += dbp
        dbias_ref[0, BLOCK:, :] += dbc
        dsink_ref[0] += jnp.broadcast_to(dsk, (8, BLOCK))

        @pl.when(last_step)
        def _():
            swap.finish()

    hq = lambda kv, g, i: kv * SWA_GROUP + g
    prev = pl.BlockSpec((1, BLOCK, HEAD_DIM), lambda kv, g, i: (kv, jnp.maximum(i * ns - 1, 0), 0))
    cur = pl.BlockSpec((1, R, HEAD_DIM), lambda kv, g, i: (kv, i, 0))
    prev_t = pl.BlockSpec((1, HEAD_DIM, BLOCK), lambda kv, g, i: (kv, 0, jnp.maximum(i * ns - 1, 0)))
    cur_t = pl.BlockSpec((1, HEAD_DIM, R), lambda kv, g, i: (kv, 0, i))
    qblk = pl.BlockSpec((1, R, HEAD_DIM), lambda kv, g, i: (hq(kv, g, i), i, 0))
    qblk_t = pl.BlockSpec((1, HEAD_DIM, R), lambda kv, g, i: (hq(kv, g, i), 0, i))
    kvacc = pl.BlockSpec((1, nb, BLOCK, HEAD_DIM), lambda kv, g, i: (kv, 0, 0, 0))
    any_spec = pl.BlockSpec(memory_space=pl.ANY)
    res = pl.pallas_call(
        body, name="swa_bwd", grid=(SWA_KV_HEADS, SWA_GROUP, nb // ns),
        in_specs=[qblk, prev, cur, prev_t, cur_t, prev, cur,
                  pl.BlockSpec((1, 2 * BLOCK, BLOCK), lambda kv, g, i: (hq(kv, g, i), 0, 0)),
                  pl.BlockSpec((1, 1, BLOCK), lambda kv, g, i: (hq(kv, g, i), 0, 0)), qblk] + [any_spec] * nw,
        out_specs=[qblk_t, kvacc, kvacc,
                   pl.BlockSpec((1, 2 * BLOCK, BLOCK), lambda kv, g, i: (hq(kv, g, i), 0, 0)),
                   pl.BlockSpec((1, 8, BLOCK), lambda kv, g, i: (hq(kv, g, i), 0, 0))] + [any_spec] * nw,
        out_shape=[_sds((SWA_HEADS, HEAD_DIM, S), F32), _sds((SWA_KV_HEADS, nb, BLOCK, HEAD_DIM), F32),
                   _sds((SWA_KV_HEADS, nb, BLOCK, HEAD_DIM), F32), _sds((SWA_HEADS, 2 * BLOCK, BLOCK), F32),
                   _sds((SWA_HEADS, 8, BLOCK), F32)] + x_outs,
        scratch_shapes=x_sems,
        compiler_params=_cp(("arbitrary", "arbitrary", "arbitrary")),
    )(q, k, k, kT, kT, v, v, bias, sink, do, *x_ins)
    return res[0], res[1], res[2], res[3], res[4], list(res[5:])


def _swa_small_grads(dbias, dsink, bucket):
    rows = REL_BUCKETS + 8

    def total(x):
        return jnp.sum(jnp.sum(x, axis=1, keepdims=True), axis=0, keepdims=True)

    def body(db_ref, ds_ref, bk_ref, o_ref):
        bk = bk_ref[...]
        r = lax.broadcasted_iota(jnp.int32, (rows, BLOCK), 0)
        c = lax.broadcasted_iota(jnp.int32, (rows, BLOCK), 1)
        out = jnp.zeros((rows, BLOCK), F32)
        for h in range(SWA_HEADS):
            db = db_ref[h]
            for b in range(REL_BUCKETS):
                s = total(jnp.where(bk == b, db, 0.0))
                out = jnp.where(jnp.logical_and(r == b, c == h), s, out)
            s = jnp.sum(ds_ref[h][0:1, :], axis=1, keepdims=True)
            out = jnp.where(jnp.logical_and(r == REL_BUCKETS, c == h), s, out)
        o_ref[...] = out

    vm = pl.BlockSpec(memory_space=pltpu.VMEM)
    return pl.pallas_call(body, name="swa_small_grads", in_specs=[vm, vm, vm], out_specs=vm,
                          out_shape=_sds((rows, BLOCK), F32))(dbias, dsink, bucket)


def _tile_rows(n):
    for t in (512, 352, 256, 176, 128, 64, 32, 16, 8):
        if n % t == 0:
            return t
    return n


def _cast_rows(x, dtype, name):
    R, C = x.shape
    tr = _tile_rows(R)

    def body(x_ref, o_ref):
        o_ref[...] = x_ref[...].astype(o_ref.dtype)

    return pl.pallas_call(body, name=name, grid=(R // tr,), in_specs=[_rows(tr, C)], out_specs=_rows(tr, C),
                          out_shape=_sds((R, C), dtype), compiler_params=_cp(("parallel",)))(x)


def _pair_sum(g, recv, c, name):
    n, half, C = recv.shape
    tr = _tile_rows(half)

    def body(c_ref, a_ref, b_ref, o_ref):
        o_ref[...] = (a_ref[0] + b_ref[...]).astype(o_ref.dtype)

    return pl.pallas_call(
        body, name=name,
        grid_spec=pltpu.PrefetchScalarGridSpec(
            num_scalar_prefetch=1, grid=(n, half // tr),
            in_specs=[pl.BlockSpec((1, 1, tr, C), lambda j, i, c_ref: (j, c_ref[0], i, 0)),
                      pl.BlockSpec((1, tr, C), lambda j, i, c_ref: (j, i, 0))],
            out_specs=pl.BlockSpec((1, tr, C), lambda j, i, c_ref: (j, i, 0))),
        out_shape=_sds((n, half, C), _MXU),
        compiler_params=_cp(("parallel", "parallel")))(c.reshape(1), g.reshape(n, 2, half, C), recv)


def _chip_sum(own, recv, me, name):
    n, R, C = recv.shape
    tr = _tile_rows(R)

    def body(me_ref, own_ref, recv_ref, o_ref):
        acc = None
        for j in range(n):
            term = jnp.where(me_ref[0] == j, own_ref[0], recv_ref[j]).astype(F32)
            acc = term if acc is None else acc + term
        o_ref[...] = acc

    return pl.pallas_call(
        body, name=name,
        grid_spec=pltpu.PrefetchScalarGridSpec(
            num_scalar_prefetch=1, grid=(R // tr,),
            in_specs=[pl.BlockSpec((1, tr, C), lambda i, me_ref: (me_ref[0], i, 0)),
                      pl.BlockSpec((n, tr, C), lambda i, me_ref: (0, i, 0))],
            out_specs=pl.BlockSpec((tr, C), lambda i, me_ref: (i, 0))),
        out_shape=_sds((R, C), F32), compiler_params=_cp(("parallel",)))(me.reshape(1), own, recv)


def _adamw_math(w, g, m, v):
    m = ADAM_B1 * m + (1.0 - ADAM_B1) * g
    v = ADAM_B2 * v + (1.0 - ADAM_B2) * (g * g)
    m_hat = m / (1.0 - ADAM_B1 ** ADAM_STEP)
    v_hat = v / (1.0 - ADAM_B2 ** ADAM_STEP)
    delta = -ADAM_LR * (m_hat / (jnp.sqrt(v_hat) + ADAM_EPS) + ADAM_WD * w)
    return delta, m, v


def _adamw(w, g, m, v, name):
    R, C = w.shape
    tr = _tile_rows(R)

    def body(w_ref, g_ref, m_ref, v_ref, d_ref, nm_ref, nv_ref):
        d, nm, nv = _adamw_math(w_ref[...], g_ref[...], m_ref[...], v_ref[...])
        d_ref[...] = d
        nm_ref[...] = nm
        nv_ref[...] = nv

    blk = _rows(tr, C)
    return pl.pallas_call(body, name=name, grid=(R // tr,), in_specs=[blk] * 4, out_specs=[blk] * 3,
                          out_shape=[_sds((R, C), F32)] * 3, compiler_params=_cp(("parallel",)))(w, g, m, v)


def _gather_weights(shards):
    nw = len(shards)
    ins, outs, sems = _gather_io(shards)

    def body(*refs):
        ex = _Gather(refs[:nw], refs[nw:2 * nw], *refs[2 * nw:])
        ex.start()
        ex.finish()

    any_spec = pl.BlockSpec(memory_space=pl.ANY)
    got = pl.pallas_call(body, name="gather_weights", in_specs=[any_spec] * nw, out_specs=[any_spec] * nw,
                         out_shape=outs, scratch_shapes=sems)(*ins)
    return _gather_assemble(got, shards)


def _swap_halves(grads, name):
    nw = len(grads)
    ins, outs, sems = _swap_io(grads)

    def body(*refs):
        ex = _Swap(refs[:nw], refs[nw:2 * nw], *refs[2 * nw:])
        ex.start()
        ex.finish()

    any_spec = pl.BlockSpec(memory_space=pl.ANY)
    return pl.pallas_call(body, name=name, in_specs=[any_spec] * nw, out_specs=[any_spec] * nw,
                          out_shape=outs, scratch_shapes=sems)(*ins)


def _scatter_partials(parts):
    nw = len(parts)
    ins, outs, sems = _scatter_io(parts)

    def body(*refs):
        ex = _Scatter(refs[:nw], refs[nw:2 * nw], *refs[2 * nw:])
        ex.start()
        ex.finish()

    any_spec = pl.BlockSpec(memory_space=pl.ANY)
    return pl.pallas_call(body, name="scatter_partials", in_specs=[any_spec] * nw, out_specs=[any_spec] * nw,
                          out_shape=outs, scratch_shapes=sems)(*ins)


def _join_halves(sums):
    nw = len(sums)

    def body(*refs):
        f_refs, out_refs = refs[:nw], refs[nw:2 * nw]
        send_sems, recv_sems = refs[2 * nw:]
        x, y, c, _ = _place()
        ws = range(nw)

        def copy(w, half_index):
            return pltpu.make_async_remote_copy(
                src_ref=f_refs[w], dst_ref=out_refs[w].at[half_index], send_sem=send_sems.at[w],
                recv_sem=recv_sems.at[w], device_id=(x, y, 1 - c), device_id_type=MESH)

        sends = [copy(w, c) for w in ws]
        for cp in sends:
            cp.start()
        for w in ws:
            copy(w, 1 - c).wait_recv()
        for cp in sends:
            cp.wait_send()

    any_spec = pl.BlockSpec(memory_space=pl.ANY)
    outs = pl.pallas_call(
        body, name="join_halves", in_specs=[any_spec] * nw, out_specs=[any_spec] * nw,
        out_shape=[_sds((2,) + f.shape, f.dtype) for f in sums],
        scratch_shapes=[pltpu.SemaphoreType.DMA((nw,)), pltpu.SemaphoreType.DMA((nw,))],
    )(*sums)
    c = lax.axis_index("c")
    return [lax.dynamic_update_slice_in_dim(o, f[None], c, axis=0).reshape(2 * f.shape[0], f.shape[1])
            for o, f in zip(outs, sums)]


def _allreduce_small(block):
    m_per, n = block.shape

    def body(x_ref, sum_ref, loss_ref, all_ref, send_sems, recv_sems, local_sem):
        x, y, c, chips = _place()
        me, sibling = (x, y, c), (x, y, 1 - c)

        def rows(px, py, pc):
            return all_ref.at[pl.ds(pl.multiple_of((4 * px + 2 * py + pc) * m_per, 8), m_per), :]

        def copy(k, blk, to, src=None):
            return pltpu.make_async_remote_copy(
                src_ref=rows(*blk) if src is None else src, dst_ref=rows(*blk), send_sem=send_sems.at[k],
                recv_sem=recv_sems.at[k], device_id=to, device_id_type=MESH)

        mine = pltpu.make_async_copy(x_ref, rows(*me), local_sem)
        mine.start()
        first = [copy(0, me, sibling, src=x_ref)]
        first += [copy(1 + j, me, (*chip, c), src=x_ref) for j, chip in enumerate(chips)]
        for cp in first:
            cp.start()
        passed = [copy(4 + j, (*chip, c), sibling) for j, chip in enumerate(chips)]
        for j, chip in enumerate(chips):
            copy(1 + j, (*chip, c), me).wait_recv()
            passed[j].start()
        copy(0, sibling, me).wait_recv()
        for j, chip in enumerate(chips):
            copy(4 + j, (*chip, 1 - c), me).wait_recv()
        for cp in first + passed:
            cp.wait_send()
        mine.wait()

        acc = all_ref[0:m_per, :]
        for d in range(1, 8):
            acc = acc + all_ref[d * m_per:(d + 1) * m_per, :]
        sum_ref[...] = acc
        tot = jnp.sum(acc[8:9, :], axis=1, keepdims=True) * (0.5 / D_MODEL)
        loss_ref[...] = jnp.broadcast_to(tot, loss_ref.shape)

    vm = pl.BlockSpec(memory_space=pltpu.VMEM)
    return pl.pallas_call(
        body, name="allreduce_small", in_specs=[vm], out_specs=[vm, vm],
        out_shape=[_sds((m_per, n), F32), _sds((8, 128), F32)],
        scratch_shapes=[pltpu.VMEM((8 * m_per, n), F32), pltpu.SemaphoreType.DMA((7,)), pltpu.SemaphoreType.DMA((7,)),
                        pltpu.SemaphoreType.DMA],
    )(block)


def _heads_rows(x, nh):
    S = x.shape[0]
    return x.reshape(S, nh, HEAD_DIM).transpose(1, 0, 2)


def _heads_cols(x, nh):
    S = x.shape[0]
    return x.reshape(S, nh, HEAD_DIM).transpose(1, 2, 0)


def _key_blocks(x, nh, t):
    S = x.shape[0]
    return x.reshape(S // t, t, nh, HEAD_DIM).transpose(2, 0, 1, 3)


def _key_blocks_t(x, nh, t):
    S = x.shape[0]
    return x.reshape(S // t, t, nh, HEAD_DIM).transpose(2, 0, 3, 1)


def _pad_row(v):
    v = v.reshape(1, -1)
    return jnp.pad(v, ((0, 0), (0, D_MODEL - v.shape[1])))


def _pack_small(ln_in_g, ln_in_b, sb_g, swa_g, sinks, rel_bias, ln1_g, ln1_b, ln2_g, ln2_b, extra):
    rows = [_pad_row(ln_in_g), _pad_row(ln_in_b), jnp.concatenate([sb_g.reshape(1, -1), swa_g.reshape(1, -1)], axis=1),
            _pad_row(jnp.concatenate([rel_bias.reshape(1, -1), sinks.reshape(1, -1)], axis=1)),
            _pad_row(ln1_g), _pad_row(ln1_b), _pad_row(ln2_g), _pad_row(ln2_b), _pad_row(extra)]
    rows.append(jnp.zeros((SMALL_ROWS - len(rows), D_MODEL), F32))
    return jnp.concatenate(rows, axis=0)


def _unpack_small(blk):
    nrb = REL_BUCKETS * SWA_HEADS
    return (blk[0], blk[1], blk[2:3, :SB_WIDTH], blk[2:3, SB_WIDTH:], blk[3:4, nrb:nrb + SWA_HEADS],
            blk[3, :nrb].reshape(REL_BUCKETS, SWA_HEADS), blk[4:5], blk[5:6], blk[6:7], blk[7:8])


def kernel(x, ln_in_g, ln_in_b, w_in, sb_norm_g, swa_norm_g, sinks, rel_bias, w_out, ln1_g, ln1_b, w_gate_up, w_down, ln2_g, ln2_b, loss_target, m_ln_in_g, m_ln_in_b, m_w_in, m_sb_norm_g, m_swa_norm_g, m_sinks, m_rel_bias, m_w_out, m_ln1_g, m_ln1_b, m_w_gate_up, m_w_down, m_ln2_g, m_ln2_b, v_ln_in_g, v_ln_in_b, v_w_in, v_sb_norm_g, v_swa_norm_g, v_sinks, v_rel_bias, v_w_out, v_ln1_g, v_ln1_b, v_w_gate_up, v_w_down, v_ln2_g, v_ln2_b):
    S = x.shape[1]
    x2 = x.reshape(S, D_MODEL)
    tgt = loss_target.reshape(S, D_MODEL)
    T = min(S, SB_TILE)
    bucket = jnp.asarray(_bucket_table().T)
    row = lambda v: v.reshape(1, -1)

    shards = [_cast_rows(w[0], _MXU, "cast_" + n) for n, w in (("w_in", w_in), ("w_out", w_out), ("w_gate_up", w_gate_up), ("w_down", w_down))]
    (w_in_sh,) = _gather_weights(shards[:1])
    w_in_f = jnp.concatenate([w_in_sh[j] for j in range(N_CHIPS)], axis=1)

    h0, h0b, proj = _ln_in_proj(x2, row(ln_in_g), row(ln_in_b), w_in_f)
    o1, o2, o3, o4, o5 = SB_WIDTH, 2 * SB_WIDTH, 3 * SB_WIDTH, 3 * SB_WIDTH + SWA_WIDTH, 3 * SB_WIDTH + SWA_WIDTH + SWA_KV_WIDTH
    q_sb, k_sb, v_sb = proj[:, :o1], proj[:, o1:o2], proj[:, o2:o3]
    q_sw, k_sw, v_sw = proj[:, o3:o4], proj[:, o4:o5], proj[:, o5:]
    qT_sb = _heads_cols(q_sb, SB_HEADS)
    kb_sb = _key_blocks(k_sb, SB_HEADS, T)
    sb_out, rsave, (w_out_sh, w_gu_sh, w_down_sh) = _sb_fwd(qT_sb, kb_sb, _key_blocks_t(v_sb, SB_HEADS, T), shards[1:])
    w_out_f = w_out_sh.reshape(D_MODEL, D_MODEL)
    w_down_f = w_down_sh.reshape(D_FF, D_MODEL)

    bias = _swa_bias(rel_bias, bucket)
    sink_rows = jnp.broadcast_to(sinks.reshape(SWA_HEADS, 1, 1), (SWA_HEADS, 1, BLOCK))
    qh_sw, kh_sw, vh_sw = _heads_rows(q_sw, SWA_HEADS), _heads_rows(k_sw, SWA_KV_HEADS), _heads_rows(v_sw, SWA_KV_HEADS)
    oT_sw = _swa_fwd(qh_sw, kh_sw, _heads_cols(v_sw, SWA_KV_HEADS), bias, sink_rows)
    swa_out = oT_sw.transpose(2, 0, 1).reshape(S, SWA_WIDTH)

    pre1, merged, h1b = _mix_out(sb_out, swa_out, sb_norm_g, swa_norm_g, w_out_f, h0, ln1_g, ln1_b)
    act, silu, dsilu_up = _ffn_up(h1b, w_gu_sh)
    dp2, dp2b, dg2, db2, errsum = _ffn_down_loss(act, w_down_f, pre1, ln1_g, ln1_b, ln2_g, ln2_b, tgt)

    g_w_down = _matmul_tn(act, dp2b, "grad_w_down", FF_CHUNK, D_MODEL)
    dgate, dup = _ffn_down_bwd(dp2b, w_down_f, silu, dsilu_up)
    g_w_gu = _matmul_tn_pair(h1b, dgate, dup, "grad_w_gate_up")
    dp1, dp1b, dg1, db1 = _ffn_up_bwd(dgate, dup, w_gu_sh, dp2, pre1, ln1_g)
    g_w_out = _matmul_tn(merged, dp1b, "grad_w_out", D_MODEL, D_MODEL)
    dsb, dsw, dgsb, dgsw = _mix_bwd(dp1b, w_out_f, sb_out, swa_out, sb_norm_g, swa_norm_g)

    c = lax.axis_index("c").astype(jnp.int32)
    me = (2 * lax.axis_index("x") + lax.axis_index("y")).astype(jnp.int32)
    grads_a = [g_w_out.reshape(N_CHIPS, D_MODEL // N_CHIPS, D_MODEL), g_w_gu, g_w_down.reshape(N_CHIPS, D_FF // N_CHIPS, D_MODEL)]
    names_a = ("w_out", "w_gate_up", "w_down")
    dqT_sw, dkh_sw, dvh_sw, dbias, dsink, swapped_a = _swa_bwd(qh_sw, kh_sw, _heads_cols(k_sw, SWA_KV_HEADS), vh_sw, bias,
                                                               sink_rows, _heads_rows(dsw, SWA_HEADS), grads_a)
    swa_small = _swa_small_grads(dbias, dsink, bucket)
    partials_a = [_pair_sum(g, r, c, "pair_sum_" + n) for g, r, n in zip(grads_a, swapped_a, names_a)]
    dq_sb, dk_sb, dv_sb, recv_a = _sb_bwd(qT_sb, kb_sb, _key_blocks_t(k_sb, SB_HEADS, T), _key_blocks(v_sb, SB_HEADS, T),
                                             _heads_cols(dsb, SB_HEADS), rsave, partials_a)
    tok = lambda t, nh: t.reshape(nh, S, HEAD_DIM).transpose(1, 0, 2).reshape(S, nh * HEAD_DIM)
    dproj = jnp.concatenate([dq_sb, dk_sb, dv_sb,
                             dqT_sw.transpose(2, 0, 1).reshape(S, SWA_WIDTH), tok(dkh_sw, SWA_KV_HEADS), tok(dvh_sw, SWA_KV_HEADS)],
                            axis=1).astype(_MXU)
    g_w_in = _matmul_tn(h0b, dproj, "grad_w_in", D_MODEL, IN_COLS // 2)

    cin = IN_COLS // N_CHIPS
    grads_b = [jnp.stack([g_w_in[:, j * cin:(j + 1) * cin] for j in range(N_CHIPS)])]
    partials_b = [_pair_sum(grads_b[0], _swap_halves(grads_b, "swap_halves_in")[0], c, "pair_sum_w_in")]
    grad_x, dg_in, db_in, recv_b = _in_proj_bwd(dproj, w_in_f, dp1, x2, row(ln_in_g), partials_b)
    names = ("w_in",) + names_a
    sums = [_chip_sum(p, r, me, "chip_sum_" + n) for p, r, n in zip(partials_b + partials_a, list(recv_b) + list(recv_a), names)]
    gs_in, gs_out, gs_gu, gs_down = _join_halves(sums)

    nrb = REL_BUCKETS * SWA_HEADS
    small = _pack_small(dg_in, db_in, dgsb, dgsw, swa_small[REL_BUCKETS, :SWA_HEADS],
                        swa_small[:REL_BUCKETS, :SWA_HEADS], dg1, db1, dg2, db2, errsum)
    g_small, loss_tile = _allreduce_small(small)
    loss = loss_tile[0, 0]

    big = []
    for name, w, g, m, v in (("adamw_w_in", w_in, gs_in, m_w_in, v_w_in), ("adamw_w_out", w_out, gs_out, m_w_out, v_w_out),
                             ("adamw_w_gate_up", w_gate_up, gs_gu, m_w_gate_up, v_w_gate_up),
                             ("adamw_w_down", w_down, gs_down, m_w_down, v_w_down)):
        d, nm, nv = _adamw(w[0], g, m[0], v[0], name)
        big.append((g[None], d[None], nm[None], nv[None]))
    zero = jnp.zeros((1,), F32)
    w_small = _pack_small(ln_in_g, ln_in_b, sb_norm_g, swa_norm_g, sinks, rel_bias, ln1_g, ln1_b, ln2_g, ln2_b, zero)
    m_small = _pack_small(m_ln_in_g, m_ln_in_b, m_sb_norm_g, m_swa_norm_g, m_sinks, m_rel_bias, m_ln1_g, m_ln1_b,
                          m_ln2_g, m_ln2_b, zero)
    v_small = _pack_small(v_ln_in_g, v_ln_in_b, v_sb_norm_g, v_swa_norm_g, v_sinks, v_rel_bias, v_ln1_g, v_ln1_b,
                          v_ln2_g, v_ln2_b, zero)
    small_out = [_unpack_small(t) for t in (g_small,) + tuple(_adamw(w_small, g_small, m_small, v_small, "adamw_small"))]

    def kind(k):
        s = small_out[k]
        return [s[0], s[1], big[0][k], s[2], s[3], s[4], s[5], big[1][k], s[6], s[7], big[2][k], big[3][k], s[8], s[9]]

    return (loss, grad_x.reshape(1, S, D_MODEL), *kind(0), *kind(1), *kind(2), *kind(3))
```

```python
import functools
import math

import numpy as np
import jax
import jax.numpy as jnp
from jax import lax
from jax.experimental import pallas as pl
from jax.experimental.pallas import tpu as pltpu

F32 = jnp.float32
_MXU = jnp.bfloat16

D_MODEL = 1024
HEAD_DIM = 64
SB_HEADS = 8
SWA_HEADS = 8
SWA_KV_HEADS = 2
SWA_GROUP = SWA_HEADS // SWA_KV_HEADS
SB_WIDTH = SB_HEADS * HEAD_DIM
SWA_WIDTH = SWA_HEADS * HEAD_DIM
SWA_KV_WIDTH = SWA_KV_HEADS * HEAD_DIM
IN_COLS = 3 * SB_WIDTH + SWA_WIDTH + 2 * SWA_KV_WIDTH
BLOCK = 128
REL_BUCKETS = 32
REL_MAX_DIST = 128
D_FF = 2816
FF_CHUNK = D_FF // 2
ALPHA = 2.0 ** 0.25
LN_EPS = 1e-5
RMS_EPS = 1e-6
SCALE = HEAD_DIM ** -0.5
SB_TILE = 256
SB_GROUP_FWD = 8
SB_GROUP_BWD = 4
SB_DEAD = -105.0
SWA_SUB = 8

ADAM_LR = 0.001
ADAM_B1 = 0.9
ADAM_B2 = 0.999
ADAM_EPS = 1e-08
ADAM_WD = 0.01
ADAM_STEP = 10

N_CHIPS = 4
SMALL_ROWS = 16

MESH = pl.DeviceIdType.MESH


def _sds(shape, dtype):
    return jax.ShapeDtypeStruct(shape, dtype)


def _cp(sem=None, vmem_mb=48):
    kw = dict(vmem_limit_bytes=vmem_mb * 1024 * 1024)
    if sem is not None:
        kw["dimension_semantics"] = sem
    return pltpu.CompilerParams(**kw)


def _dot(a, b):
    return jnp.dot(a, b, preferred_element_type=F32)


def _dot_nt(a, b):
    return lax.dot_general(a, b, (((1,), (1,)), ((), ())), preferred_element_type=F32)


def _dot_tn(a, b):
    return lax.dot_general(a, b, (((0,), (0,)), ((), ())), preferred_element_type=F32)


def _ln_hat(x):
    mu = jnp.mean(x, axis=-1, keepdims=True)
    xc = x - mu
    var = jnp.mean(xc * xc, axis=-1, keepdims=True)
    rstd = lax.rsqrt(var + LN_EPS)
    return xc * rstd, rstd


def _ln_bwd(xhat, rstd, dy, g):
    dxh = dy * g
    m1 = jnp.mean(dxh, axis=-1, keepdims=True)
    m2 = jnp.mean(dxh * xhat, axis=-1, keepdims=True)
    return rstd * (dxh - m1 - xhat * m2)


def _colsum(x):
    return jnp.sum(x, axis=0, keepdims=True)


def _split2(x):
    hi = x.astype(_MXU)
    lo = (x - hi.astype(F32)).astype(_MXU)
    return hi, lo


def _rows(tm, n):
    return pl.BlockSpec((tm, n), lambda i: (i, 0))


def _fixed(*shape):
    nd = len(shape)
    return pl.BlockSpec(shape, lambda i: (0,) * nd)


IN_SECTIONS = (SB_WIDTH, SB_WIDTH, SB_WIDTH, SWA_WIDTH, 2 * SWA_KV_WIDTH)


def _ln_in_proj(x, g, b, w):
    S = x.shape[0]
    tm = min(S, 512)
    offs = np.cumsum((0,) + IN_SECTIONS)

    def body(x_ref, g_ref, b_ref, w_ref, h_ref, hb_ref, *p_refs):
        xhat, _ = _ln_hat(x_ref[...])
        h = xhat * g_ref[...] + b_ref[...]
        h_ref[...] = h
        hb = h.astype(_MXU)
        hb_ref[...] = hb
        proj = _dot(hb, w_ref[...])
        for k, p_ref in enumerate(p_refs):
            p_ref[...] = proj[:, offs[k]:offs[k + 1]].astype(p_ref.dtype)

    return pl.pallas_call(
        body, name="ln_in_proj", grid=(S // tm,),
        in_specs=[_rows(tm, D_MODEL), _fixed(1, D_MODEL), _fixed(1, D_MODEL), _fixed(D_MODEL, IN_COLS)],
        out_specs=[_rows(tm, D_MODEL), _rows(tm, D_MODEL)] + [_rows(tm, n) for n in IN_SECTIONS],
        out_shape=[_sds((S, D_MODEL), F32), _sds((S, D_MODEL), _MXU)] + [_sds((S, n), _MXU) for n in IN_SECTIONS],
        compiler_params=_cp(("parallel",)),
    )(x, g, b, w)


def _rms(x, g):
    r = lax.rsqrt(jnp.mean(x * x, axis=-1, keepdims=True) + RMS_EPS)
    return x * r * g, r


def _mix_out(sb, sw, gsb, gsw, w_out, h0, g1, b1):
    S = sb.shape[0]
    tm = min(S, 512)

    def body(sb_ref, sw_ref, gsb_ref, gsw_ref, w_ref, h0_ref, g1_ref, b1_ref, pre_ref, mg_ref, h1_ref):
        ysb, _ = _rms(sb_ref[...], gsb_ref[...])
        ysw, _ = _rms(sw_ref[...], gsw_ref[...])
        ysb = ysb.astype(_MXU)
        ysw = ysw.astype(_MXU)
        mg_ref[:, :SB_WIDTH] = ysb
        mg_ref[:, SB_WIDTH:] = ysw
        mix = _dot(ysb, w_ref[:SB_WIDTH, :]) + _dot(ysw, w_ref[SB_WIDTH:, :])
        pre1 = ALPHA * h0_ref[...] + mix
        pre_ref[...] = pre1
        xhat, _ = _ln_hat(pre1)
        h1_ref[...] = (xhat * g1_ref[...] + b1_ref[...]).astype(h1_ref.dtype)

    vec = _fixed(1, D_MODEL)
    return pl.pallas_call(
        body, name="mix_out", grid=(S // tm,),
        in_specs=[_rows(tm, SB_WIDTH), _rows(tm, SWA_WIDTH), _fixed(1, SB_WIDTH), _fixed(1, SWA_WIDTH),
                  _fixed(D_MODEL, D_MODEL), _rows(tm, D_MODEL), vec, vec],
        out_specs=[_rows(tm, D_MODEL), _rows(tm, D_MODEL), _rows(tm, D_MODEL)],
        out_shape=[_sds((S, D_MODEL), F32), _sds((S, D_MODEL), _MXU), _sds((S, D_MODEL), _MXU)],
        compiler_params=_cp(("parallel",)),
    )(sb, sw, gsb, gsw, w_out, h0, g1, b1)


def _sigmoid(x):
    return 1.0 / (1.0 + jnp.exp(-x))


def _ffn_up(h1b, wgu):
    S = h1b.shape[0]
    tm = min(S, 1024)

    def body(h_ref, wg_ref, wu_ref, a_ref, s1_ref, s2_ref):
        h1 = h_ref[...]
        gate = _dot(h1, wg_ref[0])
        up = _dot(h1, wu_ref[0])
        sg = _sigmoid(gate)
        silu = gate * sg
        a_ref[...] = (silu * up).astype(a_ref.dtype)
        s1_ref[...] = silu.astype(s1_ref.dtype)
        s2_ref[...] = (up * (sg * (1.0 + gate * (1.0 - sg)))).astype(s2_ref.dtype)

    chunk = pl.BlockSpec((tm, FF_CHUNK), lambda j, i: (i, j))
    return pl.pallas_call(
        body, name="ffn_up", grid=(2, S // tm),
        in_specs=[pl.BlockSpec((tm, D_MODEL), lambda j, i: (i, 0)),
                  pl.BlockSpec((1, D_MODEL, FF_CHUNK), lambda j, i: (j, 0, 0)),
                  pl.BlockSpec((1, D_MODEL, FF_CHUNK), lambda j, i: (j + 2, 0, 0))],
        out_specs=[chunk, chunk, chunk],
        out_shape=[_sds((S, D_FF), _MXU)] * 3,
        compiler_params=_cp(("arbitrary", "arbitrary"), vmem_mb=56),
    )(h1b, wgu, wgu)


def _ffn_down_loss(a, w_down, pre1, g1, b1, g2, b2, tgt):
    S = a.shape[0]
    tm = min(S, 512)

    def body(a_ref, w_ref, p_ref, g1_ref, b1_ref, g2_ref, b2_ref, t_ref, d_ref, db_ref, dg2_ref, db2_ref, err_ref):
        @pl.when(pl.program_id(0) == 0)
        def _():
            dg2_ref[...] = jnp.zeros_like(dg2_ref)
            db2_ref[...] = jnp.zeros_like(db2_ref)
            err_ref[...] = jnp.zeros_like(err_ref)

        xhat1, _ = _ln_hat(p_ref[...])
        h1 = xhat1 * g1_ref[...] + b1_ref[...]
        pre2 = ALPHA * h1 + _dot(a_ref[...], w_ref[...])
        xhat2, rstd2 = _ln_hat(pre2)
        err = xhat2 * g2_ref[...] + b2_ref[...] - t_ref[...]
        dh2 = err * (1.0 / D_MODEL)
        dp2 = _ln_bwd(xhat2, rstd2, dh2, g2_ref[...])
        d_ref[...] = dp2
        db_ref[...] = dp2.astype(db_ref.dtype)
        dg2_ref[...] += _colsum(dh2 * xhat2)
        db2_ref[...] += _colsum(dh2)
        err_ref[...] += _colsum(err * err)

    vec = _fixed(1, D_MODEL)
    return pl.pallas_call(
        body, name="ffn_down_loss", grid=(S // tm,),
        in_specs=[_rows(tm, D_FF), _fixed(D_FF, D_MODEL), _rows(tm, D_MODEL), vec, vec, vec, vec, _rows(tm, D_MODEL)],
        out_specs=[_rows(tm, D_MODEL), _rows(tm, D_MODEL), vec, vec, vec],
        out_shape=[_sds((S, D_MODEL), F32), _sds((S, D_MODEL), _MXU), _sds((1, D_MODEL), F32), _sds((1, D_MODEL), F32),
                   _sds((1, D_MODEL), F32)],
        compiler_params=_cp(("arbitrary",)),
    )(a, w_down, pre1, g1, b1, g2, b2, tgt)


def _ffn_down_bwd(dp2b, w_down, s1, s2):
    S = dp2b.shape[0]
    tm = min(S, 512)

    def body(d_ref, w_ref, s1_ref, s2_ref, dg_ref, du_ref):
        da = _dot_nt(d_ref[...], w_ref[...])
        du_ref[...] = (da * s1_ref[...].astype(F32)).astype(du_ref.dtype)
        dg_ref[...] = (da * s2_ref[...].astype(F32)).astype(dg_ref.dtype)

    chunk = pl.BlockSpec((tm, FF_CHUNK), lambda j, i: (i, j))
    return pl.pallas_call(
        body, name="ffn_down_bwd", grid=(2, S // tm),
        in_specs=[pl.BlockSpec((tm, D_MODEL), lambda j, i: (i, 0)),
                  pl.BlockSpec((FF_CHUNK, D_MODEL), lambda j, i: (j, 0)), chunk, chunk],
        out_specs=[chunk, chunk],
        out_shape=[_sds((S, D_FF), _MXU), _sds((S, D_FF), _MXU)],
        compiler_params=_cp(("arbitrary", "arbitrary")),
    )(dp2b, w_down, s1, s2)


def _ffn_up_bwd(dgate, dup, wgu, dp2, pre1, g1):
    S = dgate.shape[0]
    tm = min(S, 256)

    def body(dg_ref, du_ref, w_ref, d2_ref, p_ref, g_ref, d1_ref, d1b_ref, dg1_ref, db1_ref):
        @pl.when(pl.program_id(0) == 0)
        def _():
            dg1_ref[...] = jnp.zeros_like(dg1_ref)
            db1_ref[...] = jnp.zeros_like(db1_ref)

        dh1 = ALPHA * d2_ref[...]
        for j in range(2):
            cols = slice(j * FF_CHUNK, (j + 1) * FF_CHUNK)
            dh1 += _dot_nt(dg_ref[:, cols], w_ref[j])
            dh1 += _dot_nt(du_ref[:, cols], w_ref[j + 2])
        xhat, rstd = _ln_hat(p_ref[...])
        dp1 = _ln_bwd(xhat, rstd, dh1, g_ref[...])
        d1_ref[...] = dp1
        d1b_ref[...] = dp1.astype(d1b_ref.dtype)
        dg1_ref[...] += _colsum(dh1 * xhat)
        db1_ref[...] += _colsum(dh1)

    vec = _fixed(1, D_MODEL)
    return pl.pallas_call(
        body, name="ffn_up_bwd", grid=(S // tm,),
        in_specs=[_rows(tm, D_FF), _rows(tm, D_FF), _fixed(4, D_MODEL, FF_CHUNK), _rows(tm, D_MODEL),
                  _rows(tm, D_MODEL), vec],
        out_specs=[_rows(tm, D_MODEL), _rows(tm, D_MODEL), vec, vec],
        out_shape=[_sds((S, D_MODEL), F32), _sds((S, D_MODEL), _MXU), _sds((1, D_MODEL), F32), _sds((1, D_MODEL), F32)],
        compiler_params=_cp(("arbitrary",), vmem_mb=56),
    )(dgate, dup, wgu, dp2, pre1, g1)


def _rms_bwd(x, g, dy):
    n = x.shape[-1]
    r = lax.rsqrt(jnp.mean(x * x, axis=-1, keepdims=True) + RMS_EPS)
    u = dy * g
    dx = r * u - x * (r * r * r) * (jnp.sum(u * x, axis=-1, keepdims=True) * (1.0 / n))
    return dx, _colsum(dy * x * r)


def _mix_bwd(dp1b, w_out, sb, sw, gsb, gsw):
    S = sb.shape[0]
    tm = min(S, 512)

    def body(d_ref, w_ref, sb_ref, sw_ref, gsb_ref, gsw_ref, dsb_ref, dsw_ref, dgsb_ref, dgsw_ref):
        @pl.when(pl.program_id(0) == 0)
        def _():
            dgsb_ref[...] = jnp.zeros_like(dgsb_ref)
            dgsw_ref[...] = jnp.zeros_like(dgsw_ref)

        dm = _dot_nt(d_ref[...], w_ref[...])
        dsb, dgsb = _rms_bwd(sb_ref[...], gsb_ref[...], dm[:, :SB_WIDTH])
        dsw, dgsw = _rms_bwd(sw_ref[...], gsw_ref[...], dm[:, SB_WIDTH:])
        dsb_ref[...] = dsb.astype(dsb_ref.dtype)
        dsw_ref[...] = dsw.astype(dsw_ref.dtype)
        dgsb_ref[...] += dgsb
        dgsw_ref[...] += dgsw

    return pl.pallas_call(
        body, name="mix_bwd", grid=(S // tm,),
        in_specs=[_rows(tm, D_MODEL), _fixed(D_MODEL, D_MODEL), _rows(tm, SB_WIDTH), _rows(tm, SWA_WIDTH),
                  _fixed(1, SB_WIDTH), _fixed(1, SWA_WIDTH)],
        out_specs=[_rows(tm, SB_WIDTH), _rows(tm, SWA_WIDTH), _fixed(1, SB_WIDTH), _fixed(1, SWA_WIDTH)],
        out_shape=[_sds((S, SB_WIDTH), _MXU), _sds((S, SWA_WIDTH), _MXU), _sds((1, SB_WIDTH), F32),
                   _sds((1, SWA_WIDTH), F32)],
        compiler_params=_cp(("arbitrary",)),
    )(dp1b, w_out, sb, sw, gsb, gsw)


def _in_proj_bwd(dproj, w_in, dp1, x, g, parts):
    S = x.shape[0]
    tm = min(S, 512)
    nw = len(parts)
    ns = len(IN_SECTIONS)
    offs = np.cumsum((0,) + IN_SECTIONS)
    s_ins, s_outs, s_sems = _scatter_io(parts)

    def body(*refs):
        dpj_refs = refs[:ns]
        w_ref, d1_ref, x_ref, g_ref = refs[ns:ns + 4]
        rest = refs[ns + 4:]
        gx_ref, dg_ref, db_ref = rest[nw:nw + 3]
        scatter = _Scatter(rest[:nw], rest[nw + 3:2 * nw + 3], *rest[2 * nw + 3:])

        @pl.when(pl.program_id(0) == 0)
        def _():
            scatter.start()
            dg_ref[...] = jnp.zeros_like(dg_ref)
            db_ref[...] = jnp.zeros_like(db_ref)

        dh0 = ALPHA * d1_ref[...]
        for k in range(ns):
            dh0 += _dot_nt(dpj_refs[k][...], w_ref[:, offs[k]:offs[k + 1]])
        xhat, rstd = _ln_hat(x_ref[...])
        gx_ref[...] = _ln_bwd(xhat, rstd, dh0, g_ref[...])
        dg_ref[...] += _colsum(dh0 * xhat)
        db_ref[...] += _colsum(dh0)

        @pl.when(pl.program_id(0) == pl.num_programs(0) - 1)
        def _():
            scatter.finish()

    vec = _fixed(1, D_MODEL)
    any_spec = pl.BlockSpec(memory_space=pl.ANY)
    res = pl.pallas_call(
        body, name="in_proj_bwd", grid=(S // tm,),
        in_specs=[_rows(tm, n) for n in IN_SECTIONS]
                 + [_fixed(D_MODEL, IN_COLS), _rows(tm, D_MODEL), _rows(tm, D_MODEL), vec] + [any_spec] * nw,
        out_specs=[_rows(tm, D_MODEL), vec, vec] + [any_spec] * nw,
        out_shape=[_sds((S, D_MODEL), F32), _sds((1, D_MODEL), F32), _sds((1, D_MODEL), F32)] + s_outs,
        scratch_shapes=s_sems,
        compiler_params=_cp(("arbitrary",)),
    )(*dproj, w_in, dp1, x, g, *s_ins)
    return res[0], res[1], res[2], list(res[3:])


def _matmul_tn(a, b, name, tk, tn):
    T, K = a.shape
    N = b.shape[1]
    tt = min(T, 1024)

    def body(a_ref, b_ref, o_ref):
        @pl.when(pl.program_id(2) == 0)
        def _():
            o_ref[...] = jnp.zeros_like(o_ref)

        o_ref[...] += _dot_tn(a_ref[...], b_ref[...])

    return pl.pallas_call(
        body, name=name, grid=(K // tk, N // tn, T // tt),
        in_specs=[pl.BlockSpec((tt, tk), lambda k, n, t: (t, k)), pl.BlockSpec((tt, tn), lambda k, n, t: (t, n))],
        out_specs=pl.BlockSpec((tk, tn), lambda k, n, t: (k, n)),
        out_shape=_sds((K, N), F32),
        compiler_params=_cp(("parallel", "parallel", "arbitrary")),
    )(a, b)


def _place():
    x, y, c = lax.axis_index("x"), lax.axis_index("y"), lax.axis_index("c")
    chips = [(1 - x, y), (x, 1 - y), (1 - x, 1 - y)]
    return x, y, c, chips


class _Gather:
    def __init__(self, in_refs, out_refs, send_sems, recv_sems):
        self.in_refs, self.out_refs, self.send_sems, self.recv_sems = in_refs, out_refs, send_sems, recv_sems
        self.x, self.y, self.c, self.chips = _place()

    def _copy(self, w, k, chip, hc, to, src=None):
        part = self.out_refs[w].at[2 * chip[0] + chip[1], hc]
        return pltpu.make_async_remote_copy(
            src_ref=part if src is None else src, dst_ref=part, send_sem=self.send_sems.at[w, k],
            recv_sem=self.recv_sems.at[w, k], device_id=to, device_id_type=MESH)

    def _first(self):
        x, y, c = self.x, self.y, self.c
        return [self._copy(w, j, (x, y), c, (*chip, c), src=self.in_refs[w].at[c])
                for w in range(len(self.in_refs)) for j, chip in enumerate(self.chips)]

    def start(self):
        for cp in self._first():
            cp.start()

    def finish(self):
        x, y, c = self.x, self.y, self.c
        me, sibling = (x, y, c), (x, y, 1 - c)
        ws = range(len(self.in_refs))
        passed = []
        for w in ws:
            for j, chip in enumerate(self.chips):
                self._copy(w, j, chip, c, me).wait_recv()
                passed.append(self._copy(w, 3 + j, chip, c, sibling))
                passed[-1].start()
        for w in ws:
            for j, chip in enumerate(self.chips):
                self._copy(w, 3 + j, chip, 1 - c, me).wait_recv()
        for cp in self._first() + passed:
            cp.wait_send()


def _gather_io(shards):
    halves = [(s.shape[0] // 2, s.shape[1]) for s in shards]
    ins = [s.reshape(2, h, cols) for s, (h, cols) in zip(shards, halves)]
    outs = [_sds((N_CHIPS, 2, h, cols), s.dtype) for s, (h, cols) in zip(shards, halves)]
    sems = [pltpu.SemaphoreType.DMA((len(shards), 6)), pltpu.SemaphoreType.DMA((len(shards), 6))]
    return ins, outs, sems


def _gather_assemble(outs, shards):
    me = 2 * lax.axis_index("x") + lax.axis_index("y")
    return [lax.dynamic_update_slice_in_dim(o.reshape((N_CHIPS,) + s.shape), s[None], me, axis=0)
            for o, s in zip(outs, shards)]


class _Scatter:
    def __init__(self, p_refs, out_refs, send_sems, recv_sems):
        self.p_refs, self.out_refs, self.send_sems, self.recv_sems = p_refs, out_refs, send_sems, recv_sems
        self.x, self.y, self.c, self.chips = _place()
        self.me = 2 * self.x + self.y

    def _copy(self, w, j, chip, src_chip, dst_chip):
        return pltpu.make_async_remote_copy(
            src_ref=self.p_refs[w].at[src_chip], dst_ref=self.out_refs[w].at[dst_chip], send_sem=self.send_sems.at[w, j],
            recv_sem=self.recv_sems.at[w, j], device_id=(*chip, self.c), device_id_type=MESH)

    def _sends(self):
        return [self._copy(w, j, chip, 2 * chip[0] + chip[1], self.me)
                for w in range(len(self.p_refs)) for j, chip in enumerate(self.chips)]

    def start(self):
        for cp in self._sends():
            cp.start()

    def finish(self):
        for w in range(len(self.p_refs)):
            for j, chip in enumerate(self.chips):
                self._copy(w, j, chip, self.me, 2 * chip[0] + chip[1]).wait_recv()
        for cp in self._sends():
            cp.wait_send()


def _scatter_io(parts):
    sems = [pltpu.SemaphoreType.DMA((len(parts), 3)), pltpu.SemaphoreType.DMA((len(parts), 3))]
    return list(parts), [_sds(p.shape, p.dtype) for p in parts], sems


class _Swap:
    def __init__(self, g_refs, out_refs, send_sems, recv_sems):
        x, y, c, _ = _place()
        self.copies = []
        for w in range(len(g_refs)):
            half = out_refs[w].shape[1]
            theirs = g_refs[w].at[:, pl.ds(pl.multiple_of((1 - c) * half, 8), half), :]
            self.copies.append(pltpu.make_async_remote_copy(
                src_ref=theirs, dst_ref=out_refs[w], send_sem=send_sems.at[w], recv_sem=recv_sems.at[w],
                device_id=(x, y, 1 - c), device_id_type=MESH))

    def start(self):
        for cp in self.copies:
            cp.start()

    def finish(self):
        for cp in self.copies:
            cp.wait()


def _swap_io(grads):
    outs = [_sds((g.shape[0], g.shape[1] // 2, g.shape[2]), g.dtype) for g in grads]
    return list(grads), outs, [pltpu.SemaphoreType.DMA((len(grads),)), pltpu.SemaphoreType.DMA((len(grads),))]


def _matmul_tn_pair(a, b0, b1, name):
    T, K = a.shape
    tt = min(T, 1024)

    def body(a_ref, b0_ref, b1_ref, o_ref):
        n = pl.program_id(0)

        @pl.when(pl.program_id(1) == 0)
        def _():
            o_ref[...] = jnp.zeros_like(o_ref)

        @pl.when(n < 2)
        def _():
            o_ref[0] += _dot_tn(a_ref[...], b0_ref[...])

        @pl.when(n >= 2)
        def _():
            o_ref[0] += _dot_tn(a_ref[...], b1_ref[...])

    return pl.pallas_call(
        body, name=name, grid=(4, T // tt),
        in_specs=[pl.BlockSpec((tt, K), lambda n, t: (t, 0)),
                  pl.BlockSpec((tt, FF_CHUNK), lambda n, t: (t, jnp.minimum(n, 1))),
                  pl.BlockSpec((tt, FF_CHUNK), lambda n, t: (t, jnp.maximum(n - 2, 0)))],
        out_specs=pl.BlockSpec((1, K, FF_CHUNK), lambda n, t: (n, 0, 0)),
        out_shape=_sds((4, K, FF_CHUNK), F32),
        compiler_params=_cp(("parallel", "arbitrary")),
    )(a, b0, b1)


def _sb_logs(zt, causal):
    e = jnp.exp(-jnp.abs(zt))
    lb = jnp.minimum(zt, 0.0) - jnp.log(1.0 + e)
    l1m = lb - zt
    if causal is not None:
        l1m = jnp.where(causal, l1m, 0.0)
    return lb, l1m


def _sb_weights(lb, suf, causal):
    a = jnp.exp(lb + suf)
    if causal is not None:
        a = jnp.where(causal, a, 0.0)
    return a


def _tri_masks(t):
    r = lax.broadcasted_iota(jnp.int32, (t, t), 0)
    c = lax.broadcasted_iota(jnp.int32, (t, t), 1)
    return r, c


def _sb_fwd(qT, kb, vTb, shards):
    Hh, _, S = qT.shape
    nk, T = kb.shape[1], kb.shape[2]
    nq = S // T
    G = SB_GROUP_FWD
    nw = len(shards)
    g_ins, g_outs, g_sems = _gather_io(shards)

    def body(qT_ref, k_ref, vT_ref, *rest):
        o_ref, rs_ref = rest[nw:nw + 2]
        gather = _Gather(rest[:nw], rest[nw + 2:2 * nw + 2], *rest[2 * nw + 2:])
        i = pl.program_id(1)
        first_step = jnp.logical_and(pl.program_id(0) == 0, i == 0)
        last_step = jnp.logical_and(pl.program_id(0) == pl.num_programs(0) - 1, i == pl.num_programs(1) - 1)

        @pl.when(first_step)
        def _():
            gather.start()

        qts = [(qT_ref[g].astype(F32) * SCALE).astype(_MXU) for g in range(G)]
        r, c = _tri_masks(T)
        upper = (c > r).astype(_MXU)
        causal = r < c

        def blk(j, carry, mask):
            hs = range(G)
            for g in hs:
                rs_ref[g, 0, j] = jnp.broadcast_to(carry[g][0], (8, T))
            zs = [_dot(k_ref[g, j], qts[g]) for g in hs]
            lbs, l1ms = zip(*[_sb_logs(zs[g], mask) for g in hs])
            splits = [_split2(l1ms[g]) for g in hs]
            cums = [_dot(upper, splits[g][0]) + _dot(upper, splits[g][1]) for g in hs]
            avs = [_sb_weights(lbs[g], carry[g][0] + cums[g], mask).astype(_MXU) for g in hs]
            accs = [carry[g][1] + _dot(vT_ref[g, j], avs[g]) for g in hs]
            return tuple((carry[g][0] + _colsum(l1ms[g]), accs[g]) for g in hs)

        def go_on(j, carry):
            top = carry[0][0]
            for g in range(1, G):
                top = jnp.maximum(top, carry[g][0])
            return jnp.logical_and(j >= 0, jnp.max(top) >= SB_DEAD)

        init = tuple((jnp.zeros((1, T), F32), jnp.zeros((HEAD_DIM, T), F32)) for _ in range(G))
        carry = blk(i, init, causal)
        j, carry = lax.while_loop(lambda st: go_on(*st), lambda st: (st[0] - 1, blk(st[0], st[1], None)),
                                  (i - 1, carry))

        @pl.when(j >= 0)
        def _():
            for g in range(G):
                rs_ref[g, 0, j] = jnp.broadcast_to(carry[g][0], (8, T))

        o_ref[...] = jnp.concatenate([carry[g][1] for g in range(G)], axis=0).T

        @pl.when(last_step)
        def _():
            gather.finish()

    any_spec = pl.BlockSpec(memory_space=pl.ANY)
    res = pl.pallas_call(
        body, name="sb_fwd", grid=(Hh // G, nq),
        in_specs=[pl.BlockSpec((G, HEAD_DIM, T), lambda h, i: (h, 0, i)),
                  pl.BlockSpec((G, nk, T, HEAD_DIM), lambda h, i: (h, 0, 0, 0), pipeline_mode=pl.Buffered(1)),
                  pl.BlockSpec((G, nk, HEAD_DIM, T), lambda h, i: (h, 0, 0, 0), pipeline_mode=pl.Buffered(1))]
                 + [any_spec] * nw,
        out_specs=[pl.BlockSpec((T, G * HEAD_DIM), lambda h, i: (i, h)),
                   pl.BlockSpec((G, 1, nk, 8, T), lambda h, i: (h, i, 0, 0, 0))] + [any_spec] * nw,
        out_shape=[_sds((S, Hh * HEAD_DIM), F32), _sds((Hh, nq, nk, 8, T), F32)] + g_outs,
        scratch_shapes=g_sems,
        compiler_params=_cp(("arbitrary", "arbitrary")),
    )(qT, kb, vTb, *g_ins)
    return res[0], res[1], _gather_assemble(res[2:], shards)


def _sb_bwd(qT, kb, kTb, vb, doT, rsave, parts):
    Hh, _, S = qT.shape
    nk, T = kb.shape[1], kb.shape[2]
    nq = S // T
    G = SB_GROUP_BWD
    nw = len(parts)
    s_ins, s_outs, s_sems = _scatter_io(parts)

    def body(qT_ref, k_ref, kT_ref, v_ref, doT_ref, rs_ref, *rest):
        dq_ref, dk_out_ref, dv_out_ref = rest[nw:nw + 3]
        dk_ref, dv_ref = rest[2 * nw + 3:2 * nw + 5]
        scatter = _Scatter(rest[:nw], rest[nw + 3:2 * nw + 3], *rest[2 * nw + 5:])
        i = pl.program_id(1)
        first_step = jnp.logical_and(pl.program_id(0) == 0, i == 0)
        last_step = jnp.logical_and(pl.program_id(0) == pl.num_programs(0) - 1, i == pl.num_programs(1) - 1)

        @pl.when(first_step)
        def _():
            scatter.start()

        @pl.when(i == 0)
        def _():
            dk_ref[...] = jnp.zeros_like(dk_ref)
            dv_ref[...] = jnp.zeros_like(dv_ref)

        qts = [(qT_ref[g].astype(F32) * SCALE).astype(_MXU) for g in range(G)]
        douts = [doT_ref[g] for g in range(G)]
        r, c = _tri_masks(T)
        upper = (c > r).astype(_MXU)
        lower = (c < r).astype(_MXU)
        causal = r < c

        def blk(j, carry, mask):
            hs = range(G)
            zs = [_dot(k_ref[g, j], qts[g]) for g in hs]
            das = [_dot(v_ref[g, j], douts[g]) for g in hs]
            lbs, l1ms = zip(*[_sb_logs(zs[g], mask) for g in hs])
            splits = [_split2(l1ms[g]) for g in hs]
            cums = [_dot(upper, splits[g][0]) + _dot(upper, splits[g][1]) for g in hs]
            avs = [_sb_weights(lbs[g], rs_ref[g, 0, j][0:1, :] + cums[g], mask) for g in hs]
            ets = [das[g] * avs[g] for g in hs]
            esplits = [_split2(ets[g]) for g in hs]
            ecums = [_dot(lower, esplits[g][0]) + _dot(lower, esplits[g][1]) for g in hs]
            dzs = []
            for g in hs:
                sig = jnp.exp(lbs[g])
                dz = ets[g] * (1.0 - sig) - (carry[g][0] + ecums[g]) * sig
                if mask is not None:
                    dz = jnp.where(mask, dz, 0.0)
                dzs.append(dz.astype(_MXU))
            dqs = [carry[g][1] + _dot(kT_ref[g, j], dzs[g]) for g in hs]
            for g in hs:
                dk_ref[j, g * HEAD_DIM:(g + 1) * HEAD_DIM, :] += _dot_nt(qts[g], dzs[g])
            for g in hs:
                dv_ref[j, g * HEAD_DIM:(g + 1) * HEAD_DIM, :] += _dot_nt(douts[g], avs[g].astype(_MXU))
            return tuple((carry[g][0] + _colsum(ets[g]), dqs[g]) for g in hs)

        def live(j):
            jj = jnp.maximum(j, 0)
            top = rs_ref[0, 0, jj][0:1, :]
            for g in range(1, G):
                top = jnp.maximum(top, rs_ref[g, 0, jj][0:1, :])
            return jnp.logical_and(j >= 0, jnp.max(top) >= SB_DEAD)

        first = lax.while_loop(lambda st: st[1], lambda st: (st[0] - 1, live(st[0] - 2)), (i, live(i - 1)))[0]
        carry = tuple((jnp.zeros((1, T), F32), jnp.zeros((HEAD_DIM, T), F32)) for _ in range(G))
        carry = lax.fori_loop(first, i, lambda s, cr: blk(s, cr, None), carry)
        carry = blk(i, carry, causal)
        dq_ref[...] = (jnp.concatenate([carry[g][1] for g in range(G)], axis=0) * SCALE).T.astype(dq_ref.dtype)

        @pl.when(i == pl.num_programs(1) - 1)
        def _():
            def flush(j, _):
                rows = pl.ds(pl.multiple_of(j * T, T), T)
                dk_out_ref[rows, :] = dk_ref[j].T.astype(dk_out_ref.dtype)
                dv_out_ref[rows, :] = dv_ref[j].T.astype(dv_out_ref.dtype)
                return 0
            lax.fori_loop(0, nk, flush, 0)

        @pl.when(last_step)
        def _():
            scatter.finish()

    colblk = pl.BlockSpec((G, HEAD_DIM, T), lambda h, i: (h, 0, i))
    once = pl.Buffered(1)
    kblk = pl.BlockSpec((G, nk, T, HEAD_DIM), lambda h, i: (h, 0, 0, 0), pipeline_mode=once)
    kTblk = pl.BlockSpec((G, nk, HEAD_DIM, T), lambda h, i: (h, 0, 0, 0), pipeline_mode=once)
    any_spec = pl.BlockSpec(memory_space=pl.ANY)
    res = pl.pallas_call(
        body, name="sb_bwd", grid=(Hh // G, nq),
        in_specs=[colblk, kblk, kTblk, kblk, colblk,
                  pl.BlockSpec((G, 1, nk, 8, T), lambda h, i: (h, i, 0, 0, 0))] + [any_spec] * nw,
        out_specs=[pl.BlockSpec((T, G * HEAD_DIM), lambda h, i: (i, h)),
                   pl.BlockSpec((S, G * HEAD_DIM), lambda h, i: (0, h), pipeline_mode=once),
                   pl.BlockSpec((S, G * HEAD_DIM), lambda h, i: (0, h), pipeline_mode=once)] + [any_spec] * nw,
        out_shape=[_sds((S, Hh * HEAD_DIM), _MXU)] * 3 + s_outs,
        scratch_shapes=[pltpu.VMEM((nk, G * HEAD_DIM, T), F32), pltpu.VMEM((nk, G * HEAD_DIM, T), F32)] + s_sems,
        compiler_params=_cp(("arbitrary", "arbitrary"), vmem_mb=60),
    )(qT, kb, kTb, vb, doT, rsave, *s_ins)
    return res[0], res[1], res[2], list(res[3:])


def _bucket_table():
    qi = np.arange(BLOCK)[:, None]
    cj = np.arange(2 * BLOCK)[None, :]
    dist = qi + BLOCK - cj
    exact = REL_BUCKETS // 2
    d = np.maximum(dist, 0)
    d_f = np.maximum(d, 1).astype(np.float32)
    large = exact + (np.log(d_f / np.float32(exact)) / np.float32(math.log(REL_MAX_DIST / exact))
                     * np.float32(REL_BUCKETS - exact)).astype(np.int32)
    large = np.minimum(large, REL_BUCKETS - 1)
    return np.where(d < exact, d, large).astype(np.int32)


def _swa_bias(rel_bias, bucket):
    def body(rb_ref, bk_ref, o_ref):
        bk = bk_ref[...]
        for h in range(SWA_HEADS):
            t = jnp.zeros((2 * BLOCK, BLOCK), F32)
            for b in range(REL_BUCKETS):
                t = jnp.where(bk == b, rb_ref[b, h], t)
            o_ref[h] = t

    return pl.pallas_call(
        body, name="swa_bias",
        in_specs=[pl.BlockSpec(memory_space=pltpu.SMEM), pl.BlockSpec(memory_space=pltpu.VMEM)],
        out_specs=pl.BlockSpec(memory_space=pltpu.VMEM),
        out_shape=_sds((SWA_HEADS, 2 * BLOCK, BLOCK), F32),
    )(rel_bias, bucket)


def _swa_logits(q, kp, kc):
    qs = (q.astype(F32) * SCALE).astype(_MXU)
    return qs, _dot_nt(kp, qs), _dot_nt(kc, qs)


def _swa_softmax(lp, lc, bias, sink, live_prev):
    r, c = _tri_masks(BLOCK)
    in_window = r > c if live_prev is None else jnp.logical_and(r > c, live_prev)
    lp = jnp.where(in_window, lp + bias[:BLOCK, :], -jnp.inf)
    lc = jnp.where(r <= c, lc + bias[BLOCK:, :], -jnp.inf)
    m = jnp.maximum(jnp.maximum(jnp.max(lp, axis=0, keepdims=True), jnp.max(lc, axis=0, keepdims=True)), sink)
    pp = jnp.exp(lp - m)
    pc = jnp.exp(lc - m)
    ps = jnp.exp(sink - m)
    denom = _colsum(pp) + _colsum(pc) + ps
    return pp / denom, pc / denom, ps / denom


def _swa_sub(nb):
    return min(SWA_SUB, nb)


def _swa_keys(b, prev_ref, cur_ref, i):
    cur = cur_ref[0, b * BLOCK:(b + 1) * BLOCK, :]
    if b == 0:
        return prev_ref[0], cur, i > 0
    return cur_ref[0, (b - 1) * BLOCK:b * BLOCK, :], cur, None


def _swa_keys_t(b, prev_ref, cur_ref):
    cur = cur_ref[0, :, b * BLOCK:(b + 1) * BLOCK]
    return (prev_ref[0] if b == 0 else cur_ref[0, :, (b - 1) * BLOCK:b * BLOCK]), cur


def _swa_fwd(q, k, vT, bias, sink):
    S = q.shape[1]
    nb = S // BLOCK
    ns = _swa_sub(nb)
    R = ns * BLOCK

    def body(q_ref, kp_ref, kc_ref, vp_ref, vc_ref, bias_ref, sink_ref, o_ref):
        i = pl.program_id(1)
        bias = bias_ref[0]
        sink = sink_ref[0][:, :1]
        subs = range(ns)
        keys = [_swa_keys(b, kp_ref, kc_ref, i) for b in subs]
        vals = [_swa_keys_t(b, vp_ref, vc_ref) for b in subs]
        logits = [_swa_logits(q_ref[0, b * BLOCK:(b + 1) * BLOCK, :], keys[b][0], keys[b][1]) for b in subs]
        ws = [_swa_softmax(logits[b][1], logits[b][2], bias, sink, keys[b][2]) for b in subs]
        for b in subs:
            o_ref[0, :, b * BLOCK:(b + 1) * BLOCK] = (_dot(vals[b][0], ws[b][0].astype(_MXU))
                                                      + _dot(vals[b][1], ws[b][1].astype(_MXU)))

    prev = pl.BlockSpec((1, BLOCK, HEAD_DIM), lambda h, i: (h // SWA_GROUP, jnp.maximum(i * ns - 1, 0), 0))
    cur = pl.BlockSpec((1, R, HEAD_DIM), lambda h, i: (h // SWA_GROUP, i, 0))
    prev_t = pl.BlockSpec((1, HEAD_DIM, BLOCK), lambda h, i: (h // SWA_GROUP, 0, jnp.maximum(i * ns - 1, 0)))
    cur_t = pl.BlockSpec((1, HEAD_DIM, R), lambda h, i: (h // SWA_GROUP, 0, i))
    return pl.pallas_call(
        body, name="swa_fwd", grid=(SWA_HEADS, nb // ns),
        in_specs=[pl.BlockSpec((1, R, HEAD_DIM), lambda h, i: (h, i, 0)), prev, cur, prev_t, cur_t,
                  pl.BlockSpec((1, 2 * BLOCK, BLOCK), lambda h, i: (h, 0, 0)),
                  pl.BlockSpec((1, 1, BLOCK), lambda h, i: (h, 0, 0))],
        out_specs=pl.BlockSpec((1, HEAD_DIM, R), lambda h, i: (h, 0, i)),
        out_shape=_sds((SWA_HEADS, HEAD_DIM, S), F32),
        compiler_params=_cp(("parallel", "parallel")),
    )(q, k, k, vT, vT, bias, sink)


def _swa_bwd(q, k, kT, v, bias, sink, do, grads):
    S = q.shape[1]
    nb = S // BLOCK
    ns = _swa_sub(nb)
    R = ns * BLOCK
    nw = len(grads)
    x_ins, x_outs, x_sems = _swap_io(grads)

    def body(q_ref, kp_ref, kc_ref, ktp_ref, ktc_ref, vp_ref, vc_ref, bias_ref, sink_ref, do_ref, *rest):
        dq_ref, dk_ref, dv_ref, dbias_ref, dsink_ref = rest[nw:nw + 5]
        swap = _Swap(rest[:nw], rest[nw + 5:2 * nw + 5], *rest[2 * nw + 5:])
        g = pl.program_id(1)
        i = pl.program_id(2)
        first_step = jnp.logical_and(pl.program_id(0) == 0, jnp.logical_and(g == 0, i == 0))
        last_step = jnp.logical_and(pl.program_id(0) == pl.num_programs(0) - 1,
                                    jnp.logical_and(g == pl.num_programs(1) - 1, i == pl.num_programs(2) - 1))

        @pl.when(first_step)
        def _():
            swap.start()

        @pl.when(jnp.logical_and(g == 0, i == 0))
        def _():
            dk_ref[...] = jnp.zeros_like(dk_ref)
            dv_ref[...] = jnp.zeros_like(dv_ref)

        @pl.when(i == 0)
        def _():
            dbias_ref[...] = jnp.zeros_like(dbias_ref)
            dsink_ref[...] = jnp.zeros_like(dsink_ref)

        bias = bias_ref[0]
        sink = sink_ref[0][:, :1]
        subs = range(ns)
        rows = [slice(b * BLOCK, (b + 1) * BLOCK) for b in subs]
        keys = [_swa_keys(b, kp_ref, kc_ref, i) for b in subs]
        keys_t = [_swa_keys_t(b, ktp_ref, ktc_ref) for b in subs]
        vals = [_swa_keys(b, vp_ref, vc_ref, i) for b in subs]
        douts = [do_ref[0, rows[b], :] for b in subs]
        logits = [_swa_logits(q_ref[0, rows[b], :], keys[b][0], keys[b][1]) for b in subs]
        dws = [(_dot_nt(vals[b][0], douts[b]), _dot_nt(vals[b][1], douts[b])) for b in subs]
        dbp = jnp.zeros((BLOCK, BLOCK), F32)
        dbc = jnp.zeros((BLOCK, BLOCK), F32)
        dsk = jnp.zeros((1, BLOCK), F32)
        wts, dls = [], []
        for b in subs:
            wp, wc, ws = _swa_softmax(logits[b][1], logits[b][2], bias, sink, keys[b][2])
            dwp, dwc = dws[b]
            delta = _colsum(wp * dwp) + _colsum(wc * dwc)
            dlp = wp * (dwp - delta)
            dlc = wc * (dwc - delta)
            dbp += dlp
            dbc += dlc
            dsk -= ws * delta
            wts.append((wp.astype(_MXU), wc.astype(_MXU)))
            dls.append((dlp.astype(_MXU), dlc.astype(_MXU)))
        for b in subs:
            dq_ref[0, :, rows[b]] = (_dot(keys_t[b][0], dls[b][0]) + _dot(keys_t[b][1], dls[b][1])) * SCALE
        for b in subs:
            qs = logits[b][0]
            blk = i * ns + b
            dk_ref[0, blk] += _dot(dls[b][1], qs)
            dv_ref[0, blk] += _dot(wts[b][1], douts[b])
            if b == 0:
                @pl.when(i > 0)
                def _():
                    dk_ref[0, blk - 1] += _dot(dls[0][0], qs)
                    dv_ref[0, blk - 1] += _dot(wts[0][0], douts[0])
            else:
                dk_ref[0, blk - 1] += _dot(dls[b][0], qs)
                dv_ref[0, blk - 1] += _dot(wts[b][0], douts[b])
        dbias_ref[0, :BLOCK, :] += dbp
        dbias_ref[0, BLOCK:, :] += dbc
        dsink_ref[0] += jnp.broadcast_to(dsk, (8, BLOCK))

        @pl.when(last_step)
        def _():
            swap.finish()

    hq = lambda kv, g, i: kv * SWA_GROUP + g
    prev = pl.BlockSpec((1, BLOCK, HEAD_DIM), lambda kv, g, i: (kv, jnp.maximum(i * ns - 1, 0), 0))
    cur = pl.BlockSpec((1, R, HEAD_DIM), lambda kv, g, i: (kv, i, 0))
    prev_t = pl.BlockSpec((1, HEAD_DIM, BLOCK), lambda kv, g, i: (kv, 0, jnp.maximum(i * ns - 1, 0)))
    cur_t = pl.BlockSpec((1, HEAD_DIM, R), lambda kv, g, i: (kv, 0, i))
    qblk = pl.BlockSpec((1, R, HEAD_DIM), lambda kv, g, i: (hq(kv, g, i), i, 0))
    qblk_t = pl.BlockSpec((1, HEAD_DIM, R), lambda kv, g, i: (hq(kv, g, i), 0, i))
    kvacc = pl.BlockSpec((1, nb, BLOCK, HEAD_DIM), lambda kv, g, i: (kv, 0, 0, 0))
    any_spec = pl.BlockSpec(memory_space=pl.ANY)
    res = pl.pallas_call(
        body, name="swa_bwd", grid=(SWA_KV_HEADS, SWA_GROUP, nb // ns),
        in_specs=[qblk, prev, cur, prev_t, cur_t, prev, cur,
                  pl.BlockSpec((1, 2 * BLOCK, BLOCK), lambda kv, g, i: (hq(kv, g, i), 0, 0)),
                  pl.BlockSpec((1, 1, BLOCK), lambda kv, g, i: (hq(kv, g, i), 0, 0)), qblk] + [any_spec] * nw,
        out_specs=[qblk_t, kvacc, kvacc,
                   pl.BlockSpec((1, 2 * BLOCK, BLOCK), lambda kv, g, i: (hq(kv, g, i), 0, 0)),
                   pl.BlockSpec((1, 8, BLOCK), lambda kv, g, i: (hq(kv, g, i), 0, 0))] + [any_spec] * nw,
        out_shape=[_sds((SWA_HEADS, HEAD_DIM, S), F32), _sds((SWA_KV_HEADS, nb, BLOCK, HEAD_DIM), F32),
                   _sds((SWA_KV_HEADS, nb, BLOCK, HEAD_DIM), F32), _sds((SWA_HEADS, 2 * BLOCK, BLOCK), F32),
                   _sds((SWA_HEADS, 8, BLOCK), F32)] + x_outs,
        scratch_shapes=x_sems,
        compiler_params=_cp(("arbitrary", "arbitrary", "arbitrary")),
    )(q, k, k, kT, kT, v, v, bias, sink, do, *x_ins)
    return res[0], res[1], res[2], res[3], res[4], list(res[5:])


def _swa_small_grads(dbias, dsink, bucket):
    rows = REL_BUCKETS + 8

    def total(x):
        return jnp.sum(jnp.sum(x, axis=1, keepdims=True), axis=0, keepdims=True)

    def body(db_ref, ds_ref, bk_ref, o_ref):
        bk = bk_ref[...]
        r = lax.broadcasted_iota(jnp.int32, (rows, BLOCK), 0)
        c = lax.broadcasted_iota(jnp.int32, (rows, BLOCK), 1)
        out = jnp.zeros((rows, BLOCK), F32)
        for h in range(SWA_HEADS):
            db = db_ref[h]
            for b in range(REL_BUCKETS):
                s = total(jnp.where(bk == b, db, 0.0))
                out = jnp.where(jnp.logical_and(r == b, c == h), s, out)
            s = jnp.sum(ds_ref[h][0:1, :], axis=1, keepdims=True)
            out = jnp.where(jnp.logical_and(r == REL_BUCKETS, c == h), s, out)
        o_ref[...] = out

    vm = pl.BlockSpec(memory_space=pltpu.VMEM)
    return pl.pallas_call(body, name="swa_small_grads", in_specs=[vm, vm, vm], out_specs=vm,
                          out_shape=_sds((rows, BLOCK), F32))(dbias, dsink, bucket)


def _tile_rows(n):
    for t in (512, 352, 256, 176, 128, 64, 32, 16, 8):
        if n % t == 0:
            return t
    return n


def _cast_rows(x, dtype, name):
    R, C = x.shape
    tr = _tile_rows(R)

    def body(x_ref, o_ref):
        o_ref[...] = x_ref[...].astype(o_ref.dtype)

    return pl.pallas_call(body, name=name, grid=(R // tr,), in_specs=[_rows(tr, C)], out_specs=_rows(tr, C),
                          out_shape=_sds((R, C), dtype), compiler_params=_cp(("parallel",)))(x)


def _pair_sum(g, recv, c, name):
    n, half, C = recv.shape
    tr = _tile_rows(half)

    def body(c_ref, a_ref, b_ref, o_ref):
        o_ref[...] = (a_ref[0] + b_ref[...]).astype(o_ref.dtype)

    return pl.pallas_call(
        body, name=name,
        grid_spec=pltpu.PrefetchScalarGridSpec(
            num_scalar_prefetch=1, grid=(n, half // tr),
            in_specs=[pl.BlockSpec((1, 1, tr, C), lambda j, i, c_ref: (j, c_ref[0], i, 0)),
                      pl.BlockSpec((1, tr, C), lambda j, i, c_ref: (j, i, 0))],
            out_specs=pl.BlockSpec((1, tr, C), lambda j, i, c_ref: (j, i, 0))),
        out_shape=_sds((n, half, C), _MXU),
        compiler_params=_cp(("parallel", "parallel")))(c.reshape(1), g.reshape(n, 2, half, C), recv)


def _chip_sum(own, recv, me, name):
    n, R, C = recv.shape
    tr = _tile_rows(R)

    def body(me_ref, own_ref, recv_ref, o_ref):
        acc = None
        for j in range(n):
            term = jnp.where(me_ref[0] == j, own_ref[0], recv_ref[j]).astype(F32)
            acc = term if acc is None else acc + term
        o_ref[...] = acc

    return pl.pallas_call(
        body, name=name,
        grid_spec=pltpu.PrefetchScalarGridSpec(
            num_scalar_prefetch=1, grid=(R // tr,),
            in_specs=[pl.BlockSpec((1, tr, C), lambda i, me_ref: (me_ref[0], i, 0)),
                      pl.BlockSpec((n, tr, C), lambda i, me_ref: (0, i, 0))],
            out_specs=pl.BlockSpec((tr, C), lambda i, me_ref: (i, 0))),
        out_shape=_sds((R, C), F32), compiler_params=_cp(("parallel",)))(me.reshape(1), own, recv)


def _adamw_math(w, g, m, v):
    m = ADAM_B1 * m + (1.0 - ADAM_B1) * g
    v = ADAM_B2 * v + (1.0 - ADAM_B2) * (g * g)
    m_hat = m / (1.0 - ADAM_B1 ** ADAM_STEP)
    v_hat = v / (1.0 - ADAM_B2 ** ADAM_STEP)
    delta = -ADAM_LR * (m_hat / (jnp.sqrt(v_hat) + ADAM_EPS) + ADAM_WD * w)
    return delta, m, v


def _adamw(w, g, m, v, name):
    R, C = w.shape
    tr = _tile_rows(R)

    def body(w_ref, g_ref, m_ref, v_ref, d_ref, nm_ref, nv_ref):
        d, nm, nv = _adamw_math(w_ref[...], g_ref[...], m_ref[...], v_ref[...])
        d_ref[...] = d
        nm_ref[...] = nm
        nv_ref[...] = nv

    blk = _rows(tr, C)
    return pl.pallas_call(body, name=name, grid=(R // tr,), in_specs=[blk] * 4, out_specs=[blk] * 3,
                          out_shape=[_sds((R, C), F32)] * 3, compiler_params=_cp(("parallel",)))(w, g, m, v)


def _gather_weights(shards):
    nw = len(shards)
    ins, outs, sems = _gather_io(shards)

    def body(*refs):
        ex = _Gather(refs[:nw], refs[nw:2 * nw], *refs[2 * nw:])
        ex.start()
        ex.finish()

    any_spec = pl.BlockSpec(memory_space=pl.ANY)
    got = pl.pallas_call(body, name="gather_weights", in_specs=[any_spec] * nw, out_specs=[any_spec] * nw,
                         out_shape=outs, scratch_shapes=sems)(*ins)
    return _gather_assemble(got, shards)


def _swap_halves(grads, name):
    nw = len(grads)
    ins, outs, sems = _swap_io(grads)

    def body(*refs):
        ex = _Swap(refs[:nw], refs[nw:2 * nw], *refs[2 * nw:])
        ex.start()
        ex.finish()

    any_spec = pl.BlockSpec(memory_space=pl.ANY)
    return pl.pallas_call(body, name=name, in_specs=[any_spec] * nw, out_specs=[any_spec] * nw,
                          out_shape=outs, scratch_shapes=sems)(*ins)


def _scatter_partials(parts):
    nw = len(parts)
    ins, outs, sems = _scatter_io(parts)

    def body(*refs):
        ex = _Scatter(refs[:nw], refs[nw:2 * nw], *refs[2 * nw:])
        ex.start()
        ex.finish()

    any_spec = pl.BlockSpec(memory_space=pl.ANY)
    return pl.pallas_call(body, name="scatter_partials", in_specs=[any_spec] * nw, out_specs=[any_spec] * nw,
                          out_shape=outs, scratch_shapes=sems)(*ins)


def _join_halves(sums):
    nw = len(sums)

    def body(*refs):
        f_refs, out_refs = refs[:nw], refs[nw:2 * nw]
        send_sems, recv_sems = refs[2 * nw:]
        x, y, c, _ = _place()
        ws = range(nw)

        def copy(w, half_index):
            return pltpu.make_async_remote_copy(
                src_ref=f_refs[w], dst_ref=out_refs[w].at[half_index], send_sem=send_sems.at[w],
                recv_sem=recv_sems.at[w], device_id=(x, y, 1 - c), device_id_type=MESH)

        sends = [copy(w, c) for w in ws]
        for cp in sends:
            cp.start()
        for w in ws:
            copy(w, 1 - c).wait_recv()
        for cp in sends:
            cp.wait_send()

    any_spec = pl.BlockSpec(memory_space=pl.ANY)
    outs = pl.pallas_call(
        body, name="join_halves", in_specs=[any_spec] * nw, out_specs=[any_spec] * nw,
        out_shape=[_sds((2,) + f.shape, f.dtype) for f in sums],
        scratch_shapes=[pltpu.SemaphoreType.DMA((nw,)), pltpu.SemaphoreType.DMA((nw,))],
    )(*sums)
    c = lax.axis_index("c")
    return [lax.dynamic_update_slice_in_dim(o, f[None], c, axis=0).reshape(2 * f.shape[0], f.shape[1])
            for o, f in zip(outs, sums)]


def _allreduce_small(block):
    m_per, n = block.shape

    def body(x_ref, sum_ref, loss_ref, all_ref, send_sems, recv_sems, local_sem):
        x, y, c, chips = _place()
        me, sibling = (x, y, c), (x, y, 1 - c)

        def rows(px, py, pc):
            return all_ref.at[pl.ds(pl.multiple_of((4 * px + 2 * py + pc) * m_per, 8), m_per), :]

        def copy(k, blk, to, src=None):
            return pltpu.make_async_remote_copy(
                src_ref=rows(*blk) if src is None else src, dst_ref=rows(*blk), send_sem=send_sems.at[k],
                recv_sem=recv_sems.at[k], device_id=to, device_id_type=MESH)

        mine = pltpu.make_async_copy(x_ref, rows(*me), local_sem)
        mine.start()
        first = [copy(0, me, sibling, src=x_ref)]
        first += [copy(1 + j, me, (*chip, c), src=x_ref) for j, chip in enumerate(chips)]
        for cp in first:
            cp.start()
        passed = [copy(4 + j, (*chip, c), sibling) for j, chip in enumerate(chips)]
        for j, chip in enumerate(chips):
            copy(1 + j, (*chip, c), me).wait_recv()
            passed[j].start()
        copy(0, sibling, me).wait_recv()
        for j, chip in enumerate(chips):
            copy(4 + j, (*chip, 1 - c), me).wait_recv()
        for cp in first + passed:
            cp.wait_send()
        mine.wait()

        acc = all_ref[0:m_per, :]
        for d in range(1, 8):
            acc = acc + all_ref[d * m_per:(d + 1) * m_per, :]
        sum_ref[...] = acc
        tot = jnp.sum(acc[8:9, :], axis=1, keepdims=True) * (0.5 / D_MODEL)
        loss_ref[...] = jnp.broadcast_to(tot, loss_ref.shape)

    vm = pl.BlockSpec(memory_space=pltpu.VMEM)
    return pl.pallas_call(
        body, name="allreduce_small", in_specs=[vm], out_specs=[vm, vm],
        out_shape=[_sds((m_per, n), F32), _sds((8, 128), F32)],
        scratch_shapes=[pltpu.VMEM((8 * m_per, n), F32), pltpu.SemaphoreType.DMA((7,)), pltpu.SemaphoreType.DMA((7,)),
                        pltpu.SemaphoreType.DMA],
    )(block)


def _heads_rows(x, nh):
    S = x.shape[0]
    return x.reshape(S, nh, HEAD_DIM).transpose(1, 0, 2)


def _heads_cols(x, nh):
    S = x.shape[0]
    return x.reshape(S, nh, HEAD_DIM).transpose(1, 2, 0)


def _key_blocks(x, nh, t):
    S = x.shape[0]
    return x.reshape(S // t, t, nh, HEAD_DIM).transpose(2, 0, 1, 3)


def _key_blocks_t(x, nh, t):
    S = x.shape[0]
    return x.reshape(S // t, t, nh, HEAD_DIM).transpose(2, 0, 3, 1)


def _pad_row(v):
    v = v.reshape(1, -1)
    return jnp.pad(v, ((0, 0), (0, D_MODEL - v.shape[1])))


def _pack_small(ln_in_g, ln_in_b, sb_g, swa_g, sinks, rel_bias, ln1_g, ln1_b, ln2_g, ln2_b, extra):
    rows = [_pad_row(ln_in_g), _pad_row(ln_in_b), jnp.concatenate([sb_g.reshape(1, -1), swa_g.reshape(1, -1)], axis=1),
            _pad_row(jnp.concatenate([rel_bias.reshape(1, -1), sinks.reshape(1, -1)], axis=1)),
            _pad_row(ln1_g), _pad_row(ln1_b), _pad_row(ln2_g), _pad_row(ln2_b), _pad_row(extra)]
    rows.append(jnp.zeros((SMALL_ROWS - len(rows), D_MODEL), F32))
    return jnp.concatenate(rows, axis=0)


def _unpack_small(blk):
    nrb = REL_BUCKETS * SWA_HEADS
    return (blk[0], blk[1], blk[2:3, :SB_WIDTH], blk[2:3, SB_WIDTH:], blk[3:4, nrb:nrb + SWA_HEADS],
            blk[3, :nrb].reshape(REL_BUCKETS, SWA_HEADS), blk[4:5], blk[5:6], blk[6:7], blk[7:8])


def kernel(x, ln_in_g, ln_in_b, w_in, sb_norm_g, swa_norm_g, sinks, rel_bias, w_out, ln1_g, ln1_b, w_gate_up, w_down, ln2_g, ln2_b, loss_target, m_ln_in_g, m_ln_in_b, m_w_in, m_sb_norm_g, m_swa_norm_g, m_sinks, m_rel_bias, m_w_out, m_ln1_g, m_ln1_b, m_w_gate_up, m_w_down, m_ln2_g, m_ln2_b, v_ln_in_g, v_ln_in_b, v_w_in, v_sb_norm_g, v_swa_norm_g, v_sinks, v_rel_bias, v_w_out, v_ln1_g, v_ln1_b, v_w_gate_up, v_w_down, v_ln2_g, v_ln2_b):
    S = x.shape[1]
    x2 = x.reshape(S, D_MODEL)
    tgt = loss_target.reshape(S, D_MODEL)
    T = min(S, SB_TILE)
    bucket = jnp.asarray(_bucket_table().T)
    row = lambda v: v.reshape(1, -1)

    shards = [_cast_rows(w[0], _MXU, "cast_" + n) for n, w in (("w_in", w_in), ("w_out", w_out), ("w_gate_up", w_gate_up), ("w_down", w_down))]
    (w_in_sh,) = _gather_weights(shards[:1])
    w_in_f = jnp.concatenate([w_in_sh[j] for j in range(N_CHIPS)], axis=1)

    h0, h0b, q_sb, k_sb, v_sb, q_sw, kv_sw = _ln_in_proj(x2, row(ln_in_g), row(ln_in_b), w_in_f)
    k_sw, v_sw = kv_sw[:, :SWA_KV_WIDTH], kv_sw[:, SWA_KV_WIDTH:]
    qT_sb = _heads_cols(q_sb, SB_HEADS)
    kb_sb = _key_blocks(k_sb, SB_HEADS, T)
    sb_out, rsave, (w_out_sh, w_gu_sh, w_down_sh) = _sb_fwd(qT_sb, kb_sb, _key_blocks_t(v_sb, SB_HEADS, T), shards[1:])
    w_out_f = w_out_sh.reshape(D_MODEL, D_MODEL)
    w_down_f = w_down_sh.reshape(D_FF, D_MODEL)

    bias = _swa_bias(rel_bias, bucket)
    sink_rows = jnp.broadcast_to(sinks.reshape(SWA_HEADS, 1, 1), (SWA_HEADS, 1, BLOCK))
    qh_sw, kh_sw, vh_sw = _heads_rows(q_sw, SWA_HEADS), _heads_rows(k_sw, SWA_KV_HEADS), _heads_rows(v_sw, SWA_KV_HEADS)
    oT_sw = _swa_fwd(qh_sw, kh_sw, _heads_cols(v_sw, SWA_KV_HEADS), bias, sink_rows)
    swa_out = oT_sw.transpose(2, 0, 1).reshape(S, SWA_WIDTH)

    pre1, merged, h1b = _mix_out(sb_out, swa_out, sb_norm_g, swa_norm_g, w_out_f, h0, ln1_g, ln1_b)
    act, silu, dsilu_up = _ffn_up(h1b, w_gu_sh)
    dp2, dp2b, dg2, db2, errsum = _ffn_down_loss(act, w_down_f, pre1, ln1_g, ln1_b, ln2_g, ln2_b, tgt)

    g_w_down = _matmul_tn(act, dp2b, "grad_w_down", FF_CHUNK, D_MODEL)
    dgate, dup = _ffn_down_bwd(dp2b, w_down_f, silu, dsilu_up)
    g_w_gu = _matmul_tn_pair(h1b, dgate, dup, "grad_w_gate_up")
    dp1, dp1b, dg1, db1 = _ffn_up_bwd(dgate, dup, w_gu_sh, dp2, pre1, ln1_g)
    g_w_out = _matmul_tn(merged, dp1b, "grad_w_out", D_MODEL, D_MODEL)
    dsb, dsw, dgsb, dgsw = _mix_bwd(dp1b, w_out_f, sb_out, swa_out, sb_norm_g, swa_norm_g)

    c = lax.axis_index("c").astype(jnp.int32)
    me = (2 * lax.axis_index("x") + lax.axis_index("y")).astype(jnp.int32)
    grads_a = [g_w_out.reshape(N_CHIPS, D_MODEL // N_CHIPS, D_MODEL), g_w_gu, g_w_down.reshape(N_CHIPS, D_FF // N_CHIPS, D_MODEL)]
    names_a = ("w_out", "w_gate_up", "w_down")
    dqT_sw, dkh_sw, dvh_sw, dbias, dsink, swapped_a = _swa_bwd(qh_sw, kh_sw, _heads_cols(k_sw, SWA_KV_HEADS), vh_sw, bias,
                                                               sink_rows, _heads_rows(dsw, SWA_HEADS), grads_a)
    swa_small = _swa_small_grads(dbias, dsink, bucket)
    partials_a = [_pair_sum(g, r, c, "pair_sum_" + n) for g, r, n in zip(grads_a, swapped_a, names_a)]
    dq_sb, dk_sb, dv_sb, recv_a = _sb_bwd(qT_sb, kb_sb, _key_blocks_t(k_sb, SB_HEADS, T), _key_blocks(v_sb, SB_HEADS, T),
                                             _heads_cols(dsb, SB_HEADS), rsave, partials_a)
    tok = lambda t, nh: t.reshape(nh, S, HEAD_DIM).transpose(1, 0, 2).reshape(S, nh * HEAD_DIM)
    dproj = [dq_sb, dk_sb, dv_sb, dqT_sw.transpose(2, 0, 1).reshape(S, SWA_WIDTH).astype(_MXU),
             jnp.concatenate([tok(dkh_sw, SWA_KV_HEADS), tok(dvh_sw, SWA_KV_HEADS)], axis=1).astype(_MXU)]
    g_w_in = jnp.concatenate([_matmul_tn(h0b, d, "grad_w_in_%d" % k, D_MODEL, d.shape[1]) for k, d in enumerate(dproj)],
                             axis=1)

    cin = IN_COLS // N_CHIPS
    grads_b = [jnp.stack([g_w_in[:, j * cin:(j + 1) * cin] for j in range(N_CHIPS)])]
    partials_b = [_pair_sum(grads_b[0], _swap_halves(grads_b, "swap_halves_in")[0], c, "pair_sum_w_in")]
    grad_x, dg_in, db_in, recv_b = _in_proj_bwd(dproj, w_in_f, dp1, x2, row(ln_in_g), partials_b)
    names = ("w_in",) + names_a
    sums = [_chip_sum(p, r, me, "chip_sum_" + n) for p, r, n in zip(partials_b + partials_a, list(recv_b) + list(recv_a), names)]
    gs_in, gs_out, gs_gu, gs_down = _join_halves(sums)

    nrb = REL_BUCKETS * SWA_HEADS
    small = _pack_small(dg_in, db_in, dgsb, dgsw, swa_small[REL_BUCKETS, :SWA_HEADS],
                        swa_small[:REL_BUCKETS, :SWA_HEADS], dg1, db1, dg2, db2, errsum)
    g_small, loss_tile = _allreduce_small(small)
    loss = loss_tile[0, 0]

    big = []
    for name, w, g, m, v in (("adamw_w_in", w_in, gs_in, m_w_in, v_w_in), ("adamw_w_out", w_out, gs_out, m_w_out, v_w_out),
                             ("adamw_w_gate_up", w_gate_up, gs_gu, m_w_gate_up, v_w_gate_up),
                             ("adamw_w_down", w_down, gs_down, m_w_down, v_w_down)):
        d, nm, nv = _adamw(w[0], g, m[0], v[0], name)
        big.append((g[None], d[None], nm[None], nv[None]))
    zero = jnp.zeros((1,), F32)
    w_small = _pack_small(ln_in_g, ln_in_b, sb_norm_g, swa_norm_g, sinks, rel_bias, ln1_g, ln1_b, ln2_g, ln2_b, zero)
    m_small = _pack_small(m_ln_in_g, m_ln_in_b, m_sb_norm_g, m_swa_norm_g, m_sinks, m_rel_bias, m_ln1_g, m_ln1_b,
                          m_ln2_g, m_ln2_b, zero)
    v_small = _pack_small(v_ln_in_g, v_ln_in_b, v_sb_norm_g, v_swa_norm_g, v_sinks, v_rel_bias, v_ln1_g, v_ln1_b,
                          v_ln2_g, v_ln2_b, zero)
    small_out = [_unpack_small(t) for t in (g_small,) + tuple(_adamw(w_small, g_small, m_small, v_small, "adamw_small"))]

    def kind(k):
        s = small_out[k]
        return [s[0], s[1], big[0][k], s[2], s[3], s[4], s[5], big[1][k], s[6], s[7], big[2][k], big[3][k], s[8], s[9]]

    return (loss, grad_x.reshape(1, S, D_MODEL), *kind(0), *kind(1), *kind(2), *kind(3))
```

```python
import functools
import math

import numpy as np
import jax
import jax.numpy as jnp
from jax import lax
from jax.experimental import pallas as pl
from jax.experimental.pallas import tpu as pltpu

F32 = jnp.float32
_MXU = jnp.bfloat16

D_MODEL = 1024
HEAD_DIM = 64
SB_HEADS = 8
SWA_HEADS = 8
SWA_KV_HEADS = 2
SWA_GROUP = SWA_HEADS // SWA_KV_HEADS
SB_WIDTH = SB_HEADS * HEAD_DIM
SWA_WIDTH = SWA_HEADS * HEAD_DIM
SWA_KV_WIDTH = SWA_KV_HEADS * HEAD_DIM
IN_COLS = 3 * SB_WIDTH + SWA_WIDTH + 2 * SWA_KV_WIDTH
BLOCK = 128
REL_BUCKETS = 32
REL_MAX_DIST = 128
D_FF = 2816
FF_CHUNK = D_FF // 2
ALPHA = 2.0 ** 0.25
LN_EPS = 1e-5
RMS_EPS = 1e-6
SCALE = HEAD_DIM ** -0.5
SB_TILE = 256
SB_GROUP_FWD = 8
SB_GROUP_BWD = 4
SB_DEAD = -105.0
SWA_SUB = 8

ADAM_LR = 0.001
ADAM_B1 = 0.9
ADAM_B2 = 0.999
ADAM_EPS = 1e-08
ADAM_WD = 0.01
ADAM_STEP = 10

N_CHIPS = 4
SMALL_ROWS = 16

MESH = pl.DeviceIdType.MESH


def _sds(shape, dtype):
    return jax.ShapeDtypeStruct(shape, dtype)


def _cp(sem=None, vmem_mb=48):
    kw = dict(vmem_limit_bytes=vmem_mb * 1024 * 1024)
    if sem is not None:
        kw["dimension_semantics"] = sem
    return pltpu.CompilerParams(**kw)


def _dot(a, b):
    return jnp.dot(a, b, preferred_element_type=F32)


def _dot_nt(a, b):
    return lax.dot_general(a, b, (((1,), (1,)), ((), ())), preferred_element_type=F32)


def _dot_tn(a, b):
    return lax.dot_general(a, b, (((0,), (0,)), ((), ())), preferred_element_type=F32)


def _ln_hat(x):
    mu = jnp.mean(x, axis=-1, keepdims=True)
    xc = x - mu
    var = jnp.mean(xc * xc, axis=-1, keepdims=True)
    rstd = lax.rsqrt(var + LN_EPS)
    return xc * rstd, rstd


def _ln_bwd(xhat, rstd, dy, g):
    dxh = dy * g
    m1 = jnp.mean(dxh, axis=-1, keepdims=True)
    m2 = jnp.mean(dxh * xhat, axis=-1, keepdims=True)
    return rstd * (dxh - m1 - xhat * m2)


def _colsum(x):
    return jnp.sum(x, axis=0, keepdims=True)


def _split2(x):
    hi = x.astype(_MXU)
    lo = (x - hi.astype(F32)).astype(_MXU)
    return hi, lo


def _rows(tm, n):
    return pl.BlockSpec((tm, n), lambda i: (i, 0))


def _fixed(*shape):
    nd = len(shape)
    return pl.BlockSpec(shape, lambda i: (0,) * nd)


IN_SECTIONS = (SB_WIDTH, SB_WIDTH, SB_WIDTH, SWA_WIDTH, 2 * SWA_KV_WIDTH)


def _ln_in_proj(x, g, b, w):
    S = x.shape[0]
    tm = min(S, 512)
    offs = np.cumsum((0,) + IN_SECTIONS)

    def body(x_ref, g_ref, b_ref, w_ref, h_ref, hb_ref, *p_refs):
        xhat, _ = _ln_hat(x_ref[...])
        h = xhat * g_ref[...] + b_ref[...]
        h_ref[...] = h
        hb = h.astype(_MXU)
        hb_ref[...] = hb
        proj = _dot(hb, w_ref[...])
        for k, p_ref in enumerate(p_refs):
            p_ref[...] = proj[:, offs[k]:offs[k + 1]].astype(p_ref.dtype)

    return pl.pallas_call(
        body, name="ln_in_proj", grid=(S // tm,),
        in_specs=[_rows(tm, D_MODEL), _fixed(1, D_MODEL), _fixed(1, D_MODEL), _fixed(D_MODEL, IN_COLS)],
        out_specs=[_rows(tm, D_MODEL), _rows(tm, D_MODEL)] + [_rows(tm, n) for n in IN_SECTIONS],
        out_shape=[_sds((S, D_MODEL), F32), _sds((S, D_MODEL), _MXU)] + [_sds((S, n), _MXU) for n in IN_SECTIONS],
        compiler_params=_cp(("parallel",)),
    )(x, g, b, w)


def _rms(x, g):
    r = lax.rsqrt(jnp.mean(x * x, axis=-1, keepdims=True) + RMS_EPS)
    return x * r * g, r


def _mix_out(sb, sw, gsb, gsw, w_out, h0, g1, b1):
    S = sb.shape[0]
    tm = min(S, 512)

    def body(sb_ref, sw_ref, gsb_ref, gsw_ref, w_ref, h0_ref, g1_ref, b1_ref, pre_ref, mg_ref, h1_ref):
        ysb, _ = _rms(sb_ref[...], gsb_ref[...])
        ysw, _ = _rms(sw_ref[...], gsw_ref[...])
        ysb = ysb.astype(_MXU)
        ysw = ysw.astype(_MXU)
        mg_ref[:, :SB_WIDTH] = ysb
        mg_ref[:, SB_WIDTH:] = ysw
        mix = _dot(ysb, w_ref[:SB_WIDTH, :]) + _dot(ysw, w_ref[SB_WIDTH:, :])
        pre1 = ALPHA * h0_ref[...] + mix
        pre_ref[...] = pre1
        xhat, _ = _ln_hat(pre1)
        h1_ref[...] = (xhat * g1_ref[...] + b1_ref[...]).astype(h1_ref.dtype)

    vec = _fixed(1, D_MODEL)
    return pl.pallas_call(
        body, name="mix_out", grid=(S // tm,),
        in_specs=[_rows(tm, SB_WIDTH), _rows(tm, SWA_WIDTH), _fixed(1, SB_WIDTH), _fixed(1, SWA_WIDTH),
                  _fixed(D_MODEL, D_MODEL), _rows(tm, D_MODEL), vec, vec],
        out_specs=[_rows(tm, D_MODEL), _rows(tm, D_MODEL), _rows(tm, D_MODEL)],
        out_shape=[_sds((S, D_MODEL), F32), _sds((S, D_MODEL), _MXU), _sds((S, D_MODEL), _MXU)],
        compiler_params=_cp(("parallel",)),
    )(sb, sw, gsb, gsw, w_out, h0, g1, b1)


def _sigmoid(x):
    return 1.0 / (1.0 + jnp.exp(-x))


def _ffn_up(h1b, wgu):
    S = h1b.shape[0]
    tm = min(S, 1024)

    def body(h_ref, wg_ref, wu_ref, a_ref, s1_ref, s2_ref):
        h1 = h_ref[...]
        gate = _dot(h1, wg_ref[0])
        up = _dot(h1, wu_ref[0])
        sg = _sigmoid(gate)
        silu = gate * sg
        a_ref[...] = (silu * up).astype(a_ref.dtype)
        s1_ref[...] = silu.astype(s1_ref.dtype)
        s2_ref[...] = (up * (sg * (1.0 + gate * (1.0 - sg)))).astype(s2_ref.dtype)

    chunk = pl.BlockSpec((tm, FF_CHUNK), lambda j, i: (i, j))
    return pl.pallas_call(
        body, name="ffn_up", grid=(2, S // tm),
        in_specs=[pl.BlockSpec((tm, D_MODEL), lambda j, i: (i, 0)),
                  pl.BlockSpec((1, D_MODEL, FF_CHUNK), lambda j, i: (j, 0, 0)),
                  pl.BlockSpec((1, D_MODEL, FF_CHUNK), lambda j, i: (j + 2, 0, 0))],
        out_specs=[chunk, chunk, chunk],
        out_shape=[_sds((S, D_FF), _MXU)] * 3,
        compiler_params=_cp(("arbitrary", "arbitrary"), vmem_mb=56),
    )(h1b, wgu, wgu)


def _ffn_down_loss(a, w_down, pre1, g1, b1, g2, b2, tgt):
    S = a.shape[0]
    tm = min(S, 512)

    def body(a_ref, w_ref, p_ref, g1_ref, b1_ref, g2_ref, b2_ref, t_ref, d_ref, db_ref, dg2_ref, db2_ref, err_ref):
        @pl.when(pl.program_id(0) == 0)
        def _():
            dg2_ref[...] = jnp.zeros_like(dg2_ref)
            db2_ref[...] = jnp.zeros_like(db2_ref)
            err_ref[...] = jnp.zeros_like(err_ref)

        xhat1, _ = _ln_hat(p_ref[...])
        h1 = xhat1 * g1_ref[...] + b1_ref[...]
        pre2 = ALPHA * h1 + _dot(a_ref[...], w_ref[...])
        xhat2, rstd2 = _ln_hat(pre2)
        err = xhat2 * g2_ref[...] + b2_ref[...] - t_ref[...]
        dh2 = err * (1.0 / D_MODEL)
        dp2 = _ln_bwd(xhat2, rstd2, dh2, g2_ref[...])
        d_ref[...] = dp2
        db_ref[...] = dp2.astype(db_ref.dtype)
        dg2_ref[...] += _colsum(dh2 * xhat2)
        db2_ref[...] += _colsum(dh2)
        err_ref[...] += _colsum(err * err)

    vec = _fixed(1, D_MODEL)
    return pl.pallas_call(
        body, name="ffn_down_loss", grid=(S // tm,),
        in_specs=[_rows(tm, D_FF), _fixed(D_FF, D_MODEL), _rows(tm, D_MODEL), vec, vec, vec, vec, _rows(tm, D_MODEL)],
        out_specs=[_rows(tm, D_MODEL), _rows(tm, D_MODEL), vec, vec, vec],
        out_shape=[_sds((S, D_MODEL), F32), _sds((S, D_MODEL), _MXU), _sds((1, D_MODEL), F32), _sds((1, D_MODEL), F32),
                   _sds((1, D_MODEL), F32)],
        compiler_params=_cp(("arbitrary",)),
    )(a, w_down, pre1, g1, b1, g2, b2, tgt)


def _ffn_down_bwd(dp2b, w_down, s1, s2):
    S = dp2b.shape[0]
    tm = min(S, 512)

    def body(d_ref, w_ref, s1_ref, s2_ref, dg_ref, du_ref):
        da = _dot_nt(d_ref[...], w_ref[...])
        du_ref[...] = (da * s1_ref[...].astype(F32)).astype(du_ref.dtype)
        dg_ref[...] = (da * s2_ref[...].astype(F32)).astype(dg_ref.dtype)

    chunk = pl.BlockSpec((tm, FF_CHUNK), lambda j, i: (i, j))
    return pl.pallas_call(
        body, name="ffn_down_bwd", grid=(2, S // tm),
        in_specs=[pl.BlockSpec((tm, D_MODEL), lambda j, i: (i, 0)),
                  pl.BlockSpec((FF_CHUNK, D_MODEL), lambda j, i: (j, 0)), chunk, chunk],
        out_specs=[chunk, chunk],
        out_shape=[_sds((S, D_FF), _MXU), _sds((S, D_FF), _MXU)],
        compiler_params=_cp(("arbitrary", "arbitrary")),
    )(dp2b, w_down, s1, s2)


def _ffn_up_bwd(dgate, dup, wgu, dp2, pre1, g1):
    S = dgate.shape[0]
    tm = min(S, 256)

    def body(dg_ref, du_ref, w_ref, d2_ref, p_ref, g_ref, d1_ref, d1b_ref, dg1_ref, db1_ref):
        @pl.when(pl.program_id(0) == 0)
        def _():
            dg1_ref[...] = jnp.zeros_like(dg1_ref)
            db1_ref[...] = jnp.zeros_like(db1_ref)

        dh1 = ALPHA * d2_ref[...]
        for j in range(2):
            cols = slice(j * FF_CHUNK, (j + 1) * FF_CHUNK)
            dh1 += _dot_nt(dg_ref[:, cols], w_ref[j])
            dh1 += _dot_nt(du_ref[:, cols], w_ref[j + 2])
        xhat, rstd = _ln_hat(p_ref[...])
        dp1 = _ln_bwd(xhat, rstd, dh1, g_ref[...])
        d1_ref[...] = dp1
        d1b_ref[...] = dp1.astype(d1b_ref.dtype)
        dg1_ref[...] += _colsum(dh1 * xhat)
        db1_ref[...] += _colsum(dh1)

    vec = _fixed(1, D_MODEL)
    return pl.pallas_call(
        body, name="ffn_up_bwd", grid=(S // tm,),
        in_specs=[_rows(tm, D_FF), _rows(tm, D_FF), _fixed(4, D_MODEL, FF_CHUNK), _rows(tm, D_MODEL),
                  _rows(tm, D_MODEL), vec],
        out_specs=[_rows(tm, D_MODEL), _rows(tm, D_MODEL), vec, vec],
        out_shape=[_sds((S, D_MODEL), F32), _sds((S, D_MODEL), _MXU), _sds((1, D_MODEL), F32), _sds((1, D_MODEL), F32)],
        compiler_params=_cp(("arbitrary",), vmem_mb=56),
    )(dgate, dup, wgu, dp2, pre1, g1)


def _rms_bwd(x, g, dy):
    n = x.shape[-1]
    r = lax.rsqrt(jnp.mean(x * x, axis=-1, keepdims=True) + RMS_EPS)
    u = dy * g
    dx = r * u - x * (r * r * r) * (jnp.sum(u * x, axis=-1, keepdims=True) * (1.0 / n))
    return dx, _colsum(dy * x * r)


def _mix_bwd(dp1b, w_out, sb, sw, gsb, gsw):
    S = sb.shape[0]
    tm = min(S, 512)

    def body(d_ref, w_ref, sb_ref, sw_ref, gsb_ref, gsw_ref, dsb_ref, dsw_ref, dgsb_ref, dgsw_ref):
        @pl.when(pl.program_id(0) == 0)
        def _():
            dgsb_ref[...] = jnp.zeros_like(dgsb_ref)
            dgsw_ref[...] = jnp.zeros_like(dgsw_ref)

        dm = _dot_nt(d_ref[...], w_ref[...])
        dsb, dgsb = _rms_bwd(sb_ref[...], gsb_ref[...], dm[:, :SB_WIDTH])
        dsw, dgsw = _rms_bwd(sw_ref[...], gsw_ref[...], dm[:, SB_WIDTH:])
        dsb_ref[...] = dsb.astype(dsb_ref.dtype)
        dsw_ref[...] = dsw.astype(dsw_ref.dtype)
        dgsb_ref[...] += dgsb
        dgsw_ref[...] += dgsw

    return pl.pallas_call(
        body, name="mix_bwd", grid=(S // tm,),
        in_specs=[_rows(tm, D_MODEL), _fixed(D_MODEL, D_MODEL), _rows(tm, SB_WIDTH), _rows(tm, SWA_WIDTH),
                  _fixed(1, SB_WIDTH), _fixed(1, SWA_WIDTH)],
        out_specs=[_rows(tm, SB_WIDTH), _rows(tm, SWA_WIDTH), _fixed(1, SB_WIDTH), _fixed(1, SWA_WIDTH)],
        out_shape=[_sds((S, SB_WIDTH), _MXU), _sds((S, SWA_WIDTH), _MXU), _sds((1, SB_WIDTH), F32),
                   _sds((1, SWA_WIDTH), F32)],
        compiler_params=_cp(("arbitrary",)),
    )(dp1b, w_out, sb, sw, gsb, gsw)


def _in_proj_bwd(dproj, w_in, dp1, x, g, parts):
    S = x.shape[0]
    tm = min(S, 512)
    nw = len(parts)
    ns = len(IN_SECTIONS)
    offs = np.cumsum((0,) + IN_SECTIONS)
    s_ins, s_outs, s_sems = _scatter_io(parts)

    def body(*refs):
        dpj_refs = refs[:ns]
        w_ref, d1_ref, x_ref, g_ref = refs[ns:ns + 4]
        rest = refs[ns + 4:]
        gx_ref, dg_ref, db_ref = rest[nw:nw + 3]
        scatter = _Scatter(rest[:nw], rest[nw + 3:2 * nw + 3], *rest[2 * nw + 3:])

        @pl.when(pl.program_id(0) == 0)
        def _():
            scatter.start()
            dg_ref[...] = jnp.zeros_like(dg_ref)
            db_ref[...] = jnp.zeros_like(db_ref)

        dh0 = ALPHA * d1_ref[...]
        for k in range(ns):
            dh0 += _dot_nt(dpj_refs[k][...], w_ref[:, offs[k]:offs[k + 1]])
        xhat, rstd = _ln_hat(x_ref[...])
        gx_ref[...] = _ln_bwd(xhat, rstd, dh0, g_ref[...])
        dg_ref[...] += _colsum(dh0 * xhat)
        db_ref[...] += _colsum(dh0)

        @pl.when(pl.program_id(0) == pl.num_programs(0) - 1)
        def _():
            scatter.finish()

    vec = _fixed(1, D_MODEL)
    any_spec = pl.BlockSpec(memory_space=pl.ANY)
    res = pl.pallas_call(
        body, name="in_proj_bwd", grid=(S // tm,),
        in_specs=[_rows(tm, n) for n in IN_SECTIONS]
                 + [_fixed(D_MODEL, IN_COLS), _rows(tm, D_MODEL), _rows(tm, D_MODEL), vec] + [any_spec] * nw,
        out_specs=[_rows(tm, D_MODEL), vec, vec] + [any_spec] * nw,
        out_shape=[_sds((S, D_MODEL), F32), _sds((1, D_MODEL), F32), _sds((1, D_MODEL), F32)] + s_outs,
        scratch_shapes=s_sems,
        compiler_params=_cp(("arbitrary",)),
    )(*dproj, w_in, dp1, x, g, *s_ins)
    return res[0], res[1], res[2], list(res[3:])


def _matmul_tn(a, b, name, tk, tn):
    T, K = a.shape
    N = b.shape[1]
    tt = min(T, 1024)

    def body(a_ref, b_ref, o_ref):
        @pl.when(pl.program_id(2) == 0)
        def _():
            o_ref[...] = jnp.zeros_like(o_ref)

        o_ref[...] += _dot_tn(a_ref[...], b_ref[...])

    return pl.pallas_call(
        body, name=name, grid=(K // tk, N // tn, T // tt),
        in_specs=[pl.BlockSpec((tt, tk), lambda k, n, t: (t, k)), pl.BlockSpec((tt, tn), lambda k, n, t: (t, n))],
        out_specs=pl.BlockSpec((tk, tn), lambda k, n, t: (k, n)),
        out_shape=_sds((K, N), F32),
        compiler_params=_cp(("parallel", "parallel", "arbitrary")),
    )(a, b)


def _place():
    x, y, c = lax.axis_index("x"), lax.axis_index("y"), lax.axis_index("c")
    chips = [(1 - x, y), (x, 1 - y), (1 - x, 1 - y)]
    return x, y, c, chips


class _Gather:
    def __init__(self, in_refs, out_refs, send_sems, recv_sems):
        self.in_refs, self.out_refs, self.send_sems, self.recv_sems = in_refs, out_refs, send_sems, recv_sems
        self.x, self.y, self.c, self.chips = _place()

    def _copy(self, w, k, chip, hc, to, src=None):
        part = self.out_refs[w].at[2 * chip[0] + chip[1], hc]
        return pltpu.make_async_remote_copy(
            src_ref=part if src is None else src, dst_ref=part, send_sem=self.send_sems.at[w, k],
            recv_sem=self.recv_sems.at[w, k], device_id=to, device_id_type=MESH)

    def _first(self):
        x, y, c = self.x, self.y, self.c
        return [self._copy(w, j, (x, y), c, (*chip, c), src=self.in_refs[w].at[c])
                for w in range(len(self.in_refs)) for j, chip in enumerate(self.chips)]

    def start(self):
        for cp in self._first():
            cp.start()

    def finish(self):
        x, y, c = self.x, self.y, self.c
        me, sibling = (x, y, c), (x, y, 1 - c)
        ws = range(len(self.in_refs))
        passed = []
        for w in ws:
            for j, chip in enumerate(self.chips):
                self._copy(w, j, chip, c, me).wait_recv()
                passed.append(self._copy(w, 3 + j, chip, c, sibling))
                passed[-1].start()
        for w in ws:
            for j, chip in enumerate(self.chips):
                self._copy(w, 3 + j, chip, 1 - c, me).wait_recv()
        for cp in self._first() + passed:
            cp.wait_send()


def _gather_io(shards):
    halves = [(s.shape[0] // 2, s.shape[1]) for s in shards]
    ins = [s.reshape(2, h, cols) for s, (h, cols) in zip(shards, halves)]
    outs = [_sds((N_CHIPS, 2, h, cols), s.dtype) for s, (h, cols) in zip(shards, halves)]
    sems = [pltpu.SemaphoreType.DMA((len(shards), 6)), pltpu.SemaphoreType.DMA((len(shards), 6))]
    return ins, outs, sems


def _gather_assemble(outs, shards):
    me = 2 * lax.axis_index("x") + lax.axis_index("y")
    return [lax.dynamic_update_slice_in_dim(o.reshape((N_CHIPS,) + s.shape), s[None], me, axis=0)
            for o, s in zip(outs, shards)]


class _Scatter:
    def __init__(self, p_refs, out_refs, send_sems, recv_sems):
        self.p_refs, self.out_refs, self.send_sems, self.recv_sems = p_refs, out_refs, send_sems, recv_sems
        self.x, self.y, self.c, self.chips = _place()
        self.me = 2 * self.x + self.y

    def _copy(self, w, j, chip, src_chip, dst_chip):
        return pltpu.make_async_remote_copy(
            src_ref=self.p_refs[w].at[src_chip], dst_ref=self.out_refs[w].at[dst_chip], send_sem=self.send_sems.at[w, j],
            recv_sem=self.recv_sems.at[w, j], device_id=(*chip, self.c), device_id_type=MESH)

    def _sends(self):
        return [self._copy(w, j, chip, 2 * chip[0] + chip[1], self.me)
                for w in range(len(self.p_refs)) for j, chip in enumerate(self.chips)]

    def start(self):
        for cp in self._sends():
            cp.start()

    def finish(self):
        for w in range(len(self.p_refs)):
            for j, chip in enumerate(self.chips):
                self._copy(w, j, chip, self.me, 2 * chip[0] + chip[1]).wait_recv()
        for cp in self._sends():
            cp.wait_send()


def _scatter_io(parts):
    sems = [pltpu.SemaphoreType.DMA((len(parts), 3)), pltpu.SemaphoreType.DMA((len(parts), 3))]
    return list(parts), [_sds(p.shape, p.dtype) for p in parts], sems


class _Swap:
    def __init__(self, g_refs, out_refs, send_sems, recv_sems):
        x, y, c, _ = _place()
        self.copies = []
        for w in range(len(g_refs)):
            half = out_refs[w].shape[1]
            theirs = g_refs[w].at[:, pl.ds(pl.multiple_of((1 - c) * half, 8), half), :]
            self.copies.append(pltpu.make_async_remote_copy(
                src_ref=theirs, dst_ref=out_refs[w], send_sem=send_sems.at[w], recv_sem=recv_sems.at[w],
                device_id=(x, y, 1 - c), device_id_type=MESH))

    def start(self):
        for cp in self.copies:
            cp.start()

    def finish(self):
        for cp in self.copies:
            cp.wait()


def _swap_io(grads):
    outs = [_sds((g.shape[0], g.shape[1] // 2, g.shape[2]), g.dtype) for g in grads]
    return list(grads), outs, [pltpu.SemaphoreType.DMA((len(grads),)), pltpu.SemaphoreType.DMA((len(grads),))]


def _matmul_tn_pair(a, b0, b1, name):
    T, K = a.shape
    tt = min(T, 1024)

    def body(a_ref, b0_ref, b1_ref, o_ref):
        n = pl.program_id(0)

        @pl.when(pl.program_id(1) == 0)
        def _():
            o_ref[...] = jnp.zeros_like(o_ref)

        @pl.when(n < 2)
        def _():
            o_ref[0] += _dot_tn(a_ref[...], b0_ref[...])

        @pl.when(n >= 2)
        def _():
            o_ref[0] += _dot_tn(a_ref[...], b1_ref[...])

    return pl.pallas_call(
        body, name=name, grid=(4, T // tt),
        in_specs=[pl.BlockSpec((tt, K), lambda n, t: (t, 0)),
                  pl.BlockSpec((tt, FF_CHUNK), lambda n, t: (t, jnp.minimum(n, 1))),
                  pl.BlockSpec((tt, FF_CHUNK), lambda n, t: (t, jnp.maximum(n - 2, 0)))],
        out_specs=pl.BlockSpec((1, K, FF_CHUNK), lambda n, t: (n, 0, 0)),
        out_shape=_sds((4, K, FF_CHUNK), F32),
        compiler_params=_cp(("parallel", "arbitrary")),
    )(a, b0, b1)


def _sb_logs(zt, causal):
    e = jnp.exp(-jnp.abs(zt))
    lb = jnp.minimum(zt, 0.0) - jnp.log(1.0 + e)
    l1m = lb - zt
    if causal is not None:
        l1m = jnp.where(causal, l1m, 0.0)
    return lb, l1m


def _sb_weights(lb, suf, causal):
    a = jnp.exp(lb + suf)
    if causal is not None:
        a = jnp.where(causal, a, 0.0)
    return a


def _tri_masks(t):
    r = lax.broadcasted_iota(jnp.int32, (t, t), 0)
    c = lax.broadcasted_iota(jnp.int32, (t, t), 1)
    return r, c


def _sb_fwd(qT, kb, vTb, shards):
    Hh, _, S = qT.shape
    nk, T = kb.shape[1], kb.shape[2]
    nq = S // T
    G = SB_GROUP_FWD
    nw = len(shards)
    g_ins, g_outs, g_sems = _gather_io(shards)

    def body(qT_ref, k_ref, vT_ref, *rest):
        o_ref, rs_ref = rest[nw:nw + 2]
        gather = _Gather(rest[:nw], rest[nw + 2:2 * nw + 2], *rest[2 * nw + 2:])
        i = pl.program_id(1)
        first_step = jnp.logical_and(pl.program_id(0) == 0, i == 0)
        last_step = jnp.logical_and(pl.program_id(0) == pl.num_programs(0) - 1, i == pl.num_programs(1) - 1)

        @pl.when(first_step)
        def _():
            gather.start()

        qts = [(qT_ref[g].astype(F32) * SCALE).astype(_MXU) for g in range(G)]
        r, c = _tri_masks(T)
        upper = (c > r).astype(_MXU)
        causal = r < c

        def blk(j, carry, mask):
            hs = range(G)
            for g in hs:
                rs_ref[g, 0, j] = jnp.broadcast_to(carry[g][0], (8, T))
            zs = [_dot(k_ref[g, j], qts[g]) for g in hs]
            lbs, l1ms = zip(*[_sb_logs(zs[g], mask) for g in hs])
            splits = [_split2(l1ms[g]) for g in hs]
            cums = [_dot(upper, splits[g][0]) + _dot(upper, splits[g][1]) for g in hs]
            avs = [_sb_weights(lbs[g], carry[g][0] + cums[g], mask).astype(_MXU) for g in hs]
            accs = [carry[g][1] + _dot(vT_ref[g, j], avs[g]) for g in hs]
            return tuple((carry[g][0] + _colsum(l1ms[g]), accs[g]) for g in hs)

        def go_on(j, carry):
            top = carry[0][0]
            for g in range(1, G):
                top = jnp.maximum(top, carry[g][0])
            return jnp.logical_and(j >= 0, jnp.max(top) >= SB_DEAD)

        init = tuple((jnp.zeros((1, T), F32), jnp.zeros((HEAD_DIM, T), F32)) for _ in range(G))
        carry = blk(i, init, causal)
        j, carry = lax.while_loop(lambda st: go_on(*st), lambda st: (st[0] - 1, blk(st[0], st[1], None)),
                                  (i - 1, carry))

        @pl.when(j >= 0)
        def _():
            for g in range(G):
                rs_ref[g, 0, j] = jnp.broadcast_to(carry[g][0], (8, T))

        o_ref[...] = jnp.concatenate([carry[g][1] for g in range(G)], axis=0).T

        @pl.when(last_step)
        def _():
            gather.finish()

    any_spec = pl.BlockSpec(memory_space=pl.ANY)
    res = pl.pallas_call(
        body, name="sb_fwd", grid=(Hh // G, nq),
        in_specs=[pl.BlockSpec((G, HEAD_DIM, T), lambda h, i: (h, 0, i)),
                  pl.BlockSpec((G, nk, T, HEAD_DIM), lambda h, i: (h, 0, 0, 0), pipeline_mode=pl.Buffered(1)),
                  pl.BlockSpec((G, nk, HEAD_DIM, T), lambda h, i: (h, 0, 0, 0), pipeline_mode=pl.Buffered(1))]
                 + [any_spec] * nw,
        out_specs=[pl.BlockSpec((T, G * HEAD_DIM), lambda h, i: (i, h)),
                   pl.BlockSpec((G, 1, nk, 8, T), lambda h, i: (h, i, 0, 0, 0))] + [any_spec] * nw,
        out_shape=[_sds((S, Hh * HEAD_DIM), F32), _sds((Hh, nq, nk, 8, T), F32)] + g_outs,
        scratch_shapes=g_sems,
        compiler_params=_cp(("arbitrary", "arbitrary")),
    )(qT, kb, vTb, *g_ins)
    return res[0], res[1], _gather_assemble(res[2:], shards)


def _sb_bwd(qT, kb, kTb, vb, doT, rsave, parts):
    Hh, _, S = qT.shape
    nk, T = kb.shape[1], kb.shape[2]
    nq = S // T
    G = SB_GROUP_BWD
    nw = len(parts)
    s_ins, s_outs, s_sems = _scatter_io(parts)

    def body(qT_ref, k_ref, kT_ref, v_ref, doT_ref, rs_ref, *rest):
        dq_ref, dk_out_ref, dv_out_ref = rest[nw:nw + 3]
        dk_ref, dv_ref = rest[2 * nw + 3:2 * nw + 5]
        scatter = _Scatter(rest[:nw], rest[nw + 3:2 * nw + 3], *rest[2 * nw + 5:])
        i = pl.program_id(1)
        first_step = jnp.logical_and(pl.program_id(0) == 0, i == 0)
        last_step = jnp.logical_and(pl.program_id(0) == pl.num_programs(0) - 1, i == pl.num_programs(1) - 1)

        @pl.when(first_step)
        def _():
            scatter.start()

        @pl.when(i == 0)
        def _():
            dk_ref[...] = jnp.zeros_like(dk_ref)
            dv_ref[...] = jnp.zeros_like(dv_ref)

        qts = [(qT_ref[g].astype(F32) * SCALE).astype(_MXU) for g in range(G)]
        douts = [doT_ref[g] for g in range(G)]
        r, c = _tri_masks(T)
        upper = (c > r).astype(_MXU)
        lower = (c < r).astype(_MXU)
        causal = r < c

        def blk(j, carry, mask):
            hs = range(G)
            zs = [_dot(k_ref[g, j], qts[g]) for g in hs]
            das = [_dot(v_ref[g, j], douts[g]) for g in hs]
            lbs, l1ms = zip(*[_sb_logs(zs[g], mask) for g in hs])
            splits = [_split2(l1ms[g]) for g in hs]
            cums = [_dot(upper, splits[g][0]) + _dot(upper, splits[g][1]) for g in hs]
            avs = [_sb_weights(lbs[g], rs_ref[g, 0, j][0:1, :] + cums[g], mask) for g in hs]
            ets = [das[g] * avs[g] for g in hs]
            esplits = [_split2(ets[g]) for g in hs]
            ecums = [_dot(lower, esplits[g][0]) + _dot(lower, esplits[g][1]) for g in hs]
            dzs = []
            for g in hs:
                sig = jnp.exp(lbs[g])
                dz = ets[g] * (1.0 - sig) - (carry[g][0] + ecums[g]) * sig
                if mask is not None:
                    dz = jnp.where(mask, dz, 0.0)
                dzs.append(dz.astype(_MXU))
            dqs = [carry[g][1] + _dot(kT_ref[g, j], dzs[g]) for g in hs]
            for g in hs:
                dk_ref[j, g * HEAD_DIM:(g + 1) * HEAD_DIM, :] += _dot_nt(qts[g], dzs[g])
            for g in hs:
                dv_ref[j, g * HEAD_DIM:(g + 1) * HEAD_DIM, :] += _dot_nt(douts[g], avs[g].astype(_MXU))
            return tuple((carry[g][0] + _colsum(ets[g]), dqs[g]) for g in hs)

        def live(j):
            jj = jnp.maximum(j, 0)
            top = rs_ref[0, 0, jj][0:1, :]
            for g in range(1, G):
                top = jnp.maximum(top, rs_ref[g, 0, jj][0:1, :])
            return jnp.logical_and(j >= 0, jnp.max(top) >= SB_DEAD)

        first = lax.while_loop(lambda st: st[1], lambda st: (st[0] - 1, live(st[0] - 2)), (i, live(i - 1)))[0]
        carry = tuple((jnp.zeros((1, T), F32), jnp.zeros((HEAD_DIM, T), F32)) for _ in range(G))
        carry = lax.fori_loop(first, i, lambda s, cr: blk(s, cr, None), carry)
        carry = blk(i, carry, causal)
        dq_ref[...] = (jnp.concatenate([carry[g][1] for g in range(G)], axis=0) * SCALE).T.astype(dq_ref.dtype)

        @pl.when(i == pl.num_programs(1) - 1)
        def _():
            def flush(j, _):
                rows = pl.ds(pl.multiple_of(j * T, T), T)
                dk_out_ref[rows, :] = dk_ref[j].T.astype(dk_out_ref.dtype)
                dv_out_ref[rows, :] = dv_ref[j].T.astype(dv_out_ref.dtype)
                return 0
            lax.fori_loop(0, nk, flush, 0)

        @pl.when(last_step)
        def _():
            scatter.finish()

    colblk = pl.BlockSpec((G, HEAD_DIM, T), lambda h, i: (h, 0, i))
    once = pl.Buffered(1)
    kblk = pl.BlockSpec((G, nk, T, HEAD_DIM), lambda h, i: (h, 0, 0, 0), pipeline_mode=once)
    kTblk = pl.BlockSpec((G, nk, HEAD_DIM, T), lambda h, i: (h, 0, 0, 0), pipeline_mode=once)
    any_spec = pl.BlockSpec(memory_space=pl.ANY)
    res = pl.pallas_call(
        body, name="sb_bwd", grid=(Hh // G, nq),
        in_specs=[colblk, kblk, kTblk, kblk, colblk,
                  pl.BlockSpec((G, 1, nk, 8, T), lambda h, i: (h, i, 0, 0, 0))] + [any_spec] * nw,
        out_specs=[pl.BlockSpec((T, G * HEAD_DIM), lambda h, i: (i, h)),
                   pl.BlockSpec((S, G * HEAD_DIM), lambda h, i: (0, h), pipeline_mode=once),
                   pl.BlockSpec((S, G * HEAD_DIM), lambda h, i: (0, h), pipeline_mode=once)] + [any_spec] * nw,
        out_shape=[_sds((S, Hh * HEAD_DIM), _MXU)] * 3 + s_outs,
        scratch_shapes=[pltpu.VMEM((nk, G * HEAD_DIM, T), F32), pltpu.VMEM((nk, G * HEAD_DIM, T), F32)] + s_sems,
        compiler_params=_cp(("arbitrary", "arbitrary"), vmem_mb=60),
    )(qT, kb, kTb, vb, doT, rsave, *s_ins)
    return res[0], res[1], res[2], list(res[3:])


def _bucket_table():
    qi = np.arange(BLOCK)[:, None]
    cj = np.arange(2 * BLOCK)[None, :]
    dist = qi + BLOCK - cj
    exact = REL_BUCKETS // 2
    d = np.maximum(dist, 0)
    d_f = np.maximum(d, 1).astype(np.float32)
    large = exact + (np.log(d_f / np.float32(exact)) / np.float32(math.log(REL_MAX_DIST / exact))
                     * np.float32(REL_BUCKETS - exact)).astype(np.int32)
    large = np.minimum(large, REL_BUCKETS - 1)
    return np.where(d < exact, d, large).astype(np.int32)


def _swa_bias(rel_bias, bucket):
    def body(rb_ref, bk_ref, o_ref):
        bk = bk_ref[...]
        for h in range(SWA_HEADS):
            t = jnp.zeros((2 * BLOCK, BLOCK), F32)
            for b in range(REL_BUCKETS):
                t = jnp.where(bk == b, rb_ref[b, h], t)
            o_ref[h] = t

    return pl.pallas_call(
        body, name="swa_bias",
        in_specs=[pl.BlockSpec(memory_space=pltpu.SMEM), pl.BlockSpec(memory_space=pltpu.VMEM)],
        out_specs=pl.BlockSpec(memory_space=pltpu.VMEM),
        out_shape=_sds((SWA_HEADS, 2 * BLOCK, BLOCK), F32),
    )(rel_bias, bucket)


def _swa_logits(q, kp, kc):
    qs = (q.astype(F32) * SCALE).astype(_MXU)
    return qs, _dot_nt(kp, qs), _dot_nt(kc, qs)


def _swa_softmax(lp, lc, bias, sink, live_prev):
    r, c = _tri_masks(BLOCK)
    in_window = r > c if live_prev is None else jnp.logical_and(r > c, live_prev)
    lp = jnp.where(in_window, lp + bias[:BLOCK, :], -jnp.inf)
    lc = jnp.where(r <= c, lc + bias[BLOCK:, :], -jnp.inf)
    m = jnp.maximum(jnp.maximum(jnp.max(lp, axis=0, keepdims=True), jnp.max(lc, axis=0, keepdims=True)), sink)
    pp = jnp.exp(lp - m)
    pc = jnp.exp(lc - m)
    ps = jnp.exp(sink - m)
    denom = _colsum(pp) + _colsum(pc) + ps
    return pp / denom, pc / denom, ps / denom


def _swa_sub(nb):
    return min(SWA_SUB, nb)


def _swa_keys(b, prev_ref, cur_ref, i):
    cur = cur_ref[0, b * BLOCK:(b + 1) * BLOCK, :]
    if b == 0:
        return prev_ref[0], cur, i > 0
    return cur_ref[0, (b - 1) * BLOCK:b * BLOCK, :], cur, None


def _swa_keys_t(b, prev_ref, cur_ref):
    cur = cur_ref[0, :, b * BLOCK:(b + 1) * BLOCK]
    return (prev_ref[0] if b == 0 else cur_ref[0, :, (b - 1) * BLOCK:b * BLOCK]), cur


SWA_PAIR = 2


def _swa_fwd(q, k, vT, bias, sink):
    S = q.shape[1]
    nb = S // BLOCK
    ns = _swa_sub(nb)
    R = ns * BLOCK
    P = SWA_PAIR

    def body(q_ref, kp_ref, kc_ref, vp_ref, vc_ref, bias_ref, sink_ref, o_ref):
        i = pl.program_id(1)
        units = [(hh, b) for hh in range(P) for b in range(ns)]
        keys = [_swa_keys(b, kp_ref, kc_ref, i) for b in range(ns)]
        vals = [_swa_keys_t(b, vp_ref, vc_ref) for b in range(ns)]
        logits = {u: _swa_logits(q_ref[u[0], u[1] * BLOCK:(u[1] + 1) * BLOCK, :], keys[u[1]][0], keys[u[1]][1])
                  for u in units}
        ws = {u: _swa_softmax(logits[u][1], logits[u][2], bias_ref[u[0]], sink_ref[u[0]][:, :1], keys[u[1]][2])
              for u in units}
        outs = {u: _dot(vals[u[1]][0], ws[u][0].astype(_MXU)) + _dot(vals[u[1]][1], ws[u][1].astype(_MXU))
                for u in units}
        for b in range(ns):
            o_ref[b * BLOCK:(b + 1) * BLOCK, :] = jnp.concatenate([outs[(hh, b)] for hh in range(P)], axis=0).T

    kvh = lambda p: (p * P) // SWA_GROUP
    prev = pl.BlockSpec((1, BLOCK, HEAD_DIM), lambda p, i: (kvh(p), jnp.maximum(i * ns - 1, 0), 0))
    cur = pl.BlockSpec((1, R, HEAD_DIM), lambda p, i: (kvh(p), i, 0))
    prev_t = pl.BlockSpec((1, HEAD_DIM, BLOCK), lambda p, i: (kvh(p), 0, jnp.maximum(i * ns - 1, 0)))
    cur_t = pl.BlockSpec((1, HEAD_DIM, R), lambda p, i: (kvh(p), 0, i))
    return pl.pallas_call(
        body, name="swa_fwd", grid=(SWA_HEADS // P, nb // ns),
        in_specs=[pl.BlockSpec((P, R, HEAD_DIM), lambda p, i: (p, i, 0)), prev, cur, prev_t, cur_t,
                  pl.BlockSpec((P, 2 * BLOCK, BLOCK), lambda p, i: (p, 0, 0)),
                  pl.BlockSpec((P, 1, BLOCK), lambda p, i: (p, 0, 0))],
        out_specs=pl.BlockSpec((R, P * HEAD_DIM), lambda p, i: (i, p)),
        out_shape=_sds((S, SWA_HEADS * HEAD_DIM), F32),
        compiler_params=_cp(("parallel", "parallel")),
    )(q, k, k, vT, vT, bias, sink)


def _swa_bwd(q, k, kT, v, bias, sink, do, grads):
    S = q.shape[1]
    nb = S // BLOCK
    ns = _swa_sub(nb)
    R = ns * BLOCK
    P = SWA_PAIR
    nw = len(grads)
    x_ins, x_outs, x_sems = _swap_io(grads)

    def body(q_ref, kp_ref, kc_ref, ktp_ref, ktc_ref, vp_ref, vc_ref, bias_ref, sink_ref, do_ref, *rest):
        dq_ref, dk_ref, dv_ref, dbias_ref, dsink_ref = rest[nw:nw + 5]
        swap = _Swap(rest[:nw], rest[nw + 5:2 * nw + 5], *rest[2 * nw + 5:])
        g = pl.program_id(1)
        i = pl.program_id(2)
        first_step = jnp.logical_and(pl.program_id(0) == 0, jnp.logical_and(g == 0, i == 0))
        last_step = jnp.logical_and(pl.program_id(0) == pl.num_programs(0) - 1,
                                    jnp.logical_and(g == pl.num_programs(1) - 1, i == pl.num_programs(2) - 1))

        @pl.when(first_step)
        def _():
            swap.start()

        @pl.when(jnp.logical_and(g == 0, i == 0))
        def _():
            dk_ref[...] = jnp.zeros_like(dk_ref)
            dv_ref[...] = jnp.zeros_like(dv_ref)

        @pl.when(i == 0)
        def _():
            dbias_ref[...] = jnp.zeros_like(dbias_ref)
            dsink_ref[...] = jnp.zeros_like(dsink_ref)

        subs = range(ns)
        units = [(hh, b) for hh in range(P) for b in subs]
        rows = [slice(b * BLOCK, (b + 1) * BLOCK) for b in subs]
        keys = [_swa_keys(b, kp_ref, kc_ref, i) for b in subs]
        keys_t = [_swa_keys_t(b, ktp_ref, ktc_ref) for b in subs]
        vals = [_swa_keys(b, vp_ref, vc_ref, i) for b in subs]
        douts = {u: do_ref[u[0], rows[u[1]], :] for u in units}
        logits = {u: _swa_logits(q_ref[u[0], rows[u[1]], :], keys[u[1]][0], keys[u[1]][1]) for u in units}
        dws = {u: (_dot_nt(vals[u[1]][0], douts[u]), _dot_nt(vals[u[1]][1], douts[u])) for u in units}
        wts, dls = {}, {}
        for hh in range(P):
            dbp = jnp.zeros((BLOCK, BLOCK), F32)
            dbc = jnp.zeros((BLOCK, BLOCK), F32)
            dsk = jnp.zeros((1, BLOCK), F32)
            for b in subs:
                u = (hh, b)
                wp, wc, ws = _swa_softmax(logits[u][1], logits[u][2], bias_ref[hh], sink_ref[hh][:, :1], keys[b][2])
                dwp, dwc = dws[u]
                delta = _colsum(wp * dwp) + _colsum(wc * dwc)
                dlp = wp * (dwp - delta)
                dlc = wc * (dwc - delta)
                dbp += dlp
                dbc += dlc
                dsk -= ws * delta
                wts[u] = (wp.astype(_MXU), wc.astype(_MXU))
                dls[u] = (dlp.astype(_MXU), dlc.astype(_MXU))
            dbias_ref[hh, :BLOCK, :] += dbp
            dbias_ref[hh, BLOCK:, :] += dbc
            dsink_ref[hh] += jnp.broadcast_to(dsk, (8, BLOCK))
        dqs = {u: (_dot(keys_t[u[1]][0], dls[u][0]) + _dot(keys_t[u[1]][1], dls[u][1])) * SCALE for u in units}
        for b in subs:
            dq_ref[rows[b], :] = jnp.concatenate([dqs[(hh, b)] for hh in range(P)], axis=0).T.astype(dq_ref.dtype)
        for b in subs:
            blk = i * ns + b
            dk_cur = sum(_dot(dls[(hh, b)][1], logits[(hh, b)][0]) for hh in range(P))
            dv_cur = sum(_dot(wts[(hh, b)][1], douts[(hh, b)]) for hh in range(P))
            dk_prev = sum(_dot(dls[(hh, b)][0], logits[(hh, b)][0]) for hh in range(P))
            dv_prev = sum(_dot(wts[(hh, b)][0], douts[(hh, b)]) for hh in range(P))
            dk_ref[0, blk] += dk_cur
            dv_ref[0, blk] += dv_cur
            if b == 0:
                @pl.when(i > 0)
                def _():
                    dk_ref[0, blk - 1] += dk_prev
                    dv_ref[0, blk - 1] += dv_prev
            else:
                dk_ref[0, blk - 1] += dk_prev
                dv_ref[0, blk - 1] += dv_prev

        @pl.when(last_step)
        def _():
            swap.finish()

    G2 = SWA_GROUP // P
    hp = lambda kv, g, i: kv * G2 + g
    prev = pl.BlockSpec((1, BLOCK, HEAD_DIM), lambda kv, g, i: (kv, jnp.maximum(i * ns - 1, 0), 0))
    cur = pl.BlockSpec((1, R, HEAD_DIM), lambda kv, g, i: (kv, i, 0))
    prev_t = pl.BlockSpec((1, HEAD_DIM, BLOCK), lambda kv, g, i: (kv, 0, jnp.maximum(i * ns - 1, 0)))
    cur_t = pl.BlockSpec((1, HEAD_DIM, R), lambda kv, g, i: (kv, 0, i))
    qblk = pl.BlockSpec((P, R, HEAD_DIM), lambda kv, g, i: (hp(kv, g, i), i, 0))
    kvacc = pl.BlockSpec((1, nb, BLOCK, HEAD_DIM), lambda kv, g, i: (kv, 0, 0, 0))
    any_spec = pl.BlockSpec(memory_space=pl.ANY)
    res = pl.pallas_call(
        body, name="swa_bwd", grid=(SWA_KV_HEADS, G2, nb // ns),
        in_specs=[qblk, prev, cur, prev_t, cur_t, prev, cur,
                  pl.BlockSpec((P, 2 * BLOCK, BLOCK), lambda kv, g, i: (hp(kv, g, i), 0, 0)),
                  pl.BlockSpec((P, 1, BLOCK), lambda kv, g, i: (hp(kv, g, i), 0, 0)), qblk] + [any_spec] * nw,
        out_specs=[pl.BlockSpec((R, P * HEAD_DIM), lambda kv, g, i: (i, hp(kv, g, i))), kvacc, kvacc,
                   pl.BlockSpec((P, 2 * BLOCK, BLOCK), lambda kv, g, i: (hp(kv, g, i), 0, 0)),
                   pl.BlockSpec((P, 8, BLOCK), lambda kv, g, i: (hp(kv, g, i), 0, 0))] + [any_spec] * nw,
        out_shape=[_sds((S, SWA_HEADS * HEAD_DIM), _MXU), _sds((SWA_KV_HEADS, nb, BLOCK, HEAD_DIM), F32),
                   _sds((SWA_KV_HEADS, nb, BLOCK, HEAD_DIM), F32), _sds((SWA_HEADS, 2 * BLOCK, BLOCK), F32),
                   _sds((SWA_HEADS, 8, BLOCK), F32)] + x_outs,
        scratch_shapes=x_sems,
        compiler_params=_cp(("arbitrary", "arbitrary", "arbitrary")),
    )(q, k, k, kT, kT, v, v, bias, sink, do, *x_ins)
    return res[0], res[1], res[2], res[3], res[4], list(res[5:])


def _swa_small_grads(dbias, dsink, bucket):
    rows = REL_BUCKETS + 8

    def total(x):
        return jnp.sum(jnp.sum(x, axis=1, keepdims=True), axis=0, keepdims=True)

    def body(db_ref, ds_ref, bk_ref, o_ref):
        bk = bk_ref[...]
        r = lax.broadcasted_iota(jnp.int32, (rows, BLOCK), 0)
        c = lax.broadcasted_iota(jnp.int32, (rows, BLOCK), 1)
        out = jnp.zeros((rows, BLOCK), F32)
        for h in range(SWA_HEADS):
            db = db_ref[h]
            for b in range(REL_BUCKETS):
                s = total(jnp.where(bk == b, db, 0.0))
                out = jnp.where(jnp.logical_and(r == b, c == h), s, out)
            s = jnp.sum(ds_ref[h][0:1, :], axis=1, keepdims=True)
            out = jnp.where(jnp.logical_and(r == REL_BUCKETS, c == h), s, out)
        o_ref[...] = out

    vm = pl.BlockSpec(memory_space=pltpu.VMEM)
    return pl.pallas_call(body, name="swa_small_grads", in_specs=[vm, vm, vm], out_specs=vm,
                          out_shape=_sds((rows, BLOCK), F32))(dbias, dsink, bucket)


def _tile_rows(n):
    for t in (512, 352, 256, 176, 128, 64, 32, 16, 8):
        if n % t == 0:
            return t
    return n


def _cast_rows(x, dtype, name):
    R, C = x.shape
    tr = _tile_rows(R)

    def body(x_ref, o_ref):
        o_ref[...] = x_ref[...].astype(o_ref.dtype)

    return pl.pallas_call(body, name=name, grid=(R // tr,), in_specs=[_rows(tr, C)], out_specs=_rows(tr, C),
                          out_shape=_sds((R, C), dtype), compiler_params=_cp(("parallel",)))(x)


def _pair_sum(g, recv, c, name):
    n, half, C = recv.shape
    tr = _tile_rows(half)

    def body(c_ref, a_ref, b_ref, o_ref):
        o_ref[...] = (a_ref[0] + b_ref[...]).astype(o_ref.dtype)

    return pl.pallas_call(
        body, name=name,
        grid_spec=pltpu.PrefetchScalarGridSpec(
            num_scalar_prefetch=1, grid=(n, half // tr),
            in_specs=[pl.BlockSpec((1, 1, tr, C), lambda j, i, c_ref: (j, c_ref[0], i, 0)),
                      pl.BlockSpec((1, tr, C), lambda j, i, c_ref: (j, i, 0))],
            out_specs=pl.BlockSpec((1, tr, C), lambda j, i, c_ref: (j, i, 0))),
        out_shape=_sds((n, half, C), _MXU),
        compiler_params=_cp(("parallel", "parallel")))(c.reshape(1), g.reshape(n, 2, half, C), recv)


def _chip_sum(own, recv, me, name):
    n, R, C = recv.shape
    tr = _tile_rows(R)

    def body(me_ref, own_ref, recv_ref, o_ref):
        acc = None
        for j in range(n):
            term = jnp.where(me_ref[0] == j, own_ref[0], recv_ref[j]).astype(F32)
            acc = term if acc is None else acc + term
        o_ref[...] = acc

    return pl.pallas_call(
        body, name=name,
        grid_spec=pltpu.PrefetchScalarGridSpec(
            num_scalar_prefetch=1, grid=(R // tr,),
            in_specs=[pl.BlockSpec((1, tr, C), lambda i, me_ref: (me_ref[0], i, 0)),
                      pl.BlockSpec((n, tr, C), lambda i, me_ref: (0, i, 0))],
            out_specs=pl.BlockSpec((tr, C), lambda i, me_ref: (i, 0))),
        out_shape=_sds((R, C), F32), compiler_params=_cp(("parallel",)))(me.reshape(1), own, recv)


def _adamw_math(w, g, m, v):
    m = ADAM_B1 * m + (1.0 - ADAM_B1) * g
    v = ADAM_B2 * v + (1.0 - ADAM_B2) * (g * g)
    m_hat = m / (1.0 - ADAM_B1 ** ADAM_STEP)
    v_hat = v / (1.0 - ADAM_B2 ** ADAM_STEP)
    delta = -ADAM_LR * (m_hat / (jnp.sqrt(v_hat) + ADAM_EPS) + ADAM_WD * w)
    return delta, m, v


def _adamw(w, g, m, v, name):
    R, C = w.shape
    tr = _tile_rows(R)

    def body(w_ref, g_ref, m_ref, v_ref, d_ref, nm_ref, nv_ref):
        d, nm, nv = _adamw_math(w_ref[...], g_ref[...], m_ref[...], v_ref[...])
        d_ref[...] = d
        nm_ref[...] = nm
        nv_ref[...] = nv

    blk = _rows(tr, C)
    return pl.pallas_call(body, name=name, grid=(R // tr,), in_specs=[blk] * 4, out_specs=[blk] * 3,
                          out_shape=[_sds((R, C), F32)] * 3, compiler_params=_cp(("parallel",)))(w, g, m, v)


def _gather_weights(shards):
    nw = len(shards)
    ins, outs, sems = _gather_io(shards)

    def body(*refs):
        ex = _Gather(refs[:nw], refs[nw:2 * nw], *refs[2 * nw:])
        ex.start()
        ex.finish()

    any_spec = pl.BlockSpec(memory_space=pl.ANY)
    got = pl.pallas_call(body, name="gather_weights", in_specs=[any_spec] * nw, out_specs=[any_spec] * nw,
                         out_shape=outs, scratch_shapes=sems)(*ins)
    return _gather_assemble(got, shards)


def _swap_halves(grads, name):
    nw = len(grads)
    ins, outs, sems = _swap_io(grads)

    def body(*refs):
        ex = _Swap(refs[:nw], refs[nw:2 * nw], *refs[2 * nw:])
        ex.start()
        ex.finish()

    any_spec = pl.BlockSpec(memory_space=pl.ANY)
    return pl.pallas_call(body, name=name, in_specs=[any_spec] * nw, out_specs=[any_spec] * nw,
                          out_shape=outs, scratch_shapes=sems)(*ins)


def _scatter_partials(parts):
    nw = len(parts)
    ins, outs, sems = _scatter_io(parts)

    def body(*refs):
        ex = _Scatter(refs[:nw], refs[nw:2 * nw], *refs[2 * nw:])
        ex.start()
        ex.finish()

    any_spec = pl.BlockSpec(memory_space=pl.ANY)
    return pl.pallas_call(body, name="scatter_partials", in_specs=[any_spec] * nw, out_specs=[any_spec] * nw,
                          out_shape=outs, scratch_shapes=sems)(*ins)


def _join_halves(sums):
    nw = len(sums)

    def body(*refs):
        f_refs, out_refs = refs[:nw], refs[nw:2 * nw]
        send_sems, recv_sems = refs[2 * nw:]
        x, y, c, _ = _place()
        ws = range(nw)

        def copy(w, half_index):
            return pltpu.make_async_remote_copy(
                src_ref=f_refs[w], dst_ref=out_refs[w].at[half_index], send_sem=send_sems.at[w],
                recv_sem=recv_sems.at[w], device_id=(x, y, 1 - c), device_id_type=MESH)

        sends = [copy(w, c) for w in ws]
        for cp in sends:
            cp.start()
        for w in ws:
            copy(w, 1 - c).wait_recv()
        for cp in sends:
            cp.wait_send()

    any_spec = pl.BlockSpec(memory_space=pl.ANY)
    outs = pl.pallas_call(
        body, name="join_halves", in_specs=[any_spec] * nw, out_specs=[any_spec] * nw,
        out_shape=[_sds((2,) + f.shape, f.dtype) for f in sums],
        scratch_shapes=[pltpu.SemaphoreType.DMA((nw,)), pltpu.SemaphoreType.DMA((nw,))],
    )(*sums)
    c = lax.axis_index("c")
    return [lax.dynamic_update_slice_in_dim(o, f[None], c, axis=0).reshape(2 * f.shape[0], f.shape[1])
            for o, f in zip(outs, sums)]


def _allreduce_small(block):
    m_per, n = block.shape

    def body(x_ref, sum_ref, loss_ref, all_ref, send_sems, recv_sems, local_sem):
        x, y, c, chips = _place()
        me, sibling = (x, y, c), (x, y, 1 - c)

        def rows(px, py, pc):
            return all_ref.at[pl.ds(pl.multiple_of((4 * px + 2 * py + pc) * m_per, 8), m_per), :]

        def copy(k, blk, to, src=None):
            return pltpu.make_async_remote_copy(
                src_ref=rows(*blk) if src is None else src, dst_ref=rows(*blk), send_sem=send_sems.at[k],
                recv_sem=recv_sems.at[k], device_id=to, device_id_type=MESH)

        mine = pltpu.make_async_copy(x_ref, rows(*me), local_sem)
        mine.start()
        first = [copy(0, me, sibling, src=x_ref)]
        first += [copy(1 + j, me, (*chip, c), src=x_ref) for j, chip in enumerate(chips)]
        for cp in first:
            cp.start()
        passed = [copy(4 + j, (*chip, c), sibling) for j, chip in enumerate(chips)]
        for j, chip in enumerate(chips):
            copy(1 + j, (*chip, c), me).wait_recv()
            passed[j].start()
        copy(0, sibling, me).wait_recv()
        for j, chip in enumerate(chips):
            copy(4 + j, (*chip, 1 - c), me).wait_recv()
        for cp in first + passed:
            cp.wait_send()
        mine.wait()

        acc = all_ref[0:m_per, :]
        for d in range(1, 8):
            acc = acc + all_ref[d * m_per:(d + 1) * m_per, :]
        sum_ref[...] = acc
        tot = jnp.sum(acc[8:9, :], axis=1, keepdims=True) * (0.5 / D_MODEL)
        loss_ref[...] = jnp.broadcast_to(tot, loss_ref.shape)

    vm = pl.BlockSpec(memory_space=pltpu.VMEM)
    return pl.pallas_call(
        body, name="allreduce_small", in_specs=[vm], out_specs=[vm, vm],
        out_shape=[_sds((m_per, n), F32), _sds((8, 128), F32)],
        scratch_shapes=[pltpu.VMEM((8 * m_per, n), F32), pltpu.SemaphoreType.DMA((7,)), pltpu.SemaphoreType.DMA((7,)),
                        pltpu.SemaphoreType.DMA],
    )(block)


def _heads_rows(x, nh):
    S = x.shape[0]
    return x.reshape(S, nh, HEAD_DIM).transpose(1, 0, 2)


def _heads_cols(x, nh):
    S = x.shape[0]
    return x.reshape(S, nh, HEAD_DIM).transpose(1, 2, 0)


def _key_blocks(x, nh, t):
    S = x.shape[0]
    return x.reshape(S // t, t, nh, HEAD_DIM).transpose(2, 0, 1, 3)


def _key_blocks_t(x, nh, t):
    S = x.shape[0]
    return x.reshape(S // t, t, nh, HEAD_DIM).transpose(2, 0, 3, 1)


def _pad_row(v):
    v = v.reshape(1, -1)
    return jnp.pad(v, ((0, 0), (0, D_MODEL - v.shape[1])))


def _pack_small(ln_in_g, ln_in_b, sb_g, swa_g, sinks, rel_bias, ln1_g, ln1_b, ln2_g, ln2_b, extra):
    rows = [_pad_row(ln_in_g), _pad_row(ln_in_b), jnp.concatenate([sb_g.reshape(1, -1), swa_g.reshape(1, -1)], axis=1),
            _pad_row(jnp.concatenate([rel_bias.reshape(1, -1), sinks.reshape(1, -1)], axis=1)),
            _pad_row(ln1_g), _pad_row(ln1_b), _pad_row(ln2_g), _pad_row(ln2_b), _pad_row(extra)]
    rows.append(jnp.zeros((SMALL_ROWS - len(rows), D_MODEL), F32))
    return jnp.concatenate(rows, axis=0)


def _unpack_small(blk):
    nrb = REL_BUCKETS * SWA_HEADS
    return (blk[0], blk[1], blk[2:3, :SB_WIDTH], blk[2:3, SB_WIDTH:], blk[3:4, nrb:nrb + SWA_HEADS],
            blk[3, :nrb].reshape(REL_BUCKETS, SWA_HEADS), blk[4:5], blk[5:6], blk[6:7], blk[7:8])


def kernel(x, ln_in_g, ln_in_b, w_in, sb_norm_g, swa_norm_g, sinks, rel_bias, w_out, ln1_g, ln1_b, w_gate_up, w_down, ln2_g, ln2_b, loss_target, m_ln_in_g, m_ln_in_b, m_w_in, m_sb_norm_g, m_swa_norm_g, m_sinks, m_rel_bias, m_w_out, m_ln1_g, m_ln1_b, m_w_gate_up, m_w_down, m_ln2_g, m_ln2_b, v_ln_in_g, v_ln_in_b, v_w_in, v_sb_norm_g, v_swa_norm_g, v_sinks, v_rel_bias, v_w_out, v_ln1_g, v_ln1_b, v_w_gate_up, v_w_down, v_ln2_g, v_ln2_b):
    S = x.shape[1]
    x2 = x.reshape(S, D_MODEL)
    tgt = loss_target.reshape(S, D_MODEL)
    T = min(S, SB_TILE)
    bucket = jnp.asarray(_bucket_table().T)
    row = lambda v: v.reshape(1, -1)

    shards = [_cast_rows(w[0], _MXU, "cast_" + n) for n, w in (("w_in", w_in), ("w_out", w_out), ("w_gate_up", w_gate_up), ("w_down", w_down))]
    (w_in_sh,) = _gather_weights(shards[:1])
    w_in_f = jnp.concatenate([w_in_sh[j] for j in range(N_CHIPS)], axis=1)

    h0, h0b, q_sb, k_sb, v_sb, q_sw, kv_sw = _ln_in_proj(x2, row(ln_in_g), row(ln_in_b), w_in_f)
    k_sw, v_sw = kv_sw[:, :SWA_KV_WIDTH], kv_sw[:, SWA_KV_WIDTH:]
    qT_sb = _heads_cols(q_sb, SB_HEADS)
    kb_sb = _key_blocks(k_sb, SB_HEADS, T)
    sb_out, rsave, (w_out_sh, w_gu_sh, w_down_sh) = _sb_fwd(qT_sb, kb_sb, _key_blocks_t(v_sb, SB_HEADS, T), shards[1:])
    w_out_f = w_out_sh.reshape(D_MODEL, D_MODEL)
    w_down_f = w_down_sh.reshape(D_FF, D_MODEL)

    bias = _swa_bias(rel_bias, bucket)
    sink_rows = jnp.broadcast_to(sinks.reshape(SWA_HEADS, 1, 1), (SWA_HEADS, 1, BLOCK))
    qh_sw, kh_sw, vh_sw = _heads_rows(q_sw, SWA_HEADS), _heads_rows(k_sw, SWA_KV_HEADS), _heads_rows(v_sw, SWA_KV_HEADS)
    swa_out = _swa_fwd(qh_sw, kh_sw, _heads_cols(v_sw, SWA_KV_HEADS), bias, sink_rows)

    pre1, merged, h1b = _mix_out(sb_out, swa_out, sb_norm_g, swa_norm_g, w_out_f, h0, ln1_g, ln1_b)
    act, silu, dsilu_up = _ffn_up(h1b, w_gu_sh)
    dp2, dp2b, dg2, db2, errsum = _ffn_down_loss(act, w_down_f, pre1, ln1_g, ln1_b, ln2_g, ln2_b, tgt)

    g_w_down = _matmul_tn(act, dp2b, "grad_w_down", FF_CHUNK, D_MODEL)
    dgate, dup = _ffn_down_bwd(dp2b, w_down_f, silu, dsilu_up)
    g_w_gu = _matmul_tn_pair(h1b, dgate, dup, "grad_w_gate_up")
    dp1, dp1b, dg1, db1 = _ffn_up_bwd(dgate, dup, w_gu_sh, dp2, pre1, ln1_g)
    g_w_out = _matmul_tn(merged, dp1b, "grad_w_out", D_MODEL, D_MODEL)
    dsb, dsw, dgsb, dgsw = _mix_bwd(dp1b, w_out_f, sb_out, swa_out, sb_norm_g, swa_norm_g)

    c = lax.axis_index("c").astype(jnp.int32)
    me = (2 * lax.axis_index("x") + lax.axis_index("y")).astype(jnp.int32)
    grads_a = [g_w_out.reshape(N_CHIPS, D_MODEL // N_CHIPS, D_MODEL), g_w_gu, g_w_down.reshape(N_CHIPS, D_FF // N_CHIPS, D_MODEL)]
    names_a = ("w_out", "w_gate_up", "w_down")
    dq_sw, dkh_sw, dvh_sw, dbias, dsink, swapped_a = _swa_bwd(qh_sw, kh_sw, _heads_cols(k_sw, SWA_KV_HEADS), vh_sw, bias,
                                                               sink_rows, _heads_rows(dsw, SWA_HEADS), grads_a)
    swa_small = _swa_small_grads(dbias, dsink, bucket)
    partials_a = [_pair_sum(g, r, c, "pair_sum_" + n) for g, r, n in zip(grads_a, swapped_a, names_a)]
    dq_sb, dk_sb, dv_sb, recv_a = _sb_bwd(qT_sb, kb_sb, _key_blocks_t(k_sb, SB_HEADS, T), _key_blocks(v_sb, SB_HEADS, T),
                                             _heads_cols(dsb, SB_HEADS), rsave, partials_a)
    tok = lambda t, nh: t.reshape(nh, S, HEAD_DIM).transpose(1, 0, 2).reshape(S, nh * HEAD_DIM)
    dproj = [dq_sb, dk_sb, dv_sb, dq_sw,
             jnp.concatenate([tok(dkh_sw, SWA_KV_HEADS), tok(dvh_sw, SWA_KV_HEADS)], axis=1).astype(_MXU)]
    g_w_in = jnp.concatenate([_matmul_tn(h0b, d, "grad_w_in_%d" % k, D_MODEL, d.shape[1]) for k, d in enumerate(dproj)],
                             axis=1)

    cin = IN_COLS // N_CHIPS
    grads_b = [jnp.stack([g_w_in[:, j * cin:(j + 1) * cin] for j in range(N_CHIPS)])]
    partials_b = [_pair_sum(grads_b[0], _swap_halves(grads_b, "swap_halves_in")[0], c, "pair_sum_w_in")]
    grad_x, dg_in, db_in, recv_b = _in_proj_bwd(dproj, w_in_f, dp1, x2, row(ln_in_g), partials_b)
    names = ("w_in",) + names_a
    sums = [_chip_sum(p, r, me, "chip_sum_" + n) for p, r, n in zip(partials_b + partials_a, list(recv_b) + list(recv_a), names)]
    gs_in, gs_out, gs_gu, gs_down = _join_halves(sums)

    nrb = REL_BUCKETS * SWA_HEADS
    small = _pack_small(dg_in, db_in, dgsb, dgsw, swa_small[REL_BUCKETS, :SWA_HEADS],
                        swa_small[:REL_BUCKETS, :SWA_HEADS], dg1, db1, dg2, db2, errsum)
    g_small, loss_tile = _allreduce_small(small)
    loss = loss_tile[0, 0]

    big = []
    for name, w, g, m, v in (("adamw_w_in", w_in, gs_in, m_w_in, v_w_in), ("adamw_w_out", w_out, gs_out, m_w_out, v_w_out),
                             ("adamw_w_gate_up", w_gate_up, gs_gu, m_w_gate_up, v_w_gate_up),
                             ("adamw_w_down", w_down, gs_down, m_w_down, v_w_down)):
        d, nm, nv = _adamw(w[0], g, m[0], v[0], name)
        big.append((g[None], d[None], nm[None], nv[None]))
    zero = jnp.zeros((1,), F32)
    w_small = _pack_small(ln_in_g, ln_in_b, sb_norm_g, swa_norm_g, sinks, rel_bias, ln1_g, ln1_b, ln2_g, ln2_b, zero)
    m_small = _pack_small(m_ln_in_g, m_ln_in_b, m_sb_norm_g, m_swa_norm_g, m_sinks, m_rel_bias, m_ln1_g, m_ln1_b,
                          m_ln2_g, m_ln2_b, zero)
    v_small = _pack_small(v_ln_in_g, v_ln_in_b, v_sb_norm_g, v_swa_norm_g, v_sinks, v_rel_bias, v_ln1_g, v_ln1_b,
                          v_ln2_g, v_ln2_b, zero)
    small_out = [_unpack_small(t) for t in (g_small,) + tuple(_adamw(w_small, g_small, m_small, v_small, "adamw_small"))]

    def kind(k):
        s = small_out[k]
        return [s[0], s[1], big[0][k], s[2], s[3], s[4], s[5], big[1][k], s[6], s[7], big[2][k], big[3][k], s[8], s[9]]

    return (loss, grad_x.reshape(1, S, D_MODEL), *kind(0), *kind(1), *kind(2), *kind(3))
```

```python
import functools
import math

import numpy as np
import jax
import jax.numpy as jnp
from jax import lax
from jax.experimental import pallas as pl
from jax.experimental.pallas import tpu as pltpu

F32 = jnp.float32
_MXU = jnp.bfloat16

D_MODEL = 1024
HEAD_DIM = 64
SB_HEADS = 8
SWA_HEADS = 8
SWA_KV_HEADS = 2
SWA_GROUP = SWA_HEADS // SWA_KV_HEADS
SB_WIDTH = SB_HEADS * HEAD_DIM
SWA_WIDTH = SWA_HEADS * HEAD_DIM
SWA_KV_WIDTH = SWA_KV_HEADS * HEAD_DIM
IN_COLS = 3 * SB_WIDTH + SWA_WIDTH + 2 * SWA_KV_WIDTH
BLOCK = 128
REL_BUCKETS = 32
REL_MAX_DIST = 128
D_FF = 2816
FF_CHUNK = D_FF // 2
ALPHA = 2.0 ** 0.25
LN_EPS = 1e-5
RMS_EPS = 1e-6
SCALE = HEAD_DIM ** -0.5
SB_TILE = 256
SB_GROUP_FWD = 8
SB_GROUP_BWD = 4
SB_DEAD = -105.0
SWA_SUB = 8

ADAM_LR = 0.001
ADAM_B1 = 0.9
ADAM_B2 = 0.999
ADAM_EPS = 1e-08
ADAM_WD = 0.01
ADAM_STEP = 10

N_CHIPS = 4
SMALL_ROWS = 16

MESH = pl.DeviceIdType.MESH


def _sds(shape, dtype):
    return jax.ShapeDtypeStruct(shape, dtype)


def _cp(sem=None, vmem_mb=48):
    kw = dict(vmem_limit_bytes=vmem_mb * 1024 * 1024)
    if sem is not None:
        kw["dimension_semantics"] = sem
    return pltpu.CompilerParams(**kw)


def _dot(a, b):
    return jnp.dot(a, b, preferred_element_type=F32)


def _dot_nt(a, b):
    return lax.dot_general(a, b, (((1,), (1,)), ((), ())), preferred_element_type=F32)


def _dot_tn(a, b):
    return lax.dot_general(a, b, (((0,), (0,)), ((), ())), preferred_element_type=F32)


def _ln_hat(x):
    mu = jnp.mean(x, axis=-1, keepdims=True)
    xc = x - mu
    var = jnp.mean(xc * xc, axis=-1, keepdims=True)
    rstd = lax.rsqrt(var + LN_EPS)
    return xc * rstd, rstd


def _ln_bwd(xhat, rstd, dy, g):
    dxh = dy * g
    m1 = jnp.mean(dxh, axis=-1, keepdims=True)
    m2 = jnp.mean(dxh * xhat, axis=-1, keepdims=True)
    return rstd * (dxh - m1 - xhat * m2)


def _colsum(x):
    return jnp.sum(x, axis=0, keepdims=True)


def _split2(x):
    hi = x.astype(_MXU)
    lo = (x - hi.astype(F32)).astype(_MXU)
    return hi, lo


def _rows(tm, n):
    return pl.BlockSpec((tm, n), lambda i: (i, 0))


def _fixed(*shape):
    nd = len(shape)
    return pl.BlockSpec(shape, lambda i: (0,) * nd)


IN_SECTIONS = (SB_WIDTH, SB_WIDTH, SB_WIDTH, SWA_WIDTH, 2 * SWA_KV_WIDTH)


def _ln_in_proj(x, g, b, w):
    S = x.shape[0]
    tm = min(S, SB_TILE)
    offs = np.cumsum((0,) + IN_SECTIONS)
    ns = len(IN_SECTIONS)

    def body(x_ref, g_ref, b_ref, w_ref, h_ref, hb_ref, *o_refs):
        p_refs, (qT_ref, kT_ref, vT_ref) = o_refs[:ns], o_refs[ns:]
        xhat, _ = _ln_hat(x_ref[...])
        h = xhat * g_ref[...] + b_ref[...]
        h_ref[...] = h
        hb = h.astype(_MXU)
        hb_ref[...] = hb
        proj = _dot(hb, w_ref[...])
        for k, p_ref in enumerate(p_refs):
            p_ref[...] = proj[:, offs[k]:offs[k + 1]].astype(p_ref.dtype)
        heads = lambda k: proj[:, offs[k]:offs[k + 1]].T.astype(_MXU).reshape(SB_HEADS, HEAD_DIM, tm)
        qT_ref[...] = heads(0)
        kT_ref[:, 0] = heads(1)
        vT_ref[:, 0] = heads(2)

    blocked = pl.BlockSpec((SB_HEADS, 1, HEAD_DIM, tm), lambda i: (0, i, 0, 0))
    return pl.pallas_call(
        body, name="ln_in_proj", grid=(S // tm,),
        in_specs=[_rows(tm, D_MODEL), _fixed(1, D_MODEL), _fixed(1, D_MODEL), _fixed(D_MODEL, IN_COLS)],
        out_specs=[_rows(tm, D_MODEL), _rows(tm, D_MODEL)] + [_rows(tm, n) for n in IN_SECTIONS]
                  + [pl.BlockSpec((SB_HEADS, HEAD_DIM, tm), lambda i: (0, 0, i)), blocked, blocked],
        out_shape=[_sds((S, D_MODEL), F32), _sds((S, D_MODEL), _MXU)] + [_sds((S, n), _MXU) for n in IN_SECTIONS]
                  + [_sds((SB_HEADS, HEAD_DIM, S), _MXU), _sds((SB_HEADS, S // tm, HEAD_DIM, tm), _MXU),
                     _sds((SB_HEADS, S // tm, HEAD_DIM, tm), _MXU)],
        compiler_params=_cp(("parallel",)),
    )(x, g, b, w)


def _rms(x, g):
    r = lax.rsqrt(jnp.mean(x * x, axis=-1, keepdims=True) + RMS_EPS)
    return x * r * g, r


def _mix_out(sb, sw, gsb, gsw, w_out, h0, g1, b1):
    S = sb.shape[0]
    tm = min(S, 512)

    def body(sb_ref, sw_ref, gsb_ref, gsw_ref, w_ref, h0_ref, g1_ref, b1_ref, pre_ref, mg_ref, h1_ref):
        ysb, _ = _rms(sb_ref[...], gsb_ref[...])
        ysw, _ = _rms(sw_ref[...], gsw_ref[...])
        ysb = ysb.astype(_MXU)
        ysw = ysw.astype(_MXU)
        mg_ref[:, :SB_WIDTH] = ysb
        mg_ref[:, SB_WIDTH:] = ysw
        mix = _dot(ysb, w_ref[:SB_WIDTH, :]) + _dot(ysw, w_ref[SB_WIDTH:, :])
        pre1 = ALPHA * h0_ref[...] + mix
        pre_ref[...] = pre1
        xhat, _ = _ln_hat(pre1)
        h1_ref[...] = (xhat * g1_ref[...] + b1_ref[...]).astype(h1_ref.dtype)

    vec = _fixed(1, D_MODEL)
    return pl.pallas_call(
        body, name="mix_out", grid=(S // tm,),
        in_specs=[_rows(tm, SB_WIDTH), _rows(tm, SWA_WIDTH), _fixed(1, SB_WIDTH), _fixed(1, SWA_WIDTH),
                  _fixed(D_MODEL, D_MODEL), _rows(tm, D_MODEL), vec, vec],
        out_specs=[_rows(tm, D_MODEL), _rows(tm, D_MODEL), _rows(tm, D_MODEL)],
        out_shape=[_sds((S, D_MODEL), F32), _sds((S, D_MODEL), _MXU), _sds((S, D_MODEL), _MXU)],
        compiler_params=_cp(("parallel",)),
    )(sb, sw, gsb, gsw, w_out, h0, g1, b1)


def _sigmoid(x):
    return 1.0 / (1.0 + jnp.exp(-x))


def _ffn_up(h1b, wgu):
    S = h1b.shape[0]
    tm = min(S, 1024)

    def body(h_ref, wg_ref, wu_ref, a_ref, s1_ref, s2_ref):
        h1 = h_ref[...]
        gate = _dot(h1, wg_ref[0])
        up = _dot(h1, wu_ref[0])
        sg = _sigmoid(gate)
        silu = gate * sg
        a_ref[...] = (silu * up).astype(a_ref.dtype)
        s1_ref[...] = silu.astype(s1_ref.dtype)
        s2_ref[...] = (up * (sg * (1.0 + gate * (1.0 - sg)))).astype(s2_ref.dtype)

    chunk = pl.BlockSpec((tm, FF_CHUNK), lambda j, i: (i, j))
    return pl.pallas_call(
        body, name="ffn_up", grid=(2, S // tm),
        in_specs=[pl.BlockSpec((tm, D_MODEL), lambda j, i: (i, 0)),
                  pl.BlockSpec((1, D_MODEL, FF_CHUNK), lambda j, i: (j, 0, 0)),
                  pl.BlockSpec((1, D_MODEL, FF_CHUNK), lambda j, i: (j + 2, 0, 0))],
        out_specs=[chunk, chunk, chunk],
        out_shape=[_sds((S, D_FF), _MXU)] * 3,
        compiler_params=_cp(("arbitrary", "arbitrary"), vmem_mb=56),
    )(h1b, wgu, wgu)


def _ffn_down_loss(a, w_down, pre1, g1, b1, g2, b2, tgt):
    S = a.shape[0]
    tm = min(S, 512)

    def body(a_ref, w_ref, p_ref, g1_ref, b1_ref, g2_ref, b2_ref, t_ref, d_ref, db_ref, dg2_ref, db2_ref, err_ref):
        @pl.when(pl.program_id(0) == 0)
        def _():
            dg2_ref[...] = jnp.zeros_like(dg2_ref)
            db2_ref[...] = jnp.zeros_like(db2_ref)
            err_ref[...] = jnp.zeros_like(err_ref)

        xhat1, _ = _ln_hat(p_ref[...])
        h1 = xhat1 * g1_ref[...] + b1_ref[...]
        pre2 = ALPHA * h1 + _dot(a_ref[...], w_ref[...])
        xhat2, rstd2 = _ln_hat(pre2)
        err = xhat2 * g2_ref[...] + b2_ref[...] - t_ref[...]
        dh2 = err * (1.0 / D_MODEL)
        dp2 = _ln_bwd(xhat2, rstd2, dh2, g2_ref[...])
        d_ref[...] = dp2
        db_ref[...] = dp2.astype(db_ref.dtype)
        dg2_ref[...] += _colsum(dh2 * xhat2)
        db2_ref[...] += _colsum(dh2)
        err_ref[...] += _colsum(err * err)

    vec = _fixed(1, D_MODEL)
    return pl.pallas_call(
        body, name="ffn_down_loss", grid=(S // tm,),
        in_specs=[_rows(tm, D_FF), _fixed(D_FF, D_MODEL), _rows(tm, D_MODEL), vec, vec, vec, vec, _rows(tm, D_MODEL)],
        out_specs=[_rows(tm, D_MODEL), _rows(tm, D_MODEL), vec, vec, vec],
        out_shape=[_sds((S, D_MODEL), F32), _sds((S, D_MODEL), _MXU), _sds((1, D_MODEL), F32), _sds((1, D_MODEL), F32),
                   _sds((1, D_MODEL), F32)],
        compiler_params=_cp(("arbitrary",)),
    )(a, w_down, pre1, g1, b1, g2, b2, tgt)


def _ffn_down_bwd(dp2b, w_down, s1, s2):
    S = dp2b.shape[0]
    tm = min(S, 512)

    def body(d_ref, w_ref, s1_ref, s2_ref, dg_ref, du_ref):
        da = _dot_nt(d_ref[...], w_ref[...])
        du_ref[...] = (da * s1_ref[...].astype(F32)).astype(du_ref.dtype)
        dg_ref[...] = (da * s2_ref[...].astype(F32)).astype(dg_ref.dtype)

    chunk = pl.BlockSpec((tm, FF_CHUNK), lambda j, i: (i, j))
    return pl.pallas_call(
        body, name="ffn_down_bwd", grid=(2, S // tm),
        in_specs=[pl.BlockSpec((tm, D_MODEL), lambda j, i: (i, 0)),
                  pl.BlockSpec((FF_CHUNK, D_MODEL), lambda j, i: (j, 0)), chunk, chunk],
        out_specs=[chunk, chunk],
        out_shape=[_sds((S, D_FF), _MXU), _sds((S, D_FF), _MXU)],
        compiler_params=_cp(("arbitrary", "arbitrary")),
    )(dp2b, w_down, s1, s2)


def _ffn_up_bwd(dgate, dup, wgu, dp2, pre1, g1):
    S = dgate.shape[0]
    tm = min(S, 256)

    def body(dg_ref, du_ref, w_ref, d2_ref, p_ref, g_ref, d1_ref, d1b_ref, dg1_ref, db1_ref):
        @pl.when(pl.program_id(0) == 0)
        def _():
            dg1_ref[...] = jnp.zeros_like(dg1_ref)
            db1_ref[...] = jnp.zeros_like(db1_ref)

        dh1 = ALPHA * d2_ref[...]
        for j in range(2):
            cols = slice(j * FF_CHUNK, (j + 1) * FF_CHUNK)
            dh1 += _dot_nt(dg_ref[:, cols], w_ref[j])
            dh1 += _dot_nt(du_ref[:, cols], w_ref[j + 2])
        xhat, rstd = _ln_hat(p_ref[...])
        dp1 = _ln_bwd(xhat, rstd, dh1, g_ref[...])
        d1_ref[...] = dp1
        d1b_ref[...] = dp1.astype(d1b_ref.dtype)
        dg1_ref[...] += _colsum(dh1 * xhat)
        db1_ref[...] += _colsum(dh1)

    vec = _fixed(1, D_MODEL)
    return pl.pallas_call(
        body, name="ffn_up_bwd", grid=(S // tm,),
        in_specs=[_rows(tm, D_FF), _rows(tm, D_FF), _fixed(4, D_MODEL, FF_CHUNK), _rows(tm, D_MODEL),
                  _rows(tm, D_MODEL), vec],
        out_specs=[_rows(tm, D_MODEL), _rows(tm, D_MODEL), vec, vec],
        out_shape=[_sds((S, D_MODEL), F32), _sds((S, D_MODEL), _MXU), _sds((1, D_MODEL), F32), _sds((1, D_MODEL), F32)],
        compiler_params=_cp(("arbitrary",), vmem_mb=56),
    )(dgate, dup, wgu, dp2, pre1, g1)


def _rms_bwd(x, g, dy):
    n = x.shape[-1]
    r = lax.rsqrt(jnp.mean(x * x, axis=-1, keepdims=True) + RMS_EPS)
    u = dy * g
    dx = r * u - x * (r * r * r) * (jnp.sum(u * x, axis=-1, keepdims=True) * (1.0 / n))
    return dx, _colsum(dy * x * r)


def _mix_bwd(dp1b, w_out, sb, sw, gsb, gsw):
    S = sb.shape[0]
    tm = min(S, 512)

    def body(d_ref, w_ref, sb_ref, sw_ref, gsb_ref, gsw_ref, dsb_ref, dsw_ref, dgsb_ref, dgsw_ref):
        @pl.when(pl.program_id(0) == 0)
        def _():
            dgsb_ref[...] = jnp.zeros_like(dgsb_ref)
            dgsw_ref[...] = jnp.zeros_like(dgsw_ref)

        dm = _dot_nt(d_ref[...], w_ref[...])
        dsb, dgsb = _rms_bwd(sb_ref[...], gsb_ref[...], dm[:, :SB_WIDTH])
        dsw, dgsw = _rms_bwd(sw_ref[...], gsw_ref[...], dm[:, SB_WIDTH:])
        dsb_ref[...] = dsb.T.astype(dsb_ref.dtype).reshape(dsb_ref.shape)
        dsw_ref[...] = dsw.astype(dsw_ref.dtype)
        dgsb_ref[...] += dgsb
        dgsw_ref[...] += dgsw

    return pl.pallas_call(
        body, name="mix_bwd", grid=(S // tm,),
        in_specs=[_rows(tm, D_MODEL), _fixed(D_MODEL, D_MODEL), _rows(tm, SB_WIDTH), _rows(tm, SWA_WIDTH),
                  _fixed(1, SB_WIDTH), _fixed(1, SWA_WIDTH)],
        out_specs=[pl.BlockSpec((SB_HEADS, HEAD_DIM, tm), lambda i: (0, 0, i)), _rows(tm, SWA_WIDTH),
                   _fixed(1, SB_WIDTH), _fixed(1, SWA_WIDTH)],
        out_shape=[_sds((SB_HEADS, HEAD_DIM, S), _MXU), _sds((S, SWA_WIDTH), _MXU), _sds((1, SB_WIDTH), F32),
                   _sds((1, SWA_WIDTH), F32)],
        compiler_params=_cp(("arbitrary",)),
    )(dp1b, w_out, sb, sw, gsb, gsw)


def _in_proj_bwd(dproj, w_in, dp1, x, g, parts):
    S = x.shape[0]
    tm = min(S, 512)
    nw = len(parts)
    ns = len(IN_SECTIONS)
    offs = np.cumsum((0,) + IN_SECTIONS)
    s_ins, s_outs, s_sems = _scatter_io(parts)

    def body(*refs):
        dpj_refs = refs[:ns]
        w_ref, d1_ref, x_ref, g_ref = refs[ns:ns + 4]
        rest = refs[ns + 4:]
        gx_ref, dg_ref, db_ref = rest[nw:nw + 3]
        scatter = _Scatter(rest[:nw], rest[nw + 3:2 * nw + 3], *rest[2 * nw + 3:])

        @pl.when(pl.program_id(0) == 0)
        def _():
            scatter.start()
            dg_ref[...] = jnp.zeros_like(dg_ref)
            db_ref[...] = jnp.zeros_like(db_ref)

        dh0 = ALPHA * d1_ref[...]
        for k in range(ns):
            dh0 += _dot_nt(dpj_refs[k][...], w_ref[:, offs[k]:offs[k + 1]])
        xhat, rstd = _ln_hat(x_ref[...])
        gx_ref[...] = _ln_bwd(xhat, rstd, dh0, g_ref[...])
        dg_ref[...] += _colsum(dh0 * xhat)
        db_ref[...] += _colsum(dh0)

        @pl.when(pl.program_id(0) == pl.num_programs(0) - 1)
        def _():
            scatter.finish()

    vec = _fixed(1, D_MODEL)
    any_spec = pl.BlockSpec(memory_space=pl.ANY)
    res = pl.pallas_call(
        body, name="in_proj_bwd", grid=(S // tm,),
        in_specs=[_rows(tm, n) for n in IN_SECTIONS]
                 + [_fixed(D_MODEL, IN_COLS), _rows(tm, D_MODEL), _rows(tm, D_MODEL), vec] + [any_spec] * nw,
        out_specs=[_rows(tm, D_MODEL), vec, vec] + [any_spec] * nw,
        out_shape=[_sds((S, D_MODEL), F32), _sds((1, D_MODEL), F32), _sds((1, D_MODEL), F32)] + s_outs,
        scratch_shapes=s_sems,
        compiler_params=_cp(("arbitrary",)),
    )(*dproj, w_in, dp1, x, g, *s_ins)
    return res[0], res[1], res[2], list(res[3:])


def _matmul_tn(a, b, name, tk, tn):
    T, K = a.shape
    N = b.shape[1]
    tt = min(T, 1024)

    def body(a_ref, b_ref, o_ref):
        @pl.when(pl.program_id(2) == 0)
        def _():
            o_ref[...] = jnp.zeros_like(o_ref)

        o_ref[...] += _dot_tn(a_ref[...], b_ref[...])

    return pl.pallas_call(
        body, name=name, grid=(K // tk, N // tn, T // tt),
        in_specs=[pl.BlockSpec((tt, tk), lambda k, n, t: (t, k)), pl.BlockSpec((tt, tn), lambda k, n, t: (t, n))],
        out_specs=pl.BlockSpec((tk, tn), lambda k, n, t: (k, n)),
        out_shape=_sds((K, N), F32),
        compiler_params=_cp(("parallel", "parallel", "arbitrary")),
    )(a, b)


def _place():
    x, y, c = lax.axis_index("x"), lax.axis_index("y"), lax.axis_index("c")
    chips = [(1 - x, y), (x, 1 - y), (1 - x, 1 - y)]
    return x, y, c, chips


class _Gather:
    def __init__(self, in_refs, out_refs, send_sems, recv_sems):
        self.in_refs, self.out_refs, self.send_sems, self.recv_sems = in_refs, out_refs, send_sems, recv_sems
        self.x, self.y, self.c, self.chips = _place()

    def _copy(self, w, k, chip, hc, to, src=None):
        part = self.out_refs[w].at[2 * chip[0] + chip[1], hc]
        return pltpu.make_async_remote_copy(
            src_ref=part if src is None else src, dst_ref=part, send_sem=self.send_sems.at[w, k],
            recv_sem=self.recv_sems.at[w, k], device_id=to, device_id_type=MESH)

    def _first(self):
        x, y, c = self.x, self.y, self.c
        return [self._copy(w, j, (x, y), c, (*chip, c), src=self.in_refs[w].at[c])
                for w in range(len(self.in_refs)) for j, chip in enumerate(self.chips)]

    def start(self):
        for cp in self._first():
            cp.start()

    def finish(self):
        x, y, c = self.x, self.y, self.c
        me, sibling = (x, y, c), (x, y, 1 - c)
        ws = range(len(self.in_refs))
        passed = []
        for w in ws:
            for j, chip in enumerate(self.chips):
                self._copy(w, j, chip, c, me).wait_recv()
                passed.append(self._copy(w, 3 + j, chip, c, sibling))
                passed[-1].start()
        for w in ws:
            for j, chip in enumerate(self.chips):
                self._copy(w, 3 + j, chip, 1 - c, me).wait_recv()
        for cp in self._first() + passed:
            cp.wait_send()


def _gather_io(shards):
    halves = [(s.shape[0] // 2, s.shape[1]) for s in shards]
    ins = [s.reshape(2, h, cols) for s, (h, cols) in zip(shards, halves)]
    outs = [_sds((N_CHIPS, 2, h, cols), s.dtype) for s, (h, cols) in zip(shards, halves)]
    sems = [pltpu.SemaphoreType.DMA((len(shards), 6)), pltpu.SemaphoreType.DMA((len(shards), 6))]
    return ins, outs, sems


def _gather_assemble(outs, shards):
    me = 2 * lax.axis_index("x") + lax.axis_index("y")
    return [lax.dynamic_update_slice_in_dim(o.reshape((N_CHIPS,) + s.shape), s[None], me, axis=0)
            for o, s in zip(outs, shards)]


class _Scatter:
    def __init__(self, p_refs, out_refs, send_sems, recv_sems):
        self.p_refs, self.out_refs, self.send_sems, self.recv_sems = p_refs, out_refs, send_sems, recv_sems
        self.x, self.y, self.c, self.chips = _place()
        self.me = 2 * self.x + self.y

    def _copy(self, w, j, chip, src_chip, dst_chip):
        return pltpu.make_async_remote_copy(
            src_ref=self.p_refs[w].at[src_chip], dst_ref=self.out_refs[w].at[dst_chip], send_sem=self.send_sems.at[w, j],
            recv_sem=self.recv_sems.at[w, j], device_id=(*chip, self.c), device_id_type=MESH)

    def _sends(self):
        return [self._copy(w, j, chip, 2 * chip[0] + chip[1], self.me)
                for w in range(len(self.p_refs)) for j, chip in enumerate(self.chips)]

    def start(self):
        for cp in self._sends():
            cp.start()

    def finish(self):
        for w in range(len(self.p_refs)):
            for j, chip in enumerate(self.chips):
                self._copy(w, j, chip, self.me, 2 * chip[0] + chip[1]).wait_recv()
        for cp in self._sends():
            cp.wait_send()


def _scatter_io(parts):
    sems = [pltpu.SemaphoreType.DMA((len(parts), 3)), pltpu.SemaphoreType.DMA((len(parts), 3))]
    return list(parts), [_sds(p.shape, p.dtype) for p in parts], sems


class _Swap:
    def __init__(self, g_refs, out_refs, send_sems, recv_sems):
        x, y, c, _ = _place()
        self.copies = []
        for w in range(len(g_refs)):
            half = out_refs[w].shape[1]
            theirs = g_refs[w].at[:, pl.ds(pl.multiple_of((1 - c) * half, 8), half), :]
            self.copies.append(pltpu.make_async_remote_copy(
                src_ref=theirs, dst_ref=out_refs[w], send_sem=send_sems.at[w], recv_sem=recv_sems.at[w],
                device_id=(x, y, 1 - c), device_id_type=MESH))

    def start(self):
        for cp in self.copies:
            cp.start()

    def finish(self):
        for cp in self.copies:
            cp.wait()


def _swap_io(grads):
    outs = [_sds((g.shape[0], g.shape[1] // 2, g.shape[2]), g.dtype) for g in grads]
    return list(grads), outs, [pltpu.SemaphoreType.DMA((len(grads),)), pltpu.SemaphoreType.DMA((len(grads),))]


def _matmul_tn_pair(a, b0, b1, name):
    T, K = a.shape
    tt = min(T, 1024)

    def body(a_ref, b0_ref, b1_ref, o_ref):
        n = pl.program_id(0)

        @pl.when(pl.program_id(1) == 0)
        def _():
            o_ref[...] = jnp.zeros_like(o_ref)

        @pl.when(n < 2)
        def _():
            o_ref[0] += _dot_tn(a_ref[...], b0_ref[...])

        @pl.when(n >= 2)
        def _():
            o_ref[0] += _dot_tn(a_ref[...], b1_ref[...])

    return pl.pallas_call(
        body, name=name, grid=(4, T // tt),
        in_specs=[pl.BlockSpec((tt, K), lambda n, t: (t, 0)),
                  pl.BlockSpec((tt, FF_CHUNK), lambda n, t: (t, jnp.minimum(n, 1))),
                  pl.BlockSpec((tt, FF_CHUNK), lambda n, t: (t, jnp.maximum(n - 2, 0)))],
        out_specs=pl.BlockSpec((1, K, FF_CHUNK), lambda n, t: (n, 0, 0)),
        out_shape=_sds((4, K, FF_CHUNK), F32),
        compiler_params=_cp(("parallel", "arbitrary")),
    )(a, b0, b1)


def _sb_logs(zt, causal):
    e = jnp.exp(-jnp.abs(zt))
    lb = jnp.minimum(zt, 0.0) - jnp.log(1.0 + e)
    l1m = lb - zt
    if causal is not None:
        l1m = jnp.where(causal, l1m, 0.0)
    return lb, l1m


def _sb_weights(lb, suf, causal):
    a = jnp.exp(lb + suf)
    if causal is not None:
        a = jnp.where(causal, a, 0.0)
    return a


def _tri_masks(t):
    r = lax.broadcasted_iota(jnp.int32, (t, t), 0)
    c = lax.broadcasted_iota(jnp.int32, (t, t), 1)
    return r, c


def _sb_fwd(qT, kb, vTb, shards):
    Hh, _, S = qT.shape
    nk, T = kb.shape[1], kb.shape[2]
    nq = S // T
    G = SB_GROUP_FWD
    nw = len(shards)
    g_ins, g_outs, g_sems = _gather_io(shards)

    def body(qT_ref, k_ref, vT_ref, *rest):
        o_ref, rs_ref = rest[nw:nw + 2]
        gather = _Gather(rest[:nw], rest[nw + 2:2 * nw + 2], *rest[2 * nw + 2:])
        i = pl.program_id(1)
        first_step = jnp.logical_and(pl.program_id(0) == 0, i == 0)
        last_step = jnp.logical_and(pl.program_id(0) == pl.num_programs(0) - 1, i == pl.num_programs(1) - 1)

        @pl.when(first_step)
        def _():
            gather.start()

        qts = [(qT_ref[g].astype(F32) * SCALE).astype(_MXU) for g in range(G)]
        r, c = _tri_masks(T)
        upper = (c > r).astype(_MXU)
        causal = r < c

        def blk(j, carry, mask):
            hs = range(G)
            for g in hs:
                rs_ref[g, 0, j] = jnp.broadcast_to(carry[g][0], (8, T))
            zs = [_dot(k_ref[g, j], qts[g]) for g in hs]
            lbs, l1ms = zip(*[_sb_logs(zs[g], mask) for g in hs])
            splits = [_split2(l1ms[g]) for g in hs]
            cums = [_dot(upper, splits[g][0]) + _dot(upper, splits[g][1]) for g in hs]
            avs = [_sb_weights(lbs[g], carry[g][0] + cums[g], mask).astype(_MXU) for g in hs]
            accs = [carry[g][1] + _dot(vT_ref[g, j], avs[g]) for g in hs]
            return tuple((carry[g][0] + _colsum(l1ms[g]), accs[g]) for g in hs)

        def go_on(j, carry):
            top = carry[0][0]
            for g in range(1, G):
                top = jnp.maximum(top, carry[g][0])
            return jnp.logical_and(j >= 0, jnp.max(top) >= SB_DEAD)

        init = tuple((jnp.zeros((1, T), F32), jnp.zeros((HEAD_DIM, T), F32)) for _ in range(G))
        carry = blk(i, init, causal)
        j, carry = lax.while_loop(lambda st: go_on(*st), lambda st: (st[0] - 1, blk(st[0], st[1], None)),
                                  (i - 1, carry))

        @pl.when(j >= 0)
        def _():
            for g in range(G):
                rs_ref[g, 0, j] = jnp.broadcast_to(carry[g][0], (8, T))

        o_ref[...] = jnp.concatenate([carry[g][1] for g in range(G)], axis=0).T

        @pl.when(last_step)
        def _():
            gather.finish()

    any_spec = pl.BlockSpec(memory_space=pl.ANY)
    res = pl.pallas_call(
        body, name="sb_fwd", grid=(Hh // G, nq),
        in_specs=[pl.BlockSpec((G, HEAD_DIM, T), lambda h, i: (h, 0, i)),
                  pl.BlockSpec((G, nk, T, HEAD_DIM), lambda h, i: (h, 0, 0, 0), pipeline_mode=pl.Buffered(1)),
                  pl.BlockSpec((G, nk, HEAD_DIM, T), lambda h, i: (h, 0, 0, 0), pipeline_mode=pl.Buffered(1))]
                 + [any_spec] * nw,
        out_specs=[pl.BlockSpec((T, G * HEAD_DIM), lambda h, i: (i, h)),
                   pl.BlockSpec((G, 1, nk, 8, T), lambda h, i: (h, i, 0, 0, 0))] + [any_spec] * nw,
        out_shape=[_sds((S, Hh * HEAD_DIM), F32), _sds((Hh, nq, nk, 8, T), F32)] + g_outs,
        scratch_shapes=g_sems,
        compiler_params=_cp(("arbitrary", "arbitrary")),
    )(qT, kb, vTb, *g_ins)
    return res[0], res[1], _gather_assemble(res[2:], shards)


def _sb_bwd(qT, kb, kTb, vb, doT, rsave, parts):
    Hh, _, S = qT.shape
    nk, T = kb.shape[1], kb.shape[2]
    nq = S // T
    G = SB_GROUP_BWD
    nw = len(parts)
    s_ins, s_outs, s_sems = _scatter_io(parts)

    def body(qT_ref, k_ref, kT_ref, v_ref, doT_ref, rs_ref, *rest):
        dq_ref, dk_out_ref, dv_out_ref = rest[nw:nw + 3]
        dk_ref, dv_ref = rest[2 * nw + 3:2 * nw + 5]
        scatter = _Scatter(rest[:nw], rest[nw + 3:2 * nw + 3], *rest[2 * nw + 5:])
        i = pl.program_id(1)
        first_step = jnp.logical_and(pl.program_id(0) == 0, i == 0)
        last_step = jnp.logical_and(pl.program_id(0) == pl.num_programs(0) - 1, i == pl.num_programs(1) - 1)

        @pl.when(first_step)
        def _():
            scatter.start()

        @pl.when(i == 0)
        def _():
            dk_ref[...] = jnp.zeros_like(dk_ref)
            dv_ref[...] = jnp.zeros_like(dv_ref)

        qts = [(qT_ref[g].astype(F32) * SCALE).astype(_MXU) for g in range(G)]
        douts = [doT_ref[g] for g in range(G)]
        r, c = _tri_masks(T)
        upper = (c > r).astype(_MXU)
        lower = (c < r).astype(_MXU)
        causal = r < c

        def blk(j, carry, mask):
            hs = range(G)
            zs = [_dot(k_ref[g, j], qts[g]) for g in hs]
            das = [_dot(v_ref[g, j], douts[g]) for g in hs]
            lbs, l1ms = zip(*[_sb_logs(zs[g], mask) for g in hs])
            splits = [_split2(l1ms[g]) for g in hs]
            cums = [_dot(upper, splits[g][0]) + _dot(upper, splits[g][1]) for g in hs]
            avs = [_sb_weights(lbs[g], rs_ref[g, 0, j][0:1, :] + cums[g], mask) for g in hs]
            ets = [das[g] * avs[g] for g in hs]
            esplits = [_split2(ets[g]) for g in hs]
            ecums = [_dot(lower, esplits[g][0]) + _dot(lower, esplits[g][1]) for g in hs]
            dzs = []
            for g in hs:
                sig = jnp.exp(lbs[g])
                dz = ets[g] * (1.0 - sig) - (carry[g][0] + ecums[g]) * sig
                if mask is not None:
                    dz = jnp.where(mask, dz, 0.0)
                dzs.append(dz.astype(_MXU))
            dqs = [carry[g][1] + _dot(kT_ref[g, j], dzs[g]) for g in hs]
            for g in hs:
                dk_ref[j, g * HEAD_DIM:(g + 1) * HEAD_DIM, :] += _dot_nt(qts[g], dzs[g])
            for g in hs:
                dv_ref[j, g * HEAD_DIM:(g + 1) * HEAD_DIM, :] += _dot_nt(douts[g], avs[g].astype(_MXU))
            return tuple((carry[g][0] + _colsum(ets[g]), dqs[g]) for g in hs)

        def live(j):
            jj = jnp.maximum(j, 0)
            top = rs_ref[0, 0, jj][0:1, :]
            for g in range(1, G):
                top = jnp.maximum(top, rs_ref[g, 0, jj][0:1, :])
            return jnp.logical_and(j >= 0, jnp.max(top) >= SB_DEAD)

        first = lax.while_loop(lambda st: st[1], lambda st: (st[0] - 1, live(st[0] - 2)), (i, live(i - 1)))[0]
        carry = tuple((jnp.zeros((1, T), F32), jnp.zeros((HEAD_DIM, T), F32)) for _ in range(G))
        carry = lax.fori_loop(first, i, lambda s, cr: blk(s, cr, None), carry)
        carry = blk(i, carry, causal)
        dq_ref[...] = (jnp.concatenate([carry[g][1] for g in range(G)], axis=0) * SCALE).T.astype(dq_ref.dtype)

        @pl.when(i == pl.num_programs(1) - 1)
        def _():
            def flush(j, _):
                rows = pl.ds(pl.multiple_of(j * T, T), T)
                dk_out_ref[rows, :] = dk_ref[j].T.astype(dk_out_ref.dtype)
                dv_out_ref[rows, :] = dv_ref[j].T.astype(dv_out_ref.dtype)
                return 0
            lax.fori_loop(0, nk, flush, 0)

        @pl.when(last_step)
        def _():
            scatter.finish()

    colblk = pl.BlockSpec((G, HEAD_DIM, T), lambda h, i: (h, 0, i))
    once = pl.Buffered(1)
    kblk = pl.BlockSpec((G, nk, T, HEAD_DIM), lambda h, i: (h, 0, 0, 0), pipeline_mode=once)
    kTblk = pl.BlockSpec((G, nk, HEAD_DIM, T), lambda h, i: (h, 0, 0, 0), pipeline_mode=once)
    any_spec = pl.BlockSpec(memory_space=pl.ANY)
    res = pl.pallas_call(
        body, name="sb_bwd", grid=(Hh // G, nq),
        in_specs=[colblk, kblk, kTblk, kblk, colblk,
                  pl.BlockSpec((G, 1, nk, 8, T), lambda h, i: (h, i, 0, 0, 0))] + [any_spec] * nw,
        out_specs=[pl.BlockSpec((T, G * HEAD_DIM), lambda h, i: (i, h)),
                   pl.BlockSpec((S, G * HEAD_DIM), lambda h, i: (0, h), pipeline_mode=once),
                   pl.BlockSpec((S, G * HEAD_DIM), lambda h, i: (0, h), pipeline_mode=once)] + [any_spec] * nw,
        out_shape=[_sds((S, Hh * HEAD_DIM), _MXU)] * 3 + s_outs,
        scratch_shapes=[pltpu.VMEM((nk, G * HEAD_DIM, T), F32), pltpu.VMEM((nk, G * HEAD_DIM, T), F32)] + s_sems,
        compiler_params=_cp(("arbitrary", "arbitrary"), vmem_mb=60),
    )(qT, kb, kTb, vb, doT, rsave, *s_ins)
    return res[0], res[1], res[2], list(res[3:])


def _bucket_table():
    qi = np.arange(BLOCK)[:, None]
    cj = np.arange(2 * BLOCK)[None, :]
    dist = qi + BLOCK - cj
    exact = REL_BUCKETS // 2
    d = np.maximum(dist, 0)
    d_f = np.maximum(d, 1).astype(np.float32)
    large = exact + (np.log(d_f / np.float32(exact)) / np.float32(math.log(REL_MAX_DIST / exact))
                     * np.float32(REL_BUCKETS - exact)).astype(np.int32)
    large = np.minimum(large, REL_BUCKETS - 1)
    return np.where(d < exact, d, large).astype(np.int32)


def _swa_bias(rel_bias, bucket):
    def body(rb_ref, bk_ref, o_ref):
        bk = bk_ref[...]
        for h in range(SWA_HEADS):
            t = jnp.zeros((2 * BLOCK, BLOCK), F32)
            for b in range(REL_BUCKETS):
                t = jnp.where(bk == b, rb_ref[b, h], t)
            o_ref[h] = t

    return pl.pallas_call(
        body, name="swa_bias",
        in_specs=[pl.BlockSpec(memory_space=pltpu.SMEM), pl.BlockSpec(memory_space=pltpu.VMEM)],
        out_specs=pl.BlockSpec(memory_space=pltpu.VMEM),
        out_shape=_sds((SWA_HEADS, 2 * BLOCK, BLOCK), F32),
    )(rel_bias, bucket)


def _swa_logits(q, kp, kc):
    qs = (q.astype(F32) * SCALE).astype(_MXU)
    return qs, _dot_nt(kp, qs), _dot_nt(kc, qs)


def _swa_softmax(lp, lc, bias, sink, live_prev):
    r, c = _tri_masks(BLOCK)
    in_window = r > c if live_prev is None else jnp.logical_and(r > c, live_prev)
    lp = jnp.where(in_window, lp + bias[:BLOCK, :], -jnp.inf)
    lc = jnp.where(r <= c, lc + bias[BLOCK:, :], -jnp.inf)
    m = jnp.maximum(jnp.maximum(jnp.max(lp, axis=0, keepdims=True), jnp.max(lc, axis=0, keepdims=True)), sink)
    pp = jnp.exp(lp - m)
    pc = jnp.exp(lc - m)
    ps = jnp.exp(sink - m)
    denom = _colsum(pp) + _colsum(pc) + ps
    return pp / denom, pc / denom, ps / denom


def _swa_sub(nb):
    return min(SWA_SUB, nb)


def _swa_keys(b, prev_ref, cur_ref, i):
    cur = cur_ref[0, b * BLOCK:(b + 1) * BLOCK, :]
    if b == 0:
        return prev_ref[0], cur, i > 0
    return cur_ref[0, (b - 1) * BLOCK:b * BLOCK, :], cur, None


def _swa_keys_t(b, prev_ref, cur_ref):
    cur = cur_ref[0, :, b * BLOCK:(b + 1) * BLOCK]
    return (prev_ref[0] if b == 0 else cur_ref[0, :, (b - 1) * BLOCK:b * BLOCK]), cur


SWA_PAIR = 4


def _swa_fwd(q, k, vT, bias, sink):
    S = q.shape[1]
    nb = S // BLOCK
    ns = _swa_sub(nb)
    R = ns * BLOCK
    P = SWA_PAIR

    def body(q_ref, kp_ref, kc_ref, vp_ref, vc_ref, bias_ref, sink_ref, o_ref):
        i = pl.program_id(1)
        units = [(hh, b) for hh in range(P) for b in range(ns)]
        keys = [_swa_keys(b, kp_ref, kc_ref, i) for b in range(ns)]
        vals = [_swa_keys_t(b, vp_ref, vc_ref) for b in range(ns)]
        logits = {u: _swa_logits(q_ref[u[0], u[1] * BLOCK:(u[1] + 1) * BLOCK, :], keys[u[1]][0], keys[u[1]][1])
                  for u in units}
        ws = {u: _swa_softmax(logits[u][1], logits[u][2], bias_ref[u[0]], sink_ref[u[0]][:, :1], keys[u[1]][2])
              for u in units}
        outs = {u: _dot(vals[u[1]][0], ws[u][0].astype(_MXU)) + _dot(vals[u[1]][1], ws[u][1].astype(_MXU))
                for u in units}
        for b in range(ns):
            o_ref[b * BLOCK:(b + 1) * BLOCK, :] = jnp.concatenate([outs[(hh, b)] for hh in range(P)], axis=0).T

    kvh = lambda p: (p * P) // SWA_GROUP
    prev = pl.BlockSpec((1, BLOCK, HEAD_DIM), lambda p, i: (kvh(p), jnp.maximum(i * ns - 1, 0), 0))
    cur = pl.BlockSpec((1, R, HEAD_DIM), lambda p, i: (kvh(p), i, 0))
    prev_t = pl.BlockSpec((1, HEAD_DIM, BLOCK), lambda p, i: (kvh(p), 0, jnp.maximum(i * ns - 1, 0)))
    cur_t = pl.BlockSpec((1, HEAD_DIM, R), lambda p, i: (kvh(p), 0, i))
    return pl.pallas_call(
        body, name="swa_fwd", grid=(SWA_HEADS // P, nb // ns),
        in_specs=[pl.BlockSpec((P, R, HEAD_DIM), lambda p, i: (p, i, 0)), prev, cur, prev_t, cur_t,
                  pl.BlockSpec((P, 2 * BLOCK, BLOCK), lambda p, i: (p, 0, 0)),
                  pl.BlockSpec((P, 1, BLOCK), lambda p, i: (p, 0, 0))],
        out_specs=pl.BlockSpec((R, P * HEAD_DIM), lambda p, i: (i, p)),
        out_shape=_sds((S, SWA_HEADS * HEAD_DIM), F32),
        compiler_params=_cp(("parallel", "parallel")),
    )(q, k, k, vT, vT, bias, sink)


def _swa_bwd(q, k, kT, v, bias, sink, do, grads):
    S = q.shape[1]
    nb = S // BLOCK
    ns = _swa_sub(nb)
    R = ns * BLOCK
    P = SWA_PAIR
    nw = len(grads)
    x_ins, x_outs, x_sems = _swap_io(grads)

    def body(q_ref, kp_ref, kc_ref, ktp_ref, ktc_ref, vp_ref, vc_ref, bias_ref, sink_ref, do_ref, *rest):
        dq_ref, dk_ref, dv_ref, dbias_ref, dsink_ref = rest[nw:nw + 5]
        swap = _Swap(rest[:nw], rest[nw + 5:2 * nw + 5], *rest[2 * nw + 5:])
        g = pl.program_id(1)
        i = pl.program_id(2)
        first_step = jnp.logical_and(pl.program_id(0) == 0, jnp.logical_and(g == 0, i == 0))
        last_step = jnp.logical_and(pl.program_id(0) == pl.num_programs(0) - 1,
                                    jnp.logical_and(g == pl.num_programs(1) - 1, i == pl.num_programs(2) - 1))

        @pl.when(first_step)
        def _():
            swap.start()

        @pl.when(jnp.logical_and(g == 0, i == 0))
        def _():
            dk_ref[...] = jnp.zeros_like(dk_ref)
            dv_ref[...] = jnp.zeros_like(dv_ref)

        @pl.when(i == 0)
        def _():
            dbias_ref[...] = jnp.zeros_like(dbias_ref)
            dsink_ref[...] = jnp.zeros_like(dsink_ref)

        subs = range(ns)
        units = [(hh, b) for hh in range(P) for b in subs]
        rows = [slice(b * BLOCK, (b + 1) * BLOCK) for b in subs]
        keys = [_swa_keys(b, kp_ref, kc_ref, i) for b in subs]
        keys_t = [_swa_keys_t(b, ktp_ref, ktc_ref) for b in subs]
        vals = [_swa_keys(b, vp_ref, vc_ref, i) for b in subs]
        douts = {u: do_ref[u[0], rows[u[1]], :] for u in units}
        logits = {u: _swa_logits(q_ref[u[0], rows[u[1]], :], keys[u[1]][0], keys[u[1]][1]) for u in units}
        dws = {u: (_dot_nt(vals[u[1]][0], douts[u]), _dot_nt(vals[u[1]][1], douts[u])) for u in units}
        wts, dls = {}, {}
        for hh in range(P):
            dbp = jnp.zeros((BLOCK, BLOCK), F32)
            dbc = jnp.zeros((BLOCK, BLOCK), F32)
            dsk = jnp.zeros((1, BLOCK), F32)
            for b in subs:
                u = (hh, b)
                wp, wc, ws = _swa_softmax(logits[u][1], logits[u][2], bias_ref[hh], sink_ref[hh][:, :1], keys[b][2])
                dwp, dwc = dws[u]
                delta = _colsum(wp * dwp) + _colsum(wc * dwc)
                dlp = wp * (dwp - delta)
                dlc = wc * (dwc - delta)
                dbp += dlp
                dbc += dlc
                dsk -= ws * delta
                wts[u] = (wp.astype(_MXU), wc.astype(_MXU))
                dls[u] = (dlp.astype(_MXU), dlc.astype(_MXU))
            dbias_ref[hh, :BLOCK, :] += dbp
            dbias_ref[hh, BLOCK:, :] += dbc
            dsink_ref[hh] += jnp.broadcast_to(dsk, (8, BLOCK))
        dqs = {u: (_dot(keys_t[u[1]][0], dls[u][0]) + _dot(keys_t[u[1]][1], dls[u][1])) * SCALE for u in units}
        for b in subs:
            dq_ref[rows[b], :] = jnp.concatenate([dqs[(hh, b)] for hh in range(P)], axis=0).T.astype(dq_ref.dtype)
        for b in subs:
            blk = i * ns + b
            dk_cur = sum(_dot(dls[(hh, b)][1], logits[(hh, b)][0]) for hh in range(P))
            dv_cur = sum(_dot(wts[(hh, b)][1], douts[(hh, b)]) for hh in range(P))
            dk_prev = sum(_dot(dls[(hh, b)][0], logits[(hh, b)][0]) for hh in range(P))
            dv_prev = sum(_dot(wts[(hh, b)][0], douts[(hh, b)]) for hh in range(P))
            dk_ref[0, blk] += dk_cur
            dv_ref[0, blk] += dv_cur
            if b == 0:
                @pl.when(i > 0)
                def _():
                    dk_ref[0, blk - 1] += dk_prev
                    dv_ref[0, blk - 1] += dv_prev
            else:
                dk_ref[0, blk - 1] += dk_prev
                dv_ref[0, blk - 1] += dv_prev

        @pl.when(last_step)
        def _():
            swap.finish()

    G2 = SWA_GROUP // P
    hp = lambda kv, g, i: kv * G2 + g
    prev = pl.BlockSpec((1, BLOCK, HEAD_DIM), lambda kv, g, i: (kv, jnp.maximum(i * ns - 1, 0), 0))
    cur = pl.BlockSpec((1, R, HEAD_DIM), lambda kv, g, i: (kv, i, 0))
    prev_t = pl.BlockSpec((1, HEAD_DIM, BLOCK), lambda kv, g, i: (kv, 0, jnp.maximum(i * ns - 1, 0)))
    cur_t = pl.BlockSpec((1, HEAD_DIM, R), lambda kv, g, i: (kv, 0, i))
    qblk = pl.BlockSpec((P, R, HEAD_DIM), lambda kv, g, i: (hp(kv, g, i), i, 0))
    kvacc = pl.BlockSpec((1, nb, BLOCK, HEAD_DIM), lambda kv, g, i: (kv, 0, 0, 0))
    any_spec = pl.BlockSpec(memory_space=pl.ANY)
    res = pl.pallas_call(
        body, name="swa_bwd", grid=(SWA_KV_HEADS, G2, nb // ns),
        in_specs=[qblk, prev, cur, prev_t, cur_t, prev, cur,
                  pl.BlockSpec((P, 2 * BLOCK, BLOCK), lambda kv, g, i: (hp(kv, g, i), 0, 0)),
                  pl.BlockSpec((P, 1, BLOCK), lambda kv, g, i: (hp(kv, g, i), 0, 0)), qblk] + [any_spec] * nw,
        out_specs=[pl.BlockSpec((R, P * HEAD_DIM), lambda kv, g, i: (i, hp(kv, g, i))), kvacc, kvacc,
                   pl.BlockSpec((P, 2 * BLOCK, BLOCK), lambda kv, g, i: (hp(kv, g, i), 0, 0)),
                   pl.BlockSpec((P, 8, BLOCK), lambda kv, g, i: (hp(kv, g, i), 0, 0))] + [any_spec] * nw,
        out_shape=[_sds((S, SWA_HEADS * HEAD_DIM), _MXU), _sds((SWA_KV_HEADS, nb, BLOCK, HEAD_DIM), F32),
                   _sds((SWA_KV_HEADS, nb, BLOCK, HEAD_DIM), F32), _sds((SWA_HEADS, 2 * BLOCK, BLOCK), F32),
                   _sds((SWA_HEADS, 8, BLOCK), F32)] + x_outs,
        scratch_shapes=x_sems,
        compiler_params=_cp(("arbitrary", "arbitrary", "arbitrary")),
    )(q, k, k, kT, kT, v, v, bias, sink, do, *x_ins)
    return res[0], res[1], res[2], res[3], res[4], list(res[5:])


def _swa_small_grads(dbias, dsink, bucket):
    rows = REL_BUCKETS + 8

    def total(x):
        return jnp.sum(jnp.sum(x, axis=1, keepdims=True), axis=0, keepdims=True)

    def body(db_ref, ds_ref, bk_ref, o_ref):
        bk = bk_ref[...]
        r = lax.broadcasted_iota(jnp.int32, (rows, BLOCK), 0)
        c = lax.broadcasted_iota(jnp.int32, (rows, BLOCK), 1)
        out = jnp.zeros((rows, BLOCK), F32)
        for h in range(SWA_HEADS):
            db = db_ref[h]
            for b in range(REL_BUCKETS):
                s = total(jnp.where(bk == b, db, 0.0))
                out = jnp.where(jnp.logical_and(r == b, c == h), s, out)
            s = jnp.sum(ds_ref[h][0:1, :], axis=1, keepdims=True)
            out = jnp.where(jnp.logical_and(r == REL_BUCKETS, c == h), s, out)
        o_ref[...] = out

    vm = pl.BlockSpec(memory_space=pltpu.VMEM)
    return pl.pallas_call(body, name="swa_small_grads", in_specs=[vm, vm, vm], out_specs=vm,
                          out_shape=_sds((rows, BLOCK), F32))(dbias, dsink, bucket)


def _tile_rows(n):
    for t in (512, 352, 256, 176, 128, 64, 32, 16, 8):
        if n % t == 0:
            return t
    return n


def _cast_rows(x, dtype, name):
    R, C = x.shape
    tr = _tile_rows(R)

    def body(x_ref, o_ref):
        o_ref[...] = x_ref[...].astype(o_ref.dtype)

    return pl.pallas_call(body, name=name, grid=(R // tr,), in_specs=[_rows(tr, C)], out_specs=_rows(tr, C),
                          out_shape=_sds((R, C), dtype), compiler_params=_cp(("parallel",)))(x)


def _pair_sum(g, recv, c, name):
    n, half, C = recv.shape
    tr = _tile_rows(half)

    def body(c_ref, a_ref, b_ref, o_ref):
        o_ref[...] = (a_ref[0] + b_ref[...]).astype(o_ref.dtype)

    return pl.pallas_call(
        body, name=name,
        grid_spec=pltpu.PrefetchScalarGridSpec(
            num_scalar_prefetch=1, grid=(n, half // tr),
            in_specs=[pl.BlockSpec((1, 1, tr, C), lambda j, i, c_ref: (j, c_ref[0], i, 0)),
                      pl.BlockSpec((1, tr, C), lambda j, i, c_ref: (j, i, 0))],
            out_specs=pl.BlockSpec((1, tr, C), lambda j, i, c_ref: (j, i, 0))),
        out_shape=_sds((n, half, C), _MXU),
        compiler_params=_cp(("parallel", "parallel")))(c.reshape(1), g.reshape(n, 2, half, C), recv)


def _chip_sum(own, recv, me, name):
    n, R, C = recv.shape
    tr = _tile_rows(R)

    def body(me_ref, own_ref, recv_ref, o_ref):
        acc = None
        for j in range(n):
            term = jnp.where(me_ref[0] == j, own_ref[0], recv_ref[j]).astype(F32)
            acc = term if acc is None else acc + term
        o_ref[...] = acc

    return pl.pallas_call(
        body, name=name,
        grid_spec=pltpu.PrefetchScalarGridSpec(
            num_scalar_prefetch=1, grid=(R // tr,),
            in_specs=[pl.BlockSpec((1, tr, C), lambda i, me_ref: (me_ref[0], i, 0)),
                      pl.BlockSpec((n, tr, C), lambda i, me_ref: (0, i, 0))],
            out_specs=pl.BlockSpec((tr, C), lambda i, me_ref: (i, 0))),
        out_shape=_sds((R, C), F32), compiler_params=_cp(("parallel",)))(me.reshape(1), own, recv)


def _adamw_math(w, g, m, v):
    m = ADAM_B1 * m + (1.0 - ADAM_B1) * g
    v = ADAM_B2 * v + (1.0 - ADAM_B2) * (g * g)
    m_hat = m / (1.0 - ADAM_B1 ** ADAM_STEP)
    v_hat = v / (1.0 - ADAM_B2 ** ADAM_STEP)
    delta = -ADAM_LR * (m_hat / (jnp.sqrt(v_hat) + ADAM_EPS) + ADAM_WD * w)
    return delta, m, v


def _adamw(w, g, m, v, name):
    R, C = w.shape
    tr = _tile_rows(R)

    def body(w_ref, g_ref, m_ref, v_ref, d_ref, nm_ref, nv_ref):
        d, nm, nv = _adamw_math(w_ref[...], g_ref[...], m_ref[...], v_ref[...])
        d_ref[...] = d
        nm_ref[...] = nm
        nv_ref[...] = nv

    blk = _rows(tr, C)
    return pl.pallas_call(body, name=name, grid=(R // tr,), in_specs=[blk] * 4, out_specs=[blk] * 3,
                          out_shape=[_sds((R, C), F32)] * 3, compiler_params=_cp(("parallel",)))(w, g, m, v)


def _gather_weights(shards):
    nw = len(shards)
    ins, outs, sems = _gather_io(shards)

    def body(*refs):
        ex = _Gather(refs[:nw], refs[nw:2 * nw], *refs[2 * nw:])
        ex.start()
        ex.finish()

    any_spec = pl.BlockSpec(memory_space=pl.ANY)
    got = pl.pallas_call(body, name="gather_weights", in_specs=[any_spec] * nw, out_specs=[any_spec] * nw,
                         out_shape=outs, scratch_shapes=sems)(*ins)
    return _gather_assemble(got, shards)


def _swap_halves(grads, name):
    nw = len(grads)
    ins, outs, sems = _swap_io(grads)

    def body(*refs):
        ex = _Swap(refs[:nw], refs[nw:2 * nw], *refs[2 * nw:])
        ex.start()
        ex.finish()

    any_spec = pl.BlockSpec(memory_space=pl.ANY)
    return pl.pallas_call(body, name=name, in_specs=[any_spec] * nw, out_specs=[any_spec] * nw,
                          out_shape=outs, scratch_shapes=sems)(*ins)


def _scatter_partials(parts):
    nw = len(parts)
    ins, outs, sems = _scatter_io(parts)

    def body(*refs):
        ex = _Scatter(refs[:nw], refs[nw:2 * nw], *refs[2 * nw:])
        ex.start()
        ex.finish()

    any_spec = pl.BlockSpec(memory_space=pl.ANY)
    return pl.pallas_call(body, name="scatter_partials", in_specs=[any_spec] * nw, out_specs=[any_spec] * nw,
                          out_shape=outs, scratch_shapes=sems)(*ins)


def _join_halves(sums):
    nw = len(sums)

    def body(*refs):
        f_refs, out_refs = refs[:nw], refs[nw:2 * nw]
        send_sems, recv_sems = refs[2 * nw:]
        x, y, c, _ = _place()
        ws = range(nw)

        def copy(w, half_index):
            return pltpu.make_async_remote_copy(
                src_ref=f_refs[w], dst_ref=out_refs[w].at[half_index], send_sem=send_sems.at[w],
                recv_sem=recv_sems.at[w], device_id=(x, y, 1 - c), device_id_type=MESH)

        sends = [copy(w, c) for w in ws]
        for cp in sends:
            cp.start()
        for w in ws:
            copy(w, 1 - c).wait_recv()
        for cp in sends:
            cp.wait_send()

    any_spec = pl.BlockSpec(memory_space=pl.ANY)
    outs = pl.pallas_call(
        body, name="join_halves", in_specs=[any_spec] * nw, out_specs=[any_spec] * nw,
        out_shape=[_sds((2,) + f.shape, f.dtype) for f in sums],
        scratch_shapes=[pltpu.SemaphoreType.DMA((nw,)), pltpu.SemaphoreType.DMA((nw,))],
    )(*sums)
    c = lax.axis_index("c")
    return [lax.dynamic_update_slice_in_dim(o, f[None], c, axis=0).reshape(2 * f.shape[0], f.shape[1])
            for o, f in zip(outs, sums)]


def _allreduce_small(block):
    m_per, n = block.shape

    def body(x_ref, sum_ref, loss_ref, all_ref, send_sems, recv_sems, local_sem):
        x, y, c, chips = _place()
        me, sibling = (x, y, c), (x, y, 1 - c)

        def rows(px, py, pc):
            return all_ref.at[pl.ds(pl.multiple_of((4 * px + 2 * py + pc) * m_per, 8), m_per), :]

        def copy(k, blk, to, src=None):
            return pltpu.make_async_remote_copy(
                src_ref=rows(*blk) if src is None else src, dst_ref=rows(*blk), send_sem=send_sems.at[k],
                recv_sem=recv_sems.at[k], device_id=to, device_id_type=MESH)

        mine = pltpu.make_async_copy(x_ref, rows(*me), local_sem)
        mine.start()
        first = [copy(0, me, sibling, src=x_ref)]
        first += [copy(1 + j, me, (*chip, c), src=x_ref) for j, chip in enumerate(chips)]
        for cp in first:
            cp.start()
        passed = [copy(4 + j, (*chip, c), sibling) for j, chip in enumerate(chips)]
        for j, chip in enumerate(chips):
            copy(1 + j, (*chip, c), me).wait_recv()
            passed[j].start()
        copy(0, sibling, me).wait_recv()
        for j, chip in enumerate(chips):
            copy(4 + j, (*chip, 1 - c), me).wait_recv()
        for cp in first + passed:
            cp.wait_send()
        mine.wait()

        acc = all_ref[0:m_per, :]
        for d in range(1, 8):
            acc = acc + all_ref[d * m_per:(d + 1) * m_per, :]
        sum_ref[...] = acc
        tot = jnp.sum(acc[8:9, :], axis=1, keepdims=True) * (0.5 / D_MODEL)
        loss_ref[...] = jnp.broadcast_to(tot, loss_ref.shape)

    vm = pl.BlockSpec(memory_space=pltpu.VMEM)
    return pl.pallas_call(
        body, name="allreduce_small", in_specs=[vm], out_specs=[vm, vm],
        out_shape=[_sds((m_per, n), F32), _sds((8, 128), F32)],
        scratch_shapes=[pltpu.VMEM((8 * m_per, n), F32), pltpu.SemaphoreType.DMA((7,)), pltpu.SemaphoreType.DMA((7,)),
                        pltpu.SemaphoreType.DMA],
    )(block)


def _heads_rows(x, nh):
    S = x.shape[0]
    return x.reshape(S, nh, HEAD_DIM).transpose(1, 0, 2)


def _heads_cols(x, nh):
    S = x.shape[0]
    return x.reshape(S, nh, HEAD_DIM).transpose(1, 2, 0)


def _key_blocks(x, nh, t):
    S = x.shape[0]
    return x.reshape(S // t, t, nh, HEAD_DIM).transpose(2, 0, 1, 3)


def _pad_row(v):
    v = v.reshape(1, -1)
    return jnp.pad(v, ((0, 0), (0, D_MODEL - v.shape[1])))


def _pack_small(ln_in_g, ln_in_b, sb_g, swa_g, sinks, rel_bias, ln1_g, ln1_b, ln2_g, ln2_b, extra):
    rows = [_pad_row(ln_in_g), _pad_row(ln_in_b), jnp.concatenate([sb_g.reshape(1, -1), swa_g.reshape(1, -1)], axis=1),
            _pad_row(jnp.concatenate([rel_bias.reshape(1, -1), sinks.reshape(1, -1)], axis=1)),
            _pad_row(ln1_g), _pad_row(ln1_b), _pad_row(ln2_g), _pad_row(ln2_b), _pad_row(extra)]
    rows.append(jnp.zeros((SMALL_ROWS - len(rows), D_MODEL), F32))
    return jnp.concatenate(rows, axis=0)


def _unpack_small(blk):
    nrb = REL_BUCKETS * SWA_HEADS
    return (blk[0], blk[1], blk[2:3, :SB_WIDTH], blk[2:3, SB_WIDTH:], blk[3:4, nrb:nrb + SWA_HEADS],
            blk[3, :nrb].reshape(REL_BUCKETS, SWA_HEADS), blk[4:5], blk[5:6], blk[6:7], blk[7:8])


def kernel(x, ln_in_g, ln_in_b, w_in, sb_norm_g, swa_norm_g, sinks, rel_bias, w_out, ln1_g, ln1_b, w_gate_up, w_down, ln2_g, ln2_b, loss_target, m_ln_in_g, m_ln_in_b, m_w_in, m_sb_norm_g, m_swa_norm_g, m_sinks, m_rel_bias, m_w_out, m_ln1_g, m_ln1_b, m_w_gate_up, m_w_down, m_ln2_g, m_ln2_b, v_ln_in_g, v_ln_in_b, v_w_in, v_sb_norm_g, v_swa_norm_g, v_sinks, v_rel_bias, v_w_out, v_ln1_g, v_ln1_b, v_w_gate_up, v_w_down, v_ln2_g, v_ln2_b):
    S = x.shape[1]
    x2 = x.reshape(S, D_MODEL)
    tgt = loss_target.reshape(S, D_MODEL)
    T = min(S, SB_TILE)
    bucket = jnp.asarray(_bucket_table().T)
    row = lambda v: v.reshape(1, -1)

    shards = [_cast_rows(w[0], _MXU, "cast_" + n) for n, w in (("w_in", w_in), ("w_out", w_out), ("w_gate_up", w_gate_up), ("w_down", w_down))]
    (w_in_sh,) = _gather_weights(shards[:1])
    w_in_f = jnp.concatenate([w_in_sh[j] for j in range(N_CHIPS)], axis=1)

    h0, h0b, _, k_sb, v_sb, q_sw, kv_sw, qT_sb, kTb_sb, vTb_sb = _ln_in_proj(x2, row(ln_in_g), row(ln_in_b), w_in_f)
    k_sw, v_sw = kv_sw[:, :SWA_KV_WIDTH], kv_sw[:, SWA_KV_WIDTH:]
    kb_sb = _key_blocks(k_sb, SB_HEADS, T)
    sb_out, rsave, (w_out_sh, w_gu_sh, w_down_sh) = _sb_fwd(qT_sb, kb_sb, vTb_sb, shards[1:])
    w_out_f = w_out_sh.reshape(D_MODEL, D_MODEL)
    w_down_f = w_down_sh.reshape(D_FF, D_MODEL)

    bias = _swa_bias(rel_bias, bucket)
    sink_rows = jnp.broadcast_to(sinks.reshape(SWA_HEADS, 1, 1), (SWA_HEADS, 1, BLOCK))
    qh_sw, kh_sw, vh_sw = _heads_rows(q_sw, SWA_HEADS), _heads_rows(k_sw, SWA_KV_HEADS), _heads_rows(v_sw, SWA_KV_HEADS)
    swa_out = _swa_fwd(qh_sw, kh_sw, _heads_cols(v_sw, SWA_KV_HEADS), bias, sink_rows)

    pre1, merged, h1b = _mix_out(sb_out, swa_out, sb_norm_g, swa_norm_g, w_out_f, h0, ln1_g, ln1_b)
    act, silu, dsilu_up = _ffn_up(h1b, w_gu_sh)
    dp2, dp2b, dg2, db2, errsum = _ffn_down_loss(act, w_down_f, pre1, ln1_g, ln1_b, ln2_g, ln2_b, tgt)

    g_w_down = _matmul_tn(act, dp2b, "grad_w_down", FF_CHUNK, D_MODEL)
    dgate, dup = _ffn_down_bwd(dp2b, w_down_f, silu, dsilu_up)
    g_w_gu = _matmul_tn_pair(h1b, dgate, dup, "grad_w_gate_up")
    dp1, dp1b, dg1, db1 = _ffn_up_bwd(dgate, dup, w_gu_sh, dp2, pre1, ln1_g)
    g_w_out = _matmul_tn(merged, dp1b, "grad_w_out", D_MODEL, D_MODEL)
    doT_sb, dsw, dgsb, dgsw = _mix_bwd(dp1b, w_out_f, sb_out, swa_out, sb_norm_g, swa_norm_g)

    c = lax.axis_index("c").astype(jnp.int32)
    me = (2 * lax.axis_index("x") + lax.axis_index("y")).astype(jnp.int32)
    grads_a = [g_w_out.reshape(N_CHIPS, D_MODEL // N_CHIPS, D_MODEL), g_w_gu, g_w_down.reshape(N_CHIPS, D_FF // N_CHIPS, D_MODEL)]
    names_a = ("w_out", "w_gate_up", "w_down")
    dq_sw, dkh_sw, dvh_sw, dbias, dsink, swapped_a = _swa_bwd(qh_sw, kh_sw, _heads_cols(k_sw, SWA_KV_HEADS), vh_sw, bias,
                                                               sink_rows, _heads_rows(dsw, SWA_HEADS), grads_a)
    swa_small = _swa_small_grads(dbias, dsink, bucket)
    partials_a = [_pair_sum(g, r, c, "pair_sum_" + n) for g, r, n in zip(grads_a, swapped_a, names_a)]
    dq_sb, dk_sb, dv_sb, recv_a = _sb_bwd(qT_sb, kb_sb, kTb_sb, _key_blocks(v_sb, SB_HEADS, T),
                                             doT_sb, rsave, partials_a)
    tok = lambda t, nh: t.reshape(nh, S, HEAD_DIM).transpose(1, 0, 2).reshape(S, nh * HEAD_DIM)
    dproj = [dq_sb, dk_sb, dv_sb, dq_sw,
             jnp.concatenate([tok(dkh_sw, SWA_KV_HEADS), tok(dvh_sw, SWA_KV_HEADS)], axis=1).astype(_MXU)]
    g_w_in = jnp.concatenate([_matmul_tn(h0b, d, "grad_w_in_%d" % k, D_MODEL, d.shape[1]) for k, d in enumerate(dproj)],
                             axis=1)

    cin = IN_COLS // N_CHIPS
    grads_b = [jnp.stack([g_w_in[:, j * cin:(j + 1) * cin] for j in range(N_CHIPS)])]
    partials_b = [_pair_sum(grads_b[0], _swap_halves(grads_b, "swap_halves_in")[0], c, "pair_sum_w_in")]
    grad_x, dg_in, db_in, recv_b = _in_proj_bwd(dproj, w_in_f, dp1, x2, row(ln_in_g), partials_b)
    names = ("w_in",) + names_a
    sums = [_chip_sum(p, r, me, "chip_sum_" + n) for p, r, n in zip(partials_b + partials_a, list(recv_b) + list(recv_a), names)]
    gs_in, gs_out, gs_gu, gs_down = _join_halves(sums)

    nrb = REL_BUCKETS * SWA_HEADS
    small = _pack_small(dg_in, db_in, dgsb, dgsw, swa_small[REL_BUCKETS, :SWA_HEADS],
                        swa_small[:REL_BUCKETS, :SWA_HEADS], dg1, db1, dg2, db2, errsum)
    g_small, loss_tile = _allreduce_small(small)
    loss = loss_tile[0, 0]

    big = []
    for name, w, g, m, v in (("adamw_w_in", w_in, gs_in, m_w_in, v_w_in), ("adamw_w_out", w_out, gs_out, m_w_out, v_w_out),
                             ("adamw_w_gate_up", w_gate_up, gs_gu, m_w_gate_up, v_w_gate_up),
                             ("adamw_w_down", w_down, gs_down, m_w_down, v_w_down)):
        d, nm, nv = _adamw(w[0], g, m[0], v[0], name)
        big.append((g[None], d[None], nm[None], nv[None]))
    zero = jnp.zeros((1,), F32)
    w_small = _pack_small(ln_in_g, ln_in_b, sb_norm_g, swa_norm_g, sinks, rel_bias, ln1_g, ln1_b, ln2_g, ln2_b, zero)
    m_small = _pack_small(m_ln_in_g, m_ln_in_b, m_sb_norm_g, m_swa_norm_g, m_sinks, m_rel_bias, m_ln1_g, m_ln1_b,
                          m_ln2_g, m_ln2_b, zero)
    v_small = _pack_small(v_ln_in_g, v_ln_in_b, v_sb_norm_g, v_swa_norm_g, v_sinks, v_rel_bias, v_ln1_g, v_ln1_b,
                          v_ln2_g, v_ln2_b, zero)
    small_out = [_unpack_small(t) for t in (g_small,) + tuple(_adamw(w_small, g_small, m_small, v_small, "adamw_small"))]

    def kind(k):
        s = small_out[k]
        return [s[0], s[1], big[0][k], s[2], s[3], s[4], s[5], big[1][k], s[6], s[7], big[2][k], big[3][k], s[8], s[9]]

    return (loss, grad_x.reshape(1, S, D_MODEL), *kind(0), *kind(1), *kind(2), *kind(3))
```

```python
import functools
import math

import numpy as np
import jax
import jax.numpy as jnp
from jax import lax
from jax.experimental import pallas as pl
from jax.experimental.pallas import tpu as pltpu

F32 = jnp.float32
_MXU = jnp.bfloat16

D_MODEL = 1024
HEAD_DIM = 64
SB_HEADS = 8
SWA_HEADS = 8
SWA_KV_HEADS = 2
SWA_GROUP = SWA_HEADS // SWA_KV_HEADS
SB_WIDTH = SB_HEADS * HEAD_DIM
SWA_WIDTH = SWA_HEADS * HEAD_DIM
SWA_KV_WIDTH = SWA_KV_HEADS * HEAD_DIM
IN_COLS = 3 * SB_WIDTH + SWA_WIDTH + 2 * SWA_KV_WIDTH
BLOCK = 128
REL_BUCKETS = 32
REL_MAX_DIST = 128
D_FF = 2816
FF_CHUNK = D_FF // 2
ALPHA = 2.0 ** 0.25
LN_EPS = 1e-5
RMS_EPS = 1e-6
SCALE = HEAD_DIM ** -0.5
SB_TILE = 256
SB_GROUP_FWD = 8
SB_GROUP_BWD = 4
SB_DEAD = -105.0
SWA_SUB = 8

ADAM_LR = 0.001
ADAM_B1 = 0.9
ADAM_B2 = 0.999
ADAM_EPS = 1e-08
ADAM_WD = 0.01
ADAM_STEP = 10

N_CHIPS = 4
SMALL_ROWS = 16

MESH = pl.DeviceIdType.MESH


def _sds(shape, dtype):
    return jax.ShapeDtypeStruct(shape, dtype)


def _cp(sem=None, vmem_mb=48):
    kw = dict(vmem_limit_bytes=vmem_mb * 1024 * 1024)
    if sem is not None:
        kw["dimension_semantics"] = sem
    return pltpu.CompilerParams(**kw)


def _dot(a, b):
    return jnp.dot(a, b, preferred_element_type=F32)


def _dot_nt(a, b):
    return lax.dot_general(a, b, (((1,), (1,)), ((), ())), preferred_element_type=F32)


def _dot_tn(a, b):
    return lax.dot_general(a, b, (((0,), (0,)), ((), ())), preferred_element_type=F32)


def _ln_hat(x):
    mu = jnp.mean(x, axis=-1, keepdims=True)
    xc = x - mu
    var = jnp.mean(xc * xc, axis=-1, keepdims=True)
    rstd = lax.rsqrt(var + LN_EPS)
    return xc * rstd, rstd


def _ln_bwd(xhat, rstd, dy, g):
    dxh = dy * g
    m1 = jnp.mean(dxh, axis=-1, keepdims=True)
    m2 = jnp.mean(dxh * xhat, axis=-1, keepdims=True)
    return rstd * (dxh - m1 - xhat * m2)


def _colsum(x):
    return jnp.sum(x, axis=0, keepdims=True)


def _split2(x):
    hi = x.astype(_MXU)
    lo = (x - hi.astype(F32)).astype(_MXU)
    return hi, lo


def _rows(tm, n):
    return pl.BlockSpec((tm, n), lambda i: (i, 0))


def _fixed(*shape):
    nd = len(shape)
    return pl.BlockSpec(shape, lambda i: (0,) * nd)


IN_SECTIONS = (SB_WIDTH, SB_WIDTH, SB_WIDTH, SWA_WIDTH, 2 * SWA_KV_WIDTH)


def _ln_in_proj(x, g, b, w):
    S = x.shape[0]
    tm = min(S, SB_TILE)
    offs = np.cumsum((0,) + IN_SECTIONS)
    ns = len(IN_SECTIONS)

    def body(x_ref, g_ref, b_ref, w_ref, h_ref, hb_ref, *o_refs):
        p_refs, (qT_ref, kT_ref, vT_ref, kr_ref, vr_ref) = o_refs[:ns], o_refs[ns:]
        xhat, _ = _ln_hat(x_ref[...])
        h = xhat * g_ref[...] + b_ref[...]
        h_ref[...] = h
        hb = h.astype(_MXU)
        hb_ref[...] = hb
        proj = _dot(hb, w_ref[...])
        for k, p_ref in enumerate(p_refs):
            p_ref[...] = proj[:, offs[k]:offs[k + 1]].astype(p_ref.dtype)
        heads = lambda k: proj[:, offs[k]:offs[k + 1]].T.astype(_MXU).reshape(SB_HEADS, HEAD_DIM, tm)
        qT_ref[...] = heads(0)
        kT_ref[:, 0] = heads(1)
        vT_ref[:, 0] = heads(2)
        for hd in range(SB_HEADS):
            cols = slice(hd * HEAD_DIM, (hd + 1) * HEAD_DIM)
            kr_ref[hd, 0] = proj[:, offs[1]:offs[2]][:, cols].astype(_MXU)
            vr_ref[hd, 0] = proj[:, offs[2]:offs[3]][:, cols].astype(_MXU)

    blocked = pl.BlockSpec((SB_HEADS, 1, HEAD_DIM, tm), lambda i: (0, i, 0, 0))
    blocked_rows = pl.BlockSpec((SB_HEADS, 1, tm, HEAD_DIM), lambda i: (0, i, 0, 0))
    return pl.pallas_call(
        body, name="ln_in_proj", grid=(S // tm,),
        in_specs=[_rows(tm, D_MODEL), _fixed(1, D_MODEL), _fixed(1, D_MODEL), _fixed(D_MODEL, IN_COLS)],
        out_specs=[_rows(tm, D_MODEL), _rows(tm, D_MODEL)] + [_rows(tm, n) for n in IN_SECTIONS]
                  + [pl.BlockSpec((SB_HEADS, HEAD_DIM, tm), lambda i: (0, 0, i)), blocked, blocked, blocked_rows,
                     blocked_rows],
        out_shape=[_sds((S, D_MODEL), F32), _sds((S, D_MODEL), _MXU)] + [_sds((S, n), _MXU) for n in IN_SECTIONS]
                  + [_sds((SB_HEADS, HEAD_DIM, S), _MXU), _sds((SB_HEADS, S // tm, HEAD_DIM, tm), _MXU),
                     _sds((SB_HEADS, S // tm, HEAD_DIM, tm), _MXU), _sds((SB_HEADS, S // tm, tm, HEAD_DIM), _MXU),
                     _sds((SB_HEADS, S // tm, tm, HEAD_DIM), _MXU)],
        compiler_params=_cp(("parallel",)),
    )(x, g, b, w)


def _rms(x, g):
    r = lax.rsqrt(jnp.mean(x * x, axis=-1, keepdims=True) + RMS_EPS)
    return x * r * g, r


def _mix_out(sb, sw, gsb, gsw, w_out, h0, g1, b1):
    S = sb.shape[0]
    tm = min(S, 512)

    def body(sb_ref, sw_ref, gsb_ref, gsw_ref, w_ref, h0_ref, g1_ref, b1_ref, pre_ref, mg_ref, h1_ref):
        ysb, _ = _rms(sb_ref[...], gsb_ref[...])
        ysw, _ = _rms(sw_ref[...], gsw_ref[...])
        ysb = ysb.astype(_MXU)
        ysw = ysw.astype(_MXU)
        mg_ref[:, :SB_WIDTH] = ysb
        mg_ref[:, SB_WIDTH:] = ysw
        mix = _dot(ysb, w_ref[:SB_WIDTH, :]) + _dot(ysw, w_ref[SB_WIDTH:, :])
        pre1 = ALPHA * h0_ref[...] + mix
        pre_ref[...] = pre1
        xhat, _ = _ln_hat(pre1)
        h1_ref[...] = (xhat * g1_ref[...] + b1_ref[...]).astype(h1_ref.dtype)

    vec = _fixed(1, D_MODEL)
    return pl.pallas_call(
        body, name="mix_out", grid=(S // tm,),
        in_specs=[_rows(tm, SB_WIDTH), _rows(tm, SWA_WIDTH), _fixed(1, SB_WIDTH), _fixed(1, SWA_WIDTH),
                  _fixed(D_MODEL, D_MODEL), _rows(tm, D_MODEL), vec, vec],
        out_specs=[_rows(tm, D_MODEL), _rows(tm, D_MODEL), _rows(tm, D_MODEL)],
        out_shape=[_sds((S, D_MODEL), F32), _sds((S, D_MODEL), _MXU), _sds((S, D_MODEL), _MXU)],
        compiler_params=_cp(("parallel",)),
    )(sb, sw, gsb, gsw, w_out, h0, g1, b1)


def _sigmoid(x):
    return 1.0 / (1.0 + jnp.exp(-x))


def _ffn_up(h1b, wgu):
    S = h1b.shape[0]
    tm = min(S, 1024)

    def body(h_ref, wg_ref, wu_ref, a_ref, s1_ref, s2_ref):
        h1 = h_ref[...]
        gate = _dot(h1, wg_ref[0])
        up = _dot(h1, wu_ref[0])
        sg = _sigmoid(gate)
        silu = gate * sg
        a_ref[...] = (silu * up).astype(a_ref.dtype)
        s1_ref[...] = silu.astype(s1_ref.dtype)
        s2_ref[...] = (up * (sg * (1.0 + gate * (1.0 - sg)))).astype(s2_ref.dtype)

    chunk = pl.BlockSpec((tm, FF_CHUNK), lambda j, i: (i, j))
    return pl.pallas_call(
        body, name="ffn_up", grid=(2, S // tm),
        in_specs=[pl.BlockSpec((tm, D_MODEL), lambda j, i: (i, 0)),
                  pl.BlockSpec((1, D_MODEL, FF_CHUNK), lambda j, i: (j, 0, 0)),
                  pl.BlockSpec((1, D_MODEL, FF_CHUNK), lambda j, i: (j + 2, 0, 0))],
        out_specs=[chunk, chunk, chunk],
        out_shape=[_sds((S, D_FF), _MXU)] * 3,
        compiler_params=_cp(("arbitrary", "arbitrary"), vmem_mb=56),
    )(h1b, wgu, wgu)


def _ffn_down_loss(a, w_down, pre1, g1, b1, g2, b2, tgt):
    S = a.shape[0]
    tm = min(S, 512)

    def body(a_ref, w_ref, p_ref, g1_ref, b1_ref, g2_ref, b2_ref, t_ref, d_ref, db_ref, dg2_ref, db2_ref, err_ref):
        @pl.when(pl.program_id(0) == 0)
        def _():
            dg2_ref[...] = jnp.zeros_like(dg2_ref)
            db2_ref[...] = jnp.zeros_like(db2_ref)
            err_ref[...] = jnp.zeros_like(err_ref)

        xhat1, _ = _ln_hat(p_ref[...])
        h1 = xhat1 * g1_ref[...] + b1_ref[...]
        pre2 = ALPHA * h1 + _dot(a_ref[...], w_ref[...])
        xhat2, rstd2 = _ln_hat(pre2)
        err = xhat2 * g2_ref[...] + b2_ref[...] - t_ref[...]
        dh2 = err * (1.0 / D_MODEL)
        dp2 = _ln_bwd(xhat2, rstd2, dh2, g2_ref[...])
        d_ref[...] = dp2
        db_ref[...] = dp2.astype(db_ref.dtype)
        dg2_ref[...] += _colsum(dh2 * xhat2)
        db2_ref[...] += _colsum(dh2)
        err_ref[...] += _colsum(err * err)

    vec = _fixed(1, D_MODEL)
    return pl.pallas_call(
        body, name="ffn_down_loss", grid=(S // tm,),
        in_specs=[_rows(tm, D_FF), _fixed(D_FF, D_MODEL), _rows(tm, D_MODEL), vec, vec, vec, vec, _rows(tm, D_MODEL)],
        out_specs=[_rows(tm, D_MODEL), _rows(tm, D_MODEL), vec, vec, vec],
        out_shape=[_sds((S, D_MODEL), F32), _sds((S, D_MODEL), _MXU), _sds((1, D_MODEL), F32), _sds((1, D_MODEL), F32),
                   _sds((1, D_MODEL), F32)],
        compiler_params=_cp(("arbitrary",)),
    )(a, w_down, pre1, g1, b1, g2, b2, tgt)


def _ffn_down_bwd(dp2b, w_down, s1, s2):
    S = dp2b.shape[0]
    tm = min(S, 512)

    def body(d_ref, w_ref, s1_ref, s2_ref, dg_ref, du_ref):
        da = _dot_nt(d_ref[...], w_ref[...])
        du_ref[...] = (da * s1_ref[...].astype(F32)).astype(du_ref.dtype)
        dg_ref[...] = (da * s2_ref[...].astype(F32)).astype(dg_ref.dtype)

    chunk = pl.BlockSpec((tm, FF_CHUNK), lambda j, i: (i, j))
    return pl.pallas_call(
        body, name="ffn_down_bwd", grid=(2, S // tm),
        in_specs=[pl.BlockSpec((tm, D_MODEL), lambda j, i: (i, 0)),
                  pl.BlockSpec((FF_CHUNK, D_MODEL), lambda j, i: (j, 0)), chunk, chunk],
        out_specs=[chunk, chunk],
        out_shape=[_sds((S, D_FF), _MXU), _sds((S, D_FF), _MXU)],
        compiler_params=_cp(("arbitrary", "arbitrary")),
    )(dp2b, w_down, s1, s2)


def _ffn_up_bwd(dgate, dup, wgu, dp2, pre1, g1):
    S = dgate.shape[0]
    tm = min(S, 256)

    def body(dg_ref, du_ref, w_ref, d2_ref, p_ref, g_ref, d1_ref, d1b_ref, dg1_ref, db1_ref):
        @pl.when(pl.program_id(0) == 0)
        def _():
            dg1_ref[...] = jnp.zeros_like(dg1_ref)
            db1_ref[...] = jnp.zeros_like(db1_ref)

        dh1 = ALPHA * d2_ref[...]
        for j in range(2):
            cols = slice(j * FF_CHUNK, (j + 1) * FF_CHUNK)
            dh1 += _dot_nt(dg_ref[:, cols], w_ref[j])
            dh1 += _dot_nt(du_ref[:, cols], w_ref[j + 2])
        xhat, rstd = _ln_hat(p_ref[...])
        dp1 = _ln_bwd(xhat, rstd, dh1, g_ref[...])
        d1_ref[...] = dp1
        d1b_ref[...] = dp1.astype(d1b_ref.dtype)
        dg1_ref[...] += _colsum(dh1 * xhat)
        db1_ref[...] += _colsum(dh1)

    vec = _fixed(1, D_MODEL)
    return pl.pallas_call(
        body, name="ffn_up_bwd", grid=(S // tm,),
        in_specs=[_rows(tm, D_FF), _rows(tm, D_FF), _fixed(4, D_MODEL, FF_CHUNK), _rows(tm, D_MODEL),
                  _rows(tm, D_MODEL), vec],
        out_specs=[_rows(tm, D_MODEL), _rows(tm, D_MODEL), vec, vec],
        out_shape=[_sds((S, D_MODEL), F32), _sds((S, D_MODEL), _MXU), _sds((1, D_MODEL), F32), _sds((1, D_MODEL), F32)],
        compiler_params=_cp(("arbitrary",), vmem_mb=56),
    )(dgate, dup, wgu, dp2, pre1, g1)


def _rms_bwd(x, g, dy):
    n = x.shape[-1]
    r = lax.rsqrt(jnp.mean(x * x, axis=-1, keepdims=True) + RMS_EPS)
    u = dy * g
    dx = r * u - x * (r * r * r) * (jnp.sum(u * x, axis=-1, keepdims=True) * (1.0 / n))
    return dx, _colsum(dy * x * r)


def _mix_bwd(dp1b, w_out, sb, sw, gsb, gsw):
    S = sb.shape[0]
    tm = min(S, 512)

    def body(d_ref, w_ref, sb_ref, sw_ref, gsb_ref, gsw_ref, dsb_ref, dsw_ref, dgsb_ref, dgsw_ref):
        @pl.when(pl.program_id(0) == 0)
        def _():
            dgsb_ref[...] = jnp.zeros_like(dgsb_ref)
            dgsw_ref[...] = jnp.zeros_like(dgsw_ref)

        dm = _dot_nt(d_ref[...], w_ref[...])
        dsb, dgsb = _rms_bwd(sb_ref[...], gsb_ref[...], dm[:, :SB_WIDTH])
        dsw, dgsw = _rms_bwd(sw_ref[...], gsw_ref[...], dm[:, SB_WIDTH:])
        dsb_ref[...] = dsb.T.astype(dsb_ref.dtype).reshape(dsb_ref.shape)
        dsw_ref[...] = dsw.astype(dsw_ref.dtype)
        dgsb_ref[...] += dgsb
        dgsw_ref[...] += dgsw

    return pl.pallas_call(
        body, name="mix_bwd", grid=(S // tm,),
        in_specs=[_rows(tm, D_MODEL), _fixed(D_MODEL, D_MODEL), _rows(tm, SB_WIDTH), _rows(tm, SWA_WIDTH),
                  _fixed(1, SB_WIDTH), _fixed(1, SWA_WIDTH)],
        out_specs=[pl.BlockSpec((SB_HEADS, HEAD_DIM, tm), lambda i: (0, 0, i)), _rows(tm, SWA_WIDTH),
                   _fixed(1, SB_WIDTH), _fixed(1, SWA_WIDTH)],
        out_shape=[_sds((SB_HEADS, HEAD_DIM, S), _MXU), _sds((S, SWA_WIDTH), _MXU), _sds((1, SB_WIDTH), F32),
                   _sds((1, SWA_WIDTH), F32)],
        compiler_params=_cp(("arbitrary",)),
    )(dp1b, w_out, sb, sw, gsb, gsw)


def _in_proj_bwd(dproj, w_in, dp1, x, g, parts):
    S = x.shape[0]
    tm = min(S, 512)
    nw = len(parts)
    ns = len(IN_SECTIONS)
    offs = np.cumsum((0,) + IN_SECTIONS)
    s_ins, s_outs, s_sems = _scatter_io(parts)

    def body(*refs):
        dpj_refs = refs[:ns]
        w_ref, d1_ref, x_ref, g_ref = refs[ns:ns + 4]
        rest = refs[ns + 4:]
        gx_ref, dg_ref, db_ref = rest[nw:nw + 3]
        scatter = _Scatter(rest[:nw], rest[nw + 3:2 * nw + 3], *rest[2 * nw + 3:])

        @pl.when(pl.program_id(0) == 0)
        def _():
            scatter.start()
            dg_ref[...] = jnp.zeros_like(dg_ref)
            db_ref[...] = jnp.zeros_like(db_ref)

        dh0 = ALPHA * d1_ref[...]
        for k in range(ns):
            dh0 += _dot_nt(dpj_refs[k][...], w_ref[:, offs[k]:offs[k + 1]])
        xhat, rstd = _ln_hat(x_ref[...])
        gx_ref[...] = _ln_bwd(xhat, rstd, dh0, g_ref[...])
        dg_ref[...] += _colsum(dh0 * xhat)
        db_ref[...] += _colsum(dh0)

        @pl.when(pl.program_id(0) == pl.num_programs(0) - 1)
        def _():
            scatter.finish()

    vec = _fixed(1, D_MODEL)
    any_spec = pl.BlockSpec(memory_space=pl.ANY)
    res = pl.pallas_call(
        body, name="in_proj_bwd", grid=(S // tm,),
        in_specs=[_rows(tm, n) for n in IN_SECTIONS]
                 + [_fixed(D_MODEL, IN_COLS), _rows(tm, D_MODEL), _rows(tm, D_MODEL), vec] + [any_spec] * nw,
        out_specs=[_rows(tm, D_MODEL), vec, vec] + [any_spec] * nw,
        out_shape=[_sds((S, D_MODEL), F32), _sds((1, D_MODEL), F32), _sds((1, D_MODEL), F32)] + s_outs,
        scratch_shapes=s_sems,
        compiler_params=_cp(("arbitrary",)),
    )(*dproj, w_in, dp1, x, g, *s_ins)
    return res[0], res[1], res[2], list(res[3:])


def _matmul_tn(a, b, name, tk, tn):
    T, K = a.shape
    N = b.shape[1]
    tt = min(T, 1024)

    def body(a_ref, b_ref, o_ref):
        @pl.when(pl.program_id(2) == 0)
        def _():
            o_ref[...] = jnp.zeros_like(o_ref)

        o_ref[...] += _dot_tn(a_ref[...], b_ref[...])

    return pl.pallas_call(
        body, name=name, grid=(K // tk, N // tn, T // tt),
        in_specs=[pl.BlockSpec((tt, tk), lambda k, n, t: (t, k)), pl.BlockSpec((tt, tn), lambda k, n, t: (t, n))],
        out_specs=pl.BlockSpec((tk, tn), lambda k, n, t: (k, n)),
        out_shape=_sds((K, N), F32),
        compiler_params=_cp(("parallel", "parallel", "arbitrary")),
    )(a, b)


def _place():
    x, y, c = lax.axis_index("x"), lax.axis_index("y"), lax.axis_index("c")
    chips = [(1 - x, y), (x, 1 - y), (1 - x, 1 - y)]
    return x, y, c, chips


class _Gather:
    def __init__(self, in_refs, out_refs, send_sems, recv_sems):
        self.in_refs, self.out_refs, self.send_sems, self.recv_sems = in_refs, out_refs, send_sems, recv_sems
        self.x, self.y, self.c, self.chips = _place()

    def _copy(self, w, k, chip, hc, to, src=None):
        part = self.out_refs[w].at[2 * chip[0] + chip[1], hc]
        return pltpu.make_async_remote_copy(
            src_ref=part if src is None else src, dst_ref=part, send_sem=self.send_sems.at[w, k],
            recv_sem=self.recv_sems.at[w, k], device_id=to, device_id_type=MESH)

    def _first(self):
        x, y, c = self.x, self.y, self.c
        return [self._copy(w, j, (x, y), c, (*chip, c), src=self.in_refs[w].at[c])
                for w in range(len(self.in_refs)) for j, chip in enumerate(self.chips)]

    def start(self):
        for cp in self._first():
            cp.start()

    def finish(self):
        x, y, c = self.x, self.y, self.c
        me, sibling = (x, y, c), (x, y, 1 - c)
        ws = range(len(self.in_refs))
        passed = []
        for w in ws:
            for j, chip in enumerate(self.chips):
                self._copy(w, j, chip, c, me).wait_recv()
                passed.append(self._copy(w, 3 + j, chip, c, sibling))
                passed[-1].start()
        for w in ws:
            for j, chip in enumerate(self.chips):
                self._copy(w, 3 + j, chip, 1 - c, me).wait_recv()
        for cp in self._first() + passed:
            cp.wait_send()


def _gather_io(shards):
    halves = [(s.shape[0] // 2, s.shape[1]) for s in shards]
    ins = [s.reshape(2, h, cols) for s, (h, cols) in zip(shards, halves)]
    outs = [_sds((N_CHIPS, 2, h, cols), s.dtype) for s, (h, cols) in zip(shards, halves)]
    sems = [pltpu.SemaphoreType.DMA((len(shards), 6)), pltpu.SemaphoreType.DMA((len(shards), 6))]
    return ins, outs, sems


def _gather_assemble(outs, shards):
    me = 2 * lax.axis_index("x") + lax.axis_index("y")
    return [lax.dynamic_update_slice_in_dim(o.reshape((N_CHIPS,) + s.shape), s[None], me, axis=0)
            for o, s in zip(outs, shards)]


class _Scatter:
    def __init__(self, p_refs, out_refs, send_sems, recv_sems):
        self.p_refs, self.out_refs, self.send_sems, self.recv_sems = p_refs, out_refs, send_sems, recv_sems
        self.x, self.y, self.c, self.chips = _place()
        self.me = 2 * self.x + self.y

    def _copy(self, w, j, chip, src_chip, dst_chip):
        return pltpu.make_async_remote_copy(
            src_ref=self.p_refs[w].at[src_chip], dst_ref=self.out_refs[w].at[dst_chip], send_sem=self.send_sems.at[w, j],
            recv_sem=self.recv_sems.at[w, j], device_id=(*chip, self.c), device_id_type=MESH)

    def _sends(self):
        return [self._copy(w, j, chip, 2 * chip[0] + chip[1], self.me)
                for w in range(len(self.p_refs)) for j, chip in enumerate(self.chips)]

    def start(self):
        for cp in self._sends():
            cp.start()

    def finish(self):
        for w in range(len(self.p_refs)):
            for j, chip in enumerate(self.chips):
                self._copy(w, j, chip, self.me, 2 * chip[0] + chip[1]).wait_recv()
        for cp in self._sends():
            cp.wait_send()


def _scatter_io(parts):
    sems = [pltpu.SemaphoreType.DMA((len(parts), 3)), pltpu.SemaphoreType.DMA((len(parts), 3))]
    return list(parts), [_sds(p.shape, p.dtype) for p in parts], sems


class _Swap:
    def __init__(self, g_refs, out_refs, send_sems, recv_sems):
        x, y, c, _ = _place()
        self.copies = []
        for w in range(len(g_refs)):
            half = out_refs[w].shape[1]
            theirs = g_refs[w].at[:, pl.ds(pl.multiple_of((1 - c) * half, 8), half), :]
            self.copies.append(pltpu.make_async_remote_copy(
                src_ref=theirs, dst_ref=out_refs[w], send_sem=send_sems.at[w], recv_sem=recv_sems.at[w],
                device_id=(x, y, 1 - c), device_id_type=MESH))

    def start(self):
        for cp in self.copies:
            cp.start()

    def finish(self):
        for cp in self.copies:
            cp.wait()


def _swap_io(grads):
    outs = [_sds((g.shape[0], g.shape[1] // 2, g.shape[2]), g.dtype) for g in grads]
    return list(grads), outs, [pltpu.SemaphoreType.DMA((len(grads),)), pltpu.SemaphoreType.DMA((len(grads),))]


def _matmul_tn_pair(a, b0, b1, name):
    T, K = a.shape
    tt = min(T, 1024)

    def body(a_ref, b0_ref, b1_ref, o_ref):
        n = pl.program_id(0)

        @pl.when(pl.program_id(1) == 0)
        def _():
            o_ref[...] = jnp.zeros_like(o_ref)

        @pl.when(n < 2)
        def _():
            o_ref[0] += _dot_tn(a_ref[...], b0_ref[...])

        @pl.when(n >= 2)
        def _():
            o_ref[0] += _dot_tn(a_ref[...], b1_ref[...])

    return pl.pallas_call(
        body, name=name, grid=(4, T // tt),
        in_specs=[pl.BlockSpec((tt, K), lambda n, t: (t, 0)),
                  pl.BlockSpec((tt, FF_CHUNK), lambda n, t: (t, jnp.minimum(n, 1))),
                  pl.BlockSpec((tt, FF_CHUNK), lambda n, t: (t, jnp.maximum(n - 2, 0)))],
        out_specs=pl.BlockSpec((1, K, FF_CHUNK), lambda n, t: (n, 0, 0)),
        out_shape=_sds((4, K, FF_CHUNK), F32),
        compiler_params=_cp(("parallel", "arbitrary")),
    )(a, b0, b1)


def _sb_logs(zt, causal):
    e = jnp.exp(-jnp.abs(zt))
    lb = jnp.minimum(zt, 0.0) - jnp.log(1.0 + e)
    l1m = lb - zt
    if causal is not None:
        l1m = jnp.where(causal, l1m, 0.0)
    return lb, l1m


def _sb_weights(lb, suf, causal):
    a = jnp.exp(lb + suf)
    if causal is not None:
        a = jnp.where(causal, a, 0.0)
    return a


def _tri_masks(t):
    r = lax.broadcasted_iota(jnp.int32, (t, t), 0)
    c = lax.broadcasted_iota(jnp.int32, (t, t), 1)
    return r, c


def _sb_fwd(qT, kb, vTb, shards):
    Hh, _, S = qT.shape
    nk, T = kb.shape[1], kb.shape[2]
    nq = S // T
    G = SB_GROUP_FWD
    nw = len(shards)
    g_ins, g_outs, g_sems = _gather_io(shards)

    def body(qT_ref, k_ref, vT_ref, *rest):
        o_ref, rs_ref = rest[nw:nw + 2]
        gather = _Gather(rest[:nw], rest[nw + 2:2 * nw + 2], *rest[2 * nw + 2:])
        i = pl.program_id(1)
        first_step = jnp.logical_and(pl.program_id(0) == 0, i == 0)
        last_step = jnp.logical_and(pl.program_id(0) == pl.num_programs(0) - 1, i == pl.num_programs(1) - 1)

        @pl.when(first_step)
        def _():
            gather.start()

        qts = [(qT_ref[g].astype(F32) * SCALE).astype(_MXU) for g in range(G)]
        r, c = _tri_masks(T)
        upper = (c > r).astype(_MXU)
        causal = r < c

        def blk(j, carry, mask):
            hs = range(G)
            for g in hs:
                rs_ref[g, 0, j] = jnp.broadcast_to(carry[g][0], (8, T))
            zs = [_dot(k_ref[g, j], qts[g]) for g in hs]
            lbs, l1ms = zip(*[_sb_logs(zs[g], mask) for g in hs])
            splits = [_split2(l1ms[g]) for g in hs]
            cums = [_dot(upper, splits[g][0]) + _dot(upper, splits[g][1]) for g in hs]
            avs = [_sb_weights(lbs[g], carry[g][0] + cums[g], mask).astype(_MXU) for g in hs]
            accs = [carry[g][1] + _dot(vT_ref[g, j], avs[g]) for g in hs]
            return tuple((carry[g][0] + _colsum(l1ms[g]), accs[g]) for g in hs)

        def go_on(j, carry):
            top = carry[0][0]
            for g in range(1, G):
                top = jnp.maximum(top, carry[g][0])
            return jnp.logical_and(j >= 0, jnp.max(top) >= SB_DEAD)

        init = tuple((jnp.zeros((1, T), F32), jnp.zeros((HEAD_DIM, T), F32)) for _ in range(G))
        carry = blk(i, init, causal)
        j, carry = lax.while_loop(lambda st: go_on(*st), lambda st: (st[0] - 1, blk(st[0], st[1], None)),
                                  (i - 1, carry))

        @pl.when(j >= 0)
        def _():
            for g in range(G):
                rs_ref[g, 0, j] = jnp.broadcast_to(carry[g][0], (8, T))

        o_ref[...] = jnp.concatenate([carry[g][1] for g in range(G)], axis=0).T

        @pl.when(last_step)
        def _():
            gather.finish()

    any_spec = pl.BlockSpec(memory_space=pl.ANY)
    res = pl.pallas_call(
        body, name="sb_fwd", grid=(Hh // G, nq),
        in_specs=[pl.BlockSpec((G, HEAD_DIM, T), lambda h, i: (h, 0, i)),
                  pl.BlockSpec((G, nk, T, HEAD_DIM), lambda h, i: (h, 0, 0, 0), pipeline_mode=pl.Buffered(1)),
                  pl.BlockSpec((G, nk, HEAD_DIM, T), lambda h, i: (h, 0, 0, 0), pipeline_mode=pl.Buffered(1))]
                 + [any_spec] * nw,
        out_specs=[pl.BlockSpec((T, G * HEAD_DIM), lambda h, i: (i, h)),
                   pl.BlockSpec((G, 1, nk, 8, T), lambda h, i: (h, i, 0, 0, 0))] + [any_spec] * nw,
        out_shape=[_sds((S, Hh * HEAD_DIM), F32), _sds((Hh, nq, nk, 8, T), F32)] + g_outs,
        scratch_shapes=g_sems,
        compiler_params=_cp(("arbitrary", "arbitrary")),
    )(qT, kb, vTb, *g_ins)
    return res[0], res[1], _gather_assemble(res[2:], shards)


def _sb_bwd(qT, kb, kTb, vb, doT, rsave, parts):
    Hh, _, S = qT.shape
    nk, T = kb.shape[1], kb.shape[2]
    nq = S // T
    G = SB_GROUP_BWD
    nw = len(parts)
    s_ins, s_outs, s_sems = _scatter_io(parts)

    def body(qT_ref, k_ref, kT_ref, v_ref, doT_ref, rs_ref, *rest):
        dq_ref, dk_out_ref, dv_out_ref = rest[nw:nw + 3]
        dk_ref, dv_ref = rest[2 * nw + 3:2 * nw + 5]
        scatter = _Scatter(rest[:nw], rest[nw + 3:2 * nw + 3], *rest[2 * nw + 5:])
        i = pl.program_id(1)
        first_step = jnp.logical_and(pl.program_id(0) == 0, i == 0)
        last_step = jnp.logical_and(pl.program_id(0) == pl.num_programs(0) - 1, i == pl.num_programs(1) - 1)

        @pl.when(first_step)
        def _():
            scatter.start()

        @pl.when(i == 0)
        def _():
            dk_ref[...] = jnp.zeros_like(dk_ref)
            dv_ref[...] = jnp.zeros_like(dv_ref)

        qts = [(qT_ref[g].astype(F32) * SCALE).astype(_MXU) for g in range(G)]
        douts = [doT_ref[g] for g in range(G)]
        r, c = _tri_masks(T)
        upper = (c > r).astype(_MXU)
        lower = (c < r).astype(_MXU)
        causal = r < c

        def blk(j, carry, mask):
            hs = range(G)
            zs = [_dot(k_ref[g, j], qts[g]) for g in hs]
            das = [_dot(v_ref[g, j], douts[g]) for g in hs]
            lbs, l1ms = zip(*[_sb_logs(zs[g], mask) for g in hs])
            splits = [_split2(l1ms[g]) for g in hs]
            cums = [_dot(upper, splits[g][0]) + _dot(upper, splits[g][1]) for g in hs]
            avs = [_sb_weights(lbs[g], rs_ref[g, 0, j][0:1, :] + cums[g], mask) for g in hs]
            ets = [das[g] * avs[g] for g in hs]
            esplits = [_split2(ets[g]) for g in hs]
            ecums = [_dot(lower, esplits[g][0]) + _dot(lower, esplits[g][1]) for g in hs]
            dzs = []
            for g in hs:
                sig = jnp.exp(lbs[g])
                dz = ets[g] * (1.0 - sig) - (carry[g][0] + ecums[g]) * sig
                if mask is not None:
                    dz = jnp.where(mask, dz, 0.0)
                dzs.append(dz.astype(_MXU))
            dqs = [carry[g][1] + _dot(kT_ref[g, j], dzs[g]) for g in hs]
            for g in hs:
                dk_ref[j, g * HEAD_DIM:(g + 1) * HEAD_DIM, :] += _dot_nt(qts[g], dzs[g])
            for g in hs:
                dv_ref[j, g * HEAD_DIM:(g + 1) * HEAD_DIM, :] += _dot_nt(douts[g], avs[g].astype(_MXU))
            return tuple((carry[g][0] + _colsum(ets[g]), dqs[g]) for g in hs)

        def live(j):
            jj = jnp.maximum(j, 0)
            top = rs_ref[0, 0, jj][0:1, :]
            for g in range(1, G):
                top = jnp.maximum(top, rs_ref[g, 0, jj][0:1, :])
            return jnp.logical_and(j >= 0, jnp.max(top) >= SB_DEAD)

        first = lax.while_loop(lambda st: st[1], lambda st: (st[0] - 1, live(st[0] - 2)), (i, live(i - 1)))[0]
        carry = tuple((jnp.zeros((1, T), F32), jnp.zeros((HEAD_DIM, T), F32)) for _ in range(G))
        carry = lax.fori_loop(first, i, lambda s, cr: blk(s, cr, None), carry)
        carry = blk(i, carry, causal)
        dq_ref[...] = (jnp.concatenate([carry[g][1] for g in range(G)], axis=0) * SCALE).T.astype(dq_ref.dtype)

        @pl.when(i == pl.num_programs(1) - 1)
        def _():
            def flush(j, _):
                rows = pl.ds(pl.multiple_of(j * T, T), T)
                dk_out_ref[rows, :] = dk_ref[j].T.astype(dk_out_ref.dtype)
                dv_out_ref[rows, :] = dv_ref[j].T.astype(dv_out_ref.dtype)
                return 0
            lax.fori_loop(0, nk, flush, 0)

        @pl.when(last_step)
        def _():
            scatter.finish()

    colblk = pl.BlockSpec((G, HEAD_DIM, T), lambda h, i: (h, 0, i))
    once = pl.Buffered(1)
    kblk = pl.BlockSpec((G, nk, T, HEAD_DIM), lambda h, i: (h, 0, 0, 0), pipeline_mode=once)
    kTblk = pl.BlockSpec((G, nk, HEAD_DIM, T), lambda h, i: (h, 0, 0, 0), pipeline_mode=once)
    any_spec = pl.BlockSpec(memory_space=pl.ANY)
    res = pl.pallas_call(
        body, name="sb_bwd", grid=(Hh // G, nq),
        in_specs=[colblk, kblk, kTblk, kblk, colblk,
                  pl.BlockSpec((G, 1, nk, 8, T), lambda h, i: (h, i, 0, 0, 0))] + [any_spec] * nw,
        out_specs=[pl.BlockSpec((T, G * HEAD_DIM), lambda h, i: (i, h)),
                   pl.BlockSpec((S, G * HEAD_DIM), lambda h, i: (0, h), pipeline_mode=once),
                   pl.BlockSpec((S, G * HEAD_DIM), lambda h, i: (0, h), pipeline_mode=once)] + [any_spec] * nw,
        out_shape=[_sds((S, Hh * HEAD_DIM), _MXU)] * 3 + s_outs,
        scratch_shapes=[pltpu.VMEM((nk, G * HEAD_DIM, T), F32), pltpu.VMEM((nk, G * HEAD_DIM, T), F32)] + s_sems,
        compiler_params=_cp(("arbitrary", "arbitrary"), vmem_mb=60),
    )(qT, kb, kTb, vb, doT, rsave, *s_ins)
    return res[0], res[1], res[2], list(res[3:])


def _bucket_table():
    qi = np.arange(BLOCK)[:, None]
    cj = np.arange(2 * BLOCK)[None, :]
    dist = qi + BLOCK - cj
    exact = REL_BUCKETS // 2
    d = np.maximum(dist, 0)
    d_f = np.maximum(d, 1).astype(np.float32)
    large = exact + (np.log(d_f / np.float32(exact)) / np.float32(math.log(REL_MAX_DIST / exact))
                     * np.float32(REL_BUCKETS - exact)).astype(np.int32)
    large = np.minimum(large, REL_BUCKETS - 1)
    return np.where(d < exact, d, large).astype(np.int32)


def _swa_bias(rel_bias, bucket):
    def body(rb_ref, bk_ref, o_ref):
        bk = bk_ref[...]
        for h in range(SWA_HEADS):
            t = jnp.zeros((2 * BLOCK, BLOCK), F32)
            for b in range(REL_BUCKETS):
                t = jnp.where(bk == b, rb_ref[b, h], t)
            o_ref[h] = t

    return pl.pallas_call(
        body, name="swa_bias",
        in_specs=[pl.BlockSpec(memory_space=pltpu.SMEM), pl.BlockSpec(memory_space=pltpu.VMEM)],
        out_specs=pl.BlockSpec(memory_space=pltpu.VMEM),
        out_shape=_sds((SWA_HEADS, 2 * BLOCK, BLOCK), F32),
    )(rel_bias, bucket)


def _swa_logits(q, kp, kc):
    qs = (q.astype(F32) * SCALE).astype(_MXU)
    return qs, _dot_nt(kp, qs), _dot_nt(kc, qs)


def _swa_softmax(lp, lc, bias, sink, live_prev):
    r, c = _tri_masks(BLOCK)
    in_window = r > c if live_prev is None else jnp.logical_and(r > c, live_prev)
    lp = jnp.where(in_window, lp + bias[:BLOCK, :], -jnp.inf)
    lc = jnp.where(r <= c, lc + bias[BLOCK:, :], -jnp.inf)
    m = jnp.maximum(jnp.maximum(jnp.max(lp, axis=0, keepdims=True), jnp.max(lc, axis=0, keepdims=True)), sink)
    pp = jnp.exp(lp - m)
    pc = jnp.exp(lc - m)
    ps = jnp.exp(sink - m)
    denom = _colsum(pp) + _colsum(pc) + ps
    return pp / denom, pc / denom, ps / denom


def _swa_sub(nb):
    return min(SWA_SUB, nb)


def _swa_keys(b, prev_ref, cur_ref, i):
    cur = cur_ref[0, b * BLOCK:(b + 1) * BLOCK, :]
    if b == 0:
        return prev_ref[0], cur, i > 0
    return cur_ref[0, (b - 1) * BLOCK:b * BLOCK, :], cur, None


def _swa_keys_t(b, prev_ref, cur_ref):
    cur = cur_ref[0, :, b * BLOCK:(b + 1) * BLOCK]
    return (prev_ref[0] if b == 0 else cur_ref[0, :, (b - 1) * BLOCK:b * BLOCK]), cur


SWA_PAIR = 4


def _swa_fwd(q, k, vT, bias, sink):
    S = q.shape[1]
    nb = S // BLOCK
    ns = _swa_sub(nb)
    R = ns * BLOCK
    P = SWA_PAIR

    def body(q_ref, kp_ref, kc_ref, vp_ref, vc_ref, bias_ref, sink_ref, o_ref):
        i = pl.program_id(1)
        units = [(hh, b) for hh in range(P) for b in range(ns)]
        keys = [_swa_keys(b, kp_ref, kc_ref, i) for b in range(ns)]
        vals = [_swa_keys_t(b, vp_ref, vc_ref) for b in range(ns)]
        logits = {u: _swa_logits(q_ref[u[0], u[1] * BLOCK:(u[1] + 1) * BLOCK, :], keys[u[1]][0], keys[u[1]][1])
                  for u in units}
        ws = {u: _swa_softmax(logits[u][1], logits[u][2], bias_ref[u[0]], sink_ref[u[0]][:, :1], keys[u[1]][2])
              for u in units}
        outs = {u: _dot(vals[u[1]][0], ws[u][0].astype(_MXU)) + _dot(vals[u[1]][1], ws[u][1].astype(_MXU))
                for u in units}
        for b in range(ns):
            o_ref[b * BLOCK:(b + 1) * BLOCK, :] = jnp.concatenate([outs[(hh, b)] for hh in range(P)], axis=0).T

    kvh = lambda p: (p * P) // SWA_GROUP
    prev = pl.BlockSpec((1, BLOCK, HEAD_DIM), lambda p, i: (kvh(p), jnp.maximum(i * ns - 1, 0), 0))
    cur = pl.BlockSpec((1, R, HEAD_DIM), lambda p, i: (kvh(p), i, 0))
    prev_t = pl.BlockSpec((1, HEAD_DIM, BLOCK), lambda p, i: (kvh(p), 0, jnp.maximum(i * ns - 1, 0)))
    cur_t = pl.BlockSpec((1, HEAD_DIM, R), lambda p, i: (kvh(p), 0, i))
    return pl.pallas_call(
        body, name="swa_fwd", grid=(SWA_HEADS // P, nb // ns),
        in_specs=[pl.BlockSpec((P, R, HEAD_DIM), lambda p, i: (p, i, 0)), prev, cur, prev_t, cur_t,
                  pl.BlockSpec((P, 2 * BLOCK, BLOCK), lambda p, i: (p, 0, 0)),
                  pl.BlockSpec((P, 1, BLOCK), lambda p, i: (p, 0, 0))],
        out_specs=pl.BlockSpec((R, P * HEAD_DIM), lambda p, i: (i, p)),
        out_shape=_sds((S, SWA_HEADS * HEAD_DIM), F32),
        compiler_params=_cp(("parallel", "parallel")),
    )(q, k, k, vT, vT, bias, sink)


def _swa_bwd(q, k, kT, v, bias, sink, do, grads):
    S = q.shape[1]
    nb = S // BLOCK
    ns = _swa_sub(nb)
    R = ns * BLOCK
    P = SWA_PAIR
    nw = len(grads)
    x_ins, x_outs, x_sems = _swap_io(grads)

    def body(q_ref, kp_ref, kc_ref, ktp_ref, ktc_ref, vp_ref, vc_ref, bias_ref, sink_ref, do_ref, *rest):
        dq_ref, dk_ref, dv_ref, dbias_ref, dsink_ref = rest[nw:nw + 5]
        swap = _Swap(rest[:nw], rest[nw + 5:2 * nw + 5], *rest[2 * nw + 5:])
        g = pl.program_id(1)
        i = pl.program_id(2)
        first_step = jnp.logical_and(pl.program_id(0) == 0, jnp.logical_and(g == 0, i == 0))
        last_step = jnp.logical_and(pl.program_id(0) == pl.num_programs(0) - 1,
                                    jnp.logical_and(g == pl.num_programs(1) - 1, i == pl.num_programs(2) - 1))

        @pl.when(first_step)
        def _():
            swap.start()

        @pl.when(jnp.logical_and(g == 0, i == 0))
        def _():
            dk_ref[...] = jnp.zeros_like(dk_ref)
            dv_ref[...] = jnp.zeros_like(dv_ref)

        @pl.when(i == 0)
        def _():
            dbias_ref[...] = jnp.zeros_like(dbias_ref)
            dsink_ref[...] = jnp.zeros_like(dsink_ref)

        subs = range(ns)
        units = [(hh, b) for hh in range(P) for b in subs]
        rows = [slice(b * BLOCK, (b + 1) * BLOCK) for b in subs]
        keys = [_swa_keys(b, kp_ref, kc_ref, i) for b in subs]
        keys_t = [_swa_keys_t(b, ktp_ref, ktc_ref) for b in subs]
        vals = [_swa_keys(b, vp_ref, vc_ref, i) for b in subs]
        douts = {u: do_ref[u[0], rows[u[1]], :] for u in units}
        logits = {u: _swa_logits(q_ref[u[0], rows[u[1]], :], keys[u[1]][0], keys[u[1]][1]) for u in units}
        dws = {u: (_dot_nt(vals[u[1]][0], douts[u]), _dot_nt(vals[u[1]][1], douts[u])) for u in units}
        wts, dls = {}, {}
        for hh in range(P):
            dbp = jnp.zeros((BLOCK, BLOCK), F32)
            dbc = jnp.zeros((BLOCK, BLOCK), F32)
            dsk = jnp.zeros((1, BLOCK), F32)
            for b in subs:
                u = (hh, b)
                wp, wc, ws = _swa_softmax(logits[u][1], logits[u][2], bias_ref[hh], sink_ref[hh][:, :1], keys[b][2])
                dwp, dwc = dws[u]
                delta = _colsum(wp * dwp) + _colsum(wc * dwc)
                dlp = wp * (dwp - delta)
                dlc = wc * (dwc - delta)
                dbp += dlp
                dbc += dlc
                dsk -= ws * delta
                wts[u] = (wp.astype(_MXU), wc.astype(_MXU))
                dls[u] = (dlp.astype(_MXU), dlc.astype(_MXU))
            dbias_ref[hh, :BLOCK, :] += dbp
            dbias_ref[hh, BLOCK:, :] += dbc
            dsink_ref[hh] += jnp.broadcast_to(dsk, (8, BLOCK))
        dqs = {u: (_dot(keys_t[u[1]][0], dls[u][0]) + _dot(keys_t[u[1]][1], dls[u][1])) * SCALE for u in units}
        for b in subs:
            dq_ref[rows[b], :] = jnp.concatenate([dqs[(hh, b)] for hh in range(P)], axis=0).T.astype(dq_ref.dtype)
        for b in subs:
            blk = i * ns + b
            dk_cur = sum(_dot(dls[(hh, b)][1], logits[(hh, b)][0]) for hh in range(P))
            dv_cur = sum(_dot(wts[(hh, b)][1], douts[(hh, b)]) for hh in range(P))
            dk_prev = sum(_dot(dls[(hh, b)][0], logits[(hh, b)][0]) for hh in range(P))
            dv_prev = sum(_dot(wts[(hh, b)][0], douts[(hh, b)]) for hh in range(P))
            dk_ref[0, blk] += dk_cur
            dv_ref[0, blk] += dv_cur
            if b == 0:
                @pl.when(i > 0)
                def _():
                    dk_ref[0, blk - 1] += dk_prev
                    dv_ref[0, blk - 1] += dv_prev
            else:
                dk_ref[0, blk - 1] += dk_prev
                dv_ref[0, blk - 1] += dv_prev

        @pl.when(last_step)
        def _():
            swap.finish()

    G2 = SWA_GROUP // P
    hp = lambda kv, g, i: kv * G2 + g
    prev = pl.BlockSpec((1, BLOCK, HEAD_DIM), lambda kv, g, i: (kv, jnp.maximum(i * ns - 1, 0), 0))
    cur = pl.BlockSpec((1, R, HEAD_DIM), lambda kv, g, i: (kv, i, 0))
    prev_t = pl.BlockSpec((1, HEAD_DIM, BLOCK), lambda kv, g, i: (kv, 0, jnp.maximum(i * ns - 1, 0)))
    cur_t = pl.BlockSpec((1, HEAD_DIM, R), lambda kv, g, i: (kv, 0, i))
    qblk = pl.BlockSpec((P, R, HEAD_DIM), lambda kv, g, i: (hp(kv, g, i), i, 0))
    kvacc = pl.BlockSpec((1, nb, BLOCK, HEAD_DIM), lambda kv, g, i: (kv, 0, 0, 0))
    any_spec = pl.BlockSpec(memory_space=pl.ANY)
    res = pl.pallas_call(
        body, name="swa_bwd", grid=(SWA_KV_HEADS, G2, nb // ns),
        in_specs=[qblk, prev, cur, prev_t, cur_t, prev, cur,
                  pl.BlockSpec((P, 2 * BLOCK, BLOCK), lambda kv, g, i: (hp(kv, g, i), 0, 0)),
                  pl.BlockSpec((P, 1, BLOCK), lambda kv, g, i: (hp(kv, g, i), 0, 0)), qblk] + [any_spec] * nw,
        out_specs=[pl.BlockSpec((R, P * HEAD_DIM), lambda kv, g, i: (i, hp(kv, g, i))), kvacc, kvacc,
                   pl.BlockSpec((P, 2 * BLOCK, BLOCK), lambda kv, g, i: (hp(kv, g, i), 0, 0)),
                   pl.BlockSpec((P, 8, BLOCK), lambda kv, g, i: (hp(kv, g, i), 0, 0))] + [any_spec] * nw,
        out_shape=[_sds((S, SWA_HEADS * HEAD_DIM), _MXU), _sds((SWA_KV_HEADS, nb, BLOCK, HEAD_DIM), F32),
                   _sds((SWA_KV_HEADS, nb, BLOCK, HEAD_DIM), F32), _sds((SWA_HEADS, 2 * BLOCK, BLOCK), F32),
                   _sds((SWA_HEADS, 8, BLOCK), F32)] + x_outs,
        scratch_shapes=x_sems,
        compiler_params=_cp(("arbitrary", "arbitrary", "arbitrary")),
    )(q, k, k, kT, kT, v, v, bias, sink, do, *x_ins)
    return res[0], res[1], res[2], res[3], res[4], list(res[5:])


def _swa_small_grads(dbias, dsink, bucket):
    rows = REL_BUCKETS + 8

    def total(x):
        return jnp.sum(jnp.sum(x, axis=1, keepdims=True), axis=0, keepdims=True)

    def body(db_ref, ds_ref, bk_ref, o_ref):
        bk = bk_ref[...]
        r = lax.broadcasted_iota(jnp.int32, (rows, BLOCK), 0)
        c = lax.broadcasted_iota(jnp.int32, (rows, BLOCK), 1)
        out = jnp.zeros((rows, BLOCK), F32)
        for h in range(SWA_HEADS):
            db = db_ref[h]
            for b in range(REL_BUCKETS):
                s = total(jnp.where(bk == b, db, 0.0))
                out = jnp.where(jnp.logical_and(r == b, c == h), s, out)
            s = jnp.sum(ds_ref[h][0:1, :], axis=1, keepdims=True)
            out = jnp.where(jnp.logical_and(r == REL_BUCKETS, c == h), s, out)
        o_ref[...] = out

    vm = pl.BlockSpec(memory_space=pltpu.VMEM)
    return pl.pallas_call(body, name="swa_small_grads", in_specs=[vm, vm, vm], out_specs=vm,
                          out_shape=_sds((rows, BLOCK), F32))(dbias, dsink, bucket)


def _tile_rows(n):
    for t in (512, 352, 256, 176, 128, 64, 32, 16, 8):
        if n % t == 0:
            return t
    return n


def _cast_rows(x, dtype, name):
    R, C = x.shape
    tr = _tile_rows(R)

    def body(x_ref, o_ref):
        o_ref[...] = x_ref[...].astype(o_ref.dtype)

    return pl.pallas_call(body, name=name, grid=(R // tr,), in_specs=[_rows(tr, C)], out_specs=_rows(tr, C),
                          out_shape=_sds((R, C), dtype), compiler_params=_cp(("parallel",)))(x)


def _pair_sum(g, recv, c, name):
    n, half, C = recv.shape
    tr = _tile_rows(half)

    def body(c_ref, a_ref, b_ref, o_ref):
        o_ref[...] = (a_ref[0] + b_ref[...]).astype(o_ref.dtype)

    return pl.pallas_call(
        body, name=name,
        grid_spec=pltpu.PrefetchScalarGridSpec(
            num_scalar_prefetch=1, grid=(n, half // tr),
            in_specs=[pl.BlockSpec((1, 1, tr, C), lambda j, i, c_ref: (j, c_ref[0], i, 0)),
                      pl.BlockSpec((1, tr, C), lambda j, i, c_ref: (j, i, 0))],
            out_specs=pl.BlockSpec((1, tr, C), lambda j, i, c_ref: (j, i, 0))),
        out_shape=_sds((n, half, C), _MXU),
        compiler_params=_cp(("parallel", "parallel")))(c.reshape(1), g.reshape(n, 2, half, C), recv)


def _chip_sum(own, recv, me, name):
    n, R, C = recv.shape
    tr = _tile_rows(R)

    def body(me_ref, own_ref, recv_ref, o_ref):
        acc = None
        for j in range(n):
            term = jnp.where(me_ref[0] == j, own_ref[0], recv_ref[j]).astype(F32)
            acc = term if acc is None else acc + term
        o_ref[...] = acc

    return pl.pallas_call(
        body, name=name,
        grid_spec=pltpu.PrefetchScalarGridSpec(
            num_scalar_prefetch=1, grid=(R // tr,),
            in_specs=[pl.BlockSpec((1, tr, C), lambda i, me_ref: (me_ref[0], i, 0)),
                      pl.BlockSpec((n, tr, C), lambda i, me_ref: (0, i, 0))],
            out_specs=pl.BlockSpec((tr, C), lambda i, me_ref: (i, 0))),
        out_shape=_sds((R, C), F32), compiler_params=_cp(("parallel",)))(me.reshape(1), own, recv)


def _adamw_math(w, g, m, v):
    m = ADAM_B1 * m + (1.0 - ADAM_B1) * g
    v = ADAM_B2 * v + (1.0 - ADAM_B2) * (g * g)
    m_hat = m / (1.0 - ADAM_B1 ** ADAM_STEP)
    v_hat = v / (1.0 - ADAM_B2 ** ADAM_STEP)
    delta = -ADAM_LR * (m_hat / (jnp.sqrt(v_hat) + ADAM_EPS) + ADAM_WD * w)
    return delta, m, v


def _adamw(w, g, m, v, name):
    R, C = w.shape
    tr = _tile_rows(R)

    def body(w_ref, g_ref, m_ref, v_ref, d_ref, nm_ref, nv_ref):
        d, nm, nv = _adamw_math(w_ref[...], g_ref[...], m_ref[...], v_ref[...])
        d_ref[...] = d
        nm_ref[...] = nm
        nv_ref[...] = nv

    blk = _rows(tr, C)
    return pl.pallas_call(body, name=name, grid=(R // tr,), in_specs=[blk] * 4, out_specs=[blk] * 3,
                          out_shape=[_sds((R, C), F32)] * 3, compiler_params=_cp(("parallel",)))(w, g, m, v)


def _gather_weights(shards):
    nw = len(shards)
    ins, outs, sems = _gather_io(shards)

    def body(*refs):
        ex = _Gather(refs[:nw], refs[nw:2 * nw], *refs[2 * nw:])
        ex.start()
        ex.finish()

    any_spec = pl.BlockSpec(memory_space=pl.ANY)
    got = pl.pallas_call(body, name="gather_weights", in_specs=[any_spec] * nw, out_specs=[any_spec] * nw,
                         out_shape=outs, scratch_shapes=sems)(*ins)
    return _gather_assemble(got, shards)


def _swap_halves(grads, name):
    nw = len(grads)
    ins, outs, sems = _swap_io(grads)

    def body(*refs):
        ex = _Swap(refs[:nw], refs[nw:2 * nw], *refs[2 * nw:])
        ex.start()
        ex.finish()

    any_spec = pl.BlockSpec(memory_space=pl.ANY)
    return pl.pallas_call(body, name=name, in_specs=[any_spec] * nw, out_specs=[any_spec] * nw,
                          out_shape=outs, scratch_shapes=sems)(*ins)


def _scatter_partials(parts):
    nw = len(parts)
    ins, outs, sems = _scatter_io(parts)

    def body(*refs):
        ex = _Scatter(refs[:nw], refs[nw:2 * nw], *refs[2 * nw:])
        ex.start()
        ex.finish()

    any_spec = pl.BlockSpec(memory_space=pl.ANY)
    return pl.pallas_call(body, name="scatter_partials", in_specs=[any_spec] * nw, out_specs=[any_spec] * nw,
                          out_shape=outs, scratch_shapes=sems)(*ins)


def _join_halves(sums):
    nw = len(sums)

    def body(*refs):
        f_refs, out_refs = refs[:nw], refs[nw:2 * nw]
        send_sems, recv_sems = refs[2 * nw:]
        x, y, c, _ = _place()
        ws = range(nw)

        def copy(w, half_index):
            return pltpu.make_async_remote_copy(
                src_ref=f_refs[w], dst_ref=out_refs[w].at[half_index], send_sem=send_sems.at[w],
                recv_sem=recv_sems.at[w], device_id=(x, y, 1 - c), device_id_type=MESH)

        sends = [copy(w, c) for w in ws]
        for cp in sends:
            cp.start()
        for w in ws:
            copy(w, 1 - c).wait_recv()
        for cp in sends:
            cp.wait_send()

    any_spec = pl.BlockSpec(memory_space=pl.ANY)
    outs = pl.pallas_call(
        body, name="join_halves", in_specs=[any_spec] * nw, out_specs=[any_spec] * nw,
        out_shape=[_sds((2,) + f.shape, f.dtype) for f in sums],
        scratch_shapes=[pltpu.SemaphoreType.DMA((nw,)), pltpu.SemaphoreType.DMA((nw,))],
    )(*sums)
    c = lax.axis_index("c")
    return [lax.dynamic_update_slice_in_dim(o, f[None], c, axis=0).reshape(2 * f.shape[0], f.shape[1])
            for o, f in zip(outs, sums)]


def _allreduce_small(block):
    m_per, n = block.shape

    def body(x_ref, sum_ref, loss_ref, all_ref, send_sems, recv_sems, local_sem):
        x, y, c, chips = _place()
        me, sibling = (x, y, c), (x, y, 1 - c)

        def rows(px, py, pc):
            return all_ref.at[pl.ds(pl.multiple_of((4 * px + 2 * py + pc) * m_per, 8), m_per), :]

        def copy(k, blk, to, src=None):
            return pltpu.make_async_remote_copy(
                src_ref=rows(*blk) if src is None else src, dst_ref=rows(*blk), send_sem=send_sems.at[k],
                recv_sem=recv_sems.at[k], device_id=to, device_id_type=MESH)

        mine = pltpu.make_async_copy(x_ref, rows(*me), local_sem)
        mine.start()
        first = [copy(0, me, sibling, src=x_ref)]
        first += [copy(1 + j, me, (*chip, c), src=x_ref) for j, chip in enumerate(chips)]
        for cp in first:
            cp.start()
        passed = [copy(4 + j, (*chip, c), sibling) for j, chip in enumerate(chips)]
        for j, chip in enumerate(chips):
            copy(1 + j, (*chip, c), me).wait_recv()
            passed[j].start()
        copy(0, sibling, me).wait_recv()
        for j, chip in enumerate(chips):
            copy(4 + j, (*chip, 1 - c), me).wait_recv()
        for cp in first + passed:
            cp.wait_send()
        mine.wait()

        acc = all_ref[0:m_per, :]
        for d in range(1, 8):
            acc = acc + all_ref[d * m_per:(d + 1) * m_per, :]
        sum_ref[...] = acc
        tot = jnp.sum(acc[8:9, :], axis=1, keepdims=True) * (0.5 / D_MODEL)
        loss_ref[...] = jnp.broadcast_to(tot, loss_ref.shape)

    vm = pl.BlockSpec(memory_space=pltpu.VMEM)
    return pl.pallas_call(
        body, name="allreduce_small", in_specs=[vm], out_specs=[vm, vm],
        out_shape=[_sds((m_per, n), F32), _sds((8, 128), F32)],
        scratch_shapes=[pltpu.VMEM((8 * m_per, n), F32), pltpu.SemaphoreType.DMA((7,)), pltpu.SemaphoreType.DMA((7,)),
                        pltpu.SemaphoreType.DMA],
    )(block)


def _heads_rows(x, nh):
    S = x.shape[0]
    return x.reshape(S, nh, HEAD_DIM).transpose(1, 0, 2)


def _heads_cols(x, nh):
    S = x.shape[0]
    return x.reshape(S, nh, HEAD_DIM).transpose(1, 2, 0)


def _pad_row(v):
    v = v.reshape(1, -1)
    return jnp.pad(v, ((0, 0), (0, D_MODEL - v.shape[1])))


def _pack_small(ln_in_g, ln_in_b, sb_g, swa_g, sinks, rel_bias, ln1_g, ln1_b, ln2_g, ln2_b, extra):
    rows = [_pad_row(ln_in_g), _pad_row(ln_in_b), jnp.concatenate([sb_g.reshape(1, -1), swa_g.reshape(1, -1)], axis=1),
            _pad_row(jnp.concatenate([rel_bias.reshape(1, -1), sinks.reshape(1, -1)], axis=1)),
            _pad_row(ln1_g), _pad_row(ln1_b), _pad_row(ln2_g), _pad_row(ln2_b), _pad_row(extra)]
    rows.append(jnp.zeros((SMALL_ROWS - len(rows), D_MODEL), F32))
    return jnp.concatenate(rows, axis=0)


def _unpack_small(blk):
    nrb = REL_BUCKETS * SWA_HEADS
    return (blk[0], blk[1], blk[2:3, :SB_WIDTH], blk[2:3, SB_WIDTH:], blk[3:4, nrb:nrb + SWA_HEADS],
            blk[3, :nrb].reshape(REL_BUCKETS, SWA_HEADS), blk[4:5], blk[5:6], blk[6:7], blk[7:8])


def kernel(x, ln_in_g, ln_in_b, w_in, sb_norm_g, swa_norm_g, sinks, rel_bias, w_out, ln1_g, ln1_b, w_gate_up, w_down, ln2_g, ln2_b, loss_target, m_ln_in_g, m_ln_in_b, m_w_in, m_sb_norm_g, m_swa_norm_g, m_sinks, m_rel_bias, m_w_out, m_ln1_g, m_ln1_b, m_w_gate_up, m_w_down, m_ln2_g, m_ln2_b, v_ln_in_g, v_ln_in_b, v_w_in, v_sb_norm_g, v_swa_norm_g, v_sinks, v_rel_bias, v_w_out, v_ln1_g, v_ln1_b, v_w_gate_up, v_w_down, v_ln2_g, v_ln2_b):
    S = x.shape[1]
    x2 = x.reshape(S, D_MODEL)
    tgt = loss_target.reshape(S, D_MODEL)
    T = min(S, SB_TILE)
    bucket = jnp.asarray(_bucket_table().T)
    row = lambda v: v.reshape(1, -1)

    shards = [_cast_rows(w[0], _MXU, "cast_" + n) for n, w in (("w_in", w_in), ("w_out", w_out), ("w_gate_up", w_gate_up), ("w_down", w_down))]
    (w_in_sh,) = _gather_weights(shards[:1])
    w_in_f = jnp.concatenate([w_in_sh[j] for j in range(N_CHIPS)], axis=1)

    h0, h0b, _, _, _, q_sw, kv_sw, qT_sb, kTb_sb, vTb_sb, kb_sb, vb_sb = _ln_in_proj(x2, row(ln_in_g), row(ln_in_b), w_in_f)
    k_sw, v_sw = kv_sw[:, :SWA_KV_WIDTH], kv_sw[:, SWA_KV_WIDTH:]
    sb_out, rsave, (w_out_sh, w_gu_sh, w_down_sh) = _sb_fwd(qT_sb, kb_sb, vTb_sb, shards[1:])
    w_out_f = w_out_sh.reshape(D_MODEL, D_MODEL)
    w_down_f = w_down_sh.reshape(D_FF, D_MODEL)

    bias = _swa_bias(rel_bias, bucket)
    sink_rows = jnp.broadcast_to(sinks.reshape(SWA_HEADS, 1, 1), (SWA_HEADS, 1, BLOCK))
    qh_sw, kh_sw, vh_sw = _heads_rows(q_sw, SWA_HEADS), _heads_rows(k_sw, SWA_KV_HEADS), _heads_rows(v_sw, SWA_KV_HEADS)
    swa_out = _swa_fwd(qh_sw, kh_sw, _heads_cols(v_sw, SWA_KV_HEADS), bias, sink_rows)

    pre1, merged, h1b = _mix_out(sb_out, swa_out, sb_norm_g, swa_norm_g, w_out_f, h0, ln1_g, ln1_b)
    act, silu, dsilu_up = _ffn_up(h1b, w_gu_sh)
    dp2, dp2b, dg2, db2, errsum = _ffn_down_loss(act, w_down_f, pre1, ln1_g, ln1_b, ln2_g, ln2_b, tgt)

    g_w_down = _matmul_tn(act, dp2b, "grad_w_down", FF_CHUNK, D_MODEL)
    dgate, dup = _ffn_down_bwd(dp2b, w_down_f, silu, dsilu_up)
    g_w_gu = _matmul_tn_pair(h1b, dgate, dup, "grad_w_gate_up")
    dp1, dp1b, dg1, db1 = _ffn_up_bwd(dgate, dup, w_gu_sh, dp2, pre1, ln1_g)
    g_w_out = _matmul_tn(merged, dp1b, "grad_w_out", D_MODEL, D_MODEL)
    doT_sb, dsw, dgsb, dgsw = _mix_bwd(dp1b, w_out_f, sb_out, swa_out, sb_norm_g, swa_norm_g)

    c = lax.axis_index("c").astype(jnp.int32)
    me = (2 * lax.axis_index("x") + lax.axis_index("y")).astype(jnp.int32)
    grads_a = [g_w_out.reshape(N_CHIPS, D_MODEL // N_CHIPS, D_MODEL), g_w_gu, g_w_down.reshape(N_CHIPS, D_FF // N_CHIPS, D_MODEL)]
    names_a = ("w_out", "w_gate_up", "w_down")
    dq_sw, dkh_sw, dvh_sw, dbias, dsink, swapped_a = _swa_bwd(qh_sw, kh_sw, _heads_cols(k_sw, SWA_KV_HEADS), vh_sw, bias,
                                                               sink_rows, _heads_rows(dsw, SWA_HEADS), grads_a)
    swa_small = _swa_small_grads(dbias, dsink, bucket)
    partials_a = [_pair_sum(g, r, c, "pair_sum_" + n) for g, r, n in zip(grads_a, swapped_a, names_a)]
    dq_sb, dk_sb, dv_sb, recv_a = _sb_bwd(qT_sb, kb_sb, kTb_sb, vb_sb,
                                             doT_sb, rsave, partials_a)
    tok = lambda t, nh: t.reshape(nh, S, HEAD_DIM).transpose(1, 0, 2).reshape(S, nh * HEAD_DIM)
    dproj = [dq_sb, dk_sb, dv_sb, dq_sw,
             jnp.concatenate([tok(dkh_sw, SWA_KV_HEADS), tok(dvh_sw, SWA_KV_HEADS)], axis=1).astype(_MXU)]
    g_w_in = jnp.concatenate([_matmul_tn(h0b, d, "grad_w_in_%d" % k, D_MODEL, d.shape[1]) for k, d in enumerate(dproj)],
                             axis=1)

    cin = IN_COLS // N_CHIPS
    grads_b = [jnp.stack([g_w_in[:, j * cin:(j + 1) * cin] for j in range(N_CHIPS)])]
    partials_b = [_pair_sum(grads_b[0], _swap_halves(grads_b, "swap_halves_in")[0], c, "pair_sum_w_in")]
    grad_x, dg_in, db_in, recv_b = _in_proj_bwd(dproj, w_in_f, dp1, x2, row(ln_in_g), partials_b)
    names = ("w_in",) + names_a
    sums = [_chip_sum(p, r, me, "chip_sum_" + n) for p, r, n in zip(partials_b + partials_a, list(recv_b) + list(recv_a), names)]
    gs_in, gs_out, gs_gu, gs_down = _join_halves(sums)

    nrb = REL_BUCKETS * SWA_HEADS
    small = _pack_small(dg_in, db_in, dgsb, dgsw, swa_small[REL_BUCKETS, :SWA_HEADS],
                        swa_small[:REL_BUCKETS, :SWA_HEADS], dg1, db1, dg2, db2, errsum)
    g_small, loss_tile = _allreduce_small(small)
    loss = loss_tile[0, 0]

    big = []
    for name, w, g, m, v in (("adamw_w_in", w_in, gs_in, m_w_in, v_w_in), ("adamw_w_out", w_out, gs_out, m_w_out, v_w_out),
                             ("adamw_w_gate_up", w_gate_up, gs_gu, m_w_gate_up, v_w_gate_up),
                             ("adamw_w_down", w_down, gs_down, m_w_down, v_w_down)):
        d, nm, nv = _adamw(w[0], g, m[0], v[0], name)
        big.append((g[None], d[None], nm[None], nv[None]))
    zero = jnp.zeros((1,), F32)
    w_small = _pack_small(ln_in_g, ln_in_b, sb_norm_g, swa_norm_g, sinks, rel_bias, ln1_g, ln1_b, ln2_g, ln2_b, zero)
    m_small = _pack_small(m_ln_in_g, m_ln_in_b, m_sb_norm_g, m_swa_norm_g, m_sinks, m_rel_bias, m_ln1_g, m_ln1_b,
                          m_ln2_g, m_ln2_b, zero)
    v_small = _pack_small(v_ln_in_g, v_ln_in_b, v_sb_norm_g, v_swa_norm_g, v_sinks, v_rel_bias, v_ln1_g, v_ln1_b,
                          v_ln2_g, v_ln2_b, zero)
    small_out = [_unpack_small(t) for t in (g_small,) + tuple(_adamw(w_small, g_small, m_small, v_small, "adamw_small"))]

    def kind(k):
        s = small_out[k]
        return [s[0], s[1], big[0][k], s[2], s[3], s[4], s[5], big[1][k], s[6], s[7], big[2][k], big[3][k], s[8], s[9]]

    return (loss, grad_x.reshape(1, S, D_MODEL), *kind(0), *kind(1), *kind(2), *kind(3))
```

```python
import functools
import math

import numpy as np
import jax
import jax.numpy as jnp
from jax import lax
from jax.experimental import pallas as pl
from jax.experimental.pallas import tpu as pltpu

F32 = jnp.float32
_MXU = jnp.bfloat16

D_MODEL = 1024
HEAD_DIM = 64
SB_HEADS = 8
SWA_HEADS = 8
SWA_KV_HEADS = 2
SWA_GROUP = SWA_HEADS // SWA_KV_HEADS
SB_WIDTH = SB_HEADS * HEAD_DIM
SWA_WIDTH = SWA_HEADS * HEAD_DIM
SWA_KV_WIDTH = SWA_KV_HEADS * HEAD_DIM
IN_COLS = 3 * SB_WIDTH + SWA_WIDTH + 2 * SWA_KV_WIDTH
BLOCK = 128
REL_BUCKETS = 32
REL_MAX_DIST = 128
D_FF = 2816
FF_CHUNK = D_FF // 2
ALPHA = 2.0 ** 0.25
LN_EPS = 1e-5
RMS_EPS = 1e-6
SCALE = HEAD_DIM ** -0.5
SB_TILE = 256
SB_GROUP_FWD = 8
SB_GROUP_BWD = 4
SB_FORWARD_LEAD = 8
SB_DEAD = -105.0
SWA_SUB = 8

ADAM_LR = 0.001
ADAM_B1 = 0.9
ADAM_B2 = 0.999
ADAM_EPS = 1e-08
ADAM_WD = 0.01
ADAM_STEP = 10

N_CHIPS = 4
SMALL_ROWS = 16

MESH = pl.DeviceIdType.MESH


def _sds(shape, dtype):
    return jax.ShapeDtypeStruct(shape, dtype)


def _cp(sem=None, vmem_mb=48):
    kw = dict(vmem_limit_bytes=vmem_mb * 1024 * 1024)
    if sem is not None:
        kw["dimension_semantics"] = sem
    return pltpu.CompilerParams(**kw)


def _dot(a, b):
    return jnp.dot(a, b, preferred_element_type=F32)


def _dot_nt(a, b):
    return lax.dot_general(a, b, (((1,), (1,)), ((), ())), preferred_element_type=F32)


def _dot_tn(a, b):
    return lax.dot_general(a, b, (((0,), (0,)), ((), ())), preferred_element_type=F32)


def _ln_hat(x):
    mu = jnp.mean(x, axis=-1, keepdims=True)
    xc = x - mu
    var = jnp.mean(xc * xc, axis=-1, keepdims=True)
    rstd = lax.rsqrt(var + LN_EPS)
    return xc * rstd, rstd


def _ln_bwd(xhat, rstd, dy, g):
    dxh = dy * g
    m1 = jnp.mean(dxh, axis=-1, keepdims=True)
    m2 = jnp.mean(dxh * xhat, axis=-1, keepdims=True)
    return rstd * (dxh - m1 - xhat * m2)


def _colsum(x):
    return jnp.sum(x, axis=0, keepdims=True)


def _split2(x):
    hi = x.astype(_MXU)
    lo = (x - hi.astype(F32)).astype(_MXU)
    return hi, lo


def _rows(tm, n):
    return pl.BlockSpec((tm, n), lambda i: (i, 0))


def _fixed(*shape):
    nd = len(shape)
    return pl.BlockSpec(shape, lambda i: (0,) * nd)


IN_SECTIONS = (SB_WIDTH, SB_WIDTH, SB_WIDTH, SWA_WIDTH, 2 * SWA_KV_WIDTH)


def _ln_in_proj(x, g, b, w):
    S = x.shape[0]
    tm = min(S, SB_TILE)
    offs = np.cumsum((0,) + IN_SECTIONS)
    swa = (3, 4)

    def body(x_ref, g_ref, b_ref, w_ref, h_ref, hb_ref, *o_refs):
        p_refs, (qT_ref, kT_ref, vT_ref, kr_ref, vr_ref) = o_refs[:len(swa)], o_refs[len(swa):]
        xhat, _ = _ln_hat(x_ref[...])
        h = xhat * g_ref[...] + b_ref[...]
        h_ref[...] = h
        hb = h.astype(_MXU)
        hb_ref[...] = hb
        proj = _dot(hb, w_ref[...])
        for k, p_ref in zip(swa, p_refs):
            p_ref[...] = proj[:, offs[k]:offs[k + 1]].astype(p_ref.dtype)
        heads = lambda k: proj[:, offs[k]:offs[k + 1]].T.astype(_MXU).reshape(SB_HEADS, HEAD_DIM, tm)
        qT_ref[...] = heads(0)
        kT_ref[:, 0] = heads(1)
        vT_ref[:, 0] = heads(2)
        for hd in range(SB_HEADS):
            cols = slice(hd * HEAD_DIM, (hd + 1) * HEAD_DIM)
            kr_ref[hd, 0] = proj[:, offs[1]:offs[2]][:, cols].astype(_MXU)
            vr_ref[hd, 0] = proj[:, offs[2]:offs[3]][:, cols].astype(_MXU)

    blocked = pl.BlockSpec((SB_HEADS, 1, HEAD_DIM, tm), lambda i: (0, i, 0, 0))
    blocked_rows = pl.BlockSpec((SB_HEADS, 1, tm, HEAD_DIM), lambda i: (0, i, 0, 0))
    return pl.pallas_call(
        body, name="ln_in_proj", grid=(S // tm,),
        in_specs=[_rows(tm, D_MODEL), _fixed(1, D_MODEL), _fixed(1, D_MODEL), _fixed(D_MODEL, IN_COLS)],
        out_specs=[_rows(tm, D_MODEL), _rows(tm, D_MODEL)] + [_rows(tm, IN_SECTIONS[k]) for k in swa]
                  + [pl.BlockSpec((SB_HEADS, HEAD_DIM, tm), lambda i: (0, 0, i)), blocked, blocked, blocked_rows,
                     blocked_rows],
        out_shape=[_sds((S, D_MODEL), F32), _sds((S, D_MODEL), _MXU)] + [_sds((S, IN_SECTIONS[k]), _MXU) for k in swa]
                  + [_sds((SB_HEADS, HEAD_DIM, S), _MXU), _sds((SB_HEADS, S // tm, HEAD_DIM, tm), _MXU),
                     _sds((SB_HEADS, S // tm, HEAD_DIM, tm), _MXU), _sds((SB_HEADS, S // tm, tm, HEAD_DIM), _MXU),
                     _sds((SB_HEADS, S // tm, tm, HEAD_DIM), _MXU)],
        compiler_params=_cp(("parallel",)),
    )(x, g, b, w)


def _rms(x, g):
    r = lax.rsqrt(jnp.mean(x * x, axis=-1, keepdims=True) + RMS_EPS)
    return x * r * g, r


def _mix_out(sb, sw, gsb, gsw, w_out, h0, g1, b1):
    S = sb.shape[0]
    tm = min(S, 512)

    def body(sb_ref, sw_ref, gsb_ref, gsw_ref, w_ref, h0_ref, g1_ref, b1_ref, pre_ref, mg_ref, h1_ref):
        ysb, _ = _rms(sb_ref[...], gsb_ref[...])
        ysw, _ = _rms(sw_ref[...], gsw_ref[...])
        ysb = ysb.astype(_MXU)
        ysw = ysw.astype(_MXU)
        mg_ref[:, :SB_WIDTH] = ysb
        mg_ref[:, SB_WIDTH:] = ysw
        mix = _dot(ysb, w_ref[:SB_WIDTH, :]) + _dot(ysw, w_ref[SB_WIDTH:, :])
        pre1 = ALPHA * h0_ref[...] + mix
        pre_ref[...] = pre1
        xhat, _ = _ln_hat(pre1)
        h1_ref[...] = (xhat * g1_ref[...] + b1_ref[...]).astype(h1_ref.dtype)

    vec = _fixed(1, D_MODEL)
    return pl.pallas_call(
        body, name="mix_out", grid=(S // tm,),
        in_specs=[_rows(tm, SB_WIDTH), _rows(tm, SWA_WIDTH), _fixed(1, SB_WIDTH), _fixed(1, SWA_WIDTH),
                  _fixed(D_MODEL, D_MODEL), _rows(tm, D_MODEL), vec, vec],
        out_specs=[_rows(tm, D_MODEL), _rows(tm, D_MODEL), _rows(tm, D_MODEL)],
        out_shape=[_sds((S, D_MODEL), F32), _sds((S, D_MODEL), _MXU), _sds((S, D_MODEL), _MXU)],
        compiler_params=_cp(("parallel",)),
    )(sb, sw, gsb, gsw, w_out, h0, g1, b1)


def _sigmoid(x):
    return 1.0 / (1.0 + jnp.exp(-x))


def _ffn_up(h1b, wgu):
    S = h1b.shape[0]
    tm = min(S, 1024)

    def body(h_ref, wg_ref, wu_ref, a_ref, s1_ref, s2_ref):
        h1 = h_ref[...]
        gate = _dot(h1, wg_ref[0])
        up = _dot(h1, wu_ref[0])
        sg = _sigmoid(gate)
        silu = gate * sg
        a_ref[...] = (silu * up).astype(a_ref.dtype)
        s1_ref[...] = silu.astype(s1_ref.dtype)
        s2_ref[...] = (up * (sg * (1.0 + gate * (1.0 - sg)))).astype(s2_ref.dtype)

    chunk = pl.BlockSpec((tm, FF_CHUNK), lambda j, i: (i, j))
    return pl.pallas_call(
        body, name="ffn_up", grid=(2, S // tm),
        in_specs=[pl.BlockSpec((tm, D_MODEL), lambda j, i: (i, 0)),
                  pl.BlockSpec((1, D_MODEL, FF_CHUNK), lambda j, i: (j, 0, 0)),
                  pl.BlockSpec((1, D_MODEL, FF_CHUNK), lambda j, i: (j + 2, 0, 0))],
        out_specs=[chunk, chunk, chunk],
        out_shape=[_sds((S, D_FF), _MXU)] * 3,
        compiler_params=_cp(("arbitrary", "arbitrary"), vmem_mb=56),
    )(h1b, wgu, wgu)


def _ffn_down_loss(a, w_down, pre1, g1, b1, g2, b2, tgt):
    S = a.shape[0]
    tm = min(S, 512)

    def body(a_ref, w_ref, p_ref, g1_ref, b1_ref, g2_ref, b2_ref, t_ref, d_ref, db_ref, dg2_ref, db2_ref, err_ref):
        @pl.when(pl.program_id(0) == 0)
        def _():
            dg2_ref[...] = jnp.zeros_like(dg2_ref)
            db2_ref[...] = jnp.zeros_like(db2_ref)
            err_ref[...] = jnp.zeros_like(err_ref)

        xhat1, _ = _ln_hat(p_ref[...])
        h1 = xhat1 * g1_ref[...] + b1_ref[...]
        pre2 = ALPHA * h1 + _dot(a_ref[...], w_ref[...])
        xhat2, rstd2 = _ln_hat(pre2)
        err = xhat2 * g2_ref[...] + b2_ref[...] - t_ref[...]
        dh2 = err * (1.0 / D_MODEL)
        dp2 = _ln_bwd(xhat2, rstd2, dh2, g2_ref[...])
        d_ref[...] = dp2
        db_ref[...] = dp2.astype(db_ref.dtype)
        dg2_ref[...] += _colsum(dh2 * xhat2)
        db2_ref[...] += _colsum(dh2)
        err_ref[...] += _colsum(err * err)

    vec = _fixed(1, D_MODEL)
    return pl.pallas_call(
        body, name="ffn_down_loss", grid=(S // tm,),
        in_specs=[_rows(tm, D_FF), _fixed(D_FF, D_MODEL), _rows(tm, D_MODEL), vec, vec, vec, vec, _rows(tm, D_MODEL)],
        out_specs=[_rows(tm, D_MODEL), _rows(tm, D_MODEL), vec, vec, vec],
        out_shape=[_sds((S, D_MODEL), F32), _sds((S, D_MODEL), _MXU), _sds((1, D_MODEL), F32), _sds((1, D_MODEL), F32),
                   _sds((1, D_MODEL), F32)],
        compiler_params=_cp(("arbitrary",)),
    )(a, w_down, pre1, g1, b1, g2, b2, tgt)


def _ffn_down_bwd(dp2b, w_down, s1, s2):
    S = dp2b.shape[0]
    tm = min(S, 512)

    def body(d_ref, w_ref, s1_ref, s2_ref, dg_ref, du_ref):
        da = _dot_nt(d_ref[...], w_ref[...])
        du_ref[...] = (da * s1_ref[...].astype(F32)).astype(du_ref.dtype)
        dg_ref[...] = (da * s2_ref[...].astype(F32)).astype(dg_ref.dtype)

    chunk = pl.BlockSpec((tm, FF_CHUNK), lambda j, i: (i, j))
    return pl.pallas_call(
        body, name="ffn_down_bwd", grid=(2, S // tm),
        in_specs=[pl.BlockSpec((tm, D_MODEL), lambda j, i: (i, 0)),
                  pl.BlockSpec((FF_CHUNK, D_MODEL), lambda j, i: (j, 0)), chunk, chunk],
        out_specs=[chunk, chunk],
        out_shape=[_sds((S, D_FF), _MXU), _sds((S, D_FF), _MXU)],
        compiler_params=_cp(("arbitrary", "arbitrary")),
    )(dp2b, w_down, s1, s2)


def _ffn_up_bwd(dgate, dup, wgu, dp2, pre1, g1):
    S = dgate.shape[0]
    tm = min(S, 256)

    def body(dg_ref, du_ref, w_ref, d2_ref, p_ref, g_ref, d1_ref, d1b_ref, dg1_ref, db1_ref):
        @pl.when(pl.program_id(0) == 0)
        def _():
            dg1_ref[...] = jnp.zeros_like(dg1_ref)
            db1_ref[...] = jnp.zeros_like(db1_ref)

        dh1 = ALPHA * d2_ref[...]
        for j in range(2):
            cols = slice(j * FF_CHUNK, (j + 1) * FF_CHUNK)
            dh1 += _dot_nt(dg_ref[:, cols], w_ref[j])
            dh1 += _dot_nt(du_ref[:, cols], w_ref[j + 2])
        xhat, rstd = _ln_hat(p_ref[...])
        dp1 = _ln_bwd(xhat, rstd, dh1, g_ref[...])
        d1_ref[...] = dp1
        d1b_ref[...] = dp1.astype(d1b_ref.dtype)
        dg1_ref[...] += _colsum(dh1 * xhat)
        db1_ref[...] += _colsum(dh1)

    vec = _fixed(1, D_MODEL)
    return pl.pallas_call(
        body, name="ffn_up_bwd", grid=(S // tm,),
        in_specs=[_rows(tm, D_FF), _rows(tm, D_FF), _fixed(4, D_MODEL, FF_CHUNK), _rows(tm, D_MODEL),
                  _rows(tm, D_MODEL), vec],
        out_specs=[_rows(tm, D_MODEL), _rows(tm, D_MODEL), vec, vec],
        out_shape=[_sds((S, D_MODEL), F32), _sds((S, D_MODEL), _MXU), _sds((1, D_MODEL), F32), _sds((1, D_MODEL), F32)],
        compiler_params=_cp(("arbitrary",), vmem_mb=56),
    )(dgate, dup, wgu, dp2, pre1, g1)


def _rms_bwd(x, g, dy):
    n = x.shape[-1]
    r = lax.rsqrt(jnp.mean(x * x, axis=-1, keepdims=True) + RMS_EPS)
    u = dy * g
    dx = r * u - x * (r * r * r) * (jnp.sum(u * x, axis=-1, keepdims=True) * (1.0 / n))
    return dx, _colsum(dy * x * r)


def _mix_bwd(dp1b, w_out, sb, sw, gsb, gsw):
    S = sb.shape[0]
    tm = min(S, 512)

    def body(d_ref, w_ref, sb_ref, sw_ref, gsb_ref, gsw_ref, dsb_ref, dsw_ref, dgsb_ref, dgsw_ref):
        @pl.when(pl.program_id(0) == 0)
        def _():
            dgsb_ref[...] = jnp.zeros_like(dgsb_ref)
            dgsw_ref[...] = jnp.zeros_like(dgsw_ref)

        dm = _dot_nt(d_ref[...], w_ref[...])
        dsb, dgsb = _rms_bwd(sb_ref[...], gsb_ref[...], dm[:, :SB_WIDTH])
        dsw, dgsw = _rms_bwd(sw_ref[...], gsw_ref[...], dm[:, SB_WIDTH:])
        dsb_ref[...] = dsb.T.astype(dsb_ref.dtype).reshape(dsb_ref.shape)
        dsw_ref[...] = dsw.astype(dsw_ref.dtype)
        dgsb_ref[...] += dgsb
        dgsw_ref[...] += dgsw

    return pl.pallas_call(
        body, name="mix_bwd", grid=(S // tm,),
        in_specs=[_rows(tm, D_MODEL), _fixed(D_MODEL, D_MODEL), _rows(tm, SB_WIDTH), _rows(tm, SWA_WIDTH),
                  _fixed(1, SB_WIDTH), _fixed(1, SWA_WIDTH)],
        out_specs=[pl.BlockSpec((SB_HEADS, HEAD_DIM, tm), lambda i: (0, 0, i)), _rows(tm, SWA_WIDTH),
                   _fixed(1, SB_WIDTH), _fixed(1, SWA_WIDTH)],
        out_shape=[_sds((SB_HEADS, HEAD_DIM, S), _MXU), _sds((S, SWA_WIDTH), _MXU), _sds((1, SB_WIDTH), F32),
                   _sds((1, SWA_WIDTH), F32)],
        compiler_params=_cp(("arbitrary",)),
    )(dp1b, w_out, sb, sw, gsb, gsw)


def _in_proj_bwd(dproj, w_in, dp1, x, g, parts):
    S = x.shape[0]
    tm = min(S, 512)
    nw = len(parts)
    ns = len(IN_SECTIONS)
    offs = np.cumsum((0,) + IN_SECTIONS)
    s_ins, s_outs, s_sems = _scatter_io(parts)

    def body(*refs):
        dpj_refs = refs[:ns]
        w_ref, d1_ref, x_ref, g_ref = refs[ns:ns + 4]
        rest = refs[ns + 4:]
        gx_ref, dg_ref, db_ref = rest[nw:nw + 3]
        scatter = _Scatter(rest[:nw], rest[nw + 3:2 * nw + 3], *rest[2 * nw + 3:])

        @pl.when(pl.program_id(0) == 0)
        def _():
            scatter.start()
            dg_ref[...] = jnp.zeros_like(dg_ref)
            db_ref[...] = jnp.zeros_like(db_ref)

        dh0 = ALPHA * d1_ref[...]
        for k in range(ns):
            dh0 += _dot_nt(dpj_refs[k][...], w_ref[:, offs[k]:offs[k + 1]])
        xhat, rstd = _ln_hat(x_ref[...])
        gx_ref[...] = _ln_bwd(xhat, rstd, dh0, g_ref[...])
        dg_ref[...] += _colsum(dh0 * xhat)
        db_ref[...] += _colsum(dh0)

        @pl.when(pl.program_id(0) == pl.num_programs(0) - 1)
        def _():
            scatter.finish()

    vec = _fixed(1, D_MODEL)
    any_spec = pl.BlockSpec(memory_space=pl.ANY)
    res = pl.pallas_call(
        body, name="in_proj_bwd", grid=(S // tm,),
        in_specs=[_rows(tm, n) for n in IN_SECTIONS]
                 + [_fixed(D_MODEL, IN_COLS), _rows(tm, D_MODEL), _rows(tm, D_MODEL), vec] + [any_spec] * nw,
        out_specs=[_rows(tm, D_MODEL), vec, vec] + [any_spec] * nw,
        out_shape=[_sds((S, D_MODEL), F32), _sds((1, D_MODEL), F32), _sds((1, D_MODEL), F32)] + s_outs,
        scratch_shapes=s_sems,
        compiler_params=_cp(("arbitrary",)),
    )(*dproj, w_in, dp1, x, g, *s_ins)
    return res[0], res[1], res[2], list(res[3:])


def _matmul_tn(a, b, name, tk, tn):
    T, K = a.shape
    N = b.shape[1]
    tt = min(T, 1024)

    def body(a_ref, b_ref, o_ref):
        @pl.when(pl.program_id(2) == 0)
        def _():
            o_ref[...] = jnp.zeros_like(o_ref)

        o_ref[...] += _dot_tn(a_ref[...], b_ref[...])

    return pl.pallas_call(
        body, name=name, grid=(K // tk, N // tn, T // tt),
        in_specs=[pl.BlockSpec((tt, tk), lambda k, n, t: (t, k)), pl.BlockSpec((tt, tn), lambda k, n, t: (t, n))],
        out_specs=pl.BlockSpec((tk, tn), lambda k, n, t: (k, n)),
        out_shape=_sds((K, N), F32),
        compiler_params=_cp(("parallel", "parallel", "arbitrary")),
    )(a, b)


def _place():
    x, y, c = lax.axis_index("x"), lax.axis_index("y"), lax.axis_index("c")
    chips = [(1 - x, y), (x, 1 - y), (1 - x, 1 - y)]
    return x, y, c, chips


class _Gather:
    def __init__(self, in_refs, out_refs, send_sems, recv_sems):
        self.in_refs, self.out_refs, self.send_sems, self.recv_sems = in_refs, out_refs, send_sems, recv_sems
        self.x, self.y, self.c, self.chips = _place()

    def _copy(self, w, k, chip, hc, to, src=None):
        part = self.out_refs[w].at[2 * chip[0] + chip[1], hc]
        return pltpu.make_async_remote_copy(
            src_ref=part if src is None else src, dst_ref=part, send_sem=self.send_sems.at[w, k],
            recv_sem=self.recv_sems.at[w, k], device_id=to, device_id_type=MESH)

    def _first(self):
        x, y, c = self.x, self.y, self.c
        return [self._copy(w, j, (x, y), c, (*chip, c), src=self.in_refs[w].at[c])
                for w in range(len(self.in_refs)) for j, chip in enumerate(self.chips)]

    def start(self):
        for cp in self._first():
            cp.start()

    def _passed(self):
        sibling = (self.x, self.y, 1 - self.c)
        return [self._copy(w, 3 + j, chip, self.c, sibling)
                for w in range(len(self.in_refs)) for j, chip in enumerate(self.chips)]

    def forward(self):
        me = (self.x, self.y, self.c)
        passed = self._passed()
        for w in range(len(self.in_refs)):
            for j, chip in enumerate(self.chips):
                self._copy(w, j, chip, self.c, me).wait_recv()
                passed[3 * w + j].start()

    def finish(self):
        me = (self.x, self.y, self.c)
        for w in range(len(self.in_refs)):
            for j, chip in enumerate(self.chips):
                self._copy(w, 3 + j, chip, 1 - self.c, me).wait_recv()
        for cp in self._first() + self._passed():
            cp.wait_send()


def _gather_io(shards):
    halves = [(s.shape[0] // 2, s.shape[1]) for s in shards]
    ins = [s.reshape(2, h, cols) for s, (h, cols) in zip(shards, halves)]
    outs = [_sds((N_CHIPS, 2, h, cols), s.dtype) for s, (h, cols) in zip(shards, halves)]
    sems = [pltpu.SemaphoreType.DMA((len(shards), 6)), pltpu.SemaphoreType.DMA((len(shards), 6))]
    return ins, outs, sems


def _gather_assemble(outs, shards):
    me = 2 * lax.axis_index("x") + lax.axis_index("y")
    return [lax.dynamic_update_slice_in_dim(o.reshape((N_CHIPS,) + s.shape), s[None], me, axis=0)
            for o, s in zip(outs, shards)]


class _Scatter:
    def __init__(self, p_refs, out_refs, send_sems, recv_sems):
        self.p_refs, self.out_refs, self.send_sems, self.recv_sems = p_refs, out_refs, send_sems, recv_sems
        self.x, self.y, self.c, self.chips = _place()
        self.me = 2 * self.x + self.y

    def _copy(self, w, j, chip, src_chip, dst_chip):
        return pltpu.make_async_remote_copy(
            src_ref=self.p_refs[w].at[src_chip], dst_ref=self.out_refs[w].at[dst_chip], send_sem=self.send_sems.at[w, j],
            recv_sem=self.recv_sems.at[w, j], device_id=(*chip, self.c), device_id_type=MESH)

    def _sends(self):
        return [self._copy(w, j, chip, 2 * chip[0] + chip[1], self.me)
                for w in range(len(self.p_refs)) for j, chip in enumerate(self.chips)]

    def start(self):
        for cp in self._sends():
            cp.start()

    def finish(self):
        for w in range(len(self.p_refs)):
            for j, chip in enumerate(self.chips):
                self._copy(w, j, chip, self.me, 2 * chip[0] + chip[1]).wait_recv()
        for cp in self._sends():
            cp.wait_send()


def _scatter_io(parts):
    sems = [pltpu.SemaphoreType.DMA((len(parts), 3)), pltpu.SemaphoreType.DMA((len(parts), 3))]
    return list(parts), [_sds(p.shape, p.dtype) for p in parts], sems


class _Swap:
    def __init__(self, g_refs, out_refs, send_sems, recv_sems):
        x, y, c, _ = _place()
        self.copies = []
        for w in range(len(g_refs)):
            half = out_refs[w].shape[1]
            theirs = g_refs[w].at[:, pl.ds(pl.multiple_of((1 - c) * half, 8), half), :]
            self.copies.append(pltpu.make_async_remote_copy(
                src_ref=theirs, dst_ref=out_refs[w], send_sem=send_sems.at[w], recv_sem=recv_sems.at[w],
                device_id=(x, y, 1 - c), device_id_type=MESH))

    def start(self):
        for cp in self.copies:
            cp.start()

    def finish(self):
        for cp in self.copies:
            cp.wait()


def _swap_io(grads):
    outs = [_sds((g.shape[0], g.shape[1] // 2, g.shape[2]), g.dtype) for g in grads]
    return list(grads), outs, [pltpu.SemaphoreType.DMA((len(grads),)), pltpu.SemaphoreType.DMA((len(grads),))]


def _matmul_tn_pair(a, b0, b1, name):
    T, K = a.shape
    tt = min(T, 1024)

    def body(a_ref, b0_ref, b1_ref, o_ref):
        n = pl.program_id(0)

        @pl.when(pl.program_id(1) == 0)
        def _():
            o_ref[...] = jnp.zeros_like(o_ref)

        @pl.when(n < 2)
        def _():
            o_ref[0] += _dot_tn(a_ref[...], b0_ref[...])

        @pl.when(n >= 2)
        def _():
            o_ref[0] += _dot_tn(a_ref[...], b1_ref[...])

    return pl.pallas_call(
        body, name=name, grid=(4, T // tt),
        in_specs=[pl.BlockSpec((tt, K), lambda n, t: (t, 0)),
                  pl.BlockSpec((tt, FF_CHUNK), lambda n, t: (t, jnp.minimum(n, 1))),
                  pl.BlockSpec((tt, FF_CHUNK), lambda n, t: (t, jnp.maximum(n - 2, 0)))],
        out_specs=pl.BlockSpec((1, K, FF_CHUNK), lambda n, t: (n, 0, 0)),
        out_shape=_sds((4, K, FF_CHUNK), F32),
        compiler_params=_cp(("parallel", "arbitrary")),
    )(a, b0, b1)


def _sb_logs(zt, causal):
    e = jnp.exp(-jnp.abs(zt))
    lb = jnp.minimum(zt, 0.0) - jnp.log(1.0 + e)
    l1m = lb - zt
    if causal is not None:
        l1m = jnp.where(causal, l1m, 0.0)
    return lb, l1m


def _sb_weights(lb, suf, causal):
    a = jnp.exp(lb + suf)
    if causal is not None:
        a = jnp.where(causal, a, 0.0)
    return a


def _tri_masks(t):
    r = lax.broadcasted_iota(jnp.int32, (t, t), 0)
    c = lax.broadcasted_iota(jnp.int32, (t, t), 1)
    return r, c


def _sb_fwd(qT, kb, vTb, shards):
    Hh, _, S = qT.shape
    nk, T = kb.shape[1], kb.shape[2]
    nq = S // T
    G = SB_GROUP_FWD
    nw = len(shards)
    g_ins, g_outs, g_sems = _gather_io(shards)
    forward_step = max(nq - 1 - SB_FORWARD_LEAD, 0)

    def body(qT_ref, k_ref, vT_ref, *rest):
        o_ref, rs_ref = rest[nw:nw + 2]
        gather = _Gather(rest[:nw], rest[nw + 2:2 * nw + 2], *rest[2 * nw + 2:])
        i = pl.program_id(1)
        first_step = jnp.logical_and(pl.program_id(0) == 0, i == 0)
        last_step = jnp.logical_and(pl.program_id(0) == pl.num_programs(0) - 1, i == pl.num_programs(1) - 1)

        @pl.when(first_step)
        def _():
            gather.start()

        qts = [(qT_ref[g].astype(F32) * SCALE).astype(_MXU) for g in range(G)]
        r, c = _tri_masks(T)
        upper = (c > r).astype(_MXU)
        causal = r < c

        def blk(j, carry, mask):
            hs = range(G)
            for g in hs:
                rs_ref[g, 0, j] = jnp.broadcast_to(carry[g][0], (8, T))
            zs = [_dot(k_ref[g, j], qts[g]) for g in hs]
            lbs, l1ms = zip(*[_sb_logs(zs[g], mask) for g in hs])
            splits = [_split2(l1ms[g]) for g in hs]
            cums = [_dot(upper, splits[g][0]) + _dot(upper, splits[g][1]) for g in hs]
            avs = [_sb_weights(lbs[g], carry[g][0] + cums[g], mask).astype(_MXU) for g in hs]
            accs = [carry[g][1] + _dot(vT_ref[g, j], avs[g]) for g in hs]
            return tuple((carry[g][0] + _colsum(l1ms[g]), accs[g]) for g in hs)

        def go_on(j, carry):
            top = carry[0][0]
            for g in range(1, G):
                top = jnp.maximum(top, carry[g][0])
            return jnp.logical_and(j >= 0, jnp.max(top) >= SB_DEAD)

        init = tuple((jnp.zeros((1, T), F32), jnp.zeros((HEAD_DIM, T), F32)) for _ in range(G))
        carry = blk(i, init, causal)
        j, carry = lax.while_loop(lambda st: go_on(*st), lambda st: (st[0] - 1, blk(st[0], st[1], None)),
                                  (i - 1, carry))

        @pl.when(j >= 0)
        def _():
            for g in range(G):
                rs_ref[g, 0, j] = jnp.broadcast_to(carry[g][0], (8, T))

        o_ref[...] = jnp.concatenate([carry[g][1] for g in range(G)], axis=0).T

        @pl.when(jnp.logical_and(pl.program_id(0) == pl.num_programs(0) - 1, i == forward_step))
        def _():
            gather.forward()

        @pl.when(last_step)
        def _():
            gather.finish()

    any_spec = pl.BlockSpec(memory_space=pl.ANY)
    res = pl.pallas_call(
        body, name="sb_fwd", grid=(Hh // G, nq),
        in_specs=[pl.BlockSpec((G, HEAD_DIM, T), lambda h, i: (h, 0, i)),
                  pl.BlockSpec((G, nk, T, HEAD_DIM), lambda h, i: (h, 0, 0, 0), pipeline_mode=pl.Buffered(1)),
                  pl.BlockSpec((G, nk, HEAD_DIM, T), lambda h, i: (h, 0, 0, 0), pipeline_mode=pl.Buffered(1))]
                 + [any_spec] * nw,
        out_specs=[pl.BlockSpec((T, G * HEAD_DIM), lambda h, i: (i, h)),
                   pl.BlockSpec((G, 1, nk, 8, T), lambda h, i: (h, i, 0, 0, 0))] + [any_spec] * nw,
        out_shape=[_sds((S, Hh * HEAD_DIM), F32), _sds((Hh, nq, nk, 8, T), F32)] + g_outs,
        scratch_shapes=g_sems,
        compiler_params=_cp(("arbitrary", "arbitrary")),
    )(qT, kb, vTb, *g_ins)
    return res[0], res[1], _gather_assemble(res[2:], shards)


def _sb_bwd(qT, kb, kTb, vb, doT, rsave, parts):
    Hh, _, S = qT.shape
    nk, T = kb.shape[1], kb.shape[2]
    nq = S // T
    G = SB_GROUP_BWD
    nw = len(parts)
    s_ins, s_outs, s_sems = _scatter_io(parts)

    def body(qT_ref, k_ref, kT_ref, v_ref, doT_ref, rs_ref, *rest):
        dq_ref, dk_out_ref, dv_out_ref = rest[nw:nw + 3]
        dk_ref, dv_ref = rest[2 * nw + 3:2 * nw + 5]
        scatter = _Scatter(rest[:nw], rest[nw + 3:2 * nw + 3], *rest[2 * nw + 5:])
        i = pl.program_id(1)
        first_step = jnp.logical_and(pl.program_id(0) == 0, i == 0)
        last_step = jnp.logical_and(pl.program_id(0) == pl.num_programs(0) - 1, i == pl.num_programs(1) - 1)

        @pl.when(first_step)
        def _():
            scatter.start()

        @pl.when(i == 0)
        def _():
            dk_ref[...] = jnp.zeros_like(dk_ref)
            dv_ref[...] = jnp.zeros_like(dv_ref)

        qts = [(qT_ref[g].astype(F32) * SCALE).astype(_MXU) for g in range(G)]
        douts = [doT_ref[g] for g in range(G)]
        r, c = _tri_masks(T)
        upper = (c > r).astype(_MXU)
        lower = (c < r).astype(_MXU)
        causal = r < c

        def blk(j, carry, mask):
            hs = range(G)
            zs = [_dot(k_ref[g, j], qts[g]) for g in hs]
            das = [_dot(v_ref[g, j], douts[g]) for g in hs]
            lbs, l1ms = zip(*[_sb_logs(zs[g], mask) for g in hs])
            splits = [_split2(l1ms[g]) for g in hs]
            cums = [_dot(upper, splits[g][0]) + _dot(upper, splits[g][1]) for g in hs]
            avs = [_sb_weights(lbs[g], rs_ref[g, 0, j][0:1, :] + cums[g], mask) for g in hs]
            ets = [das[g] * avs[g] for g in hs]
            esplits = [_split2(ets[g]) for g in hs]
            ecums = [_dot(lower, esplits[g][0]) + _dot(lower, esplits[g][1]) for g in hs]
            dzs = []
            for g in hs:
                sig = jnp.exp(lbs[g])
                dz = ets[g] * (1.0 - sig) - (carry[g][0] + ecums[g]) * sig
                if mask is not None:
                    dz = jnp.where(mask, dz, 0.0)
                dzs.append(dz.astype(_MXU))
            dqs = [carry[g][1] + _dot(kT_ref[g, j], dzs[g]) for g in hs]
            for g in hs:
                dk_ref[j, g * HEAD_DIM:(g + 1) * HEAD_DIM, :] += _dot_nt(qts[g], dzs[g])
            for g in hs:
                dv_ref[j, g * HEAD_DIM:(g + 1) * HEAD_DIM, :] += _dot_nt(douts[g], avs[g].astype(_MXU))
            return tuple((carry[g][0] + _colsum(ets[g]), dqs[g]) for g in hs)

        def live(j):
            jj = jnp.maximum(j, 0)
            top = rs_ref[0, 0, jj][0:1, :]
            for g in range(1, G):
                top = jnp.maximum(top, rs_ref[g, 0, jj][0:1, :])
            return jnp.logical_and(j >= 0, jnp.max(top) >= SB_DEAD)

        first = lax.while_loop(lambda st: st[1], lambda st: (st[0] - 1, live(st[0] - 2)), (i, live(i - 1)))[0]
        carry = tuple((jnp.zeros((1, T), F32), jnp.zeros((HEAD_DIM, T), F32)) for _ in range(G))
        carry = lax.fori_loop(first, i, lambda s, cr: blk(s, cr, None), carry)
        carry = blk(i, carry, causal)
        dq_ref[...] = (jnp.concatenate([carry[g][1] for g in range(G)], axis=0) * SCALE).T.astype(dq_ref.dtype)

        @pl.when(i == pl.num_programs(1) - 1)
        def _():
            def flush(j, _):
                rows = pl.ds(pl.multiple_of(j * T, T), T)
                dk_out_ref[rows, :] = dk_ref[j].T.astype(dk_out_ref.dtype)
                dv_out_ref[rows, :] = dv_ref[j].T.astype(dv_out_ref.dtype)
                return 0
            lax.fori_loop(0, nk, flush, 0)

        @pl.when(last_step)
        def _():
            scatter.finish()

    colblk = pl.BlockSpec((G, HEAD_DIM, T), lambda h, i: (h, 0, i))
    once = pl.Buffered(1)
    kblk = pl.BlockSpec((G, nk, T, HEAD_DIM), lambda h, i: (h, 0, 0, 0), pipeline_mode=once)
    kTblk = pl.BlockSpec((G, nk, HEAD_DIM, T), lambda h, i: (h, 0, 0, 0), pipeline_mode=once)
    any_spec = pl.BlockSpec(memory_space=pl.ANY)
    res = pl.pallas_call(
        body, name="sb_bwd", grid=(Hh // G, nq),
        in_specs=[colblk, kblk, kTblk, kblk, colblk,
                  pl.BlockSpec((G, 1, nk, 8, T), lambda h, i: (h, i, 0, 0, 0))] + [any_spec] * nw,
        out_specs=[pl.BlockSpec((T, G * HEAD_DIM), lambda h, i: (i, h)),
                   pl.BlockSpec((S, G * HEAD_DIM), lambda h, i: (0, h), pipeline_mode=once),
                   pl.BlockSpec((S, G * HEAD_DIM), lambda h, i: (0, h), pipeline_mode=once)] + [any_spec] * nw,
        out_shape=[_sds((S, Hh * HEAD_DIM), _MXU)] * 3 + s_outs,
        scratch_shapes=[pltpu.VMEM((nk, G * HEAD_DIM, T), F32), pltpu.VMEM((nk, G * HEAD_DIM, T), F32)] + s_sems,
        compiler_params=_cp(("arbitrary", "arbitrary"), vmem_mb=60),
    )(qT, kb, kTb, vb, doT, rsave, *s_ins)
    return res[0], res[1], res[2], list(res[3:])


def _bucket_table():
    qi = np.arange(BLOCK)[:, None]
    cj = np.arange(2 * BLOCK)[None, :]
    dist = qi + BLOCK - cj
    exact = REL_BUCKETS // 2
    d = np.maximum(dist, 0)
    d_f = np.maximum(d, 1).astype(np.float32)
    large = exact + (np.log(d_f / np.float32(exact)) / np.float32(math.log(REL_MAX_DIST / exact))
                     * np.float32(REL_BUCKETS - exact)).astype(np.int32)
    large = np.minimum(large, REL_BUCKETS - 1)
    return np.where(d < exact, d, large).astype(np.int32)


def _swa_bias(rel_bias, bucket):
    def body(rb_ref, bk_ref, o_ref):
        bk = bk_ref[...]
        for h in range(SWA_HEADS):
            t = jnp.zeros((2 * BLOCK, BLOCK), F32)
            for b in range(REL_BUCKETS):
                t = jnp.where(bk == b, rb_ref[b, h], t)
            o_ref[h] = t

    return pl.pallas_call(
        body, name="swa_bias",
        in_specs=[pl.BlockSpec(memory_space=pltpu.SMEM), pl.BlockSpec(memory_space=pltpu.VMEM)],
        out_specs=pl.BlockSpec(memory_space=pltpu.VMEM),
        out_shape=_sds((SWA_HEADS, 2 * BLOCK, BLOCK), F32),
    )(rel_bias, bucket)


def _swa_logits(q, kp, kc):
    qs = (q.astype(F32) * SCALE).astype(_MXU)
    return qs, _dot_nt(kp, qs), _dot_nt(kc, qs)


def _swa_softmax(lp, lc, bias, sink, live_prev):
    r, c = _tri_masks(BLOCK)
    in_window = r > c if live_prev is None else jnp.logical_and(r > c, live_prev)
    lp = jnp.where(in_window, lp + bias[:BLOCK, :], -jnp.inf)
    lc = jnp.where(r <= c, lc + bias[BLOCK:, :], -jnp.inf)
    m = jnp.maximum(jnp.maximum(jnp.max(lp, axis=0, keepdims=True), jnp.max(lc, axis=0, keepdims=True)), sink)
    pp = jnp.exp(lp - m)
    pc = jnp.exp(lc - m)
    ps = jnp.exp(sink - m)
    denom = _colsum(pp) + _colsum(pc) + ps
    return pp / denom, pc / denom, ps / denom


def _swa_sub(nb):
    return min(SWA_SUB, nb)


def _swa_keys(b, prev_ref, cur_ref, i):
    cur = cur_ref[0, b * BLOCK:(b + 1) * BLOCK, :]
    if b == 0:
        return prev_ref[0], cur, i > 0
    return cur_ref[0, (b - 1) * BLOCK:b * BLOCK, :], cur, None


def _swa_keys_t(b, prev_ref, cur_ref):
    cur = cur_ref[0, :, b * BLOCK:(b + 1) * BLOCK]
    return (prev_ref[0] if b == 0 else cur_ref[0, :, (b - 1) * BLOCK:b * BLOCK]), cur


SWA_PAIR = 4


def _swa_fwd(q, k, vT, bias, sink):
    S = q.shape[1]
    nb = S // BLOCK
    ns = _swa_sub(nb)
    R = ns * BLOCK
    P = SWA_PAIR

    def body(q_ref, kp_ref, kc_ref, vp_ref, vc_ref, bias_ref, sink_ref, o_ref):
        i = pl.program_id(1)
        units = [(hh, b) for hh in range(P) for b in range(ns)]
        keys = [_swa_keys(b, kp_ref, kc_ref, i) for b in range(ns)]
        vals = [_swa_keys_t(b, vp_ref, vc_ref) for b in range(ns)]
        logits = {u: _swa_logits(q_ref[u[0], u[1] * BLOCK:(u[1] + 1) * BLOCK, :], keys[u[1]][0], keys[u[1]][1])
                  for u in units}
        ws = {u: _swa_softmax(logits[u][1], logits[u][2], bias_ref[u[0]], sink_ref[u[0]][:, :1], keys[u[1]][2])
              for u in units}
        outs = {u: _dot(vals[u[1]][0], ws[u][0].astype(_MXU)) + _dot(vals[u[1]][1], ws[u][1].astype(_MXU))
                for u in units}
        for b in range(ns):
            o_ref[b * BLOCK:(b + 1) * BLOCK, :] = jnp.concatenate([outs[(hh, b)] for hh in range(P)], axis=0).T

    kvh = lambda p: (p * P) // SWA_GROUP
    prev = pl.BlockSpec((1, BLOCK, HEAD_DIM), lambda p, i: (kvh(p), jnp.maximum(i * ns - 1, 0), 0))
    cur = pl.BlockSpec((1, R, HEAD_DIM), lambda p, i: (kvh(p), i, 0))
    prev_t = pl.BlockSpec((1, HEAD_DIM, BLOCK), lambda p, i: (kvh(p), 0, jnp.maximum(i * ns - 1, 0)))
    cur_t = pl.BlockSpec((1, HEAD_DIM, R), lambda p, i: (kvh(p), 0, i))
    return pl.pallas_call(
        body, name="swa_fwd", grid=(SWA_HEADS // P, nb // ns),
        in_specs=[pl.BlockSpec((P, R, HEAD_DIM), lambda p, i: (p, i, 0)), prev, cur, prev_t, cur_t,
                  pl.BlockSpec((P, 2 * BLOCK, BLOCK), lambda p, i: (p, 0, 0)),
                  pl.BlockSpec((P, 1, BLOCK), lambda p, i: (p, 0, 0))],
        out_specs=pl.BlockSpec((R, P * HEAD_DIM), lambda p, i: (i, p)),
        out_shape=_sds((S, SWA_HEADS * HEAD_DIM), F32),
        compiler_params=_cp(("parallel", "parallel")),
    )(q, k, k, vT, vT, bias, sink)


def _swa_bwd(q, k, kT, v, bias, sink, do, grads):
    S = q.shape[1]
    nb = S // BLOCK
    ns = _swa_sub(nb)
    R = ns * BLOCK
    P = SWA_PAIR
    nw = len(grads)
    x_ins, x_outs, x_sems = _swap_io(grads)

    def body(q_ref, kp_ref, kc_ref, ktp_ref, ktc_ref, vp_ref, vc_ref, bias_ref, sink_ref, do_ref, *rest):
        dq_ref, dk_ref, dv_ref, dbias_ref, dsink_ref = rest[nw:nw + 5]
        swap = _Swap(rest[:nw], rest[nw + 5:2 * nw + 5], *rest[2 * nw + 5:])
        g = pl.program_id(1)
        i = pl.program_id(2)
        first_step = jnp.logical_and(pl.program_id(0) == 0, jnp.logical_and(g == 0, i == 0))
        last_step = jnp.logical_and(pl.program_id(0) == pl.num_programs(0) - 1,
                                    jnp.logical_and(g == pl.num_programs(1) - 1, i == pl.num_programs(2) - 1))

        @pl.when(first_step)
        def _():
            swap.start()

        @pl.when(jnp.logical_and(g == 0, i == 0))
        def _():
            dk_ref[...] = jnp.zeros_like(dk_ref)
            dv_ref[...] = jnp.zeros_like(dv_ref)

        @pl.when(i == 0)
        def _():
            dbias_ref[...] = jnp.zeros_like(dbias_ref)
            dsink_ref[...] = jnp.zeros_like(dsink_ref)

        subs = range(ns)
        units = [(hh, b) for hh in range(P) for b in subs]
        rows = [slice(b * BLOCK, (b + 1) * BLOCK) for b in subs]
        keys = [_swa_keys(b, kp_ref, kc_ref, i) for b in subs]
        keys_t = [_swa_keys_t(b, ktp_ref, ktc_ref) for b in subs]
        vals = [_swa_keys(b, vp_ref, vc_ref, i) for b in subs]
        douts = {u: do_ref[u[0], rows[u[1]], :] for u in units}
        logits = {u: _swa_logits(q_ref[u[0], rows[u[1]], :], keys[u[1]][0], keys[u[1]][1]) for u in units}
        dws = {u: (_dot_nt(vals[u[1]][0], douts[u]), _dot_nt(vals[u[1]][1], douts[u])) for u in units}
        wts, dls = {}, {}
        for hh in range(P):
            dbp = jnp.zeros((BLOCK, BLOCK), F32)
            dbc = jnp.zeros((BLOCK, BLOCK), F32)
            dsk = jnp.zeros((1, BLOCK), F32)
            for b in subs:
                u = (hh, b)
                wp, wc, ws = _swa_softmax(logits[u][1], logits[u][2], bias_ref[hh], sink_ref[hh][:, :1], keys[b][2])
                dwp, dwc = dws[u]
                delta = _colsum(wp * dwp) + _colsum(wc * dwc)
                dlp = wp * (dwp - delta)
                dlc = wc * (dwc - delta)
                dbp += dlp
                dbc += dlc
                dsk -= ws * delta
                wts[u] = (wp.astype(_MXU), wc.astype(_MXU))
                dls[u] = (dlp.astype(_MXU), dlc.astype(_MXU))
            dbias_ref[hh, :BLOCK, :] += dbp
            dbias_ref[hh, BLOCK:, :] += dbc
            dsink_ref[hh] += jnp.broadcast_to(dsk, (8, BLOCK))
        dqs = {u: (_dot(keys_t[u[1]][0], dls[u][0]) + _dot(keys_t[u[1]][1], dls[u][1])) * SCALE for u in units}
        for b in subs:
            dq_ref[rows[b], :] = jnp.concatenate([dqs[(hh, b)] for hh in range(P)], axis=0).T.astype(dq_ref.dtype)
        for b in subs:
            blk = i * ns + b
            dk_cur = sum(_dot(dls[(hh, b)][1], logits[(hh, b)][0]) for hh in range(P))
            dv_cur = sum(_dot(wts[(hh, b)][1], douts[(hh, b)]) for hh in range(P))
            dk_prev = sum(_dot(dls[(hh, b)][0], logits[(hh, b)][0]) for hh in range(P))
            dv_prev = sum(_dot(wts[(hh, b)][0], douts[(hh, b)]) for hh in range(P))
            dk_ref[0, blk] += dk_cur
            dv_ref[0, blk] += dv_cur
            if b == 0:
                @pl.when(i > 0)
                def _():
                    dk_ref[0, blk - 1] += dk_prev
                    dv_ref[0, blk - 1] += dv_prev
            else:
                dk_ref[0, blk - 1] += dk_prev
                dv_ref[0, blk - 1] += dv_prev

        @pl.when(last_step)
        def _():
            swap.finish()

    G2 = SWA_GROUP // P
    hp = lambda kv, g, i: kv * G2 + g
    prev = pl.BlockSpec((1, BLOCK, HEAD_DIM), lambda kv, g, i: (kv, jnp.maximum(i * ns - 1, 0), 0))
    cur = pl.BlockSpec((1, R, HEAD_DIM), lambda kv, g, i: (kv, i, 0))
    prev_t = pl.BlockSpec((1, HEAD_DIM, BLOCK), lambda kv, g, i: (kv, 0, jnp.maximum(i * ns - 1, 0)))
    cur_t = pl.BlockSpec((1, HEAD_DIM, R), lambda kv, g, i: (kv, 0, i))
    qblk = pl.BlockSpec((P, R, HEAD_DIM), lambda kv, g, i: (hp(kv, g, i), i, 0))
    kvacc = pl.BlockSpec((1, nb, BLOCK, HEAD_DIM), lambda kv, g, i: (kv, 0, 0, 0))
    any_spec = pl.BlockSpec(memory_space=pl.ANY)
    res = pl.pallas_call(
        body, name="swa_bwd", grid=(SWA_KV_HEADS, G2, nb // ns),
        in_specs=[qblk, prev, cur, prev_t, cur_t, prev, cur,
                  pl.BlockSpec((P, 2 * BLOCK, BLOCK), lambda kv, g, i: (hp(kv, g, i), 0, 0)),
                  pl.BlockSpec((P, 1, BLOCK), lambda kv, g, i: (hp(kv, g, i), 0, 0)), qblk] + [any_spec] * nw,
        out_specs=[pl.BlockSpec((R, P * HEAD_DIM), lambda kv, g, i: (i, hp(kv, g, i))), kvacc, kvacc,
                   pl.BlockSpec((P, 2 * BLOCK, BLOCK), lambda kv, g, i: (hp(kv, g, i), 0, 0)),
                   pl.BlockSpec((P, 8, BLOCK), lambda kv, g, i: (hp(kv, g, i), 0, 0))] + [any_spec] * nw,
        out_shape=[_sds((S, SWA_HEADS * HEAD_DIM), _MXU), _sds((SWA_KV_HEADS, nb, BLOCK, HEAD_DIM), F32),
                   _sds((SWA_KV_HEADS, nb, BLOCK, HEAD_DIM), F32), _sds((SWA_HEADS, 2 * BLOCK, BLOCK), F32),
                   _sds((SWA_HEADS, 8, BLOCK), F32)] + x_outs,
        scratch_shapes=x_sems,
        compiler_params=_cp(("arbitrary", "arbitrary", "arbitrary")),
    )(q, k, k, kT, kT, v, v, bias, sink, do, *x_ins)
    return res[0], res[1], res[2], res[3], res[4], list(res[5:])


def _swa_small_grads(dbias, dsink, bucket):
    rows = REL_BUCKETS + 8

    def total(x):
        return jnp.sum(jnp.sum(x, axis=1, keepdims=True), axis=0, keepdims=True)

    def body(db_ref, ds_ref, bk_ref, o_ref):
        bk = bk_ref[...]
        r = lax.broadcasted_iota(jnp.int32, (rows, BLOCK), 0)
        c = lax.broadcasted_iota(jnp.int32, (rows, BLOCK), 1)
        out = jnp.zeros((rows, BLOCK), F32)
        for h in range(SWA_HEADS):
            db = db_ref[h]
            for b in range(REL_BUCKETS):
                s = total(jnp.where(bk == b, db, 0.0))
                out = jnp.where(jnp.logical_and(r == b, c == h), s, out)
            s = jnp.sum(ds_ref[h][0:1, :], axis=1, keepdims=True)
            out = jnp.where(jnp.logical_and(r == REL_BUCKETS, c == h), s, out)
        o_ref[...] = out

    vm = pl.BlockSpec(memory_space=pltpu.VMEM)
    return pl.pallas_call(body, name="swa_small_grads", in_specs=[vm, vm, vm], out_specs=vm,
                          out_shape=_sds((rows, BLOCK), F32))(dbias, dsink, bucket)


def _tile_rows(n):
    for t in (512, 352, 256, 176, 128, 64, 32, 16, 8):
        if n % t == 0:
            return t
    return n


def _cast_rows(x, dtype, name):
    R, C = x.shape
    tr = _tile_rows(R)

    def body(x_ref, o_ref):
        o_ref[...] = x_ref[...].astype(o_ref.dtype)

    return pl.pallas_call(body, name=name, grid=(R // tr,), in_specs=[_rows(tr, C)], out_specs=_rows(tr, C),
                          out_shape=_sds((R, C), dtype), compiler_params=_cp(("parallel",)))(x)


def _pair_sum(g, recv, c, name):
    n, half, C = recv.shape
    tr = _tile_rows(half)

    def body(c_ref, a_ref, b_ref, o_ref):
        o_ref[...] = (a_ref[0] + b_ref[...]).astype(o_ref.dtype)

    return pl.pallas_call(
        body, name=name,
        grid_spec=pltpu.PrefetchScalarGridSpec(
            num_scalar_prefetch=1, grid=(n, half // tr),
            in_specs=[pl.BlockSpec((1, 1, tr, C), lambda j, i, c_ref: (j, c_ref[0], i, 0)),
                      pl.BlockSpec((1, tr, C), lambda j, i, c_ref: (j, i, 0))],
            out_specs=pl.BlockSpec((1, tr, C), lambda j, i, c_ref: (j, i, 0))),
        out_shape=_sds((n, half, C), _MXU),
        compiler_params=_cp(("parallel", "parallel")))(c.reshape(1), g.reshape(n, 2, half, C), recv)


def _chip_sum(own, recv, me, name):
    n, R, C = recv.shape
    tr = _tile_rows(R)

    def body(me_ref, own_ref, recv_ref, o_ref):
        acc = None
        for j in range(n):
            term = jnp.where(me_ref[0] == j, own_ref[0], recv_ref[j]).astype(F32)
            acc = term if acc is None else acc + term
        o_ref[...] = acc

    return pl.pallas_call(
        body, name=name,
        grid_spec=pltpu.PrefetchScalarGridSpec(
            num_scalar_prefetch=1, grid=(R // tr,),
            in_specs=[pl.BlockSpec((1, tr, C), lambda i, me_ref: (me_ref[0], i, 0)),
                      pl.BlockSpec((n, tr, C), lambda i, me_ref: (0, i, 0))],
            out_specs=pl.BlockSpec((tr, C), lambda i, me_ref: (i, 0))),
        out_shape=_sds((R, C), F32), compiler_params=_cp(("parallel",)))(me.reshape(1), own, recv)


def _adamw_math(w, g, m, v):
    m = ADAM_B1 * m + (1.0 - ADAM_B1) * g
    v = ADAM_B2 * v + (1.0 - ADAM_B2) * (g * g)
    m_hat = m / (1.0 - ADAM_B1 ** ADAM_STEP)
    v_hat = v / (1.0 - ADAM_B2 ** ADAM_STEP)
    delta = -ADAM_LR * (m_hat / (jnp.sqrt(v_hat) + ADAM_EPS) + ADAM_WD * w)
    return delta, m, v


def _adamw(w, g, m, v, name):
    R, C = w.shape
    tr = _tile_rows(R)

    def body(w_ref, g_ref, m_ref, v_ref, d_ref, nm_ref, nv_ref):
        d, nm, nv = _adamw_math(w_ref[...], g_ref[...], m_ref[...], v_ref[...])
        d_ref[...] = d
        nm_ref[...] = nm
        nv_ref[...] = nv

    blk = _rows(tr, C)
    return pl.pallas_call(body, name=name, grid=(R // tr,), in_specs=[blk] * 4, out_specs=[blk] * 3,
                          out_shape=[_sds((R, C), F32)] * 3, compiler_params=_cp(("parallel",)))(w, g, m, v)


def _gather_weights(shards):
    nw = len(shards)
    ins, outs, sems = _gather_io(shards)

    def body(*refs):
        ex = _Gather(refs[:nw], refs[nw:2 * nw], *refs[2 * nw:])
        ex.start()
        ex.forward()
        ex.finish()

    any_spec = pl.BlockSpec(memory_space=pl.ANY)
    got = pl.pallas_call(body, name="gather_weights", in_specs=[any_spec] * nw, out_specs=[any_spec] * nw,
                         out_shape=outs, scratch_shapes=sems)(*ins)
    return _gather_assemble(got, shards)


def _swap_halves(grads, name):
    nw = len(grads)
    ins, outs, sems = _swap_io(grads)

    def body(*refs):
        ex = _Swap(refs[:nw], refs[nw:2 * nw], *refs[2 * nw:])
        ex.start()
        ex.finish()

    any_spec = pl.BlockSpec(memory_space=pl.ANY)
    return pl.pallas_call(body, name=name, in_specs=[any_spec] * nw, out_specs=[any_spec] * nw,
                          out_shape=outs, scratch_shapes=sems)(*ins)


def _scatter_partials(parts):
    nw = len(parts)
    ins, outs, sems = _scatter_io(parts)

    def body(*refs):
        ex = _Scatter(refs[:nw], refs[nw:2 * nw], *refs[2 * nw:])
        ex.start()
        ex.finish()

    any_spec = pl.BlockSpec(memory_space=pl.ANY)
    return pl.pallas_call(body, name="scatter_partials", in_specs=[any_spec] * nw, out_specs=[any_spec] * nw,
                          out_shape=outs, scratch_shapes=sems)(*ins)


def _join_halves(sums):
    nw = len(sums)

    def body(*refs):
        f_refs, out_refs = refs[:nw], refs[nw:2 * nw]
        send_sems, recv_sems = refs[2 * nw:]
        x, y, c, _ = _place()
        ws = range(nw)

        def copy(w, half_index):
            return pltpu.make_async_remote_copy(
                src_ref=f_refs[w], dst_ref=out_refs[w].at[half_index], send_sem=send_sems.at[w],
                recv_sem=recv_sems.at[w], device_id=(x, y, 1 - c), device_id_type=MESH)

        sends = [copy(w, c) for w in ws]
        for cp in sends:
            cp.start()
        for w in ws:
            copy(w, 1 - c).wait_recv()
        for cp in sends:
            cp.wait_send()

    any_spec = pl.BlockSpec(memory_space=pl.ANY)
    outs = pl.pallas_call(
        body, name="join_halves", in_specs=[any_spec] * nw, out_specs=[any_spec] * nw,
        out_shape=[_sds((2,) + f.shape, f.dtype) for f in sums],
        scratch_shapes=[pltpu.SemaphoreType.DMA((nw,)), pltpu.SemaphoreType.DMA((nw,))],
    )(*sums)
    c = lax.axis_index("c")
    return [lax.dynamic_update_slice_in_dim(o, f[None], c, axis=0).reshape(2 * f.shape[0], f.shape[1])
            for o, f in zip(outs, sums)]


def _allreduce_small(block):
    m_per, n = block.shape

    def body(x_ref, sum_ref, loss_ref, all_ref, send_sems, recv_sems, local_sem):
        x, y, c, chips = _place()
        me, sibling = (x, y, c), (x, y, 1 - c)

        def rows(px, py, pc):
            return all_ref.at[pl.ds(pl.multiple_of((4 * px + 2 * py + pc) * m_per, 8), m_per), :]

        def copy(k, blk, to, src=None):
            return pltpu.make_async_remote_copy(
                src_ref=rows(*blk) if src is None else src, dst_ref=rows(*blk), send_sem=send_sems.at[k],
                recv_sem=recv_sems.at[k], device_id=to, device_id_type=MESH)

        mine = pltpu.make_async_copy(x_ref, rows(*me), local_sem)
        mine.start()
        first = [copy(0, me, sibling, src=x_ref)]
        first += [copy(1 + j, me, (*chip, c), src=x_ref) for j, chip in enumerate(chips)]
        for cp in first:
            cp.start()
        passed = [copy(4 + j, (*chip, c), sibling) for j, chip in enumerate(chips)]
        for j, chip in enumerate(chips):
            copy(1 + j, (*chip, c), me).wait_recv()
            passed[j].start()
        copy(0, sibling, me).wait_recv()
        for j, chip in enumerate(chips):
            copy(4 + j, (*chip, 1 - c), me).wait_recv()
        for cp in first + passed:
            cp.wait_send()
        mine.wait()

        acc = all_ref[0:m_per, :]
        for d in range(1, 8):
            acc = acc + all_ref[d * m_per:(d + 1) * m_per, :]
        sum_ref[...] = acc
        tot = jnp.sum(acc[8:9, :], axis=1, keepdims=True) * (0.5 / D_MODEL)
        loss_ref[...] = jnp.broadcast_to(tot, loss_ref.shape)

    vm = pl.BlockSpec(memory_space=pltpu.VMEM)
    return pl.pallas_call(
        body, name="allreduce_small", in_specs=[vm], out_specs=[vm, vm],
        out_shape=[_sds((m_per, n), F32), _sds((8, 128), F32)],
        scratch_shapes=[pltpu.VMEM((8 * m_per, n), F32), pltpu.SemaphoreType.DMA((7,)), pltpu.SemaphoreType.DMA((7,)),
                        pltpu.SemaphoreType.DMA],
    )(block)


def _heads_rows(x, nh):
    S = x.shape[0]
    return x.reshape(S, nh, HEAD_DIM).transpose(1, 0, 2)


def _heads_cols(x, nh):
    S = x.shape[0]
    return x.reshape(S, nh, HEAD_DIM).transpose(1, 2, 0)


def _pad_row(v):
    v = v.reshape(1, -1)
    return jnp.pad(v, ((0, 0), (0, D_MODEL - v.shape[1])))


def _pack_small(ln_in_g, ln_in_b, sb_g, swa_g, sinks, rel_bias, ln1_g, ln1_b, ln2_g, ln2_b, extra):
    rows = [_pad_row(ln_in_g), _pad_row(ln_in_b), jnp.concatenate([sb_g.reshape(1, -1), swa_g.reshape(1, -1)], axis=1),
            _pad_row(jnp.concatenate([rel_bias.reshape(1, -1), sinks.reshape(1, -1)], axis=1)),
            _pad_row(ln1_g), _pad_row(ln1_b), _pad_row(ln2_g), _pad_row(ln2_b), _pad_row(extra)]
    rows.append(jnp.zeros((SMALL_ROWS - len(rows), D_MODEL), F32))
    return jnp.concatenate(rows, axis=0)


def _unpack_small(blk):
    nrb = REL_BUCKETS * SWA_HEADS
    return (blk[0], blk[1], blk[2:3, :SB_WIDTH], blk[2:3, SB_WIDTH:], blk[3:4, nrb:nrb + SWA_HEADS],
            blk[3, :nrb].reshape(REL_BUCKETS, SWA_HEADS), blk[4:5], blk[5:6], blk[6:7], blk[7:8])


def kernel(x, ln_in_g, ln_in_b, w_in, sb_norm_g, swa_norm_g, sinks, rel_bias, w_out, ln1_g, ln1_b, w_gate_up, w_down, ln2_g, ln2_b, loss_target, m_ln_in_g, m_ln_in_b, m_w_in, m_sb_norm_g, m_swa_norm_g, m_sinks, m_rel_bias, m_w_out, m_ln1_g, m_ln1_b, m_w_gate_up, m_w_down, m_ln2_g, m_ln2_b, v_ln_in_g, v_ln_in_b, v_w_in, v_sb_norm_g, v_swa_norm_g, v_sinks, v_rel_bias, v_w_out, v_ln1_g, v_ln1_b, v_w_gate_up, v_w_down, v_ln2_g, v_ln2_b):
    S = x.shape[1]
    x2 = x.reshape(S, D_MODEL)
    tgt = loss_target.reshape(S, D_MODEL)
    T = min(S, SB_TILE)
    bucket = jnp.asarray(_bucket_table().T)
    row = lambda v: v.reshape(1, -1)

    shards = [_cast_rows(w[0], _MXU, "cast_" + n) for n, w in (("w_in", w_in), ("w_out", w_out), ("w_gate_up", w_gate_up), ("w_down", w_down))]
    (w_in_sh,) = _gather_weights(shards[:1])
    w_in_f = jnp.concatenate([w_in_sh[j] for j in range(N_CHIPS)], axis=1)

    h0, h0b, q_sw, kv_sw, qT_sb, kTb_sb, vTb_sb, kb_sb, vb_sb = _ln_in_proj(x2, row(ln_in_g), row(ln_in_b), w_in_f)
    k_sw, v_sw = kv_sw[:, :SWA_KV_WIDTH], kv_sw[:, SWA_KV_WIDTH:]
    sb_out, rsave, (w_out_sh, w_gu_sh, w_down_sh) = _sb_fwd(qT_sb, kb_sb, vTb_sb, shards[1:])
    w_out_f = w_out_sh.reshape(D_MODEL, D_MODEL)
    w_down_f = w_down_sh.reshape(D_FF, D_MODEL)

    bias = _swa_bias(rel_bias, bucket)
    sink_rows = jnp.broadcast_to(sinks.reshape(SWA_HEADS, 1, 1), (SWA_HEADS, 1, BLOCK))
    qh_sw, kh_sw, vh_sw = _heads_rows(q_sw, SWA_HEADS), _heads_rows(k_sw, SWA_KV_HEADS), _heads_rows(v_sw, SWA_KV_HEADS)
    swa_out = _swa_fwd(qh_sw, kh_sw, _heads_cols(v_sw, SWA_KV_HEADS), bias, sink_rows)

    pre1, merged, h1b = _mix_out(sb_out, swa_out, sb_norm_g, swa_norm_g, w_out_f, h0, ln1_g, ln1_b)
    act, silu, dsilu_up = _ffn_up(h1b, w_gu_sh)
    dp2, dp2b, dg2, db2, errsum = _ffn_down_loss(act, w_down_f, pre1, ln1_g, ln1_b, ln2_g, ln2_b, tgt)

    g_w_down = _matmul_tn(act, dp2b, "grad_w_down", FF_CHUNK, D_MODEL)
    dgate, dup = _ffn_down_bwd(dp2b, w_down_f, silu, dsilu_up)
    g_w_gu = _matmul_tn_pair(h1b, dgate, dup, "grad_w_gate_up")
    dp1, dp1b, dg1, db1 = _ffn_up_bwd(dgate, dup, w_gu_sh, dp2, pre1, ln1_g)
    g_w_out = _matmul_tn(merged, dp1b, "grad_w_out", D_MODEL, D_MODEL)
    doT_sb, dsw, dgsb, dgsw = _mix_bwd(dp1b, w_out_f, sb_out, swa_out, sb_norm_g, swa_norm_g)

    c = lax.axis_index("c").astype(jnp.int32)
    me = (2 * lax.axis_index("x") + lax.axis_index("y")).astype(jnp.int32)
    grads_a = [g_w_out.reshape(N_CHIPS, D_MODEL // N_CHIPS, D_MODEL), g_w_gu, g_w_down.reshape(N_CHIPS, D_FF // N_CHIPS, D_MODEL)]
    names_a = ("w_out", "w_gate_up", "w_down")
    dq_sw, dkh_sw, dvh_sw, dbias, dsink, swapped_a = _swa_bwd(qh_sw, kh_sw, _heads_cols(k_sw, SWA_KV_HEADS), vh_sw, bias,
                                                               sink_rows, _heads_rows(dsw, SWA_HEADS), grads_a)
    swa_small = _swa_small_grads(dbias, dsink, bucket)
    partials_a = [_pair_sum(g, r, c, "pair_sum_" + n) for g, r, n in zip(grads_a, swapped_a, names_a)]
    dq_sb, dk_sb, dv_sb, recv_a = _sb_bwd(qT_sb, kb_sb, kTb_sb, vb_sb,
                                             doT_sb, rsave, partials_a)
    tok = lambda t, nh: t.reshape(nh, S, HEAD_DIM).transpose(1, 0, 2).reshape(S, nh * HEAD_DIM)
    dproj = [dq_sb, dk_sb, dv_sb, dq_sw,
             jnp.concatenate([tok(dkh_sw, SWA_KV_HEADS), tok(dvh_sw, SWA_KV_HEADS)], axis=1).astype(_MXU)]
    g_w_in = jnp.concatenate([_matmul_tn(h0b, d, "grad_w_in_%d" % k, D_MODEL, d.shape[1]) for k, d in enumerate(dproj)],
                             axis=1)

    cin = IN_COLS // N_CHIPS
    grads_b = [jnp.stack([g_w_in[:, j * cin:(j + 1) * cin] for j in range(N_CHIPS)])]
    partials_b = [_pair_sum(grads_b[0], _swap_halves(grads_b, "swap_halves_in")[0], c, "pair_sum_w_in")]
    grad_x, dg_in, db_in, recv_b = _in_proj_bwd(dproj, w_in_f, dp1, x2, row(ln_in_g), partials_b)
    names = ("w_in",) + names_a
    sums = [_chip_sum(p, r, me, "chip_sum_" + n) for p, r, n in zip(partials_b + partials_a, list(recv_b) + list(recv_a), names)]
    gs_in, gs_out, gs_gu, gs_down = _join_halves(sums)

    nrb = REL_BUCKETS * SWA_HEADS
    small = _pack_small(dg_in, db_in, dgsb, dgsw, swa_small[REL_BUCKETS, :SWA_HEADS],
                        swa_small[:REL_BUCKETS, :SWA_HEADS], dg1, db1, dg2, db2, errsum)
    g_small, loss_tile = _allreduce_small(small)
    loss = loss_tile[0, 0]

    big = []
    for name, w, g, m, v in (("adamw_w_in", w_in, gs_in, m_w_in, v_w_in), ("adamw_w_out", w_out, gs_out, m_w_out, v_w_out),
                             ("adamw_w_gate_up", w_gate_up, gs_gu, m_w_gate_up, v_w_gate_up),
                             ("adamw_w_down", w_down, gs_down, m_w_down, v_w_down)):
        d, nm, nv = _adamw(w[0], g, m[0], v[0], name)
        big.append((g[None], d[None], nm[None], nv[None]))
    zero = jnp.zeros((1,), F32)
    w_small = _pack_small(ln_in_g, ln_in_b, sb_norm_g, swa_norm_g, sinks, rel_bias, ln1_g, ln1_b, ln2_g, ln2_b, zero)
    m_small = _pack_small(m_ln_in_g, m_ln_in_b, m_sb_norm_g, m_swa_norm_g, m_sinks, m_rel_bias, m_ln1_g, m_ln1_b,
                          m_ln2_g, m_ln2_b, zero)
    v_small = _pack_small(v_ln_in_g, v_ln_in_b, v_sb_norm_g, v_swa_norm_g, v_sinks, v_rel_bias, v_ln1_g, v_ln1_b,
                          v_ln2_g, v_ln2_b, zero)
    small_out = [_unpack_small(t) for t in (g_small,) + tuple(_adamw(w_small, g_small, m_small, v_small, "adamw_small"))]

    def kind(k):
        s = small_out[k]
        return [s[0], s[1], big[0][k], s[2], s[3], s[4], s[5], big[1][k], s[6], s[7], big[2][k], big[3][k], s[8], s[9]]

    return (loss, grad_x.reshape(1, S, D_MODEL), *kind(0), *kind(1), *kind(2), *kind(3))
```

```python
import math

import numpy as np
import jax
import jax.numpy as jnp
from jax import lax
from jax.experimental import pallas as pl
from jax.experimental.pallas import tpu as pltpu

F32 = jnp.float32
_MXU = jnp.bfloat16

D_MODEL = 1024
HEAD_DIM = 64
SB_HEADS = 8
SWA_HEADS = 8
SWA_KV_HEADS = 2
SWA_GROUP = SWA_HEADS // SWA_KV_HEADS
SB_WIDTH = SB_HEADS * HEAD_DIM
SWA_WIDTH = SWA_HEADS * HEAD_DIM
SWA_KV_WIDTH = SWA_KV_HEADS * HEAD_DIM
IN_COLS = 3 * SB_WIDTH + SWA_WIDTH + 2 * SWA_KV_WIDTH
BLOCK = 128
REL_BUCKETS = 32
REL_MAX_DIST = 128
D_FF = 2816
FF_CHUNK = D_FF // 2
ALPHA = 2.0 ** 0.25
LN_EPS = 1e-5
RMS_EPS = 1e-6
SCALE = HEAD_DIM ** -0.5
SB_TILE = 256
SB_GROUP_FWD = 8
SB_GROUP_BWD = 4
SB_FORWARD_LEAD = 8
SB_DEAD = -105.0
SWA_SUB = 8

ADAM_LR = 0.001
ADAM_B1 = 0.9
ADAM_B2 = 0.999
ADAM_EPS = 1e-08
ADAM_WD = 0.01
ADAM_STEP = 10

N_CHIPS = 4
SMALL_ROWS = 16

MESH = pl.DeviceIdType.MESH


def _sds(shape, dtype):
    return jax.ShapeDtypeStruct(shape, dtype)


def _cp(sem=None, vmem_mb=48):
    kw = dict(vmem_limit_bytes=vmem_mb * 1024 * 1024)
    if sem is not None:
        kw["dimension_semantics"] = sem
    return pltpu.CompilerParams(**kw)


def _dot(a, b):
    return jnp.dot(a, b, preferred_element_type=F32)


def _dot_nt(a, b):
    return lax.dot_general(a, b, (((1,), (1,)), ((), ())), preferred_element_type=F32)


def _dot_tn(a, b):
    return lax.dot_general(a, b, (((0,), (0,)), ((), ())), preferred_element_type=F32)


def _ln_hat(x):
    mu = jnp.mean(x, axis=-1, keepdims=True)
    xc = x - mu
    var = jnp.mean(xc * xc, axis=-1, keepdims=True)
    rstd = lax.rsqrt(var + LN_EPS)
    return xc * rstd, rstd


def _ln_bwd(xhat, rstd, dy, g):
    dxh = dy * g
    m1 = jnp.mean(dxh, axis=-1, keepdims=True)
    m2 = jnp.mean(dxh * xhat, axis=-1, keepdims=True)
    return rstd * (dxh - m1 - xhat * m2)


def _colsum(x):
    return jnp.sum(x, axis=0, keepdims=True)


def _split2(x):
    hi = x.astype(_MXU)
    lo = (x - hi.astype(F32)).astype(_MXU)
    return hi, lo


def _rows(tm, n):
    return pl.BlockSpec((tm, n), lambda i: (i, 0))


def _fixed(*shape):
    nd = len(shape)
    return pl.BlockSpec(shape, lambda i: (0,) * nd)


IN_SECTIONS = (SB_WIDTH, SB_WIDTH, SB_WIDTH, SWA_WIDTH, 2 * SWA_KV_WIDTH)


def _ln_in_proj(x, g, b, w):
    S = x.shape[0]
    tm = min(S, SB_TILE)
    offs = np.cumsum((0,) + IN_SECTIONS)
    swa = (3, 4)

    def body(x_ref, g_ref, b_ref, w_ref, h_ref, hb_ref, *o_refs):
        p_refs, (qT_ref, kT_ref, vT_ref, kr_ref, vr_ref) = o_refs[:len(swa)], o_refs[len(swa):]
        xhat, _ = _ln_hat(x_ref[...])
        h = xhat * g_ref[...] + b_ref[...]
        h_ref[...] = h
        hb = h.astype(_MXU)
        hb_ref[...] = hb
        proj = _dot(hb, w_ref[...])
        for k, p_ref in zip(swa, p_refs):
            p_ref[...] = proj[:, offs[k]:offs[k + 1]].astype(p_ref.dtype)
        heads = lambda k: proj[:, offs[k]:offs[k + 1]].T.astype(_MXU).reshape(SB_HEADS, HEAD_DIM, tm)
        qT_ref[...] = heads(0)
        kT_ref[:, 0] = heads(1)
        vT_ref[:, 0] = heads(2)
        for hd in range(SB_HEADS):
            cols = slice(hd * HEAD_DIM, (hd + 1) * HEAD_DIM)
            kr_ref[hd, 0] = proj[:, offs[1]:offs[2]][:, cols].astype(_MXU)
            vr_ref[hd, 0] = proj[:, offs[2]:offs[3]][:, cols].astype(_MXU)

    blocked = pl.BlockSpec((SB_HEADS, 1, HEAD_DIM, tm), lambda i: (0, i, 0, 0))
    blocked_rows = pl.BlockSpec((SB_HEADS, 1, tm, HEAD_DIM), lambda i: (0, i, 0, 0))
    return pl.pallas_call(
        body, name="ln_in_proj", grid=(S // tm,),
        in_specs=[_rows(tm, D_MODEL), _fixed(1, D_MODEL), _fixed(1, D_MODEL), _fixed(D_MODEL, IN_COLS)],
        out_specs=[_rows(tm, D_MODEL), _rows(tm, D_MODEL)] + [_rows(tm, IN_SECTIONS[k]) for k in swa]
                  + [pl.BlockSpec((SB_HEADS, HEAD_DIM, tm), lambda i: (0, 0, i)), blocked, blocked, blocked_rows,
                     blocked_rows],
        out_shape=[_sds((S, D_MODEL), F32), _sds((S, D_MODEL), _MXU)] + [_sds((S, IN_SECTIONS[k]), _MXU) for k in swa]
                  + [_sds((SB_HEADS, HEAD_DIM, S), _MXU), _sds((SB_HEADS, S // tm, HEAD_DIM, tm), _MXU),
                     _sds((SB_HEADS, S // tm, HEAD_DIM, tm), _MXU), _sds((SB_HEADS, S // tm, tm, HEAD_DIM), _MXU),
                     _sds((SB_HEADS, S // tm, tm, HEAD_DIM), _MXU)],
        compiler_params=_cp(("parallel",)),
    )(x, g, b, w)


def _rms(x, g):
    r = lax.rsqrt(jnp.mean(x * x, axis=-1, keepdims=True) + RMS_EPS)
    return x * r * g, r


def _mix_out(sb, sw, gsb, gsw, w_out, h0, g1, b1):
    S = sb.shape[0]
    tm = min(S, 512)

    def body(sb_ref, sw_ref, gsb_ref, gsw_ref, w_ref, h0_ref, g1_ref, b1_ref, pre_ref, mg_ref, h1_ref):
        ysb, _ = _rms(sb_ref[...], gsb_ref[...])
        ysw, _ = _rms(sw_ref[...], gsw_ref[...])
        ysb = ysb.astype(_MXU)
        ysw = ysw.astype(_MXU)
        mg_ref[:, :SB_WIDTH] = ysb
        mg_ref[:, SB_WIDTH:] = ysw
        mix = _dot(ysb, w_ref[:SB_WIDTH, :]) + _dot(ysw, w_ref[SB_WIDTH:, :])
        pre1 = ALPHA * h0_ref[...] + mix
        pre_ref[...] = pre1
        xhat, _ = _ln_hat(pre1)
        h1_ref[...] = (xhat * g1_ref[...] + b1_ref[...]).astype(h1_ref.dtype)

    vec = _fixed(1, D_MODEL)
    return pl.pallas_call(
        body, name="mix_out", grid=(S // tm,),
        in_specs=[_rows(tm, SB_WIDTH), _rows(tm, SWA_WIDTH), _fixed(1, SB_WIDTH), _fixed(1, SWA_WIDTH),
                  _fixed(D_MODEL, D_MODEL), _rows(tm, D_MODEL), vec, vec],
        out_specs=[_rows(tm, D_MODEL), _rows(tm, D_MODEL), _rows(tm, D_MODEL)],
        out_shape=[_sds((S, D_MODEL), F32), _sds((S, D_MODEL), _MXU), _sds((S, D_MODEL), _MXU)],
        compiler_params=_cp(("parallel",)),
    )(sb, sw, gsb, gsw, w_out, h0, g1, b1)


def _sigmoid(x):
    return 1.0 / (1.0 + jnp.exp(-x))


def _ffn_up(h1b, wgu):
    S = h1b.shape[0]
    tm = min(S, 1024)

    def body(h_ref, wg_ref, wu_ref, a_ref, s1_ref, s2_ref):
        h1 = h_ref[...]
        gate = _dot(h1, wg_ref[0])
        up = _dot(h1, wu_ref[0])
        sg = _sigmoid(gate)
        silu = gate * sg
        a_ref[...] = (silu * up).astype(a_ref.dtype)
        s1_ref[...] = silu.astype(s1_ref.dtype)
        s2_ref[...] = (up * (sg * (1.0 + gate * (1.0 - sg)))).astype(s2_ref.dtype)

    chunk = pl.BlockSpec((tm, FF_CHUNK), lambda j, i: (i, j))
    return pl.pallas_call(
        body, name="ffn_up", grid=(2, S // tm),
        in_specs=[pl.BlockSpec((tm, D_MODEL), lambda j, i: (i, 0)),
                  pl.BlockSpec((1, D_MODEL, FF_CHUNK), lambda j, i: (j, 0, 0)),
                  pl.BlockSpec((1, D_MODEL, FF_CHUNK), lambda j, i: (j + 2, 0, 0))],
        out_specs=[chunk, chunk, chunk],
        out_shape=[_sds((S, D_FF), _MXU)] * 3,
        compiler_params=_cp(("arbitrary", "arbitrary"), vmem_mb=56),
    )(h1b, wgu, wgu)


def _ffn_down_loss(a, w_down, pre1, g1, b1, g2, b2, tgt):
    S = a.shape[0]
    tm = min(S, 512)

    def body(a_ref, w_ref, p_ref, g1_ref, b1_ref, g2_ref, b2_ref, t_ref, d_ref, db_ref, dg2_ref, db2_ref, err_ref):
        @pl.when(pl.program_id(0) == 0)
        def _():
            dg2_ref[...] = jnp.zeros_like(dg2_ref)
            db2_ref[...] = jnp.zeros_like(db2_ref)
            err_ref[...] = jnp.zeros_like(err_ref)

        xhat1, _ = _ln_hat(p_ref[...])
        h1 = xhat1 * g1_ref[...] + b1_ref[...]
        pre2 = ALPHA * h1 + _dot(a_ref[...], w_ref[...])
        xhat2, rstd2 = _ln_hat(pre2)
        err = xhat2 * g2_ref[...] + b2_ref[...] - t_ref[...]
        dh2 = err * (1.0 / D_MODEL)
        dp2 = _ln_bwd(xhat2, rstd2, dh2, g2_ref[...])
        d_ref[...] = dp2
        db_ref[...] = dp2.astype(db_ref.dtype)
        dg2_ref[...] += _colsum(dh2 * xhat2)
        db2_ref[...] += _colsum(dh2)
        err_ref[...] += _colsum(err * err)

    vec = _fixed(1, D_MODEL)
    return pl.pallas_call(
        body, name="ffn_down_loss", grid=(S // tm,),
        in_specs=[_rows(tm, D_FF), _fixed(D_FF, D_MODEL), _rows(tm, D_MODEL), vec, vec, vec, vec, _rows(tm, D_MODEL)],
        out_specs=[_rows(tm, D_MODEL), _rows(tm, D_MODEL), vec, vec, vec],
        out_shape=[_sds((S, D_MODEL), F32), _sds((S, D_MODEL), _MXU), _sds((1, D_MODEL), F32), _sds((1, D_MODEL), F32),
                   _sds((1, D_MODEL), F32)],
        compiler_params=_cp(("arbitrary",)),
    )(a, w_down, pre1, g1, b1, g2, b2, tgt)


def _ffn_down_bwd(dp2b, w_down, s1, s2):
    S = dp2b.shape[0]
    tm = min(S, 1024)

    def body(d_ref, w_ref, s1_ref, s2_ref, dg_ref, du_ref):
        da = _dot_nt(d_ref[...], w_ref[...])
        du_ref[...] = (da * s1_ref[...].astype(F32)).astype(du_ref.dtype)
        dg_ref[...] = (da * s2_ref[...].astype(F32)).astype(dg_ref.dtype)

    chunk = pl.BlockSpec((tm, FF_CHUNK), lambda j, i: (i, j))
    return pl.pallas_call(
        body, name="ffn_down_bwd", grid=(2, S // tm),
        in_specs=[pl.BlockSpec((tm, D_MODEL), lambda j, i: (i, 0)),
                  pl.BlockSpec((FF_CHUNK, D_MODEL), lambda j, i: (j, 0)), chunk, chunk],
        out_specs=[chunk, chunk],
        out_shape=[_sds((S, D_FF), _MXU), _sds((S, D_FF), _MXU)],
        compiler_params=_cp(("arbitrary", "arbitrary")),
    )(dp2b, w_down, s1, s2)


def _ffn_up_bwd(dgate, dup, wgu, dp2, pre1, g1):
    S = dgate.shape[0]
    tm = min(S, 256)

    def body(dg_ref, du_ref, w_ref, d2_ref, p_ref, g_ref, d1_ref, d1b_ref, dg1_ref, db1_ref):
        @pl.when(pl.program_id(0) == 0)
        def _():
            dg1_ref[...] = jnp.zeros_like(dg1_ref)
            db1_ref[...] = jnp.zeros_like(db1_ref)

        dh1 = ALPHA * d2_ref[...]
        for j in range(2):
            cols = slice(j * FF_CHUNK, (j + 1) * FF_CHUNK)
            dh1 += _dot_nt(dg_ref[:, cols], w_ref[j])
            dh1 += _dot_nt(du_ref[:, cols], w_ref[j + 2])
        xhat, rstd = _ln_hat(p_ref[...])
        dp1 = _ln_bwd(xhat, rstd, dh1, g_ref[...])
        d1_ref[...] = dp1
        d1b_ref[...] = dp1.astype(d1b_ref.dtype)
        dg1_ref[...] += _colsum(dh1 * xhat)
        db1_ref[...] += _colsum(dh1)

    vec = _fixed(1, D_MODEL)
    return pl.pallas_call(
        body, name="ffn_up_bwd", grid=(S // tm,),
        in_specs=[_rows(tm, D_FF), _rows(tm, D_FF), _fixed(4, D_MODEL, FF_CHUNK), _rows(tm, D_MODEL),
                  _rows(tm, D_MODEL), vec],
        out_specs=[_rows(tm, D_MODEL), _rows(tm, D_MODEL), vec, vec],
        out_shape=[_sds((S, D_MODEL), F32), _sds((S, D_MODEL), _MXU), _sds((1, D_MODEL), F32), _sds((1, D_MODEL), F32)],
        compiler_params=_cp(("arbitrary",), vmem_mb=56),
    )(dgate, dup, wgu, dp2, pre1, g1)


def _rms_bwd(x, g, dy):
    n = x.shape[-1]
    r = lax.rsqrt(jnp.mean(x * x, axis=-1, keepdims=True) + RMS_EPS)
    u = dy * g
    dx = r * u - x * (r * r * r) * (jnp.sum(u * x, axis=-1, keepdims=True) * (1.0 / n))
    return dx, _colsum(dy * x * r)


def _mix_bwd(dp1b, w_out, sb, sw, gsb, gsw):
    S = sb.shape[0]
    tm = min(S, 512)

    def body(d_ref, w_ref, sb_ref, sw_ref, gsb_ref, gsw_ref, dsb_ref, dsw_ref, dgsb_ref, dgsw_ref):
        @pl.when(pl.program_id(0) == 0)
        def _():
            dgsb_ref[...] = jnp.zeros_like(dgsb_ref)
            dgsw_ref[...] = jnp.zeros_like(dgsw_ref)

        dm = _dot_nt(d_ref[...], w_ref[...])
        dsb, dgsb = _rms_bwd(sb_ref[...], gsb_ref[...], dm[:, :SB_WIDTH])
        dsw, dgsw = _rms_bwd(sw_ref[...], gsw_ref[...], dm[:, SB_WIDTH:])
        dsb_ref[...] = dsb.T.astype(dsb_ref.dtype).reshape(dsb_ref.shape)
        dsw_ref[...] = dsw.astype(dsw_ref.dtype)
        dgsb_ref[...] += dgsb
        dgsw_ref[...] += dgsw

    return pl.pallas_call(
        body, name="mix_bwd", grid=(S // tm,),
        in_specs=[_rows(tm, D_MODEL), _fixed(D_MODEL, D_MODEL), _rows(tm, SB_WIDTH), _rows(tm, SWA_WIDTH),
                  _fixed(1, SB_WIDTH), _fixed(1, SWA_WIDTH)],
        out_specs=[pl.BlockSpec((SB_HEADS, HEAD_DIM, tm), lambda i: (0, 0, i)), _rows(tm, SWA_WIDTH),
                   _fixed(1, SB_WIDTH), _fixed(1, SWA_WIDTH)],
        out_shape=[_sds((SB_HEADS, HEAD_DIM, S), _MXU), _sds((S, SWA_WIDTH), _MXU), _sds((1, SB_WIDTH), F32),
                   _sds((1, SWA_WIDTH), F32)],
        compiler_params=_cp(("arbitrary",)),
    )(dp1b, w_out, sb, sw, gsb, gsw)


def _in_proj_bwd(dproj, w_in, dp1, x, g, parts):
    S = x.shape[0]
    tm = min(S, 512)
    nw = len(parts)
    ns = len(IN_SECTIONS)
    offs = np.cumsum((0,) + IN_SECTIONS)
    s_ins, s_outs, s_sems = _scatter_io(parts)

    def body(*refs):
        dpj_refs = refs[:ns]
        w_ref, d1_ref, x_ref, g_ref = refs[ns:ns + 4]
        rest = refs[ns + 4:]
        gx_ref, dg_ref, db_ref = rest[nw:nw + 3]
        scatter = _Scatter(rest[:nw], rest[nw + 3:2 * nw + 3], *rest[2 * nw + 3:])

        @pl.when(pl.program_id(0) == 0)
        def _():
            scatter.start()
            dg_ref[...] = jnp.zeros_like(dg_ref)
            db_ref[...] = jnp.zeros_like(db_ref)

        dh0 = ALPHA * d1_ref[...]
        for k in range(ns):
            dh0 += _dot_nt(dpj_refs[k][...], w_ref[:, offs[k]:offs[k + 1]])
        xhat, rstd = _ln_hat(x_ref[...])
        gx_ref[...] = _ln_bwd(xhat, rstd, dh0, g_ref[...])
        dg_ref[...] += _colsum(dh0 * xhat)
        db_ref[...] += _colsum(dh0)

        @pl.when(pl.program_id(0) == pl.num_programs(0) - 1)
        def _():
            scatter.finish()

    vec = _fixed(1, D_MODEL)
    any_spec = pl.BlockSpec(memory_space=pl.ANY)
    res = pl.pallas_call(
        body, name="in_proj_bwd", grid=(S // tm,),
        in_specs=[_rows(tm, n) for n in IN_SECTIONS]
                 + [_fixed(D_MODEL, IN_COLS), _rows(tm, D_MODEL), _rows(tm, D_MODEL), vec] + [any_spec] * nw,
        out_specs=[_rows(tm, D_MODEL), vec, vec] + [any_spec] * nw,
        out_shape=[_sds((S, D_MODEL), F32), _sds((1, D_MODEL), F32), _sds((1, D_MODEL), F32)] + s_outs,
        scratch_shapes=s_sems,
        compiler_params=_cp(("arbitrary",)),
    )(*dproj, w_in, dp1, x, g, *s_ins)
    return res[0], res[1], res[2], list(res[3:])


def _matmul_tn(a, b, name, tk, tn):
    T, K = a.shape
    N = b.shape[1]
    tt = min(T, 1024)

    def body(a_ref, b_ref, o_ref):
        @pl.when(pl.program_id(2) == 0)
        def _():
            o_ref[...] = jnp.zeros_like(o_ref)

        o_ref[...] += _dot_tn(a_ref[...], b_ref[...])

    return pl.pallas_call(
        body, name=name, grid=(K // tk, N // tn, T // tt),
        in_specs=[pl.BlockSpec((tt, tk), lambda k, n, t: (t, k)), pl.BlockSpec((tt, tn), lambda k, n, t: (t, n))],
        out_specs=pl.BlockSpec((tk, tn), lambda k, n, t: (k, n)),
        out_shape=_sds((K, N), F32),
        compiler_params=_cp(("parallel", "parallel", "arbitrary")),
    )(a, b)


def _place():
    x, y, c = lax.axis_index("x"), lax.axis_index("y"), lax.axis_index("c")
    chips = [(1 - x, y), (x, 1 - y), (1 - x, 1 - y)]
    return x, y, c, chips


class _Gather:
    def __init__(self, in_refs, out_refs, send_sems, recv_sems):
        self.in_refs, self.out_refs, self.send_sems, self.recv_sems = in_refs, out_refs, send_sems, recv_sems
        self.x, self.y, self.c, self.chips = _place()

    def _copy(self, w, k, chip, hc, to, src=None):
        part = self.out_refs[w].at[2 * chip[0] + chip[1], hc]
        return pltpu.make_async_remote_copy(
            src_ref=part if src is None else src, dst_ref=part, send_sem=self.send_sems.at[w, k],
            recv_sem=self.recv_sems.at[w, k], device_id=to, device_id_type=MESH)

    def _first(self):
        x, y, c = self.x, self.y, self.c
        return [self._copy(w, j, (x, y), c, (*chip, c), src=self.in_refs[w].at[c])
                for w in range(len(self.in_refs)) for j, chip in enumerate(self.chips)]

    def start(self):
        for cp in self._first():
            cp.start()

    def _passed(self):
        sibling = (self.x, self.y, 1 - self.c)
        return [self._copy(w, 3 + j, chip, self.c, sibling)
                for w in range(len(self.in_refs)) for j, chip in enumerate(self.chips)]

    def forward(self):
        me = (self.x, self.y, self.c)
        passed = self._passed()
        for w in range(len(self.in_refs)):
            for j, chip in enumerate(self.chips):
                self._copy(w, j, chip, self.c, me).wait_recv()
                passed[3 * w + j].start()

    def finish(self):
        me = (self.x, self.y, self.c)
        for w in range(len(self.in_refs)):
            for j, chip in enumerate(self.chips):
                self._copy(w, 3 + j, chip, 1 - self.c, me).wait_recv()
        for cp in self._first() + self._passed():
            cp.wait_send()


def _gather_io(shards):
    halves = [(s.shape[0] // 2, s.shape[1]) for s in shards]
    ins = [s.reshape(2, h, cols) for s, (h, cols) in zip(shards, halves)]
    outs = [_sds((N_CHIPS, 2, h, cols), s.dtype) for s, (h, cols) in zip(shards, halves)]
    sems = [pltpu.SemaphoreType.DMA((len(shards), 6)), pltpu.SemaphoreType.DMA((len(shards), 6))]
    return ins, outs, sems


def _gather_assemble(outs, shards):
    me = 2 * lax.axis_index("x") + lax.axis_index("y")
    return [lax.dynamic_update_slice_in_dim(o.reshape((N_CHIPS,) + s.shape), s[None], me, axis=0)
            for o, s in zip(outs, shards)]


class _Scatter:
    def __init__(self, p_refs, out_refs, send_sems, recv_sems):
        self.p_refs, self.out_refs, self.send_sems, self.recv_sems = p_refs, out_refs, send_sems, recv_sems
        self.x, self.y, self.c, self.chips = _place()
        self.me = 2 * self.x + self.y

    def _copy(self, w, j, chip, src_chip, dst_chip):
        return pltpu.make_async_remote_copy(
            src_ref=self.p_refs[w].at[src_chip], dst_ref=self.out_refs[w].at[dst_chip], send_sem=self.send_sems.at[w, j],
            recv_sem=self.recv_sems.at[w, j], device_id=(*chip, self.c), device_id_type=MESH)

    def _sends(self):
        return [self._copy(w, j, chip, 2 * chip[0] + chip[1], self.me)
                for w in range(len(self.p_refs)) for j, chip in enumerate(self.chips)]

    def start(self):
        for cp in self._sends():
            cp.start()

    def finish(self):
        for w in range(len(self.p_refs)):
            for j, chip in enumerate(self.chips):
                self._copy(w, j, chip, self.me, 2 * chip[0] + chip[1]).wait_recv()
        for cp in self._sends():
            cp.wait_send()


def _scatter_io(parts):
    sems = [pltpu.SemaphoreType.DMA((len(parts), 3)), pltpu.SemaphoreType.DMA((len(parts), 3))]
    return list(parts), [_sds(p.shape, p.dtype) for p in parts], sems


class _Swap:
    def __init__(self, g_refs, out_refs, send_sems, recv_sems):
        x, y, c, _ = _place()
        self.copies = []
        for w in range(len(g_refs)):
            half = out_refs[w].shape[1]
            theirs = g_refs[w].at[:, pl.ds(pl.multiple_of((1 - c) * half, 8), half), :]
            self.copies.append(pltpu.make_async_remote_copy(
                src_ref=theirs, dst_ref=out_refs[w], send_sem=send_sems.at[w], recv_sem=recv_sems.at[w],
                device_id=(x, y, 1 - c), device_id_type=MESH))

    def start(self):
        for cp in self.copies:
            cp.start()

    def finish(self):
        for cp in self.copies:
            cp.wait()


def _swap_io(grads):
    outs = [_sds((g.shape[0], g.shape[1] // 2, g.shape[2]), g.dtype) for g in grads]
    return list(grads), outs, [pltpu.SemaphoreType.DMA((len(grads),)), pltpu.SemaphoreType.DMA((len(grads),))]


def _matmul_tn_pair(a, b0, b1, name):
    T, K = a.shape
    tt = min(T, 1024)

    def body(a_ref, b0_ref, b1_ref, o_ref):
        n = pl.program_id(0)

        @pl.when(pl.program_id(1) == 0)
        def _():
            o_ref[...] = jnp.zeros_like(o_ref)

        @pl.when(n < 2)
        def _():
            o_ref[0] += _dot_tn(a_ref[...], b0_ref[...])

        @pl.when(n >= 2)
        def _():
            o_ref[0] += _dot_tn(a_ref[...], b1_ref[...])

    return pl.pallas_call(
        body, name=name, grid=(4, T // tt),
        in_specs=[pl.BlockSpec((tt, K), lambda n, t: (t, 0)),
                  pl.BlockSpec((tt, FF_CHUNK), lambda n, t: (t, jnp.minimum(n, 1))),
                  pl.BlockSpec((tt, FF_CHUNK), lambda n, t: (t, jnp.maximum(n - 2, 0)))],
        out_specs=pl.BlockSpec((1, K, FF_CHUNK), lambda n, t: (n, 0, 0)),
        out_shape=_sds((4, K, FF_CHUNK), F32),
        compiler_params=_cp(("parallel", "arbitrary")),
    )(a, b0, b1)


def _sb_logs(zt, causal):
    e = jnp.exp(-jnp.abs(zt))
    lb = jnp.minimum(zt, 0.0) - jnp.log(1.0 + e)
    l1m = lb - zt
    if causal is not None:
        l1m = jnp.where(causal, l1m, 0.0)
    return lb, l1m


def _sb_weights(lb, suf, causal):
    a = jnp.exp(lb + suf)
    if causal is not None:
        a = jnp.where(causal, a, 0.0)
    return a


def _tri_masks(t):
    r = lax.broadcasted_iota(jnp.int32, (t, t), 0)
    c = lax.broadcasted_iota(jnp.int32, (t, t), 1)
    return r, c


def _sb_fwd(qT, kb, vTb, shards):
    Hh, _, S = qT.shape
    nk, T = kb.shape[1], kb.shape[2]
    nq = S // T
    G = SB_GROUP_FWD
    nw = len(shards)
    g_ins, g_outs, g_sems = _gather_io(shards)
    forward_step = max(nq - 1 - SB_FORWARD_LEAD, 0)

    def body(qT_ref, k_ref, vT_ref, *rest):
        o_ref, rs_ref = rest[nw:nw + 2]
        gather = _Gather(rest[:nw], rest[nw + 2:2 * nw + 2], *rest[2 * nw + 2:])
        i = pl.program_id(1)
        first_step = jnp.logical_and(pl.program_id(0) == 0, i == 0)
        last_step = jnp.logical_and(pl.program_id(0) == pl.num_programs(0) - 1, i == pl.num_programs(1) - 1)

        @pl.when(first_step)
        def _():
            gather.start()

        qts = [(qT_ref[g].astype(F32) * SCALE).astype(_MXU) for g in range(G)]
        r, c = _tri_masks(T)
        upper = (c > r).astype(_MXU)
        causal = r < c

        def blk(j, carry, mask):
            hs = range(G)
            for g in hs:
                rs_ref[g, 0, j] = jnp.broadcast_to(carry[g][0], (8, T))
            zs = [_dot(k_ref[g, j], qts[g]) for g in hs]
            lbs, l1ms = zip(*[_sb_logs(zs[g], mask) for g in hs])
            splits = [_split2(l1ms[g]) for g in hs]
            cums = [_dot(upper, splits[g][0]) + _dot(upper, splits[g][1]) for g in hs]
            avs = [_sb_weights(lbs[g], carry[g][0] + cums[g], mask).astype(_MXU) for g in hs]
            accs = [carry[g][1] + _dot(vT_ref[g, j], avs[g]) for g in hs]
            return tuple((carry[g][0] + _colsum(l1ms[g]), accs[g]) for g in hs)

        def go_on(j, carry):
            top = carry[0][0]
            for g in range(1, G):
                top = jnp.maximum(top, carry[g][0])
            return jnp.logical_and(j >= 0, jnp.max(top) >= SB_DEAD)

        init = tuple((jnp.zeros((1, T), F32), jnp.zeros((HEAD_DIM, T), F32)) for _ in range(G))
        carry = blk(i, init, causal)
        j, carry = lax.while_loop(lambda st: go_on(*st), lambda st: (st[0] - 1, blk(st[0], st[1], None)),
                                  (i - 1, carry))

        @pl.when(j >= 0)
        def _():
            for g in range(G):
                rs_ref[g, 0, j] = jnp.broadcast_to(carry[g][0], (8, T))

        o_ref[...] = jnp.concatenate([carry[g][1] for g in range(G)], axis=0).T

        @pl.when(jnp.logical_and(pl.program_id(0) == pl.num_programs(0) - 1, i == forward_step))
        def _():
            gather.forward()

        @pl.when(last_step)
        def _():
            gather.finish()

    any_spec = pl.BlockSpec(memory_space=pl.ANY)
    res = pl.pallas_call(
        body, name="sb_fwd", grid=(Hh // G, nq),
        in_specs=[pl.BlockSpec((G, HEAD_DIM, T), lambda h, i: (h, 0, i)),
                  pl.BlockSpec((G, nk, T, HEAD_DIM), lambda h, i: (h, 0, 0, 0), pipeline_mode=pl.Buffered(1)),
                  pl.BlockSpec((G, nk, HEAD_DIM, T), lambda h, i: (h, 0, 0, 0), pipeline_mode=pl.Buffered(1))]
                 + [any_spec] * nw,
        out_specs=[pl.BlockSpec((T, G * HEAD_DIM), lambda h, i: (i, h)),
                   pl.BlockSpec((G, 1, nk, 8, T), lambda h, i: (h, i, 0, 0, 0))] + [any_spec] * nw,
        out_shape=[_sds((S, Hh * HEAD_DIM), F32), _sds((Hh, nq, nk, 8, T), F32)] + g_outs,
        scratch_shapes=g_sems,
        compiler_params=_cp(("arbitrary", "arbitrary")),
    )(qT, kb, vTb, *g_ins)
    return res[0], res[1], _gather_assemble(res[2:], shards)


def _sb_bwd(qT, kb, kTb, vb, doT, rsave, parts):
    Hh, _, S = qT.shape
    nk, T = kb.shape[1], kb.shape[2]
    nq = S // T
    G = SB_GROUP_BWD
    nw = len(parts)
    s_ins, s_outs, s_sems = _scatter_io(parts)

    def body(qT_ref, k_ref, kT_ref, v_ref, doT_ref, rs_ref, *rest):
        dq_ref, dk_out_ref, dv_out_ref = rest[nw:nw + 3]
        dk_ref, dv_ref = rest[2 * nw + 3:2 * nw + 5]
        scatter = _Scatter(rest[:nw], rest[nw + 3:2 * nw + 3], *rest[2 * nw + 5:])
        i = pl.program_id(1)
        first_step = jnp.logical_and(pl.program_id(0) == 0, i == 0)
        last_step = jnp.logical_and(pl.program_id(0) == pl.num_programs(0) - 1, i == pl.num_programs(1) - 1)

        @pl.when(first_step)
        def _():
            scatter.start()

        @pl.when(i == 0)
        def _():
            dk_ref[...] = jnp.zeros_like(dk_ref)
            dv_ref[...] = jnp.zeros_like(dv_ref)

        qts = [(qT_ref[g].astype(F32) * SCALE).astype(_MXU) for g in range(G)]
        douts = [doT_ref[g] for g in range(G)]
        r, c = _tri_masks(T)
        upper = (c > r).astype(_MXU)
        lower = (c < r).astype(_MXU)
        causal = r < c

        def blk(j, carry, mask):
            hs = range(G)
            zs = [_dot(k_ref[g, j], qts[g]) for g in hs]
            das = [_dot(v_ref[g, j], douts[g]) for g in hs]
            lbs, l1ms = zip(*[_sb_logs(zs[g], mask) for g in hs])
            splits = [_split2(l1ms[g]) for g in hs]
            cums = [_dot(upper, splits[g][0]) + _dot(upper, splits[g][1]) for g in hs]
            avs = [_sb_weights(lbs[g], rs_ref[g, 0, j][0:1, :] + cums[g], mask) for g in hs]
            ets = [das[g] * avs[g] for g in hs]
            esplits = [_split2(ets[g]) for g in hs]
            ecums = [_dot(lower, esplits[g][0]) + _dot(lower, esplits[g][1]) for g in hs]
            dzs = []
            for g in hs:
                sig = jnp.exp(lbs[g])
                dz = ets[g] * (1.0 - sig) - (carry[g][0] + ecums[g]) * sig
                if mask is not None:
                    dz = jnp.where(mask, dz, 0.0)
                dzs.append(dz.astype(_MXU))
            dqs = [carry[g][1] + _dot(kT_ref[g, j], dzs[g]) for g in hs]
            for g in hs:
                dk_ref[j, g * HEAD_DIM:(g + 1) * HEAD_DIM, :] += _dot_nt(qts[g], dzs[g])
            for g in hs:
                dv_ref[j, g * HEAD_DIM:(g + 1) * HEAD_DIM, :] += _dot_nt(douts[g], avs[g].astype(_MXU))
            return tuple((carry[g][0] + _colsum(ets[g]), dqs[g]) for g in hs)

        def live(j):
            jj = jnp.maximum(j, 0)
            top = rs_ref[0, 0, jj][0:1, :]
            for g in range(1, G):
                top = jnp.maximum(top, rs_ref[g, 0, jj][0:1, :])
            return jnp.logical_and(j >= 0, jnp.max(top) >= SB_DEAD)

        first = lax.while_loop(lambda st: st[1], lambda st: (st[0] - 1, live(st[0] - 2)), (i, live(i - 1)))[0]
        carry = tuple((jnp.zeros((1, T), F32), jnp.zeros((HEAD_DIM, T), F32)) for _ in range(G))
        carry = lax.fori_loop(first, i, lambda s, cr: blk(s, cr, None), carry)
        carry = blk(i, carry, causal)
        dq_ref[...] = (jnp.concatenate([carry[g][1] for g in range(G)], axis=0) * SCALE).T.astype(dq_ref.dtype)

        @pl.when(i == pl.num_programs(1) - 1)
        def _():
            def flush(j, _):
                rows = pl.ds(pl.multiple_of(j * T, T), T)
                dk_out_ref[rows, :] = dk_ref[j].T.astype(dk_out_ref.dtype)
                dv_out_ref[rows, :] = dv_ref[j].T.astype(dv_out_ref.dtype)
                return 0
            lax.fori_loop(0, nk, flush, 0)

        @pl.when(last_step)
        def _():
            scatter.finish()

    colblk = pl.BlockSpec((G, HEAD_DIM, T), lambda h, i: (h, 0, i))
    once = pl.Buffered(1)
    kblk = pl.BlockSpec((G, nk, T, HEAD_DIM), lambda h, i: (h, 0, 0, 0), pipeline_mode=once)
    kTblk = pl.BlockSpec((G, nk, HEAD_DIM, T), lambda h, i: (h, 0, 0, 0), pipeline_mode=once)
    any_spec = pl.BlockSpec(memory_space=pl.ANY)
    res = pl.pallas_call(
        body, name="sb_bwd", grid=(Hh // G, nq),
        in_specs=[colblk, kblk, kTblk, kblk, colblk,
                  pl.BlockSpec((G, 1, nk, 8, T), lambda h, i: (h, i, 0, 0, 0))] + [any_spec] * nw,
        out_specs=[pl.BlockSpec((T, G * HEAD_DIM), lambda h, i: (i, h)),
                   pl.BlockSpec((S, G * HEAD_DIM), lambda h, i: (0, h), pipeline_mode=once),
                   pl.BlockSpec((S, G * HEAD_DIM), lambda h, i: (0, h), pipeline_mode=once)] + [any_spec] * nw,
        out_shape=[_sds((S, Hh * HEAD_DIM), _MXU)] * 3 + s_outs,
        scratch_shapes=[pltpu.VMEM((nk, G * HEAD_DIM, T), F32), pltpu.VMEM((nk, G * HEAD_DIM, T), F32)] + s_sems,
        compiler_params=_cp(("arbitrary", "arbitrary"), vmem_mb=60),
    )(qT, kb, kTb, vb, doT, rsave, *s_ins)
    return res[0], res[1], res[2], list(res[3:])


def _bucket_table():
    qi = np.arange(BLOCK)[:, None]
    cj = np.arange(2 * BLOCK)[None, :]
    dist = qi + BLOCK - cj
    exact = REL_BUCKETS // 2
    d = np.maximum(dist, 0)
    d_f = np.maximum(d, 1).astype(np.float32)
    large = exact + (np.log(d_f / np.float32(exact)) / np.float32(math.log(REL_MAX_DIST / exact))
                     * np.float32(REL_BUCKETS - exact)).astype(np.int32)
    large = np.minimum(large, REL_BUCKETS - 1)
    return np.where(d < exact, d, large).astype(np.int32)


def _swa_bias(rel_bias, bucket):
    def body(rb_ref, bk_ref, o_ref):
        bk = bk_ref[...]
        for h in range(SWA_HEADS):
            t = jnp.zeros((2 * BLOCK, BLOCK), F32)
            for b in range(REL_BUCKETS):
                t = jnp.where(bk == b, rb_ref[b, h], t)
            o_ref[h] = t

    return pl.pallas_call(
        body, name="swa_bias",
        in_specs=[pl.BlockSpec(memory_space=pltpu.SMEM), pl.BlockSpec(memory_space=pltpu.VMEM)],
        out_specs=pl.BlockSpec(memory_space=pltpu.VMEM),
        out_shape=_sds((SWA_HEADS, 2 * BLOCK, BLOCK), F32),
    )(rel_bias, bucket)


def _swa_logits(q, kp, kc):
    qs = (q.astype(F32) * SCALE).astype(_MXU)
    return qs, _dot_nt(kp, qs), _dot_nt(kc, qs)


def _swa_softmax(lp, lc, bias, sink, live_prev):
    r, c = _tri_masks(BLOCK)
    in_window = r > c if live_prev is None else jnp.logical_and(r > c, live_prev)
    lp = jnp.where(in_window, lp + bias[:BLOCK, :], -jnp.inf)
    lc = jnp.where(r <= c, lc + bias[BLOCK:, :], -jnp.inf)
    m = jnp.maximum(jnp.maximum(jnp.max(lp, axis=0, keepdims=True), jnp.max(lc, axis=0, keepdims=True)), sink)
    pp = jnp.exp(lp - m)
    pc = jnp.exp(lc - m)
    ps = jnp.exp(sink - m)
    denom = _colsum(pp) + _colsum(pc) + ps
    return pp / denom, pc / denom, ps / denom


def _swa_sub(nb):
    return min(SWA_SUB, nb)


def _swa_keys(b, prev_ref, cur_ref, i):
    cur = cur_ref[0, b * BLOCK:(b + 1) * BLOCK, :]
    if b == 0:
        return prev_ref[0], cur, i > 0
    return cur_ref[0, (b - 1) * BLOCK:b * BLOCK, :], cur, None


def _swa_keys_t(b, prev_ref, cur_ref):
    cur = cur_ref[0, :, b * BLOCK:(b + 1) * BLOCK]
    return (prev_ref[0] if b == 0 else cur_ref[0, :, (b - 1) * BLOCK:b * BLOCK]), cur


SWA_PAIR = 4


def _swa_fwd(q, k, vT, bias, sink):
    S = q.shape[1]
    nb = S // BLOCK
    ns = _swa_sub(nb)
    R = ns * BLOCK
    P = SWA_PAIR

    def body(q_ref, kp_ref, kc_ref, vp_ref, vc_ref, bias_ref, sink_ref, o_ref):
        i = pl.program_id(1)
        units = [(hh, b) for hh in range(P) for b in range(ns)]
        keys = [_swa_keys(b, kp_ref, kc_ref, i) for b in range(ns)]
        vals = [_swa_keys_t(b, vp_ref, vc_ref) for b in range(ns)]
        logits = {u: _swa_logits(q_ref[u[0], u[1] * BLOCK:(u[1] + 1) * BLOCK, :], keys[u[1]][0], keys[u[1]][1])
                  for u in units}
        ws = {u: _swa_softmax(logits[u][1], logits[u][2], bias_ref[u[0]], sink_ref[u[0]][:, :1], keys[u[1]][2])
              for u in units}
        outs = {u: _dot(vals[u[1]][0], ws[u][0].astype(_MXU)) + _dot(vals[u[1]][1], ws[u][1].astype(_MXU))
                for u in units}
        for b in range(ns):
            o_ref[b * BLOCK:(b + 1) * BLOCK, :] = jnp.concatenate([outs[(hh, b)] for hh in range(P)], axis=0).T

    kvh = lambda p: (p * P) // SWA_GROUP
    prev = pl.BlockSpec((1, BLOCK, HEAD_DIM), lambda p, i: (kvh(p), jnp.maximum(i * ns - 1, 0), 0))
    cur = pl.BlockSpec((1, R, HEAD_DIM), lambda p, i: (kvh(p), i, 0))
    prev_t = pl.BlockSpec((1, HEAD_DIM, BLOCK), lambda p, i: (kvh(p), 0, jnp.maximum(i * ns - 1, 0)))
    cur_t = pl.BlockSpec((1, HEAD_DIM, R), lambda p, i: (kvh(p), 0, i))
    return pl.pallas_call(
        body, name="swa_fwd", grid=(SWA_HEADS // P, nb // ns),
        in_specs=[pl.BlockSpec((P, R, HEAD_DIM), lambda p, i: (p, i, 0)), prev, cur, prev_t, cur_t,
                  pl.BlockSpec((P, 2 * BLOCK, BLOCK), lambda p, i: (p, 0, 0)),
                  pl.BlockSpec((P, 1, BLOCK), lambda p, i: (p, 0, 0))],
        out_specs=pl.BlockSpec((R, P * HEAD_DIM), lambda p, i: (i, p)),
        out_shape=_sds((S, SWA_HEADS * HEAD_DIM), F32),
        compiler_params=_cp(("parallel", "parallel")),
    )(q, k, k, vT, vT, bias, sink)


def _swa_bwd(q, k, kT, v, bias, sink, do, grads):
    S = q.shape[1]
    nb = S // BLOCK
    ns = _swa_sub(nb)
    R = ns * BLOCK
    P = SWA_PAIR
    nw = len(grads)
    x_ins, x_outs, x_sems = _swap_io(grads)

    def body(q_ref, kp_ref, kc_ref, ktp_ref, ktc_ref, vp_ref, vc_ref, bias_ref, sink_ref, do_ref, *rest):
        dq_ref, dk_ref, dv_ref, dbias_ref, dsink_ref = rest[nw:nw + 5]
        swap = _Swap(rest[:nw], rest[nw + 5:2 * nw + 5], *rest[2 * nw + 5:])
        g = pl.program_id(1)
        i = pl.program_id(2)
        first_step = jnp.logical_and(pl.program_id(0) == 0, jnp.logical_and(g == 0, i == 0))
        last_step = jnp.logical_and(pl.program_id(0) == pl.num_programs(0) - 1,
                                    jnp.logical_and(g == pl.num_programs(1) - 1, i == pl.num_programs(2) - 1))

        @pl.when(first_step)
        def _():
            swap.start()

        @pl.when(jnp.logical_and(g == 0, i == 0))
        def _():
            dk_ref[...] = jnp.zeros_like(dk_ref)
            dv_ref[...] = jnp.zeros_like(dv_ref)

        @pl.when(i == 0)
        def _():
            dbias_ref[...] = jnp.zeros_like(dbias_ref)
            dsink_ref[...] = jnp.zeros_like(dsink_ref)

        subs = range(ns)
        units = [(hh, b) for hh in range(P) for b in subs]
        rows = [slice(b * BLOCK, (b + 1) * BLOCK) for b in subs]
        keys = [_swa_keys(b, kp_ref, kc_ref, i) for b in subs]
        keys_t = [_swa_keys_t(b, ktp_ref, ktc_ref) for b in subs]
        vals = [_swa_keys(b, vp_ref, vc_ref, i) for b in subs]
        douts = {u: do_ref[u[0], rows[u[1]], :] for u in units}
        logits = {u: _swa_logits(q_ref[u[0], rows[u[1]], :], keys[u[1]][0], keys[u[1]][1]) for u in units}
        dws = {u: (_dot_nt(vals[u[1]][0], douts[u]), _dot_nt(vals[u[1]][1], douts[u])) for u in units}
        wts, dls = {}, {}
        for hh in range(P):
            dbp = jnp.zeros((BLOCK, BLOCK), F32)
            dbc = jnp.zeros((BLOCK, BLOCK), F32)
            dsk = jnp.zeros((1, BLOCK), F32)
            for b in subs:
                u = (hh, b)
                wp, wc, ws = _swa_softmax(logits[u][1], logits[u][2], bias_ref[hh], sink_ref[hh][:, :1], keys[b][2])
                dwp, dwc = dws[u]
                delta = _colsum(wp * dwp) + _colsum(wc * dwc)
                dlp = wp * (dwp - delta)
                dlc = wc * (dwc - delta)
                dbp += dlp
                dbc += dlc
                dsk -= ws * delta
                wts[u] = (wp.astype(_MXU), wc.astype(_MXU))
                dls[u] = (dlp.astype(_MXU), dlc.astype(_MXU))
            dbias_ref[hh, :BLOCK, :] += dbp
            dbias_ref[hh, BLOCK:, :] += dbc
            dsink_ref[hh] += jnp.broadcast_to(dsk, (8, BLOCK))
        dqs = {u: (_dot(keys_t[u[1]][0], dls[u][0]) + _dot(keys_t[u[1]][1], dls[u][1])) * SCALE for u in units}
        for b in subs:
            dq_ref[rows[b], :] = jnp.concatenate([dqs[(hh, b)] for hh in range(P)], axis=0).T.astype(dq_ref.dtype)
        for b in subs:
            blk = i * ns + b
            dk_cur = sum(_dot(dls[(hh, b)][1], logits[(hh, b)][0]) for hh in range(P))
            dv_cur = sum(_dot(wts[(hh, b)][1], douts[(hh, b)]) for hh in range(P))
            dk_prev = sum(_dot(dls[(hh, b)][0], logits[(hh, b)][0]) for hh in range(P))
            dv_prev = sum(_dot(wts[(hh, b)][0], douts[(hh, b)]) for hh in range(P))
            dk_ref[0, blk] += dk_cur
            dv_ref[0, blk] += dv_cur
            if b == 0:
                @pl.when(i > 0)
                def _():
                    dk_ref[0, blk - 1] += dk_prev
                    dv_ref[0, blk - 1] += dv_prev
            else:
                dk_ref[0, blk - 1] += dk_prev
                dv_ref[0, blk - 1] += dv_prev

        @pl.when(last_step)
        def _():
            swap.finish()

    G2 = SWA_GROUP // P
    hp = lambda kv, g, i: kv * G2 + g
    prev = pl.BlockSpec((1, BLOCK, HEAD_DIM), lambda kv, g, i: (kv, jnp.maximum(i * ns - 1, 0), 0))
    cur = pl.BlockSpec((1, R, HEAD_DIM), lambda kv, g, i: (kv, i, 0))
    prev_t = pl.BlockSpec((1, HEAD_DIM, BLOCK), lambda kv, g, i: (kv, 0, jnp.maximum(i * ns - 1, 0)))
    cur_t = pl.BlockSpec((1, HEAD_DIM, R), lambda kv, g, i: (kv, 0, i))
    qblk = pl.BlockSpec((P, R, HEAD_DIM), lambda kv, g, i: (hp(kv, g, i), i, 0))
    kvacc = pl.BlockSpec((1, nb, BLOCK, HEAD_DIM), lambda kv, g, i: (kv, 0, 0, 0))
    any_spec = pl.BlockSpec(memory_space=pl.ANY)
    res = pl.pallas_call(
        body, name="swa_bwd", grid=(SWA_KV_HEADS, G2, nb // ns),
        in_specs=[qblk, prev, cur, prev_t, cur_t, prev, cur,
                  pl.BlockSpec((P, 2 * BLOCK, BLOCK), lambda kv, g, i: (hp(kv, g, i), 0, 0)),
                  pl.BlockSpec((P, 1, BLOCK), lambda kv, g, i: (hp(kv, g, i), 0, 0)), qblk] + [any_spec] * nw,
        out_specs=[pl.BlockSpec((R, P * HEAD_DIM), lambda kv, g, i: (i, hp(kv, g, i))), kvacc, kvacc,
                   pl.BlockSpec((P, 2 * BLOCK, BLOCK), lambda kv, g, i: (hp(kv, g, i), 0, 0)),
                   pl.BlockSpec((P, 8, BLOCK), lambda kv, g, i: (hp(kv, g, i), 0, 0))] + [any_spec] * nw,
        out_shape=[_sds((S, SWA_HEADS * HEAD_DIM), _MXU), _sds((SWA_KV_HEADS, nb, BLOCK, HEAD_DIM), F32),
                   _sds((SWA_KV_HEADS, nb, BLOCK, HEAD_DIM), F32), _sds((SWA_HEADS, 2 * BLOCK, BLOCK), F32),
                   _sds((SWA_HEADS, 8, BLOCK), F32)] + x_outs,
        scratch_shapes=x_sems,
        compiler_params=_cp(("arbitrary", "arbitrary", "arbitrary")),
    )(q, k, k, kT, kT, v, v, bias, sink, do, *x_ins)
    return res[0], res[1], res[2], res[3], res[4], list(res[5:])


def _swa_small_grads(dbias, dsink, bucket):
    rows = REL_BUCKETS + 8

    def total(x):
        return jnp.sum(jnp.sum(x, axis=1, keepdims=True), axis=0, keepdims=True)

    def body(db_ref, ds_ref, bk_ref, o_ref):
        bk = bk_ref[...]
        r = lax.broadcasted_iota(jnp.int32, (rows, BLOCK), 0)
        c = lax.broadcasted_iota(jnp.int32, (rows, BLOCK), 1)
        out = jnp.zeros((rows, BLOCK), F32)
        for h in range(SWA_HEADS):
            db = db_ref[h]
            for b in range(REL_BUCKETS):
                s = total(jnp.where(bk == b, db, 0.0))
                out = jnp.where(jnp.logical_and(r == b, c == h), s, out)
            s = jnp.sum(ds_ref[h][0:1, :], axis=1, keepdims=True)
            out = jnp.where(jnp.logical_and(r == REL_BUCKETS, c == h), s, out)
        o_ref[...] = out

    vm = pl.BlockSpec(memory_space=pltpu.VMEM)
    return pl.pallas_call(body, name="swa_small_grads", in_specs=[vm, vm, vm], out_specs=vm,
                          out_shape=_sds((rows, BLOCK), F32))(dbias, dsink, bucket)


def _tile_rows(n):
    for t in (512, 352, 256, 176, 128, 64, 32, 16, 8):
        if n % t == 0:
            return t
    return n


def _cast_rows(x, dtype, name):
    R, C = x.shape
    tr = _tile_rows(R)

    def body(x_ref, o_ref):
        o_ref[...] = x_ref[...].astype(o_ref.dtype)

    return pl.pallas_call(body, name=name, grid=(R // tr,), in_specs=[_rows(tr, C)], out_specs=_rows(tr, C),
                          out_shape=_sds((R, C), dtype), compiler_params=_cp(("parallel",)))(x)


def _pair_sum(g, recv, c, name):
    n, half, C = recv.shape
    tr = _tile_rows(half)

    def body(c_ref, a_ref, b_ref, o_ref):
        o_ref[...] = (a_ref[0] + b_ref[...]).astype(o_ref.dtype)

    return pl.pallas_call(
        body, name=name,
        grid_spec=pltpu.PrefetchScalarGridSpec(
            num_scalar_prefetch=1, grid=(n, half // tr),
            in_specs=[pl.BlockSpec((1, 1, tr, C), lambda j, i, c_ref: (j, c_ref[0], i, 0)),
                      pl.BlockSpec((1, tr, C), lambda j, i, c_ref: (j, i, 0))],
            out_specs=pl.BlockSpec((1, tr, C), lambda j, i, c_ref: (j, i, 0))),
        out_shape=_sds((n, half, C), _MXU),
        compiler_params=_cp(("parallel", "parallel")))(c.reshape(1), g.reshape(n, 2, half, C), recv)


def _chip_sum(own, recv, me, name):
    n, R, C = recv.shape
    tr = _tile_rows(R)

    def body(me_ref, own_ref, recv_ref, o_ref):
        acc = None
        for j in range(n):
            term = jnp.where(me_ref[0] == j, own_ref[0], recv_ref[j]).astype(F32)
            acc = term if acc is None else acc + term
        o_ref[...] = acc

    return pl.pallas_call(
        body, name=name,
        grid_spec=pltpu.PrefetchScalarGridSpec(
            num_scalar_prefetch=1, grid=(R // tr,),
            in_specs=[pl.BlockSpec((1, tr, C), lambda i, me_ref: (me_ref[0], i, 0)),
                      pl.BlockSpec((n, tr, C), lambda i, me_ref: (0, i, 0))],
            out_specs=pl.BlockSpec((tr, C), lambda i, me_ref: (i, 0))),
        out_shape=_sds((R, C), F32), compiler_params=_cp(("parallel",)))(me.reshape(1), own, recv)


def _adamw_math(w, g, m, v):
    m = ADAM_B1 * m + (1.0 - ADAM_B1) * g
    v = ADAM_B2 * v + (1.0 - ADAM_B2) * (g * g)
    m_hat = m / (1.0 - ADAM_B1 ** ADAM_STEP)
    v_hat = v / (1.0 - ADAM_B2 ** ADAM_STEP)
    delta = -ADAM_LR * (m_hat / (jnp.sqrt(v_hat) + ADAM_EPS) + ADAM_WD * w)
    return delta, m, v


def _adamw(w, g, m, v, name):
    R, C = w.shape
    tr = _tile_rows(R)

    def body(w_ref, g_ref, m_ref, v_ref, d_ref, nm_ref, nv_ref):
        d, nm, nv = _adamw_math(w_ref[...], g_ref[...], m_ref[...], v_ref[...])
        d_ref[...] = d
        nm_ref[...] = nm
        nv_ref[...] = nv

    blk = _rows(tr, C)
    return pl.pallas_call(body, name=name, grid=(R // tr,), in_specs=[blk] * 4, out_specs=[blk] * 3,
                          out_shape=[_sds((R, C), F32)] * 3, compiler_params=_cp(("parallel",)))(w, g, m, v)


def _gather_weights(shards):
    nw = len(shards)
    ins, outs, sems = _gather_io(shards)

    def body(*refs):
        ex = _Gather(refs[:nw], refs[nw:2 * nw], *refs[2 * nw:])
        ex.start()
        ex.forward()
        ex.finish()

    any_spec = pl.BlockSpec(memory_space=pl.ANY)
    got = pl.pallas_call(body, name="gather_weights", in_specs=[any_spec] * nw, out_specs=[any_spec] * nw,
                         out_shape=outs, scratch_shapes=sems)(*ins)
    return _gather_assemble(got, shards)


def _swap_halves(grads, name):
    nw = len(grads)
    ins, outs, sems = _swap_io(grads)

    def body(*refs):
        ex = _Swap(refs[:nw], refs[nw:2 * nw], *refs[2 * nw:])
        ex.start()
        ex.finish()

    any_spec = pl.BlockSpec(memory_space=pl.ANY)
    return pl.pallas_call(body, name=name, in_specs=[any_spec] * nw, out_specs=[any_spec] * nw,
                          out_shape=outs, scratch_shapes=sems)(*ins)


def _join_halves(sums):
    nw = len(sums)

    def body(*refs):
        f_refs, out_refs = refs[:nw], refs[nw:2 * nw]
        send_sems, recv_sems = refs[2 * nw:]
        x, y, c, _ = _place()
        ws = range(nw)

        def copy(w, half_index):
            return pltpu.make_async_remote_copy(
                src_ref=f_refs[w], dst_ref=out_refs[w].at[half_index], send_sem=send_sems.at[w],
                recv_sem=recv_sems.at[w], device_id=(x, y, 1 - c), device_id_type=MESH)

        sends = [copy(w, c) for w in ws]
        for cp in sends:
            cp.start()
        for w in ws:
            copy(w, 1 - c).wait_recv()
        for cp in sends:
            cp.wait_send()

    any_spec = pl.BlockSpec(memory_space=pl.ANY)
    outs = pl.pallas_call(
        body, name="join_halves", in_specs=[any_spec] * nw, out_specs=[any_spec] * nw,
        out_shape=[_sds((2,) + f.shape, f.dtype) for f in sums],
        scratch_shapes=[pltpu.SemaphoreType.DMA((nw,)), pltpu.SemaphoreType.DMA((nw,))],
    )(*sums)
    c = lax.axis_index("c")
    return [lax.dynamic_update_slice_in_dim(o, f[None], c, axis=0).reshape(2 * f.shape[0], f.shape[1])
            for o, f in zip(outs, sums)]


def _allreduce_small(block):
    m_per, n = block.shape

    def body(x_ref, sum_ref, loss_ref, all_ref, send_sems, recv_sems, local_sem):
        x, y, c, chips = _place()
        me, sibling = (x, y, c), (x, y, 1 - c)

        def rows(px, py, pc):
            return all_ref.at[pl.ds(pl.multiple_of((4 * px + 2 * py + pc) * m_per, 8), m_per), :]

        def copy(k, blk, to, src=None):
            return pltpu.make_async_remote_copy(
                src_ref=rows(*blk) if src is None else src, dst_ref=rows(*blk), send_sem=send_sems.at[k],
                recv_sem=recv_sems.at[k], device_id=to, device_id_type=MESH)

        mine = pltpu.make_async_copy(x_ref, rows(*me), local_sem)
        mine.start()
        first = [copy(0, me, sibling, src=x_ref)]
        first += [copy(1 + j, me, (*chip, c), src=x_ref) for j, chip in enumerate(chips)]
        for cp in first:
            cp.start()
        passed = [copy(4 + j, (*chip, c), sibling) for j, chip in enumerate(chips)]
        for j, chip in enumerate(chips):
            copy(1 + j, (*chip, c), me).wait_recv()
            passed[j].start()
        copy(0, sibling, me).wait_recv()
        for j, chip in enumerate(chips):
            copy(4 + j, (*chip, 1 - c), me).wait_recv()
        for cp in first + passed:
            cp.wait_send()
        mine.wait()

        acc = all_ref[0:m_per, :]
        for d in range(1, 8):
            acc = acc + all_ref[d * m_per:(d + 1) * m_per, :]
        sum_ref[...] = acc
        tot = jnp.sum(acc[8:9, :], axis=1, keepdims=True) * (0.5 / D_MODEL)
        loss_ref[...] = jnp.broadcast_to(tot, loss_ref.shape)

    vm = pl.BlockSpec(memory_space=pltpu.VMEM)
    return pl.pallas_call(
        body, name="allreduce_small", in_specs=[vm], out_specs=[vm, vm],
        out_shape=[_sds((m_per, n), F32), _sds((8, 128), F32)],
        scratch_shapes=[pltpu.VMEM((8 * m_per, n), F32), pltpu.SemaphoreType.DMA((7,)), pltpu.SemaphoreType.DMA((7,)),
                        pltpu.SemaphoreType.DMA],
    )(block)


def _heads_rows(x, nh):
    S = x.shape[0]
    return x.reshape(S, nh, HEAD_DIM).transpose(1, 0, 2)


def _heads_cols(x, nh):
    S = x.shape[0]
    return x.reshape(S, nh, HEAD_DIM).transpose(1, 2, 0)


def _pad_row(v):
    v = v.reshape(1, -1)
    return jnp.pad(v, ((0, 0), (0, D_MODEL - v.shape[1])))


def _pack_small(ln_in_g, ln_in_b, sb_g, swa_g, sinks, rel_bias, ln1_g, ln1_b, ln2_g, ln2_b, extra):
    rows = [_pad_row(ln_in_g), _pad_row(ln_in_b), jnp.concatenate([sb_g.reshape(1, -1), swa_g.reshape(1, -1)], axis=1),
            _pad_row(jnp.concatenate([rel_bias.reshape(1, -1), sinks.reshape(1, -1)], axis=1)),
            _pad_row(ln1_g), _pad_row(ln1_b), _pad_row(ln2_g), _pad_row(ln2_b), _pad_row(extra)]
    rows.append(jnp.zeros((SMALL_ROWS - len(rows), D_MODEL), F32))
    return jnp.concatenate(rows, axis=0)


def _unpack_small(blk):
    nrb = REL_BUCKETS * SWA_HEADS
    return (blk[0], blk[1], blk[2:3, :SB_WIDTH], blk[2:3, SB_WIDTH:], blk[3:4, nrb:nrb + SWA_HEADS],
            blk[3, :nrb].reshape(REL_BUCKETS, SWA_HEADS), blk[4:5], blk[5:6], blk[6:7], blk[7:8])


def kernel(x, ln_in_g, ln_in_b, w_in, sb_norm_g, swa_norm_g, sinks, rel_bias, w_out, ln1_g, ln1_b, w_gate_up, w_down, ln2_g, ln2_b, loss_target, m_ln_in_g, m_ln_in_b, m_w_in, m_sb_norm_g, m_swa_norm_g, m_sinks, m_rel_bias, m_w_out, m_ln1_g, m_ln1_b, m_w_gate_up, m_w_down, m_ln2_g, m_ln2_b, v_ln_in_g, v_ln_in_b, v_w_in, v_sb_norm_g, v_swa_norm_g, v_sinks, v_rel_bias, v_w_out, v_ln1_g, v_ln1_b, v_w_gate_up, v_w_down, v_ln2_g, v_ln2_b):
    S = x.shape[1]
    x2 = x.reshape(S, D_MODEL)
    tgt = loss_target.reshape(S, D_MODEL)
    T = min(S, SB_TILE)
    bucket = jnp.asarray(_bucket_table().T)
    row = lambda v: v.reshape(1, -1)

    shards = [_cast_rows(w[0], _MXU, "cast_" + n) for n, w in (("w_in", w_in), ("w_out", w_out), ("w_gate_up", w_gate_up), ("w_down", w_down))]
    (w_in_sh,) = _gather_weights(shards[:1])
    w_in_f = jnp.concatenate([w_in_sh[j] for j in range(N_CHIPS)], axis=1)

    h0, h0b, q_sw, kv_sw, qT_sb, kTb_sb, vTb_sb, kb_sb, vb_sb = _ln_in_proj(x2, row(ln_in_g), row(ln_in_b), w_in_f)
    k_sw, v_sw = kv_sw[:, :SWA_KV_WIDTH], kv_sw[:, SWA_KV_WIDTH:]
    sb_out, rsave, (w_out_sh, w_gu_sh, w_down_sh) = _sb_fwd(qT_sb, kb_sb, vTb_sb, shards[1:])
    w_out_f = w_out_sh.reshape(D_MODEL, D_MODEL)
    w_down_f = w_down_sh.reshape(D_FF, D_MODEL)

    bias = _swa_bias(rel_bias, bucket)
    sink_rows = jnp.broadcast_to(sinks.reshape(SWA_HEADS, 1, 1), (SWA_HEADS, 1, BLOCK))
    qh_sw, kh_sw, vh_sw = _heads_rows(q_sw, SWA_HEADS), _heads_rows(k_sw, SWA_KV_HEADS), _heads_rows(v_sw, SWA_KV_HEADS)
    swa_out = _swa_fwd(qh_sw, kh_sw, _heads_cols(v_sw, SWA_KV_HEADS), bias, sink_rows)

    pre1, merged, h1b = _mix_out(sb_out, swa_out, sb_norm_g, swa_norm_g, w_out_f, h0, ln1_g, ln1_b)
    act, silu, dsilu_up = _ffn_up(h1b, w_gu_sh)
    dp2, dp2b, dg2, db2, errsum = _ffn_down_loss(act, w_down_f, pre1, ln1_g, ln1_b, ln2_g, ln2_b, tgt)

    g_w_down = _matmul_tn(act, dp2b, "grad_w_down", FF_CHUNK, D_MODEL)
    dgate, dup = _ffn_down_bwd(dp2b, w_down_f, silu, dsilu_up)
    g_w_gu = _matmul_tn_pair(h1b, dgate, dup, "grad_w_gate_up")
    dp1, dp1b, dg1, db1 = _ffn_up_bwd(dgate, dup, w_gu_sh, dp2, pre1, ln1_g)
    g_w_out = _matmul_tn(merged, dp1b, "grad_w_out", D_MODEL, D_MODEL)
    doT_sb, dsw, dgsb, dgsw = _mix_bwd(dp1b, w_out_f, sb_out, swa_out, sb_norm_g, swa_norm_g)

    c = lax.axis_index("c").astype(jnp.int32)
    me = (2 * lax.axis_index("x") + lax.axis_index("y")).astype(jnp.int32)
    grads_a = [g_w_out.reshape(N_CHIPS, D_MODEL // N_CHIPS, D_MODEL), g_w_gu, g_w_down.reshape(N_CHIPS, D_FF // N_CHIPS, D_MODEL)]
    names_a = ("w_out", "w_gate_up", "w_down")
    dq_sw, dkh_sw, dvh_sw, dbias, dsink, swapped_a = _swa_bwd(qh_sw, kh_sw, _heads_cols(k_sw, SWA_KV_HEADS), vh_sw, bias,
                                                               sink_rows, _heads_rows(dsw, SWA_HEADS), grads_a)
    swa_small = _swa_small_grads(dbias, dsink, bucket)
    partials_a = [_pair_sum(g, r, c, "pair_sum_" + n) for g, r, n in zip(grads_a, swapped_a, names_a)]
    dq_sb, dk_sb, dv_sb, recv_a = _sb_bwd(qT_sb, kb_sb, kTb_sb, vb_sb,
                                             doT_sb, rsave, partials_a)
    tok = lambda t, nh: t.reshape(nh, S, HEAD_DIM).transpose(1, 0, 2).reshape(S, nh * HEAD_DIM)
    dproj = [dq_sb, dk_sb, dv_sb, dq_sw,
             jnp.concatenate([tok(dkh_sw, SWA_KV_HEADS), tok(dvh_sw, SWA_KV_HEADS)], axis=1).astype(_MXU)]
    g_w_in = jnp.concatenate([_matmul_tn(h0b, d, "grad_w_in_%d" % k, D_MODEL, d.shape[1]) for k, d in enumerate(dproj)],
                             axis=1)

    cin = IN_COLS // N_CHIPS
    grads_b = [jnp.stack([g_w_in[:, j * cin:(j + 1) * cin] for j in range(N_CHIPS)])]
    partials_b = [_pair_sum(grads_b[0], _swap_halves(grads_b, "swap_halves_in")[0], c, "pair_sum_w_in")]
    grad_x, dg_in, db_in, recv_b = _in_proj_bwd(dproj, w_in_f, dp1, x2, row(ln_in_g), partials_b)
    names = ("w_in",) + names_a
    sums = [_chip_sum(p, r, me, "chip_sum_" + n) for p, r, n in zip(partials_b + partials_a, list(recv_b) + list(recv_a), names)]
    gs_in, gs_out, gs_gu, gs_down = _join_halves(sums)

    nrb = REL_BUCKETS * SWA_HEADS
    small = _pack_small(dg_in, db_in, dgsb, dgsw, swa_small[REL_BUCKETS, :SWA_HEADS],
                        swa_small[:REL_BUCKETS, :SWA_HEADS], dg1, db1, dg2, db2, errsum)
    g_small, loss_tile = _allreduce_small(small)
    loss = loss_tile[0, 0]

    big = []
    for name, w, g, m, v in (("adamw_w_in", w_in, gs_in, m_w_in, v_w_in), ("adamw_w_out", w_out, gs_out, m_w_out, v_w_out),
                             ("adamw_w_gate_up", w_gate_up, gs_gu, m_w_gate_up, v_w_gate_up),
                             ("adamw_w_down", w_down, gs_down, m_w_down, v_w_down)):
        d, nm, nv = _adamw(w[0], g, m[0], v[0], name)
        big.append((g[None], d[None], nm[None], nv[None]))
    zero = jnp.zeros((1,), F32)
    w_small = _pack_small(ln_in_g, ln_in_b, sb_norm_g, swa_norm_g, sinks, rel_bias, ln1_g, ln1_b, ln2_g, ln2_b, zero)
    m_small = _pack_small(m_ln_in_g, m_ln_in_b, m_sb_norm_g, m_swa_norm_g, m_sinks, m_rel_bias, m_ln1_g, m_ln1_b,
                          m_ln2_g, m_ln2_b, zero)
    v_small = _pack_small(v_ln_in_g, v_ln_in_b, v_sb_norm_g, v_swa_norm_g, v_sinks, v_rel_bias, v_ln1_g, v_ln1_b,
                          v_ln2_g, v_ln2_b, zero)
    small_out = [_unpack_small(t) for t in (g_small,) + tuple(_adamw(w_small, g_small, m_small, v_small, "adamw_small"))]

    def kind(k):
        s = small_out[k]
        return [s[0], s[1], big[0][k], s[2], s[3], s[4], s[5], big[1][k], s[6], s[7], big[2][k], big[3][k], s[8], s[9]]

    return (loss, grad_x.reshape(1, S, D_MODEL), *kind(0), *kind(1), *kind(2), *kind(3))
```

```python
import math

import numpy as np
import jax
import jax.numpy as jnp
from jax import lax
from jax.experimental import pallas as pl
from jax.experimental.pallas import tpu as pltpu

F32 = jnp.float32
_MXU = jnp.bfloat16

D_MODEL = 1024
HEAD_DIM = 64
SB_HEADS = 8
SWA_HEADS = 8
SWA_KV_HEADS = 2
SWA_GROUP = SWA_HEADS // SWA_KV_HEADS
SB_WIDTH = SB_HEADS * HEAD_DIM
SWA_WIDTH = SWA_HEADS * HEAD_DIM
SWA_KV_WIDTH = SWA_KV_HEADS * HEAD_DIM
IN_COLS = 3 * SB_WIDTH + SWA_WIDTH + 2 * SWA_KV_WIDTH
BLOCK = 128
REL_BUCKETS = 32
REL_MAX_DIST = 128
D_FF = 2816
FF_CHUNK = D_FF // 2
ALPHA = 2.0 ** 0.25
LN_EPS = 1e-5
RMS_EPS = 1e-6
SCALE = HEAD_DIM ** -0.5
SB_TILE = 256
SB_GROUP_FWD = 8
SB_GROUP_BWD = 4
SB_FORWARD_LEAD = 8
SB_DEAD = -105.0
SWA_SUB = 8

ADAM_LR = 0.001
ADAM_B1 = 0.9
ADAM_B2 = 0.999
ADAM_EPS = 1e-08
ADAM_WD = 0.01
ADAM_STEP = 10

N_CHIPS = 4
SMALL_ROWS = 16

MESH = pl.DeviceIdType.MESH


def _sds(shape, dtype):
    return jax.ShapeDtypeStruct(shape, dtype)


def _cp(sem=None, vmem_mb=48):
    kw = dict(vmem_limit_bytes=vmem_mb * 1024 * 1024)
    if sem is not None:
        kw["dimension_semantics"] = sem
    return pltpu.CompilerParams(**kw)


def _dot(a, b):
    return jnp.dot(a, b, preferred_element_type=F32)


def _dot_nt(a, b):
    return lax.dot_general(a, b, (((1,), (1,)), ((), ())), preferred_element_type=F32)


def _dot_tn(a, b):
    return lax.dot_general(a, b, (((0,), (0,)), ((), ())), preferred_element_type=F32)


def _ln_hat(x):
    mu = jnp.mean(x, axis=-1, keepdims=True)
    xc = x - mu
    var = jnp.mean(xc * xc, axis=-1, keepdims=True)
    rstd = lax.rsqrt(var + LN_EPS)
    return xc * rstd, rstd


def _ln_bwd(xhat, rstd, dy, g):
    dxh = dy * g
    m1 = jnp.mean(dxh, axis=-1, keepdims=True)
    m2 = jnp.mean(dxh * xhat, axis=-1, keepdims=True)
    return rstd * (dxh - m1 - xhat * m2)


def _colsum(x):
    return jnp.sum(x, axis=0, keepdims=True)


def _split2(x):
    hi = x.astype(_MXU)
    lo = (x - hi.astype(F32)).astype(_MXU)
    return hi, lo


def _rows(tm, n):
    return pl.BlockSpec((tm, n), lambda i: (i, 0))


def _fixed(*shape):
    nd = len(shape)
    return pl.BlockSpec(shape, lambda i: (0,) * nd)


IN_SECTIONS = (SB_WIDTH, SB_WIDTH, SB_WIDTH, SWA_WIDTH, 2 * SWA_KV_WIDTH)


def _ln_in_proj(x, g, b, w):
    S = x.shape[0]
    tm = min(S, SB_TILE)
    offs = np.cumsum((0,) + IN_SECTIONS)
    swa = (3, 4)

    def body(x_ref, g_ref, b_ref, w_ref, h_ref, hb_ref, *o_refs):
        p_refs, (qT_ref, kT_ref, vT_ref, kr_ref, vr_ref) = o_refs[:len(swa)], o_refs[len(swa):]
        xhat, _ = _ln_hat(x_ref[...])
        h = xhat * g_ref[...] + b_ref[...]
        h_ref[...] = h
        hb = h.astype(_MXU)
        hb_ref[...] = hb
        proj = _dot(hb, w_ref[...])
        for k, p_ref in zip(swa, p_refs):
            p_ref[...] = proj[:, offs[k]:offs[k + 1]].astype(p_ref.dtype)
        heads = lambda k: proj[:, offs[k]:offs[k + 1]].T.astype(_MXU).reshape(SB_HEADS, HEAD_DIM, tm)
        qT_ref[...] = heads(0)
        kT_ref[:, 0] = heads(1)
        vT_ref[:, 0] = heads(2)
        for hd in range(SB_HEADS):
            cols = slice(hd * HEAD_DIM, (hd + 1) * HEAD_DIM)
            kr_ref[hd, 0] = proj[:, offs[1]:offs[2]][:, cols].astype(_MXU)
            vr_ref[hd, 0] = proj[:, offs[2]:offs[3]][:, cols].astype(_MXU)

    blocked = pl.BlockSpec((SB_HEADS, 1, HEAD_DIM, tm), lambda i: (0, i, 0, 0))
    blocked_rows = pl.BlockSpec((SB_HEADS, 1, tm, HEAD_DIM), lambda i: (0, i, 0, 0))
    return pl.pallas_call(
        body, name="ln_in_proj", grid=(S // tm,),
        in_specs=[_rows(tm, D_MODEL), _fixed(1, D_MODEL), _fixed(1, D_MODEL), _fixed(D_MODEL, IN_COLS)],
        out_specs=[_rows(tm, D_MODEL), _rows(tm, D_MODEL)] + [_rows(tm, IN_SECTIONS[k]) for k in swa]
                  + [pl.BlockSpec((SB_HEADS, HEAD_DIM, tm), lambda i: (0, 0, i)), blocked, blocked, blocked_rows,
                     blocked_rows],
        out_shape=[_sds((S, D_MODEL), F32), _sds((S, D_MODEL), _MXU)] + [_sds((S, IN_SECTIONS[k]), _MXU) for k in swa]
                  + [_sds((SB_HEADS, HEAD_DIM, S), _MXU), _sds((SB_HEADS, S // tm, HEAD_DIM, tm), _MXU),
                     _sds((SB_HEADS, S // tm, HEAD_DIM, tm), _MXU), _sds((SB_HEADS, S // tm, tm, HEAD_DIM), _MXU),
                     _sds((SB_HEADS, S // tm, tm, HEAD_DIM), _MXU)],
        compiler_params=_cp(("parallel",)),
    )(x, g, b, w)


def _rms(x, g):
    r = lax.rsqrt(jnp.mean(x * x, axis=-1, keepdims=True) + RMS_EPS)
    return x * r * g, r


def _mix_out(sb, sw, gsb, gsw, w_out, h0, g1, b1):
    S = sb.shape[0]
    tm = min(S, 512)

    def body(sb_ref, sw_ref, gsb_ref, gsw_ref, w_ref, h0_ref, g1_ref, b1_ref, pre_ref, mg_ref, h1_ref):
        ysb, _ = _rms(sb_ref[...], gsb_ref[...])
        ysw, _ = _rms(sw_ref[...], gsw_ref[...])
        ysb = ysb.astype(_MXU)
        ysw = ysw.astype(_MXU)
        mg_ref[:, :SB_WIDTH] = ysb
        mg_ref[:, SB_WIDTH:] = ysw
        mix = _dot(ysb, w_ref[:SB_WIDTH, :]) + _dot(ysw, w_ref[SB_WIDTH:, :])
        pre1 = ALPHA * h0_ref[...] + mix
        pre_ref[...] = pre1
        xhat, _ = _ln_hat(pre1)
        h1_ref[...] = (xhat * g1_ref[...] + b1_ref[...]).astype(h1_ref.dtype)

    vec = _fixed(1, D_MODEL)
    return pl.pallas_call(
        body, name="mix_out", grid=(S // tm,),
        in_specs=[_rows(tm, SB_WIDTH), _rows(tm, SWA_WIDTH), _fixed(1, SB_WIDTH), _fixed(1, SWA_WIDTH),
                  _fixed(D_MODEL, D_MODEL), _rows(tm, D_MODEL), vec, vec],
        out_specs=[_rows(tm, D_MODEL), _rows(tm, D_MODEL), _rows(tm, D_MODEL)],
        out_shape=[_sds((S, D_MODEL), F32), _sds((S, D_MODEL), _MXU), _sds((S, D_MODEL), _MXU)],
        compiler_params=_cp(("parallel",)),
    )(sb, sw, gsb, gsw, w_out, h0, g1, b1)


def _sigmoid(x):
    return 1.0 / (1.0 + jnp.exp(-x))


def _ffn_up(h1b, wgu):
    S = h1b.shape[0]
    tm = min(S, 1024)

    def body(h_ref, wg_ref, wu_ref, a_ref, s1_ref, s2_ref):
        h1 = h_ref[...]
        gate = _dot(h1, wg_ref[0])
        up = _dot(h1, wu_ref[0])
        sg = _sigmoid(gate)
        silu = gate * sg
        a_ref[...] = (silu * up).astype(a_ref.dtype)
        s1_ref[...] = silu.astype(s1_ref.dtype)
        s2_ref[...] = (up * (sg * (1.0 + gate * (1.0 - sg)))).astype(s2_ref.dtype)

    chunk = pl.BlockSpec((tm, FF_CHUNK), lambda j, i: (i, j))
    return pl.pallas_call(
        body, name="ffn_up", grid=(2, S // tm),
        in_specs=[pl.BlockSpec((tm, D_MODEL), lambda j, i: (i, 0)),
                  pl.BlockSpec((1, D_MODEL, FF_CHUNK), lambda j, i: (j, 0, 0)),
                  pl.BlockSpec((1, D_MODEL, FF_CHUNK), lambda j, i: (j + 2, 0, 0))],
        out_specs=[chunk, chunk, chunk],
        out_shape=[_sds((S, D_FF), _MXU)] * 3,
        compiler_params=_cp(("arbitrary", "arbitrary"), vmem_mb=56),
    )(h1b, wgu, wgu)


def _ffn_down_loss(a, w_down, pre1, g1, b1, g2, b2, tgt):
    S = a.shape[0]
    tm = min(S, 512)

    def body(a_ref, w_ref, p_ref, g1_ref, b1_ref, g2_ref, b2_ref, t_ref, d_ref, db_ref, dg2_ref, db2_ref, err_ref):
        @pl.when(pl.program_id(0) == 0)
        def _():
            dg2_ref[...] = jnp.zeros_like(dg2_ref)
            db2_ref[...] = jnp.zeros_like(db2_ref)
            err_ref[...] = jnp.zeros_like(err_ref)

        xhat1, _ = _ln_hat(p_ref[...])
        h1 = xhat1 * g1_ref[...] + b1_ref[...]
        pre2 = ALPHA * h1 + _dot(a_ref[...], w_ref[...])
        xhat2, rstd2 = _ln_hat(pre2)
        err = xhat2 * g2_ref[...] + b2_ref[...] - t_ref[...]
        dh2 = err * (1.0 / D_MODEL)
        dp2 = _ln_bwd(xhat2, rstd2, dh2, g2_ref[...])
        d_ref[...] = dp2
        db_ref[...] = dp2.astype(db_ref.dtype)
        dg2_ref[...] += _colsum(dh2 * xhat2)
        db2_ref[...] += _colsum(dh2)
        err_ref[...] += _colsum(err * err)

    vec = _fixed(1, D_MODEL)
    return pl.pallas_call(
        body, name="ffn_down_loss", grid=(S // tm,),
        in_specs=[_rows(tm, D_FF), _fixed(D_FF, D_MODEL), _rows(tm, D_MODEL), vec, vec, vec, vec, _rows(tm, D_MODEL)],
        out_specs=[_rows(tm, D_MODEL), _rows(tm, D_MODEL), vec, vec, vec],
        out_shape=[_sds((S, D_MODEL), F32), _sds((S, D_MODEL), _MXU), _sds((1, D_MODEL), F32), _sds((1, D_MODEL), F32),
                   _sds((1, D_MODEL), F32)],
        compiler_params=_cp(("arbitrary",)),
    )(a, w_down, pre1, g1, b1, g2, b2, tgt)


def _ffn_down_bwd(dp2b, w_down, s1, s2):
    S = dp2b.shape[0]
    tm = min(S, 1024)

    def body(d_ref, w_ref, s1_ref, s2_ref, dg_ref, du_ref):
        da = _dot_nt(d_ref[...], w_ref[...])
        du_ref[...] = (da * s1_ref[...].astype(F32)).astype(du_ref.dtype)
        dg_ref[...] = (da * s2_ref[...].astype(F32)).astype(dg_ref.dtype)

    chunk = pl.BlockSpec((tm, FF_CHUNK), lambda j, i: (i, j))
    return pl.pallas_call(
        body, name="ffn_down_bwd", grid=(2, S // tm),
        in_specs=[pl.BlockSpec((tm, D_MODEL), lambda j, i: (i, 0)),
                  pl.BlockSpec((FF_CHUNK, D_MODEL), lambda j, i: (j, 0)), chunk, chunk],
        out_specs=[chunk, chunk],
        out_shape=[_sds((S, D_FF), _MXU), _sds((S, D_FF), _MXU)],
        compiler_params=_cp(("arbitrary", "arbitrary")),
    )(dp2b, w_down, s1, s2)


def _ffn_up_bwd(dgate, dup, wgu, dp2, pre1, g1):
    S = dgate.shape[0]
    tm = min(S, 256)

    def body(dg_ref, du_ref, w_ref, d2_ref, p_ref, g_ref, d1_ref, d1b_ref, dg1_ref, db1_ref):
        @pl.when(pl.program_id(0) == 0)
        def _():
            dg1_ref[...] = jnp.zeros_like(dg1_ref)
            db1_ref[...] = jnp.zeros_like(db1_ref)

        dh1 = ALPHA * d2_ref[...]
        for j in range(2):
            cols = slice(j * FF_CHUNK, (j + 1) * FF_CHUNK)
            dh1 += _dot_nt(dg_ref[:, cols], w_ref[j])
            dh1 += _dot_nt(du_ref[:, cols], w_ref[j + 2])
        xhat, rstd = _ln_hat(p_ref[...])
        dp1 = _ln_bwd(xhat, rstd, dh1, g_ref[...])
        d1_ref[...] = dp1
        d1b_ref[...] = dp1.astype(d1b_ref.dtype)
        dg1_ref[...] += _colsum(dh1 * xhat)
        db1_ref[...] += _colsum(dh1)

    vec = _fixed(1, D_MODEL)
    return pl.pallas_call(
        body, name="ffn_up_bwd", grid=(S // tm,),
        in_specs=[_rows(tm, D_FF), _rows(tm, D_FF), _fixed(4, D_MODEL, FF_CHUNK), _rows(tm, D_MODEL),
                  _rows(tm, D_MODEL), vec],
        out_specs=[_rows(tm, D_MODEL), _rows(tm, D_MODEL), vec, vec],
        out_shape=[_sds((S, D_MODEL), F32), _sds((S, D_MODEL), _MXU), _sds((1, D_MODEL), F32), _sds((1, D_MODEL), F32)],
        compiler_params=_cp(("arbitrary",), vmem_mb=56),
    )(dgate, dup, wgu, dp2, pre1, g1)


def _rms_bwd(x, g, dy):
    n = x.shape[-1]
    r = lax.rsqrt(jnp.mean(x * x, axis=-1, keepdims=True) + RMS_EPS)
    u = dy * g
    dx = r * u - x * (r * r * r) * (jnp.sum(u * x, axis=-1, keepdims=True) * (1.0 / n))
    return dx, _colsum(dy * x * r)


def _mix_bwd(dp1b, w_out, sb, sw, gsb, gsw):
    S = sb.shape[0]
    tm = min(S, 512)

    def body(d_ref, w_ref, sb_ref, sw_ref, gsb_ref, gsw_ref, dsb_ref, dsw_ref, dgsb_ref, dgsw_ref):
        @pl.when(pl.program_id(0) == 0)
        def _():
            dgsb_ref[...] = jnp.zeros_like(dgsb_ref)
            dgsw_ref[...] = jnp.zeros_like(dgsw_ref)

        dm = _dot_nt(d_ref[...], w_ref[...])
        dsb, dgsb = _rms_bwd(sb_ref[...], gsb_ref[...], dm[:, :SB_WIDTH])
        dsw, dgsw = _rms_bwd(sw_ref[...], gsw_ref[...], dm[:, SB_WIDTH:])
        dsb_ref[...] = dsb.T.astype(dsb_ref.dtype).reshape(dsb_ref.shape)
        dsw_ref[...] = dsw.astype(dsw_ref.dtype)
        dgsb_ref[...] += dgsb
        dgsw_ref[...] += dgsw

    return pl.pallas_call(
        body, name="mix_bwd", grid=(S // tm,),
        in_specs=[_rows(tm, D_MODEL), _fixed(D_MODEL, D_MODEL), _rows(tm, SB_WIDTH), _rows(tm, SWA_WIDTH),
                  _fixed(1, SB_WIDTH), _fixed(1, SWA_WIDTH)],
        out_specs=[pl.BlockSpec((SB_HEADS, HEAD_DIM, tm), lambda i: (0, 0, i)), _rows(tm, SWA_WIDTH),
                   _fixed(1, SB_WIDTH), _fixed(1, SWA_WIDTH)],
        out_shape=[_sds((SB_HEADS, HEAD_DIM, S), _MXU), _sds((S, SWA_WIDTH), _MXU), _sds((1, SB_WIDTH), F32),
                   _sds((1, SWA_WIDTH), F32)],
        compiler_params=_cp(("arbitrary",)),
    )(dp1b, w_out, sb, sw, gsb, gsw)


def _in_proj_bwd(dproj, w_in, dp1, x, g, parts):
    S = x.shape[0]
    tm = min(S, 512)
    nw = len(parts)
    ns = len(IN_SECTIONS)
    offs = np.cumsum((0,) + IN_SECTIONS)
    s_ins, s_outs, s_sems = _scatter_io(parts)

    def body(*refs):
        dpj_refs = refs[:ns]
        w_ref, d1_ref, x_ref, g_ref = refs[ns:ns + 4]
        rest = refs[ns + 4:]
        gx_ref, dg_ref, db_ref = rest[nw:nw + 3]
        scatter = _Scatter(rest[:nw], rest[nw + 3:2 * nw + 3], *rest[2 * nw + 3:])

        @pl.when(pl.program_id(0) == 0)
        def _():
            scatter.start()
            dg_ref[...] = jnp.zeros_like(dg_ref)
            db_ref[...] = jnp.zeros_like(db_ref)

        dh0 = ALPHA * d1_ref[...]
        for k in range(ns):
            dh0 += _dot_nt(dpj_refs[k][...], w_ref[:, offs[k]:offs[k + 1]])
        xhat, rstd = _ln_hat(x_ref[...])
        gx_ref[...] = _ln_bwd(xhat, rstd, dh0, g_ref[...])
        dg_ref[...] += _colsum(dh0 * xhat)
        db_ref[...] += _colsum(dh0)

        @pl.when(pl.program_id(0) == pl.num_programs(0) - 1)
        def _():
            scatter.finish()

    vec = _fixed(1, D_MODEL)
    any_spec = pl.BlockSpec(memory_space=pl.ANY)
    res = pl.pallas_call(
        body, name="in_proj_bwd", grid=(S // tm,),
        in_specs=[_rows(tm, n) for n in IN_SECTIONS]
                 + [_fixed(D_MODEL, IN_COLS), _rows(tm, D_MODEL), _rows(tm, D_MODEL), vec] + [any_spec] * nw,
        out_specs=[_rows(tm, D_MODEL), vec, vec] + [any_spec] * nw,
        out_shape=[_sds((S, D_MODEL), F32), _sds((1, D_MODEL), F32), _sds((1, D_MODEL), F32)] + s_outs,
        scratch_shapes=s_sems,
        compiler_params=_cp(("arbitrary",)),
    )(*dproj, w_in, dp1, x, g, *s_ins)
    return res[0], res[1], res[2], list(res[3:])


def _matmul_tn(a, b, name, tk, tn):
    T, K = a.shape
    N = b.shape[1]
    tt = min(T, 1024)

    def body(a_ref, b_ref, o_ref):
        @pl.when(pl.program_id(2) == 0)
        def _():
            o_ref[...] = jnp.zeros_like(o_ref)

        o_ref[...] += _dot_tn(a_ref[...], b_ref[...])

    return pl.pallas_call(
        body, name=name, grid=(K // tk, N // tn, T // tt),
        in_specs=[pl.BlockSpec((tt, tk), lambda k, n, t: (t, k)), pl.BlockSpec((tt, tn), lambda k, n, t: (t, n))],
        out_specs=pl.BlockSpec((tk, tn), lambda k, n, t: (k, n)),
        out_shape=_sds((K, N), F32),
        compiler_params=_cp(("parallel", "parallel", "arbitrary")),
    )(a, b)


def _place():
    x, y, c = lax.axis_index("x"), lax.axis_index("y"), lax.axis_index("c")
    chips = [(1 - x, y), (x, 1 - y), (1 - x, 1 - y)]
    return x, y, c, chips


class _Gather:
    def __init__(self, in_refs, out_refs, send_sems, recv_sems):
        self.in_refs, self.out_refs, self.send_sems, self.recv_sems = in_refs, out_refs, send_sems, recv_sems
        self.x, self.y, self.c, self.chips = _place()

    def _copy(self, w, k, chip, hc, to, src=None):
        part = self.out_refs[w].at[2 * chip[0] + chip[1], hc]
        return pltpu.make_async_remote_copy(
            src_ref=part if src is None else src, dst_ref=part, send_sem=self.send_sems.at[w, k],
            recv_sem=self.recv_sems.at[w, k], device_id=to, device_id_type=MESH)

    def _first(self):
        x, y, c = self.x, self.y, self.c
        return [self._copy(w, j, (x, y), c, (*chip, c), src=self.in_refs[w].at[c])
                for w in range(len(self.in_refs)) for j, chip in enumerate(self.chips)]

    def start(self):
        for cp in self._first():
            cp.start()

    def _passed(self):
        sibling = (self.x, self.y, 1 - self.c)
        return [self._copy(w, 3 + j, chip, self.c, sibling)
                for w in range(len(self.in_refs)) for j, chip in enumerate(self.chips)]

    def forward(self):
        me = (self.x, self.y, self.c)
        passed = self._passed()
        for w in range(len(self.in_refs)):
            for j, chip in enumerate(self.chips):
                self._copy(w, j, chip, self.c, me).wait_recv()
                passed[3 * w + j].start()

    def finish(self):
        me = (self.x, self.y, self.c)
        for w in range(len(self.in_refs)):
            for j, chip in enumerate(self.chips):
                self._copy(w, 3 + j, chip, 1 - self.c, me).wait_recv()
        for cp in self._first() + self._passed():
            cp.wait_send()


def _gather_io(shards):
    halves = [(s.shape[0] // 2, s.shape[1]) for s in shards]
    ins = [s.reshape(2, h, cols) for s, (h, cols) in zip(shards, halves)]
    outs = [_sds((N_CHIPS, 2, h, cols), s.dtype) for s, (h, cols) in zip(shards, halves)]
    sems = [pltpu.SemaphoreType.DMA((len(shards), 6)), pltpu.SemaphoreType.DMA((len(shards), 6))]
    return ins, outs, sems


def _gather_assemble(outs, shards):
    me = 2 * lax.axis_index("x") + lax.axis_index("y")
    return [lax.dynamic_update_slice_in_dim(o.reshape((N_CHIPS,) + s.shape), s[None], me, axis=0)
            for o, s in zip(outs, shards)]


class _Scatter:
    def __init__(self, p_refs, out_refs, send_sems, recv_sems):
        self.p_refs, self.out_refs, self.send_sems, self.recv_sems = p_refs, out_refs, send_sems, recv_sems
        self.x, self.y, self.c, self.chips = _place()
        self.me = 2 * self.x + self.y

    def _copy(self, w, j, chip, src_chip, dst_chip):
        return pltpu.make_async_remote_copy(
            src_ref=self.p_refs[w].at[src_chip], dst_ref=self.out_refs[w].at[dst_chip], send_sem=self.send_sems.at[w, j],
            recv_sem=self.recv_sems.at[w, j], device_id=(*chip, self.c), device_id_type=MESH)

    def _sends(self):
        return [self._copy(w, j, chip, 2 * chip[0] + chip[1], self.me)
                for w in range(len(self.p_refs)) for j, chip in enumerate(self.chips)]

    def start(self):
        for cp in self._sends():
            cp.start()

    def finish(self):
        for w in range(len(self.p_refs)):
            for j, chip in enumerate(self.chips):
                self._copy(w, j, chip, self.me, 2 * chip[0] + chip[1]).wait_recv()
        for cp in self._sends():
            cp.wait_send()


def _scatter_io(parts):
    sems = [pltpu.SemaphoreType.DMA((len(parts), 3)), pltpu.SemaphoreType.DMA((len(parts), 3))]
    return list(parts), [_sds(p.shape, p.dtype) for p in parts], sems


class _Swap:
    def __init__(self, g_refs, out_refs, send_sems, recv_sems):
        x, y, c, _ = _place()
        self.copies = []
        for w in range(len(g_refs)):
            half = out_refs[w].shape[1]
            theirs = g_refs[w].at[:, pl.ds(pl.multiple_of((1 - c) * half, 8), half), :]
            self.copies.append(pltpu.make_async_remote_copy(
                src_ref=theirs, dst_ref=out_refs[w], send_sem=send_sems.at[w], recv_sem=recv_sems.at[w],
                device_id=(x, y, 1 - c), device_id_type=MESH))

    def start(self):
        for cp in self.copies:
            cp.start()

    def finish(self):
        for cp in self.copies:
            cp.wait()


def _swap_io(grads):
    outs = [_sds((g.shape[0], g.shape[1] // 2, g.shape[2]), g.dtype) for g in grads]
    return list(grads), outs, [pltpu.SemaphoreType.DMA((len(grads),)), pltpu.SemaphoreType.DMA((len(grads),))]


def _matmul_tn_pair(a, b0, b1, name):
    T, K = a.shape
    tt = min(T, 1024)

    def body(a_ref, b0_ref, b1_ref, o_ref):
        n = pl.program_id(0)

        @pl.when(pl.program_id(1) == 0)
        def _():
            o_ref[...] = jnp.zeros_like(o_ref)

        @pl.when(n < 2)
        def _():
            o_ref[0] += _dot_tn(a_ref[...], b0_ref[...])

        @pl.when(n >= 2)
        def _():
            o_ref[0] += _dot_tn(a_ref[...], b1_ref[...])

    return pl.pallas_call(
        body, name=name, grid=(4, T // tt),
        in_specs=[pl.BlockSpec((tt, K), lambda n, t: (t, 0)),
                  pl.BlockSpec((tt, FF_CHUNK), lambda n, t: (t, jnp.minimum(n, 1))),
                  pl.BlockSpec((tt, FF_CHUNK), lambda n, t: (t, jnp.maximum(n - 2, 0)))],
        out_specs=pl.BlockSpec((1, K, FF_CHUNK), lambda n, t: (n, 0, 0)),
        out_shape=_sds((4, K, FF_CHUNK), F32),
        compiler_params=_cp(("parallel", "arbitrary")),
    )(a, b0, b1)


def _sb_logs(zt, causal):
    e = jnp.exp(-jnp.abs(zt))
    lb = jnp.minimum(zt, 0.0) - jnp.log(1.0 + e)
    l1m = lb - zt
    if causal is not None:
        l1m = jnp.where(causal, l1m, 0.0)
    return lb, l1m


def _sb_weights(lb, suf, causal):
    a = jnp.exp(lb + suf)
    if causal is not None:
        a = jnp.where(causal, a, 0.0)
    return a


def _tri_masks(t):
    r = lax.broadcasted_iota(jnp.int32, (t, t), 0)
    c = lax.broadcasted_iota(jnp.int32, (t, t), 1)
    return r, c


def _sb_fwd(qT, kb, vTb, shards):
    Hh, _, S = qT.shape
    nk, T = kb.shape[1], kb.shape[2]
    nq = S // T
    G = SB_GROUP_FWD
    nw = len(shards)
    g_ins, g_outs, g_sems = _gather_io(shards)
    forward_step = max(nq - 1 - SB_FORWARD_LEAD, 0)

    def body(qT_ref, k_ref, vT_ref, *rest):
        o_ref, rs_ref, first_ref = rest[nw:nw + 3]
        gather = _Gather(rest[:nw], rest[nw + 3:2 * nw + 3], *rest[2 * nw + 3:])
        i = pl.program_id(1)
        first_step = jnp.logical_and(pl.program_id(0) == 0, i == 0)
        last_step = jnp.logical_and(pl.program_id(0) == pl.num_programs(0) - 1, i == pl.num_programs(1) - 1)

        @pl.when(first_step)
        def _():
            gather.start()

        qts = [(qT_ref[g].astype(F32) * SCALE).astype(_MXU) for g in range(G)]
        r, c = _tri_masks(T)
        upper = (c > r).astype(_MXU)
        causal = r < c

        def blk(j, carry, mask):
            hs = range(G)
            for g in hs:
                rs_ref[g, 0, j] = jnp.broadcast_to(carry[g][0], (8, T))
            zs = [_dot(k_ref[g, j], qts[g]) for g in hs]
            lbs, l1ms = zip(*[_sb_logs(zs[g], mask) for g in hs])
            splits = [_split2(l1ms[g]) for g in hs]
            cums = [_dot(upper, splits[g][0]) + _dot(upper, splits[g][1]) for g in hs]
            avs = [_sb_weights(lbs[g], carry[g][0] + cums[g], mask).astype(_MXU) for g in hs]
            accs = [carry[g][1] + _dot(vT_ref[g, j], avs[g]) for g in hs]
            return tuple((carry[g][0] + _colsum(l1ms[g]), accs[g]) for g in hs)

        def go_on(j, carry):
            top = carry[0][0]
            for g in range(1, G):
                top = jnp.maximum(top, carry[g][0])
            return jnp.logical_and(j >= 0, jnp.max(top) >= SB_DEAD)

        init = tuple((jnp.zeros((1, T), F32), jnp.zeros((HEAD_DIM, T), F32)) for _ in range(G))
        carry = blk(i, init, causal)
        j, carry = lax.while_loop(lambda st: go_on(*st), lambda st: (st[0] - 1, blk(st[0], st[1], None)),
                                  (i - 1, carry))

        first_ref[...] = jnp.broadcast_to((j + 1).astype(F32), first_ref.shape)

        o_ref[...] = jnp.concatenate([carry[g][1] for g in range(G)], axis=0).T

        @pl.when(jnp.logical_and(pl.program_id(0) == pl.num_programs(0) - 1, i == forward_step))
        def _():
            gather.forward()

        @pl.when(last_step)
        def _():
            gather.finish()

    any_spec = pl.BlockSpec(memory_space=pl.ANY)
    res = pl.pallas_call(
        body, name="sb_fwd", grid=(Hh // G, nq),
        in_specs=[pl.BlockSpec((G, HEAD_DIM, T), lambda h, i: (h, 0, i)),
                  pl.BlockSpec((G, nk, T, HEAD_DIM), lambda h, i: (h, 0, 0, 0), pipeline_mode=pl.Buffered(1)),
                  pl.BlockSpec((G, nk, HEAD_DIM, T), lambda h, i: (h, 0, 0, 0), pipeline_mode=pl.Buffered(1))]
                 + [any_spec] * nw,
        out_specs=[pl.BlockSpec((T, G * HEAD_DIM), lambda h, i: (i, h)),
                   pl.BlockSpec((G, 1, nk, 8, T), lambda h, i: (h, i, 0, 0, 0)),
                   pl.BlockSpec((1, 1, 8, 128), lambda h, i: (h, i, 0, 0))] + [any_spec] * nw,
        out_shape=[_sds((S, Hh * HEAD_DIM), F32), _sds((Hh, nq, nk, 8, T), F32), _sds((Hh // G, nq, 8, 128), F32)]
                  + g_outs,
        scratch_shapes=g_sems,
        compiler_params=_cp(("arbitrary", "arbitrary")),
    )(qT, kb, vTb, *g_ins)
    return res[0], res[1], res[2], _gather_assemble(res[3:], shards)


def _sb_bwd(qT, kb, kTb, vb, doT, rsave, first, parts):
    Hh, _, S = qT.shape
    nk, T = kb.shape[1], kb.shape[2]
    nq = S // T
    G = SB_GROUP_BWD
    nw = len(parts)
    s_ins, s_outs, s_sems = _scatter_io(parts)

    def body(qT_ref, k_ref, kT_ref, v_ref, doT_ref, rs_ref, first_ref, *rest):
        dq_ref, dk_out_ref, dv_out_ref = rest[nw:nw + 3]
        dk_ref, dv_ref = rest[2 * nw + 3:2 * nw + 5]
        scatter = _Scatter(rest[:nw], rest[nw + 3:2 * nw + 3], *rest[2 * nw + 5:])
        i = pl.program_id(1)
        first_step = jnp.logical_and(pl.program_id(0) == 0, i == 0)
        last_step = jnp.logical_and(pl.program_id(0) == pl.num_programs(0) - 1, i == pl.num_programs(1) - 1)

        @pl.when(first_step)
        def _():
            scatter.start()

        @pl.when(i == 0)
        def _():
            dk_ref[...] = jnp.zeros_like(dk_ref)
            dv_ref[...] = jnp.zeros_like(dv_ref)

        qts = [(qT_ref[g].astype(F32) * SCALE).astype(_MXU) for g in range(G)]
        douts = [doT_ref[g] for g in range(G)]
        r, c = _tri_masks(T)
        upper = (c > r).astype(_MXU)
        lower = (c < r).astype(_MXU)
        causal = r < c

        def blk(j, carry, mask):
            hs = range(G)
            zs = [_dot(k_ref[g, j], qts[g]) for g in hs]
            das = [_dot(v_ref[g, j], douts[g]) for g in hs]
            lbs, l1ms = zip(*[_sb_logs(zs[g], mask) for g in hs])
            splits = [_split2(l1ms[g]) for g in hs]
            cums = [_dot(upper, splits[g][0]) + _dot(upper, splits[g][1]) for g in hs]
            avs = [_sb_weights(lbs[g], rs_ref[g, 0, j][0:1, :] + cums[g], mask) for g in hs]
            ets = [das[g] * avs[g] for g in hs]
            esplits = [_split2(ets[g]) for g in hs]
            ecums = [_dot(lower, esplits[g][0]) + _dot(lower, esplits[g][1]) for g in hs]
            dzs = []
            for g in hs:
                sig = jnp.exp(lbs[g])
                dz = ets[g] * (1.0 - sig) - (carry[g][0] + ecums[g]) * sig
                if mask is not None:
                    dz = jnp.where(mask, dz, 0.0)
                dzs.append(dz.astype(_MXU))
            dqs = [carry[g][1] + _dot(kT_ref[g, j], dzs[g]) for g in hs]
            for g in hs:
                dk_ref[j, g * HEAD_DIM:(g + 1) * HEAD_DIM, :] += _dot_nt(qts[g], dzs[g])
            for g in hs:
                dv_ref[j, g * HEAD_DIM:(g + 1) * HEAD_DIM, :] += _dot_nt(douts[g], avs[g].astype(_MXU))
            return tuple((carry[g][0] + _colsum(ets[g]), dqs[g]) for g in hs)

        first = jnp.clip(jnp.max(first_ref[0, 0][0:1, 0:1]).astype(jnp.int32), 0, i)
        carry = tuple((jnp.zeros((1, T), F32), jnp.zeros((HEAD_DIM, T), F32)) for _ in range(G))
        carry = lax.fori_loop(first, i, lambda s, cr: blk(s, cr, None), carry)
        carry = blk(i, carry, causal)
        dq_ref[...] = (jnp.concatenate([carry[g][1] for g in range(G)], axis=0) * SCALE).T.astype(dq_ref.dtype)

        @pl.when(i == pl.num_programs(1) - 1)
        def _():
            def flush(j, _):
                rows = pl.ds(pl.multiple_of(j * T, T), T)
                dk_out_ref[rows, :] = dk_ref[j].T.astype(dk_out_ref.dtype)
                dv_out_ref[rows, :] = dv_ref[j].T.astype(dv_out_ref.dtype)
                return 0
            lax.fori_loop(0, nk, flush, 0)

        @pl.when(last_step)
        def _():
            scatter.finish()

    colblk = pl.BlockSpec((G, HEAD_DIM, T), lambda h, i: (h, 0, i))
    once = pl.Buffered(1)
    kblk = pl.BlockSpec((G, nk, T, HEAD_DIM), lambda h, i: (h, 0, 0, 0), pipeline_mode=once)
    kTblk = pl.BlockSpec((G, nk, HEAD_DIM, T), lambda h, i: (h, 0, 0, 0), pipeline_mode=once)
    any_spec = pl.BlockSpec(memory_space=pl.ANY)
    res = pl.pallas_call(
        body, name="sb_bwd", grid=(Hh // G, nq),
        in_specs=[colblk, kblk, kTblk, kblk, colblk,
                  pl.BlockSpec((G, 1, nk, 8, T), lambda h, i: (h, i, 0, 0, 0)),
                  pl.BlockSpec((1, 1, 8, 128), lambda h, i: ((h * G) // SB_GROUP_FWD, i, 0, 0))] + [any_spec] * nw,
        out_specs=[pl.BlockSpec((T, G * HEAD_DIM), lambda h, i: (i, h)),
                   pl.BlockSpec((S, G * HEAD_DIM), lambda h, i: (0, h), pipeline_mode=once),
                   pl.BlockSpec((S, G * HEAD_DIM), lambda h, i: (0, h), pipeline_mode=once)] + [any_spec] * nw,
        out_shape=[_sds((S, Hh * HEAD_DIM), _MXU)] * 3 + s_outs,
        scratch_shapes=[pltpu.VMEM((nk, G * HEAD_DIM, T), F32), pltpu.VMEM((nk, G * HEAD_DIM, T), F32)] + s_sems,
        compiler_params=_cp(("arbitrary", "arbitrary"), vmem_mb=60),
    )(qT, kb, kTb, vb, doT, rsave, first, *s_ins)
    return res[0], res[1], res[2], list(res[3:])


def _bucket_table():
    qi = np.arange(BLOCK)[:, None]
    cj = np.arange(2 * BLOCK)[None, :]
    dist = qi + BLOCK - cj
    exact = REL_BUCKETS // 2
    d = np.maximum(dist, 0)
    d_f = np.maximum(d, 1).astype(np.float32)
    large = exact + (np.log(d_f / np.float32(exact)) / np.float32(math.log(REL_MAX_DIST / exact))
                     * np.float32(REL_BUCKETS - exact)).astype(np.int32)
    large = np.minimum(large, REL_BUCKETS - 1)
    return np.where(d < exact, d, large).astype(np.int32)


def _swa_bias(rel_bias, bucket):
    def body(rb_ref, bk_ref, o_ref):
        bk = bk_ref[...]
        for h in range(SWA_HEADS):
            t = jnp.zeros((2 * BLOCK, BLOCK), F32)
            for b in range(REL_BUCKETS):
                t = jnp.where(bk == b, rb_ref[b, h], t)
            o_ref[h] = t

    return pl.pallas_call(
        body, name="swa_bias",
        in_specs=[pl.BlockSpec(memory_space=pltpu.SMEM), pl.BlockSpec(memory_space=pltpu.VMEM)],
        out_specs=pl.BlockSpec(memory_space=pltpu.VMEM),
        out_shape=_sds((SWA_HEADS, 2 * BLOCK, BLOCK), F32),
    )(rel_bias, bucket)


def _swa_logits(q, kp, kc):
    qs = (q.astype(F32) * SCALE).astype(_MXU)
    return qs, _dot_nt(kp, qs), _dot_nt(kc, qs)


def _swa_softmax(lp, lc, bias, sink, live_prev):
    r, c = _tri_masks(BLOCK)
    in_window = r > c if live_prev is None else jnp.logical_and(r > c, live_prev)
    lp = jnp.where(in_window, lp + bias[:BLOCK, :], -jnp.inf)
    lc = jnp.where(r <= c, lc + bias[BLOCK:, :], -jnp.inf)
    m = jnp.maximum(jnp.maximum(jnp.max(lp, axis=0, keepdims=True), jnp.max(lc, axis=0, keepdims=True)), sink)
    pp = jnp.exp(lp - m)
    pc = jnp.exp(lc - m)
    ps = jnp.exp(sink - m)
    denom = _colsum(pp) + _colsum(pc) + ps
    return pp / denom, pc / denom, ps / denom


def _swa_sub(nb):
    return min(SWA_SUB, nb)


def _swa_keys(b, prev_ref, cur_ref, i):
    cur = cur_ref[0, b * BLOCK:(b + 1) * BLOCK, :]
    if b == 0:
        return prev_ref[0], cur, i > 0
    return cur_ref[0, (b - 1) * BLOCK:b * BLOCK, :], cur, None


def _swa_keys_t(b, prev_ref, cur_ref):
    cur = cur_ref[0, :, b * BLOCK:(b + 1) * BLOCK]
    return (prev_ref[0] if b == 0 else cur_ref[0, :, (b - 1) * BLOCK:b * BLOCK]), cur


SWA_PAIR = 4


def _swa_fwd(q, k, vT, bias, sink):
    S = q.shape[1]
    nb = S // BLOCK
    ns = _swa_sub(nb)
    R = ns * BLOCK
    P = SWA_PAIR

    def body(q_ref, kp_ref, kc_ref, vp_ref, vc_ref, bias_ref, sink_ref, o_ref):
        i = pl.program_id(1)
        units = [(hh, b) for hh in range(P) for b in range(ns)]
        keys = [_swa_keys(b, kp_ref, kc_ref, i) for b in range(ns)]
        vals = [_swa_keys_t(b, vp_ref, vc_ref) for b in range(ns)]
        logits = {u: _swa_logits(q_ref[u[0], u[1] * BLOCK:(u[1] + 1) * BLOCK, :], keys[u[1]][0], keys[u[1]][1])
                  for u in units}
        ws = {u: _swa_softmax(logits[u][1], logits[u][2], bias_ref[u[0]], sink_ref[u[0]][:, :1], keys[u[1]][2])
              for u in units}
        outs = {u: _dot(vals[u[1]][0], ws[u][0].astype(_MXU)) + _dot(vals[u[1]][1], ws[u][1].astype(_MXU))
                for u in units}
        for b in range(ns):
            o_ref[b * BLOCK:(b + 1) * BLOCK, :] = jnp.concatenate([outs[(hh, b)] for hh in range(P)], axis=0).T

    kvh = lambda p: (p * P) // SWA_GROUP
    prev = pl.BlockSpec((1, BLOCK, HEAD_DIM), lambda p, i: (kvh(p), jnp.maximum(i * ns - 1, 0), 0))
    cur = pl.BlockSpec((1, R, HEAD_DIM), lambda p, i: (kvh(p), i, 0))
    prev_t = pl.BlockSpec((1, HEAD_DIM, BLOCK), lambda p, i: (kvh(p), 0, jnp.maximum(i * ns - 1, 0)))
    cur_t = pl.BlockSpec((1, HEAD_DIM, R), lambda p, i: (kvh(p), 0, i))
    return pl.pallas_call(
        body, name="swa_fwd", grid=(SWA_HEADS // P, nb // ns),
        in_specs=[pl.BlockSpec((P, R, HEAD_DIM), lambda p, i: (p, i, 0)), prev, cur, prev_t, cur_t,
                  pl.BlockSpec((P, 2 * BLOCK, BLOCK), lambda p, i: (p, 0, 0)),
                  pl.BlockSpec((P, 1, BLOCK), lambda p, i: (p, 0, 0))],
        out_specs=pl.BlockSpec((R, P * HEAD_DIM), lambda p, i: (i, p)),
        out_shape=_sds((S, SWA_HEADS * HEAD_DIM), F32),
        compiler_params=_cp(("parallel", "parallel")),
    )(q, k, k, vT, vT, bias, sink)


def _swa_bwd(q, k, kT, v, bias, sink, do, grads):
    S = q.shape[1]
    nb = S // BLOCK
    ns = _swa_sub(nb)
    R = ns * BLOCK
    P = SWA_PAIR
    nw = len(grads)
    x_ins, x_outs, x_sems = _swap_io(grads)

    def body(q_ref, kp_ref, kc_ref, ktp_ref, ktc_ref, vp_ref, vc_ref, bias_ref, sink_ref, do_ref, *rest):
        dq_ref, dk_ref, dv_ref, dbias_ref, dsink_ref = rest[nw:nw + 5]
        swap = _Swap(rest[:nw], rest[nw + 5:2 * nw + 5], *rest[2 * nw + 5:])
        g = pl.program_id(1)
        i = pl.program_id(2)
        first_step = jnp.logical_and(pl.program_id(0) == 0, jnp.logical_and(g == 0, i == 0))
        last_step = jnp.logical_and(pl.program_id(0) == pl.num_programs(0) - 1,
                                    jnp.logical_and(g == pl.num_programs(1) - 1, i == pl.num_programs(2) - 1))

        @pl.when(first_step)
        def _():
            swap.start()

        @pl.when(jnp.logical_and(g == 0, i == 0))
        def _():
            dk_ref[...] = jnp.zeros_like(dk_ref)
            dv_ref[...] = jnp.zeros_like(dv_ref)

        @pl.when(i == 0)
        def _():
            dbias_ref[...] = jnp.zeros_like(dbias_ref)
            dsink_ref[...] = jnp.zeros_like(dsink_ref)

        subs = range(ns)
        units = [(hh, b) for hh in range(P) for b in subs]
        rows = [slice(b * BLOCK, (b + 1) * BLOCK) for b in subs]
        keys = [_swa_keys(b, kp_ref, kc_ref, i) for b in subs]
        keys_t = [_swa_keys_t(b, ktp_ref, ktc_ref) for b in subs]
        vals = [_swa_keys(b, vp_ref, vc_ref, i) for b in subs]
        douts = {u: do_ref[u[0], rows[u[1]], :] for u in units}
        logits = {u: _swa_logits(q_ref[u[0], rows[u[1]], :], keys[u[1]][0], keys[u[1]][1]) for u in units}
        dws = {u: (_dot_nt(vals[u[1]][0], douts[u]), _dot_nt(vals[u[1]][1], douts[u])) for u in units}
        wts, dls = {}, {}
        for hh in range(P):
            dbp = jnp.zeros((BLOCK, BLOCK), F32)
            dbc = jnp.zeros((BLOCK, BLOCK), F32)
            dsk = jnp.zeros((1, BLOCK), F32)
            for b in subs:
                u = (hh, b)
                wp, wc, ws = _swa_softmax(logits[u][1], logits[u][2], bias_ref[hh], sink_ref[hh][:, :1], keys[b][2])
                dwp, dwc = dws[u]
                delta = _colsum(wp * dwp) + _colsum(wc * dwc)
                dlp = wp * (dwp - delta)
                dlc = wc * (dwc - delta)
                dbp += dlp
                dbc += dlc
                dsk -= ws * delta
                wts[u] = (wp.astype(_MXU), wc.astype(_MXU))
                dls[u] = (dlp.astype(_MXU), dlc.astype(_MXU))
            dbias_ref[hh, :BLOCK, :] += dbp
            dbias_ref[hh, BLOCK:, :] += dbc
            dsink_ref[hh] += jnp.broadcast_to(dsk, (8, BLOCK))
        dqs = {u: (_dot(keys_t[u[1]][0], dls[u][0]) + _dot(keys_t[u[1]][1], dls[u][1])) * SCALE for u in units}
        for b in subs:
            dq_ref[rows[b], :] = jnp.concatenate([dqs[(hh, b)] for hh in range(P)], axis=0).T.astype(dq_ref.dtype)
        dk_cur = [sum(_dot(dls[(hh, b)][1], logits[(hh, b)][0]) for hh in range(P)) for b in subs]
        dv_cur = [sum(_dot(wts[(hh, b)][1], douts[(hh, b)]) for hh in range(P)) for b in subs]
        dk_prev = [sum(_dot(dls[(hh, b)][0], logits[(hh, b)][0]) for hh in range(P)) for b in subs]
        dv_prev = [sum(_dot(wts[(hh, b)][0], douts[(hh, b)]) for hh in range(P)) for b in subs]
        for b in subs:
            last = b + 1 == ns
            dk_ref[0, i * ns + b] += dk_cur[b] if last else dk_cur[b] + dk_prev[b + 1]
            dv_ref[0, i * ns + b] += dv_cur[b] if last else dv_cur[b] + dv_prev[b + 1]

        @pl.when(i > 0)
        def _():
            dk_ref[0, i * ns - 1] += dk_prev[0]
            dv_ref[0, i * ns - 1] += dv_prev[0]

        @pl.when(last_step)
        def _():
            swap.finish()

    G2 = SWA_GROUP // P
    hp = lambda kv, g, i: kv * G2 + g
    prev = pl.BlockSpec((1, BLOCK, HEAD_DIM), lambda kv, g, i: (kv, jnp.maximum(i * ns - 1, 0), 0))
    cur = pl.BlockSpec((1, R, HEAD_DIM), lambda kv, g, i: (kv, i, 0))
    prev_t = pl.BlockSpec((1, HEAD_DIM, BLOCK), lambda kv, g, i: (kv, 0, jnp.maximum(i * ns - 1, 0)))
    cur_t = pl.BlockSpec((1, HEAD_DIM, R), lambda kv, g, i: (kv, 0, i))
    qblk = pl.BlockSpec((P, R, HEAD_DIM), lambda kv, g, i: (hp(kv, g, i), i, 0))
    kvacc = pl.BlockSpec((1, nb, BLOCK, HEAD_DIM), lambda kv, g, i: (kv, 0, 0, 0))
    any_spec = pl.BlockSpec(memory_space=pl.ANY)
    res = pl.pallas_call(
        body, name="swa_bwd", grid=(SWA_KV_HEADS, G2, nb // ns),
        in_specs=[qblk, prev, cur, prev_t, cur_t, prev, cur,
                  pl.BlockSpec((P, 2 * BLOCK, BLOCK), lambda kv, g, i: (hp(kv, g, i), 0, 0)),
                  pl.BlockSpec((P, 1, BLOCK), lambda kv, g, i: (hp(kv, g, i), 0, 0)), qblk] + [any_spec] * nw,
        out_specs=[pl.BlockSpec((R, P * HEAD_DIM), lambda kv, g, i: (i, hp(kv, g, i))), kvacc, kvacc,
                   pl.BlockSpec((P, 2 * BLOCK, BLOCK), lambda kv, g, i: (hp(kv, g, i), 0, 0)),
                   pl.BlockSpec((P, 8, BLOCK), lambda kv, g, i: (hp(kv, g, i), 0, 0))] + [any_spec] * nw,
        out_shape=[_sds((S, SWA_HEADS * HEAD_DIM), _MXU), _sds((SWA_KV_HEADS, nb, BLOCK, HEAD_DIM), F32),
                   _sds((SWA_KV_HEADS, nb, BLOCK, HEAD_DIM), F32), _sds((SWA_HEADS, 2 * BLOCK, BLOCK), F32),
                   _sds((SWA_HEADS, 8, BLOCK), F32)] + x_outs,
        scratch_shapes=x_sems,
        compiler_params=_cp(("arbitrary", "arbitrary", "arbitrary")),
    )(q, k, k, kT, kT, v, v, bias, sink, do, *x_ins)
    return res[0], res[1], res[2], res[3], res[4], list(res[5:])


def _swa_small_grads(dbias, dsink, bucket):
    rows = REL_BUCKETS + 8

    def total(x):
        return jnp.sum(jnp.sum(x, axis=1, keepdims=True), axis=0, keepdims=True)

    def body(db_ref, ds_ref, bk_ref, o_ref):
        bk = bk_ref[...]
        r = lax.broadcasted_iota(jnp.int32, (rows, BLOCK), 0)
        c = lax.broadcasted_iota(jnp.int32, (rows, BLOCK), 1)
        out = jnp.zeros((rows, BLOCK), F32)
        for h in range(SWA_HEADS):
            db = db_ref[h]
            for b in range(REL_BUCKETS):
                s = total(jnp.where(bk == b, db, 0.0))
                out = jnp.where(jnp.logical_and(r == b, c == h), s, out)
            s = jnp.sum(ds_ref[h][0:1, :], axis=1, keepdims=True)
            out = jnp.where(jnp.logical_and(r == REL_BUCKETS, c == h), s, out)
        o_ref[...] = out

    vm = pl.BlockSpec(memory_space=pltpu.VMEM)
    return pl.pallas_call(body, name="swa_small_grads", in_specs=[vm, vm, vm], out_specs=vm,
                          out_shape=_sds((rows, BLOCK), F32))(dbias, dsink, bucket)


def _tile_rows(n):
    for t in (512, 352, 256, 176, 128, 64, 32, 16, 8):
        if n % t == 0:
            return t
    return n


def _cast_rows(x, dtype, name):
    R, C = x.shape
    tr = _tile_rows(R)

    def body(x_ref, o_ref):
        o_ref[...] = x_ref[...].astype(o_ref.dtype)

    return pl.pallas_call(body, name=name, grid=(R // tr,), in_specs=[_rows(tr, C)], out_specs=_rows(tr, C),
                          out_shape=_sds((R, C), dtype), compiler_params=_cp(("parallel",)))(x)


def _pair_sum(g, recv, c, name):
    n, half, C = recv.shape
    tr = _tile_rows(half)

    def body(c_ref, a_ref, b_ref, o_ref):
        o_ref[...] = (a_ref[0] + b_ref[...]).astype(o_ref.dtype)

    return pl.pallas_call(
        body, name=name,
        grid_spec=pltpu.PrefetchScalarGridSpec(
            num_scalar_prefetch=1, grid=(n, half // tr),
            in_specs=[pl.BlockSpec((1, 1, tr, C), lambda j, i, c_ref: (j, c_ref[0], i, 0)),
                      pl.BlockSpec((1, tr, C), lambda j, i, c_ref: (j, i, 0))],
            out_specs=pl.BlockSpec((1, tr, C), lambda j, i, c_ref: (j, i, 0))),
        out_shape=_sds((n, half, C), _MXU),
        compiler_params=_cp(("parallel", "parallel")))(c.reshape(1), g.reshape(n, 2, half, C), recv)


def _chip_sum(own, recv, me, name):
    n, R, C = recv.shape
    tr = _tile_rows(R)

    def body(me_ref, own_ref, recv_ref, o_ref):
        acc = None
        for j in range(n):
            term = jnp.where(me_ref[0] == j, own_ref[0], recv_ref[j]).astype(F32)
            acc = term if acc is None else acc + term
        o_ref[...] = acc

    return pl.pallas_call(
        body, name=name,
        grid_spec=pltpu.PrefetchScalarGridSpec(
            num_scalar_prefetch=1, grid=(R // tr,),
            in_specs=[pl.BlockSpec((1, tr, C), lambda i, me_ref: (me_ref[0], i, 0)),
                      pl.BlockSpec((n, tr, C), lambda i, me_ref: (0, i, 0))],
            out_specs=pl.BlockSpec((tr, C), lambda i, me_ref: (i, 0))),
        out_shape=_sds((R, C), F32), compiler_params=_cp(("parallel",)))(me.reshape(1), own, recv)


def _adamw_math(w, g, m, v):
    m = ADAM_B1 * m + (1.0 - ADAM_B1) * g
    v = ADAM_B2 * v + (1.0 - ADAM_B2) * (g * g)
    m_hat = m / (1.0 - ADAM_B1 ** ADAM_STEP)
    v_hat = v / (1.0 - ADAM_B2 ** ADAM_STEP)
    delta = -ADAM_LR * (m_hat / (jnp.sqrt(v_hat) + ADAM_EPS) + ADAM_WD * w)
    return delta, m, v


def _adamw(w, g, m, v, name):
    R, C = w.shape
    tr = _tile_rows(R)

    def body(w_ref, g_ref, m_ref, v_ref, d_ref, nm_ref, nv_ref):
        d, nm, nv = _adamw_math(w_ref[...], g_ref[...], m_ref[...], v_ref[...])
        d_ref[...] = d
        nm_ref[...] = nm
        nv_ref[...] = nv

    blk = _rows(tr, C)
    return pl.pallas_call(body, name=name, grid=(R // tr,), in_specs=[blk] * 4, out_specs=[blk] * 3,
                          out_shape=[_sds((R, C), F32)] * 3, compiler_params=_cp(("parallel",)))(w, g, m, v)


def _gather_weights(shards):
    nw = len(shards)
    ins, outs, sems = _gather_io(shards)

    def body(*refs):
        ex = _Gather(refs[:nw], refs[nw:2 * nw], *refs[2 * nw:])
        ex.start()
        ex.forward()
        ex.finish()

    any_spec = pl.BlockSpec(memory_space=pl.ANY)
    got = pl.pallas_call(body, name="gather_weights", in_specs=[any_spec] * nw, out_specs=[any_spec] * nw,
                         out_shape=outs, scratch_shapes=sems)(*ins)
    return _gather_assemble(got, shards)


def _swap_halves(grads, name):
    nw = len(grads)
    ins, outs, sems = _swap_io(grads)

    def body(*refs):
        ex = _Swap(refs[:nw], refs[nw:2 * nw], *refs[2 * nw:])
        ex.start()
        ex.finish()

    any_spec = pl.BlockSpec(memory_space=pl.ANY)
    return pl.pallas_call(body, name=name, in_specs=[any_spec] * nw, out_specs=[any_spec] * nw,
                          out_shape=outs, scratch_shapes=sems)(*ins)


def _join_halves(sums):
    nw = len(sums)

    def body(*refs):
        f_refs, out_refs = refs[:nw], refs[nw:2 * nw]
        send_sems, recv_sems = refs[2 * nw:]
        x, y, c, _ = _place()
        ws = range(nw)

        def copy(w, half_index):
            return pltpu.make_async_remote_copy(
                src_ref=f_refs[w], dst_ref=out_refs[w].at[half_index], send_sem=send_sems.at[w],
                recv_sem=recv_sems.at[w], device_id=(x, y, 1 - c), device_id_type=MESH)

        sends = [copy(w, c) for w in ws]
        for cp in sends:
            cp.start()
        for w in ws:
            copy(w, 1 - c).wait_recv()
        for cp in sends:
            cp.wait_send()

    any_spec = pl.BlockSpec(memory_space=pl.ANY)
    outs = pl.pallas_call(
        body, name="join_halves", in_specs=[any_spec] * nw, out_specs=[any_spec] * nw,
        out_shape=[_sds((2,) + f.shape, f.dtype) for f in sums],
        scratch_shapes=[pltpu.SemaphoreType.DMA((nw,)), pltpu.SemaphoreType.DMA((nw,))],
    )(*sums)
    c = lax.axis_index("c")
    return [lax.dynamic_update_slice_in_dim(o, f[None], c, axis=0).reshape(2 * f.shape[0], f.shape[1])
            for o, f in zip(outs, sums)]


def _allreduce_small(block):
    m_per, n = block.shape

    def body(x_ref, sum_ref, loss_ref, all_ref, send_sems, recv_sems, local_sem):
        x, y, c, chips = _place()
        me, sibling = (x, y, c), (x, y, 1 - c)

        def rows(px, py, pc):
            return all_ref.at[pl.ds(pl.multiple_of((4 * px + 2 * py + pc) * m_per, 8), m_per), :]

        def copy(k, blk, to, src=None):
            return pltpu.make_async_remote_copy(
                src_ref=rows(*blk) if src is None else src, dst_ref=rows(*blk), send_sem=send_sems.at[k],
                recv_sem=recv_sems.at[k], device_id=to, device_id_type=MESH)

        mine = pltpu.make_async_copy(x_ref, rows(*me), local_sem)
        mine.start()
        first = [copy(0, me, sibling, src=x_ref)]
        first += [copy(1 + j, me, (*chip, c), src=x_ref) for j, chip in enumerate(chips)]
        for cp in first:
            cp.start()
        passed = [copy(4 + j, (*chip, c), sibling) for j, chip in enumerate(chips)]
        for j, chip in enumerate(chips):
            copy(1 + j, (*chip, c), me).wait_recv()
            passed[j].start()
        copy(0, sibling, me).wait_recv()
        for j, chip in enumerate(chips):
            copy(4 + j, (*chip, 1 - c), me).wait_recv()
        for cp in first + passed:
            cp.wait_send()
        mine.wait()

        acc = all_ref[0:m_per, :]
        for d in range(1, 8):
            acc = acc + all_ref[d * m_per:(d + 1) * m_per, :]
        sum_ref[...] = acc
        tot = jnp.sum(acc[8:9, :], axis=1, keepdims=True) * (0.5 / D_MODEL)
        loss_ref[...] = jnp.broadcast_to(tot, loss_ref.shape)

    vm = pl.BlockSpec(memory_space=pltpu.VMEM)
    return pl.pallas_call(
        body, name="allreduce_small", in_specs=[vm], out_specs=[vm, vm],
        out_shape=[_sds((m_per, n), F32), _sds((8, 128), F32)],
        scratch_shapes=[pltpu.VMEM((8 * m_per, n), F32), pltpu.SemaphoreType.DMA((7,)), pltpu.SemaphoreType.DMA((7,)),
                        pltpu.SemaphoreType.DMA],
    )(block)


def _heads_rows(x, nh):
    S = x.shape[0]
    return x.reshape(S, nh, HEAD_DIM).transpose(1, 0, 2)


def _heads_cols(x, nh):
    S = x.shape[0]
    return x.reshape(S, nh, HEAD_DIM).transpose(1, 2, 0)


def _pad_row(v):
    v = v.reshape(1, -1)
    return jnp.pad(v, ((0, 0), (0, D_MODEL - v.shape[1])))


def _pack_small(ln_in_g, ln_in_b, sb_g, swa_g, sinks, rel_bias, ln1_g, ln1_b, ln2_g, ln2_b, extra):
    rows = [_pad_row(ln_in_g), _pad_row(ln_in_b), jnp.concatenate([sb_g.reshape(1, -1), swa_g.reshape(1, -1)], axis=1),
            _pad_row(jnp.concatenate([rel_bias.reshape(1, -1), sinks.reshape(1, -1)], axis=1)),
            _pad_row(ln1_g), _pad_row(ln1_b), _pad_row(ln2_g), _pad_row(ln2_b), _pad_row(extra)]
    rows.append(jnp.zeros((SMALL_ROWS - len(rows), D_MODEL), F32))
    return jnp.concatenate(rows, axis=0)


def _unpack_small(blk):
    nrb = REL_BUCKETS * SWA_HEADS
    return (blk[0], blk[1], blk[2:3, :SB_WIDTH], blk[2:3, SB_WIDTH:], blk[3:4, nrb:nrb + SWA_HEADS],
            blk[3, :nrb].reshape(REL_BUCKETS, SWA_HEADS), blk[4:5], blk[5:6], blk[6:7], blk[7:8])


def kernel(x, ln_in_g, ln_in_b, w_in, sb_norm_g, swa_norm_g, sinks, rel_bias, w_out, ln1_g, ln1_b, w_gate_up, w_down, ln2_g, ln2_b, loss_target, m_ln_in_g, m_ln_in_b, m_w_in, m_sb_norm_g, m_swa_norm_g, m_sinks, m_rel_bias, m_w_out, m_ln1_g, m_ln1_b, m_w_gate_up, m_w_down, m_ln2_g, m_ln2_b, v_ln_in_g, v_ln_in_b, v_w_in, v_sb_norm_g, v_swa_norm_g, v_sinks, v_rel_bias, v_w_out, v_ln1_g, v_ln1_b, v_w_gate_up, v_w_down, v_ln2_g, v_ln2_b):
    S = x.shape[1]
    x2 = x.reshape(S, D_MODEL)
    tgt = loss_target.reshape(S, D_MODEL)
    T = min(S, SB_TILE)
    bucket = jnp.asarray(_bucket_table().T)
    row = lambda v: v.reshape(1, -1)

    shards = [_cast_rows(w[0], _MXU, "cast_" + n) for n, w in (("w_in", w_in), ("w_out", w_out), ("w_gate_up", w_gate_up), ("w_down", w_down))]
    (w_in_sh,) = _gather_weights(shards[:1])
    w_in_f = jnp.concatenate([w_in_sh[j] for j in range(N_CHIPS)], axis=1)

    h0, h0b, q_sw, kv_sw, qT_sb, kTb_sb, vTb_sb, kb_sb, vb_sb = _ln_in_proj(x2, row(ln_in_g), row(ln_in_b), w_in_f)
    k_sw, v_sw = kv_sw[:, :SWA_KV_WIDTH], kv_sw[:, SWA_KV_WIDTH:]
    sb_out, rsave, sb_first, (w_out_sh, w_gu_sh, w_down_sh) = _sb_fwd(qT_sb, kb_sb, vTb_sb, shards[1:])
    w_out_f = w_out_sh.reshape(D_MODEL, D_MODEL)
    w_down_f = w_down_sh.reshape(D_FF, D_MODEL)

    bias = _swa_bias(rel_bias, bucket)
    sink_rows = jnp.broadcast_to(sinks.reshape(SWA_HEADS, 1, 1), (SWA_HEADS, 1, BLOCK))
    qh_sw, kh_sw, vh_sw = _heads_rows(q_sw, SWA_HEADS), _heads_rows(k_sw, SWA_KV_HEADS), _heads_rows(v_sw, SWA_KV_HEADS)
    swa_out = _swa_fwd(qh_sw, kh_sw, _heads_cols(v_sw, SWA_KV_HEADS), bias, sink_rows)

    pre1, merged, h1b = _mix_out(sb_out, swa_out, sb_norm_g, swa_norm_g, w_out_f, h0, ln1_g, ln1_b)
    act, silu, dsilu_up = _ffn_up(h1b, w_gu_sh)
    dp2, dp2b, dg2, db2, errsum = _ffn_down_loss(act, w_down_f, pre1, ln1_g, ln1_b, ln2_g, ln2_b, tgt)

    g_w_down = _matmul_tn(act, dp2b, "grad_w_down", FF_CHUNK, D_MODEL)
    dgate, dup = _ffn_down_bwd(dp2b, w_down_f, silu, dsilu_up)
    g_w_gu = _matmul_tn_pair(h1b, dgate, dup, "grad_w_gate_up")
    dp1, dp1b, dg1, db1 = _ffn_up_bwd(dgate, dup, w_gu_sh, dp2, pre1, ln1_g)
    g_w_out = _matmul_tn(merged, dp1b, "grad_w_out", D_MODEL, D_MODEL)
    doT_sb, dsw, dgsb, dgsw = _mix_bwd(dp1b, w_out_f, sb_out, swa_out, sb_norm_g, swa_norm_g)

    c = lax.axis_index("c").astype(jnp.int32)
    me = (2 * lax.axis_index("x") + lax.axis_index("y")).astype(jnp.int32)
    grads_a = [g_w_out.reshape(N_CHIPS, D_MODEL // N_CHIPS, D_MODEL), g_w_gu, g_w_down.reshape(N_CHIPS, D_FF // N_CHIPS, D_MODEL)]
    names_a = ("w_out", "w_gate_up", "w_down")
    dq_sw, dkh_sw, dvh_sw, dbias, dsink, swapped_a = _swa_bwd(qh_sw, kh_sw, _heads_cols(k_sw, SWA_KV_HEADS), vh_sw, bias,
                                                               sink_rows, _heads_rows(dsw, SWA_HEADS), grads_a)
    swa_small = _swa_small_grads(dbias, dsink, bucket)
    partials_a = [_pair_sum(g, r, c, "pair_sum_" + n) for g, r, n in zip(grads_a, swapped_a, names_a)]
    dq_sb, dk_sb, dv_sb, recv_a = _sb_bwd(qT_sb, kb_sb, kTb_sb, vb_sb,
                                             doT_sb, rsave, sb_first, partials_a)
    tok = lambda t, nh: t.reshape(nh, S, HEAD_DIM).transpose(1, 0, 2).reshape(S, nh * HEAD_DIM)
    dproj = [dq_sb, dk_sb, dv_sb, dq_sw,
             jnp.concatenate([tok(dkh_sw, SWA_KV_HEADS), tok(dvh_sw, SWA_KV_HEADS)], axis=1).astype(_MXU)]
    g_w_in = jnp.concatenate([_matmul_tn(h0b, d, "grad_w_in_%d" % k, D_MODEL, d.shape[1]) for k, d in enumerate(dproj)],
                             axis=1)

    cin = IN_COLS // N_CHIPS
    grads_b = [jnp.stack([g_w_in[:, j * cin:(j + 1) * cin] for j in range(N_CHIPS)])]
    partials_b = [_pair_sum(grads_b[0], _swap_halves(grads_b, "swap_halves_in")[0], c, "pair_sum_w_in")]
    grad_x, dg_in, db_in, recv_b = _in_proj_bwd(dproj, w_in_f, dp1, x2, row(ln_in_g), partials_b)
    names = ("w_in",) + names_a
    sums = [_chip_sum(p, r, me, "chip_sum_" + n) for p, r, n in zip(partials_b + partials_a, list(recv_b) + list(recv_a), names)]
    gs_in, gs_out, gs_gu, gs_down = _join_halves(sums)

    nrb = REL_BUCKETS * SWA_HEADS
    small = _pack_small(dg_in, db_in, dgsb, dgsw, swa_small[REL_BUCKETS, :SWA_HEADS],
                        swa_small[:REL_BUCKETS, :SWA_HEADS], dg1, db1, dg2, db2, errsum)
    g_small, loss_tile = _allreduce_small(small)
    loss = loss_tile[0, 0]

    big = []
    for name, w, g, m, v in (("adamw_w_in", w_in, gs_in, m_w_in, v_w_in), ("adamw_w_out", w_out, gs_out, m_w_out, v_w_out),
                             ("adamw_w_gate_up", w_gate_up, gs_gu, m_w_gate_up, v_w_gate_up),
                             ("adamw_w_down", w_down, gs_down, m_w_down, v_w_down)):
        d, nm, nv = _adamw(w[0], g, m[0], v[0], name)
        big.append((g[None], d[None], nm[None], nv[None]))
    zero = jnp.zeros((1,), F32)
    w_small = _pack_small(ln_in_g, ln_in_b, sb_norm_g, swa_norm_g, sinks, rel_bias, ln1_g, ln1_b, ln2_g, ln2_b, zero)
    m_small = _pack_small(m_ln_in_g, m_ln_in_b, m_sb_norm_g, m_swa_norm_g, m_sinks, m_rel_bias, m_ln1_g, m_ln1_b,
                          m_ln2_g, m_ln2_b, zero)
    v_small = _pack_small(v_ln_in_g, v_ln_in_b, v_sb_norm_g, v_swa_norm_g, v_sinks, v_rel_bias, v_ln1_g, v_ln1_b,
                          v_ln2_g, v_ln2_b, zero)
    small_out = [_unpack_small(t) for t in (g_small,) + tuple(_adamw(w_small, g_small, m_small, v_small, "adamw_small"))]

    def kind(k):
        s = small_out[k]
        return [s[0], s[1], big[0][k], s[2], s[3], s[4], s[5], big[1][k], s[6], s[7], big[2][k], big[3][k], s[8], s[9]]

    return (loss, grad_x.reshape(1, S, D_MODEL), *kind(0), *kind(1), *kind(2), *kind(3))
```

```python
import math

import numpy as np
import jax
import jax.numpy as jnp
from jax import lax
from jax.experimental import pallas as pl
from jax.experimental.pallas import tpu as pltpu

F32 = jnp.float32
_MXU = jnp.bfloat16

D_MODEL = 1024
HEAD_DIM = 64
SB_HEADS = 8
SWA_HEADS = 8
SWA_KV_HEADS = 2
SWA_GROUP = SWA_HEADS // SWA_KV_HEADS
SB_WIDTH = SB_HEADS * HEAD_DIM
SWA_WIDTH = SWA_HEADS * HEAD_DIM
SWA_KV_WIDTH = SWA_KV_HEADS * HEAD_DIM
IN_COLS = 3 * SB_WIDTH + SWA_WIDTH + 2 * SWA_KV_WIDTH
BLOCK = 128
REL_BUCKETS = 32
REL_MAX_DIST = 128
D_FF = 2816
FF_CHUNK = D_FF // 2
ALPHA = 2.0 ** 0.25
LN_EPS = 1e-5
RMS_EPS = 1e-6
SCALE = HEAD_DIM ** -0.5
SB_TILE = 256
SB_GROUP_FWD = 8
SB_GROUP_BWD = 4
SB_FORWARD_LEAD = 8
SB_DEAD = -105.0
SWA_SUB = 8

ADAM_LR = 0.001
ADAM_B1 = 0.9
ADAM_B2 = 0.999
ADAM_EPS = 1e-08
ADAM_WD = 0.01
ADAM_STEP = 10

N_CHIPS = 4
SMALL_ROWS = 16

MESH = pl.DeviceIdType.MESH


def _sds(shape, dtype):
    return jax.ShapeDtypeStruct(shape, dtype)


def _cp(sem=None, vmem_mb=48):
    kw = dict(vmem_limit_bytes=vmem_mb * 1024 * 1024)
    if sem is not None:
        kw["dimension_semantics"] = sem
    return pltpu.CompilerParams(**kw)


def _dot(a, b):
    return jnp.dot(a, b, preferred_element_type=F32)


def _dot_nt(a, b):
    return lax.dot_general(a, b, (((1,), (1,)), ((), ())), preferred_element_type=F32)


def _dot_tn(a, b):
    return lax.dot_general(a, b, (((0,), (0,)), ((), ())), preferred_element_type=F32)


def _ln_hat(x):
    mu = jnp.mean(x, axis=-1, keepdims=True)
    xc = x - mu
    var = jnp.mean(xc * xc, axis=-1, keepdims=True)
    rstd = lax.rsqrt(var + LN_EPS)
    return xc * rstd, rstd


def _ln_bwd(xhat, rstd, dy, g):
    dxh = dy * g
    m1 = jnp.mean(dxh, axis=-1, keepdims=True)
    m2 = jnp.mean(dxh * xhat, axis=-1, keepdims=True)
    return rstd * (dxh - m1 - xhat * m2)


def _colsum(x):
    return jnp.sum(x, axis=0, keepdims=True)


def _split2(x):
    hi = x.astype(_MXU)
    lo = (x - hi.astype(F32)).astype(_MXU)
    return hi, lo


def _rows(tm, n):
    return pl.BlockSpec((tm, n), lambda i: (i, 0))


def _fixed(*shape):
    nd = len(shape)
    return pl.BlockSpec(shape, lambda i: (0,) * nd)


IN_SECTIONS = (SB_WIDTH, SB_WIDTH, SB_WIDTH, SWA_WIDTH, 2 * SWA_KV_WIDTH)


def _ln_in_proj(x, g, b, w):
    S = x.shape[0]
    tm = min(S, SB_TILE)
    offs = np.cumsum((0,) + IN_SECTIONS)
    swa = (3, 4)

    def body(x_ref, g_ref, b_ref, w_ref, h_ref, hb_ref, *o_refs):
        p_refs, (qT_ref, kT_ref, vT_ref, kr_ref, vr_ref) = o_refs[:len(swa)], o_refs[len(swa):]
        xhat, _ = _ln_hat(x_ref[...])
        h = xhat * g_ref[...] + b_ref[...]
        h_ref[...] = h
        hb = h.astype(_MXU)
        hb_ref[...] = hb
        proj = _dot(hb, w_ref[...])
        for k, p_ref in zip(swa, p_refs):
            p_ref[...] = proj[:, offs[k]:offs[k + 1]].astype(p_ref.dtype)
        heads = lambda k: proj[:, offs[k]:offs[k + 1]].T.astype(_MXU).reshape(SB_HEADS, HEAD_DIM, tm)
        qT_ref[...] = heads(0)
        kT_ref[:, 0] = heads(1)
        vT_ref[:, 0] = heads(2)
        for hd in range(SB_HEADS):
            cols = slice(hd * HEAD_DIM, (hd + 1) * HEAD_DIM)
            kr_ref[hd, 0] = proj[:, offs[1]:offs[2]][:, cols].astype(_MXU)
            vr_ref[hd, 0] = proj[:, offs[2]:offs[3]][:, cols].astype(_MXU)

    blocked = pl.BlockSpec((SB_HEADS, 1, HEAD_DIM, tm), lambda i: (0, i, 0, 0))
    blocked_rows = pl.BlockSpec((SB_HEADS, 1, tm, HEAD_DIM), lambda i: (0, i, 0, 0))
    return pl.pallas_call(
        body, name="ln_in_proj", grid=(S // tm,),
        in_specs=[_rows(tm, D_MODEL), _fixed(1, D_MODEL), _fixed(1, D_MODEL), _fixed(D_MODEL, IN_COLS)],
        out_specs=[_rows(tm, D_MODEL), _rows(tm, D_MODEL)] + [_rows(tm, IN_SECTIONS[k]) for k in swa]
                  + [pl.BlockSpec((SB_HEADS, HEAD_DIM, tm), lambda i: (0, 0, i)), blocked, blocked, blocked_rows,
                     blocked_rows],
        out_shape=[_sds((S, D_MODEL), F32), _sds((S, D_MODEL), _MXU)] + [_sds((S, IN_SECTIONS[k]), _MXU) for k in swa]
                  + [_sds((SB_HEADS, HEAD_DIM, S), _MXU), _sds((SB_HEADS, S // tm, HEAD_DIM, tm), _MXU),
                     _sds((SB_HEADS, S // tm, HEAD_DIM, tm), _MXU), _sds((SB_HEADS, S // tm, tm, HEAD_DIM), _MXU),
                     _sds((SB_HEADS, S // tm, tm, HEAD_DIM), _MXU)],
        compiler_params=_cp(("parallel",)),
    )(x, g, b, w)


def _rms(x, g):
    r = lax.rsqrt(jnp.mean(x * x, axis=-1, keepdims=True) + RMS_EPS)
    return x * r * g, r


def _mix_out(sb, sw, gsb, gsw, w_out, h0, g1, b1):
    S = sb.shape[0]
    tm = min(S, 512)

    def body(sb_ref, sw_ref, gsb_ref, gsw_ref, w_ref, h0_ref, g1_ref, b1_ref, pre_ref, mg_ref, h1_ref):
        ysb, _ = _rms(sb_ref[...], gsb_ref[...])
        ysw, _ = _rms(sw_ref[...], gsw_ref[...])
        ysb = ysb.astype(_MXU)
        ysw = ysw.astype(_MXU)
        mg_ref[:, :SB_WIDTH] = ysb
        mg_ref[:, SB_WIDTH:] = ysw
        mix = _dot(ysb, w_ref[:SB_WIDTH, :]) + _dot(ysw, w_ref[SB_WIDTH:, :])
        pre1 = ALPHA * h0_ref[...] + mix
        pre_ref[...] = pre1
        xhat, _ = _ln_hat(pre1)
        h1_ref[...] = (xhat * g1_ref[...] + b1_ref[...]).astype(h1_ref.dtype)

    vec = _fixed(1, D_MODEL)
    return pl.pallas_call(
        body, name="mix_out", grid=(S // tm,),
        in_specs=[_rows(tm, SB_WIDTH), _rows(tm, SWA_WIDTH), _fixed(1, SB_WIDTH), _fixed(1, SWA_WIDTH),
                  _fixed(D_MODEL, D_MODEL), _rows(tm, D_MODEL), vec, vec],
        out_specs=[_rows(tm, D_MODEL), _rows(tm, D_MODEL), _rows(tm, D_MODEL)],
        out_shape=[_sds((S, D_MODEL), F32), _sds((S, D_MODEL), _MXU), _sds((S, D_MODEL), _MXU)],
        compiler_params=_cp(("parallel",)),
    )(sb, sw, gsb, gsw, w_out, h0, g1, b1)


def _sigmoid(x):
    return 1.0 / (1.0 + jnp.exp(-x))


def _ffn_up(h1b, wgu):
    S = h1b.shape[0]
    tm = min(S, 1024)

    def body(h_ref, wg_ref, wu_ref, a_ref, s1_ref, s2_ref):
        h1 = h_ref[...]
        gate = _dot(h1, wg_ref[0])
        up = _dot(h1, wu_ref[0])
        sg = _sigmoid(gate)
        silu = gate * sg
        a_ref[...] = (silu * up).astype(a_ref.dtype)
        s1_ref[...] = silu.astype(s1_ref.dtype)
        s2_ref[...] = (up * (sg * (1.0 + gate * (1.0 - sg)))).astype(s2_ref.dtype)

    chunk = pl.BlockSpec((tm, FF_CHUNK), lambda j, i: (i, j))
    return pl.pallas_call(
        body, name="ffn_up", grid=(2, S // tm),
        in_specs=[pl.BlockSpec((tm, D_MODEL), lambda j, i: (i, 0)),
                  pl.BlockSpec((1, D_MODEL, FF_CHUNK), lambda j, i: (j, 0, 0)),
                  pl.BlockSpec((1, D_MODEL, FF_CHUNK), lambda j, i: (j + 2, 0, 0))],
        out_specs=[chunk, chunk, chunk],
        out_shape=[_sds((S, D_FF), _MXU)] * 3,
        compiler_params=_cp(("arbitrary", "arbitrary"), vmem_mb=56),
    )(h1b, wgu, wgu)


def _ffn_down_loss(a, w_down, pre1, g1, b1, g2, b2, tgt):
    S = a.shape[0]
    tm = min(S, 512)

    def body(a_ref, w_ref, p_ref, g1_ref, b1_ref, g2_ref, b2_ref, t_ref, d_ref, db_ref, dg2_ref, db2_ref, err_ref):
        @pl.when(pl.program_id(0) == 0)
        def _():
            dg2_ref[...] = jnp.zeros_like(dg2_ref)
            db2_ref[...] = jnp.zeros_like(db2_ref)
            err_ref[...] = jnp.zeros_like(err_ref)

        xhat1, _ = _ln_hat(p_ref[...])
        h1 = xhat1 * g1_ref[...] + b1_ref[...]
        pre2 = ALPHA * h1 + _dot(a_ref[...], w_ref[...])
        xhat2, rstd2 = _ln_hat(pre2)
        err = xhat2 * g2_ref[...] + b2_ref[...] - t_ref[...]
        dh2 = err * (1.0 / D_MODEL)
        dp2 = _ln_bwd(xhat2, rstd2, dh2, g2_ref[...])
        d_ref[...] = dp2
        db_ref[...] = dp2.astype(db_ref.dtype)
        dg2_ref[...] += _colsum(dh2 * xhat2)
        db2_ref[...] += _colsum(dh2)
        err_ref[...] += _colsum(err * err)

    vec = _fixed(1, D_MODEL)
    return pl.pallas_call(
        body, name="ffn_down_loss", grid=(S // tm,),
        in_specs=[_rows(tm, D_FF), _fixed(D_FF, D_MODEL), _rows(tm, D_MODEL), vec, vec, vec, vec, _rows(tm, D_MODEL)],
        out_specs=[_rows(tm, D_MODEL), _rows(tm, D_MODEL), vec, vec, vec],
        out_shape=[_sds((S, D_MODEL), F32), _sds((S, D_MODEL), _MXU), _sds((1, D_MODEL), F32), _sds((1, D_MODEL), F32),
                   _sds((1, D_MODEL), F32)],
        compiler_params=_cp(("arbitrary",)),
    )(a, w_down, pre1, g1, b1, g2, b2, tgt)


def _ffn_down_bwd(dp2b, w_down, s1, s2):
    S = dp2b.shape[0]
    tm = min(S, 1024)

    def body(d_ref, w_ref, s1_ref, s2_ref, dg_ref, du_ref):
        da = _dot_nt(d_ref[...], w_ref[...])
        du_ref[...] = (da * s1_ref[...].astype(F32)).astype(du_ref.dtype)
        dg_ref[...] = (da * s2_ref[...].astype(F32)).astype(dg_ref.dtype)

    chunk = pl.BlockSpec((tm, FF_CHUNK), lambda j, i: (i, j))
    return pl.pallas_call(
        body, name="ffn_down_bwd", grid=(2, S // tm),
        in_specs=[pl.BlockSpec((tm, D_MODEL), lambda j, i: (i, 0)),
                  pl.BlockSpec((FF_CHUNK, D_MODEL), lambda j, i: (j, 0)), chunk, chunk],
        out_specs=[chunk, chunk],
        out_shape=[_sds((S, D_FF), _MXU), _sds((S, D_FF), _MXU)],
        compiler_params=_cp(("arbitrary", "arbitrary")),
    )(dp2b, w_down, s1, s2)


def _ffn_up_bwd(dgate, dup, wgu, dp2, pre1, g1):
    S = dgate.shape[0]
    tm = min(S, 256)

    def body(dg_ref, du_ref, w_ref, d2_ref, p_ref, g_ref, d1_ref, d1b_ref, dg1_ref, db1_ref):
        @pl.when(pl.program_id(0) == 0)
        def _():
            dg1_ref[...] = jnp.zeros_like(dg1_ref)
            db1_ref[...] = jnp.zeros_like(db1_ref)

        dh1 = ALPHA * d2_ref[...]
        for j in range(2):
            cols = slice(j * FF_CHUNK, (j + 1) * FF_CHUNK)
            dh1 += _dot_nt(dg_ref[:, cols], w_ref[j])
            dh1 += _dot_nt(du_ref[:, cols], w_ref[j + 2])
        xhat, rstd = _ln_hat(p_ref[...])
        dp1 = _ln_bwd(xhat, rstd, dh1, g_ref[...])
        d1_ref[...] = dp1
        d1b_ref[...] = dp1.astype(d1b_ref.dtype)
        dg1_ref[...] += _colsum(dh1 * xhat)
        db1_ref[...] += _colsum(dh1)

    vec = _fixed(1, D_MODEL)
    return pl.pallas_call(
        body, name="ffn_up_bwd", grid=(S // tm,),
        in_specs=[_rows(tm, D_FF), _rows(tm, D_FF), _fixed(4, D_MODEL, FF_CHUNK), _rows(tm, D_MODEL),
                  _rows(tm, D_MODEL), vec],
        out_specs=[_rows(tm, D_MODEL), _rows(tm, D_MODEL), vec, vec],
        out_shape=[_sds((S, D_MODEL), F32), _sds((S, D_MODEL), _MXU), _sds((1, D_MODEL), F32), _sds((1, D_MODEL), F32)],
        compiler_params=_cp(("arbitrary",), vmem_mb=56),
    )(dgate, dup, wgu, dp2, pre1, g1)


def _rms_bwd(x, g, dy):
    n = x.shape[-1]
    r = lax.rsqrt(jnp.mean(x * x, axis=-1, keepdims=True) + RMS_EPS)
    u = dy * g
    dx = r * u - x * (r * r * r) * (jnp.sum(u * x, axis=-1, keepdims=True) * (1.0 / n))
    return dx, _colsum(dy * x * r)


def _mix_bwd(dp1b, w_out, sb, sw, gsb, gsw):
    S = sb.shape[0]
    tm = min(S, 512)

    def body(d_ref, w_ref, sb_ref, sw_ref, gsb_ref, gsw_ref, dsb_ref, dsw_ref, dgsb_ref, dgsw_ref):
        @pl.when(pl.program_id(0) == 0)
        def _():
            dgsb_ref[...] = jnp.zeros_like(dgsb_ref)
            dgsw_ref[...] = jnp.zeros_like(dgsw_ref)

        dm = _dot_nt(d_ref[...], w_ref[...])
        dsb, dgsb = _rms_bwd(sb_ref[...], gsb_ref[...], dm[:, :SB_WIDTH])
        dsw, dgsw = _rms_bwd(sw_ref[...], gsw_ref[...], dm[:, SB_WIDTH:])
        dsb_ref[...] = dsb.T.astype(dsb_ref.dtype).reshape(dsb_ref.shape)
        dsw_ref[...] = dsw.astype(dsw_ref.dtype)
        dgsb_ref[...] += dgsb
        dgsw_ref[...] += dgsw

    return pl.pallas_call(
        body, name="mix_bwd", grid=(S // tm,),
        in_specs=[_rows(tm, D_MODEL), _fixed(D_MODEL, D_MODEL), _rows(tm, SB_WIDTH), _rows(tm, SWA_WIDTH),
                  _fixed(1, SB_WIDTH), _fixed(1, SWA_WIDTH)],
        out_specs=[pl.BlockSpec((SB_HEADS, HEAD_DIM, tm), lambda i: (0, 0, i)), _rows(tm, SWA_WIDTH),
                   _fixed(1, SB_WIDTH), _fixed(1, SWA_WIDTH)],
        out_shape=[_sds((SB_HEADS, HEAD_DIM, S), _MXU), _sds((S, SWA_WIDTH), _MXU), _sds((1, SB_WIDTH), F32),
                   _sds((1, SWA_WIDTH), F32)],
        compiler_params=_cp(("arbitrary",)),
    )(dp1b, w_out, sb, sw, gsb, gsw)


def _in_proj_bwd(dproj, w_in, dp1, x, g, parts):
    S = x.shape[0]
    tm = min(S, 512)
    nw = len(parts)
    ns = len(IN_SECTIONS)
    offs = np.cumsum((0,) + IN_SECTIONS)
    s_ins, s_outs, s_sems = _scatter_io(parts)

    def body(*refs):
        dpj_refs = refs[:ns]
        w_ref, d1_ref, x_ref, g_ref = refs[ns:ns + 4]
        rest = refs[ns + 4:]
        gx_ref, dg_ref, db_ref = rest[nw:nw + 3]
        scatter = _Scatter(rest[:nw], rest[nw + 3:2 * nw + 3], *rest[2 * nw + 3:])

        @pl.when(pl.program_id(0) == 0)
        def _():
            scatter.start()
            dg_ref[...] = jnp.zeros_like(dg_ref)
            db_ref[...] = jnp.zeros_like(db_ref)

        dh0 = ALPHA * d1_ref[...]
        for k in range(ns):
            dh0 += _dot_nt(dpj_refs[k][...], w_ref[:, offs[k]:offs[k + 1]])
        xhat, rstd = _ln_hat(x_ref[...])
        gx_ref[...] = _ln_bwd(xhat, rstd, dh0, g_ref[...])
        dg_ref[...] += _colsum(dh0 * xhat)
        db_ref[...] += _colsum(dh0)

        @pl.when(pl.program_id(0) == pl.num_programs(0) - 1)
        def _():
            scatter.finish()

    vec = _fixed(1, D_MODEL)
    any_spec = pl.BlockSpec(memory_space=pl.ANY)
    res = pl.pallas_call(
        body, name="in_proj_bwd", grid=(S // tm,),
        in_specs=[_rows(tm, n) for n in IN_SECTIONS]
                 + [_fixed(D_MODEL, IN_COLS), _rows(tm, D_MODEL), _rows(tm, D_MODEL), vec] + [any_spec] * nw,
        out_specs=[_rows(tm, D_MODEL), vec, vec] + [any_spec] * nw,
        out_shape=[_sds((S, D_MODEL), F32), _sds((1, D_MODEL), F32), _sds((1, D_MODEL), F32)] + s_outs,
        scratch_shapes=s_sems,
        compiler_params=_cp(("arbitrary",)),
    )(*dproj, w_in, dp1, x, g, *s_ins)
    return res[0], res[1], res[2], list(res[3:])


def _matmul_tn(a, b, name, tk, tn):
    T, K = a.shape
    N = b.shape[1]
    tt = min(T, 1024)

    def body(a_ref, b_ref, o_ref):
        @pl.when(pl.program_id(2) == 0)
        def _():
            o_ref[...] = jnp.zeros_like(o_ref)

        o_ref[...] += _dot_tn(a_ref[...], b_ref[...])

    return pl.pallas_call(
        body, name=name, grid=(K // tk, N // tn, T // tt),
        in_specs=[pl.BlockSpec((tt, tk), lambda k, n, t: (t, k)), pl.BlockSpec((tt, tn), lambda k, n, t: (t, n))],
        out_specs=pl.BlockSpec((tk, tn), lambda k, n, t: (k, n)),
        out_shape=_sds((K, N), F32),
        compiler_params=_cp(("parallel", "parallel", "arbitrary")),
    )(a, b)


def _place():
    x, y, c = lax.axis_index("x"), lax.axis_index("y"), lax.axis_index("c")
    chips = [(1 - x, y), (x, 1 - y), (1 - x, 1 - y)]
    return x, y, c, chips


class _Gather:
    def __init__(self, in_refs, out_refs, send_sems, recv_sems):
        self.in_refs, self.out_refs, self.send_sems, self.recv_sems = in_refs, out_refs, send_sems, recv_sems
        self.x, self.y, self.c, self.chips = _place()

    def _copy(self, w, k, chip, hc, to, src=None):
        part = self.out_refs[w].at[2 * chip[0] + chip[1], hc]
        return pltpu.make_async_remote_copy(
            src_ref=part if src is None else src, dst_ref=part, send_sem=self.send_sems.at[w, k],
            recv_sem=self.recv_sems.at[w, k], device_id=to, device_id_type=MESH)

    def _first(self):
        x, y, c = self.x, self.y, self.c
        return [self._copy(w, j, (x, y), c, (*chip, c), src=self.in_refs[w].at[c])
                for w in range(len(self.in_refs)) for j, chip in enumerate(self.chips)]

    def start(self):
        for cp in self._first():
            cp.start()

    def _passed(self):
        sibling = (self.x, self.y, 1 - self.c)
        return [self._copy(w, 3 + j, chip, self.c, sibling)
                for w in range(len(self.in_refs)) for j, chip in enumerate(self.chips)]

    def forward(self):
        me = (self.x, self.y, self.c)
        passed = self._passed()
        for w in range(len(self.in_refs)):
            for j, chip in enumerate(self.chips):
                self._copy(w, j, chip, self.c, me).wait_recv()
                passed[3 * w + j].start()

    def finish(self):
        me = (self.x, self.y, self.c)
        for w in range(len(self.in_refs)):
            for j, chip in enumerate(self.chips):
                self._copy(w, 3 + j, chip, 1 - self.c, me).wait_recv()
        for cp in self._first() + self._passed():
            cp.wait_send()


def _gather_io(shards):
    halves = [(s.shape[0] // 2, s.shape[1]) for s in shards]
    ins = [s.reshape(2, h, cols) for s, (h, cols) in zip(shards, halves)]
    outs = [_sds((N_CHIPS, 2, h, cols), s.dtype) for s, (h, cols) in zip(shards, halves)]
    sems = [pltpu.SemaphoreType.DMA((len(shards), 6)), pltpu.SemaphoreType.DMA((len(shards), 6))]
    return ins, outs, sems


def _gather_assemble(outs, shards):
    me = 2 * lax.axis_index("x") + lax.axis_index("y")
    return [lax.dynamic_update_slice_in_dim(o.reshape((N_CHIPS,) + s.shape), s[None], me, axis=0)
            for o, s in zip(outs, shards)]


class _Scatter:
    def __init__(self, p_refs, out_refs, send_sems, recv_sems):
        self.p_refs, self.out_refs, self.send_sems, self.recv_sems = p_refs, out_refs, send_sems, recv_sems
        self.x, self.y, self.c, self.chips = _place()
        self.me = 2 * self.x + self.y

    def _copy(self, w, j, chip, src_chip, dst_chip):
        return pltpu.make_async_remote_copy(
            src_ref=self.p_refs[w].at[src_chip], dst_ref=self.out_refs[w].at[dst_chip], send_sem=self.send_sems.at[w, j],
            recv_sem=self.recv_sems.at[w, j], device_id=(*chip, self.c), device_id_type=MESH)

    def _sends(self):
        return [self._copy(w, j, chip, 2 * chip[0] + chip[1], self.me)
                for w in range(len(self.p_refs)) for j, chip in enumerate(self.chips)]

    def start(self):
        for cp in self._sends():
            cp.start()

    def finish(self):
        for w in range(len(self.p_refs)):
            for j, chip in enumerate(self.chips):
                self._copy(w, j, chip, self.me, 2 * chip[0] + chip[1]).wait_recv()
        for cp in self._sends():
            cp.wait_send()


def _scatter_io(parts):
    sems = [pltpu.SemaphoreType.DMA((len(parts), 3)), pltpu.SemaphoreType.DMA((len(parts), 3))]
    return list(parts), [_sds(p.shape, p.dtype) for p in parts], sems


def _matmul_tn_sections(a, pieces, name):
    T, K = a.shape
    tt = min(T, 1024)
    widths = [p.shape[1] for p in pieces]
    wmax = max(widths)

    def body(a_ref, *refs):
        p_refs, o_ref = refs[:-1], refs[-1]
        n = pl.program_id(0)

        @pl.when(pl.program_id(1) == 0)
        def _():
            o_ref[...] = jnp.zeros_like(o_ref)

        for k, p_ref in enumerate(p_refs):
            @pl.when(n == k)
            def _(k=k, p_ref=p_ref):
                o_ref[0, :, :widths[k]] += _dot_tn(a_ref[...], p_ref[...])

    def piece_spec(k):
        return pl.BlockSpec((tt, widths[k]), lambda n, t: (jnp.where(n == k, t, 0), 0))

    return pl.pallas_call(
        body, name=name, grid=(len(pieces), T // tt),
        in_specs=[pl.BlockSpec((tt, K), lambda n, t: (t, 0))] + [piece_spec(k) for k in range(len(pieces))],
        out_specs=pl.BlockSpec((1, K, wmax), lambda n, t: (n, 0, 0)),
        out_shape=_sds((len(pieces), K, wmax), F32),
        compiler_params=_cp(("arbitrary", "arbitrary")),
    )(a, *pieces)


class _Swap:
    def __init__(self, g_refs, out_refs, send_sems, recv_sems):
        x, y, c, _ = _place()
        self.copies = []
        for w in range(len(g_refs)):
            half = out_refs[w].shape[1]
            theirs = g_refs[w].at[:, pl.ds(pl.multiple_of((1 - c) * half, 8), half), :]
            self.copies.append(pltpu.make_async_remote_copy(
                src_ref=theirs, dst_ref=out_refs[w], send_sem=send_sems.at[w], recv_sem=recv_sems.at[w],
                device_id=(x, y, 1 - c), device_id_type=MESH))

    def start(self):
        for cp in self.copies:
            cp.start()

    def finish(self):
        for cp in self.copies:
            cp.wait()


def _swap_io(grads):
    outs = [_sds((g.shape[0], g.shape[1] // 2, g.shape[2]), g.dtype) for g in grads]
    return list(grads), outs, [pltpu.SemaphoreType.DMA((len(grads),)), pltpu.SemaphoreType.DMA((len(grads),))]


def _matmul_tn_pair(a, b0, b1, name):
    T, K = a.shape
    tt = min(T, 1024)

    def body(a_ref, b0_ref, b1_ref, o_ref):
        n = pl.program_id(0)

        @pl.when(pl.program_id(1) == 0)
        def _():
            o_ref[...] = jnp.zeros_like(o_ref)

        @pl.when(n < 2)
        def _():
            o_ref[0] += _dot_tn(a_ref[...], b0_ref[...])

        @pl.when(n >= 2)
        def _():
            o_ref[0] += _dot_tn(a_ref[...], b1_ref[...])

    return pl.pallas_call(
        body, name=name, grid=(4, T // tt),
        in_specs=[pl.BlockSpec((tt, K), lambda n, t: (t, 0)),
                  pl.BlockSpec((tt, FF_CHUNK), lambda n, t: (t, jnp.minimum(n, 1))),
                  pl.BlockSpec((tt, FF_CHUNK), lambda n, t: (t, jnp.maximum(n - 2, 0)))],
        out_specs=pl.BlockSpec((1, K, FF_CHUNK), lambda n, t: (n, 0, 0)),
        out_shape=_sds((4, K, FF_CHUNK), F32),
        compiler_params=_cp(("parallel", "arbitrary")),
    )(a, b0, b1)


def _sb_logs(zt, causal):
    e = jnp.exp(-jnp.abs(zt))
    lb = jnp.minimum(zt, 0.0) - jnp.log(1.0 + e)
    l1m = lb - zt
    if causal is not None:
        l1m = jnp.where(causal, l1m, 0.0)
    return lb, l1m


def _sb_weights(lb, suf, causal):
    a = jnp.exp(lb + suf)
    if causal is not None:
        a = jnp.where(causal, a, 0.0)
    return a


def _tri_masks(t):
    r = lax.broadcasted_iota(jnp.int32, (t, t), 0)
    c = lax.broadcasted_iota(jnp.int32, (t, t), 1)
    return r, c


def _sb_fwd(qT, kb, vTb, shards):
    Hh, _, S = qT.shape
    nk, T = kb.shape[1], kb.shape[2]
    nq = S // T
    G = SB_GROUP_FWD
    nw = len(shards)
    g_ins, g_outs, g_sems = _gather_io(shards)
    forward_step = max(nq - 1 - SB_FORWARD_LEAD, 0)

    def body(qT_ref, k_ref, vT_ref, *rest):
        o_ref, rs_ref, first_ref = rest[nw:nw + 3]
        gather = _Gather(rest[:nw], rest[nw + 3:2 * nw + 3], *rest[2 * nw + 3:])
        i = pl.program_id(1)
        first_step = jnp.logical_and(pl.program_id(0) == 0, i == 0)
        last_step = jnp.logical_and(pl.program_id(0) == pl.num_programs(0) - 1, i == pl.num_programs(1) - 1)

        @pl.when(first_step)
        def _():
            gather.start()

        qts = [(qT_ref[g].astype(F32) * SCALE).astype(_MXU) for g in range(G)]
        r, c = _tri_masks(T)
        upper = (c > r).astype(_MXU)
        causal = r < c

        def blk(j, carry, mask):
            hs = range(G)
            for g in hs:
                rs_ref[g, 0, j] = jnp.broadcast_to(carry[g][0], (8, T))
            zs = [_dot(k_ref[g, j], qts[g]) for g in hs]
            lbs, l1ms = zip(*[_sb_logs(zs[g], mask) for g in hs])
            splits = [_split2(l1ms[g]) for g in hs]
            cums = [_dot(upper, splits[g][0]) + _dot(upper, splits[g][1]) for g in hs]
            avs = [_sb_weights(lbs[g], carry[g][0] + cums[g], mask).astype(_MXU) for g in hs]
            accs = [carry[g][1] + _dot(vT_ref[g, j], avs[g]) for g in hs]
            return tuple((carry[g][0] + _colsum(l1ms[g]), accs[g]) for g in hs)

        def go_on(j, carry):
            top = carry[0][0]
            for g in range(1, G):
                top = jnp.maximum(top, carry[g][0])
            return jnp.logical_and(j >= 0, jnp.max(top) >= SB_DEAD)

        init = tuple((jnp.zeros((1, T), F32), jnp.zeros((HEAD_DIM, T), F32)) for _ in range(G))
        carry = blk(i, init, causal)
        j, carry = lax.while_loop(lambda st: go_on(*st), lambda st: (st[0] - 1, blk(st[0], st[1], None)),
                                  (i - 1, carry))

        first_ref[...] = jnp.broadcast_to((j + 1).astype(F32), first_ref.shape)

        o_ref[...] = jnp.concatenate([carry[g][1] for g in range(G)], axis=0).T

        @pl.when(jnp.logical_and(pl.program_id(0) == pl.num_programs(0) - 1, i == forward_step))
        def _():
            gather.forward()

        @pl.when(last_step)
        def _():
            gather.finish()

    any_spec = pl.BlockSpec(memory_space=pl.ANY)
    res = pl.pallas_call(
        body, name="sb_fwd", grid=(Hh // G, nq),
        in_specs=[pl.BlockSpec((G, HEAD_DIM, T), lambda h, i: (h, 0, i)),
                  pl.BlockSpec((G, nk, T, HEAD_DIM), lambda h, i: (h, 0, 0, 0), pipeline_mode=pl.Buffered(1)),
                  pl.BlockSpec((G, nk, HEAD_DIM, T), lambda h, i: (h, 0, 0, 0), pipeline_mode=pl.Buffered(1))]
                 + [any_spec] * nw,
        out_specs=[pl.BlockSpec((T, G * HEAD_DIM), lambda h, i: (i, h)),
                   pl.BlockSpec((G, 1, nk, 8, T), lambda h, i: (h, i, 0, 0, 0)),
                   pl.BlockSpec((1, 1, 8, 128), lambda h, i: (h, i, 0, 0))] + [any_spec] * nw,
        out_shape=[_sds((S, Hh * HEAD_DIM), F32), _sds((Hh, nq, nk, 8, T), F32), _sds((Hh // G, nq, 8, 128), F32)]
                  + g_outs,
        scratch_shapes=g_sems,
        compiler_params=_cp(("arbitrary", "arbitrary")),
    )(qT, kb, vTb, *g_ins)
    return res[0], res[1], res[2], _gather_assemble(res[3:], shards)


def _sb_bwd(qT, kb, kTb, vb, doT, rsave, first, parts):
    Hh, _, S = qT.shape
    nk, T = kb.shape[1], kb.shape[2]
    nq = S // T
    G = SB_GROUP_BWD
    nw = len(parts)
    s_ins, s_outs, s_sems = _scatter_io(parts)

    def body(qT_ref, k_ref, kT_ref, v_ref, doT_ref, rs_ref, first_ref, *rest):
        dq_ref, dk_out_ref, dv_out_ref = rest[nw:nw + 3]
        dk_ref, dv_ref = rest[2 * nw + 3:2 * nw + 5]
        scatter = _Scatter(rest[:nw], rest[nw + 3:2 * nw + 3], *rest[2 * nw + 5:])
        i = pl.program_id(1)
        first_step = jnp.logical_and(pl.program_id(0) == 0, i == 0)
        last_step = jnp.logical_and(pl.program_id(0) == pl.num_programs(0) - 1, i == pl.num_programs(1) - 1)

        @pl.when(first_step)
        def _():
            scatter.start()

        @pl.when(i == 0)
        def _():
            dk_ref[...] = jnp.zeros_like(dk_ref)
            dv_ref[...] = jnp.zeros_like(dv_ref)

        qts = [(qT_ref[g].astype(F32) * SCALE).astype(_MXU) for g in range(G)]
        douts = [doT_ref[g] for g in range(G)]
        r, c = _tri_masks(T)
        upper = (c > r).astype(_MXU)
        lower = (c < r).astype(_MXU)
        causal = r < c

        def blk(j, carry, mask):
            hs = range(G)
            zs = [_dot(k_ref[g, j], qts[g]) for g in hs]
            das = [_dot(v_ref[g, j], douts[g]) for g in hs]
            lbs, l1ms = zip(*[_sb_logs(zs[g], mask) for g in hs])
            splits = [_split2(l1ms[g]) for g in hs]
            cums = [_dot(upper, splits[g][0]) + _dot(upper, splits[g][1]) for g in hs]
            avs = [_sb_weights(lbs[g], rs_ref[g, 0, j][0:1, :] + cums[g], mask) for g in hs]
            ets = [das[g] * avs[g] for g in hs]
            esplits = [_split2(ets[g]) for g in hs]
            ecums = [_dot(lower, esplits[g][0]) + _dot(lower, esplits[g][1]) for g in hs]
            dzs = []
            for g in hs:
                sig = jnp.exp(lbs[g])
                dz = ets[g] * (1.0 - sig) - (carry[g][0] + ecums[g]) * sig
                if mask is not None:
                    dz = jnp.where(mask, dz, 0.0)
                dzs.append(dz.astype(_MXU))
            dqs = [carry[g][1] + _dot(kT_ref[g, j], dzs[g]) for g in hs]
            for g in hs:
                dk_ref[j, g * HEAD_DIM:(g + 1) * HEAD_DIM, :] += _dot_nt(qts[g], dzs[g])
            for g in hs:
                dv_ref[j, g * HEAD_DIM:(g + 1) * HEAD_DIM, :] += _dot_nt(douts[g], avs[g].astype(_MXU))
            return tuple((carry[g][0] + _colsum(ets[g]), dqs[g]) for g in hs)

        first = jnp.clip(jnp.max(first_ref[0, 0][0:1, 0:1]).astype(jnp.int32), 0, i)
        carry = tuple((jnp.zeros((1, T), F32), jnp.zeros((HEAD_DIM, T), F32)) for _ in range(G))
        carry = lax.fori_loop(first, i, lambda s, cr: blk(s, cr, None), carry)
        carry = blk(i, carry, causal)
        dq_ref[...] = (jnp.concatenate([carry[g][1] for g in range(G)], axis=0) * SCALE).T.astype(dq_ref.dtype)

        @pl.when(i == pl.num_programs(1) - 1)
        def _():
            def flush(j, _):
                rows = pl.ds(pl.multiple_of(j * T, T), T)
                dk_out_ref[rows, :] = dk_ref[j].T.astype(dk_out_ref.dtype)
                dv_out_ref[rows, :] = dv_ref[j].T.astype(dv_out_ref.dtype)
                return 0
            lax.fori_loop(0, nk, flush, 0)

        @pl.when(last_step)
        def _():
            scatter.finish()

    colblk = pl.BlockSpec((G, HEAD_DIM, T), lambda h, i: (h, 0, i))
    once = pl.Buffered(1)
    kblk = pl.BlockSpec((G, nk, T, HEAD_DIM), lambda h, i: (h, 0, 0, 0), pipeline_mode=once)
    kTblk = pl.BlockSpec((G, nk, HEAD_DIM, T), lambda h, i: (h, 0, 0, 0), pipeline_mode=once)
    any_spec = pl.BlockSpec(memory_space=pl.ANY)
    res = pl.pallas_call(
        body, name="sb_bwd", grid=(Hh // G, nq),
        in_specs=[colblk, kblk, kTblk, kblk, colblk,
                  pl.BlockSpec((G, 1, nk, 8, T), lambda h, i: (h, i, 0, 0, 0)),
                  pl.BlockSpec((1, 1, 8, 128), lambda h, i: ((h * G) // SB_GROUP_FWD, i, 0, 0))] + [any_spec] * nw,
        out_specs=[pl.BlockSpec((T, G * HEAD_DIM), lambda h, i: (i, h)),
                   pl.BlockSpec((S, G * HEAD_DIM), lambda h, i: (0, h), pipeline_mode=once),
                   pl.BlockSpec((S, G * HEAD_DIM), lambda h, i: (0, h), pipeline_mode=once)] + [any_spec] * nw,
        out_shape=[_sds((S, Hh * HEAD_DIM), _MXU)] * 3 + s_outs,
        scratch_shapes=[pltpu.VMEM((nk, G * HEAD_DIM, T), F32), pltpu.VMEM((nk, G * HEAD_DIM, T), F32)] + s_sems,
        compiler_params=_cp(("arbitrary", "arbitrary"), vmem_mb=60),
    )(qT, kb, kTb, vb, doT, rsave, first, *s_ins)
    return res[0], res[1], res[2], list(res[3:])


def _bucket_table():
    qi = np.arange(BLOCK)[:, None]
    cj = np.arange(2 * BLOCK)[None, :]
    dist = qi + BLOCK - cj
    exact = REL_BUCKETS // 2
    d = np.maximum(dist, 0)
    d_f = np.maximum(d, 1).astype(np.float32)
    large = exact + (np.log(d_f / np.float32(exact)) / np.float32(math.log(REL_MAX_DIST / exact))
                     * np.float32(REL_BUCKETS - exact)).astype(np.int32)
    large = np.minimum(large, REL_BUCKETS - 1)
    return np.where(d < exact, d, large).astype(np.int32)


def _swa_bias(rel_bias, bucket):
    def body(rb_ref, bk_ref, o_ref):
        bk = bk_ref[...]
        for h in range(SWA_HEADS):
            t = jnp.zeros((2 * BLOCK, BLOCK), F32)
            for b in range(REL_BUCKETS):
                t = jnp.where(bk == b, rb_ref[b, h], t)
            o_ref[h] = t

    return pl.pallas_call(
        body, name="swa_bias",
        in_specs=[pl.BlockSpec(memory_space=pltpu.SMEM), pl.BlockSpec(memory_space=pltpu.VMEM)],
        out_specs=pl.BlockSpec(memory_space=pltpu.VMEM),
        out_shape=_sds((SWA_HEADS, 2 * BLOCK, BLOCK), F32),
    )(rel_bias, bucket)


def _swa_logits(q, kp, kc):
    qs = (q.astype(F32) * SCALE).astype(_MXU)
    return qs, _dot_nt(kp, qs), _dot_nt(kc, qs)


def _swa_softmax(lp, lc, bias, sink, live_prev):
    r, c = _tri_masks(BLOCK)
    in_window = r > c if live_prev is None else jnp.logical_and(r > c, live_prev)
    lp = jnp.where(in_window, lp + bias[:BLOCK, :], -jnp.inf)
    lc = jnp.where(r <= c, lc + bias[BLOCK:, :], -jnp.inf)
    m = jnp.maximum(jnp.maximum(jnp.max(lp, axis=0, keepdims=True), jnp.max(lc, axis=0, keepdims=True)), sink)
    pp = jnp.exp(lp - m)
    pc = jnp.exp(lc - m)
    ps = jnp.exp(sink - m)
    denom = _colsum(pp) + _colsum(pc) + ps
    return pp / denom, pc / denom, ps / denom


def _swa_sub(nb):
    return min(SWA_SUB, nb)


def _swa_keys(b, prev_ref, cur_ref, i):
    cur = cur_ref[0, b * BLOCK:(b + 1) * BLOCK, :]
    if b == 0:
        return prev_ref[0], cur, i > 0
    return cur_ref[0, (b - 1) * BLOCK:b * BLOCK, :], cur, None


def _swa_keys_t(b, prev_ref, cur_ref):
    cur = cur_ref[0, :, b * BLOCK:(b + 1) * BLOCK]
    return (prev_ref[0] if b == 0 else cur_ref[0, :, (b - 1) * BLOCK:b * BLOCK]), cur


SWA_PAIR = 4


def _swa_fwd(q, k, vT, bias, sink):
    S = q.shape[1]
    nb = S // BLOCK
    ns = _swa_sub(nb)
    R = ns * BLOCK
    P = SWA_PAIR

    def body(q_ref, kp_ref, kc_ref, vp_ref, vc_ref, bias_ref, sink_ref, o_ref):
        i = pl.program_id(1)
        units = [(hh, b) for hh in range(P) for b in range(ns)]
        keys = [_swa_keys(b, kp_ref, kc_ref, i) for b in range(ns)]
        vals = [_swa_keys_t(b, vp_ref, vc_ref) for b in range(ns)]
        logits = {u: _swa_logits(q_ref[u[0], u[1] * BLOCK:(u[1] + 1) * BLOCK, :], keys[u[1]][0], keys[u[1]][1])
                  for u in units}
        ws = {u: _swa_softmax(logits[u][1], logits[u][2], bias_ref[u[0]], sink_ref[u[0]][:, :1], keys[u[1]][2])
              for u in units}
        outs = {u: _dot(vals[u[1]][0], ws[u][0].astype(_MXU)) + _dot(vals[u[1]][1], ws[u][1].astype(_MXU))
                for u in units}
        for b in range(ns):
            o_ref[b * BLOCK:(b + 1) * BLOCK, :] = jnp.concatenate([outs[(hh, b)] for hh in range(P)], axis=0).T

    kvh = lambda p: (p * P) // SWA_GROUP
    prev = pl.BlockSpec((1, BLOCK, HEAD_DIM), lambda p, i: (kvh(p), jnp.maximum(i * ns - 1, 0), 0))
    cur = pl.BlockSpec((1, R, HEAD_DIM), lambda p, i: (kvh(p), i, 0))
    prev_t = pl.BlockSpec((1, HEAD_DIM, BLOCK), lambda p, i: (kvh(p), 0, jnp.maximum(i * ns - 1, 0)))
    cur_t = pl.BlockSpec((1, HEAD_DIM, R), lambda p, i: (kvh(p), 0, i))
    return pl.pallas_call(
        body, name="swa_fwd", grid=(SWA_HEADS // P, nb // ns),
        in_specs=[pl.BlockSpec((P, R, HEAD_DIM), lambda p, i: (p, i, 0)), prev, cur, prev_t, cur_t,
                  pl.BlockSpec((P, 2 * BLOCK, BLOCK), lambda p, i: (p, 0, 0)),
                  pl.BlockSpec((P, 1, BLOCK), lambda p, i: (p, 0, 0))],
        out_specs=pl.BlockSpec((R, P * HEAD_DIM), lambda p, i: (i, p)),
        out_shape=_sds((S, SWA_HEADS * HEAD_DIM), F32),
        compiler_params=_cp(("parallel", "parallel")),
    )(q, k, k, vT, vT, bias, sink)


def _swa_bwd(q, k, kT, v, bias, sink, do, grads):
    S = q.shape[1]
    nb = S // BLOCK
    ns = _swa_sub(nb)
    R = ns * BLOCK
    P = SWA_PAIR
    nw = len(grads)
    x_ins, x_outs, x_sems = _swap_io(grads)

    def body(q_ref, kp_ref, kc_ref, ktp_ref, ktc_ref, vp_ref, vc_ref, bias_ref, sink_ref, do_ref, *rest):
        dq_ref, dk_ref, dv_ref, dbias_ref, dsink_ref = rest[nw:nw + 5]
        swap = _Swap(rest[:nw], rest[nw + 5:2 * nw + 5], *rest[2 * nw + 5:])
        g = pl.program_id(1)
        i = pl.program_id(2)
        first_step = jnp.logical_and(pl.program_id(0) == 0, jnp.logical_and(g == 0, i == 0))
        last_step = jnp.logical_and(pl.program_id(0) == pl.num_programs(0) - 1,
                                    jnp.logical_and(g == pl.num_programs(1) - 1, i == pl.num_programs(2) - 1))

        @pl.when(first_step)
        def _():
            swap.start()

        @pl.when(jnp.logical_and(g == 0, i == 0))
        def _():
            dk_ref[...] = jnp.zeros_like(dk_ref)
            dv_ref[...] = jnp.zeros_like(dv_ref)

        @pl.when(i == 0)
        def _():
            dbias_ref[...] = jnp.zeros_like(dbias_ref)
            dsink_ref[...] = jnp.zeros_like(dsink_ref)

        subs = range(ns)
        units = [(hh, b) for hh in range(P) for b in subs]
        rows = [slice(b * BLOCK, (b + 1) * BLOCK) for b in subs]
        keys = [_swa_keys(b, kp_ref, kc_ref, i) for b in subs]
        keys_t = [_swa_keys_t(b, ktp_ref, ktc_ref) for b in subs]
        vals = [_swa_keys(b, vp_ref, vc_ref, i) for b in subs]
        douts = {u: do_ref[u[0], rows[u[1]], :] for u in units}
        logits = {u: _swa_logits(q_ref[u[0], rows[u[1]], :], keys[u[1]][0], keys[u[1]][1]) for u in units}
        dws = {u: (_dot_nt(vals[u[1]][0], douts[u]), _dot_nt(vals[u[1]][1], douts[u])) for u in units}
        wts, dls = {}, {}
        for hh in range(P):
            dbp = jnp.zeros((BLOCK, BLOCK), F32)
            dbc = jnp.zeros((BLOCK, BLOCK), F32)
            dsk = jnp.zeros((1, BLOCK), F32)
            for b in subs:
                u = (hh, b)
                wp, wc, ws = _swa_softmax(logits[u][1], logits[u][2], bias_ref[hh], sink_ref[hh][:, :1], keys[b][2])
                dwp, dwc = dws[u]
                delta = _colsum(wp * dwp) + _colsum(wc * dwc)
                dlp = wp * (dwp - delta)
                dlc = wc * (dwc - delta)
                dbp += dlp
                dbc += dlc
                dsk -= ws * delta
                wts[u] = (wp.astype(_MXU), wc.astype(_MXU))
                dls[u] = (dlp.astype(_MXU), dlc.astype(_MXU))
            dbias_ref[hh, :BLOCK, :] += dbp
            dbias_ref[hh, BLOCK:, :] += dbc
            dsink_ref[hh] += jnp.broadcast_to(dsk, (8, BLOCK))
        dqs = {u: (_dot(keys_t[u[1]][0], dls[u][0]) + _dot(keys_t[u[1]][1], dls[u][1])) * SCALE for u in units}
        for b in subs:
            dq_ref[rows[b], :] = jnp.concatenate([dqs[(hh, b)] for hh in range(P)], axis=0).T.astype(dq_ref.dtype)
        dk_cur = [sum(_dot(dls[(hh, b)][1], logits[(hh, b)][0]) for hh in range(P)) for b in subs]
        dv_cur = [sum(_dot(wts[(hh, b)][1], douts[(hh, b)]) for hh in range(P)) for b in subs]
        dk_prev = [sum(_dot(dls[(hh, b)][0], logits[(hh, b)][0]) for hh in range(P)) for b in subs]
        dv_prev = [sum(_dot(wts[(hh, b)][0], douts[(hh, b)]) for hh in range(P)) for b in subs]
        for b in subs:
            last = b + 1 == ns
            dk_ref[0, i * ns + b] += dk_cur[b] if last else dk_cur[b] + dk_prev[b + 1]
            dv_ref[0, i * ns + b] += dv_cur[b] if last else dv_cur[b] + dv_prev[b + 1]

        @pl.when(i > 0)
        def _():
            dk_ref[0, i * ns - 1] += dk_prev[0]
            dv_ref[0, i * ns - 1] += dv_prev[0]

        @pl.when(last_step)
        def _():
            swap.finish()

    G2 = SWA_GROUP // P
    hp = lambda kv, g, i: kv * G2 + g
    prev = pl.BlockSpec((1, BLOCK, HEAD_DIM), lambda kv, g, i: (kv, jnp.maximum(i * ns - 1, 0), 0))
    cur = pl.BlockSpec((1, R, HEAD_DIM), lambda kv, g, i: (kv, i, 0))
    prev_t = pl.BlockSpec((1, HEAD_DIM, BLOCK), lambda kv, g, i: (kv, 0, jnp.maximum(i * ns - 1, 0)))
    cur_t = pl.BlockSpec((1, HEAD_DIM, R), lambda kv, g, i: (kv, 0, i))
    qblk = pl.BlockSpec((P, R, HEAD_DIM), lambda kv, g, i: (hp(kv, g, i), i, 0))
    kvacc = pl.BlockSpec((1, nb, BLOCK, HEAD_DIM), lambda kv, g, i: (kv, 0, 0, 0))
    any_spec = pl.BlockSpec(memory_space=pl.ANY)
    res = pl.pallas_call(
        body, name="swa_bwd", grid=(SWA_KV_HEADS, G2, nb // ns),
        in_specs=[qblk, prev, cur, prev_t, cur_t, prev, cur,
                  pl.BlockSpec((P, 2 * BLOCK, BLOCK), lambda kv, g, i: (hp(kv, g, i), 0, 0)),
                  pl.BlockSpec((P, 1, BLOCK), lambda kv, g, i: (hp(kv, g, i), 0, 0)), qblk] + [any_spec] * nw,
        out_specs=[pl.BlockSpec((R, P * HEAD_DIM), lambda kv, g, i: (i, hp(kv, g, i))), kvacc, kvacc,
                   pl.BlockSpec((P, 2 * BLOCK, BLOCK), lambda kv, g, i: (hp(kv, g, i), 0, 0)),
                   pl.BlockSpec((P, 8, BLOCK), lambda kv, g, i: (hp(kv, g, i), 0, 0))] + [any_spec] * nw,
        out_shape=[_sds((S, SWA_HEADS * HEAD_DIM), _MXU), _sds((SWA_KV_HEADS, nb, BLOCK, HEAD_DIM), F32),
                   _sds((SWA_KV_HEADS, nb, BLOCK, HEAD_DIM), F32), _sds((SWA_HEADS, 2 * BLOCK, BLOCK), F32),
                   _sds((SWA_HEADS, 8, BLOCK), F32)] + x_outs,
        scratch_shapes=x_sems,
        compiler_params=_cp(("arbitrary", "arbitrary", "arbitrary")),
    )(q, k, k, kT, kT, v, v, bias, sink, do, *x_ins)
    return res[0], res[1], res[2], res[3], res[4], list(res[5:])


def _swa_small_grads(dbias, dsink, bucket):
    rows = REL_BUCKETS + 8

    def total(x):
        return jnp.sum(jnp.sum(x, axis=1, keepdims=True), axis=0, keepdims=True)

    def body(db_ref, ds_ref, bk_ref, o_ref):
        bk = bk_ref[...]
        r = lax.broadcasted_iota(jnp.int32, (rows, BLOCK), 0)
        c = lax.broadcasted_iota(jnp.int32, (rows, BLOCK), 1)
        out = jnp.zeros((rows, BLOCK), F32)
        for h in range(SWA_HEADS):
            db = db_ref[h]
            for b in range(REL_BUCKETS):
                s = total(jnp.where(bk == b, db, 0.0))
                out = jnp.where(jnp.logical_and(r == b, c == h), s, out)
            s = jnp.sum(ds_ref[h][0:1, :], axis=1, keepdims=True)
            out = jnp.where(jnp.logical_and(r == REL_BUCKETS, c == h), s, out)
        o_ref[...] = out

    vm = pl.BlockSpec(memory_space=pltpu.VMEM)
    return pl.pallas_call(body, name="swa_small_grads", in_specs=[vm, vm, vm], out_specs=vm,
                          out_shape=_sds((rows, BLOCK), F32))(dbias, dsink, bucket)


def _tile_rows(n):
    for t in (512, 352, 256, 176, 128, 64, 32, 16, 8):
        if n % t == 0:
            return t
    return n


def _cast_rows(x, dtype, name):
    R, C = x.shape
    tr = _tile_rows(R)

    def body(x_ref, o_ref):
        o_ref[...] = x_ref[...].astype(o_ref.dtype)

    return pl.pallas_call(body, name=name, grid=(R // tr,), in_specs=[_rows(tr, C)], out_specs=_rows(tr, C),
                          out_shape=_sds((R, C), dtype), compiler_params=_cp(("parallel",)))(x)


def _pair_sum(g, recv, c, name):
    n, half, C = recv.shape
    tr = _tile_rows(half)

    def body(c_ref, a_ref, b_ref, o_ref):
        o_ref[...] = (a_ref[0] + b_ref[...]).astype(o_ref.dtype)

    return pl.pallas_call(
        body, name=name,
        grid_spec=pltpu.PrefetchScalarGridSpec(
            num_scalar_prefetch=1, grid=(n, half // tr),
            in_specs=[pl.BlockSpec((1, 1, tr, C), lambda j, i, c_ref: (j, c_ref[0], i, 0)),
                      pl.BlockSpec((1, tr, C), lambda j, i, c_ref: (j, i, 0))],
            out_specs=pl.BlockSpec((1, tr, C), lambda j, i, c_ref: (j, i, 0))),
        out_shape=_sds((n, half, C), _MXU),
        compiler_params=_cp(("parallel", "parallel")))(c.reshape(1), g.reshape(n, 2, half, C), recv)


def _chip_sum(own, recv, me, name):
    n, R, C = recv.shape
    tr = _tile_rows(R)

    def body(me_ref, own_ref, recv_ref, o_ref):
        acc = None
        for j in range(n):
            term = jnp.where(me_ref[0] == j, own_ref[0], recv_ref[j]).astype(F32)
            acc = term if acc is None else acc + term
        o_ref[...] = acc

    return pl.pallas_call(
        body, name=name,
        grid_spec=pltpu.PrefetchScalarGridSpec(
            num_scalar_prefetch=1, grid=(R // tr,),
            in_specs=[pl.BlockSpec((1, tr, C), lambda i, me_ref: (me_ref[0], i, 0)),
                      pl.BlockSpec((n, tr, C), lambda i, me_ref: (0, i, 0))],
            out_specs=pl.BlockSpec((tr, C), lambda i, me_ref: (i, 0))),
        out_shape=_sds((R, C), F32), compiler_params=_cp(("parallel",)))(me.reshape(1), own, recv)


def _adamw_math(w, g, m, v):
    m = ADAM_B1 * m + (1.0 - ADAM_B1) * g
    v = ADAM_B2 * v + (1.0 - ADAM_B2) * (g * g)
    m_hat = m / (1.0 - ADAM_B1 ** ADAM_STEP)
    v_hat = v / (1.0 - ADAM_B2 ** ADAM_STEP)
    delta = -ADAM_LR * (m_hat / (jnp.sqrt(v_hat) + ADAM_EPS) + ADAM_WD * w)
    return delta, m, v


def _adamw(w, g, m, v, name):
    R, C = w.shape
    tr = _tile_rows(R)

    def body(w_ref, g_ref, m_ref, v_ref, d_ref, nm_ref, nv_ref):
        d, nm, nv = _adamw_math(w_ref[...], g_ref[...], m_ref[...], v_ref[...])
        d_ref[...] = d
        nm_ref[...] = nm
        nv_ref[...] = nv

    blk = _rows(tr, C)
    return pl.pallas_call(body, name=name, grid=(R // tr,), in_specs=[blk] * 4, out_specs=[blk] * 3,
                          out_shape=[_sds((R, C), F32)] * 3, compiler_params=_cp(("parallel",)))(w, g, m, v)


def _gather_weights(shards):
    nw = len(shards)
    ins, outs, sems = _gather_io(shards)

    def body(*refs):
        ex = _Gather(refs[:nw], refs[nw:2 * nw], *refs[2 * nw:])
        ex.start()
        ex.forward()
        ex.finish()

    any_spec = pl.BlockSpec(memory_space=pl.ANY)
    got = pl.pallas_call(body, name="gather_weights", in_specs=[any_spec] * nw, out_specs=[any_spec] * nw,
                         out_shape=outs, scratch_shapes=sems)(*ins)
    return _gather_assemble(got, shards)


def _swap_halves(grads, name):
    nw = len(grads)
    ins, outs, sems = _swap_io(grads)

    def body(*refs):
        ex = _Swap(refs[:nw], refs[nw:2 * nw], *refs[2 * nw:])
        ex.start()
        ex.finish()

    any_spec = pl.BlockSpec(memory_space=pl.ANY)
    return pl.pallas_call(body, name=name, in_specs=[any_spec] * nw, out_specs=[any_spec] * nw,
                          out_shape=outs, scratch_shapes=sems)(*ins)


def _join_halves(sums):
    nw = len(sums)

    def body(*refs):
        f_refs, out_refs = refs[:nw], refs[nw:2 * nw]
        send_sems, recv_sems = refs[2 * nw:]
        x, y, c, _ = _place()
        ws = range(nw)

        def copy(w, half_index):
            return pltpu.make_async_remote_copy(
                src_ref=f_refs[w], dst_ref=out_refs[w].at[half_index], send_sem=send_sems.at[w],
                recv_sem=recv_sems.at[w], device_id=(x, y, 1 - c), device_id_type=MESH)

        sends = [copy(w, c) for w in ws]
        for cp in sends:
            cp.start()
        for w in ws:
            copy(w, 1 - c).wait_recv()
        for cp in sends:
            cp.wait_send()

    any_spec = pl.BlockSpec(memory_space=pl.ANY)
    outs = pl.pallas_call(
        body, name="join_halves", in_specs=[any_spec] * nw, out_specs=[any_spec] * nw,
        out_shape=[_sds((2,) + f.shape, f.dtype) for f in sums],
        scratch_shapes=[pltpu.SemaphoreType.DMA((nw,)), pltpu.SemaphoreType.DMA((nw,))],
    )(*sums)
    c = lax.axis_index("c")
    return [lax.dynamic_update_slice_in_dim(o, f[None], c, axis=0).reshape(2 * f.shape[0], f.shape[1])
            for o, f in zip(outs, sums)]


def _allreduce_small(block):
    m_per, n = block.shape

    def body(x_ref, sum_ref, loss_ref, all_ref, send_sems, recv_sems, local_sem):
        x, y, c, chips = _place()
        me, sibling = (x, y, c), (x, y, 1 - c)

        def rows(px, py, pc):
            return all_ref.at[pl.ds(pl.multiple_of((4 * px + 2 * py + pc) * m_per, 8), m_per), :]

        def copy(k, blk, to, src=None):
            return pltpu.make_async_remote_copy(
                src_ref=rows(*blk) if src is None else src, dst_ref=rows(*blk), send_sem=send_sems.at[k],
                recv_sem=recv_sems.at[k], device_id=to, device_id_type=MESH)

        mine = pltpu.make_async_copy(x_ref, rows(*me), local_sem)
        mine.start()
        first = [copy(0, me, sibling, src=x_ref)]
        first += [copy(1 + j, me, (*chip, c), src=x_ref) for j, chip in enumerate(chips)]
        for cp in first:
            cp.start()
        passed = [copy(4 + j, (*chip, c), sibling) for j, chip in enumerate(chips)]
        for j, chip in enumerate(chips):
            copy(1 + j, (*chip, c), me).wait_recv()
            passed[j].start()
        copy(0, sibling, me).wait_recv()
        for j, chip in enumerate(chips):
            copy(4 + j, (*chip, 1 - c), me).wait_recv()
        for cp in first + passed:
            cp.wait_send()
        mine.wait()

        acc = all_ref[0:m_per, :]
        for d in range(1, 8):
            acc = acc + all_ref[d * m_per:(d + 1) * m_per, :]
        sum_ref[...] = acc
        tot = jnp.sum(acc[8:9, :], axis=1, keepdims=True) * (0.5 / D_MODEL)
        loss_ref[...] = jnp.broadcast_to(tot, loss_ref.shape)

    vm = pl.BlockSpec(memory_space=pltpu.VMEM)
    return pl.pallas_call(
        body, name="allreduce_small", in_specs=[vm], out_specs=[vm, vm],
        out_shape=[_sds((m_per, n), F32), _sds((8, 128), F32)],
        scratch_shapes=[pltpu.VMEM((8 * m_per, n), F32), pltpu.SemaphoreType.DMA((7,)), pltpu.SemaphoreType.DMA((7,)),
                        pltpu.SemaphoreType.DMA],
    )(block)


def _heads_rows(x, nh):
    S = x.shape[0]
    return x.reshape(S, nh, HEAD_DIM).transpose(1, 0, 2)


def _heads_cols(x, nh):
    S = x.shape[0]
    return x.reshape(S, nh, HEAD_DIM).transpose(1, 2, 0)


def _pad_row(v):
    v = v.reshape(1, -1)
    return jnp.pad(v, ((0, 0), (0, D_MODEL - v.shape[1])))


def _pack_small(ln_in_g, ln_in_b, sb_g, swa_g, sinks, rel_bias, ln1_g, ln1_b, ln2_g, ln2_b, extra):
    rows = [_pad_row(ln_in_g), _pad_row(ln_in_b), jnp.concatenate([sb_g.reshape(1, -1), swa_g.reshape(1, -1)], axis=1),
            _pad_row(jnp.concatenate([rel_bias.reshape(1, -1), sinks.reshape(1, -1)], axis=1)),
            _pad_row(ln1_g), _pad_row(ln1_b), _pad_row(ln2_g), _pad_row(ln2_b), _pad_row(extra)]
    rows.append(jnp.zeros((SMALL_ROWS - len(rows), D_MODEL), F32))
    return jnp.concatenate(rows, axis=0)


def _unpack_small(blk):
    nrb = REL_BUCKETS * SWA_HEADS
    return (blk[0], blk[1], blk[2:3, :SB_WIDTH], blk[2:3, SB_WIDTH:], blk[3:4, nrb:nrb + SWA_HEADS],
            blk[3, :nrb].reshape(REL_BUCKETS, SWA_HEADS), blk[4:5], blk[5:6], blk[6:7], blk[7:8])


def kernel(x, ln_in_g, ln_in_b, w_in, sb_norm_g, swa_norm_g, sinks, rel_bias, w_out, ln1_g, ln1_b, w_gate_up, w_down, ln2_g, ln2_b, loss_target, m_ln_in_g, m_ln_in_b, m_w_in, m_sb_norm_g, m_swa_norm_g, m_sinks, m_rel_bias, m_w_out, m_ln1_g, m_ln1_b, m_w_gate_up, m_w_down, m_ln2_g, m_ln2_b, v_ln_in_g, v_ln_in_b, v_w_in, v_sb_norm_g, v_swa_norm_g, v_sinks, v_rel_bias, v_w_out, v_ln1_g, v_ln1_b, v_w_gate_up, v_w_down, v_ln2_g, v_ln2_b):
    S = x.shape[1]
    x2 = x.reshape(S, D_MODEL)
    tgt = loss_target.reshape(S, D_MODEL)
    T = min(S, SB_TILE)
    bucket = jnp.asarray(_bucket_table().T)
    row = lambda v: v.reshape(1, -1)

    shards = [_cast_rows(w[0], _MXU, "cast_" + n) for n, w in (("w_in", w_in), ("w_out", w_out), ("w_gate_up", w_gate_up), ("w_down", w_down))]
    (w_in_sh,) = _gather_weights(shards[:1])
    w_in_f = jnp.concatenate([w_in_sh[j] for j in range(N_CHIPS)], axis=1)

    h0, h0b, q_sw, kv_sw, qT_sb, kTb_sb, vTb_sb, kb_sb, vb_sb = _ln_in_proj(x2, row(ln_in_g), row(ln_in_b), w_in_f)
    k_sw, v_sw = kv_sw[:, :SWA_KV_WIDTH], kv_sw[:, SWA_KV_WIDTH:]
    sb_out, rsave, sb_first, (w_out_sh, w_gu_sh, w_down_sh) = _sb_fwd(qT_sb, kb_sb, vTb_sb, shards[1:])
    w_out_f = w_out_sh.reshape(D_MODEL, D_MODEL)
    w_down_f = w_down_sh.reshape(D_FF, D_MODEL)

    bias = _swa_bias(rel_bias, bucket)
    sink_rows = jnp.broadcast_to(sinks.reshape(SWA_HEADS, 1, 1), (SWA_HEADS, 1, BLOCK))
    qh_sw, kh_sw, vh_sw = _heads_rows(q_sw, SWA_HEADS), _heads_rows(k_sw, SWA_KV_HEADS), _heads_rows(v_sw, SWA_KV_HEADS)
    swa_out = _swa_fwd(qh_sw, kh_sw, _heads_cols(v_sw, SWA_KV_HEADS), bias, sink_rows)

    pre1, merged, h1b = _mix_out(sb_out, swa_out, sb_norm_g, swa_norm_g, w_out_f, h0, ln1_g, ln1_b)
    act, silu, dsilu_up = _ffn_up(h1b, w_gu_sh)
    dp2, dp2b, dg2, db2, errsum = _ffn_down_loss(act, w_down_f, pre1, ln1_g, ln1_b, ln2_g, ln2_b, tgt)

    g_w_down = _matmul_tn(act, dp2b, "grad_w_down", FF_CHUNK, D_MODEL)
    dgate, dup = _ffn_down_bwd(dp2b, w_down_f, silu, dsilu_up)
    g_w_gu = _matmul_tn_pair(h1b, dgate, dup, "grad_w_gate_up")
    dp1, dp1b, dg1, db1 = _ffn_up_bwd(dgate, dup, w_gu_sh, dp2, pre1, ln1_g)
    g_w_out = _matmul_tn(merged, dp1b, "grad_w_out", D_MODEL, D_MODEL)
    doT_sb, dsw, dgsb, dgsw = _mix_bwd(dp1b, w_out_f, sb_out, swa_out, sb_norm_g, swa_norm_g)

    c = lax.axis_index("c").astype(jnp.int32)
    me = (2 * lax.axis_index("x") + lax.axis_index("y")).astype(jnp.int32)
    grads_a = [g_w_out.reshape(N_CHIPS, D_MODEL // N_CHIPS, D_MODEL), g_w_gu, g_w_down.reshape(N_CHIPS, D_FF // N_CHIPS, D_MODEL)]
    names_a = ("w_out", "w_gate_up", "w_down")
    dq_sw, dkh_sw, dvh_sw, dbias, dsink, swapped_a = _swa_bwd(qh_sw, kh_sw, _heads_cols(k_sw, SWA_KV_HEADS), vh_sw, bias,
                                                               sink_rows, _heads_rows(dsw, SWA_HEADS), grads_a)
    swa_small = _swa_small_grads(dbias, dsink, bucket)
    partials_a = [_pair_sum(g, r, c, "pair_sum_" + n) for g, r, n in zip(grads_a, swapped_a, names_a)]
    dq_sb, dk_sb, dv_sb, recv_a = _sb_bwd(qT_sb, kb_sb, kTb_sb, vb_sb,
                                             doT_sb, rsave, sb_first, partials_a)
    tok = lambda t, nh: t.reshape(nh, S, HEAD_DIM).transpose(1, 0, 2).reshape(S, nh * HEAD_DIM)
    dproj = [dq_sb, dk_sb, dv_sb, dq_sw,
             jnp.concatenate([tok(dkh_sw, SWA_KV_HEADS), tok(dvh_sw, SWA_KV_HEADS)], axis=1).astype(_MXU)]
    g_w_in_sec = _matmul_tn_sections(h0b, dproj, "grad_w_in")
    g_w_in = jnp.concatenate([g_w_in_sec[k, :, :n] for k, n in enumerate(IN_SECTIONS)], axis=1)

    cin = IN_COLS // N_CHIPS
    grads_b = [jnp.stack([g_w_in[:, j * cin:(j + 1) * cin] for j in range(N_CHIPS)])]
    partials_b = [_pair_sum(grads_b[0], _swap_halves(grads_b, "swap_halves_in")[0], c, "pair_sum_w_in")]
    grad_x, dg_in, db_in, recv_b = _in_proj_bwd(dproj, w_in_f, dp1, x2, row(ln_in_g), partials_b)
    names = ("w_in",) + names_a
    sums = [_chip_sum(p, r, me, "chip_sum_" + n) for p, r, n in zip(partials_b + partials_a, list(recv_b) + list(recv_a), names)]
    gs_in, gs_out, gs_gu, gs_down = _join_halves(sums)

    nrb = REL_BUCKETS * SWA_HEADS
    small = _pack_small(dg_in, db_in, dgsb, dgsw, swa_small[REL_BUCKETS, :SWA_HEADS],
                        swa_small[:REL_BUCKETS, :SWA_HEADS], dg1, db1, dg2, db2, errsum)
    g_small, loss_tile = _allreduce_small(small)
    loss = loss_tile[0, 0]

    big = []
    for name, w, g, m, v in (("adamw_w_in", w_in, gs_in, m_w_in, v_w_in), ("adamw_w_out", w_out, gs_out, m_w_out, v_w_out),
                             ("adamw_w_gate_up", w_gate_up, gs_gu, m_w_gate_up, v_w_gate_up),
                             ("adamw_w_down", w_down, gs_down, m_w_down, v_w_down)):
        d, nm, nv = _adamw(w[0], g, m[0], v[0], name)
        big.append((g[None], d[None], nm[None], nv[None]))
    zero = jnp.zeros((1,), F32)
    w_small = _pack_small(ln_in_g, ln_in_b, sb_norm_g, swa_norm_g, sinks, rel_bias, ln1_g, ln1_b, ln2_g, ln2_b, zero)
    m_small = _pack_small(m_ln_in_g, m_ln_in_b, m_sb_norm_g, m_swa_norm_g, m_sinks, m_rel_bias, m_ln1_g, m_ln1_b,
                          m_ln2_g, m_ln2_b, zero)
    v_small = _pack_small(v_ln_in_g, v_ln_in_b, v_sb_norm_g, v_swa_norm_g, v_sinks, v_rel_bias, v_ln1_g, v_ln1_b,
                          v_ln2_g, v_ln2_b, zero)
    small_out = [_unpack_small(t) for t in (g_small,) + tuple(_adamw(w_small, g_small, m_small, v_small, "adamw_small"))]

    def kind(k):
        s = small_out[k]
        return [s[0], s[1], big[0][k], s[2], s[3], s[4], s[5], big[1][k], s[6], s[7], big[2][k], big[3][k], s[8], s[9]]

    return (loss, grad_x.reshape(1, S, D_MODEL), *kind(0), *kind(1), *kind(2), *kind(3))
```

```python
import math

import numpy as np
import jax
import jax.numpy as jnp
from jax import lax
from jax.experimental import pallas as pl
from jax.experimental.pallas import tpu as pltpu

F32 = jnp.float32
_MXU = jnp.bfloat16

D_MODEL = 1024
HEAD_DIM = 64
SB_HEADS = 8
SWA_HEADS = 8
SWA_KV_HEADS = 2
SWA_GROUP = SWA_HEADS // SWA_KV_HEADS
SB_WIDTH = SB_HEADS * HEAD_DIM
SWA_WIDTH = SWA_HEADS * HEAD_DIM
SWA_KV_WIDTH = SWA_KV_HEADS * HEAD_DIM
IN_COLS = 3 * SB_WIDTH + SWA_WIDTH + 2 * SWA_KV_WIDTH
BLOCK = 128
REL_BUCKETS = 32
REL_MAX_DIST = 128
D_FF = 2816
FF_CHUNK = D_FF // 2
ALPHA = 2.0 ** 0.25
LN_EPS = 1e-5
RMS_EPS = 1e-6
SCALE = HEAD_DIM ** -0.5
SB_TILE = 256
SB_GROUP_FWD = 8
SB_GROUP_BWD = 4
SB_FORWARD_LEAD = 8
SB_DEAD = -105.0
SWA_SUB = 8

ADAM_LR = 0.001
ADAM_B1 = 0.9
ADAM_B2 = 0.999
ADAM_EPS = 1e-08
ADAM_WD = 0.01
ADAM_STEP = 10

N_CHIPS = 4
SMALL_ROWS = 16

MESH = pl.DeviceIdType.MESH


def _sds(shape, dtype):
    return jax.ShapeDtypeStruct(shape, dtype)


def _cp(sem=None, vmem_mb=48):
    kw = dict(vmem_limit_bytes=vmem_mb * 1024 * 1024)
    if sem is not None:
        kw["dimension_semantics"] = sem
    return pltpu.CompilerParams(**kw)


def _dot(a, b):
    return jnp.dot(a, b, preferred_element_type=F32)


def _dot_nt(a, b):
    return lax.dot_general(a, b, (((1,), (1,)), ((), ())), preferred_element_type=F32)


def _dot_tn(a, b):
    return lax.dot_general(a, b, (((0,), (0,)), ((), ())), preferred_element_type=F32)


def _ln_hat(x):
    mu = jnp.mean(x, axis=-1, keepdims=True)
    xc = x - mu
    var = jnp.mean(xc * xc, axis=-1, keepdims=True)
    rstd = lax.rsqrt(var + LN_EPS)
    return xc * rstd, rstd


def _ln_bwd(xhat, rstd, dy, g):
    dxh = dy * g
    m1 = jnp.mean(dxh, axis=-1, keepdims=True)
    m2 = jnp.mean(dxh * xhat, axis=-1, keepdims=True)
    return rstd * (dxh - m1 - xhat * m2)


def _colsum(x):
    return jnp.sum(x, axis=0, keepdims=True)


def _split2(x):
    hi = x.astype(_MXU)
    lo = (x - hi.astype(F32)).astype(_MXU)
    return hi, lo


def _rows(tm, n):
    return pl.BlockSpec((tm, n), lambda i: (i, 0))


def _fixed(*shape):
    nd = len(shape)
    return pl.BlockSpec(shape, lambda i: (0,) * nd)


IN_SECTIONS = (SB_WIDTH, SB_WIDTH, SB_WIDTH, SWA_WIDTH, 2 * SWA_KV_WIDTH)


def _ln_in_proj(x, g, b, w):
    S = x.shape[0]
    tm = min(S, SB_TILE)
    offs = np.cumsum((0,) + IN_SECTIONS)
    swa = (4,)

    def body(x_ref, g_ref, b_ref, w_ref, h_ref, hb_ref, *o_refs):
        p_refs, (qT_ref, kT_ref, vT_ref, kr_ref, vr_ref, qw_ref) = o_refs[:len(swa)], o_refs[len(swa):]
        xhat, _ = _ln_hat(x_ref[...])
        h = xhat * g_ref[...] + b_ref[...]
        h_ref[...] = h
        hb = h.astype(_MXU)
        hb_ref[...] = hb
        proj = _dot(hb, w_ref[...])
        for k, p_ref in zip(swa, p_refs):
            p_ref[...] = proj[:, offs[k]:offs[k + 1]].astype(p_ref.dtype)
        heads = lambda k: proj[:, offs[k]:offs[k + 1]].T.astype(_MXU).reshape(SB_HEADS, HEAD_DIM, tm)
        qT_ref[...] = heads(0)
        kT_ref[:, 0] = heads(1)
        vT_ref[:, 0] = heads(2)
        for hd in range(SB_HEADS):
            cols = slice(hd * HEAD_DIM, (hd + 1) * HEAD_DIM)
            kr_ref[hd, 0] = proj[:, offs[1]:offs[2]][:, cols].astype(_MXU)
            vr_ref[hd, 0] = proj[:, offs[2]:offs[3]][:, cols].astype(_MXU)
            qw_ref[hd] = proj[:, offs[3]:offs[4]][:, cols].astype(_MXU)

    blocked = pl.BlockSpec((SB_HEADS, 1, HEAD_DIM, tm), lambda i: (0, i, 0, 0))
    blocked_rows = pl.BlockSpec((SB_HEADS, 1, tm, HEAD_DIM), lambda i: (0, i, 0, 0))
    return pl.pallas_call(
        body, name="ln_in_proj", grid=(S // tm,),
        in_specs=[_rows(tm, D_MODEL), _fixed(1, D_MODEL), _fixed(1, D_MODEL), _fixed(D_MODEL, IN_COLS)],
        out_specs=[_rows(tm, D_MODEL), _rows(tm, D_MODEL)] + [_rows(tm, IN_SECTIONS[k]) for k in swa]
                  + [pl.BlockSpec((SB_HEADS, HEAD_DIM, tm), lambda i: (0, 0, i)), blocked, blocked, blocked_rows,
                     blocked_rows, pl.BlockSpec((SWA_HEADS, tm, HEAD_DIM), lambda i: (0, i, 0))],
        out_shape=[_sds((S, D_MODEL), F32), _sds((S, D_MODEL), _MXU)] + [_sds((S, IN_SECTIONS[k]), _MXU) for k in swa]
                  + [_sds((SB_HEADS, HEAD_DIM, S), _MXU), _sds((SB_HEADS, S // tm, HEAD_DIM, tm), _MXU),
                     _sds((SB_HEADS, S // tm, HEAD_DIM, tm), _MXU), _sds((SB_HEADS, S // tm, tm, HEAD_DIM), _MXU),
                     _sds((SB_HEADS, S // tm, tm, HEAD_DIM), _MXU), _sds((SWA_HEADS, S, HEAD_DIM), _MXU)],
        compiler_params=_cp(("parallel",)),
    )(x, g, b, w)


def _rms(x, g):
    r = lax.rsqrt(jnp.mean(x * x, axis=-1, keepdims=True) + RMS_EPS)
    return x * r * g, r


def _mix_out(sb, sw, gsb, gsw, w_out, h0, g1, b1):
    S = sb.shape[0]
    tm = min(S, 512)

    def body(sb_ref, sw_ref, gsb_ref, gsw_ref, w_ref, h0_ref, g1_ref, b1_ref, pre_ref, mg_ref, h1_ref):
        ysb, _ = _rms(sb_ref[...], gsb_ref[...])
        ysw, _ = _rms(sw_ref[...], gsw_ref[...])
        ysb = ysb.astype(_MXU)
        ysw = ysw.astype(_MXU)
        mg_ref[:, :SB_WIDTH] = ysb
        mg_ref[:, SB_WIDTH:] = ysw
        mix = _dot(ysb, w_ref[:SB_WIDTH, :]) + _dot(ysw, w_ref[SB_WIDTH:, :])
        pre1 = ALPHA * h0_ref[...] + mix
        pre_ref[...] = pre1
        xhat, _ = _ln_hat(pre1)
        h1_ref[...] = (xhat * g1_ref[...] + b1_ref[...]).astype(h1_ref.dtype)

    vec = _fixed(1, D_MODEL)
    return pl.pallas_call(
        body, name="mix_out", grid=(S // tm,),
        in_specs=[_rows(tm, SB_WIDTH), _rows(tm, SWA_WIDTH), _fixed(1, SB_WIDTH), _fixed(1, SWA_WIDTH),
                  _fixed(D_MODEL, D_MODEL), _rows(tm, D_MODEL), vec, vec],
        out_specs=[_rows(tm, D_MODEL), _rows(tm, D_MODEL), _rows(tm, D_MODEL)],
        out_shape=[_sds((S, D_MODEL), F32), _sds((S, D_MODEL), _MXU), _sds((S, D_MODEL), _MXU)],
        compiler_params=_cp(("parallel",)),
    )(sb, sw, gsb, gsw, w_out, h0, g1, b1)


def _sigmoid(x):
    return 1.0 / (1.0 + jnp.exp(-x))


def _ffn_up(h1b, wgu):
    S = h1b.shape[0]
    tm = min(S, 1024)

    def body(h_ref, wg_ref, wu_ref, a_ref, s1_ref, s2_ref):
        h1 = h_ref[...]
        gate = _dot(h1, wg_ref[0])
        up = _dot(h1, wu_ref[0])
        sg = _sigmoid(gate)
        silu = gate * sg
        a_ref[...] = (silu * up).astype(a_ref.dtype)
        s1_ref[...] = silu.astype(s1_ref.dtype)
        s2_ref[...] = (up * (sg * (1.0 + gate * (1.0 - sg)))).astype(s2_ref.dtype)

    chunk = pl.BlockSpec((tm, FF_CHUNK), lambda j, i: (i, j))
    return pl.pallas_call(
        body, name="ffn_up", grid=(2, S // tm),
        in_specs=[pl.BlockSpec((tm, D_MODEL), lambda j, i: (i, 0)),
                  pl.BlockSpec((1, D_MODEL, FF_CHUNK), lambda j, i: (j, 0, 0)),
                  pl.BlockSpec((1, D_MODEL, FF_CHUNK), lambda j, i: (j + 2, 0, 0))],
        out_specs=[chunk, chunk, chunk],
        out_shape=[_sds((S, D_FF), _MXU)] * 3,
        compiler_params=_cp(("arbitrary", "arbitrary"), vmem_mb=56),
    )(h1b, wgu, wgu)


def _ffn_down_loss(a, w_down, pre1, g1, b1, g2, b2, tgt):
    S = a.shape[0]
    tm = min(S, 512)

    def body(a_ref, w_ref, p_ref, g1_ref, b1_ref, g2_ref, b2_ref, t_ref, d_ref, db_ref, dg2_ref, db2_ref, err_ref):
        @pl.when(pl.program_id(0) == 0)
        def _():
            dg2_ref[...] = jnp.zeros_like(dg2_ref)
            db2_ref[...] = jnp.zeros_like(db2_ref)
            err_ref[...] = jnp.zeros_like(err_ref)

        xhat1, _ = _ln_hat(p_ref[...])
        h1 = xhat1 * g1_ref[...] + b1_ref[...]
        pre2 = ALPHA * h1 + _dot(a_ref[...], w_ref[...])
        xhat2, rstd2 = _ln_hat(pre2)
        err = xhat2 * g2_ref[...] + b2_ref[...] - t_ref[...]
        dh2 = err * (1.0 / D_MODEL)
        dp2 = _ln_bwd(xhat2, rstd2, dh2, g2_ref[...])
        d_ref[...] = dp2
        db_ref[...] = dp2.astype(db_ref.dtype)
        dg2_ref[...] += _colsum(dh2 * xhat2)
        db2_ref[...] += _colsum(dh2)
        err_ref[...] += _colsum(err * err)

    vec = _fixed(1, D_MODEL)
    return pl.pallas_call(
        body, name="ffn_down_loss", grid=(S // tm,),
        in_specs=[_rows(tm, D_FF), _fixed(D_FF, D_MODEL), _rows(tm, D_MODEL), vec, vec, vec, vec, _rows(tm, D_MODEL)],
        out_specs=[_rows(tm, D_MODEL), _rows(tm, D_MODEL), vec, vec, vec],
        out_shape=[_sds((S, D_MODEL), F32), _sds((S, D_MODEL), _MXU), _sds((1, D_MODEL), F32), _sds((1, D_MODEL), F32),
                   _sds((1, D_MODEL), F32)],
        compiler_params=_cp(("arbitrary",)),
    )(a, w_down, pre1, g1, b1, g2, b2, tgt)


def _ffn_down_bwd(dp2b, w_down, s1, s2):
    S = dp2b.shape[0]
    tm = min(S, 1024)

    def body(d_ref, w_ref, s1_ref, s2_ref, dg_ref, du_ref):
        da = _dot_nt(d_ref[...], w_ref[...])
        du_ref[...] = (da * s1_ref[...].astype(F32)).astype(du_ref.dtype)
        dg_ref[...] = (da * s2_ref[...].astype(F32)).astype(dg_ref.dtype)

    chunk = pl.BlockSpec((tm, FF_CHUNK), lambda j, i: (i, j))
    return pl.pallas_call(
        body, name="ffn_down_bwd", grid=(2, S // tm),
        in_specs=[pl.BlockSpec((tm, D_MODEL), lambda j, i: (i, 0)),
                  pl.BlockSpec((FF_CHUNK, D_MODEL), lambda j, i: (j, 0)), chunk, chunk],
        out_specs=[chunk, chunk],
        out_shape=[_sds((S, D_FF), _MXU), _sds((S, D_FF), _MXU)],
        compiler_params=_cp(("arbitrary", "arbitrary")),
    )(dp2b, w_down, s1, s2)


def _ffn_up_bwd(dgate, dup, wgu, dp2, pre1, g1):
    S = dgate.shape[0]
    tm = min(S, 256)

    def body(dg_ref, du_ref, w_ref, d2_ref, p_ref, g_ref, d1_ref, d1b_ref, dg1_ref, db1_ref):
        @pl.when(pl.program_id(0) == 0)
        def _():
            dg1_ref[...] = jnp.zeros_like(dg1_ref)
            db1_ref[...] = jnp.zeros_like(db1_ref)

        dh1 = ALPHA * d2_ref[...]
        for j in range(2):
            cols = slice(j * FF_CHUNK, (j + 1) * FF_CHUNK)
            dh1 += _dot_nt(dg_ref[:, cols], w_ref[j])
            dh1 += _dot_nt(du_ref[:, cols], w_ref[j + 2])
        xhat, rstd = _ln_hat(p_ref[...])
        dp1 = _ln_bwd(xhat, rstd, dh1, g_ref[...])
        d1_ref[...] = dp1
        d1b_ref[...] = dp1.astype(d1b_ref.dtype)
        dg1_ref[...] += _colsum(dh1 * xhat)
        db1_ref[...] += _colsum(dh1)

    vec = _fixed(1, D_MODEL)
    return pl.pallas_call(
        body, name="ffn_up_bwd", grid=(S // tm,),
        in_specs=[_rows(tm, D_FF), _rows(tm, D_FF), _fixed(4, D_MODEL, FF_CHUNK), _rows(tm, D_MODEL),
                  _rows(tm, D_MODEL), vec],
        out_specs=[_rows(tm, D_MODEL), _rows(tm, D_MODEL), vec, vec],
        out_shape=[_sds((S, D_MODEL), F32), _sds((S, D_MODEL), _MXU), _sds((1, D_MODEL), F32), _sds((1, D_MODEL), F32)],
        compiler_params=_cp(("arbitrary",), vmem_mb=56),
    )(dgate, dup, wgu, dp2, pre1, g1)


def _rms_bwd(x, g, dy):
    n = x.shape[-1]
    r = lax.rsqrt(jnp.mean(x * x, axis=-1, keepdims=True) + RMS_EPS)
    u = dy * g
    dx = r * u - x * (r * r * r) * (jnp.sum(u * x, axis=-1, keepdims=True) * (1.0 / n))
    return dx, _colsum(dy * x * r)


def _mix_bwd(dp1b, w_out, sb, sw, gsb, gsw):
    S = sb.shape[0]
    tm = min(S, 512)

    def body(d_ref, w_ref, sb_ref, sw_ref, gsb_ref, gsw_ref, dsb_ref, dsw_ref, dgsb_ref, dgsw_ref):
        @pl.when(pl.program_id(0) == 0)
        def _():
            dgsb_ref[...] = jnp.zeros_like(dgsb_ref)
            dgsw_ref[...] = jnp.zeros_like(dgsw_ref)

        dm = _dot_nt(d_ref[...], w_ref[...])
        dsb, dgsb = _rms_bwd(sb_ref[...], gsb_ref[...], dm[:, :SB_WIDTH])
        dsw, dgsw = _rms_bwd(sw_ref[...], gsw_ref[...], dm[:, SB_WIDTH:])
        dsb_ref[...] = dsb.T.astype(dsb_ref.dtype).reshape(dsb_ref.shape)
        for hd in range(SWA_HEADS):
            dsw_ref[hd] = dsw[:, hd * HEAD_DIM:(hd + 1) * HEAD_DIM].astype(dsw_ref.dtype)
        dgsb_ref[...] += dgsb
        dgsw_ref[...] += dgsw

    return pl.pallas_call(
        body, name="mix_bwd", grid=(S // tm,),
        in_specs=[_rows(tm, D_MODEL), _fixed(D_MODEL, D_MODEL), _rows(tm, SB_WIDTH), _rows(tm, SWA_WIDTH),
                  _fixed(1, SB_WIDTH), _fixed(1, SWA_WIDTH)],
        out_specs=[pl.BlockSpec((SB_HEADS, HEAD_DIM, tm), lambda i: (0, 0, i)),
                   pl.BlockSpec((SWA_HEADS, tm, HEAD_DIM), lambda i: (0, i, 0)), _fixed(1, SB_WIDTH), _fixed(1, SWA_WIDTH)],
        out_shape=[_sds((SB_HEADS, HEAD_DIM, S), _MXU), _sds((SWA_HEADS, S, HEAD_DIM), _MXU), _sds((1, SB_WIDTH), F32),
                   _sds((1, SWA_WIDTH), F32)],
        compiler_params=_cp(("arbitrary",)),
    )(dp1b, w_out, sb, sw, gsb, gsw)


def _in_proj_bwd(dproj, w_in, dp1, x, g, parts):
    S = x.shape[0]
    tm = min(S, 512)
    nw = len(parts)
    ns = len(IN_SECTIONS)
    offs = np.cumsum((0,) + IN_SECTIONS)
    s_ins, s_outs, s_sems = _scatter_io(parts)

    def body(*refs):
        dpj_refs = refs[:ns]
        w_ref, d1_ref, x_ref, g_ref = refs[ns:ns + 4]
        rest = refs[ns + 4:]
        gx_ref, dg_ref, db_ref = rest[nw:nw + 3]
        scatter = _Scatter(rest[:nw], rest[nw + 3:2 * nw + 3], *rest[2 * nw + 3:])

        @pl.when(pl.program_id(0) == 0)
        def _():
            scatter.start()
            dg_ref[...] = jnp.zeros_like(dg_ref)
            db_ref[...] = jnp.zeros_like(db_ref)

        dh0 = ALPHA * d1_ref[...]
        for k in range(ns):
            dh0 += _dot_nt(dpj_refs[k][...], w_ref[:, offs[k]:offs[k + 1]])
        xhat, rstd = _ln_hat(x_ref[...])
        gx_ref[...] = _ln_bwd(xhat, rstd, dh0, g_ref[...])
        dg_ref[...] += _colsum(dh0 * xhat)
        db_ref[...] += _colsum(dh0)

        @pl.when(pl.program_id(0) == pl.num_programs(0) - 1)
        def _():
            scatter.finish()

    vec = _fixed(1, D_MODEL)
    any_spec = pl.BlockSpec(memory_space=pl.ANY)
    res = pl.pallas_call(
        body, name="in_proj_bwd", grid=(S // tm,),
        in_specs=[_rows(tm, n) for n in IN_SECTIONS]
                 + [_fixed(D_MODEL, IN_COLS), _rows(tm, D_MODEL), _rows(tm, D_MODEL), vec] + [any_spec] * nw,
        out_specs=[_rows(tm, D_MODEL), vec, vec] + [any_spec] * nw,
        out_shape=[_sds((S, D_MODEL), F32), _sds((1, D_MODEL), F32), _sds((1, D_MODEL), F32)] + s_outs,
        scratch_shapes=s_sems,
        compiler_params=_cp(("arbitrary",)),
    )(*dproj, w_in, dp1, x, g, *s_ins)
    return res[0], res[1], res[2], list(res[3:])


def _matmul_tn(a, b, name, tk, tn):
    T, K = a.shape
    N = b.shape[1]
    tt = min(T, 1024)

    def body(a_ref, b_ref, o_ref):
        @pl.when(pl.program_id(2) == 0)
        def _():
            o_ref[...] = jnp.zeros_like(o_ref)

        o_ref[...] += _dot_tn(a_ref[...], b_ref[...])

    return pl.pallas_call(
        body, name=name, grid=(K // tk, N // tn, T // tt),
        in_specs=[pl.BlockSpec((tt, tk), lambda k, n, t: (t, k)), pl.BlockSpec((tt, tn), lambda k, n, t: (t, n))],
        out_specs=pl.BlockSpec((tk, tn), lambda k, n, t: (k, n)),
        out_shape=_sds((K, N), F32),
        compiler_params=_cp(("parallel", "parallel", "arbitrary")),
    )(a, b)


def _place():
    x, y, c = lax.axis_index("x"), lax.axis_index("y"), lax.axis_index("c")
    chips = [(1 - x, y), (x, 1 - y), (1 - x, 1 - y)]
    return x, y, c, chips


class _Gather:
    def __init__(self, in_refs, out_refs, send_sems, recv_sems):
        self.in_refs, self.out_refs, self.send_sems, self.recv_sems = in_refs, out_refs, send_sems, recv_sems
        self.x, self.y, self.c, self.chips = _place()

    def _copy(self, w, k, chip, hc, to, src=None):
        part = self.out_refs[w].at[2 * chip[0] + chip[1], hc]
        return pltpu.make_async_remote_copy(
            src_ref=part if src is None else src, dst_ref=part, send_sem=self.send_sems.at[w, k],
            recv_sem=self.recv_sems.at[w, k], device_id=to, device_id_type=MESH)

    def _first(self):
        x, y, c = self.x, self.y, self.c
        return [self._copy(w, j, (x, y), c, (*chip, c), src=self.in_refs[w].at[c])
                for w in range(len(self.in_refs)) for j, chip in enumerate(self.chips)]

    def start(self):
        for cp in self._first():
            cp.start()

    def _passed(self):
        sibling = (self.x, self.y, 1 - self.c)
        return [self._copy(w, 3 + j, chip, self.c, sibling)
                for w in range(len(self.in_refs)) for j, chip in enumerate(self.chips)]

    def forward(self):
        me = (self.x, self.y, self.c)
        passed = self._passed()
        for w in range(len(self.in_refs)):
            for j, chip in enumerate(self.chips):
                self._copy(w, j, chip, self.c, me).wait_recv()
                passed[3 * w + j].start()

    def finish(self):
        me = (self.x, self.y, self.c)
        for w in range(len(self.in_refs)):
            for j, chip in enumerate(self.chips):
                self._copy(w, 3 + j, chip, 1 - self.c, me).wait_recv()
        for cp in self._first() + self._passed():
            cp.wait_send()


def _gather_io(shards):
    halves = [(s.shape[0] // 2, s.shape[1]) for s in shards]
    ins = [s.reshape(2, h, cols) for s, (h, cols) in zip(shards, halves)]
    outs = [_sds((N_CHIPS, 2, h, cols), s.dtype) for s, (h, cols) in zip(shards, halves)]
    sems = [pltpu.SemaphoreType.DMA((len(shards), 6)), pltpu.SemaphoreType.DMA((len(shards), 6))]
    return ins, outs, sems


def _gather_assemble(outs, shards):
    me = 2 * lax.axis_index("x") + lax.axis_index("y")
    return [lax.dynamic_update_slice_in_dim(o.reshape((N_CHIPS,) + s.shape), s[None], me, axis=0)
            for o, s in zip(outs, shards)]


class _Scatter:
    def __init__(self, p_refs, out_refs, send_sems, recv_sems):
        self.p_refs, self.out_refs, self.send_sems, self.recv_sems = p_refs, out_refs, send_sems, recv_sems
        self.x, self.y, self.c, self.chips = _place()
        self.me = 2 * self.x + self.y

    def _copy(self, w, j, chip, src_chip, dst_chip):
        return pltpu.make_async_remote_copy(
            src_ref=self.p_refs[w].at[src_chip], dst_ref=self.out_refs[w].at[dst_chip], send_sem=self.send_sems.at[w, j],
            recv_sem=self.recv_sems.at[w, j], device_id=(*chip, self.c), device_id_type=MESH)

    def _sends(self):
        return [self._copy(w, j, chip, 2 * chip[0] + chip[1], self.me)
                for w in range(len(self.p_refs)) for j, chip in enumerate(self.chips)]

    def start(self):
        for cp in self._sends():
            cp.start()

    def finish(self):
        for w in range(len(self.p_refs)):
            for j, chip in enumerate(self.chips):
                self._copy(w, j, chip, self.me, 2 * chip[0] + chip[1]).wait_recv()
        for cp in self._sends():
            cp.wait_send()


def _scatter_io(parts):
    sems = [pltpu.SemaphoreType.DMA((len(parts), 3)), pltpu.SemaphoreType.DMA((len(parts), 3))]
    return list(parts), [_sds(p.shape, p.dtype) for p in parts], sems


class _Swap:
    def __init__(self, g_refs, out_refs, send_sems, recv_sems):
        x, y, c, _ = _place()
        self.copies = []
        for w in range(len(g_refs)):
            half = out_refs[w].shape[1]
            theirs = g_refs[w].at[:, pl.ds(pl.multiple_of((1 - c) * half, 8), half), :]
            self.copies.append(pltpu.make_async_remote_copy(
                src_ref=theirs, dst_ref=out_refs[w], send_sem=send_sems.at[w], recv_sem=recv_sems.at[w],
                device_id=(x, y, 1 - c), device_id_type=MESH))

    def start(self):
        for cp in self.copies:
            cp.start()

    def finish(self):
        for cp in self.copies:
            cp.wait()


def _swap_io(grads):
    outs = [_sds((g.shape[0], g.shape[1] // 2, g.shape[2]), g.dtype) for g in grads]
    return list(grads), outs, [pltpu.SemaphoreType.DMA((len(grads),)), pltpu.SemaphoreType.DMA((len(grads),))]


def _matmul_tn_pair(a, b0, b1, name):
    T, K = a.shape
    tt = min(T, 1024)

    def body(a_ref, b0_ref, b1_ref, o_ref):
        n = pl.program_id(0)

        @pl.when(pl.program_id(1) == 0)
        def _():
            o_ref[...] = jnp.zeros_like(o_ref)

        @pl.when(n < 2)
        def _():
            o_ref[0] += _dot_tn(a_ref[...], b0_ref[...])

        @pl.when(n >= 2)
        def _():
            o_ref[0] += _dot_tn(a_ref[...], b1_ref[...])

    return pl.pallas_call(
        body, name=name, grid=(4, T // tt),
        in_specs=[pl.BlockSpec((tt, K), lambda n, t: (t, 0)),
                  pl.BlockSpec((tt, FF_CHUNK), lambda n, t: (t, jnp.minimum(n, 1))),
                  pl.BlockSpec((tt, FF_CHUNK), lambda n, t: (t, jnp.maximum(n - 2, 0)))],
        out_specs=pl.BlockSpec((1, K, FF_CHUNK), lambda n, t: (n, 0, 0)),
        out_shape=_sds((4, K, FF_CHUNK), F32),
        compiler_params=_cp(("parallel", "arbitrary")),
    )(a, b0, b1)


def _sb_logs(zt, causal):
    e = jnp.exp(-jnp.abs(zt))
    lb = jnp.minimum(zt, 0.0) - jnp.log(1.0 + e)
    l1m = lb - zt
    if causal is not None:
        l1m = jnp.where(causal, l1m, 0.0)
    return lb, l1m


def _sb_weights(lb, suf, causal):
    a = jnp.exp(lb + suf)
    if causal is not None:
        a = jnp.where(causal, a, 0.0)
    return a


def _tri_masks(t):
    r = lax.broadcasted_iota(jnp.int32, (t, t), 0)
    c = lax.broadcasted_iota(jnp.int32, (t, t), 1)
    return r, c


def _sb_fwd(qT, kb, vTb, shards):
    Hh, _, S = qT.shape
    nk, T = kb.shape[1], kb.shape[2]
    nq = S // T
    G = SB_GROUP_FWD
    nw = len(shards)
    g_ins, g_outs, g_sems = _gather_io(shards)
    forward_step = max(nq - 1 - SB_FORWARD_LEAD, 0)

    def body(qT_ref, k_ref, vT_ref, *rest):
        o_ref, rs_ref, first_ref = rest[nw:nw + 3]
        gather = _Gather(rest[:nw], rest[nw + 3:2 * nw + 3], *rest[2 * nw + 3:])
        i = pl.program_id(1)
        first_step = jnp.logical_and(pl.program_id(0) == 0, i == 0)
        last_step = jnp.logical_and(pl.program_id(0) == pl.num_programs(0) - 1, i == pl.num_programs(1) - 1)

        @pl.when(first_step)
        def _():
            gather.start()

        qts = [(qT_ref[g].astype(F32) * SCALE).astype(_MXU) for g in range(G)]
        r, c = _tri_masks(T)
        upper = (c > r).astype(_MXU)
        causal = r < c

        def blk(j, carry, mask):
            hs = range(G)
            for g in hs:
                rs_ref[g, 0, j] = jnp.broadcast_to(carry[g][0], (8, T))
            zs = [_dot(k_ref[g, j], qts[g]) for g in hs]
            lbs, l1ms = zip(*[_sb_logs(zs[g], mask) for g in hs])
            splits = [_split2(l1ms[g]) for g in hs]
            cums = [_dot(upper, splits[g][0]) + _dot(upper, splits[g][1]) for g in hs]
            avs = [_sb_weights(lbs[g], carry[g][0] + cums[g], mask).astype(_MXU) for g in hs]
            accs = [carry[g][1] + _dot(vT_ref[g, j], avs[g]) for g in hs]
            return tuple((carry[g][0] + _colsum(l1ms[g]), accs[g]) for g in hs)

        def go_on(j, carry):
            top = carry[0][0]
            for g in range(1, G):
                top = jnp.maximum(top, carry[g][0])
            return jnp.logical_and(j >= 0, jnp.max(top) >= SB_DEAD)

        init = tuple((jnp.zeros((1, T), F32), jnp.zeros((HEAD_DIM, T), F32)) for _ in range(G))
        carry = blk(i, init, causal)
        j, carry = lax.while_loop(lambda st: go_on(*st), lambda st: (st[0] - 1, blk(st[0], st[1], None)),
                                  (i - 1, carry))

        first_ref[...] = jnp.broadcast_to((j + 1).astype(F32), first_ref.shape)

        o_ref[...] = jnp.concatenate([carry[g][1] for g in range(G)], axis=0).T

        @pl.when(jnp.logical_and(pl.program_id(0) == pl.num_programs(0) - 1, i == forward_step))
        def _():
            gather.forward()

        @pl.when(last_step)
        def _():
            gather.finish()

    any_spec = pl.BlockSpec(memory_space=pl.ANY)
    res = pl.pallas_call(
        body, name="sb_fwd", grid=(Hh // G, nq),
        in_specs=[pl.BlockSpec((G, HEAD_DIM, T), lambda h, i: (h, 0, i)),
                  pl.BlockSpec((G, nk, T, HEAD_DIM), lambda h, i: (h, 0, 0, 0), pipeline_mode=pl.Buffered(1)),
                  pl.BlockSpec((G, nk, HEAD_DIM, T), lambda h, i: (h, 0, 0, 0), pipeline_mode=pl.Buffered(1))]
                 + [any_spec] * nw,
        out_specs=[pl.BlockSpec((T, G * HEAD_DIM), lambda h, i: (i, h)),
                   pl.BlockSpec((G, 1, nk, 8, T), lambda h, i: (h, i, 0, 0, 0)),
                   pl.BlockSpec((1, 1, 8, 128), lambda h, i: (h, i, 0, 0))] + [any_spec] * nw,
        out_shape=[_sds((S, Hh * HEAD_DIM), F32), _sds((Hh, nq, nk, 8, T), F32), _sds((Hh // G, nq, 8, 128), F32)]
                  + g_outs,
        scratch_shapes=g_sems,
        compiler_params=_cp(("arbitrary", "arbitrary")),
    )(qT, kb, vTb, *g_ins)
    return res[0], res[1], res[2], _gather_assemble(res[3:], shards)


def _sb_bwd(qT, kb, kTb, vb, doT, rsave, first, parts):
    Hh, _, S = qT.shape
    nk, T = kb.shape[1], kb.shape[2]
    nq = S // T
    G = SB_GROUP_BWD
    nw = len(parts)
    s_ins, s_outs, s_sems = _scatter_io(parts)

    def body(qT_ref, k_ref, kT_ref, v_ref, doT_ref, rs_ref, first_ref, *rest):
        dq_ref, dk_out_ref, dv_out_ref = rest[nw:nw + 3]
        dk_ref, dv_ref = rest[2 * nw + 3:2 * nw + 5]
        scatter = _Scatter(rest[:nw], rest[nw + 3:2 * nw + 3], *rest[2 * nw + 5:])
        i = pl.program_id(1)
        first_step = jnp.logical_and(pl.program_id(0) == 0, i == 0)
        last_step = jnp.logical_and(pl.program_id(0) == pl.num_programs(0) - 1, i == pl.num_programs(1) - 1)

        @pl.when(first_step)
        def _():
            scatter.start()

        @pl.when(i == 0)
        def _():
            dk_ref[...] = jnp.zeros_like(dk_ref)
            dv_ref[...] = jnp.zeros_like(dv_ref)

        qts = [(qT_ref[g].astype(F32) * SCALE).astype(_MXU) for g in range(G)]
        douts = [doT_ref[g] for g in range(G)]
        r, c = _tri_masks(T)
        upper = (c > r).astype(_MXU)
        lower = (c < r).astype(_MXU)
        causal = r < c

        def blk(j, carry, mask):
            hs = range(G)
            zs = [_dot(k_ref[g, j], qts[g]) for g in hs]
            das = [_dot(v_ref[g, j], douts[g]) for g in hs]
            lbs, l1ms = zip(*[_sb_logs(zs[g], mask) for g in hs])
            splits = [_split2(l1ms[g]) for g in hs]
            cums = [_dot(upper, splits[g][0]) + _dot(upper, splits[g][1]) for g in hs]
            avs = [_sb_weights(lbs[g], rs_ref[g, 0, j][0:1, :] + cums[g], mask) for g in hs]
            ets = [das[g] * avs[g] for g in hs]
            esplits = [_split2(ets[g]) for g in hs]
            ecums = [_dot(lower, esplits[g][0]) + _dot(lower, esplits[g][1]) for g in hs]
            dzs = []
            for g in hs:
                sig = jnp.exp(lbs[g])
                dz = ets[g] * (1.0 - sig) - (carry[g][0] + ecums[g]) * sig
                if mask is not None:
                    dz = jnp.where(mask, dz, 0.0)
                dzs.append(dz.astype(_MXU))
            dqs = [carry[g][1] + _dot(kT_ref[g, j], dzs[g]) for g in hs]
            for g in hs:
                dk_ref[j, g * HEAD_DIM:(g + 1) * HEAD_DIM, :] += _dot_nt(qts[g], dzs[g])
            for g in hs:
                dv_ref[j, g * HEAD_DIM:(g + 1) * HEAD_DIM, :] += _dot_nt(douts[g], avs[g].astype(_MXU))
            return tuple((carry[g][0] + _colsum(ets[g]), dqs[g]) for g in hs)

        first = jnp.clip(jnp.max(first_ref[0, 0][0:1, 0:1]).astype(jnp.int32), 0, i)
        carry = tuple((jnp.zeros((1, T), F32), jnp.zeros((HEAD_DIM, T), F32)) for _ in range(G))
        carry = lax.fori_loop(first, i, lambda s, cr: blk(s, cr, None), carry)
        carry = blk(i, carry, causal)
        dq_ref[...] = (jnp.concatenate([carry[g][1] for g in range(G)], axis=0) * SCALE).T.astype(dq_ref.dtype)

        @pl.when(i == pl.num_programs(1) - 1)
        def _():
            def flush(j, _):
                rows = pl.ds(pl.multiple_of(j * T, T), T)
                dk_out_ref[rows, :] = dk_ref[j].T.astype(dk_out_ref.dtype)
                dv_out_ref[rows, :] = dv_ref[j].T.astype(dv_out_ref.dtype)
                return 0
            lax.fori_loop(0, nk, flush, 0)

        @pl.when(last_step)
        def _():
            scatter.finish()

    colblk = pl.BlockSpec((G, HEAD_DIM, T), lambda h, i: (h, 0, i))
    once = pl.Buffered(1)
    kblk = pl.BlockSpec((G, nk, T, HEAD_DIM), lambda h, i: (h, 0, 0, 0), pipeline_mode=once)
    kTblk = pl.BlockSpec((G, nk, HEAD_DIM, T), lambda h, i: (h, 0, 0, 0), pipeline_mode=once)
    any_spec = pl.BlockSpec(memory_space=pl.ANY)
    res = pl.pallas_call(
        body, name="sb_bwd", grid=(Hh // G, nq),
        in_specs=[colblk, kblk, kTblk, kblk, colblk,
                  pl.BlockSpec((G, 1, nk, 8, T), lambda h, i: (h, i, 0, 0, 0)),
                  pl.BlockSpec((1, 1, 8, 128), lambda h, i: ((h * G) // SB_GROUP_FWD, i, 0, 0))] + [any_spec] * nw,
        out_specs=[pl.BlockSpec((T, G * HEAD_DIM), lambda h, i: (i, h)),
                   pl.BlockSpec((S, G * HEAD_DIM), lambda h, i: (0, h), pipeline_mode=once),
                   pl.BlockSpec((S, G * HEAD_DIM), lambda h, i: (0, h), pipeline_mode=once)] + [any_spec] * nw,
        out_shape=[_sds((S, Hh * HEAD_DIM), _MXU)] * 3 + s_outs,
        scratch_shapes=[pltpu.VMEM((nk, G * HEAD_DIM, T), F32), pltpu.VMEM((nk, G * HEAD_DIM, T), F32)] + s_sems,
        compiler_params=_cp(("arbitrary", "arbitrary"), vmem_mb=60),
    )(qT, kb, kTb, vb, doT, rsave, first, *s_ins)
    return res[0], res[1], res[2], list(res[3:])


def _bucket_table():
    qi = np.arange(BLOCK)[:, None]
    cj = np.arange(2 * BLOCK)[None, :]
    dist = qi + BLOCK - cj
    exact = REL_BUCKETS // 2
    d = np.maximum(dist, 0)
    d_f = np.maximum(d, 1).astype(np.float32)
    large = exact + (np.log(d_f / np.float32(exact)) / np.float32(math.log(REL_MAX_DIST / exact))
                     * np.float32(REL_BUCKETS - exact)).astype(np.int32)
    large = np.minimum(large, REL_BUCKETS - 1)
    return np.where(d < exact, d, large).astype(np.int32)


def _swa_bias(rel_bias, bucket):
    def body(rb_ref, bk_ref, o_ref):
        bk = bk_ref[...]
        for h in range(SWA_HEADS):
            t = jnp.zeros((2 * BLOCK, BLOCK), F32)
            for b in range(REL_BUCKETS):
                t = jnp.where(bk == b, rb_ref[b, h], t)
            o_ref[h] = t

    return pl.pallas_call(
        body, name="swa_bias",
        in_specs=[pl.BlockSpec(memory_space=pltpu.SMEM), pl.BlockSpec(memory_space=pltpu.VMEM)],
        out_specs=pl.BlockSpec(memory_space=pltpu.VMEM),
        out_shape=_sds((SWA_HEADS, 2 * BLOCK, BLOCK), F32),
    )(rel_bias, bucket)


def _swa_logits(q, kp, kc):
    qs = (q.astype(F32) * SCALE).astype(_MXU)
    return qs, _dot_nt(kp, qs), _dot_nt(kc, qs)


def _swa_softmax(lp, lc, bias, sink, live_prev):
    r, c = _tri_masks(BLOCK)
    in_window = r > c if live_prev is None else jnp.logical_and(r > c, live_prev)
    lp = jnp.where(in_window, lp + bias[:BLOCK, :], -jnp.inf)
    lc = jnp.where(r <= c, lc + bias[BLOCK:, :], -jnp.inf)
    m = jnp.maximum(jnp.maximum(jnp.max(lp, axis=0, keepdims=True), jnp.max(lc, axis=0, keepdims=True)), sink)
    pp = jnp.exp(lp - m)
    pc = jnp.exp(lc - m)
    ps = jnp.exp(sink - m)
    denom = _colsum(pp) + _colsum(pc) + ps
    return pp / denom, pc / denom, ps / denom


def _swa_sub(nb):
    return min(SWA_SUB, nb)


def _swa_keys(b, prev_ref, cur_ref, i):
    cur = cur_ref[0, b * BLOCK:(b + 1) * BLOCK, :]
    if b == 0:
        return prev_ref[0], cur, i > 0
    return cur_ref[0, (b - 1) * BLOCK:b * BLOCK, :], cur, None


def _swa_keys_t(b, prev_ref, cur_ref):
    cur = cur_ref[0, :, b * BLOCK:(b + 1) * BLOCK]
    return (prev_ref[0] if b == 0 else cur_ref[0, :, (b - 1) * BLOCK:b * BLOCK]), cur


SWA_PAIR = 4


def _swa_fwd(q, k, vT, bias, sink):
    S = q.shape[1]
    nb = S // BLOCK
    ns = _swa_sub(nb)
    R = ns * BLOCK
    P = SWA_PAIR

    def body(q_ref, kp_ref, kc_ref, vp_ref, vc_ref, bias_ref, sink_ref, o_ref):
        i = pl.program_id(1)
        units = [(hh, b) for hh in range(P) for b in range(ns)]
        keys = [_swa_keys(b, kp_ref, kc_ref, i) for b in range(ns)]
        vals = [_swa_keys_t(b, vp_ref, vc_ref) for b in range(ns)]
        logits = {u: _swa_logits(q_ref[u[0], u[1] * BLOCK:(u[1] + 1) * BLOCK, :], keys[u[1]][0], keys[u[1]][1])
                  for u in units}
        ws = {u: _swa_softmax(logits[u][1], logits[u][2], bias_ref[u[0]], sink_ref[u[0]][:, :1], keys[u[1]][2])
              for u in units}
        outs = {u: _dot(vals[u[1]][0], ws[u][0].astype(_MXU)) + _dot(vals[u[1]][1], ws[u][1].astype(_MXU))
                for u in units}
        for b in range(ns):
            o_ref[b * BLOCK:(b + 1) * BLOCK, :] = jnp.concatenate([outs[(hh, b)] for hh in range(P)], axis=0).T

    kvh = lambda p: (p * P) // SWA_GROUP
    prev = pl.BlockSpec((1, BLOCK, HEAD_DIM), lambda p, i: (kvh(p), jnp.maximum(i * ns - 1, 0), 0))
    cur = pl.BlockSpec((1, R, HEAD_DIM), lambda p, i: (kvh(p), i, 0))
    prev_t = pl.BlockSpec((1, HEAD_DIM, BLOCK), lambda p, i: (kvh(p), 0, jnp.maximum(i * ns - 1, 0)))
    cur_t = pl.BlockSpec((1, HEAD_DIM, R), lambda p, i: (kvh(p), 0, i))
    return pl.pallas_call(
        body, name="swa_fwd", grid=(SWA_HEADS // P, nb // ns),
        in_specs=[pl.BlockSpec((P, R, HEAD_DIM), lambda p, i: (p, i, 0)), prev, cur, prev_t, cur_t,
                  pl.BlockSpec((P, 2 * BLOCK, BLOCK), lambda p, i: (p, 0, 0)),
                  pl.BlockSpec((P, 1, BLOCK), lambda p, i: (p, 0, 0))],
        out_specs=pl.BlockSpec((R, P * HEAD_DIM), lambda p, i: (i, p)),
        out_shape=_sds((S, SWA_HEADS * HEAD_DIM), F32),
        compiler_params=_cp(("parallel", "parallel")),
    )(q, k, k, vT, vT, bias, sink)


def _swa_bwd(q, k, kT, v, bias, sink, do, grads):
    S = q.shape[1]
    nb = S // BLOCK
    ns = _swa_sub(nb)
    R = ns * BLOCK
    P = SWA_PAIR
    nw = len(grads)
    x_ins, x_outs, x_sems = _swap_io(grads)

    def body(q_ref, kp_ref, kc_ref, ktp_ref, ktc_ref, vp_ref, vc_ref, bias_ref, sink_ref, do_ref, *rest):
        dq_ref, dk_ref, dv_ref, dbias_ref, dsink_ref = rest[nw:nw + 5]
        swap = _Swap(rest[:nw], rest[nw + 5:2 * nw + 5], *rest[2 * nw + 5:])
        g = pl.program_id(1)
        i = pl.program_id(2)
        first_step = jnp.logical_and(pl.program_id(0) == 0, jnp.logical_and(g == 0, i == 0))
        last_step = jnp.logical_and(pl.program_id(0) == pl.num_programs(0) - 1,
                                    jnp.logical_and(g == pl.num_programs(1) - 1, i == pl.num_programs(2) - 1))

        @pl.when(first_step)
        def _():
            swap.start()

        @pl.when(jnp.logical_and(g == 0, i == 0))
        def _():
            dk_ref[...] = jnp.zeros_like(dk_ref)
            dv_ref[...] = jnp.zeros_like(dv_ref)

        @pl.when(i == 0)
        def _():
            dbias_ref[...] = jnp.zeros_like(dbias_ref)
            dsink_ref[...] = jnp.zeros_like(dsink_ref)

        subs = range(ns)
        units = [(hh, b) for hh in range(P) for b in subs]
        rows = [slice(b * BLOCK, (b + 1) * BLOCK) for b in subs]
        keys = [_swa_keys(b, kp_ref, kc_ref, i) for b in subs]
        keys_t = [_swa_keys_t(b, ktp_ref, ktc_ref) for b in subs]
        vals = [_swa_keys(b, vp_ref, vc_ref, i) for b in subs]
        douts = {u: do_ref[u[0], rows[u[1]], :] for u in units}
        logits = {u: _swa_logits(q_ref[u[0], rows[u[1]], :], keys[u[1]][0], keys[u[1]][1]) for u in units}
        dws = {u: (_dot_nt(vals[u[1]][0], douts[u]), _dot_nt(vals[u[1]][1], douts[u])) for u in units}
        wts, dls = {}, {}
        for hh in range(P):
            dbp = jnp.zeros((BLOCK, BLOCK), F32)
            dbc = jnp.zeros((BLOCK, BLOCK), F32)
            dsk = jnp.zeros((1, BLOCK), F32)
            for b in subs:
                u = (hh, b)
                wp, wc, ws = _swa_softmax(logits[u][1], logits[u][2], bias_ref[hh], sink_ref[hh][:, :1], keys[b][2])
                dwp, dwc = dws[u]
                delta = _colsum(wp * dwp) + _colsum(wc * dwc)
                dlp = wp * (dwp - delta)
                dlc = wc * (dwc - delta)
                dbp += dlp
                dbc += dlc
                dsk -= ws * delta
                wts[u] = (wp.astype(_MXU), wc.astype(_MXU))
                dls[u] = (dlp.astype(_MXU), dlc.astype(_MXU))
            dbias_ref[hh, :BLOCK, :] += dbp
            dbias_ref[hh, BLOCK:, :] += dbc
            dsink_ref[hh] += jnp.broadcast_to(dsk, (8, BLOCK))
        dqs = {u: (_dot(keys_t[u[1]][0], dls[u][0]) + _dot(keys_t[u[1]][1], dls[u][1])) * SCALE for u in units}
        for b in subs:
            dq_ref[rows[b], :] = jnp.concatenate([dqs[(hh, b)] for hh in range(P)], axis=0).T.astype(dq_ref.dtype)
        dk_cur = [sum(_dot(dls[(hh, b)][1], logits[(hh, b)][0]) for hh in range(P)) for b in subs]
        dv_cur = [sum(_dot(wts[(hh, b)][1], douts[(hh, b)]) for hh in range(P)) for b in subs]
        dk_prev = [sum(_dot(dls[(hh, b)][0], logits[(hh, b)][0]) for hh in range(P)) for b in subs]
        dv_prev = [sum(_dot(wts[(hh, b)][0], douts[(hh, b)]) for hh in range(P)) for b in subs]
        for b in subs:
            last = b + 1 == ns
            dk_ref[0, i * ns + b] += dk_cur[b] if last else dk_cur[b] + dk_prev[b + 1]
            dv_ref[0, i * ns + b] += dv_cur[b] if last else dv_cur[b] + dv_prev[b + 1]

        @pl.when(i > 0)
        def _():
            dk_ref[0, i * ns - 1] += dk_prev[0]
            dv_ref[0, i * ns - 1] += dv_prev[0]

        @pl.when(last_step)
        def _():
            swap.finish()

    G2 = SWA_GROUP // P
    hp = lambda kv, g, i: kv * G2 + g
    prev = pl.BlockSpec((1, BLOCK, HEAD_DIM), lambda kv, g, i: (kv, jnp.maximum(i * ns - 1, 0), 0))
    cur = pl.BlockSpec((1, R, HEAD_DIM), lambda kv, g, i: (kv, i, 0))
    prev_t = pl.BlockSpec((1, HEAD_DIM, BLOCK), lambda kv, g, i: (kv, 0, jnp.maximum(i * ns - 1, 0)))
    cur_t = pl.BlockSpec((1, HEAD_DIM, R), lambda kv, g, i: (kv, 0, i))
    qblk = pl.BlockSpec((P, R, HEAD_DIM), lambda kv, g, i: (hp(kv, g, i), i, 0))
    kvacc = pl.BlockSpec((1, nb, BLOCK, HEAD_DIM), lambda kv, g, i: (kv, 0, 0, 0))
    any_spec = pl.BlockSpec(memory_space=pl.ANY)
    res = pl.pallas_call(
        body, name="swa_bwd", grid=(SWA_KV_HEADS, G2, nb // ns),
        in_specs=[qblk, prev, cur, prev_t, cur_t, prev, cur,
                  pl.BlockSpec((P, 2 * BLOCK, BLOCK), lambda kv, g, i: (hp(kv, g, i), 0, 0)),
                  pl.BlockSpec((P, 1, BLOCK), lambda kv, g, i: (hp(kv, g, i), 0, 0)), qblk] + [any_spec] * nw,
        out_specs=[pl.BlockSpec((R, P * HEAD_DIM), lambda kv, g, i: (i, hp(kv, g, i))), kvacc, kvacc,
                   pl.BlockSpec((P, 2 * BLOCK, BLOCK), lambda kv, g, i: (hp(kv, g, i), 0, 0)),
                   pl.BlockSpec((P, 8, BLOCK), lambda kv, g, i: (hp(kv, g, i), 0, 0))] + [any_spec] * nw,
        out_shape=[_sds((S, SWA_HEADS * HEAD_DIM), _MXU), _sds((SWA_KV_HEADS, nb, BLOCK, HEAD_DIM), F32),
                   _sds((SWA_KV_HEADS, nb, BLOCK, HEAD_DIM), F32), _sds((SWA_HEADS, 2 * BLOCK, BLOCK), F32),
                   _sds((SWA_HEADS, 8, BLOCK), F32)] + x_outs,
        scratch_shapes=x_sems,
        compiler_params=_cp(("arbitrary", "arbitrary", "arbitrary")),
    )(q, k, k, kT, kT, v, v, bias, sink, do, *x_ins)
    return res[0], res[1], res[2], res[3], res[4], list(res[5:])


def _swa_small_grads(dbias, dsink, bucket):
    rows = REL_BUCKETS + 8

    def total(x):
        return jnp.sum(jnp.sum(x, axis=1, keepdims=True), axis=0, keepdims=True)

    def body(db_ref, ds_ref, bk_ref, o_ref):
        bk = bk_ref[...]
        r = lax.broadcasted_iota(jnp.int32, (rows, BLOCK), 0)
        c = lax.broadcasted_iota(jnp.int32, (rows, BLOCK), 1)
        out = jnp.zeros((rows, BLOCK), F32)
        for h in range(SWA_HEADS):
            db = db_ref[h]
            for b in range(REL_BUCKETS):
                s = total(jnp.where(bk == b, db, 0.0))
                out = jnp.where(jnp.logical_and(r == b, c == h), s, out)
            s = jnp.sum(ds_ref[h][0:1, :], axis=1, keepdims=True)
            out = jnp.where(jnp.logical_and(r == REL_BUCKETS, c == h), s, out)
        o_ref[...] = out

    vm = pl.BlockSpec(memory_space=pltpu.VMEM)
    return pl.pallas_call(body, name="swa_small_grads", in_specs=[vm, vm, vm], out_specs=vm,
                          out_shape=_sds((rows, BLOCK), F32))(dbias, dsink, bucket)


def _tile_rows(n):
    for t in (512, 352, 256, 176, 128, 64, 32, 16, 8):
        if n % t == 0:
            return t
    return n


def _cast_rows(x, dtype, name):
    R, C = x.shape
    tr = _tile_rows(R)

    def body(x_ref, o_ref):
        o_ref[...] = x_ref[...].astype(o_ref.dtype)

    return pl.pallas_call(body, name=name, grid=(R // tr,), in_specs=[_rows(tr, C)], out_specs=_rows(tr, C),
                          out_shape=_sds((R, C), dtype), compiler_params=_cp(("parallel",)))(x)


def _pair_sum(g, recv, c, name):
    n, half, C = recv.shape
    tr = _tile_rows(half)

    def body(c_ref, a_ref, b_ref, o_ref):
        o_ref[...] = (a_ref[0] + b_ref[...]).astype(o_ref.dtype)

    return pl.pallas_call(
        body, name=name,
        grid_spec=pltpu.PrefetchScalarGridSpec(
            num_scalar_prefetch=1, grid=(n, half // tr),
            in_specs=[pl.BlockSpec((1, 1, tr, C), lambda j, i, c_ref: (j, c_ref[0], i, 0)),
                      pl.BlockSpec((1, tr, C), lambda j, i, c_ref: (j, i, 0))],
            out_specs=pl.BlockSpec((1, tr, C), lambda j, i, c_ref: (j, i, 0))),
        out_shape=_sds((n, half, C), _MXU),
        compiler_params=_cp(("parallel", "parallel")))(c.reshape(1), g.reshape(n, 2, half, C), recv)


def _chip_sum(own, recv, me, name):
    n, R, C = recv.shape
    tr = _tile_rows(R)

    def body(me_ref, own_ref, recv_ref, o_ref):
        acc = None
        for j in range(n):
            term = jnp.where(me_ref[0] == j, own_ref[0], recv_ref[j]).astype(F32)
            acc = term if acc is None else acc + term
        o_ref[...] = acc

    return pl.pallas_call(
        body, name=name,
        grid_spec=pltpu.PrefetchScalarGridSpec(
            num_scalar_prefetch=1, grid=(R // tr,),
            in_specs=[pl.BlockSpec((1, tr, C), lambda i, me_ref: (me_ref[0], i, 0)),
                      pl.BlockSpec((n, tr, C), lambda i, me_ref: (0, i, 0))],
            out_specs=pl.BlockSpec((tr, C), lambda i, me_ref: (i, 0))),
        out_shape=_sds((R, C), F32), compiler_params=_cp(("parallel",)))(me.reshape(1), own, recv)


def _adamw_math(w, g, m, v):
    m = ADAM_B1 * m + (1.0 - ADAM_B1) * g
    v = ADAM_B2 * v + (1.0 - ADAM_B2) * (g * g)
    m_hat = m / (1.0 - ADAM_B1 ** ADAM_STEP)
    v_hat = v / (1.0 - ADAM_B2 ** ADAM_STEP)
    delta = -ADAM_LR * (m_hat / (jnp.sqrt(v_hat) + ADAM_EPS) + ADAM_WD * w)
    return delta, m, v


def _adamw(w, g, m, v, name):
    R, C = w.shape
    tr = _tile_rows(R)

    def body(w_ref, g_ref, m_ref, v_ref, d_ref, nm_ref, nv_ref):
        d, nm, nv = _adamw_math(w_ref[...], g_ref[...], m_ref[...], v_ref[...])
        d_ref[...] = d
        nm_ref[...] = nm
        nv_ref[...] = nv

    blk = _rows(tr, C)
    return pl.pallas_call(body, name=name, grid=(R // tr,), in_specs=[blk] * 4, out_specs=[blk] * 3,
                          out_shape=[_sds((R, C), F32)] * 3, compiler_params=_cp(("parallel",)))(w, g, m, v)


def _gather_weights(shards):
    nw = len(shards)
    ins, outs, sems = _gather_io(shards)

    def body(*refs):
        ex = _Gather(refs[:nw], refs[nw:2 * nw], *refs[2 * nw:])
        ex.start()
        ex.forward()
        ex.finish()

    any_spec = pl.BlockSpec(memory_space=pl.ANY)
    got = pl.pallas_call(body, name="gather_weights", in_specs=[any_spec] * nw, out_specs=[any_spec] * nw,
                         out_shape=outs, scratch_shapes=sems)(*ins)
    return _gather_assemble(got, shards)


def _swap_halves(grads, name):
    nw = len(grads)
    ins, outs, sems = _swap_io(grads)

    def body(*refs):
        ex = _Swap(refs[:nw], refs[nw:2 * nw], *refs[2 * nw:])
        ex.start()
        ex.finish()

    any_spec = pl.BlockSpec(memory_space=pl.ANY)
    return pl.pallas_call(body, name=name, in_specs=[any_spec] * nw, out_specs=[any_spec] * nw,
                          out_shape=outs, scratch_shapes=sems)(*ins)


def _join_halves(sums):
    nw = len(sums)

    def body(*refs):
        f_refs, out_refs = refs[:nw], refs[nw:2 * nw]
        send_sems, recv_sems = refs[2 * nw:]
        x, y, c, _ = _place()
        ws = range(nw)

        def copy(w, half_index):
            return pltpu.make_async_remote_copy(
                src_ref=f_refs[w], dst_ref=out_refs[w].at[half_index], send_sem=send_sems.at[w],
                recv_sem=recv_sems.at[w], device_id=(x, y, 1 - c), device_id_type=MESH)

        sends = [copy(w, c) for w in ws]
        for cp in sends:
            cp.start()
        for w in ws:
            copy(w, 1 - c).wait_recv()
        for cp in sends:
            cp.wait_send()

    any_spec = pl.BlockSpec(memory_space=pl.ANY)
    outs = pl.pallas_call(
        body, name="join_halves", in_specs=[any_spec] * nw, out_specs=[any_spec] * nw,
        out_shape=[_sds((2,) + f.shape, f.dtype) for f in sums],
        scratch_shapes=[pltpu.SemaphoreType.DMA((nw,)), pltpu.SemaphoreType.DMA((nw,))],
    )(*sums)
    c = lax.axis_index("c")
    return [lax.dynamic_update_slice_in_dim(o, f[None], c, axis=0).reshape(2 * f.shape[0], f.shape[1])
            for o, f in zip(outs, sums)]


def _allreduce_small(block):
    m_per, n = block.shape

    def body(x_ref, sum_ref, loss_ref, all_ref, send_sems, recv_sems, local_sem):
        x, y, c, chips = _place()
        me, sibling = (x, y, c), (x, y, 1 - c)

        def rows(px, py, pc):
            return all_ref.at[pl.ds(pl.multiple_of((4 * px + 2 * py + pc) * m_per, 8), m_per), :]

        def copy(k, blk, to, src=None):
            return pltpu.make_async_remote_copy(
                src_ref=rows(*blk) if src is None else src, dst_ref=rows(*blk), send_sem=send_sems.at[k],
                recv_sem=recv_sems.at[k], device_id=to, device_id_type=MESH)

        mine = pltpu.make_async_copy(x_ref, rows(*me), local_sem)
        mine.start()
        first = [copy(0, me, sibling, src=x_ref)]
        first += [copy(1 + j, me, (*chip, c), src=x_ref) for j, chip in enumerate(chips)]
        for cp in first:
            cp.start()
        passed = [copy(4 + j, (*chip, c), sibling) for j, chip in enumerate(chips)]
        for j, chip in enumerate(chips):
            copy(1 + j, (*chip, c), me).wait_recv()
            passed[j].start()
        copy(0, sibling, me).wait_recv()
        for j, chip in enumerate(chips):
            copy(4 + j, (*chip, 1 - c), me).wait_recv()
        for cp in first + passed:
            cp.wait_send()
        mine.wait()

        acc = all_ref[0:m_per, :]
        for d in range(1, 8):
            acc = acc + all_ref[d * m_per:(d + 1) * m_per, :]
        sum_ref[...] = acc
        tot = jnp.sum(acc[8:9, :], axis=1, keepdims=True) * (0.5 / D_MODEL)
        loss_ref[...] = jnp.broadcast_to(tot, loss_ref.shape)

    vm = pl.BlockSpec(memory_space=pltpu.VMEM)
    return pl.pallas_call(
        body, name="allreduce_small", in_specs=[vm], out_specs=[vm, vm],
        out_shape=[_sds((m_per, n), F32), _sds((8, 128), F32)],
        scratch_shapes=[pltpu.VMEM((8 * m_per, n), F32), pltpu.SemaphoreType.DMA((7,)), pltpu.SemaphoreType.DMA((7,)),
                        pltpu.SemaphoreType.DMA],
    )(block)


def _heads_rows(x, nh):
    S = x.shape[0]
    return x.reshape(S, nh, HEAD_DIM).transpose(1, 0, 2)


def _heads_cols(x, nh):
    S = x.shape[0]
    return x.reshape(S, nh, HEAD_DIM).transpose(1, 2, 0)


def _pad_row(v):
    v = v.reshape(1, -1)
    return jnp.pad(v, ((0, 0), (0, D_MODEL - v.shape[1])))


def _pack_small(ln_in_g, ln_in_b, sb_g, swa_g, sinks, rel_bias, ln1_g, ln1_b, ln2_g, ln2_b, extra):
    rows = [_pad_row(ln_in_g), _pad_row(ln_in_b), jnp.concatenate([sb_g.reshape(1, -1), swa_g.reshape(1, -1)], axis=1),
            _pad_row(jnp.concatenate([rel_bias.reshape(1, -1), sinks.reshape(1, -1)], axis=1)),
            _pad_row(ln1_g), _pad_row(ln1_b), _pad_row(ln2_g), _pad_row(ln2_b), _pad_row(extra)]
    rows.append(jnp.zeros((SMALL_ROWS - len(rows), D_MODEL), F32))
    return jnp.concatenate(rows, axis=0)


def _unpack_small(blk):
    nrb = REL_BUCKETS * SWA_HEADS
    return (blk[0], blk[1], blk[2:3, :SB_WIDTH], blk[2:3, SB_WIDTH:], blk[3:4, nrb:nrb + SWA_HEADS],
            blk[3, :nrb].reshape(REL_BUCKETS, SWA_HEADS), blk[4:5], blk[5:6], blk[6:7], blk[7:8])


def kernel(x, ln_in_g, ln_in_b, w_in, sb_norm_g, swa_norm_g, sinks, rel_bias, w_out, ln1_g, ln1_b, w_gate_up, w_down, ln2_g, ln2_b, loss_target, m_ln_in_g, m_ln_in_b, m_w_in, m_sb_norm_g, m_swa_norm_g, m_sinks, m_rel_bias, m_w_out, m_ln1_g, m_ln1_b, m_w_gate_up, m_w_down, m_ln2_g, m_ln2_b, v_ln_in_g, v_ln_in_b, v_w_in, v_sb_norm_g, v_swa_norm_g, v_sinks, v_rel_bias, v_w_out, v_ln1_g, v_ln1_b, v_w_gate_up, v_w_down, v_ln2_g, v_ln2_b):
    S = x.shape[1]
    x2 = x.reshape(S, D_MODEL)
    tgt = loss_target.reshape(S, D_MODEL)
    T = min(S, SB_TILE)
    bucket = jnp.asarray(_bucket_table().T)
    row = lambda v: v.reshape(1, -1)

    shards = [_cast_rows(w[0], _MXU, "cast_" + n) for n, w in (("w_in", w_in), ("w_out", w_out), ("w_gate_up", w_gate_up), ("w_down", w_down))]
    (w_in_sh,) = _gather_weights(shards[:1])
    w_in_f = jnp.concatenate([w_in_sh[j] for j in range(N_CHIPS)], axis=1)

    h0, h0b, kv_sw, qT_sb, kTb_sb, vTb_sb, kb_sb, vb_sb, qh_sw = _ln_in_proj(x2, row(ln_in_g), row(ln_in_b), w_in_f)
    k_sw, v_sw = kv_sw[:, :SWA_KV_WIDTH], kv_sw[:, SWA_KV_WIDTH:]
    sb_out, rsave, sb_first, (w_out_sh, w_gu_sh, w_down_sh) = _sb_fwd(qT_sb, kb_sb, vTb_sb, shards[1:])
    w_out_f = w_out_sh.reshape(D_MODEL, D_MODEL)
    w_down_f = w_down_sh.reshape(D_FF, D_MODEL)

    bias = _swa_bias(rel_bias, bucket)
    sink_rows = jnp.broadcast_to(sinks.reshape(SWA_HEADS, 1, 1), (SWA_HEADS, 1, BLOCK))
    kh_sw, vh_sw = _heads_rows(k_sw, SWA_KV_HEADS), _heads_rows(v_sw, SWA_KV_HEADS)
    swa_out = _swa_fwd(qh_sw, kh_sw, _heads_cols(v_sw, SWA_KV_HEADS), bias, sink_rows)

    pre1, merged, h1b = _mix_out(sb_out, swa_out, sb_norm_g, swa_norm_g, w_out_f, h0, ln1_g, ln1_b)
    act, silu, dsilu_up = _ffn_up(h1b, w_gu_sh)
    dp2, dp2b, dg2, db2, errsum = _ffn_down_loss(act, w_down_f, pre1, ln1_g, ln1_b, ln2_g, ln2_b, tgt)

    g_w_down = _matmul_tn(act, dp2b, "grad_w_down", FF_CHUNK, D_MODEL)
    dgate, dup = _ffn_down_bwd(dp2b, w_down_f, silu, dsilu_up)
    g_w_gu = _matmul_tn_pair(h1b, dgate, dup, "grad_w_gate_up")
    dp1, dp1b, dg1, db1 = _ffn_up_bwd(dgate, dup, w_gu_sh, dp2, pre1, ln1_g)
    g_w_out = _matmul_tn(merged, dp1b, "grad_w_out", D_MODEL, D_MODEL)
    doT_sb, doh_sw, dgsb, dgsw = _mix_bwd(dp1b, w_out_f, sb_out, swa_out, sb_norm_g, swa_norm_g)

    c = lax.axis_index("c").astype(jnp.int32)
    me = (2 * lax.axis_index("x") + lax.axis_index("y")).astype(jnp.int32)
    grads_a = [g_w_out.reshape(N_CHIPS, D_MODEL // N_CHIPS, D_MODEL), g_w_gu, g_w_down.reshape(N_CHIPS, D_FF // N_CHIPS, D_MODEL)]
    names_a = ("w_out", "w_gate_up", "w_down")
    dq_sw, dkh_sw, dvh_sw, dbias, dsink, swapped_a = _swa_bwd(qh_sw, kh_sw, _heads_cols(k_sw, SWA_KV_HEADS), vh_sw, bias,
                                                               sink_rows, doh_sw, grads_a)
    swa_small = _swa_small_grads(dbias, dsink, bucket)
    partials_a = [_pair_sum(g, r, c, "pair_sum_" + n) for g, r, n in zip(grads_a, swapped_a, names_a)]
    dq_sb, dk_sb, dv_sb, recv_a = _sb_bwd(qT_sb, kb_sb, kTb_sb, vb_sb,
                                             doT_sb, rsave, sb_first, partials_a)
    tok = lambda t, nh: t.reshape(nh, S, HEAD_DIM).transpose(1, 0, 2).reshape(S, nh * HEAD_DIM)
    dproj = [dq_sb, dk_sb, dv_sb, dq_sw,
             jnp.concatenate([tok(dkh_sw, SWA_KV_HEADS), tok(dvh_sw, SWA_KV_HEADS)], axis=1).astype(_MXU)]
    g_w_in = jnp.concatenate([_matmul_tn(h0b, d, "grad_w_in_%d" % k, D_MODEL, d.shape[1]) for k, d in enumerate(dproj)],
                             axis=1)

    cin = IN_COLS // N_CHIPS
    grads_b = [jnp.stack([g_w_in[:, j * cin:(j + 1) * cin] for j in range(N_CHIPS)])]
    partials_b = [_pair_sum(grads_b[0], _swap_halves(grads_b, "swap_halves_in")[0], c, "pair_sum_w_in")]
    grad_x, dg_in, db_in, recv_b = _in_proj_bwd(dproj, w_in_f, dp1, x2, row(ln_in_g), partials_b)
    names = ("w_in",) + names_a
    sums = [_chip_sum(p, r, me, "chip_sum_" + n) for p, r, n in zip(partials_b + partials_a, list(recv_b) + list(recv_a), names)]
    gs_in, gs_out, gs_gu, gs_down = _join_halves(sums)

    nrb = REL_BUCKETS * SWA_HEADS
    small = _pack_small(dg_in, db_in, dgsb, dgsw, swa_small[REL_BUCKETS, :SWA_HEADS],
                        swa_small[:REL_BUCKETS, :SWA_HEADS], dg1, db1, dg2, db2, errsum)
    g_small, loss_tile = _allreduce_small(small)
    loss = loss_tile[0, 0]

    big = []
    for name, w, g, m, v in (("adamw_w_in", w_in, gs_in, m_w_in, v_w_in), ("adamw_w_out", w_out, gs_out, m_w_out, v_w_out),
                             ("adamw_w_gate_up", w_gate_up, gs_gu, m_w_gate_up, v_w_gate_up),
                             ("adamw_w_down", w_down, gs_down, m_w_down, v_w_down)):
        d, nm, nv = _adamw(w[0], g, m[0], v[0], name)
        big.append((g[None], d[None], nm[None], nv[None]))
    zero = jnp.zeros((1,), F32)
    w_small = _pack_small(ln_in_g, ln_in_b, sb_norm_g, swa_norm_g, sinks, rel_bias, ln1_g, ln1_b, ln2_g, ln2_b, zero)
    m_small = _pack_small(m_ln_in_g, m_ln_in_b, m_sb_norm_g, m_swa_norm_g, m_sinks, m_rel_bias, m_ln1_g, m_ln1_b,
                          m_ln2_g, m_ln2_b, zero)
    v_small = _pack_small(v_ln_in_g, v_ln_in_b, v_sb_norm_g, v_swa_norm_g, v_sinks, v_rel_bias, v_ln1_g, v_ln1_b,
                          v_ln2_g, v_ln2_b, zero)
    small_out = [_unpack_small(t) for t in (g_small,) + tuple(_adamw(w_small, g_small, m_small, v_small, "adamw_small"))]

    def kind(k):
        s = small_out[k]
        return [s[0], s[1], big[0][k], s[2], s[3], s[4], s[5], big[1][k], s[6], s[7], big[2][k], big[3][k], s[8], s[9]]

    return (loss, grad_x.reshape(1, S, D_MODEL), *kind(0), *kind(1), *kind(2), *kind(3))
```

```python
import math

import numpy as np
import jax
import jax.numpy as jnp
from jax import lax
from jax.experimental import pallas as pl
from jax.experimental.pallas import tpu as pltpu

F32 = jnp.float32
_MXU = jnp.bfloat16

D_MODEL = 1024
HEAD_DIM = 64
SB_HEADS = 8
SWA_HEADS = 8
SWA_KV_HEADS = 2
SWA_GROUP = SWA_HEADS // SWA_KV_HEADS
SB_WIDTH = SB_HEADS * HEAD_DIM
SWA_WIDTH = SWA_HEADS * HEAD_DIM
SWA_KV_WIDTH = SWA_KV_HEADS * HEAD_DIM
IN_COLS = 3 * SB_WIDTH + SWA_WIDTH + 2 * SWA_KV_WIDTH
BLOCK = 128
REL_BUCKETS = 32
REL_MAX_DIST = 128
D_FF = 2816
FF_CHUNK = D_FF // 2
ALPHA = 2.0 ** 0.25
LN_EPS = 1e-5
RMS_EPS = 1e-6
SCALE = HEAD_DIM ** -0.5
SB_TILE = 256
SB_GROUP_FWD = 8
SB_GROUP_BWD = 4
SB_FORWARD_LEAD = 8
SB_DEAD = -105.0
SWA_SUB = 8

ADAM_LR = 0.001
ADAM_B1 = 0.9
ADAM_B2 = 0.999
ADAM_EPS = 1e-08
ADAM_WD = 0.01
ADAM_STEP = 10

N_CHIPS = 4
SMALL_ROWS = 16

MESH = pl.DeviceIdType.MESH


def _sds(shape, dtype):
    return jax.ShapeDtypeStruct(shape, dtype)


def _cp(sem=None, vmem_mb=48):
    kw = dict(vmem_limit_bytes=vmem_mb * 1024 * 1024)
    if sem is not None:
        kw["dimension_semantics"] = sem
    return pltpu.CompilerParams(**kw)


def _dot(a, b):
    return jnp.dot(a, b, preferred_element_type=F32)


def _dot_nt(a, b):
    return lax.dot_general(a, b, (((1,), (1,)), ((), ())), preferred_element_type=F32)


def _dot_tn(a, b):
    return lax.dot_general(a, b, (((0,), (0,)), ((), ())), preferred_element_type=F32)


def _ln_hat(x):
    mu = jnp.mean(x, axis=-1, keepdims=True)
    xc = x - mu
    var = jnp.mean(xc * xc, axis=-1, keepdims=True)
    rstd = lax.rsqrt(var + LN_EPS)
    return xc * rstd, rstd


def _ln_bwd(xhat, rstd, dy, g):
    dxh = dy * g
    m1 = jnp.mean(dxh, axis=-1, keepdims=True)
    m2 = jnp.mean(dxh * xhat, axis=-1, keepdims=True)
    return rstd * (dxh - m1 - xhat * m2)


def _colsum(x):
    return jnp.sum(x, axis=0, keepdims=True)


def _split2(x):
    hi = x.astype(_MXU)
    lo = (x - hi.astype(F32)).astype(_MXU)
    return hi, lo


def _rows(tm, n):
    return pl.BlockSpec((tm, n), lambda i: (i, 0))


def _fixed(*shape):
    nd = len(shape)
    return pl.BlockSpec(shape, lambda i: (0,) * nd)


IN_SECTIONS = (SB_WIDTH, SB_WIDTH, SB_WIDTH, SWA_WIDTH, 2 * SWA_KV_WIDTH)


def _ln_in_proj(x, g, b, w):
    S = x.shape[0]
    tm = min(S, SB_TILE)
    offs = np.cumsum((0,) + IN_SECTIONS)
    swa = (4,)

    def body(x_ref, g_ref, b_ref, w_ref, h_ref, hb_ref, *o_refs):
        p_refs, (qT_ref, kT_ref, vT_ref, kr_ref, vr_ref, qw_ref) = o_refs[:len(swa)], o_refs[len(swa):]
        xhat, _ = _ln_hat(x_ref[...])
        h = xhat * g_ref[...] + b_ref[...]
        h_ref[...] = h
        hb = h.astype(_MXU)
        hb_ref[...] = hb
        proj = _dot(hb, w_ref[...])
        for k, p_ref in zip(swa, p_refs):
            p_ref[...] = proj[:, offs[k]:offs[k + 1]].astype(p_ref.dtype)
        heads = lambda k: proj[:, offs[k]:offs[k + 1]].T.astype(_MXU).reshape(SB_HEADS, HEAD_DIM, tm)
        qT_ref[...] = heads(0)
        kT_ref[:, 0] = heads(1)
        vT_ref[:, 0] = heads(2)
        for hd in range(SB_HEADS):
            cols = slice(hd * HEAD_DIM, (hd + 1) * HEAD_DIM)
            kr_ref[hd, 0] = proj[:, offs[1]:offs[2]][:, cols].astype(_MXU)
            vr_ref[hd, 0] = proj[:, offs[2]:offs[3]][:, cols].astype(_MXU)
            qw_ref[hd] = proj[:, offs[3]:offs[4]][:, cols].astype(_MXU)

    blocked = pl.BlockSpec((SB_HEADS, 1, HEAD_DIM, tm), lambda i: (0, i, 0, 0))
    blocked_rows = pl.BlockSpec((SB_HEADS, 1, tm, HEAD_DIM), lambda i: (0, i, 0, 0))
    return pl.pallas_call(
        body, name="ln_in_proj", grid=(S // tm,),
        in_specs=[_rows(tm, D_MODEL), _fixed(1, D_MODEL), _fixed(1, D_MODEL), _fixed(D_MODEL, IN_COLS)],
        out_specs=[_rows(tm, D_MODEL), _rows(tm, D_MODEL)] + [_rows(tm, IN_SECTIONS[k]) for k in swa]
                  + [pl.BlockSpec((SB_HEADS, HEAD_DIM, tm), lambda i: (0, 0, i)), blocked, blocked, blocked_rows,
                     blocked_rows, pl.BlockSpec((SWA_HEADS, tm, HEAD_DIM), lambda i: (0, i, 0))],
        out_shape=[_sds((S, D_MODEL), F32), _sds((S, D_MODEL), _MXU)] + [_sds((S, IN_SECTIONS[k]), _MXU) for k in swa]
                  + [_sds((SB_HEADS, HEAD_DIM, S), _MXU), _sds((SB_HEADS, S // tm, HEAD_DIM, tm), _MXU),
                     _sds((SB_HEADS, S // tm, HEAD_DIM, tm), _MXU), _sds((SB_HEADS, S // tm, tm, HEAD_DIM), _MXU),
                     _sds((SB_HEADS, S // tm, tm, HEAD_DIM), _MXU), _sds((SWA_HEADS, S, HEAD_DIM), _MXU)],
        compiler_params=_cp(("parallel",)),
    )(x, g, b, w)


def _rms(x, g):
    r = lax.rsqrt(jnp.mean(x * x, axis=-1, keepdims=True) + RMS_EPS)
    return x * r * g, r


def _mix_out(sb, sw, gsb, gsw, w_out, h0, g1, b1):
    S = sb.shape[0]
    tm = min(S, 512)

    def body(sb_ref, sw_ref, gsb_ref, gsw_ref, w_ref, h0_ref, g1_ref, b1_ref, pre_ref, mg_ref, h1_ref):
        ysb, _ = _rms(sb_ref[...], gsb_ref[...])
        ysw, _ = _rms(sw_ref[...], gsw_ref[...])
        ysb = ysb.astype(_MXU)
        ysw = ysw.astype(_MXU)
        mg_ref[:, :SB_WIDTH] = ysb
        mg_ref[:, SB_WIDTH:] = ysw
        mix = _dot(ysb, w_ref[:SB_WIDTH, :]) + _dot(ysw, w_ref[SB_WIDTH:, :])
        pre1 = ALPHA * h0_ref[...] + mix
        pre_ref[...] = pre1
        xhat, _ = _ln_hat(pre1)
        h1_ref[...] = (xhat * g1_ref[...] + b1_ref[...]).astype(h1_ref.dtype)

    vec = _fixed(1, D_MODEL)
    return pl.pallas_call(
        body, name="mix_out", grid=(S // tm,),
        in_specs=[_rows(tm, SB_WIDTH), _rows(tm, SWA_WIDTH), _fixed(1, SB_WIDTH), _fixed(1, SWA_WIDTH),
                  _fixed(D_MODEL, D_MODEL), _rows(tm, D_MODEL), vec, vec],
        out_specs=[_rows(tm, D_MODEL), _rows(tm, D_MODEL), _rows(tm, D_MODEL)],
        out_shape=[_sds((S, D_MODEL), F32), _sds((S, D_MODEL), _MXU), _sds((S, D_MODEL), _MXU)],
        compiler_params=_cp(("parallel",)),
    )(sb, sw, gsb, gsw, w_out, h0, g1, b1)


def _sigmoid(x):
    return 1.0 / (1.0 + jnp.exp(-x))


def _ffn_up(h1b, wgu):
    S = h1b.shape[0]
    tm = min(S, 1024)

    def body(h_ref, wg_ref, wu_ref, a_ref, s1_ref, s2_ref):
        h1 = h_ref[...]
        gate = _dot(h1, wg_ref[0])
        up = _dot(h1, wu_ref[0])
        sg = _sigmoid(gate)
        silu = gate * sg
        a_ref[...] = (silu * up).astype(a_ref.dtype)
        s1_ref[...] = silu.astype(s1_ref.dtype)
        s2_ref[...] = (up * (sg * (1.0 + gate * (1.0 - sg)))).astype(s2_ref.dtype)

    chunk = pl.BlockSpec((tm, FF_CHUNK), lambda j, i: (i, j))
    return pl.pallas_call(
        body, name="ffn_up", grid=(2, S // tm),
        in_specs=[pl.BlockSpec((tm, D_MODEL), lambda j, i: (i, 0)),
                  pl.BlockSpec((1, D_MODEL, FF_CHUNK), lambda j, i: (j, 0, 0)),
                  pl.BlockSpec((1, D_MODEL, FF_CHUNK), lambda j, i: (j + 2, 0, 0))],
        out_specs=[chunk, chunk, chunk],
        out_shape=[_sds((S, D_FF), _MXU)] * 3,
        compiler_params=_cp(("arbitrary", "arbitrary"), vmem_mb=56),
    )(h1b, wgu, wgu)


def _ffn_down_loss(a, w_down, pre1, g1, b1, g2, b2, tgt):
    S = a.shape[0]
    tm = min(S, 512)

    def body(a_ref, w_ref, p_ref, g1_ref, b1_ref, g2_ref, b2_ref, t_ref, d_ref, db_ref, dg2_ref, db2_ref, err_ref):
        @pl.when(pl.program_id(0) == 0)
        def _():
            dg2_ref[...] = jnp.zeros_like(dg2_ref)
            db2_ref[...] = jnp.zeros_like(db2_ref)
            err_ref[...] = jnp.zeros_like(err_ref)

        xhat1, _ = _ln_hat(p_ref[...])
        h1 = xhat1 * g1_ref[...] + b1_ref[...]
        pre2 = ALPHA * h1 + _dot(a_ref[...], w_ref[...])
        xhat2, rstd2 = _ln_hat(pre2)
        err = xhat2 * g2_ref[...] + b2_ref[...] - t_ref[...]
        dh2 = err * (1.0 / D_MODEL)
        dp2 = _ln_bwd(xhat2, rstd2, dh2, g2_ref[...])
        d_ref[...] = dp2
        db_ref[...] = dp2.astype(db_ref.dtype)
        dg2_ref[...] += _colsum(dh2 * xhat2)
        db2_ref[...] += _colsum(dh2)
        err_ref[...] += _colsum(err * err)

    vec = _fixed(1, D_MODEL)
    return pl.pallas_call(
        body, name="ffn_down_loss", grid=(S // tm,),
        in_specs=[_rows(tm, D_FF), _fixed(D_FF, D_MODEL), _rows(tm, D_MODEL), vec, vec, vec, vec, _rows(tm, D_MODEL)],
        out_specs=[_rows(tm, D_MODEL), _rows(tm, D_MODEL), vec, vec, vec],
        out_shape=[_sds((S, D_MODEL), F32), _sds((S, D_MODEL), _MXU), _sds((1, D_MODEL), F32), _sds((1, D_MODEL), F32),
                   _sds((1, D_MODEL), F32)],
        compiler_params=_cp(("arbitrary",)),
    )(a, w_down, pre1, g1, b1, g2, b2, tgt)


def _ffn_down_bwd(dp2b, w_down, s1, s2):
    S = dp2b.shape[0]
    tm = min(S, 1024)

    def body(d_ref, w_ref, s1_ref, s2_ref, dg_ref, du_ref):
        da = _dot_nt(d_ref[...], w_ref[...])
        du_ref[...] = (da * s1_ref[...].astype(F32)).astype(du_ref.dtype)
        dg_ref[...] = (da * s2_ref[...].astype(F32)).astype(dg_ref.dtype)

    chunk = pl.BlockSpec((tm, FF_CHUNK), lambda j, i: (i, j))
    return pl.pallas_call(
        body, name="ffn_down_bwd", grid=(2, S // tm),
        in_specs=[pl.BlockSpec((tm, D_MODEL), lambda j, i: (i, 0)),
                  pl.BlockSpec((FF_CHUNK, D_MODEL), lambda j, i: (j, 0)), chunk, chunk],
        out_specs=[chunk, chunk],
        out_shape=[_sds((S, D_FF), _MXU), _sds((S, D_FF), _MXU)],
        compiler_params=_cp(("arbitrary", "arbitrary")),
    )(dp2b, w_down, s1, s2)


def _ffn_up_bwd(dgate, dup, wgu, dp2, pre1, g1):
    S = dgate.shape[0]
    tm = min(S, 256)

    def body(dg_ref, du_ref, w_ref, d2_ref, p_ref, g_ref, d1_ref, d1b_ref, dg1_ref, db1_ref):
        @pl.when(pl.program_id(0) == 0)
        def _():
            dg1_ref[...] = jnp.zeros_like(dg1_ref)
            db1_ref[...] = jnp.zeros_like(db1_ref)

        dh1 = ALPHA * d2_ref[...]
        for j in range(2):
            cols = slice(j * FF_CHUNK, (j + 1) * FF_CHUNK)
            dh1 += _dot_nt(dg_ref[:, cols], w_ref[j])
            dh1 += _dot_nt(du_ref[:, cols], w_ref[j + 2])
        xhat, rstd = _ln_hat(p_ref[...])
        dp1 = _ln_bwd(xhat, rstd, dh1, g_ref[...])
        d1_ref[...] = dp1
        d1b_ref[...] = dp1.astype(d1b_ref.dtype)
        dg1_ref[...] += _colsum(dh1 * xhat)
        db1_ref[...] += _colsum(dh1)

    vec = _fixed(1, D_MODEL)
    return pl.pallas_call(
        body, name="ffn_up_bwd", grid=(S // tm,),
        in_specs=[_rows(tm, D_FF), _rows(tm, D_FF), _fixed(4, D_MODEL, FF_CHUNK), _rows(tm, D_MODEL),
                  _rows(tm, D_MODEL), vec],
        out_specs=[_rows(tm, D_MODEL), _rows(tm, D_MODEL), vec, vec],
        out_shape=[_sds((S, D_MODEL), F32), _sds((S, D_MODEL), _MXU), _sds((1, D_MODEL), F32), _sds((1, D_MODEL), F32)],
        compiler_params=_cp(("arbitrary",), vmem_mb=56),
    )(dgate, dup, wgu, dp2, pre1, g1)


def _rms_bwd(x, g, dy):
    n = x.shape[-1]
    r = lax.rsqrt(jnp.mean(x * x, axis=-1, keepdims=True) + RMS_EPS)
    u = dy * g
    dx = r * u - x * (r * r * r) * (jnp.sum(u * x, axis=-1, keepdims=True) * (1.0 / n))
    return dx, _colsum(dy * x * r)


def _mix_bwd(dp1b, w_out, sb, sw, gsb, gsw):
    S = sb.shape[0]
    tm = min(S, 512)

    def body(d_ref, w_ref, sb_ref, sw_ref, gsb_ref, gsw_ref, dsb_ref, dsw_ref, dgsb_ref, dgsw_ref):
        @pl.when(pl.program_id(0) == 0)
        def _():
            dgsb_ref[...] = jnp.zeros_like(dgsb_ref)
            dgsw_ref[...] = jnp.zeros_like(dgsw_ref)

        dm = _dot_nt(d_ref[...], w_ref[...])
        dsb, dgsb = _rms_bwd(sb_ref[...], gsb_ref[...], dm[:, :SB_WIDTH])
        dsw, dgsw = _rms_bwd(sw_ref[...], gsw_ref[...], dm[:, SB_WIDTH:])
        dsb_ref[...] = dsb.T.astype(dsb_ref.dtype).reshape(dsb_ref.shape)
        for hd in range(SWA_HEADS):
            dsw_ref[hd] = dsw[:, hd * HEAD_DIM:(hd + 1) * HEAD_DIM].astype(dsw_ref.dtype)
        dgsb_ref[...] += dgsb
        dgsw_ref[...] += dgsw

    return pl.pallas_call(
        body, name="mix_bwd", grid=(S // tm,),
        in_specs=[_rows(tm, D_MODEL), _fixed(D_MODEL, D_MODEL), _rows(tm, SB_WIDTH), _rows(tm, SWA_WIDTH),
                  _fixed(1, SB_WIDTH), _fixed(1, SWA_WIDTH)],
        out_specs=[pl.BlockSpec((SB_HEADS, HEAD_DIM, tm), lambda i: (0, 0, i)),
                   pl.BlockSpec((SWA_HEADS, tm, HEAD_DIM), lambda i: (0, i, 0)), _fixed(1, SB_WIDTH), _fixed(1, SWA_WIDTH)],
        out_shape=[_sds((SB_HEADS, HEAD_DIM, S), _MXU), _sds((SWA_HEADS, S, HEAD_DIM), _MXU), _sds((1, SB_WIDTH), F32),
                   _sds((1, SWA_WIDTH), F32)],
        compiler_params=_cp(("arbitrary",)),
    )(dp1b, w_out, sb, sw, gsb, gsw)


def _in_proj_bwd(dproj, w_in, dp1, x, g, parts):
    S = x.shape[0]
    tm = min(S, 512)
    nw = len(parts)
    ns = len(IN_SECTIONS)
    offs = np.cumsum((0,) + IN_SECTIONS)
    s_ins, s_outs, s_sems = _scatter_io(parts)

    def body(*refs):
        dpj_refs = refs[:ns]
        w_ref, d1_ref, x_ref, g_ref = refs[ns:ns + 4]
        rest = refs[ns + 4:]
        gx_ref, dg_ref, db_ref = rest[nw:nw + 3]
        scatter = _Scatter(rest[:nw], rest[nw + 3:2 * nw + 3], *rest[2 * nw + 3:])

        @pl.when(pl.program_id(0) == 0)
        def _():
            scatter.start()
            dg_ref[...] = jnp.zeros_like(dg_ref)
            db_ref[...] = jnp.zeros_like(db_ref)

        dh0 = ALPHA * d1_ref[...]
        for k in range(ns):
            dh0 += _dot_nt(dpj_refs[k][...], w_ref[:, offs[k]:offs[k + 1]])
        xhat, rstd = _ln_hat(x_ref[...])
        gx_ref[...] = _ln_bwd(xhat, rstd, dh0, g_ref[...])
        dg_ref[...] += _colsum(dh0 * xhat)
        db_ref[...] += _colsum(dh0)

        @pl.when(pl.program_id(0) == pl.num_programs(0) - 1)
        def _():
            scatter.finish()

    vec = _fixed(1, D_MODEL)
    any_spec = pl.BlockSpec(memory_space=pl.ANY)
    res = pl.pallas_call(
        body, name="in_proj_bwd", grid=(S // tm,),
        in_specs=[_rows(tm, n) for n in IN_SECTIONS]
                 + [_fixed(D_MODEL, IN_COLS), _rows(tm, D_MODEL), _rows(tm, D_MODEL), vec] + [any_spec] * nw,
        out_specs=[_rows(tm, D_MODEL), vec, vec] + [any_spec] * nw,
        out_shape=[_sds((S, D_MODEL), F32), _sds((1, D_MODEL), F32), _sds((1, D_MODEL), F32)] + s_outs,
        scratch_shapes=s_sems,
        compiler_params=_cp(("arbitrary",)),
    )(*dproj, w_in, dp1, x, g, *s_ins)
    return res[0], res[1], res[2], list(res[3:])


def _matmul_tn(a, b, name, tk, tn):
    T, K = a.shape
    N = b.shape[1]
    tt = min(T, 1024)

    def body(a_ref, b_ref, o_ref):
        @pl.when(pl.program_id(2) == 0)
        def _():
            o_ref[...] = jnp.zeros_like(o_ref)

        o_ref[...] += _dot_tn(a_ref[...], b_ref[...])

    return pl.pallas_call(
        body, name=name, grid=(K // tk, N // tn, T // tt),
        in_specs=[pl.BlockSpec((tt, tk), lambda k, n, t: (t, k)), pl.BlockSpec((tt, tn), lambda k, n, t: (t, n))],
        out_specs=pl.BlockSpec((tk, tn), lambda k, n, t: (k, n)),
        out_shape=_sds((K, N), F32),
        compiler_params=_cp(("parallel", "parallel", "arbitrary")),
    )(a, b)


def _place():
    x, y, c = lax.axis_index("x"), lax.axis_index("y"), lax.axis_index("c")
    chips = [(1 - x, y), (x, 1 - y), (1 - x, 1 - y)]
    return x, y, c, chips


class _Gather:
    def __init__(self, in_refs, out_refs, send_sems, recv_sems):
        self.in_refs, self.out_refs, self.send_sems, self.recv_sems = in_refs, out_refs, send_sems, recv_sems
        self.x, self.y, self.c, self.chips = _place()

    def _copy(self, w, k, chip, hc, to, src=None):
        part = self.out_refs[w].at[2 * chip[0] + chip[1], hc]
        return pltpu.make_async_remote_copy(
            src_ref=part if src is None else src, dst_ref=part, send_sem=self.send_sems.at[w, k],
            recv_sem=self.recv_sems.at[w, k], device_id=to, device_id_type=MESH)

    def _first(self):
        x, y, c = self.x, self.y, self.c
        return [self._copy(w, j, (x, y), c, (*chip, c), src=self.in_refs[w].at[c])
                for w in range(len(self.in_refs)) for j, chip in enumerate(self.chips)]

    def start(self):
        for cp in self._first():
            cp.start()

    def _passed(self):
        sibling = (self.x, self.y, 1 - self.c)
        return [self._copy(w, 3 + j, chip, self.c, sibling)
                for w in range(len(self.in_refs)) for j, chip in enumerate(self.chips)]

    def forward(self):
        me = (self.x, self.y, self.c)
        passed = self._passed()
        for w in range(len(self.in_refs)):
            for j, chip in enumerate(self.chips):
                self._copy(w, j, chip, self.c, me).wait_recv()
                passed[3 * w + j].start()

    def finish(self):
        me = (self.x, self.y, self.c)
        for w in range(len(self.in_refs)):
            for j, chip in enumerate(self.chips):
                self._copy(w, 3 + j, chip, 1 - self.c, me).wait_recv()
        for cp in self._first() + self._passed():
            cp.wait_send()


def _gather_io(shards):
    halves = [(s.shape[0] // 2, s.shape[1]) for s in shards]
    ins = [s.reshape(2, h, cols) for s, (h, cols) in zip(shards, halves)]
    outs = [_sds((N_CHIPS, 2, h, cols), s.dtype) for s, (h, cols) in zip(shards, halves)]
    sems = [pltpu.SemaphoreType.DMA((len(shards), 6)), pltpu.SemaphoreType.DMA((len(shards), 6))]
    return ins, outs, sems


def _gather_assemble(outs, shards):
    me = 2 * lax.axis_index("x") + lax.axis_index("y")
    return [lax.dynamic_update_slice_in_dim(o.reshape((N_CHIPS,) + s.shape), s[None], me, axis=0)
            for o, s in zip(outs, shards)]


class _Scatter:
    def __init__(self, p_refs, out_refs, send_sems, recv_sems):
        self.p_refs, self.out_refs, self.send_sems, self.recv_sems = p_refs, out_refs, send_sems, recv_sems
        self.x, self.y, self.c, self.chips = _place()
        self.me = 2 * self.x + self.y

    def _copy(self, w, j, chip, src_chip, dst_chip):
        return pltpu.make_async_remote_copy(
            src_ref=self.p_refs[w].at[src_chip], dst_ref=self.out_refs[w].at[dst_chip], send_sem=self.send_sems.at[w, j],
            recv_sem=self.recv_sems.at[w, j], device_id=(*chip, self.c), device_id_type=MESH)

    def _sends(self):
        return [self._copy(w, j, chip, 2 * chip[0] + chip[1], self.me)
                for w in range(len(self.p_refs)) for j, chip in enumerate(self.chips)]

    def start(self):
        for cp in self._sends():
            cp.start()

    def finish(self):
        for w in range(len(self.p_refs)):
            for j, chip in enumerate(self.chips):
                self._copy(w, j, chip, self.me, 2 * chip[0] + chip[1]).wait_recv()
        for cp in self._sends():
            cp.wait_send()


def _scatter_io(parts):
    sems = [pltpu.SemaphoreType.DMA((len(parts), 3)), pltpu.SemaphoreType.DMA((len(parts), 3))]
    return list(parts), [_sds(p.shape, p.dtype) for p in parts], sems


class _Swap:
    def __init__(self, g_refs, out_refs, send_sems, recv_sems):
        x, y, c, _ = _place()
        self.copies = []
        for w in range(len(g_refs)):
            half = out_refs[w].shape[1]
            theirs = g_refs[w].at[:, pl.ds(pl.multiple_of((1 - c) * half, 8), half), :]
            self.copies.append(pltpu.make_async_remote_copy(
                src_ref=theirs, dst_ref=out_refs[w], send_sem=send_sems.at[w], recv_sem=recv_sems.at[w],
                device_id=(x, y, 1 - c), device_id_type=MESH))

    def start(self):
        for cp in self.copies:
            cp.start()

    def finish(self):
        for cp in self.copies:
            cp.wait()


def _swap_io(grads):
    outs = [_sds((g.shape[0], g.shape[1] // 2, g.shape[2]), g.dtype) for g in grads]
    return list(grads), outs, [pltpu.SemaphoreType.DMA((len(grads),)), pltpu.SemaphoreType.DMA((len(grads),))]


def _matmul_tn_pair(a, b0, b1, name):
    T, K = a.shape
    tt = min(T, 1024)

    def body(a_ref, b0_ref, b1_ref, o_ref):
        n = pl.program_id(0)

        @pl.when(pl.program_id(1) == 0)
        def _():
            o_ref[...] = jnp.zeros_like(o_ref)

        @pl.when(n < 2)
        def _():
            o_ref[0] += _dot_tn(a_ref[...], b0_ref[...])

        @pl.when(n >= 2)
        def _():
            o_ref[0] += _dot_tn(a_ref[...], b1_ref[...])

    return pl.pallas_call(
        body, name=name, grid=(4, T // tt),
        in_specs=[pl.BlockSpec((tt, K), lambda n, t: (t, 0)),
                  pl.BlockSpec((tt, FF_CHUNK), lambda n, t: (t, jnp.minimum(n, 1))),
                  pl.BlockSpec((tt, FF_CHUNK), lambda n, t: (t, jnp.maximum(n - 2, 0)))],
        out_specs=pl.BlockSpec((1, K, FF_CHUNK), lambda n, t: (n, 0, 0)),
        out_shape=_sds((4, K, FF_CHUNK), F32),
        compiler_params=_cp(("parallel", "arbitrary")),
    )(a, b0, b1)


def _sb_logs(zt, causal):
    e = jnp.exp(-jnp.abs(zt))
    lb = jnp.minimum(zt, 0.0) - jnp.log(1.0 + e)
    l1m = lb - zt
    if causal is not None:
        l1m = jnp.where(causal, l1m, 0.0)
    return lb, l1m


def _sb_weights(lb, suf, causal):
    a = jnp.exp(lb + suf)
    if causal is not None:
        a = jnp.where(causal, a, 0.0)
    return a


def _tri_masks(t):
    r = lax.broadcasted_iota(jnp.int32, (t, t), 0)
    c = lax.broadcasted_iota(jnp.int32, (t, t), 1)
    return r, c


def _sb_fwd(qT, kb, vTb, shards):
    Hh, _, S = qT.shape
    nk, T = kb.shape[1], kb.shape[2]
    nq = S // T
    G = SB_GROUP_FWD
    nw = len(shards)
    g_ins, g_outs, g_sems = _gather_io(shards)
    forward_step = max(nq - 1 - SB_FORWARD_LEAD, 0)

    def body(qT_ref, k_ref, vT_ref, *rest):
        o_ref, rs_ref, first_ref = rest[nw:nw + 3]
        gather = _Gather(rest[:nw], rest[nw + 3:2 * nw + 3], *rest[2 * nw + 3:])
        i = pl.program_id(1)
        first_step = jnp.logical_and(pl.program_id(0) == 0, i == 0)
        last_step = jnp.logical_and(pl.program_id(0) == pl.num_programs(0) - 1, i == pl.num_programs(1) - 1)

        @pl.when(first_step)
        def _():
            gather.start()

        qts = [(qT_ref[g].astype(F32) * SCALE).astype(_MXU) for g in range(G)]
        r, c = _tri_masks(T)
        upper = (c > r).astype(_MXU)
        causal = r < c

        def blk(j, carry, mask):
            hs = range(G)
            for g in hs:
                rs_ref[g, 0, j] = jnp.broadcast_to(carry[g][0], (8, T))
            zs = [_dot(k_ref[g, j], qts[g]) for g in hs]
            lbs, l1ms = zip(*[_sb_logs(zs[g], mask) for g in hs])
            splits = [_split2(l1ms[g]) for g in hs]
            cums = [_dot(upper, splits[g][0]) + _dot(upper, splits[g][1]) for g in hs]
            avs = [_sb_weights(lbs[g], carry[g][0] + cums[g], mask).astype(_MXU) for g in hs]
            accs = [carry[g][1] + _dot(vT_ref[g, j], avs[g]) for g in hs]
            return tuple((carry[g][0] + _colsum(l1ms[g]), accs[g]) for g in hs)

        def go_on(j, carry):
            top = carry[0][0]
            for g in range(1, G):
                top = jnp.maximum(top, carry[g][0])
            return jnp.logical_and(j >= 0, jnp.max(top) >= SB_DEAD)

        init = tuple((jnp.zeros((1, T), F32), jnp.zeros((HEAD_DIM, T), F32)) for _ in range(G))
        carry = blk(i, init, causal)
        j, carry = lax.while_loop(lambda st: go_on(*st), lambda st: (st[0] - 1, blk(st[0], st[1], None)),
                                  (i - 1, carry))

        first_ref[...] = jnp.broadcast_to((j + 1).astype(F32), first_ref.shape)

        o_ref[...] = jnp.concatenate([carry[g][1] for g in range(G)], axis=0).T

        @pl.when(jnp.logical_and(pl.program_id(0) == pl.num_programs(0) - 1, i == forward_step))
        def _():
            gather.forward()

        @pl.when(last_step)
        def _():
            gather.finish()

    any_spec = pl.BlockSpec(memory_space=pl.ANY)
    res = pl.pallas_call(
        body, name="sb_fwd", grid=(Hh // G, nq),
        in_specs=[pl.BlockSpec((G, HEAD_DIM, T), lambda h, i: (h, 0, i)),
                  pl.BlockSpec((G, nk, T, HEAD_DIM), lambda h, i: (h, 0, 0, 0), pipeline_mode=pl.Buffered(1)),
                  pl.BlockSpec((G, nk, HEAD_DIM, T), lambda h, i: (h, 0, 0, 0), pipeline_mode=pl.Buffered(1))]
                 + [any_spec] * nw,
        out_specs=[pl.BlockSpec((T, G * HEAD_DIM), lambda h, i: (i, h)),
                   pl.BlockSpec((G, 1, nk, 8, T), lambda h, i: (h, i, 0, 0, 0)),
                   pl.BlockSpec((1, 1, 8, 128), lambda h, i: (h, i, 0, 0))] + [any_spec] * nw,
        out_shape=[_sds((S, Hh * HEAD_DIM), F32), _sds((Hh, nq, nk, 8, T), F32), _sds((Hh // G, nq, 8, 128), F32)]
                  + g_outs,
        scratch_shapes=g_sems,
        compiler_params=_cp(("arbitrary", "arbitrary")),
    )(qT, kb, vTb, *g_ins)
    return res[0], res[1], res[2], _gather_assemble(res[3:], shards)


def _sb_bwd(qT, kb, kTb, vb, doT, rsave, first, parts):
    Hh, _, S = qT.shape
    nk, T = kb.shape[1], kb.shape[2]
    nq = S // T
    G = SB_GROUP_BWD
    nw = len(parts)
    s_ins, s_outs, s_sems = _scatter_io(parts)

    def body(qT_ref, k_ref, kT_ref, v_ref, doT_ref, rs_ref, first_ref, *rest):
        dq_ref, dk_out_ref, dv_out_ref = rest[nw:nw + 3]
        dk_ref, dv_ref = rest[2 * nw + 3:2 * nw + 5]
        scatter = _Scatter(rest[:nw], rest[nw + 3:2 * nw + 3], *rest[2 * nw + 5:])
        i = pl.program_id(1)
        first_step = jnp.logical_and(pl.program_id(0) == 0, i == 0)
        last_step = jnp.logical_and(pl.program_id(0) == pl.num_programs(0) - 1, i == pl.num_programs(1) - 1)

        @pl.when(first_step)
        def _():
            scatter.start()

        @pl.when(i == 0)
        def _():
            dk_ref[...] = jnp.zeros_like(dk_ref)
            dv_ref[...] = jnp.zeros_like(dv_ref)

        qts = [(qT_ref[g].astype(F32) * SCALE).astype(_MXU) for g in range(G)]
        douts = [doT_ref[g] for g in range(G)]
        r, c = _tri_masks(T)
        upper = (c > r).astype(_MXU)
        lower = (c < r).astype(_MXU)
        causal = r < c

        def blk(j, carry, mask):
            hs = range(G)
            zs = [_dot(k_ref[g, j], qts[g]) for g in hs]
            das = [_dot(v_ref[g, j], douts[g]) for g in hs]
            lbs, l1ms = zip(*[_sb_logs(zs[g], mask) for g in hs])
            splits = [_split2(l1ms[g]) for g in hs]
            cums = [_dot(upper, splits[g][0]) + _dot(upper, splits[g][1]) for g in hs]
            avs = [_sb_weights(lbs[g], rs_ref[g, 0, j][0:1, :] + cums[g], mask) for g in hs]
            ets = [das[g] * avs[g] for g in hs]
            esplits = [_split2(ets[g]) for g in hs]
            ecums = [_dot(lower, esplits[g][0]) + _dot(lower, esplits[g][1]) for g in hs]
            dzs = []
            for g in hs:
                sig = jnp.exp(lbs[g])
                dz = ets[g] * (1.0 - sig) - (carry[g][0] + ecums[g]) * sig
                if mask is not None:
                    dz = jnp.where(mask, dz, 0.0)
                dzs.append(dz.astype(_MXU))
            dqs = [carry[g][1] + _dot(kT_ref[g, j], dzs[g]) for g in hs]
            for g in hs:
                dk_ref[j, g * HEAD_DIM:(g + 1) * HEAD_DIM, :] += _dot_nt(qts[g], dzs[g])
            for g in hs:
                dv_ref[j, g * HEAD_DIM:(g + 1) * HEAD_DIM, :] += _dot_nt(douts[g], avs[g].astype(_MXU))
            return tuple((carry[g][0] + _colsum(ets[g]), dqs[g]) for g in hs)

        first = jnp.clip(jnp.max(first_ref[0, 0][0:1, 0:1]).astype(jnp.int32), 0, i)
        carry = tuple((jnp.zeros((1, T), F32), jnp.zeros((HEAD_DIM, T), F32)) for _ in range(G))
        carry = lax.fori_loop(first, i, lambda s, cr: blk(s, cr, None), carry)
        carry = blk(i, carry, causal)
        dq_ref[...] = (jnp.concatenate([carry[g][1] for g in range(G)], axis=0) * SCALE).T.astype(dq_ref.dtype)

        @pl.when(i == pl.num_programs(1) - 1)
        def _():
            def flush(j, _):
                rows = pl.ds(pl.multiple_of(j * T, T), T)
                dk_out_ref[rows, :] = dk_ref[j].T.astype(dk_out_ref.dtype)
                dv_out_ref[rows, :] = dv_ref[j].T.astype(dv_out_ref.dtype)
                return 0
            lax.fori_loop(0, nk, flush, 0)

        @pl.when(last_step)
        def _():
            scatter.finish()

    colblk = pl.BlockSpec((G, HEAD_DIM, T), lambda h, i: (h, 0, i))
    once = pl.Buffered(1)
    kblk = pl.BlockSpec((G, nk, T, HEAD_DIM), lambda h, i: (h, 0, 0, 0), pipeline_mode=once)
    kTblk = pl.BlockSpec((G, nk, HEAD_DIM, T), lambda h, i: (h, 0, 0, 0), pipeline_mode=once)
    any_spec = pl.BlockSpec(memory_space=pl.ANY)
    res = pl.pallas_call(
        body, name="sb_bwd", grid=(Hh // G, nq),
        in_specs=[colblk, kblk, kTblk, kblk, colblk,
                  pl.BlockSpec((G, 1, nk, 8, T), lambda h, i: (h, i, 0, 0, 0)),
                  pl.BlockSpec((1, 1, 8, 128), lambda h, i: ((h * G) // SB_GROUP_FWD, i, 0, 0))] + [any_spec] * nw,
        out_specs=[pl.BlockSpec((T, G * HEAD_DIM), lambda h, i: (i, h)),
                   pl.BlockSpec((S, G * HEAD_DIM), lambda h, i: (0, h), pipeline_mode=once),
                   pl.BlockSpec((S, G * HEAD_DIM), lambda h, i: (0, h), pipeline_mode=once)] + [any_spec] * nw,
        out_shape=[_sds((S, Hh * HEAD_DIM), _MXU)] * 3 + s_outs,
        scratch_shapes=[pltpu.VMEM((nk, G * HEAD_DIM, T), F32), pltpu.VMEM((nk, G * HEAD_DIM, T), F32)] + s_sems,
        compiler_params=_cp(("arbitrary", "arbitrary"), vmem_mb=60),
    )(qT, kb, kTb, vb, doT, rsave, first, *s_ins)
    return res[0], res[1], res[2], list(res[3:])


def _bucket_table():
    qi = np.arange(BLOCK)[:, None]
    cj = np.arange(2 * BLOCK)[None, :]
    dist = qi + BLOCK - cj
    exact = REL_BUCKETS // 2
    d = np.maximum(dist, 0)
    d_f = np.maximum(d, 1).astype(np.float32)
    large = exact + (np.log(d_f / np.float32(exact)) / np.float32(math.log(REL_MAX_DIST / exact))
                     * np.float32(REL_BUCKETS - exact)).astype(np.int32)
    large = np.minimum(large, REL_BUCKETS - 1)
    return np.where(d < exact, d, large).astype(np.int32)


def _swa_bias(rel_bias, bucket):
    def body(rb_ref, bk_ref, o_ref):
        bk = bk_ref[...]
        for h in range(SWA_HEADS):
            t = jnp.zeros((2 * BLOCK, BLOCK), F32)
            for b in range(REL_BUCKETS):
                t = jnp.where(bk == b, rb_ref[b, h], t)
            o_ref[h] = t

    return pl.pallas_call(
        body, name="swa_bias",
        in_specs=[pl.BlockSpec(memory_space=pltpu.SMEM), pl.BlockSpec(memory_space=pltpu.VMEM)],
        out_specs=pl.BlockSpec(memory_space=pltpu.VMEM),
        out_shape=_sds((SWA_HEADS, 2 * BLOCK, BLOCK), F32),
    )(rel_bias, bucket)


def _swa_logits(q, kp, kc):
    qs = (q.astype(F32) * SCALE).astype(_MXU)
    return qs, _dot_nt(kp, qs), _dot_nt(kc, qs)


def _swa_softmax(lp, lc, bias, sink, live_prev):
    r, c = _tri_masks(BLOCK)
    in_window = r > c if live_prev is None else jnp.logical_and(r > c, live_prev)
    lp = jnp.where(in_window, lp + bias[:BLOCK, :], -jnp.inf)
    lc = jnp.where(r <= c, lc + bias[BLOCK:, :], -jnp.inf)
    m = jnp.maximum(jnp.maximum(jnp.max(lp, axis=0, keepdims=True), jnp.max(lc, axis=0, keepdims=True)), sink)
    pp = jnp.exp(lp - m)
    pc = jnp.exp(lc - m)
    ps = jnp.exp(sink - m)
    denom = _colsum(pp) + _colsum(pc) + ps
    return pp / denom, pc / denom, ps / denom


def _swa_sub(nb):
    return min(SWA_SUB, nb)


def _swa_keys(b, prev_ref, cur_ref, i):
    cur = cur_ref[0, b * BLOCK:(b + 1) * BLOCK, :]
    if b == 0:
        return prev_ref[0], cur, i > 0
    return cur_ref[0, (b - 1) * BLOCK:b * BLOCK, :], cur, None


def _swa_keys_t(b, prev_ref, cur_ref):
    cur = cur_ref[0, :, b * BLOCK:(b + 1) * BLOCK]
    return (prev_ref[0] if b == 0 else cur_ref[0, :, (b - 1) * BLOCK:b * BLOCK]), cur


SWA_PAIR = 4


def _swa_fwd(q, k, vT, bias, sink):
    S = q.shape[1]
    nb = S // BLOCK
    ns = _swa_sub(nb)
    R = ns * BLOCK
    P = SWA_PAIR

    def body(q_ref, kp_ref, kc_ref, vp_ref, vc_ref, bias_ref, sink_ref, o_ref):
        i = pl.program_id(1)
        units = [(hh, b) for hh in range(P) for b in range(ns)]
        keys = [_swa_keys(b, kp_ref, kc_ref, i) for b in range(ns)]
        vals = [_swa_keys_t(b, vp_ref, vc_ref) for b in range(ns)]
        logits = {u: _swa_logits(q_ref[u[0], u[1] * BLOCK:(u[1] + 1) * BLOCK, :], keys[u[1]][0], keys[u[1]][1])
                  for u in units}
        ws = {u: _swa_softmax(logits[u][1], logits[u][2], bias_ref[u[0]], sink_ref[u[0]][:, :1], keys[u[1]][2])
              for u in units}
        outs = {u: _dot(vals[u[1]][0], ws[u][0].astype(_MXU)) + _dot(vals[u[1]][1], ws[u][1].astype(_MXU))
                for u in units}
        for b in range(ns):
            o_ref[b * BLOCK:(b + 1) * BLOCK, :] = jnp.concatenate([outs[(hh, b)] for hh in range(P)], axis=0).T

    kvh = lambda p: (p * P) // SWA_GROUP
    prev = pl.BlockSpec((1, BLOCK, HEAD_DIM), lambda p, i: (kvh(p), jnp.maximum(i * ns - 1, 0), 0))
    cur = pl.BlockSpec((1, R, HEAD_DIM), lambda p, i: (kvh(p), i, 0))
    prev_t = pl.BlockSpec((1, HEAD_DIM, BLOCK), lambda p, i: (kvh(p), 0, jnp.maximum(i * ns - 1, 0)))
    cur_t = pl.BlockSpec((1, HEAD_DIM, R), lambda p, i: (kvh(p), 0, i))
    return pl.pallas_call(
        body, name="swa_fwd", grid=(SWA_HEADS // P, nb // ns),
        in_specs=[pl.BlockSpec((P, R, HEAD_DIM), lambda p, i: (p, i, 0)), prev, cur, prev_t, cur_t,
                  pl.BlockSpec((P, 2 * BLOCK, BLOCK), lambda p, i: (p, 0, 0)),
                  pl.BlockSpec((P, 1, BLOCK), lambda p, i: (p, 0, 0))],
        out_specs=pl.BlockSpec((R, P * HEAD_DIM), lambda p, i: (i, p)),
        out_shape=_sds((S, SWA_HEADS * HEAD_DIM), F32),
        compiler_params=_cp(("parallel", "parallel")),
    )(q, k, k, vT, vT, bias, sink)


def _swa_bwd(q, k, kT, v, bias, sink, do, grads):
    S = q.shape[1]
    nb = S // BLOCK
    ns = _swa_sub(nb)
    R = ns * BLOCK
    P = SWA_PAIR
    nw = len(grads)
    x_ins, x_outs, x_sems = _swap_io(grads)

    def body(q_ref, kp_ref, kc_ref, ktp_ref, ktc_ref, vp_ref, vc_ref, bias_ref, sink_ref, do_ref, *rest):
        dq_ref, dk_ref, dv_ref, dbias_ref, dsink_ref = rest[nw:nw + 5]
        swap = _Swap(rest[:nw], rest[nw + 5:2 * nw + 5], *rest[2 * nw + 5:])
        g = pl.program_id(1)
        i = pl.program_id(2)
        first_step = jnp.logical_and(pl.program_id(0) == 0, jnp.logical_and(g == 0, i == 0))
        last_step = jnp.logical_and(pl.program_id(0) == pl.num_programs(0) - 1,
                                    jnp.logical_and(g == pl.num_programs(1) - 1, i == pl.num_programs(2) - 1))

        @pl.when(first_step)
        def _():
            swap.start()

        @pl.when(jnp.logical_and(g == 0, i == 0))
        def _():
            dk_ref[...] = jnp.zeros_like(dk_ref)
            dv_ref[...] = jnp.zeros_like(dv_ref)

        @pl.when(i == 0)
        def _():
            dbias_ref[...] = jnp.zeros_like(dbias_ref)
            dsink_ref[...] = jnp.zeros_like(dsink_ref)

        subs = range(ns)
        units = [(hh, b) for hh in range(P) for b in subs]
        rows = [slice(b * BLOCK, (b + 1) * BLOCK) for b in subs]
        keys = [_swa_keys(b, kp_ref, kc_ref, i) for b in subs]
        keys_t = [_swa_keys_t(b, ktp_ref, ktc_ref) for b in subs]
        vals = [_swa_keys(b, vp_ref, vc_ref, i) for b in subs]
        douts = {u: do_ref[u[0], rows[u[1]], :] for u in units}
        logits = {u: _swa_logits(q_ref[u[0], rows[u[1]], :], keys[u[1]][0], keys[u[1]][1]) for u in units}
        dws = {u: (_dot_nt(vals[u[1]][0], douts[u]), _dot_nt(vals[u[1]][1], douts[u])) for u in units}
        wts, dls = {}, {}
        for hh in range(P):
            dbp = jnp.zeros((BLOCK, BLOCK), F32)
            dbc = jnp.zeros((BLOCK, BLOCK), F32)
            dsk = jnp.zeros((1, BLOCK), F32)
            for b in subs:
                u = (hh, b)
                wp, wc, ws = _swa_softmax(logits[u][1], logits[u][2], bias_ref[hh], sink_ref[hh][:, :1], keys[b][2])
                dwp, dwc = dws[u]
                delta = _colsum(wp * dwp) + _colsum(wc * dwc)
                dlp = wp * (dwp - delta)
                dlc = wc * (dwc - delta)
                dbp += dlp
                dbc += dlc
                dsk -= ws * delta
                wts[u] = (wp.astype(_MXU), wc.astype(_MXU))
                dls[u] = (dlp.astype(_MXU), dlc.astype(_MXU))
            dbias_ref[hh, :BLOCK, :] += dbp
            dbias_ref[hh, BLOCK:, :] += dbc
            dsink_ref[hh] += jnp.broadcast_to(dsk, (8, BLOCK))
        dqs = {u: (_dot(keys_t[u[1]][0], dls[u][0]) + _dot(keys_t[u[1]][1], dls[u][1])) * SCALE for u in units}
        for b in subs:
            dq_ref[rows[b], :] = jnp.concatenate([dqs[(hh, b)] for hh in range(P)], axis=0).T.astype(dq_ref.dtype)
        dk_cur = [sum(_dot(dls[(hh, b)][1], logits[(hh, b)][0]) for hh in range(P)) for b in subs]
        dv_cur = [sum(_dot(wts[(hh, b)][1], douts[(hh, b)]) for hh in range(P)) for b in subs]
        dk_prev = [sum(_dot(dls[(hh, b)][0], logits[(hh, b)][0]) for hh in range(P)) for b in subs]
        dv_prev = [sum(_dot(wts[(hh, b)][0], douts[(hh, b)]) for hh in range(P)) for b in subs]
        for b in subs:
            last = b + 1 == ns
            dk_ref[0, i * ns + b] += dk_cur[b] if last else dk_cur[b] + dk_prev[b + 1]
            dv_ref[0, i * ns + b] += dv_cur[b] if last else dv_cur[b] + dv_prev[b + 1]

        @pl.when(i > 0)
        def _():
            dk_ref[0, i * ns - 1] += dk_prev[0]
            dv_ref[0, i * ns - 1] += dv_prev[0]

        @pl.when(last_step)
        def _():
            swap.finish()

    G2 = SWA_GROUP // P
    hp = lambda kv, g, i: kv * G2 + g
    prev = pl.BlockSpec((1, BLOCK, HEAD_DIM), lambda kv, g, i: (kv, jnp.maximum(i * ns - 1, 0), 0))
    cur = pl.BlockSpec((1, R, HEAD_DIM), lambda kv, g, i: (kv, i, 0))
    prev_t = pl.BlockSpec((1, HEAD_DIM, BLOCK), lambda kv, g, i: (kv, 0, jnp.maximum(i * ns - 1, 0)))
    cur_t = pl.BlockSpec((1, HEAD_DIM, R), lambda kv, g, i: (kv, 0, i))
    qblk = pl.BlockSpec((P, R, HEAD_DIM), lambda kv, g, i: (hp(kv, g, i), i, 0))
    kvacc = pl.BlockSpec((1, nb, BLOCK, HEAD_DIM), lambda kv, g, i: (kv, 0, 0, 0))
    any_spec = pl.BlockSpec(memory_space=pl.ANY)
    res = pl.pallas_call(
        body, name="swa_bwd", grid=(SWA_KV_HEADS, G2, nb // ns),
        in_specs=[qblk, prev, cur, prev_t, cur_t, prev, cur,
                  pl.BlockSpec((P, 2 * BLOCK, BLOCK), lambda kv, g, i: (hp(kv, g, i), 0, 0)),
                  pl.BlockSpec((P, 1, BLOCK), lambda kv, g, i: (hp(kv, g, i), 0, 0)), qblk] + [any_spec] * nw,
        out_specs=[pl.BlockSpec((R, P * HEAD_DIM), lambda kv, g, i: (i, hp(kv, g, i))), kvacc, kvacc,
                   pl.BlockSpec((P, 2 * BLOCK, BLOCK), lambda kv, g, i: (hp(kv, g, i), 0, 0)),
                   pl.BlockSpec((P, 8, BLOCK), lambda kv, g, i: (hp(kv, g, i), 0, 0))] + [any_spec] * nw,
        out_shape=[_sds((S, SWA_HEADS * HEAD_DIM), _MXU), _sds((SWA_KV_HEADS, nb, BLOCK, HEAD_DIM), F32),
                   _sds((SWA_KV_HEADS, nb, BLOCK, HEAD_DIM), F32), _sds((SWA_HEADS, 2 * BLOCK, BLOCK), F32),
                   _sds((SWA_HEADS, 8, BLOCK), F32)] + x_outs,
        scratch_shapes=x_sems,
        compiler_params=_cp(("arbitrary", "arbitrary", "arbitrary")),
    )(q, k, k, kT, kT, v, v, bias, sink, do, *x_ins)
    return res[0], res[1], res[2], res[3], res[4], list(res[5:])


def _swa_small_grads(dbias, dsink, bucket):
    rows = REL_BUCKETS + 8

    def total(x):
        return jnp.sum(jnp.sum(x, axis=1, keepdims=True), axis=0, keepdims=True)

    def body(db_ref, ds_ref, bk_ref, o_ref):
        bk = bk_ref[...]
        r = lax.broadcasted_iota(jnp.int32, (rows, BLOCK), 0)
        c = lax.broadcasted_iota(jnp.int32, (rows, BLOCK), 1)
        out = jnp.zeros((rows, BLOCK), F32)
        for h in range(SWA_HEADS):
            db = db_ref[h]
            for b in range(REL_BUCKETS):
                s = total(jnp.where(bk == b, db, 0.0))
                out = jnp.where(jnp.logical_and(r == b, c == h), s, out)
            s = jnp.sum(ds_ref[h][0:1, :], axis=1, keepdims=True)
            out = jnp.where(jnp.logical_and(r == REL_BUCKETS, c == h), s, out)
        o_ref[...] = out

    vm = pl.BlockSpec(memory_space=pltpu.VMEM)
    return pl.pallas_call(body, name="swa_small_grads", in_specs=[vm, vm, vm], out_specs=vm,
                          out_shape=_sds((rows, BLOCK), F32))(dbias, dsink, bucket)


def _tile_rows(n):
    for t in (512, 352, 256, 176, 128, 64, 32, 16, 8):
        if n % t == 0:
            return t
    return n


def _cast_rows(x, dtype, name):
    R, C = x.shape
    tr = _tile_rows(R)

    def body(x_ref, o_ref):
        o_ref[...] = x_ref[...].astype(o_ref.dtype)

    return pl.pallas_call(body, name=name, grid=(R // tr,), in_specs=[_rows(tr, C)], out_specs=_rows(tr, C),
                          out_shape=_sds((R, C), dtype), compiler_params=_cp(("parallel",)))(x)


def _pair_sum(g, recv, c, name):
    n, half, C = recv.shape
    tr = _tile_rows(half)

    def body(c_ref, a_ref, b_ref, o_ref):
        o_ref[...] = (a_ref[0] + b_ref[...]).astype(o_ref.dtype)

    return pl.pallas_call(
        body, name=name,
        grid_spec=pltpu.PrefetchScalarGridSpec(
            num_scalar_prefetch=1, grid=(n, half // tr),
            in_specs=[pl.BlockSpec((1, 1, tr, C), lambda j, i, c_ref: (j, c_ref[0], i, 0)),
                      pl.BlockSpec((1, tr, C), lambda j, i, c_ref: (j, i, 0))],
            out_specs=pl.BlockSpec((1, tr, C), lambda j, i, c_ref: (j, i, 0))),
        out_shape=_sds((n, half, C), _MXU),
        compiler_params=_cp(("parallel", "parallel")))(c.reshape(1), g.reshape(n, 2, half, C), recv)


def _chip_sum(own, recv, me, name):
    n, R, C = recv.shape
    tr = _tile_rows(R)

    def body(me_ref, own_ref, recv_ref, o_ref):
        acc = None
        for j in range(n):
            term = jnp.where(me_ref[0] == j, own_ref[0], recv_ref[j]).astype(F32)
            acc = term if acc is None else acc + term
        o_ref[...] = acc

    return pl.pallas_call(
        body, name=name,
        grid_spec=pltpu.PrefetchScalarGridSpec(
            num_scalar_prefetch=1, grid=(R // tr,),
            in_specs=[pl.BlockSpec((1, tr, C), lambda i, me_ref: (me_ref[0], i, 0)),
                      pl.BlockSpec((n, tr, C), lambda i, me_ref: (0, i, 0))],
            out_specs=pl.BlockSpec((tr, C), lambda i, me_ref: (i, 0))),
        out_shape=_sds((R, C), F32), compiler_params=_cp(("parallel",)))(me.reshape(1), own, recv)


def _adamw_math(w, g, m, v):
    m = ADAM_B1 * m + (1.0 - ADAM_B1) * g
    v = ADAM_B2 * v + (1.0 - ADAM_B2) * (g * g)
    m_hat = m / (1.0 - ADAM_B1 ** ADAM_STEP)
    v_hat = v / (1.0 - ADAM_B2 ** ADAM_STEP)
    delta = -ADAM_LR * (m_hat / (jnp.sqrt(v_hat) + ADAM_EPS) + ADAM_WD * w)
    return delta, m, v


def _adamw(w, g, m, v, name):
    R, C = w.shape
    tr = _tile_rows(R)

    def body(w_ref, g_ref, m_ref, v_ref, d_ref, nm_ref, nv_ref):
        d, nm, nv = _adamw_math(w_ref[...], g_ref[...], m_ref[...], v_ref[...])
        d_ref[...] = d
        nm_ref[...] = nm
        nv_ref[...] = nv

    blk = _rows(tr, C)
    return pl.pallas_call(body, name=name, grid=(R // tr,), in_specs=[blk] * 4, out_specs=[blk] * 3,
                          out_shape=[_sds((R, C), F32)] * 3, compiler_params=_cp(("parallel",)))(w, g, m, v)


def _gather_weights(shards):
    nw = len(shards)
    ins, outs, sems = _gather_io(shards)

    def body(*refs):
        ex = _Gather(refs[:nw], refs[nw:2 * nw], *refs[2 * nw:])
        ex.start()
        ex.forward()
        ex.finish()

    any_spec = pl.BlockSpec(memory_space=pl.ANY)
    got = pl.pallas_call(body, name="gather_weights", in_specs=[any_spec] * nw, out_specs=[any_spec] * nw,
                         out_shape=outs, scratch_shapes=sems)(*ins)
    return _gather_assemble(got, shards)


def _swap_halves(grads, name):
    nw = len(grads)
    ins, outs, sems = _swap_io(grads)

    def body(*refs):
        ex = _Swap(refs[:nw], refs[nw:2 * nw], *refs[2 * nw:])
        ex.start()
        ex.finish()

    any_spec = pl.BlockSpec(memory_space=pl.ANY)
    return pl.pallas_call(body, name=name, in_specs=[any_spec] * nw, out_specs=[any_spec] * nw,
                          out_shape=outs, scratch_shapes=sems)(*ins)


def _join_halves(sums):
    nw = len(sums)

    def body(*refs):
        f_refs, out_refs = refs[:nw], refs[nw:2 * nw]
        send_sems, recv_sems = refs[2 * nw:]
        x, y, c, _ = _place()
        ws = range(nw)

        def copy(w, half_index):
            return pltpu.make_async_remote_copy(
                src_ref=f_refs[w], dst_ref=out_refs[w].at[half_index], send_sem=send_sems.at[w],
                recv_sem=recv_sems.at[w], device_id=(x, y, 1 - c), device_id_type=MESH)

        sends = [copy(w, c) for w in ws]
        for cp in sends:
            cp.start()
        for w in ws:
            copy(w, 1 - c).wait_recv()
        for cp in sends:
            cp.wait_send()

    any_spec = pl.BlockSpec(memory_space=pl.ANY)
    outs = pl.pallas_call(
        body, name="join_halves", in_specs=[any_spec] * nw, out_specs=[any_spec] * nw,
        out_shape=[_sds((2,) + f.shape, f.dtype) for f in sums],
        scratch_shapes=[pltpu.SemaphoreType.DMA((nw,)), pltpu.SemaphoreType.DMA((nw,))],
    )(*sums)
    c = lax.axis_index("c")
    return [lax.dynamic_update_slice_in_dim(o, f[None], c, axis=0).reshape(2 * f.shape[0], f.shape[1])
            for o, f in zip(outs, sums)]


def _allreduce_small(block):
    m_per, n = block.shape

    def body(x_ref, sum_ref, loss_ref, all_ref, send_sems, recv_sems, local_sem):
        x, y, c, chips = _place()
        me, sibling = (x, y, c), (x, y, 1 - c)

        def rows(px, py, pc):
            return all_ref.at[pl.ds(pl.multiple_of((4 * px + 2 * py + pc) * m_per, 8), m_per), :]

        def copy(k, blk, to, src=None):
            return pltpu.make_async_remote_copy(
                src_ref=rows(*blk) if src is None else src, dst_ref=rows(*blk), send_sem=send_sems.at[k],
                recv_sem=recv_sems.at[k], device_id=to, device_id_type=MESH)

        mine = pltpu.make_async_copy(x_ref, rows(*me), local_sem)
        mine.start()
        first = [copy(0, me, sibling, src=x_ref)]
        first += [copy(1 + j, me, (*chip, c), src=x_ref) for j, chip in enumerate(chips)]
        for cp in first:
            cp.start()
        passed = [copy(4 + j, (*chip, c), sibling) for j, chip in enumerate(chips)]
        for j, chip in enumerate(chips):
            copy(1 + j, (*chip, c), me).wait_recv()
            passed[j].start()
        copy(0, sibling, me).wait_recv()
        for j, chip in enumerate(chips):
            copy(4 + j, (*chip, 1 - c), me).wait_recv()
        for cp in first + passed:
            cp.wait_send()
        mine.wait()

        acc = all_ref[0:m_per, :]
        for d in range(1, 8):
            acc = acc + all_ref[d * m_per:(d + 1) * m_per, :]
        sum_ref[...] = acc
        tot = jnp.sum(acc[8:9, :], axis=1, keepdims=True) * (0.5 / D_MODEL)
        loss_ref[...] = jnp.broadcast_to(tot, loss_ref.shape)

    vm = pl.BlockSpec(memory_space=pltpu.VMEM)
    return pl.pallas_call(
        body, name="allreduce_small", in_specs=[vm], out_specs=[vm, vm],
        out_shape=[_sds((m_per, n), F32), _sds((8, 128), F32)],
        scratch_shapes=[pltpu.VMEM((8 * m_per, n), F32), pltpu.SemaphoreType.DMA((7,)), pltpu.SemaphoreType.DMA((7,)),
                        pltpu.SemaphoreType.DMA],
    )(block)


def _heads_rows(x, nh):
    S = x.shape[0]
    return x.reshape(S, nh, HEAD_DIM).transpose(1, 0, 2)


def _heads_cols(x, nh):
    S = x.shape[0]
    return x.reshape(S, nh, HEAD_DIM).transpose(1, 2, 0)


def _pad_row(v):
    v = v.reshape(1, -1)
    return jnp.pad(v, ((0, 0), (0, D_MODEL - v.shape[1])))


def _pack_small(ln_in_g, ln_in_b, sb_g, swa_g, sinks, rel_bias, ln1_g, ln1_b, ln2_g, ln2_b, extra):
    rows = [_pad_row(ln_in_g), _pad_row(ln_in_b), jnp.concatenate([sb_g.reshape(1, -1), swa_g.reshape(1, -1)], axis=1),
            _pad_row(jnp.concatenate([rel_bias.reshape(1, -1), sinks.reshape(1, -1)], axis=1)),
            _pad_row(ln1_g), _pad_row(ln1_b), _pad_row(ln2_g), _pad_row(ln2_b), _pad_row(extra)]
    rows.append(jnp.zeros((SMALL_ROWS - len(rows), D_MODEL), F32))
    return jnp.concatenate(rows, axis=0)


def _unpack_small(blk):
    nrb = REL_BUCKETS * SWA_HEADS
    return (blk[0], blk[1], blk[2:3, :SB_WIDTH], blk[2:3, SB_WIDTH:], blk[3:4, nrb:nrb + SWA_HEADS],
            blk[3, :nrb].reshape(REL_BUCKETS, SWA_HEADS), blk[4:5], blk[5:6], blk[6:7], blk[7:8])


def kernel(x, ln_in_g, ln_in_b, w_in, sb_norm_g, swa_norm_g, sinks, rel_bias, w_out, ln1_g, ln1_b, w_gate_up, w_down, ln2_g, ln2_b, loss_target, m_ln_in_g, m_ln_in_b, m_w_in, m_sb_norm_g, m_swa_norm_g, m_sinks, m_rel_bias, m_w_out, m_ln1_g, m_ln1_b, m_w_gate_up, m_w_down, m_ln2_g, m_ln2_b, v_ln_in_g, v_ln_in_b, v_w_in, v_sb_norm_g, v_swa_norm_g, v_sinks, v_rel_bias, v_w_out, v_ln1_g, v_ln1_b, v_w_gate_up, v_w_down, v_ln2_g, v_ln2_b):
    S = x.shape[1]
    x2 = x.reshape(S, D_MODEL)
    tgt = loss_target.reshape(S, D_MODEL)
    T = min(S, SB_TILE)
    bucket = jnp.asarray(_bucket_table().T)
    row = lambda v: v.reshape(1, -1)

    shards = [_cast_rows(w[0], _MXU, "cast_" + n) for n, w in (("w_in", w_in), ("w_out", w_out), ("w_gate_up", w_gate_up), ("w_down", w_down))]
    (w_in_sh,) = _gather_weights(shards[:1])
    w_in_f = jnp.concatenate([w_in_sh[j] for j in range(N_CHIPS)], axis=1)

    h0, h0b, kv_sw, qT_sb, kTb_sb, vTb_sb, kb_sb, vb_sb, qh_sw = _ln_in_proj(x2, row(ln_in_g), row(ln_in_b), w_in_f)
    k_sw, v_sw = kv_sw[:, :SWA_KV_WIDTH], kv_sw[:, SWA_KV_WIDTH:]
    sb_out, rsave, sb_first, (w_out_sh, w_gu_sh, w_down_sh) = _sb_fwd(qT_sb, kb_sb, vTb_sb, shards[1:])
    w_out_f = w_out_sh.reshape(D_MODEL, D_MODEL)
    w_down_f = w_down_sh.reshape(D_FF, D_MODEL)

    bias = _swa_bias(rel_bias, bucket)
    sink_rows = jnp.broadcast_to(sinks.reshape(SWA_HEADS, 1, 1), (SWA_HEADS, 1, BLOCK))
    kh_sw, vh_sw = _heads_rows(k_sw, SWA_KV_HEADS), _heads_rows(v_sw, SWA_KV_HEADS)
    swa_out = _swa_fwd(qh_sw, kh_sw, _heads_cols(v_sw, SWA_KV_HEADS), bias, sink_rows)

    pre1, merged, h1b = _mix_out(sb_out, swa_out, sb_norm_g, swa_norm_g, w_out_f, h0, ln1_g, ln1_b)
    act, silu, dsilu_up = _ffn_up(h1b, w_gu_sh)
    dp2, dp2b, dg2, db2, errsum = _ffn_down_loss(act, w_down_f, pre1, ln1_g, ln1_b, ln2_g, ln2_b, tgt)

    g_w_down = _matmul_tn(act, dp2b, "grad_w_down", FF_CHUNK, D_MODEL)
    dgate, dup = _ffn_down_bwd(dp2b, w_down_f, silu, dsilu_up)
    g_w_gu = _matmul_tn_pair(h1b, dgate, dup, "grad_w_gate_up")
    dp1, dp1b, dg1, db1 = _ffn_up_bwd(dgate, dup, w_gu_sh, dp2, pre1, ln1_g)
    g_w_out = _matmul_tn(merged, dp1b, "grad_w_out", D_MODEL, D_MODEL)
    doT_sb, doh_sw, dgsb, dgsw = _mix_bwd(dp1b, w_out_f, sb_out, swa_out, sb_norm_g, swa_norm_g)

    c = lax.axis_index("c").astype(jnp.int32)
    me = (2 * lax.axis_index("x") + lax.axis_index("y")).astype(jnp.int32)
    grads_a = [g_w_out.reshape(N_CHIPS, D_MODEL // N_CHIPS, D_MODEL), g_w_gu, g_w_down.reshape(N_CHIPS, D_FF // N_CHIPS, D_MODEL)]
    names_a = ("w_out", "w_gate_up", "w_down")
    dq_sw, dkh_sw, dvh_sw, dbias, dsink, swapped_a = _swa_bwd(qh_sw, kh_sw, _heads_cols(k_sw, SWA_KV_HEADS), vh_sw, bias,
                                                               sink_rows, doh_sw, grads_a)
    swa_small = _swa_small_grads(dbias, dsink, bucket)
    partials_a = [_pair_sum(g, r, c, "pair_sum_" + n) for g, r, n in zip(grads_a, swapped_a, names_a)]
    dq_sb, dk_sb, dv_sb, recv_a = _sb_bwd(qT_sb, kb_sb, kTb_sb, vb_sb,
                                             doT_sb, rsave, sb_first, partials_a)
    tok = lambda t, nh: t.reshape(nh, S, HEAD_DIM).transpose(1, 0, 2).reshape(S, nh * HEAD_DIM)
    dproj = [dq_sb, dk_sb, dv_sb, dq_sw,
             jnp.concatenate([tok(dkh_sw, SWA_KV_HEADS), tok(dvh_sw, SWA_KV_HEADS)], axis=1).astype(_MXU)]
    g_w_in = jnp.concatenate([_matmul_tn(h0b, d, "grad_w_in_%d" % k, D_MODEL, d.shape[1]) for k, d in enumerate(dproj)],
                             axis=1)

    cin = IN_COLS // N_CHIPS
    grads_b = [jnp.stack([g_w_in[:, j * cin:(j + 1) * cin] for j in range(N_CHIPS)])]
    partials_b = [_pair_sum(grads_b[0], _swap_halves(grads_b, "swap_halves_in")[0], c, "pair_sum_w_in")]
    grad_x, dg_in, db_in, recv_b = _in_proj_bwd(dproj, w_in_f, dp1, x2, row(ln_in_g), partials_b)
    names = ("w_in",) + names_a
    sums = [_chip_sum(p, r, me, "chip_sum_" + n) for p, r, n in zip(partials_b + partials_a, list(recv_b) + list(recv_a), names)]
    gs_in, gs_out, gs_gu, gs_down = _join_halves(sums)

    nrb = REL_BUCKETS * SWA_HEADS
    small = _pack_small(dg_in, db_in, dgsb, dgsw, swa_small[REL_BUCKETS, :SWA_HEADS],
                        swa_small[:REL_BUCKETS, :SWA_HEADS], dg1, db1, dg2, db2, errsum)
    g_small, loss_tile = _allreduce_small(small)
    loss = loss_tile[0, 0]

    gT_in = gs_in.T
    dT, nmT, nvT = _adamw(w_in[0].T, gT_in, m_w_in[0].T, v_w_in[0].T, "adamw_w_in")
    big = [(gT_in.T[None], dT.T[None], nmT.T[None], nvT.T[None])]
    for name, w, g, m, v in (("adamw_w_out", w_out, gs_out, m_w_out, v_w_out),
                             ("adamw_w_gate_up", w_gate_up, gs_gu, m_w_gate_up, v_w_gate_up),
                             ("adamw_w_down", w_down, gs_down, m_w_down, v_w_down)):
        d, nm, nv = _adamw(w[0], g, m[0], v[0], name)
        big.append((g[None], d[None], nm[None], nv[None]))
    zero = jnp.zeros((1,), F32)
    w_small = _pack_small(ln_in_g, ln_in_b, sb_norm_g, swa_norm_g, sinks, rel_bias, ln1_g, ln1_b, ln2_g, ln2_b, zero)
    m_small = _pack_small(m_ln_in_g, m_ln_in_b, m_sb_norm_g, m_swa_norm_g, m_sinks, m_rel_bias, m_ln1_g, m_ln1_b,
                          m_ln2_g, m_ln2_b, zero)
    v_small = _pack_small(v_ln_in_g, v_ln_in_b, v_sb_norm_g, v_swa_norm_g, v_sinks, v_rel_bias, v_ln1_g, v_ln1_b,
                          v_ln2_g, v_ln2_b, zero)
    small_out = [_unpack_small(t) for t in (g_small,) + tuple(_adamw(w_small, g_small, m_small, v_small, "adamw_small"))]

    def kind(k):
        s = small_out[k]
        return [s[0], s[1], big[0][k], s[2], s[3], s[4], s[5], big[1][k], s[6], s[7], big[2][k], big[3][k], s[8], s[9]]

    return (loss, grad_x.reshape(1, S, D_MODEL), *kind(0), *kind(1), *kind(2), *kind(3))
```

```python
import math

import numpy as np
import jax
import jax.numpy as jnp
from jax import lax
from jax.experimental import pallas as pl
from jax.experimental.pallas import tpu as pltpu

F32 = jnp.float32
_MXU = jnp.bfloat16

D_MODEL = 1024
HEAD_DIM = 64
SB_HEADS = 8
SWA_HEADS = 8
SWA_KV_HEADS = 2
SWA_GROUP = SWA_HEADS // SWA_KV_HEADS
SB_WIDTH = SB_HEADS * HEAD_DIM
SWA_WIDTH = SWA_HEADS * HEAD_DIM
SWA_KV_WIDTH = SWA_KV_HEADS * HEAD_DIM
IN_COLS = 3 * SB_WIDTH + SWA_WIDTH + 2 * SWA_KV_WIDTH
BLOCK = 128
REL_BUCKETS = 32
REL_MAX_DIST = 128
D_FF = 2816
FF_CHUNK = D_FF // 2
ALPHA = 2.0 ** 0.25
LN_EPS = 1e-5
RMS_EPS = 1e-6
SCALE = HEAD_DIM ** -0.5
SB_TILE = 256
SB_GROUP_FWD = 8
SB_GROUP_BWD = 4
SB_FORWARD_LEAD = 8
SB_DEAD = -105.0
SWA_SUB = 8

ADAM_LR = 0.001
ADAM_B1 = 0.9
ADAM_B2 = 0.999
ADAM_EPS = 1e-08
ADAM_WD = 0.01
ADAM_STEP = 10

N_CHIPS = 4
SMALL_ROWS = 16

MESH = pl.DeviceIdType.MESH


def _sds(shape, dtype):
    return jax.ShapeDtypeStruct(shape, dtype)


def _cp(sem=None, vmem_mb=48):
    kw = dict(vmem_limit_bytes=vmem_mb * 1024 * 1024)
    if sem is not None:
        kw["dimension_semantics"] = sem
    return pltpu.CompilerParams(**kw)


def _dot(a, b):
    return jnp.dot(a, b, preferred_element_type=F32)


def _dot_nt(a, b):
    return lax.dot_general(a, b, (((1,), (1,)), ((), ())), preferred_element_type=F32)


def _dot_tn(a, b):
    return lax.dot_general(a, b, (((0,), (0,)), ((), ())), preferred_element_type=F32)


def _ln_hat(x):
    mu = jnp.mean(x, axis=-1, keepdims=True)
    xc = x - mu
    var = jnp.mean(xc * xc, axis=-1, keepdims=True)
    rstd = lax.rsqrt(var + LN_EPS)
    return xc * rstd, rstd


def _ln_bwd(xhat, rstd, dy, g):
    dxh = dy * g
    m1 = jnp.mean(dxh, axis=-1, keepdims=True)
    m2 = jnp.mean(dxh * xhat, axis=-1, keepdims=True)
    return rstd * (dxh - m1 - xhat * m2)


def _colsum(x):
    return jnp.sum(x, axis=0, keepdims=True)


def _split2(x):
    hi = x.astype(_MXU)
    lo = (x - hi.astype(F32)).astype(_MXU)
    return hi, lo


def _rows(tm, n):
    return pl.BlockSpec((tm, n), lambda i: (i, 0))


def _fixed(*shape):
    nd = len(shape)
    return pl.BlockSpec(shape, lambda i: (0,) * nd)


IN_SECTIONS = (SB_WIDTH, SB_WIDTH, SB_WIDTH, SWA_WIDTH, 2 * SWA_KV_WIDTH)


def _ln_in_proj(x, g, b, w):
    S = x.shape[0]
    tm = min(S, SB_TILE)
    offs = np.cumsum((0,) + IN_SECTIONS)
    swa = (4,)

    def body(x_ref, g_ref, b_ref, w_ref, h_ref, hb_ref, *o_refs):
        p_refs, (qT_ref, kT_ref, vT_ref, kr_ref, vr_ref, qw_ref) = o_refs[:len(swa)], o_refs[len(swa):]
        xhat, _ = _ln_hat(x_ref[...])
        h = xhat * g_ref[...] + b_ref[...]
        h_ref[...] = h
        hb = h.astype(_MXU)
        hb_ref[...] = hb
        proj = _dot(hb, w_ref[...])
        for k, p_ref in zip(swa, p_refs):
            p_ref[...] = proj[:, offs[k]:offs[k + 1]].astype(p_ref.dtype)
        heads = lambda k: proj[:, offs[k]:offs[k + 1]].T.astype(_MXU).reshape(SB_HEADS, HEAD_DIM, tm)
        qT_ref[...] = heads(0)
        kT_ref[:, 0] = heads(1)
        vT_ref[:, 0] = heads(2)
        for hd in range(SB_HEADS):
            cols = slice(hd * HEAD_DIM, (hd + 1) * HEAD_DIM)
            kr_ref[hd, 0] = proj[:, offs[1]:offs[2]][:, cols].astype(_MXU)
            vr_ref[hd, 0] = proj[:, offs[2]:offs[3]][:, cols].astype(_MXU)
            qw_ref[hd] = proj[:, offs[3]:offs[4]][:, cols].astype(_MXU)

    blocked = pl.BlockSpec((SB_HEADS, 1, HEAD_DIM, tm), lambda i: (0, i, 0, 0))
    blocked_rows = pl.BlockSpec((SB_HEADS, 1, tm, HEAD_DIM), lambda i: (0, i, 0, 0))
    return pl.pallas_call(
        body, name="ln_in_proj", grid=(S // tm,),
        in_specs=[_rows(tm, D_MODEL), _fixed(1, D_MODEL), _fixed(1, D_MODEL), _fixed(D_MODEL, IN_COLS)],
        out_specs=[_rows(tm, D_MODEL), _rows(tm, D_MODEL)] + [_rows(tm, IN_SECTIONS[k]) for k in swa]
                  + [pl.BlockSpec((SB_HEADS, HEAD_DIM, tm), lambda i: (0, 0, i)), blocked, blocked, blocked_rows,
                     blocked_rows, pl.BlockSpec((SWA_HEADS, tm, HEAD_DIM), lambda i: (0, i, 0))],
        out_shape=[_sds((S, D_MODEL), F32), _sds((S, D_MODEL), _MXU)] + [_sds((S, IN_SECTIONS[k]), _MXU) for k in swa]
                  + [_sds((SB_HEADS, HEAD_DIM, S), _MXU), _sds((SB_HEADS, S // tm, HEAD_DIM, tm), _MXU),
                     _sds((SB_HEADS, S // tm, HEAD_DIM, tm), _MXU), _sds((SB_HEADS, S // tm, tm, HEAD_DIM), _MXU),
                     _sds((SB_HEADS, S // tm, tm, HEAD_DIM), _MXU), _sds((SWA_HEADS, S, HEAD_DIM), _MXU)],
        compiler_params=_cp(("parallel",)),
    )(x, g, b, w)


def _rms(x, g):
    r = lax.rsqrt(jnp.mean(x * x, axis=-1, keepdims=True) + RMS_EPS)
    return x * r * g, r


def _mix_out(sb, sw, gsb, gsw, w_out, h0, g1, b1):
    S = sb.shape[0]
    tm = min(S, 512)

    def body(sb_ref, sw_ref, gsb_ref, gsw_ref, w_ref, h0_ref, g1_ref, b1_ref, pre_ref, mg_ref, h1_ref):
        ysb, _ = _rms(sb_ref[...], gsb_ref[...])
        ysw, _ = _rms(sw_ref[...], gsw_ref[...])
        ysb = ysb.astype(_MXU)
        ysw = ysw.astype(_MXU)
        mg_ref[:, :SB_WIDTH] = ysb
        mg_ref[:, SB_WIDTH:] = ysw
        mix = _dot(ysb, w_ref[:SB_WIDTH, :]) + _dot(ysw, w_ref[SB_WIDTH:, :])
        pre1 = ALPHA * h0_ref[...] + mix
        pre_ref[...] = pre1
        xhat, _ = _ln_hat(pre1)
        h1_ref[...] = (xhat * g1_ref[...] + b1_ref[...]).astype(h1_ref.dtype)

    vec = _fixed(1, D_MODEL)
    return pl.pallas_call(
        body, name="mix_out", grid=(S // tm,),
        in_specs=[_rows(tm, SB_WIDTH), _rows(tm, SWA_WIDTH), _fixed(1, SB_WIDTH), _fixed(1, SWA_WIDTH),
                  _fixed(D_MODEL, D_MODEL), _rows(tm, D_MODEL), vec, vec],
        out_specs=[_rows(tm, D_MODEL), _rows(tm, D_MODEL), _rows(tm, D_MODEL)],
        out_shape=[_sds((S, D_MODEL), F32), _sds((S, D_MODEL), _MXU), _sds((S, D_MODEL), _MXU)],
        compiler_params=_cp(("parallel",)),
    )(sb, sw, gsb, gsw, w_out, h0, g1, b1)


def _sigmoid(x):
    return 1.0 / (1.0 + jnp.exp(-x))


def _ffn_up(h1b, wgu):
    S = h1b.shape[0]
    tm = min(S, 1024)

    def body(h_ref, wg_ref, wu_ref, a_ref, s1_ref, s2_ref):
        h1 = h_ref[...]
        gate = _dot(h1, wg_ref[0])
        up = _dot(h1, wu_ref[0])
        sg = _sigmoid(gate)
        silu = gate * sg
        a_ref[...] = (silu * up).astype(a_ref.dtype)
        s1_ref[...] = silu.astype(s1_ref.dtype)
        s2_ref[...] = (up * (sg * (1.0 + gate * (1.0 - sg)))).astype(s2_ref.dtype)

    chunk = pl.BlockSpec((tm, FF_CHUNK), lambda j, i: (i, j))
    return pl.pallas_call(
        body, name="ffn_up", grid=(2, S // tm),
        in_specs=[pl.BlockSpec((tm, D_MODEL), lambda j, i: (i, 0)),
                  pl.BlockSpec((1, D_MODEL, FF_CHUNK), lambda j, i: (j, 0, 0)),
                  pl.BlockSpec((1, D_MODEL, FF_CHUNK), lambda j, i: (j + 2, 0, 0))],
        out_specs=[chunk, chunk, chunk],
        out_shape=[_sds((S, D_FF), _MXU)] * 3,
        compiler_params=_cp(("arbitrary", "arbitrary"), vmem_mb=56),
    )(h1b, wgu, wgu)


def _ffn_down_loss(a, w_down, pre1, g1, b1, g2, b2, tgt):
    S = a.shape[0]
    tm = min(S, 512)

    def body(a_ref, w_ref, p_ref, g1_ref, b1_ref, g2_ref, b2_ref, t_ref, d_ref, db_ref, dg2_ref, db2_ref, err_ref):
        @pl.when(pl.program_id(0) == 0)
        def _():
            dg2_ref[...] = jnp.zeros_like(dg2_ref)
            db2_ref[...] = jnp.zeros_like(db2_ref)
            err_ref[...] = jnp.zeros_like(err_ref)

        xhat1, _ = _ln_hat(p_ref[...])
        h1 = xhat1 * g1_ref[...] + b1_ref[...]
        pre2 = ALPHA * h1 + _dot(a_ref[...], w_ref[...])
        xhat2, rstd2 = _ln_hat(pre2)
        err = xhat2 * g2_ref[...] + b2_ref[...] - t_ref[...]
        dh2 = err * (1.0 / D_MODEL)
        dp2 = _ln_bwd(xhat2, rstd2, dh2, g2_ref[...])
        d_ref[...] = dp2
        db_ref[...] = dp2.astype(db_ref.dtype)
        dg2_ref[...] += _colsum(dh2 * xhat2)
        db2_ref[...] += _colsum(dh2)
        err_ref[...] += _colsum(err * err)

    vec = _fixed(1, D_MODEL)
    return pl.pallas_call(
        body, name="ffn_down_loss", grid=(S // tm,),
        in_specs=[_rows(tm, D_FF), _fixed(D_FF, D_MODEL), _rows(tm, D_MODEL), vec, vec, vec, vec, _rows(tm, D_MODEL)],
        out_specs=[_rows(tm, D_MODEL), _rows(tm, D_MODEL), vec, vec, vec],
        out_shape=[_sds((S, D_MODEL), F32), _sds((S, D_MODEL), _MXU), _sds((1, D_MODEL), F32), _sds((1, D_MODEL), F32),
                   _sds((1, D_MODEL), F32)],
        compiler_params=_cp(("arbitrary",)),
    )(a, w_down, pre1, g1, b1, g2, b2, tgt)


def _ffn_down_bwd(dp2b, w_down, s1, s2, act):
    S = dp2b.shape[0]
    tm = min(S, 512)

    def body(d_ref, w_ref, s1_ref, s2_ref, a_ref, dg_ref, du_ref, gw_ref):
        @pl.when(pl.program_id(1) == 0)
        def _():
            gw_ref[...] = jnp.zeros_like(gw_ref)

        d = d_ref[...]
        da = _dot_nt(d, w_ref[...])
        gw_ref[...] += _dot_tn(a_ref[...], d)
        du_ref[...] = (da * s1_ref[...].astype(F32)).astype(du_ref.dtype)
        dg_ref[...] = (da * s2_ref[...].astype(F32)).astype(dg_ref.dtype)

    chunk = pl.BlockSpec((tm, FF_CHUNK), lambda j, i: (i, j))
    return pl.pallas_call(
        body, name="ffn_down_bwd", grid=(2, S // tm),
        in_specs=[pl.BlockSpec((tm, D_MODEL), lambda j, i: (i, 0)),
                  pl.BlockSpec((FF_CHUNK, D_MODEL), lambda j, i: (j, 0)), chunk, chunk, chunk],
        out_specs=[chunk, chunk, pl.BlockSpec((FF_CHUNK, D_MODEL), lambda j, i: (j, 0))],
        out_shape=[_sds((S, D_FF), _MXU), _sds((S, D_FF), _MXU), _sds((D_FF, D_MODEL), F32)],
        compiler_params=_cp(("arbitrary", "arbitrary")),
    )(dp2b, w_down, s1, s2, act)


def _ffn_up_bwd(dgate, dup, wgu, dp2, pre1, g1):
    S = dgate.shape[0]
    tm = min(S, 256)

    def body(dg_ref, du_ref, w_ref, d2_ref, p_ref, g_ref, d1_ref, d1b_ref, dg1_ref, db1_ref):
        @pl.when(pl.program_id(0) == 0)
        def _():
            dg1_ref[...] = jnp.zeros_like(dg1_ref)
            db1_ref[...] = jnp.zeros_like(db1_ref)

        dh1 = ALPHA * d2_ref[...]
        for j in range(2):
            cols = slice(j * FF_CHUNK, (j + 1) * FF_CHUNK)
            dh1 += _dot_nt(dg_ref[:, cols], w_ref[j])
            dh1 += _dot_nt(du_ref[:, cols], w_ref[j + 2])
        xhat, rstd = _ln_hat(p_ref[...])
        dp1 = _ln_bwd(xhat, rstd, dh1, g_ref[...])
        d1_ref[...] = dp1
        d1b_ref[...] = dp1.astype(d1b_ref.dtype)
        dg1_ref[...] += _colsum(dh1 * xhat)
        db1_ref[...] += _colsum(dh1)

    vec = _fixed(1, D_MODEL)
    return pl.pallas_call(
        body, name="ffn_up_bwd", grid=(S // tm,),
        in_specs=[_rows(tm, D_FF), _rows(tm, D_FF), _fixed(4, D_MODEL, FF_CHUNK), _rows(tm, D_MODEL),
                  _rows(tm, D_MODEL), vec],
        out_specs=[_rows(tm, D_MODEL), _rows(tm, D_MODEL), vec, vec],
        out_shape=[_sds((S, D_MODEL), F32), _sds((S, D_MODEL), _MXU), _sds((1, D_MODEL), F32), _sds((1, D_MODEL), F32)],
        compiler_params=_cp(("arbitrary",), vmem_mb=56),
    )(dgate, dup, wgu, dp2, pre1, g1)


def _rms_bwd(x, g, dy):
    n = x.shape[-1]
    r = lax.rsqrt(jnp.mean(x * x, axis=-1, keepdims=True) + RMS_EPS)
    u = dy * g
    dx = r * u - x * (r * r * r) * (jnp.sum(u * x, axis=-1, keepdims=True) * (1.0 / n))
    return dx, _colsum(dy * x * r)


def _mix_bwd(dp1b, w_out, sb, sw, gsb, gsw):
    S = sb.shape[0]
    tm = min(S, 512)

    def body(d_ref, w_ref, sb_ref, sw_ref, gsb_ref, gsw_ref, dsb_ref, dsw_ref, dgsb_ref, dgsw_ref):
        @pl.when(pl.program_id(0) == 0)
        def _():
            dgsb_ref[...] = jnp.zeros_like(dgsb_ref)
            dgsw_ref[...] = jnp.zeros_like(dgsw_ref)

        dm = _dot_nt(d_ref[...], w_ref[...])
        dsb, dgsb = _rms_bwd(sb_ref[...], gsb_ref[...], dm[:, :SB_WIDTH])
        dsw, dgsw = _rms_bwd(sw_ref[...], gsw_ref[...], dm[:, SB_WIDTH:])
        dsb_ref[...] = dsb.T.astype(dsb_ref.dtype).reshape(dsb_ref.shape)
        for hd in range(SWA_HEADS):
            dsw_ref[hd] = dsw[:, hd * HEAD_DIM:(hd + 1) * HEAD_DIM].astype(dsw_ref.dtype)
        dgsb_ref[...] += dgsb
        dgsw_ref[...] += dgsw

    return pl.pallas_call(
        body, name="mix_bwd", grid=(S // tm,),
        in_specs=[_rows(tm, D_MODEL), _fixed(D_MODEL, D_MODEL), _rows(tm, SB_WIDTH), _rows(tm, SWA_WIDTH),
                  _fixed(1, SB_WIDTH), _fixed(1, SWA_WIDTH)],
        out_specs=[pl.BlockSpec((SB_HEADS, HEAD_DIM, tm), lambda i: (0, 0, i)),
                   pl.BlockSpec((SWA_HEADS, tm, HEAD_DIM), lambda i: (0, i, 0)), _fixed(1, SB_WIDTH), _fixed(1, SWA_WIDTH)],
        out_shape=[_sds((SB_HEADS, HEAD_DIM, S), _MXU), _sds((SWA_HEADS, S, HEAD_DIM), _MXU), _sds((1, SB_WIDTH), F32),
                   _sds((1, SWA_WIDTH), F32)],
        compiler_params=_cp(("arbitrary",)),
    )(dp1b, w_out, sb, sw, gsb, gsw)


def _in_proj_bwd(dproj, w_in, dp1, x, g, parts):
    S = x.shape[0]
    tm = min(S, 512)
    nw = len(parts)
    ns = len(IN_SECTIONS)
    offs = np.cumsum((0,) + IN_SECTIONS)
    s_ins, s_outs, s_sems = _scatter_io(parts)

    def body(*refs):
        dpj_refs = refs[:ns]
        w_ref, d1_ref, x_ref, g_ref = refs[ns:ns + 4]
        rest = refs[ns + 4:]
        gx_ref, dg_ref, db_ref = rest[nw:nw + 3]
        scatter = _Scatter(rest[:nw], rest[nw + 3:2 * nw + 3], *rest[2 * nw + 3:])

        @pl.when(pl.program_id(0) == 0)
        def _():
            scatter.start()
            dg_ref[...] = jnp.zeros_like(dg_ref)
            db_ref[...] = jnp.zeros_like(db_ref)

        dh0 = ALPHA * d1_ref[...]
        for k in range(ns):
            dh0 += _dot_nt(dpj_refs[k][...], w_ref[:, offs[k]:offs[k + 1]])
        xhat, rstd = _ln_hat(x_ref[...])
        gx_ref[...] = _ln_bwd(xhat, rstd, dh0, g_ref[...])
        dg_ref[...] += _colsum(dh0 * xhat)
        db_ref[...] += _colsum(dh0)

        @pl.when(pl.program_id(0) == pl.num_programs(0) - 1)
        def _():
            scatter.finish()

    vec = _fixed(1, D_MODEL)
    any_spec = pl.BlockSpec(memory_space=pl.ANY)
    res = pl.pallas_call(
        body, name="in_proj_bwd", grid=(S // tm,),
        in_specs=[_rows(tm, n) for n in IN_SECTIONS]
                 + [_fixed(D_MODEL, IN_COLS), _rows(tm, D_MODEL), _rows(tm, D_MODEL), vec] + [any_spec] * nw,
        out_specs=[_rows(tm, D_MODEL), vec, vec] + [any_spec] * nw,
        out_shape=[_sds((S, D_MODEL), F32), _sds((1, D_MODEL), F32), _sds((1, D_MODEL), F32)] + s_outs,
        scratch_shapes=s_sems,
        compiler_params=_cp(("arbitrary",)),
    )(*dproj, w_in, dp1, x, g, *s_ins)
    return res[0], res[1], res[2], list(res[3:])


def _matmul_tn(a, b, name, tk, tn):
    T, K = a.shape
    N = b.shape[1]
    tt = min(T, 1024)

    def body(a_ref, b_ref, o_ref):
        @pl.when(pl.program_id(2) == 0)
        def _():
            o_ref[...] = jnp.zeros_like(o_ref)

        o_ref[...] += _dot_tn(a_ref[...], b_ref[...])

    return pl.pallas_call(
        body, name=name, grid=(K // tk, N // tn, T // tt),
        in_specs=[pl.BlockSpec((tt, tk), lambda k, n, t: (t, k)), pl.BlockSpec((tt, tn), lambda k, n, t: (t, n))],
        out_specs=pl.BlockSpec((tk, tn), lambda k, n, t: (k, n)),
        out_shape=_sds((K, N), F32),
        compiler_params=_cp(("parallel", "parallel", "arbitrary")),
    )(a, b)


def _place():
    x, y, c = lax.axis_index("x"), lax.axis_index("y"), lax.axis_index("c")
    chips = [(1 - x, y), (x, 1 - y), (1 - x, 1 - y)]
    return x, y, c, chips


class _Gather:
    def __init__(self, in_refs, out_refs, send_sems, recv_sems):
        self.in_refs, self.out_refs, self.send_sems, self.recv_sems = in_refs, out_refs, send_sems, recv_sems
        self.x, self.y, self.c, self.chips = _place()

    def _copy(self, w, k, chip, hc, to, src=None):
        part = self.out_refs[w].at[2 * chip[0] + chip[1], hc]
        return pltpu.make_async_remote_copy(
            src_ref=part if src is None else src, dst_ref=part, send_sem=self.send_sems.at[w, k],
            recv_sem=self.recv_sems.at[w, k], device_id=to, device_id_type=MESH)

    def _first(self):
        x, y, c = self.x, self.y, self.c
        return [self._copy(w, j, (x, y), c, (*chip, c), src=self.in_refs[w].at[c])
                for w in range(len(self.in_refs)) for j, chip in enumerate(self.chips)]

    def start(self):
        for cp in self._first():
            cp.start()

    def _passed(self):
        sibling = (self.x, self.y, 1 - self.c)
        return [self._copy(w, 3 + j, chip, self.c, sibling)
                for w in range(len(self.in_refs)) for j, chip in enumerate(self.chips)]

    def forward(self):
        me = (self.x, self.y, self.c)
        passed = self._passed()
        for w in range(len(self.in_refs)):
            for j, chip in enumerate(self.chips):
                self._copy(w, j, chip, self.c, me).wait_recv()
                passed[3 * w + j].start()

    def finish(self):
        me = (self.x, self.y, self.c)
        for w in range(len(self.in_refs)):
            for j, chip in enumerate(self.chips):
                self._copy(w, 3 + j, chip, 1 - self.c, me).wait_recv()
        for cp in self._first() + self._passed():
            cp.wait_send()


def _gather_io(shards):
    halves = [(s.shape[0] // 2, s.shape[1]) for s in shards]
    ins = [s.reshape(2, h, cols) for s, (h, cols) in zip(shards, halves)]
    outs = [_sds((N_CHIPS, 2, h, cols), s.dtype) for s, (h, cols) in zip(shards, halves)]
    sems = [pltpu.SemaphoreType.DMA((len(shards), 6)), pltpu.SemaphoreType.DMA((len(shards), 6))]
    return ins, outs, sems


def _gather_assemble(outs, shards):
    me = 2 * lax.axis_index("x") + lax.axis_index("y")
    return [lax.dynamic_update_slice_in_dim(o.reshape((N_CHIPS,) + s.shape), s[None], me, axis=0)
            for o, s in zip(outs, shards)]


class _Scatter:
    def __init__(self, p_refs, out_refs, send_sems, recv_sems):
        self.p_refs, self.out_refs, self.send_sems, self.recv_sems = p_refs, out_refs, send_sems, recv_sems
        self.x, self.y, self.c, self.chips = _place()
        self.me = 2 * self.x + self.y

    def _copy(self, w, j, chip, src_chip, dst_chip):
        return pltpu.make_async_remote_copy(
            src_ref=self.p_refs[w].at[src_chip], dst_ref=self.out_refs[w].at[dst_chip], send_sem=self.send_sems.at[w, j],
            recv_sem=self.recv_sems.at[w, j], device_id=(*chip, self.c), device_id_type=MESH)

    def _sends(self):
        return [self._copy(w, j, chip, 2 * chip[0] + chip[1], self.me)
                for w in range(len(self.p_refs)) for j, chip in enumerate(self.chips)]

    def start(self):
        for cp in self._sends():
            cp.start()

    def finish(self):
        for w in range(len(self.p_refs)):
            for j, chip in enumerate(self.chips):
                self._copy(w, j, chip, self.me, 2 * chip[0] + chip[1]).wait_recv()
        for cp in self._sends():
            cp.wait_send()


def _scatter_io(parts):
    sems = [pltpu.SemaphoreType.DMA((len(parts), 3)), pltpu.SemaphoreType.DMA((len(parts), 3))]
    return list(parts), [_sds(p.shape, p.dtype) for p in parts], sems


class _Swap:
    def __init__(self, g_refs, out_refs, send_sems, recv_sems):
        x, y, c, _ = _place()
        self.copies = []
        for w in range(len(g_refs)):
            half = out_refs[w].shape[1]
            theirs = g_refs[w].at[:, pl.ds(pl.multiple_of((1 - c) * half, 8), half), :]
            self.copies.append(pltpu.make_async_remote_copy(
                src_ref=theirs, dst_ref=out_refs[w], send_sem=send_sems.at[w], recv_sem=recv_sems.at[w],
                device_id=(x, y, 1 - c), device_id_type=MESH))

    def start(self):
        for cp in self.copies:
            cp.start()

    def finish(self):
        for cp in self.copies:
            cp.wait()


def _swap_io(grads):
    outs = [_sds((g.shape[0], g.shape[1] // 2, g.shape[2]), g.dtype) for g in grads]
    return list(grads), outs, [pltpu.SemaphoreType.DMA((len(grads),)), pltpu.SemaphoreType.DMA((len(grads),))]


def _matmul_tn_pair(a, b0, b1, name):
    T, K = a.shape
    tt = min(T, 1024)

    def body(a_ref, b0_ref, b1_ref, o_ref):
        n = pl.program_id(0)

        @pl.when(pl.program_id(1) == 0)
        def _():
            o_ref[...] = jnp.zeros_like(o_ref)

        @pl.when(n < 2)
        def _():
            o_ref[0] += _dot_tn(a_ref[...], b0_ref[...])

        @pl.when(n >= 2)
        def _():
            o_ref[0] += _dot_tn(a_ref[...], b1_ref[...])

    return pl.pallas_call(
        body, name=name, grid=(4, T // tt),
        in_specs=[pl.BlockSpec((tt, K), lambda n, t: (t, 0)),
                  pl.BlockSpec((tt, FF_CHUNK), lambda n, t: (t, jnp.minimum(n, 1))),
                  pl.BlockSpec((tt, FF_CHUNK), lambda n, t: (t, jnp.maximum(n - 2, 0)))],
        out_specs=pl.BlockSpec((1, K, FF_CHUNK), lambda n, t: (n, 0, 0)),
        out_shape=_sds((4, K, FF_CHUNK), F32),
        compiler_params=_cp(("parallel", "arbitrary")),
    )(a, b0, b1)


def _sb_logs(zt, causal):
    e = jnp.exp(-jnp.abs(zt))
    lb = jnp.minimum(zt, 0.0) - jnp.log(1.0 + e)
    l1m = lb - zt
    if causal is not None:
        l1m = jnp.where(causal, l1m, 0.0)
    return lb, l1m


def _sb_weights(lb, suf, causal):
    a = jnp.exp(lb + suf)
    if causal is not None:
        a = jnp.where(causal, a, 0.0)
    return a


def _tri_masks(t):
    r = lax.broadcasted_iota(jnp.int32, (t, t), 0)
    c = lax.broadcasted_iota(jnp.int32, (t, t), 1)
    return r, c


def _sb_fwd(qT, kb, vTb, shards):
    Hh, _, S = qT.shape
    nk, T = kb.shape[1], kb.shape[2]
    nq = S // T
    G = SB_GROUP_FWD
    nw = len(shards)
    g_ins, g_outs, g_sems = _gather_io(shards)
    forward_step = max(nq - 1 - SB_FORWARD_LEAD, 0)

    def body(qT_ref, k_ref, vT_ref, *rest):
        o_ref, rs_ref, first_ref = rest[nw:nw + 3]
        gather = _Gather(rest[:nw], rest[nw + 3:2 * nw + 3], *rest[2 * nw + 3:])
        i = pl.program_id(1)
        first_step = jnp.logical_and(pl.program_id(0) == 0, i == 0)
        last_step = jnp.logical_and(pl.program_id(0) == pl.num_programs(0) - 1, i == pl.num_programs(1) - 1)

        @pl.when(first_step)
        def _():
            gather.start()

        qts = [(qT_ref[g].astype(F32) * SCALE).astype(_MXU) for g in range(G)]
        r, c = _tri_masks(T)
        upper = (c > r).astype(_MXU)
        causal = r < c

        def blk(j, carry, mask):
            hs = range(G)
            for g in hs:
                rs_ref[g, 0, j] = jnp.broadcast_to(carry[g][0], (8, T))
            zs = [_dot(k_ref[g, j], qts[g]) for g in hs]
            lbs, l1ms = zip(*[_sb_logs(zs[g], mask) for g in hs])
            splits = [_split2(l1ms[g]) for g in hs]
            cums = [_dot(upper, splits[g][0]) + _dot(upper, splits[g][1]) for g in hs]
            avs = [_sb_weights(lbs[g], carry[g][0] + cums[g], mask).astype(_MXU) for g in hs]
            accs = [carry[g][1] + _dot(vT_ref[g, j], avs[g]) for g in hs]
            return tuple((carry[g][0] + _colsum(l1ms[g]), accs[g]) for g in hs)

        def go_on(j, carry):
            top = carry[0][0]
            for g in range(1, G):
                top = jnp.maximum(top, carry[g][0])
            return jnp.logical_and(j >= 0, jnp.max(top) >= SB_DEAD)

        init = tuple((jnp.zeros((1, T), F32), jnp.zeros((HEAD_DIM, T), F32)) for _ in range(G))
        carry = blk(i, init, causal)
        j, carry = lax.while_loop(lambda st: go_on(*st), lambda st: (st[0] - 1, blk(st[0], st[1], None)),
                                  (i - 1, carry))

        first_ref[...] = jnp.broadcast_to((j + 1).astype(F32), first_ref.shape)

        o_ref[...] = jnp.concatenate([carry[g][1] for g in range(G)], axis=0).T

        @pl.when(jnp.logical_and(pl.program_id(0) == pl.num_programs(0) - 1, i == forward_step))
        def _():
            gather.forward()

        @pl.when(last_step)
        def _():
            gather.finish()

    any_spec = pl.BlockSpec(memory_space=pl.ANY)
    res = pl.pallas_call(
        body, name="sb_fwd", grid=(Hh // G, nq),
        in_specs=[pl.BlockSpec((G, HEAD_DIM, T), lambda h, i: (h, 0, i)),
                  pl.BlockSpec((G, nk, T, HEAD_DIM), lambda h, i: (h, 0, 0, 0), pipeline_mode=pl.Buffered(1)),
                  pl.BlockSpec((G, nk, HEAD_DIM, T), lambda h, i: (h, 0, 0, 0), pipeline_mode=pl.Buffered(1))]
                 + [any_spec] * nw,
        out_specs=[pl.BlockSpec((T, G * HEAD_DIM), lambda h, i: (i, h)),
                   pl.BlockSpec((G, 1, nk, 8, T), lambda h, i: (h, i, 0, 0, 0)),
                   pl.BlockSpec((1, 1, 8, 128), lambda h, i: (h, i, 0, 0))] + [any_spec] * nw,
        out_shape=[_sds((S, Hh * HEAD_DIM), F32), _sds((Hh, nq, nk, 8, T), F32), _sds((Hh // G, nq, 8, 128), F32)]
                  + g_outs,
        scratch_shapes=g_sems,
        compiler_params=_cp(("arbitrary", "arbitrary")),
    )(qT, kb, vTb, *g_ins)
    return res[0], res[1], res[2], _gather_assemble(res[3:], shards)


def _sb_bwd(qT, kb, kTb, vb, doT, rsave, first, parts):
    Hh, _, S = qT.shape
    nk, T = kb.shape[1], kb.shape[2]
    nq = S // T
    G = SB_GROUP_BWD
    nw = len(parts)
    s_ins, s_outs, s_sems = _scatter_io(parts)

    def body(qT_ref, k_ref, kT_ref, v_ref, doT_ref, rs_ref, first_ref, *rest):
        dq_ref, dk_out_ref, dv_out_ref = rest[nw:nw + 3]
        dk_ref, dv_ref = rest[2 * nw + 3:2 * nw + 5]
        scatter = _Scatter(rest[:nw], rest[nw + 3:2 * nw + 3], *rest[2 * nw + 5:])
        i = pl.program_id(1)
        first_step = jnp.logical_and(pl.program_id(0) == 0, i == 0)
        last_step = jnp.logical_and(pl.program_id(0) == pl.num_programs(0) - 1, i == pl.num_programs(1) - 1)

        @pl.when(first_step)
        def _():
            scatter.start()

        @pl.when(i == 0)
        def _():
            dk_ref[...] = jnp.zeros_like(dk_ref)
            dv_ref[...] = jnp.zeros_like(dv_ref)

        qts = [(qT_ref[g].astype(F32) * SCALE).astype(_MXU) for g in range(G)]
        douts = [doT_ref[g] for g in range(G)]
        r, c = _tri_masks(T)
        upper = (c > r).astype(_MXU)
        lower = (c < r).astype(_MXU)
        causal = r < c

        def blk(j, carry, mask):
            hs = range(G)
            zs = [_dot(k_ref[g, j], qts[g]) for g in hs]
            das = [_dot(v_ref[g, j], douts[g]) for g in hs]
            lbs, l1ms = zip(*[_sb_logs(zs[g], mask) for g in hs])
            splits = [_split2(l1ms[g]) for g in hs]
            cums = [_dot(upper, splits[g][0]) + _dot(upper, splits[g][1]) for g in hs]
            avs = [_sb_weights(lbs[g], rs_ref[g, 0, j][0:1, :] + cums[g], mask) for g in hs]
            ets = [das[g] * avs[g] for g in hs]
            esplits = [_split2(ets[g]) for g in hs]
            ecums = [_dot(lower, esplits[g][0]) + _dot(lower, esplits[g][1]) for g in hs]
            dzs = []
            for g in hs:
                sig = jnp.exp(lbs[g])
                dz = ets[g] * (1.0 - sig) - (carry[g][0] + ecums[g]) * sig
                if mask is not None:
                    dz = jnp.where(mask, dz, 0.0)
                dzs.append(dz.astype(_MXU))
            dqs = [carry[g][1] + _dot(kT_ref[g, j], dzs[g]) for g in hs]
            for g in hs:
                dk_ref[j, g * HEAD_DIM:(g + 1) * HEAD_DIM, :] += _dot_nt(qts[g], dzs[g])
            for g in hs:
                dv_ref[j, g * HEAD_DIM:(g + 1) * HEAD_DIM, :] += _dot_nt(douts[g], avs[g].astype(_MXU))
            return tuple((carry[g][0] + _colsum(ets[g]), dqs[g]) for g in hs)

        first = jnp.clip(jnp.max(first_ref[0, 0][0:1, 0:1]).astype(jnp.int32), 0, i)
        carry = tuple((jnp.zeros((1, T), F32), jnp.zeros((HEAD_DIM, T), F32)) for _ in range(G))
        carry = lax.fori_loop(first, i, lambda s, cr: blk(s, cr, None), carry)
        carry = blk(i, carry, causal)
        dq_ref[...] = (jnp.concatenate([carry[g][1] for g in range(G)], axis=0) * SCALE).T.astype(dq_ref.dtype)

        @pl.when(i == pl.num_programs(1) - 1)
        def _():
            def flush(j, _):
                rows = pl.ds(pl.multiple_of(j * T, T), T)
                dk_out_ref[rows, :] = dk_ref[j].T.astype(dk_out_ref.dtype)
                dv_out_ref[rows, :] = dv_ref[j].T.astype(dv_out_ref.dtype)
                return 0
            lax.fori_loop(0, nk, flush, 0)

        @pl.when(last_step)
        def _():
            scatter.finish()

    colblk = pl.BlockSpec((G, HEAD_DIM, T), lambda h, i: (h, 0, i))
    once = pl.Buffered(1)
    kblk = pl.BlockSpec((G, nk, T, HEAD_DIM), lambda h, i: (h, 0, 0, 0), pipeline_mode=once)
    kTblk = pl.BlockSpec((G, nk, HEAD_DIM, T), lambda h, i: (h, 0, 0, 0), pipeline_mode=once)
    any_spec = pl.BlockSpec(memory_space=pl.ANY)
    res = pl.pallas_call(
        body, name="sb_bwd", grid=(Hh // G, nq),
        in_specs=[colblk, kblk, kTblk, kblk, colblk,
                  pl.BlockSpec((G, 1, nk, 8, T), lambda h, i: (h, i, 0, 0, 0)),
                  pl.BlockSpec((1, 1, 8, 128), lambda h, i: ((h * G) // SB_GROUP_FWD, i, 0, 0))] + [any_spec] * nw,
        out_specs=[pl.BlockSpec((T, G * HEAD_DIM), lambda h, i: (i, h)),
                   pl.BlockSpec((S, G * HEAD_DIM), lambda h, i: (0, h), pipeline_mode=once),
                   pl.BlockSpec((S, G * HEAD_DIM), lambda h, i: (0, h), pipeline_mode=once)] + [any_spec] * nw,
        out_shape=[_sds((S, Hh * HEAD_DIM), _MXU)] * 3 + s_outs,
        scratch_shapes=[pltpu.VMEM((nk, G * HEAD_DIM, T), F32), pltpu.VMEM((nk, G * HEAD_DIM, T), F32)] + s_sems,
        compiler_params=_cp(("arbitrary", "arbitrary"), vmem_mb=60),
    )(qT, kb, kTb, vb, doT, rsave, first, *s_ins)
    return res[0], res[1], res[2], list(res[3:])


def _bucket_table():
    qi = np.arange(BLOCK)[:, None]
    cj = np.arange(2 * BLOCK)[None, :]
    dist = qi + BLOCK - cj
    exact = REL_BUCKETS // 2
    d = np.maximum(dist, 0)
    d_f = np.maximum(d, 1).astype(np.float32)
    large = exact + (np.log(d_f / np.float32(exact)) / np.float32(math.log(REL_MAX_DIST / exact))
                     * np.float32(REL_BUCKETS - exact)).astype(np.int32)
    large = np.minimum(large, REL_BUCKETS - 1)
    return np.where(d < exact, d, large).astype(np.int32)


def _swa_bias(rel_bias, bucket):
    def body(rb_ref, bk_ref, o_ref):
        bk = bk_ref[...]
        for h in range(SWA_HEADS):
            t = jnp.zeros((2 * BLOCK, BLOCK), F32)
            for b in range(REL_BUCKETS):
                t = jnp.where(bk == b, rb_ref[b, h], t)
            o_ref[h] = t

    return pl.pallas_call(
        body, name="swa_bias",
        in_specs=[pl.BlockSpec(memory_space=pltpu.SMEM), pl.BlockSpec(memory_space=pltpu.VMEM)],
        out_specs=pl.BlockSpec(memory_space=pltpu.VMEM),
        out_shape=_sds((SWA_HEADS, 2 * BLOCK, BLOCK), F32),
    )(rel_bias, bucket)


def _swa_logits(q, kp, kc):
    qs = (q.astype(F32) * SCALE).astype(_MXU)
    return qs, _dot_nt(kp, qs), _dot_nt(kc, qs)


def _swa_softmax(lp, lc, bias, sink, live_prev):
    r, c = _tri_masks(BLOCK)
    in_window = r > c if live_prev is None else jnp.logical_and(r > c, live_prev)
    lp = jnp.where(in_window, lp + bias[:BLOCK, :], -jnp.inf)
    lc = jnp.where(r <= c, lc + bias[BLOCK:, :], -jnp.inf)
    m = jnp.maximum(jnp.maximum(jnp.max(lp, axis=0, keepdims=True), jnp.max(lc, axis=0, keepdims=True)), sink)
    pp = jnp.exp(lp - m)
    pc = jnp.exp(lc - m)
    ps = jnp.exp(sink - m)
    denom = _colsum(pp) + _colsum(pc) + ps
    return pp / denom, pc / denom, ps / denom


def _swa_sub(nb):
    return min(SWA_SUB, nb)


def _swa_keys(b, prev_ref, cur_ref, i):
    cur = cur_ref[0, b * BLOCK:(b + 1) * BLOCK, :]
    if b == 0:
        return prev_ref[0], cur, i > 0
    return cur_ref[0, (b - 1) * BLOCK:b * BLOCK, :], cur, None


def _swa_keys_t(b, prev_ref, cur_ref):
    cur = cur_ref[0, :, b * BLOCK:(b + 1) * BLOCK]
    return (prev_ref[0] if b == 0 else cur_ref[0, :, (b - 1) * BLOCK:b * BLOCK]), cur


SWA_PAIR = 4


def _swa_fwd(q, k, vT, bias, sink):
    S = q.shape[1]
    nb = S // BLOCK
    ns = _swa_sub(nb)
    R = ns * BLOCK
    P = SWA_PAIR

    def body(q_ref, kp_ref, kc_ref, vp_ref, vc_ref, bias_ref, sink_ref, o_ref):
        i = pl.program_id(1)
        units = [(hh, b) for hh in range(P) for b in range(ns)]
        keys = [_swa_keys(b, kp_ref, kc_ref, i) for b in range(ns)]
        vals = [_swa_keys_t(b, vp_ref, vc_ref) for b in range(ns)]
        logits = {u: _swa_logits(q_ref[u[0], u[1] * BLOCK:(u[1] + 1) * BLOCK, :], keys[u[1]][0], keys[u[1]][1])
                  for u in units}
        ws = {u: _swa_softmax(logits[u][1], logits[u][2], bias_ref[u[0]], sink_ref[u[0]][:, :1], keys[u[1]][2])
              for u in units}
        outs = {u: _dot(vals[u[1]][0], ws[u][0].astype(_MXU)) + _dot(vals[u[1]][1], ws[u][1].astype(_MXU))
                for u in units}
        for b in range(ns):
            o_ref[b * BLOCK:(b + 1) * BLOCK, :] = jnp.concatenate([outs[(hh, b)] for hh in range(P)], axis=0).T

    kvh = lambda p: (p * P) // SWA_GROUP
    prev = pl.BlockSpec((1, BLOCK, HEAD_DIM), lambda p, i: (kvh(p), jnp.maximum(i * ns - 1, 0), 0))
    cur = pl.BlockSpec((1, R, HEAD_DIM), lambda p, i: (kvh(p), i, 0))
    prev_t = pl.BlockSpec((1, HEAD_DIM, BLOCK), lambda p, i: (kvh(p), 0, jnp.maximum(i * ns - 1, 0)))
    cur_t = pl.BlockSpec((1, HEAD_DIM, R), lambda p, i: (kvh(p), 0, i))
    return pl.pallas_call(
        body, name="swa_fwd", grid=(SWA_HEADS // P, nb // ns),
        in_specs=[pl.BlockSpec((P, R, HEAD_DIM), lambda p, i: (p, i, 0)), prev, cur, prev_t, cur_t,
                  pl.BlockSpec((P, 2 * BLOCK, BLOCK), lambda p, i: (p, 0, 0)),
                  pl.BlockSpec((P, 1, BLOCK), lambda p, i: (p, 0, 0))],
        out_specs=pl.BlockSpec((R, P * HEAD_DIM), lambda p, i: (i, p)),
        out_shape=_sds((S, SWA_HEADS * HEAD_DIM), F32),
        compiler_params=_cp(("parallel", "parallel")),
    )(q, k, k, vT, vT, bias, sink)


def _swa_bwd(q, k, kT, v, bias, sink, do, grads):
    S = q.shape[1]
    nb = S // BLOCK
    ns = _swa_sub(nb)
    R = ns * BLOCK
    P = SWA_PAIR
    nw = len(grads)
    x_ins, x_outs, x_sems = _swap_io(grads)

    def body(q_ref, kp_ref, kc_ref, ktp_ref, ktc_ref, vp_ref, vc_ref, bias_ref, sink_ref, do_ref, *rest):
        dq_ref, dk_ref, dv_ref, dbias_ref, dsink_ref = rest[nw:nw + 5]
        swap = _Swap(rest[:nw], rest[nw + 5:2 * nw + 5], *rest[2 * nw + 5:])
        g = pl.program_id(1)
        i = pl.program_id(2)
        first_step = jnp.logical_and(pl.program_id(0) == 0, jnp.logical_and(g == 0, i == 0))
        last_step = jnp.logical_and(pl.program_id(0) == pl.num_programs(0) - 1,
                                    jnp.logical_and(g == pl.num_programs(1) - 1, i == pl.num_programs(2) - 1))

        @pl.when(first_step)
        def _():
            swap.start()

        @pl.when(jnp.logical_and(g == 0, i == 0))
        def _():
            dk_ref[...] = jnp.zeros_like(dk_ref)
            dv_ref[...] = jnp.zeros_like(dv_ref)

        @pl.when(i == 0)
        def _():
            dbias_ref[...] = jnp.zeros_like(dbias_ref)
            dsink_ref[...] = jnp.zeros_like(dsink_ref)

        subs = range(ns)
        units = [(hh, b) for hh in range(P) for b in subs]
        rows = [slice(b * BLOCK, (b + 1) * BLOCK) for b in subs]
        keys = [_swa_keys(b, kp_ref, kc_ref, i) for b in subs]
        keys_t = [_swa_keys_t(b, ktp_ref, ktc_ref) for b in subs]
        vals = [_swa_keys(b, vp_ref, vc_ref, i) for b in subs]
        douts = {u: do_ref[u[0], rows[u[1]], :] for u in units}
        logits = {u: _swa_logits(q_ref[u[0], rows[u[1]], :], keys[u[1]][0], keys[u[1]][1]) for u in units}
        dws = {u: (_dot_nt(vals[u[1]][0], douts[u]), _dot_nt(vals[u[1]][1], douts[u])) for u in units}
        wts, dls = {}, {}
        for hh in range(P):
            dbp = jnp.zeros((BLOCK, BLOCK), F32)
            dbc = jnp.zeros((BLOCK, BLOCK), F32)
            dsk = jnp.zeros((1, BLOCK), F32)
            for b in subs:
                u = (hh, b)
                wp, wc, ws = _swa_softmax(logits[u][1], logits[u][2], bias_ref[hh], sink_ref[hh][:, :1], keys[b][2])
                dwp, dwc = dws[u]
                delta = _colsum(wp * dwp) + _colsum(wc * dwc)
                dlp = wp * (dwp - delta)
                dlc = wc * (dwc - delta)
                dbp += dlp
                dbc += dlc
                dsk -= ws * delta
                wts[u] = (wp.astype(_MXU), wc.astype(_MXU))
                dls[u] = (dlp.astype(_MXU), dlc.astype(_MXU))
            dbias_ref[hh, :BLOCK, :] += dbp
            dbias_ref[hh, BLOCK:, :] += dbc
            dsink_ref[hh] += jnp.broadcast_to(dsk, (8, BLOCK))
        dqs = {u: (_dot(keys_t[u[1]][0], dls[u][0]) + _dot(keys_t[u[1]][1], dls[u][1])) * SCALE for u in units}
        for b in subs:
            dq_ref[rows[b], :] = jnp.concatenate([dqs[(hh, b)] for hh in range(P)], axis=0).T.astype(dq_ref.dtype)
        dk_cur = [sum(_dot(dls[(hh, b)][1], logits[(hh, b)][0]) for hh in range(P)) for b in subs]
        dv_cur = [sum(_dot(wts[(hh, b)][1], douts[(hh, b)]) for hh in range(P)) for b in subs]
        dk_prev = [sum(_dot(dls[(hh, b)][0], logits[(hh, b)][0]) for hh in range(P)) for b in subs]
        dv_prev = [sum(_dot(wts[(hh, b)][0], douts[(hh, b)]) for hh in range(P)) for b in subs]
        for b in subs:
            last = b + 1 == ns
            dk_ref[0, i * ns + b] += dk_cur[b] if last else dk_cur[b] + dk_prev[b + 1]
            dv_ref[0, i * ns + b] += dv_cur[b] if last else dv_cur[b] + dv_prev[b + 1]

        @pl.when(i > 0)
        def _():
            dk_ref[0, i * ns - 1] += dk_prev[0]
            dv_ref[0, i * ns - 1] += dv_prev[0]

        @pl.when(last_step)
        def _():
            swap.finish()

    G2 = SWA_GROUP // P
    hp = lambda kv, g, i: kv * G2 + g
    prev = pl.BlockSpec((1, BLOCK, HEAD_DIM), lambda kv, g, i: (kv, jnp.maximum(i * ns - 1, 0), 0))
    cur = pl.BlockSpec((1, R, HEAD_DIM), lambda kv, g, i: (kv, i, 0))
    prev_t = pl.BlockSpec((1, HEAD_DIM, BLOCK), lambda kv, g, i: (kv, 0, jnp.maximum(i * ns - 1, 0)))
    cur_t = pl.BlockSpec((1, HEAD_DIM, R), lambda kv, g, i: (kv, 0, i))
    qblk = pl.BlockSpec((P, R, HEAD_DIM), lambda kv, g, i: (hp(kv, g, i), i, 0))
    kvacc = pl.BlockSpec((1, nb, BLOCK, HEAD_DIM), lambda kv, g, i: (kv, 0, 0, 0))
    any_spec = pl.BlockSpec(memory_space=pl.ANY)
    res = pl.pallas_call(
        body, name="swa_bwd", grid=(SWA_KV_HEADS, G2, nb // ns),
        in_specs=[qblk, prev, cur, prev_t, cur_t, prev, cur,
                  pl.BlockSpec((P, 2 * BLOCK, BLOCK), lambda kv, g, i: (hp(kv, g, i), 0, 0)),
                  pl.BlockSpec((P, 1, BLOCK), lambda kv, g, i: (hp(kv, g, i), 0, 0)), qblk] + [any_spec] * nw,
        out_specs=[pl.BlockSpec((R, P * HEAD_DIM), lambda kv, g, i: (i, hp(kv, g, i))), kvacc, kvacc,
                   pl.BlockSpec((P, 2 * BLOCK, BLOCK), lambda kv, g, i: (hp(kv, g, i), 0, 0)),
                   pl.BlockSpec((P, 8, BLOCK), lambda kv, g, i: (hp(kv, g, i), 0, 0))] + [any_spec] * nw,
        out_shape=[_sds((S, SWA_HEADS * HEAD_DIM), _MXU), _sds((SWA_KV_HEADS, nb, BLOCK, HEAD_DIM), F32),
                   _sds((SWA_KV_HEADS, nb, BLOCK, HEAD_DIM), F32), _sds((SWA_HEADS, 2 * BLOCK, BLOCK), F32),
                   _sds((SWA_HEADS, 8, BLOCK), F32)] + x_outs,
        scratch_shapes=x_sems,
        compiler_params=_cp(("arbitrary", "arbitrary", "arbitrary")),
    )(q, k, k, kT, kT, v, v, bias, sink, do, *x_ins)
    return res[0], res[1], res[2], res[3], res[4], list(res[5:])


def _swa_small_grads(dbias, dsink, bucket):
    rows = REL_BUCKETS + 8

    def total(x):
        return jnp.sum(jnp.sum(x, axis=1, keepdims=True), axis=0, keepdims=True)

    def body(db_ref, ds_ref, bk_ref, o_ref):
        bk = bk_ref[...]
        r = lax.broadcasted_iota(jnp.int32, (rows, BLOCK), 0)
        c = lax.broadcasted_iota(jnp.int32, (rows, BLOCK), 1)
        out = jnp.zeros((rows, BLOCK), F32)
        for h in range(SWA_HEADS):
            db = db_ref[h]
            for b in range(REL_BUCKETS):
                s = total(jnp.where(bk == b, db, 0.0))
                out = jnp.where(jnp.logical_and(r == b, c == h), s, out)
            s = jnp.sum(ds_ref[h][0:1, :], axis=1, keepdims=True)
            out = jnp.where(jnp.logical_and(r == REL_BUCKETS, c == h), s, out)
        o_ref[...] = out

    vm = pl.BlockSpec(memory_space=pltpu.VMEM)
    return pl.pallas_call(body, name="swa_small_grads", in_specs=[vm, vm, vm], out_specs=vm,
                          out_shape=_sds((rows, BLOCK), F32))(dbias, dsink, bucket)


def _tile_rows(n):
    for t in (512, 352, 256, 176, 128, 64, 32, 16, 8):
        if n % t == 0:
            return t
    return n


def _cast_rows(x, dtype, name):
    R, C = x.shape
    tr = _tile_rows(R)

    def body(x_ref, o_ref):
        o_ref[...] = x_ref[...].astype(o_ref.dtype)

    return pl.pallas_call(body, name=name, grid=(R // tr,), in_specs=[_rows(tr, C)], out_specs=_rows(tr, C),
                          out_shape=_sds((R, C), dtype), compiler_params=_cp(("parallel",)))(x)


def _pair_sum(g, recv, c, name):
    n, half, C = recv.shape
    tr = _tile_rows(half)

    def body(c_ref, a_ref, b_ref, o_ref):
        o_ref[...] = (a_ref[0] + b_ref[...]).astype(o_ref.dtype)

    return pl.pallas_call(
        body, name=name,
        grid_spec=pltpu.PrefetchScalarGridSpec(
            num_scalar_prefetch=1, grid=(n, half // tr),
            in_specs=[pl.BlockSpec((1, 1, tr, C), lambda j, i, c_ref: (j, c_ref[0], i, 0)),
                      pl.BlockSpec((1, tr, C), lambda j, i, c_ref: (j, i, 0))],
            out_specs=pl.BlockSpec((1, tr, C), lambda j, i, c_ref: (j, i, 0))),
        out_shape=_sds((n, half, C), _MXU),
        compiler_params=_cp(("parallel", "parallel")))(c.reshape(1), g.reshape(n, 2, half, C), recv)


def _chip_sum(own, recv, me, name):
    n, R, C = recv.shape
    tr = _tile_rows(R)

    def body(me_ref, own_ref, recv_ref, o_ref):
        acc = None
        for j in range(n):
            term = jnp.where(me_ref[0] == j, own_ref[0], recv_ref[j]).astype(F32)
            acc = term if acc is None else acc + term
        o_ref[...] = acc

    return pl.pallas_call(
        body, name=name,
        grid_spec=pltpu.PrefetchScalarGridSpec(
            num_scalar_prefetch=1, grid=(R // tr,),
            in_specs=[pl.BlockSpec((1, tr, C), lambda i, me_ref: (me_ref[0], i, 0)),
                      pl.BlockSpec((n, tr, C), lambda i, me_ref: (0, i, 0))],
            out_specs=pl.BlockSpec((tr, C), lambda i, me_ref: (i, 0))),
        out_shape=_sds((R, C), F32), compiler_params=_cp(("parallel",)))(me.reshape(1), own, recv)


def _adamw_math(w, g, m, v):
    m = ADAM_B1 * m + (1.0 - ADAM_B1) * g
    v = ADAM_B2 * v + (1.0 - ADAM_B2) * (g * g)
    m_hat = m / (1.0 - ADAM_B1 ** ADAM_STEP)
    v_hat = v / (1.0 - ADAM_B2 ** ADAM_STEP)
    delta = -ADAM_LR * (m_hat / (jnp.sqrt(v_hat) + ADAM_EPS) + ADAM_WD * w)
    return delta, m, v


def _adamw(w, g, m, v, name):
    R, C = w.shape
    tr = _tile_rows(R)

    def body(w_ref, g_ref, m_ref, v_ref, d_ref, nm_ref, nv_ref):
        d, nm, nv = _adamw_math(w_ref[...], g_ref[...], m_ref[...], v_ref[...])
        d_ref[...] = d
        nm_ref[...] = nm
        nv_ref[...] = nv

    blk = _rows(tr, C)
    return pl.pallas_call(body, name=name, grid=(R // tr,), in_specs=[blk] * 4, out_specs=[blk] * 3,
                          out_shape=[_sds((R, C), F32)] * 3, compiler_params=_cp(("parallel",)))(w, g, m, v)


def _gather_weights(shards):
    nw = len(shards)
    ins, outs, sems = _gather_io(shards)

    def body(*refs):
        ex = _Gather(refs[:nw], refs[nw:2 * nw], *refs[2 * nw:])
        ex.start()
        ex.forward()
        ex.finish()

    any_spec = pl.BlockSpec(memory_space=pl.ANY)
    got = pl.pallas_call(body, name="gather_weights", in_specs=[any_spec] * nw, out_specs=[any_spec] * nw,
                         out_shape=outs, scratch_shapes=sems)(*ins)
    return _gather_assemble(got, shards)


def _swap_halves(grads, name):
    nw = len(grads)
    ins, outs, sems = _swap_io(grads)

    def body(*refs):
        ex = _Swap(refs[:nw], refs[nw:2 * nw], *refs[2 * nw:])
        ex.start()
        ex.finish()

    any_spec = pl.BlockSpec(memory_space=pl.ANY)
    return pl.pallas_call(body, name=name, in_specs=[any_spec] * nw, out_specs=[any_spec] * nw,
                          out_shape=outs, scratch_shapes=sems)(*ins)


def _join_halves(sums):
    nw = len(sums)

    def body(*refs):
        f_refs, out_refs = refs[:nw], refs[nw:2 * nw]
        send_sems, recv_sems = refs[2 * nw:]
        x, y, c, _ = _place()
        ws = range(nw)

        def copy(w, half_index):
            return pltpu.make_async_remote_copy(
                src_ref=f_refs[w], dst_ref=out_refs[w].at[half_index], send_sem=send_sems.at[w],
                recv_sem=recv_sems.at[w], device_id=(x, y, 1 - c), device_id_type=MESH)

        sends = [copy(w, c) for w in ws]
        for cp in sends:
            cp.start()
        for w in ws:
            copy(w, 1 - c).wait_recv()
        for cp in sends:
            cp.wait_send()

    any_spec = pl.BlockSpec(memory_space=pl.ANY)
    outs = pl.pallas_call(
        body, name="join_halves", in_specs=[any_spec] * nw, out_specs=[any_spec] * nw,
        out_shape=[_sds((2,) + f.shape, f.dtype) for f in sums],
        scratch_shapes=[pltpu.SemaphoreType.DMA((nw,)), pltpu.SemaphoreType.DMA((nw,))],
    )(*sums)
    c = lax.axis_index("c")
    return [lax.dynamic_update_slice_in_dim(o, f[None], c, axis=0).reshape(2 * f.shape[0], f.shape[1])
            for o, f in zip(outs, sums)]


def _allreduce_small(block):
    m_per, n = block.shape

    def body(x_ref, sum_ref, loss_ref, all_ref, send_sems, recv_sems, local_sem):
        x, y, c, chips = _place()
        me, sibling = (x, y, c), (x, y, 1 - c)

        def rows(px, py, pc):
            return all_ref.at[pl.ds(pl.multiple_of((4 * px + 2 * py + pc) * m_per, 8), m_per), :]

        def copy(k, blk, to, src=None):
            return pltpu.make_async_remote_copy(
                src_ref=rows(*blk) if src is None else src, dst_ref=rows(*blk), send_sem=send_sems.at[k],
                recv_sem=recv_sems.at[k], device_id=to, device_id_type=MESH)

        mine = pltpu.make_async_copy(x_ref, rows(*me), local_sem)
        mine.start()
        first = [copy(0, me, sibling, src=x_ref)]
        first += [copy(1 + j, me, (*chip, c), src=x_ref) for j, chip in enumerate(chips)]
        for cp in first:
            cp.start()
        passed = [copy(4 + j, (*chip, c), sibling) for j, chip in enumerate(chips)]
        for j, chip in enumerate(chips):
            copy(1 + j, (*chip, c), me).wait_recv()
            passed[j].start()
        copy(0, sibling, me).wait_recv()
        for j, chip in enumerate(chips):
            copy(4 + j, (*chip, 1 - c), me).wait_recv()
        for cp in first + passed:
            cp.wait_send()
        mine.wait()

        acc = all_ref[0:m_per, :]
        for d in range(1, 8):
            acc = acc + all_ref[d * m_per:(d + 1) * m_per, :]
        sum_ref[...] = acc
        tot = jnp.sum(acc[8:9, :], axis=1, keepdims=True) * (0.5 / D_MODEL)
        loss_ref[...] = jnp.broadcast_to(tot, loss_ref.shape)

    vm = pl.BlockSpec(memory_space=pltpu.VMEM)
    return pl.pallas_call(
        body, name="allreduce_small", in_specs=[vm], out_specs=[vm, vm],
        out_shape=[_sds((m_per, n), F32), _sds((8, 128), F32)],
        scratch_shapes=[pltpu.VMEM((8 * m_per, n), F32), pltpu.SemaphoreType.DMA((7,)), pltpu.SemaphoreType.DMA((7,)),
                        pltpu.SemaphoreType.DMA],
    )(block)


def _heads_rows(x, nh):
    S = x.shape[0]
    return x.reshape(S, nh, HEAD_DIM).transpose(1, 0, 2)


def _heads_cols(x, nh):
    S = x.shape[0]
    return x.reshape(S, nh, HEAD_DIM).transpose(1, 2, 0)


def _pad_row(v):
    v = v.reshape(1, -1)
    return jnp.pad(v, ((0, 0), (0, D_MODEL - v.shape[1])))


def _pack_small(ln_in_g, ln_in_b, sb_g, swa_g, sinks, rel_bias, ln1_g, ln1_b, ln2_g, ln2_b, extra):
    rows = [_pad_row(ln_in_g), _pad_row(ln_in_b), jnp.concatenate([sb_g.reshape(1, -1), swa_g.reshape(1, -1)], axis=1),
            _pad_row(jnp.concatenate([rel_bias.reshape(1, -1), sinks.reshape(1, -1)], axis=1)),
            _pad_row(ln1_g), _pad_row(ln1_b), _pad_row(ln2_g), _pad_row(ln2_b), _pad_row(extra)]
    rows.append(jnp.zeros((SMALL_ROWS - len(rows), D_MODEL), F32))
    return jnp.concatenate(rows, axis=0)


def _unpack_small(blk):
    nrb = REL_BUCKETS * SWA_HEADS
    return (blk[0], blk[1], blk[2:3, :SB_WIDTH], blk[2:3, SB_WIDTH:], blk[3:4, nrb:nrb + SWA_HEADS],
            blk[3, :nrb].reshape(REL_BUCKETS, SWA_HEADS), blk[4:5], blk[5:6], blk[6:7], blk[7:8])


def kernel(x, ln_in_g, ln_in_b, w_in, sb_norm_g, swa_norm_g, sinks, rel_bias, w_out, ln1_g, ln1_b, w_gate_up, w_down, ln2_g, ln2_b, loss_target, m_ln_in_g, m_ln_in_b, m_w_in, m_sb_norm_g, m_swa_norm_g, m_sinks, m_rel_bias, m_w_out, m_ln1_g, m_ln1_b, m_w_gate_up, m_w_down, m_ln2_g, m_ln2_b, v_ln_in_g, v_ln_in_b, v_w_in, v_sb_norm_g, v_swa_norm_g, v_sinks, v_rel_bias, v_w_out, v_ln1_g, v_ln1_b, v_w_gate_up, v_w_down, v_ln2_g, v_ln2_b):
    S = x.shape[1]
    x2 = x.reshape(S, D_MODEL)
    tgt = loss_target.reshape(S, D_MODEL)
    T = min(S, SB_TILE)
    bucket = jnp.asarray(_bucket_table().T)
    row = lambda v: v.reshape(1, -1)

    shards = [_cast_rows(w[0], _MXU, "cast_" + n) for n, w in (("w_in", w_in), ("w_out", w_out), ("w_gate_up", w_gate_up), ("w_down", w_down))]
    (w_in_sh,) = _gather_weights(shards[:1])
    w_in_f = jnp.concatenate([w_in_sh[j] for j in range(N_CHIPS)], axis=1)

    h0, h0b, kv_sw, qT_sb, kTb_sb, vTb_sb, kb_sb, vb_sb, qh_sw = _ln_in_proj(x2, row(ln_in_g), row(ln_in_b), w_in_f)
    k_sw, v_sw = kv_sw[:, :SWA_KV_WIDTH], kv_sw[:, SWA_KV_WIDTH:]
    sb_out, rsave, sb_first, (w_out_sh, w_gu_sh, w_down_sh) = _sb_fwd(qT_sb, kb_sb, vTb_sb, shards[1:])
    w_out_f = w_out_sh.reshape(D_MODEL, D_MODEL)
    w_down_f = w_down_sh.reshape(D_FF, D_MODEL)

    bias = _swa_bias(rel_bias, bucket)
    sink_rows = jnp.broadcast_to(sinks.reshape(SWA_HEADS, 1, 1), (SWA_HEADS, 1, BLOCK))
    kh_sw, vh_sw = _heads_rows(k_sw, SWA_KV_HEADS), _heads_rows(v_sw, SWA_KV_HEADS)
    swa_out = _swa_fwd(qh_sw, kh_sw, _heads_cols(v_sw, SWA_KV_HEADS), bias, sink_rows)

    pre1, merged, h1b = _mix_out(sb_out, swa_out, sb_norm_g, swa_norm_g, w_out_f, h0, ln1_g, ln1_b)
    act, silu, dsilu_up = _ffn_up(h1b, w_gu_sh)
    dp2, dp2b, dg2, db2, errsum = _ffn_down_loss(act, w_down_f, pre1, ln1_g, ln1_b, ln2_g, ln2_b, tgt)

    dgate, dup, g_w_down = _ffn_down_bwd(dp2b, w_down_f, silu, dsilu_up, act)
    g_w_gu = _matmul_tn_pair(h1b, dgate, dup, "grad_w_gate_up")
    dp1, dp1b, dg1, db1 = _ffn_up_bwd(dgate, dup, w_gu_sh, dp2, pre1, ln1_g)
    g_w_out = _matmul_tn(merged, dp1b, "grad_w_out", D_MODEL, D_MODEL)
    doT_sb, doh_sw, dgsb, dgsw = _mix_bwd(dp1b, w_out_f, sb_out, swa_out, sb_norm_g, swa_norm_g)

    c = lax.axis_index("c").astype(jnp.int32)
    me = (2 * lax.axis_index("x") + lax.axis_index("y")).astype(jnp.int32)
    grads_a = [g_w_out.reshape(N_CHIPS, D_MODEL // N_CHIPS, D_MODEL), g_w_gu, g_w_down.reshape(N_CHIPS, D_FF // N_CHIPS, D_MODEL)]
    names_a = ("w_out", "w_gate_up", "w_down")
    dq_sw, dkh_sw, dvh_sw, dbias, dsink, swapped_a = _swa_bwd(qh_sw, kh_sw, _heads_cols(k_sw, SWA_KV_HEADS), vh_sw, bias,
                                                               sink_rows, doh_sw, grads_a)
    swa_small = _swa_small_grads(dbias, dsink, bucket)
    partials_a = [_pair_sum(g, r, c, "pair_sum_" + n) for g, r, n in zip(grads_a, swapped_a, names_a)]
    dq_sb, dk_sb, dv_sb, recv_a = _sb_bwd(qT_sb, kb_sb, kTb_sb, vb_sb,
                                             doT_sb, rsave, sb_first, partials_a)
    tok = lambda t, nh: t.reshape(nh, S, HEAD_DIM).transpose(1, 0, 2).reshape(S, nh * HEAD_DIM)
    dproj = [dq_sb, dk_sb, dv_sb, dq_sw,
             jnp.concatenate([tok(dkh_sw, SWA_KV_HEADS), tok(dvh_sw, SWA_KV_HEADS)], axis=1).astype(_MXU)]
    g_w_in = jnp.concatenate([_matmul_tn(h0b, d, "grad_w_in_%d" % k, D_MODEL, d.shape[1]) for k, d in enumerate(dproj)],
                             axis=1)

    cin = IN_COLS // N_CHIPS
    grads_b = [jnp.stack([g_w_in[:, j * cin:(j + 1) * cin] for j in range(N_CHIPS)])]
    partials_b = [_pair_sum(grads_b[0], _swap_halves(grads_b, "swap_halves_in")[0], c, "pair_sum_w_in")]
    grad_x, dg_in, db_in, recv_b = _in_proj_bwd(dproj, w_in_f, dp1, x2, row(ln_in_g), partials_b)
    names = ("w_in",) + names_a
    sums = [_chip_sum(p, r, me, "chip_sum_" + n) for p, r, n in zip(partials_b + partials_a, list(recv_b) + list(recv_a), names)]
    gs_in, gs_out, gs_gu, gs_down = _join_halves(sums)

    nrb = REL_BUCKETS * SWA_HEADS
    small = _pack_small(dg_in, db_in, dgsb, dgsw, swa_small[REL_BUCKETS, :SWA_HEADS],
                        swa_small[:REL_BUCKETS, :SWA_HEADS], dg1, db1, dg2, db2, errsum)
    g_small, loss_tile = _allreduce_small(small)
    loss = loss_tile[0, 0]

    big = []
    for name, w, g, m, v in (("adamw_w_in", w_in, gs_in, m_w_in, v_w_in), ("adamw_w_out", w_out, gs_out, m_w_out, v_w_out),
                             ("adamw_w_gate_up", w_gate_up, gs_gu, m_w_gate_up, v_w_gate_up),
                             ("adamw_w_down", w_down, gs_down, m_w_down, v_w_down)):
        d, nm, nv = _adamw(w[0], g, m[0], v[0], name)
        big.append((g[None], d[None], nm[None], nv[None]))
    zero = jnp.zeros((1,), F32)
    w_small = _pack_small(ln_in_g, ln_in_b, sb_norm_g, swa_norm_g, sinks, rel_bias, ln1_g, ln1_b, ln2_g, ln2_b, zero)
    m_small = _pack_small(m_ln_in_g, m_ln_in_b, m_sb_norm_g, m_swa_norm_g, m_sinks, m_rel_bias, m_ln1_g, m_ln1_b,
                          m_ln2_g, m_ln2_b, zero)
    v_small = _pack_small(v_ln_in_g, v_ln_in_b, v_sb_norm_g, v_swa_norm_g, v_sinks, v_rel_bias, v_ln1_g, v_ln1_b,
                          v_ln2_g, v_ln2_b, zero)
    small_out = [_unpack_small(t) for t in (g_small,) + tuple(_adamw(w_small, g_small, m_small, v_small, "adamw_small"))]

    def kind(k):
        s = small_out[k]
        return [s[0], s[1], big[0][k], s[2], s[3], s[4], s[5], big[1][k], s[6], s[7], big[2][k], big[3][k], s[8], s[9]]

    return (loss, grad_x.reshape(1, S, D_MODEL), *kind(0), *kind(1), *kind(2), *kind(3))
```

```python
import math

import numpy as np
import jax
import jax.numpy as jnp
from jax import lax
from jax.experimental import pallas as pl
from jax.experimental.pallas import tpu as pltpu

F32 = jnp.float32
_MXU = jnp.bfloat16

D_MODEL = 1024
HEAD_DIM = 64
SB_HEADS = 8
SWA_HEADS = 8
SWA_KV_HEADS = 2
SWA_GROUP = SWA_HEADS // SWA_KV_HEADS
SB_WIDTH = SB_HEADS * HEAD_DIM
SWA_WIDTH = SWA_HEADS * HEAD_DIM
SWA_KV_WIDTH = SWA_KV_HEADS * HEAD_DIM
IN_COLS = 3 * SB_WIDTH + SWA_WIDTH + 2 * SWA_KV_WIDTH
BLOCK = 128
REL_BUCKETS = 32
REL_MAX_DIST = 128
D_FF = 2816
FF_CHUNK = D_FF // 2
ALPHA = 2.0 ** 0.25
LN_EPS = 1e-5
RMS_EPS = 1e-6
SCALE = HEAD_DIM ** -0.5
SB_TILE = 256
SB_GROUP_FWD = 8
SB_GROUP_BWD = 4
SB_FORWARD_LEAD = 8
SB_DEAD = -105.0
SWA_SUB = 8

ADAM_LR = 0.001
ADAM_B1 = 0.9
ADAM_B2 = 0.999
ADAM_EPS = 1e-08
ADAM_WD = 0.01
ADAM_STEP = 10

N_CHIPS = 4
SMALL_ROWS = 16

MESH = pl.DeviceIdType.MESH


def _sds(shape, dtype):
    return jax.ShapeDtypeStruct(shape, dtype)


def _cp(sem=None, vmem_mb=48):
    kw = dict(vmem_limit_bytes=vmem_mb * 1024 * 1024)
    if sem is not None:
        kw["dimension_semantics"] = sem
    return pltpu.CompilerParams(**kw)


def _dot(a, b):
    return jnp.dot(a, b, preferred_element_type=F32)


def _dot_nt(a, b):
    return lax.dot_general(a, b, (((1,), (1,)), ((), ())), preferred_element_type=F32)


def _dot_tn(a, b):
    return lax.dot_general(a, b, (((0,), (0,)), ((), ())), preferred_element_type=F32)


def _ln_hat(x):
    mu = jnp.mean(x, axis=-1, keepdims=True)
    xc = x - mu
    var = jnp.mean(xc * xc, axis=-1, keepdims=True)
    rstd = lax.rsqrt(var + LN_EPS)
    return xc * rstd, rstd


def _ln_bwd(xhat, rstd, dy, g):
    dxh = dy * g
    m1 = jnp.mean(dxh, axis=-1, keepdims=True)
    m2 = jnp.mean(dxh * xhat, axis=-1, keepdims=True)
    return rstd * (dxh - m1 - xhat * m2)


def _colsum(x):
    return jnp.sum(x, axis=0, keepdims=True)


def _split2(x):
    hi = x.astype(_MXU)
    lo = (x - hi.astype(F32)).astype(_MXU)
    return hi, lo


def _rows(tm, n):
    return pl.BlockSpec((tm, n), lambda i: (i, 0))


def _fixed(*shape):
    nd = len(shape)
    return pl.BlockSpec(shape, lambda i: (0,) * nd)


IN_SECTIONS = (SB_WIDTH, SB_WIDTH, SB_WIDTH, SWA_WIDTH, 2 * SWA_KV_WIDTH)


def _ln_in_proj(x, g, b, w):
    S = x.shape[0]
    tm = min(S, SB_TILE)
    offs = np.cumsum((0,) + IN_SECTIONS)
    swa = (4,)

    def body(x_ref, g_ref, b_ref, w_ref, h_ref, hb_ref, *o_refs):
        p_refs, (qT_ref, kT_ref, vT_ref, kr_ref, vr_ref, qw_ref) = o_refs[:len(swa)], o_refs[len(swa):]
        xhat, _ = _ln_hat(x_ref[...])
        h = xhat * g_ref[...] + b_ref[...]
        h_ref[...] = h
        hb = h.astype(_MXU)
        hb_ref[...] = hb
        proj = _dot(hb, w_ref[...])
        for k, p_ref in zip(swa, p_refs):
            p_ref[...] = proj[:, offs[k]:offs[k + 1]].astype(p_ref.dtype)
        heads = lambda k: proj[:, offs[k]:offs[k + 1]].T.astype(_MXU).reshape(SB_HEADS, HEAD_DIM, tm)
        qT_ref[...] = heads(0)
        kT_ref[:, 0] = heads(1)
        vT_ref[:, 0] = heads(2)
        for hd in range(SB_HEADS):
            cols = slice(hd * HEAD_DIM, (hd + 1) * HEAD_DIM)
            kr_ref[hd, 0] = proj[:, offs[1]:offs[2]][:, cols].astype(_MXU)
            vr_ref[hd, 0] = proj[:, offs[2]:offs[3]][:, cols].astype(_MXU)
            qw_ref[hd] = proj[:, offs[3]:offs[4]][:, cols].astype(_MXU)

    blocked = pl.BlockSpec((SB_HEADS, 1, HEAD_DIM, tm), lambda i: (0, i, 0, 0))
    blocked_rows = pl.BlockSpec((SB_HEADS, 1, tm, HEAD_DIM), lambda i: (0, i, 0, 0))
    return pl.pallas_call(
        body, name="ln_in_proj", grid=(S // tm,),
        in_specs=[_rows(tm, D_MODEL), _fixed(1, D_MODEL), _fixed(1, D_MODEL), _fixed(D_MODEL, IN_COLS)],
        out_specs=[_rows(tm, D_MODEL), _rows(tm, D_MODEL)] + [_rows(tm, IN_SECTIONS[k]) for k in swa]
                  + [pl.BlockSpec((SB_HEADS, HEAD_DIM, tm), lambda i: (0, 0, i)), blocked, blocked, blocked_rows,
                     blocked_rows, pl.BlockSpec((SWA_HEADS, tm, HEAD_DIM), lambda i: (0, i, 0))],
        out_shape=[_sds((S, D_MODEL), F32), _sds((S, D_MODEL), _MXU)] + [_sds((S, IN_SECTIONS[k]), _MXU) for k in swa]
                  + [_sds((SB_HEADS, HEAD_DIM, S), _MXU), _sds((SB_HEADS, S // tm, HEAD_DIM, tm), _MXU),
                     _sds((SB_HEADS, S // tm, HEAD_DIM, tm), _MXU), _sds((SB_HEADS, S // tm, tm, HEAD_DIM), _MXU),
                     _sds((SB_HEADS, S // tm, tm, HEAD_DIM), _MXU), _sds((SWA_HEADS, S, HEAD_DIM), _MXU)],
        compiler_params=_cp(("parallel",)),
    )(x, g, b, w)


def _rms(x, g):
    r = lax.rsqrt(jnp.mean(x * x, axis=-1, keepdims=True) + RMS_EPS)
    return x * r * g, r


def _mix_out(sb, sw, gsb, gsw, w_out, h0, g1, b1):
    S = sb.shape[0]
    tm = min(S, 512)

    def body(sb_ref, sw_ref, gsb_ref, gsw_ref, w_ref, h0_ref, g1_ref, b1_ref, pre_ref, mg_ref, h1_ref):
        ysb, _ = _rms(sb_ref[...], gsb_ref[...])
        ysw, _ = _rms(sw_ref[...], gsw_ref[...])
        ysb = ysb.astype(_MXU)
        ysw = ysw.astype(_MXU)
        mg_ref[:, :SB_WIDTH] = ysb
        mg_ref[:, SB_WIDTH:] = ysw
        mix = _dot(ysb, w_ref[:SB_WIDTH, :]) + _dot(ysw, w_ref[SB_WIDTH:, :])
        pre1 = ALPHA * h0_ref[...] + mix
        pre_ref[...] = pre1
        xhat, _ = _ln_hat(pre1)
        h1_ref[...] = (xhat * g1_ref[...] + b1_ref[...]).astype(h1_ref.dtype)

    vec = _fixed(1, D_MODEL)
    return pl.pallas_call(
        body, name="mix_out", grid=(S // tm,),
        in_specs=[_rows(tm, SB_WIDTH), _rows(tm, SWA_WIDTH), _fixed(1, SB_WIDTH), _fixed(1, SWA_WIDTH),
                  _fixed(D_MODEL, D_MODEL), _rows(tm, D_MODEL), vec, vec],
        out_specs=[_rows(tm, D_MODEL), _rows(tm, D_MODEL), _rows(tm, D_MODEL)],
        out_shape=[_sds((S, D_MODEL), F32), _sds((S, D_MODEL), _MXU), _sds((S, D_MODEL), _MXU)],
        compiler_params=_cp(("parallel",)),
    )(sb, sw, gsb, gsw, w_out, h0, g1, b1)


def _sigmoid(x):
    return 1.0 / (1.0 + jnp.exp(-x))


def _ffn_up(h1b, wgu):
    S = h1b.shape[0]
    tm = min(S, 1024)

    def body(h_ref, wg_ref, wu_ref, a_ref, s1_ref, s2_ref):
        h1 = h_ref[...]
        gate = _dot(h1, wg_ref[0])
        up = _dot(h1, wu_ref[0])
        sg = _sigmoid(gate)
        silu = gate * sg
        a_ref[...] = (silu * up).astype(a_ref.dtype)
        s1_ref[...] = silu.astype(s1_ref.dtype)
        s2_ref[...] = (up * (sg * (1.0 + gate * (1.0 - sg)))).astype(s2_ref.dtype)

    chunk = pl.BlockSpec((tm, FF_CHUNK), lambda j, i: (i, j))
    return pl.pallas_call(
        body, name="ffn_up", grid=(2, S // tm),
        in_specs=[pl.BlockSpec((tm, D_MODEL), lambda j, i: (i, 0)),
                  pl.BlockSpec((1, D_MODEL, FF_CHUNK), lambda j, i: (j, 0, 0)),
                  pl.BlockSpec((1, D_MODEL, FF_CHUNK), lambda j, i: (j + 2, 0, 0))],
        out_specs=[chunk, chunk, chunk],
        out_shape=[_sds((S, D_FF), _MXU)] * 3,
        compiler_params=_cp(("arbitrary", "arbitrary"), vmem_mb=56),
    )(h1b, wgu, wgu)


def _ffn_down_loss(a, w_down, pre1, g1, b1, g2, b2, tgt):
    S = a.shape[0]
    tm = min(S, 512)

    def body(a_ref, w_ref, p_ref, g1_ref, b1_ref, g2_ref, b2_ref, t_ref, d_ref, db_ref, dg2_ref, db2_ref, err_ref):
        @pl.when(pl.program_id(0) == 0)
        def _():
            dg2_ref[...] = jnp.zeros_like(dg2_ref)
            db2_ref[...] = jnp.zeros_like(db2_ref)
            err_ref[...] = jnp.zeros_like(err_ref)

        xhat1, _ = _ln_hat(p_ref[...])
        h1 = xhat1 * g1_ref[...] + b1_ref[...]
        pre2 = ALPHA * h1 + _dot(a_ref[...], w_ref[...])
        xhat2, rstd2 = _ln_hat(pre2)
        err = xhat2 * g2_ref[...] + b2_ref[...] - t_ref[...]
        dh2 = err * (1.0 / D_MODEL)
        dp2 = _ln_bwd(xhat2, rstd2, dh2, g2_ref[...])
        d_ref[...] = dp2
        db_ref[...] = dp2.astype(db_ref.dtype)
        dg2_ref[...] += _colsum(dh2 * xhat2)
        db2_ref[...] += _colsum(dh2)
        err_ref[...] += _colsum(err * err)

    vec = _fixed(1, D_MODEL)
    return pl.pallas_call(
        body, name="ffn_down_loss", grid=(S // tm,),
        in_specs=[_rows(tm, D_FF), _fixed(D_FF, D_MODEL), _rows(tm, D_MODEL), vec, vec, vec, vec, _rows(tm, D_MODEL)],
        out_specs=[_rows(tm, D_MODEL), _rows(tm, D_MODEL), vec, vec, vec],
        out_shape=[_sds((S, D_MODEL), F32), _sds((S, D_MODEL), _MXU), _sds((1, D_MODEL), F32), _sds((1, D_MODEL), F32),
                   _sds((1, D_MODEL), F32)],
        compiler_params=_cp(("arbitrary",)),
    )(a, w_down, pre1, g1, b1, g2, b2, tgt)


def _ffn_down_bwd(dp2b, w_down, s1, s2, act):
    S = dp2b.shape[0]
    tm = min(S, 1024)

    def body(d_ref, w_ref, s1_ref, s2_ref, a_ref, dg_ref, du_ref, gw_ref):
        @pl.when(pl.program_id(1) == 0)
        def _():
            gw_ref[...] = jnp.zeros_like(gw_ref)

        d = d_ref[...]
        da = _dot_nt(d, w_ref[...])
        gw_ref[...] += _dot_tn(a_ref[...], d)
        du_ref[...] = (da * s1_ref[...].astype(F32)).astype(du_ref.dtype)
        dg_ref[...] = (da * s2_ref[...].astype(F32)).astype(dg_ref.dtype)

    chunk = pl.BlockSpec((tm, FF_CHUNK), lambda j, i: (i, j))
    return pl.pallas_call(
        body, name="ffn_down_bwd", grid=(2, S // tm),
        in_specs=[pl.BlockSpec((tm, D_MODEL), lambda j, i: (i, 0)),
                  pl.BlockSpec((FF_CHUNK, D_MODEL), lambda j, i: (j, 0)), chunk, chunk, chunk],
        out_specs=[chunk, chunk, pl.BlockSpec((FF_CHUNK, D_MODEL), lambda j, i: (j, 0))],
        out_shape=[_sds((S, D_FF), _MXU), _sds((S, D_FF), _MXU), _sds((D_FF, D_MODEL), F32)],
        compiler_params=_cp(("arbitrary", "arbitrary"), vmem_mb=56),
    )(dp2b, w_down, s1, s2, act)


def _ffn_up_bwd(dgate, dup, wgu, dp2, pre1, g1):
    S = dgate.shape[0]
    tm = min(S, 256)

    def body(dg_ref, du_ref, w_ref, d2_ref, p_ref, g_ref, d1_ref, d1b_ref, dg1_ref, db1_ref):
        @pl.when(pl.program_id(0) == 0)
        def _():
            dg1_ref[...] = jnp.zeros_like(dg1_ref)
            db1_ref[...] = jnp.zeros_like(db1_ref)

        dh1 = ALPHA * d2_ref[...]
        for j in range(2):
            cols = slice(j * FF_CHUNK, (j + 1) * FF_CHUNK)
            dh1 += _dot_nt(dg_ref[:, cols], w_ref[j])
            dh1 += _dot_nt(du_ref[:, cols], w_ref[j + 2])
        xhat, rstd = _ln_hat(p_ref[...])
        dp1 = _ln_bwd(xhat, rstd, dh1, g_ref[...])
        d1_ref[...] = dp1
        d1b_ref[...] = dp1.astype(d1b_ref.dtype)
        dg1_ref[...] += _colsum(dh1 * xhat)
        db1_ref[...] += _colsum(dh1)

    vec = _fixed(1, D_MODEL)
    return pl.pallas_call(
        body, name="ffn_up_bwd", grid=(S // tm,),
        in_specs=[_rows(tm, D_FF), _rows(tm, D_FF), _fixed(4, D_MODEL, FF_CHUNK), _rows(tm, D_MODEL),
                  _rows(tm, D_MODEL), vec],
        out_specs=[_rows(tm, D_MODEL), _rows(tm, D_MODEL), vec, vec],
        out_shape=[_sds((S, D_MODEL), F32), _sds((S, D_MODEL), _MXU), _sds((1, D_MODEL), F32), _sds((1, D_MODEL), F32)],
        compiler_params=_cp(("arbitrary",), vmem_mb=56),
    )(dgate, dup, wgu, dp2, pre1, g1)


def _rms_bwd(x, g, dy):
    n = x.shape[-1]
    r = lax.rsqrt(jnp.mean(x * x, axis=-1, keepdims=True) + RMS_EPS)
    u = dy * g
    dx = r * u - x * (r * r * r) * (jnp.sum(u * x, axis=-1, keepdims=True) * (1.0 / n))
    return dx, _colsum(dy * x * r)


def _mix_bwd(dp1b, w_out, sb, sw, gsb, gsw):
    S = sb.shape[0]
    tm = min(S, 512)

    def body(d_ref, w_ref, sb_ref, sw_ref, gsb_ref, gsw_ref, dsb_ref, dsw_ref, dgsb_ref, dgsw_ref):
        @pl.when(pl.program_id(0) == 0)
        def _():
            dgsb_ref[...] = jnp.zeros_like(dgsb_ref)
            dgsw_ref[...] = jnp.zeros_like(dgsw_ref)

        dm = _dot_nt(d_ref[...], w_ref[...])
        dsb, dgsb = _rms_bwd(sb_ref[...], gsb_ref[...], dm[:, :SB_WIDTH])
        dsw, dgsw = _rms_bwd(sw_ref[...], gsw_ref[...], dm[:, SB_WIDTH:])
        dsb_ref[...] = dsb.T.astype(dsb_ref.dtype).reshape(dsb_ref.shape)
        for hd in range(SWA_HEADS):
            dsw_ref[hd] = dsw[:, hd * HEAD_DIM:(hd + 1) * HEAD_DIM].astype(dsw_ref.dtype)
        dgsb_ref[...] += dgsb
        dgsw_ref[...] += dgsw

    return pl.pallas_call(
        body, name="mix_bwd", grid=(S // tm,),
        in_specs=[_rows(tm, D_MODEL), _fixed(D_MODEL, D_MODEL), _rows(tm, SB_WIDTH), _rows(tm, SWA_WIDTH),
                  _fixed(1, SB_WIDTH), _fixed(1, SWA_WIDTH)],
        out_specs=[pl.BlockSpec((SB_HEADS, HEAD_DIM, tm), lambda i: (0, 0, i)),
                   pl.BlockSpec((SWA_HEADS, tm, HEAD_DIM), lambda i: (0, i, 0)), _fixed(1, SB_WIDTH), _fixed(1, SWA_WIDTH)],
        out_shape=[_sds((SB_HEADS, HEAD_DIM, S), _MXU), _sds((SWA_HEADS, S, HEAD_DIM), _MXU), _sds((1, SB_WIDTH), F32),
                   _sds((1, SWA_WIDTH), F32)],
        compiler_params=_cp(("arbitrary",)),
    )(dp1b, w_out, sb, sw, gsb, gsw)


def _in_proj_bwd(dproj, w_in, dp1, x, g, parts):
    S = x.shape[0]
    tm = min(S, 512)
    nw = len(parts)
    ns = len(IN_SECTIONS)
    offs = np.cumsum((0,) + IN_SECTIONS)
    s_ins, s_outs, s_sems = _scatter_io(parts)

    def body(*refs):
        dpj_refs = refs[:ns]
        w_ref, d1_ref, x_ref, g_ref = refs[ns:ns + 4]
        rest = refs[ns + 4:]
        gx_ref, dg_ref, db_ref = rest[nw:nw + 3]
        scatter = _Scatter(rest[:nw], rest[nw + 3:2 * nw + 3], *rest[2 * nw + 3:])

        @pl.when(pl.program_id(0) == 0)
        def _():
            scatter.start()
            dg_ref[...] = jnp.zeros_like(dg_ref)
            db_ref[...] = jnp.zeros_like(db_ref)

        dh0 = ALPHA * d1_ref[...]
        for k in range(ns):
            dh0 += _dot_nt(dpj_refs[k][...], w_ref[:, offs[k]:offs[k + 1]])
        xhat, rstd = _ln_hat(x_ref[...])
        gx_ref[...] = _ln_bwd(xhat, rstd, dh0, g_ref[...])
        dg_ref[...] += _colsum(dh0 * xhat)
        db_ref[...] += _colsum(dh0)

        @pl.when(pl.program_id(0) == pl.num_programs(0) - 1)
        def _():
            scatter.finish()

    vec = _fixed(1, D_MODEL)
    any_spec = pl.BlockSpec(memory_space=pl.ANY)
    res = pl.pallas_call(
        body, name="in_proj_bwd", grid=(S // tm,),
        in_specs=[_rows(tm, n) for n in IN_SECTIONS]
                 + [_fixed(D_MODEL, IN_COLS), _rows(tm, D_MODEL), _rows(tm, D_MODEL), vec] + [any_spec] * nw,
        out_specs=[_rows(tm, D_MODEL), vec, vec] + [any_spec] * nw,
        out_shape=[_sds((S, D_MODEL), F32), _sds((1, D_MODEL), F32), _sds((1, D_MODEL), F32)] + s_outs,
        scratch_shapes=s_sems,
        compiler_params=_cp(("arbitrary",)),
    )(*dproj, w_in, dp1, x, g, *s_ins)
    return res[0], res[1], res[2], list(res[3:])


def _matmul_tn(a, b, name, tk, tn):
    T, K = a.shape
    N = b.shape[1]
    tt = min(T, 1024)

    def body(a_ref, b_ref, o_ref):
        @pl.when(pl.program_id(2) == 0)
        def _():
            o_ref[...] = jnp.zeros_like(o_ref)

        o_ref[...] += _dot_tn(a_ref[...], b_ref[...])

    return pl.pallas_call(
        body, name=name, grid=(K // tk, N // tn, T // tt),
        in_specs=[pl.BlockSpec((tt, tk), lambda k, n, t: (t, k)), pl.BlockSpec((tt, tn), lambda k, n, t: (t, n))],
        out_specs=pl.BlockSpec((tk, tn), lambda k, n, t: (k, n)),
        out_shape=_sds((K, N), F32),
        compiler_params=_cp(("parallel", "parallel", "arbitrary")),
    )(a, b)


def _place():
    x, y, c = lax.axis_index("x"), lax.axis_index("y"), lax.axis_index("c")
    chips = [(1 - x, y), (x, 1 - y), (1 - x, 1 - y)]
    return x, y, c, chips


class _Gather:
    def __init__(self, in_refs, out_refs, send_sems, recv_sems):
        self.in_refs, self.out_refs, self.send_sems, self.recv_sems = in_refs, out_refs, send_sems, recv_sems
        self.x, self.y, self.c, self.chips = _place()

    def _copy(self, w, k, chip, hc, to, src=None):
        part = self.out_refs[w].at[2 * chip[0] + chip[1], hc]
        return pltpu.make_async_remote_copy(
            src_ref=part if src is None else src, dst_ref=part, send_sem=self.send_sems.at[w, k],
            recv_sem=self.recv_sems.at[w, k], device_id=to, device_id_type=MESH)

    def _first(self):
        x, y, c = self.x, self.y, self.c
        return [self._copy(w, j, (x, y), c, (*chip, c), src=self.in_refs[w].at[c])
                for w in range(len(self.in_refs)) for j, chip in enumerate(self.chips)]

    def start(self):
        for cp in self._first():
            cp.start()

    def _passed(self):
        sibling = (self.x, self.y, 1 - self.c)
        return [self._copy(w, 3 + j, chip, self.c, sibling)
                for w in range(len(self.in_refs)) for j, chip in enumerate(self.chips)]

    def forward(self):
        me = (self.x, self.y, self.c)
        passed = self._passed()
        for w in range(len(self.in_refs)):
            for j, chip in enumerate(self.chips):
                self._copy(w, j, chip, self.c, me).wait_recv()
                passed[3 * w + j].start()

    def finish(self):
        me = (self.x, self.y, self.c)
        for w in range(len(self.in_refs)):
            for j, chip in enumerate(self.chips):
                self._copy(w, 3 + j, chip, 1 - self.c, me).wait_recv()
        for cp in self._first() + self._passed():
            cp.wait_send()


def _gather_io(shards):
    halves = [(s.shape[0] // 2, s.shape[1]) for s in shards]
    ins = [s.reshape(2, h, cols) for s, (h, cols) in zip(shards, halves)]
    outs = [_sds((N_CHIPS, 2, h, cols), s.dtype) for s, (h, cols) in zip(shards, halves)]
    sems = [pltpu.SemaphoreType.DMA((len(shards), 6)), pltpu.SemaphoreType.DMA((len(shards), 6))]
    return ins, outs, sems


def _gather_assemble(outs, shards):
    me = 2 * lax.axis_index("x") + lax.axis_index("y")
    return [lax.dynamic_update_slice_in_dim(o.reshape((N_CHIPS,) + s.shape), s[None], me, axis=0)
            for o, s in zip(outs, shards)]


class _Scatter:
    def __init__(self, p_refs, out_refs, send_sems, recv_sems):
        self.p_refs, self.out_refs, self.send_sems, self.recv_sems = p_refs, out_refs, send_sems, recv_sems
        self.x, self.y, self.c, self.chips = _place()
        self.me = 2 * self.x + self.y

    def _copy(self, w, j, chip, src_chip, dst_chip):
        return pltpu.make_async_remote_copy(
            src_ref=self.p_refs[w].at[src_chip], dst_ref=self.out_refs[w].at[dst_chip], send_sem=self.send_sems.at[w, j],
            recv_sem=self.recv_sems.at[w, j], device_id=(*chip, self.c), device_id_type=MESH)

    def _sends(self):
        return [self._copy(w, j, chip, 2 * chip[0] + chip[1], self.me)
                for w in range(len(self.p_refs)) for j, chip in enumerate(self.chips)]

    def start(self):
        for cp in self._sends():
            cp.start()

    def finish(self):
        for w in range(len(self.p_refs)):
            for j, chip in enumerate(self.chips):
                self._copy(w, j, chip, self.me, 2 * chip[0] + chip[1]).wait_recv()
        for cp in self._sends():
            cp.wait_send()


def _scatter_io(parts):
    sems = [pltpu.SemaphoreType.DMA((len(parts), 3)), pltpu.SemaphoreType.DMA((len(parts), 3))]
    return list(parts), [_sds(p.shape, p.dtype) for p in parts], sems


class _Swap:
    def __init__(self, g_refs, out_refs, send_sems, recv_sems):
        x, y, c, _ = _place()
        self.copies = []
        for w in range(len(g_refs)):
            half = out_refs[w].shape[1]
            theirs = g_refs[w].at[:, pl.ds(pl.multiple_of((1 - c) * half, 8), half), :]
            self.copies.append(pltpu.make_async_remote_copy(
                src_ref=theirs, dst_ref=out_refs[w], send_sem=send_sems.at[w], recv_sem=recv_sems.at[w],
                device_id=(x, y, 1 - c), device_id_type=MESH))

    def start(self):
        for cp in self.copies:
            cp.start()

    def finish(self):
        for cp in self.copies:
            cp.wait()


def _swap_io(grads):
    outs = [_sds((g.shape[0], g.shape[1] // 2, g.shape[2]), g.dtype) for g in grads]
    return list(grads), outs, [pltpu.SemaphoreType.DMA((len(grads),)), pltpu.SemaphoreType.DMA((len(grads),))]


def _matmul_tn_pair(a, b0, b1, name):
    T, K = a.shape
    tt = min(T, 1024)

    def body(a_ref, b0_ref, b1_ref, o_ref):
        n = pl.program_id(0)

        @pl.when(pl.program_id(1) == 0)
        def _():
            o_ref[...] = jnp.zeros_like(o_ref)

        @pl.when(n < 2)
        def _():
            o_ref[0] += _dot_tn(a_ref[...], b0_ref[...])

        @pl.when(n >= 2)
        def _():
            o_ref[0] += _dot_tn(a_ref[...], b1_ref[...])

    return pl.pallas_call(
        body, name=name, grid=(4, T // tt),
        in_specs=[pl.BlockSpec((tt, K), lambda n, t: (t, 0)),
                  pl.BlockSpec((tt, FF_CHUNK), lambda n, t: (t, jnp.minimum(n, 1))),
                  pl.BlockSpec((tt, FF_CHUNK), lambda n, t: (t, jnp.maximum(n - 2, 0)))],
        out_specs=pl.BlockSpec((1, K, FF_CHUNK), lambda n, t: (n, 0, 0)),
        out_shape=_sds((4, K, FF_CHUNK), F32),
        compiler_params=_cp(("parallel", "arbitrary")),
    )(a, b0, b1)


def _sb_logs(zt, causal):
    e = jnp.exp(-jnp.abs(zt))
    lb = jnp.minimum(zt, 0.0) - jnp.log(1.0 + e)
    l1m = lb - zt
    if causal is not None:
        l1m = jnp.where(causal, l1m, 0.0)
    return lb, l1m


def _sb_weights(lb, suf, causal):
    a = jnp.exp(lb + suf)
    if causal is not None:
        a = jnp.where(causal, a, 0.0)
    return a


def _tri_masks(t):
    r = lax.broadcasted_iota(jnp.int32, (t, t), 0)
    c = lax.broadcasted_iota(jnp.int32, (t, t), 1)
    return r, c


def _sb_fwd(qT, kb, vTb, shards):
    Hh, _, S = qT.shape
    nk, T = kb.shape[1], kb.shape[2]
    nq = S // T
    G = SB_GROUP_FWD
    nw = len(shards)
    g_ins, g_outs, g_sems = _gather_io(shards)
    forward_step = max(nq - 1 - SB_FORWARD_LEAD, 0)

    def body(qT_ref, k_ref, vT_ref, *rest):
        o_ref, rs_ref, first_ref = rest[nw:nw + 3]
        gather = _Gather(rest[:nw], rest[nw + 3:2 * nw + 3], *rest[2 * nw + 3:])
        i = pl.program_id(1)
        first_step = jnp.logical_and(pl.program_id(0) == 0, i == 0)
        last_step = jnp.logical_and(pl.program_id(0) == pl.num_programs(0) - 1, i == pl.num_programs(1) - 1)

        @pl.when(first_step)
        def _():
            gather.start()

        qts = [(qT_ref[g].astype(F32) * SCALE).astype(_MXU) for g in range(G)]
        r, c = _tri_masks(T)
        upper = (c > r).astype(_MXU)
        causal = r < c

        def blk(j, carry, mask):
            hs = range(G)
            for g in hs:
                rs_ref[g, 0, j] = jnp.broadcast_to(carry[g][0], (8, T))
            zs = [_dot(k_ref[g, j], qts[g]) for g in hs]
            lbs, l1ms = zip(*[_sb_logs(zs[g], mask) for g in hs])
            splits = [_split2(l1ms[g]) for g in hs]
            cums = [_dot(upper, splits[g][0]) + _dot(upper, splits[g][1]) for g in hs]
            avs = [_sb_weights(lbs[g], carry[g][0] + cums[g], mask).astype(_MXU) for g in hs]
            accs = [carry[g][1] + _dot(vT_ref[g, j], avs[g]) for g in hs]
            return tuple((carry[g][0] + _colsum(l1ms[g]), accs[g]) for g in hs)

        def go_on(j, carry):
            top = carry[0][0]
            for g in range(1, G):
                top = jnp.maximum(top, carry[g][0])
            return jnp.logical_and(j >= 0, jnp.max(top) >= SB_DEAD)

        init = tuple((jnp.zeros((1, T), F32), jnp.zeros((HEAD_DIM, T), F32)) for _ in range(G))
        carry = blk(i, init, causal)
        j, carry = lax.while_loop(lambda st: go_on(*st), lambda st: (st[0] - 1, blk(st[0], st[1], None)),
                                  (i - 1, carry))

        first_ref[...] = jnp.broadcast_to((j + 1).astype(F32), first_ref.shape)

        o_ref[...] = jnp.concatenate([carry[g][1] for g in range(G)], axis=0).T

        @pl.when(jnp.logical_and(pl.program_id(0) == pl.num_programs(0) - 1, i == forward_step))
        def _():
            gather.forward()

        @pl.when(last_step)
        def _():
            gather.finish()

    any_spec = pl.BlockSpec(memory_space=pl.ANY)
    res = pl.pallas_call(
        body, name="sb_fwd", grid=(Hh // G, nq),
        in_specs=[pl.BlockSpec((G, HEAD_DIM, T), lambda h, i: (h, 0, i)),
                  pl.BlockSpec((G, nk, T, HEAD_DIM), lambda h, i: (h, 0, 0, 0), pipeline_mode=pl.Buffered(1)),
                  pl.BlockSpec((G, nk, HEAD_DIM, T), lambda h, i: (h, 0, 0, 0), pipeline_mode=pl.Buffered(1))]
                 + [any_spec] * nw,
        out_specs=[pl.BlockSpec((T, G * HEAD_DIM), lambda h, i: (i, h)),
                   pl.BlockSpec((G, 1, nk, 8, T), lambda h, i: (h, i, 0, 0, 0)),
                   pl.BlockSpec((1, 1, 8, 128), lambda h, i: (h, i, 0, 0))] + [any_spec] * nw,
        out_shape=[_sds((S, Hh * HEAD_DIM), F32), _sds((Hh, nq, nk, 8, T), F32), _sds((Hh // G, nq, 8, 128), F32)]
                  + g_outs,
        scratch_shapes=g_sems,
        compiler_params=_cp(("arbitrary", "arbitrary")),
    )(qT, kb, vTb, *g_ins)
    return res[0], res[1], res[2], _gather_assemble(res[3:], shards)


def _sb_bwd(qT, kb, kTb, vb, doT, rsave, first, parts):
    Hh, _, S = qT.shape
    nk, T = kb.shape[1], kb.shape[2]
    nq = S // T
    G = SB_GROUP_BWD
    nw = len(parts)
    s_ins, s_outs, s_sems = _scatter_io(parts)

    def body(qT_ref, k_ref, kT_ref, v_ref, doT_ref, rs_ref, first_ref, *rest):
        dq_ref, dk_out_ref, dv_out_ref = rest[nw:nw + 3]
        dk_ref, dv_ref = rest[2 * nw + 3:2 * nw + 5]
        scatter = _Scatter(rest[:nw], rest[nw + 3:2 * nw + 3], *rest[2 * nw + 5:])
        i = pl.program_id(1)
        first_step = jnp.logical_and(pl.program_id(0) == 0, i == 0)
        last_step = jnp.logical_and(pl.program_id(0) == pl.num_programs(0) - 1, i == pl.num_programs(1) - 1)

        @pl.when(first_step)
        def _():
            scatter.start()

        @pl.when(i == 0)
        def _():
            dk_ref[...] = jnp.zeros_like(dk_ref)
            dv_ref[...] = jnp.zeros_like(dv_ref)

        qts = [(qT_ref[g].astype(F32) * SCALE).astype(_MXU) for g in range(G)]
        douts = [doT_ref[g] for g in range(G)]
        r, c = _tri_masks(T)
        upper = (c > r).astype(_MXU)
        lower = (c < r).astype(_MXU)
        causal = r < c

        def blk(j, carry, mask):
            hs = range(G)
            zs = [_dot(k_ref[g, j], qts[g]) for g in hs]
            das = [_dot(v_ref[g, j], douts[g]) for g in hs]
            lbs, l1ms = zip(*[_sb_logs(zs[g], mask) for g in hs])
            splits = [_split2(l1ms[g]) for g in hs]
            cums = [_dot(upper, splits[g][0]) + _dot(upper, splits[g][1]) for g in hs]
            avs = [_sb_weights(lbs[g], rs_ref[g, 0, j][0:1, :] + cums[g], mask) for g in hs]
            ets = [das[g] * avs[g] for g in hs]
            esplits = [_split2(ets[g]) for g in hs]
            ecums = [_dot(lower, esplits[g][0]) + _dot(lower, esplits[g][1]) for g in hs]
            dzs = []
            for g in hs:
                sig = jnp.exp(lbs[g])
                dz = ets[g] * (1.0 - sig) - (carry[g][0] + ecums[g]) * sig
                if mask is not None:
                    dz = jnp.where(mask, dz, 0.0)
                dzs.append(dz.astype(_MXU))
            dqs = [carry[g][1] + _dot(kT_ref[g, j], dzs[g]) for g in hs]
            for g in hs:
                dk_ref[j, g * HEAD_DIM:(g + 1) * HEAD_DIM, :] += _dot_nt(qts[g], dzs[g])
            for g in hs:
                dv_ref[j, g * HEAD_DIM:(g + 1) * HEAD_DIM, :] += _dot_nt(douts[g], avs[g].astype(_MXU))
            return tuple((carry[g][0] + _colsum(ets[g]), dqs[g]) for g in hs)

        first = jnp.clip(jnp.max(first_ref[0, 0][0:1, 0:1]).astype(jnp.int32), 0, i)
        carry = tuple((jnp.zeros((1, T), F32), jnp.zeros((HEAD_DIM, T), F32)) for _ in range(G))
        carry = lax.fori_loop(first, i, lambda s, cr: blk(s, cr, None), carry)
        carry = blk(i, carry, causal)
        dq_ref[...] = (jnp.concatenate([carry[g][1] for g in range(G)], axis=0) * SCALE).T.astype(dq_ref.dtype)

        @pl.when(i == pl.num_programs(1) - 1)
        def _():
            def flush(j, _):
                rows = pl.ds(pl.multiple_of(j * T, T), T)
                dk_out_ref[rows, :] = dk_ref[j].T.astype(dk_out_ref.dtype)
                dv_out_ref[rows, :] = dv_ref[j].T.astype(dv_out_ref.dtype)
                return 0
            lax.fori_loop(0, nk, flush, 0)

        @pl.when(last_step)
        def _():
            scatter.finish()

    colblk = pl.BlockSpec((G, HEAD_DIM, T), lambda h, i: (h, 0, i))
    once = pl.Buffered(1)
    kblk = pl.BlockSpec((G, nk, T, HEAD_DIM), lambda h, i: (h, 0, 0, 0), pipeline_mode=once)
    kTblk = pl.BlockSpec((G, nk, HEAD_DIM, T), lambda h, i: (h, 0, 0, 0), pipeline_mode=once)
    any_spec = pl.BlockSpec(memory_space=pl.ANY)
    res = pl.pallas_call(
        body, name="sb_bwd", grid=(Hh // G, nq),
        in_specs=[colblk, kblk, kTblk, kblk, colblk,
                  pl.BlockSpec((G, 1, nk, 8, T), lambda h, i: (h, i, 0, 0, 0)),
                  pl.BlockSpec((1, 1, 8, 128), lambda h, i: ((h * G) // SB_GROUP_FWD, i, 0, 0))] + [any_spec] * nw,
        out_specs=[pl.BlockSpec((T, G * HEAD_DIM), lambda h, i: (i, h)),
                   pl.BlockSpec((S, G * HEAD_DIM), lambda h, i: (0, h), pipeline_mode=once),
                   pl.BlockSpec((S, G * HEAD_DIM), lambda h, i: (0, h), pipeline_mode=once)] + [any_spec] * nw,
        out_shape=[_sds((S, Hh * HEAD_DIM), _MXU)] * 3 + s_outs,
        scratch_shapes=[pltpu.VMEM((nk, G * HEAD_DIM, T), F32), pltpu.VMEM((nk, G * HEAD_DIM, T), F32)] + s_sems,
        compiler_params=_cp(("arbitrary", "arbitrary"), vmem_mb=60),
    )(qT, kb, kTb, vb, doT, rsave, first, *s_ins)
    return res[0], res[1], res[2], list(res[3:])


def _bucket_table():
    qi = np.arange(BLOCK)[:, None]
    cj = np.arange(2 * BLOCK)[None, :]
    dist = qi + BLOCK - cj
    exact = REL_BUCKETS // 2
    d = np.maximum(dist, 0)
    d_f = np.maximum(d, 1).astype(np.float32)
    large = exact + (np.log(d_f / np.float32(exact)) / np.float32(math.log(REL_MAX_DIST / exact))
                     * np.float32(REL_BUCKETS - exact)).astype(np.int32)
    large = np.minimum(large, REL_BUCKETS - 1)
    return np.where(d < exact, d, large).astype(np.int32)


def _swa_bias(rel_bias, bucket):
    def body(rb_ref, bk_ref, o_ref):
        bk = bk_ref[...]
        for h in range(SWA_HEADS):
            t = jnp.zeros((2 * BLOCK, BLOCK), F32)
            for b in range(REL_BUCKETS):
                t = jnp.where(bk == b, rb_ref[b, h], t)
            o_ref[h] = t

    return pl.pallas_call(
        body, name="swa_bias",
        in_specs=[pl.BlockSpec(memory_space=pltpu.SMEM), pl.BlockSpec(memory_space=pltpu.VMEM)],
        out_specs=pl.BlockSpec(memory_space=pltpu.VMEM),
        out_shape=_sds((SWA_HEADS, 2 * BLOCK, BLOCK), F32),
    )(rel_bias, bucket)


def _swa_logits(q, kp, kc):
    qs = (q.astype(F32) * SCALE).astype(_MXU)
    return qs, _dot_nt(kp, qs), _dot_nt(kc, qs)


def _swa_softmax(lp, lc, bias, sink, live_prev):
    r, c = _tri_masks(BLOCK)
    in_window = r > c if live_prev is None else jnp.logical_and(r > c, live_prev)
    lp = jnp.where(in_window, lp + bias[:BLOCK, :], -jnp.inf)
    lc = jnp.where(r <= c, lc + bias[BLOCK:, :], -jnp.inf)
    m = jnp.maximum(jnp.maximum(jnp.max(lp, axis=0, keepdims=True), jnp.max(lc, axis=0, keepdims=True)), sink)
    pp = jnp.exp(lp - m)
    pc = jnp.exp(lc - m)
    ps = jnp.exp(sink - m)
    denom = _colsum(pp) + _colsum(pc) + ps
    return pp / denom, pc / denom, ps / denom


def _swa_sub(nb):
    return min(SWA_SUB, nb)


def _swa_keys(b, prev_ref, cur_ref, i):
    cur = cur_ref[0, b * BLOCK:(b + 1) * BLOCK, :]
    if b == 0:
        return prev_ref[0], cur, i > 0
    return cur_ref[0, (b - 1) * BLOCK:b * BLOCK, :], cur, None


def _swa_keys_t(b, prev_ref, cur_ref):
    cur = cur_ref[0, :, b * BLOCK:(b + 1) * BLOCK]
    return (prev_ref[0] if b == 0 else cur_ref[0, :, (b - 1) * BLOCK:b * BLOCK]), cur


SWA_PAIR = 4


def _swa_fwd(q, k, vT, bias, sink):
    S = q.shape[1]
    nb = S // BLOCK
    ns = _swa_sub(nb)
    R = ns * BLOCK
    P = SWA_PAIR

    def body(q_ref, kp_ref, kc_ref, vp_ref, vc_ref, bias_ref, sink_ref, o_ref):
        i = pl.program_id(1)
        units = [(hh, b) for hh in range(P) for b in range(ns)]
        keys = [_swa_keys(b, kp_ref, kc_ref, i) for b in range(ns)]
        vals = [_swa_keys_t(b, vp_ref, vc_ref) for b in range(ns)]
        logits = {u: _swa_logits(q_ref[u[0], u[1] * BLOCK:(u[1] + 1) * BLOCK, :], keys[u[1]][0], keys[u[1]][1])
                  for u in units}
        ws = {u: _swa_softmax(logits[u][1], logits[u][2], bias_ref[u[0]], sink_ref[u[0]][:, :1], keys[u[1]][2])
              for u in units}
        outs = {u: _dot(vals[u[1]][0], ws[u][0].astype(_MXU)) + _dot(vals[u[1]][1], ws[u][1].astype(_MXU))
                for u in units}
        for b in range(ns):
            o_ref[b * BLOCK:(b + 1) * BLOCK, :] = jnp.concatenate([outs[(hh, b)] for hh in range(P)], axis=0).T

    kvh = lambda p: (p * P) // SWA_GROUP
    prev = pl.BlockSpec((1, BLOCK, HEAD_DIM), lambda p, i: (kvh(p), jnp.maximum(i * ns - 1, 0), 0))
    cur = pl.BlockSpec((1, R, HEAD_DIM), lambda p, i: (kvh(p), i, 0))
    prev_t = pl.BlockSpec((1, HEAD_DIM, BLOCK), lambda p, i: (kvh(p), 0, jnp.maximum(i * ns - 1, 0)))
    cur_t = pl.BlockSpec((1, HEAD_DIM, R), lambda p, i: (kvh(p), 0, i))
    return pl.pallas_call(
        body, name="swa_fwd", grid=(SWA_HEADS // P, nb // ns),
        in_specs=[pl.BlockSpec((P, R, HEAD_DIM), lambda p, i: (p, i, 0)), prev, cur, prev_t, cur_t,
                  pl.BlockSpec((P, 2 * BLOCK, BLOCK), lambda p, i: (p, 0, 0)),
                  pl.BlockSpec((P, 1, BLOCK), lambda p, i: (p, 0, 0))],
        out_specs=pl.BlockSpec((R, P * HEAD_DIM), lambda p, i: (i, p)),
        out_shape=_sds((S, SWA_HEADS * HEAD_DIM), F32),
        compiler_params=_cp(("parallel", "parallel")),
    )(q, k, k, vT, vT, bias, sink)


def _swa_bwd(q, k, kT, v, bias, sink, do, grads):
    S = q.shape[1]
    nb = S // BLOCK
    ns = _swa_sub(nb)
    R = ns * BLOCK
    P = SWA_PAIR
    nw = len(grads)
    x_ins, x_outs, x_sems = _swap_io(grads)

    def body(q_ref, kp_ref, kc_ref, ktp_ref, ktc_ref, vp_ref, vc_ref, bias_ref, sink_ref, do_ref, *rest):
        dq_ref, dk_ref, dv_ref, dbias_ref, dsink_ref = rest[nw:nw + 5]
        swap = _Swap(rest[:nw], rest[nw + 5:2 * nw + 5], *rest[2 * nw + 5:])
        g = pl.program_id(1)
        i = pl.program_id(2)
        first_step = jnp.logical_and(pl.program_id(0) == 0, jnp.logical_and(g == 0, i == 0))
        last_step = jnp.logical_and(pl.program_id(0) == pl.num_programs(0) - 1,
                                    jnp.logical_and(g == pl.num_programs(1) - 1, i == pl.num_programs(2) - 1))

        @pl.when(first_step)
        def _():
            swap.start()

        @pl.when(jnp.logical_and(g == 0, i == 0))
        def _():
            dk_ref[...] = jnp.zeros_like(dk_ref)
            dv_ref[...] = jnp.zeros_like(dv_ref)

        @pl.when(i == 0)
        def _():
            dbias_ref[...] = jnp.zeros_like(dbias_ref)
            dsink_ref[...] = jnp.zeros_like(dsink_ref)

        subs = range(ns)
        units = [(hh, b) for hh in range(P) for b in subs]
        rows = [slice(b * BLOCK, (b + 1) * BLOCK) for b in subs]
        keys = [_swa_keys(b, kp_ref, kc_ref, i) for b in subs]
        keys_t = [_swa_keys_t(b, ktp_ref, ktc_ref) for b in subs]
        vals = [_swa_keys(b, vp_ref, vc_ref, i) for b in subs]
        douts = {u: do_ref[u[0], rows[u[1]], :] for u in units}
        logits = {u: _swa_logits(q_ref[u[0], rows[u[1]], :], keys[u[1]][0], keys[u[1]][1]) for u in units}
        dws = {u: (_dot_nt(vals[u[1]][0], douts[u]), _dot_nt(vals[u[1]][1], douts[u])) for u in units}
        wts, dls = {}, {}
        for hh in range(P):
            dbp = jnp.zeros((BLOCK, BLOCK), F32)
            dbc = jnp.zeros((BLOCK, BLOCK), F32)
            dsk = jnp.zeros((1, BLOCK), F32)
            for b in subs:
                u = (hh, b)
                wp, wc, ws = _swa_softmax(logits[u][1], logits[u][2], bias_ref[hh], sink_ref[hh][:, :1], keys[b][2])
                dwp, dwc = dws[u]
                delta = _colsum(wp * dwp) + _colsum(wc * dwc)
                dlp = wp * (dwp - delta)
                dlc = wc * (dwc - delta)
                dbp += dlp
                dbc += dlc
                dsk -= ws * delta
                wts[u] = (wp.astype(_MXU), wc.astype(_MXU))
                dls[u] = (dlp.astype(_MXU), dlc.astype(_MXU))
            dbias_ref[hh, :BLOCK, :] += dbp
            dbias_ref[hh, BLOCK:, :] += dbc
            dsink_ref[hh] += jnp.broadcast_to(dsk, (8, BLOCK))
        dqs = {u: (_dot(keys_t[u[1]][0], dls[u][0]) + _dot(keys_t[u[1]][1], dls[u][1])) * SCALE for u in units}
        for b in subs:
            dq_ref[rows[b], :] = jnp.concatenate([dqs[(hh, b)] for hh in range(P)], axis=0).T.astype(dq_ref.dtype)
        dk_cur = [sum(_dot(dls[(hh, b)][1], logits[(hh, b)][0]) for hh in range(P)) for b in subs]
        dv_cur = [sum(_dot(wts[(hh, b)][1], douts[(hh, b)]) for hh in range(P)) for b in subs]
        dk_prev = [sum(_dot(dls[(hh, b)][0], logits[(hh, b)][0]) for hh in range(P)) for b in subs]
        dv_prev = [sum(_dot(wts[(hh, b)][0], douts[(hh, b)]) for hh in range(P)) for b in subs]
        for b in subs:
            last = b + 1 == ns
            dk_ref[0, i * ns + b] += dk_cur[b] if last else dk_cur[b] + dk_prev[b + 1]
            dv_ref[0, i * ns + b] += dv_cur[b] if last else dv_cur[b] + dv_prev[b + 1]

        @pl.when(i > 0)
        def _():
            dk_ref[0, i * ns - 1] += dk_prev[0]
            dv_ref[0, i * ns - 1] += dv_prev[0]

        @pl.when(last_step)
        def _():
            swap.finish()

    G2 = SWA_GROUP // P
    hp = lambda kv, g, i: kv * G2 + g
    prev = pl.BlockSpec((1, BLOCK, HEAD_DIM), lambda kv, g, i: (kv, jnp.maximum(i * ns - 1, 0), 0))
    cur = pl.BlockSpec((1, R, HEAD_DIM), lambda kv, g, i: (kv, i, 0))
    prev_t = pl.BlockSpec((1, HEAD_DIM, BLOCK), lambda kv, g, i: (kv, 0, jnp.maximum(i * ns - 1, 0)))
    cur_t = pl.BlockSpec((1, HEAD_DIM, R), lambda kv, g, i: (kv, 0, i))
    qblk = pl.BlockSpec((P, R, HEAD_DIM), lambda kv, g, i: (hp(kv, g, i), i, 0))
    kvacc = pl.BlockSpec((1, nb, BLOCK, HEAD_DIM), lambda kv, g, i: (kv, 0, 0, 0))
    any_spec = pl.BlockSpec(memory_space=pl.ANY)
    res = pl.pallas_call(
        body, name="swa_bwd", grid=(SWA_KV_HEADS, G2, nb // ns),
        in_specs=[qblk, prev, cur, prev_t, cur_t, prev, cur,
                  pl.BlockSpec((P, 2 * BLOCK, BLOCK), lambda kv, g, i: (hp(kv, g, i), 0, 0)),
                  pl.BlockSpec((P, 1, BLOCK), lambda kv, g, i: (hp(kv, g, i), 0, 0)), qblk] + [any_spec] * nw,
        out_specs=[pl.BlockSpec((R, P * HEAD_DIM), lambda kv, g, i: (i, hp(kv, g, i))), kvacc, kvacc,
                   pl.BlockSpec((P, 2 * BLOCK, BLOCK), lambda kv, g, i: (hp(kv, g, i), 0, 0)),
                   pl.BlockSpec((P, 8, BLOCK), lambda kv, g, i: (hp(kv, g, i), 0, 0))] + [any_spec] * nw,
        out_shape=[_sds((S, SWA_HEADS * HEAD_DIM), _MXU), _sds((SWA_KV_HEADS, nb, BLOCK, HEAD_DIM), F32),
                   _sds((SWA_KV_HEADS, nb, BLOCK, HEAD_DIM), F32), _sds((SWA_HEADS, 2 * BLOCK, BLOCK), F32),
                   _sds((SWA_HEADS, 8, BLOCK), F32)] + x_outs,
        scratch_shapes=x_sems,
        compiler_params=_cp(("arbitrary", "arbitrary", "arbitrary")),
    )(q, k, k, kT, kT, v, v, bias, sink, do, *x_ins)
    return res[0], res[1], res[2], res[3], res[4], list(res[5:])


def _swa_small_grads(dbias, dsink, bucket):
    rows = REL_BUCKETS + 8

    def total(x):
        return jnp.sum(jnp.sum(x, axis=1, keepdims=True), axis=0, keepdims=True)

    def body(db_ref, ds_ref, bk_ref, o_ref):
        bk = bk_ref[...]
        r = lax.broadcasted_iota(jnp.int32, (rows, BLOCK), 0)
        c = lax.broadcasted_iota(jnp.int32, (rows, BLOCK), 1)
        out = jnp.zeros((rows, BLOCK), F32)
        for h in range(SWA_HEADS):
            db = db_ref[h]
            for b in range(REL_BUCKETS):
                s = total(jnp.where(bk == b, db, 0.0))
                out = jnp.where(jnp.logical_and(r == b, c == h), s, out)
            s = jnp.sum(ds_ref[h][0:1, :], axis=1, keepdims=True)
            out = jnp.where(jnp.logical_and(r == REL_BUCKETS, c == h), s, out)
        o_ref[...] = out

    vm = pl.BlockSpec(memory_space=pltpu.VMEM)
    return pl.pallas_call(body, name="swa_small_grads", in_specs=[vm, vm, vm], out_specs=vm,
                          out_shape=_sds((rows, BLOCK), F32))(dbias, dsink, bucket)


def _tile_rows(n):
    for t in (512, 352, 256, 176, 128, 64, 32, 16, 8):
        if n % t == 0:
            return t
    return n


def _cast_rows(x, dtype, name):
    R, C = x.shape
    tr = _tile_rows(R)

    def body(x_ref, o_ref):
        o_ref[...] = x_ref[...].astype(o_ref.dtype)

    return pl.pallas_call(body, name=name, grid=(R // tr,), in_specs=[_rows(tr, C)], out_specs=_rows(tr, C),
                          out_shape=_sds((R, C), dtype), compiler_params=_cp(("parallel",)))(x)


def _pair_sum(g, recv, c, name):
    n, half, C = recv.shape
    tr = _tile_rows(half)

    def body(c_ref, a_ref, b_ref, o_ref):
        o_ref[...] = (a_ref[0] + b_ref[...]).astype(o_ref.dtype)

    return pl.pallas_call(
        body, name=name,
        grid_spec=pltpu.PrefetchScalarGridSpec(
            num_scalar_prefetch=1, grid=(n, half // tr),
            in_specs=[pl.BlockSpec((1, 1, tr, C), lambda j, i, c_ref: (j, c_ref[0], i, 0)),
                      pl.BlockSpec((1, tr, C), lambda j, i, c_ref: (j, i, 0))],
            out_specs=pl.BlockSpec((1, tr, C), lambda j, i, c_ref: (j, i, 0))),
        out_shape=_sds((n, half, C), _MXU),
        compiler_params=_cp(("parallel", "parallel")))(c.reshape(1), g.reshape(n, 2, half, C), recv)


def _chip_sum(own, recv, me, name):
    n, R, C = recv.shape
    tr = _tile_rows(R)

    def body(me_ref, own_ref, recv_ref, o_ref):
        acc = None
        for j in range(n):
            term = jnp.where(me_ref[0] == j, own_ref[0], recv_ref[j]).astype(F32)
            acc = term if acc is None else acc + term
        o_ref[...] = acc

    return pl.pallas_call(
        body, name=name,
        grid_spec=pltpu.PrefetchScalarGridSpec(
            num_scalar_prefetch=1, grid=(R // tr,),
            in_specs=[pl.BlockSpec((1, tr, C), lambda i, me_ref: (me_ref[0], i, 0)),
                      pl.BlockSpec((n, tr, C), lambda i, me_ref: (0, i, 0))],
            out_specs=pl.BlockSpec((tr, C), lambda i, me_ref: (i, 0))),
        out_shape=_sds((R, C), F32), compiler_params=_cp(("parallel",)))(me.reshape(1), own, recv)


def _adamw_math(w, g, m, v):
    m = ADAM_B1 * m + (1.0 - ADAM_B1) * g
    v = ADAM_B2 * v + (1.0 - ADAM_B2) * (g * g)
    m_hat = m / (1.0 - ADAM_B1 ** ADAM_STEP)
    v_hat = v / (1.0 - ADAM_B2 ** ADAM_STEP)
    delta = -ADAM_LR * (m_hat / (jnp.sqrt(v_hat) + ADAM_EPS) + ADAM_WD * w)
    return delta, m, v


def _adamw(w, g, m, v, name):
    R, C = w.shape
    tr = _tile_rows(R)

    def body(w_ref, g_ref, m_ref, v_ref, d_ref, nm_ref, nv_ref):
        d, nm, nv = _adamw_math(w_ref[...], g_ref[...], m_ref[...], v_ref[...])
        d_ref[...] = d
        nm_ref[...] = nm
        nv_ref[...] = nv

    blk = _rows(tr, C)
    return pl.pallas_call(body, name=name, grid=(R // tr,), in_specs=[blk] * 4, out_specs=[blk] * 3,
                          out_shape=[_sds((R, C), F32)] * 3, compiler_params=_cp(("parallel",)))(w, g, m, v)


def _gather_weights(shards):
    nw = len(shards)
    ins, outs, sems = _gather_io(shards)

    def body(*refs):
        ex = _Gather(refs[:nw], refs[nw:2 * nw], *refs[2 * nw:])
        ex.start()
        ex.forward()
        ex.finish()

    any_spec = pl.BlockSpec(memory_space=pl.ANY)
    got = pl.pallas_call(body, name="gather_weights", in_specs=[any_spec] * nw, out_specs=[any_spec] * nw,
                         out_shape=outs, scratch_shapes=sems)(*ins)
    return _gather_assemble(got, shards)


def _swap_halves(grads, name):
    nw = len(grads)
    ins, outs, sems = _swap_io(grads)

    def body(*refs):
        ex = _Swap(refs[:nw], refs[nw:2 * nw], *refs[2 * nw:])
        ex.start()
        ex.finish()

    any_spec = pl.BlockSpec(memory_space=pl.ANY)
    return pl.pallas_call(body, name=name, in_specs=[any_spec] * nw, out_specs=[any_spec] * nw,
                          out_shape=outs, scratch_shapes=sems)(*ins)


def _join_halves(sums):
    nw = len(sums)

    def body(*refs):
        f_refs, out_refs = refs[:nw], refs[nw:2 * nw]
        send_sems, recv_sems = refs[2 * nw:]
        x, y, c, _ = _place()
        ws = range(nw)

        def copy(w, half_index):
            return pltpu.make_async_remote_copy(
                src_ref=f_refs[w], dst_ref=out_refs[w].at[half_index], send_sem=send_sems.at[w],
                recv_sem=recv_sems.at[w], device_id=(x, y, 1 - c), device_id_type=MESH)

        sends = [copy(w, c) for w in ws]
        for cp in sends:
            cp.start()
        for w in ws:
            copy(w, 1 - c).wait_recv()
        for cp in sends:
            cp.wait_send()

    any_spec = pl.BlockSpec(memory_space=pl.ANY)
    outs = pl.pallas_call(
        body, name="join_halves", in_specs=[any_spec] * nw, out_specs=[any_spec] * nw,
        out_shape=[_sds((2,) + f.shape, f.dtype) for f in sums],
        scratch_shapes=[pltpu.SemaphoreType.DMA((nw,)), pltpu.SemaphoreType.DMA((nw,))],
    )(*sums)
    c = lax.axis_index("c")
    return [lax.dynamic_update_slice_in_dim(o, f[None], c, axis=0).reshape(2 * f.shape[0], f.shape[1])
            for o, f in zip(outs, sums)]


def _allreduce_small(block):
    m_per, n = block.shape

    def body(x_ref, sum_ref, loss_ref, all_ref, send_sems, recv_sems, local_sem):
        x, y, c, chips = _place()
        me, sibling = (x, y, c), (x, y, 1 - c)

        def rows(px, py, pc):
            return all_ref.at[pl.ds(pl.multiple_of((4 * px + 2 * py + pc) * m_per, 8), m_per), :]

        def copy(k, blk, to, src=None):
            return pltpu.make_async_remote_copy(
                src_ref=rows(*blk) if src is None else src, dst_ref=rows(*blk), send_sem=send_sems.at[k],
                recv_sem=recv_sems.at[k], device_id=to, device_id_type=MESH)

        mine = pltpu.make_async_copy(x_ref, rows(*me), local_sem)
        mine.start()
        first = [copy(0, me, sibling, src=x_ref)]
        first += [copy(1 + j, me, (*chip, c), src=x_ref) for j, chip in enumerate(chips)]
        for cp in first:
            cp.start()
        passed = [copy(4 + j, (*chip, c), sibling) for j, chip in enumerate(chips)]
        for j, chip in enumerate(chips):
            copy(1 + j, (*chip, c), me).wait_recv()
            passed[j].start()
        copy(0, sibling, me).wait_recv()
        for j, chip in enumerate(chips):
            copy(4 + j, (*chip, 1 - c), me).wait_recv()
        for cp in first + passed:
            cp.wait_send()
        mine.wait()

        acc = all_ref[0:m_per, :]
        for d in range(1, 8):
            acc = acc + all_ref[d * m_per:(d + 1) * m_per, :]
        sum_ref[...] = acc
        tot = jnp.sum(acc[8:9, :], axis=1, keepdims=True) * (0.5 / D_MODEL)
        loss_ref[...] = jnp.broadcast_to(tot, loss_ref.shape)

    vm = pl.BlockSpec(memory_space=pltpu.VMEM)
    return pl.pallas_call(
        body, name="allreduce_small", in_specs=[vm], out_specs=[vm, vm],
        out_shape=[_sds((m_per, n), F32), _sds((8, 128), F32)],
        scratch_shapes=[pltpu.VMEM((8 * m_per, n), F32), pltpu.SemaphoreType.DMA((7,)), pltpu.SemaphoreType.DMA((7,)),
                        pltpu.SemaphoreType.DMA],
    )(block)


def _heads_rows(x, nh):
    S = x.shape[0]
    return x.reshape(S, nh, HEAD_DIM).transpose(1, 0, 2)


def _heads_cols(x, nh):
    S = x.shape[0]
    return x.reshape(S, nh, HEAD_DIM).transpose(1, 2, 0)


def _pad_row(v):
    v = v.reshape(1, -1)
    return jnp.pad(v, ((0, 0), (0, D_MODEL - v.shape[1])))


def _pack_small(ln_in_g, ln_in_b, sb_g, swa_g, sinks, rel_bias, ln1_g, ln1_b, ln2_g, ln2_b, extra):
    rows = [_pad_row(ln_in_g), _pad_row(ln_in_b), jnp.concatenate([sb_g.reshape(1, -1), swa_g.reshape(1, -1)], axis=1),
            _pad_row(jnp.concatenate([rel_bias.reshape(1, -1), sinks.reshape(1, -1)], axis=1)),
            _pad_row(ln1_g), _pad_row(ln1_b), _pad_row(ln2_g), _pad_row(ln2_b), _pad_row(extra)]
    rows.append(jnp.zeros((SMALL_ROWS - len(rows), D_MODEL), F32))
    return jnp.concatenate(rows, axis=0)


def _unpack_small(blk):
    nrb = REL_BUCKETS * SWA_HEADS
    return (blk[0], blk[1], blk[2:3, :SB_WIDTH], blk[2:3, SB_WIDTH:], blk[3:4, nrb:nrb + SWA_HEADS],
            blk[3, :nrb].reshape(REL_BUCKETS, SWA_HEADS), blk[4:5], blk[5:6], blk[6:7], blk[7:8])


def kernel(x, ln_in_g, ln_in_b, w_in, sb_norm_g, swa_norm_g, sinks, rel_bias, w_out, ln1_g, ln1_b, w_gate_up, w_down, ln2_g, ln2_b, loss_target, m_ln_in_g, m_ln_in_b, m_w_in, m_sb_norm_g, m_swa_norm_g, m_sinks, m_rel_bias, m_w_out, m_ln1_g, m_ln1_b, m_w_gate_up, m_w_down, m_ln2_g, m_ln2_b, v_ln_in_g, v_ln_in_b, v_w_in, v_sb_norm_g, v_swa_norm_g, v_sinks, v_rel_bias, v_w_out, v_ln1_g, v_ln1_b, v_w_gate_up, v_w_down, v_ln2_g, v_ln2_b):
    S = x.shape[1]
    x2 = x.reshape(S, D_MODEL)
    tgt = loss_target.reshape(S, D_MODEL)
    T = min(S, SB_TILE)
    bucket = jnp.asarray(_bucket_table().T)
    row = lambda v: v.reshape(1, -1)

    shards = [_cast_rows(w[0], _MXU, "cast_" + n) for n, w in (("w_in", w_in), ("w_out", w_out), ("w_gate_up", w_gate_up), ("w_down", w_down))]
    (w_in_sh,) = _gather_weights(shards[:1])
    w_in_f = jnp.concatenate([w_in_sh[j] for j in range(N_CHIPS)], axis=1)

    h0, h0b, kv_sw, qT_sb, kTb_sb, vTb_sb, kb_sb, vb_sb, qh_sw = _ln_in_proj(x2, row(ln_in_g), row(ln_in_b), w_in_f)
    k_sw, v_sw = kv_sw[:, :SWA_KV_WIDTH], kv_sw[:, SWA_KV_WIDTH:]
    sb_out, rsave, sb_first, (w_out_sh, w_gu_sh, w_down_sh) = _sb_fwd(qT_sb, kb_sb, vTb_sb, shards[1:])
    w_out_f = w_out_sh.reshape(D_MODEL, D_MODEL)
    w_down_f = w_down_sh.reshape(D_FF, D_MODEL)

    bias = _swa_bias(rel_bias, bucket)
    sink_rows = jnp.broadcast_to(sinks.reshape(SWA_HEADS, 1, 1), (SWA_HEADS, 1, BLOCK))
    kh_sw, vh_sw = _heads_rows(k_sw, SWA_KV_HEADS), _heads_rows(v_sw, SWA_KV_HEADS)
    swa_out = _swa_fwd(qh_sw, kh_sw, _heads_cols(v_sw, SWA_KV_HEADS), bias, sink_rows)

    pre1, merged, h1b = _mix_out(sb_out, swa_out, sb_norm_g, swa_norm_g, w_out_f, h0, ln1_g, ln1_b)
    act, silu, dsilu_up = _ffn_up(h1b, w_gu_sh)
    dp2, dp2b, dg2, db2, errsum = _ffn_down_loss(act, w_down_f, pre1, ln1_g, ln1_b, ln2_g, ln2_b, tgt)

    dgate, dup, g_w_down = _ffn_down_bwd(dp2b, w_down_f, silu, dsilu_up, act)
    g_w_gu = _matmul_tn_pair(h1b, dgate, dup, "grad_w_gate_up")
    dp1, dp1b, dg1, db1 = _ffn_up_bwd(dgate, dup, w_gu_sh, dp2, pre1, ln1_g)
    g_w_out = _matmul_tn(merged, dp1b, "grad_w_out", D_MODEL, D_MODEL)
    doT_sb, doh_sw, dgsb, dgsw = _mix_bwd(dp1b, w_out_f, sb_out, swa_out, sb_norm_g, swa_norm_g)

    c = lax.axis_index("c").astype(jnp.int32)
    me = (2 * lax.axis_index("x") + lax.axis_index("y")).astype(jnp.int32)
    grads_a = [g_w_out.reshape(N_CHIPS, D_MODEL // N_CHIPS, D_MODEL), g_w_gu, g_w_down.reshape(N_CHIPS, D_FF // N_CHIPS, D_MODEL)]
    names_a = ("w_out", "w_gate_up", "w_down")
    dq_sw, dkh_sw, dvh_sw, dbias, dsink, swapped_a = _swa_bwd(qh_sw, kh_sw, _heads_cols(k_sw, SWA_KV_HEADS), vh_sw, bias,
                                                               sink_rows, doh_sw, grads_a)
    swa_small = _swa_small_grads(dbias, dsink, bucket)
    partials_a = [_pair_sum(g, r, c, "pair_sum_" + n) for g, r, n in zip(grads_a, swapped_a, names_a)]
    dq_sb, dk_sb, dv_sb, recv_a = _sb_bwd(qT_sb, kb_sb, kTb_sb, vb_sb,
                                             doT_sb, rsave, sb_first, partials_a)
    tok = lambda t, nh: t.reshape(nh, S, HEAD_DIM).transpose(1, 0, 2).reshape(S, nh * HEAD_DIM)
    dproj = [dq_sb, dk_sb, dv_sb, dq_sw,
             jnp.concatenate([tok(dkh_sw, SWA_KV_HEADS), tok(dvh_sw, SWA_KV_HEADS)], axis=1).astype(_MXU)]
    g_w_in = jnp.concatenate([_matmul_tn(h0b, d, "grad_w_in_%d" % k, D_MODEL, d.shape[1]) for k, d in enumerate(dproj)],
                             axis=1)

    cin = IN_COLS // N_CHIPS
    grads_b = [jnp.stack([g_w_in[:, j * cin:(j + 1) * cin] for j in range(N_CHIPS)])]
    partials_b = [_pair_sum(grads_b[0], _swap_halves(grads_b, "swap_halves_in")[0], c, "pair_sum_w_in")]
    grad_x, dg_in, db_in, recv_b = _in_proj_bwd(dproj, w_in_f, dp1, x2, row(ln_in_g), partials_b)
    names = ("w_in",) + names_a
    sums = [_chip_sum(p, r, me, "chip_sum_" + n) for p, r, n in zip(partials_b + partials_a, list(recv_b) + list(recv_a), names)]
    gs_in, gs_out, gs_gu, gs_down = _join_halves(sums)

    nrb = REL_BUCKETS * SWA_HEADS
    small = _pack_small(dg_in, db_in, dgsb, dgsw, swa_small[REL_BUCKETS, :SWA_HEADS],
                        swa_small[:REL_BUCKETS, :SWA_HEADS], dg1, db1, dg2, db2, errsum)
    g_small, loss_tile = _allreduce_small(small)
    loss = loss_tile[0, 0]

    big = []
    for name, w, g, m, v in (("adamw_w_in", w_in, gs_in, m_w_in, v_w_in), ("adamw_w_out", w_out, gs_out, m_w_out, v_w_out),
                             ("adamw_w_gate_up", w_gate_up, gs_gu, m_w_gate_up, v_w_gate_up),
                             ("adamw_w_down", w_down, gs_down, m_w_down, v_w_down)):
        d, nm, nv = _adamw(w[0], g, m[0], v[0], name)
        big.append((g[None], d[None], nm[None], nv[None]))
    zero = jnp.zeros((1,), F32)
    w_small = _pack_small(ln_in_g, ln_in_b, sb_norm_g, swa_norm_g, sinks, rel_bias, ln1_g, ln1_b, ln2_g, ln2_b, zero)
    m_small = _pack_small(m_ln_in_g, m_ln_in_b, m_sb_norm_g, m_swa_norm_g, m_sinks, m_rel_bias, m_ln1_g, m_ln1_b,
                          m_ln2_g, m_ln2_b, zero)
    v_small = _pack_small(v_ln_in_g, v_ln_in_b, v_sb_norm_g, v_swa_norm_g, v_sinks, v_rel_bias, v_ln1_g, v_ln1_b,
                          v_ln2_g, v_ln2_b, zero)
    small_out = [_unpack_small(t) for t in (g_small,) + tuple(_adamw(w_small, g_small, m_small, v_small, "adamw_small"))]

    def kind(k):
        s = small_out[k]
        return [s[0], s[1], big[0][k], s[2], s[3], s[4], s[5], big[1][k], s[6], s[7], big[2][k], big[3][k], s[8], s[9]]

    return (loss, grad_x.reshape(1, S, D_MODEL), *kind(0), *kind(1), *kind(2), *kind(3))
```

```python
import math

import numpy as np
import jax
import jax.numpy as jnp
from jax import lax
from jax.experimental import pallas as pl
from jax.experimental.pallas import tpu as pltpu

F32 = jnp.float32
_MXU = jnp.bfloat16

D_MODEL = 1024
HEAD_DIM = 64
SB_HEADS = 8
SWA_HEADS = 8
SWA_KV_HEADS = 2
SWA_GROUP = SWA_HEADS // SWA_KV_HEADS
SB_WIDTH = SB_HEADS * HEAD_DIM
SWA_WIDTH = SWA_HEADS * HEAD_DIM
SWA_KV_WIDTH = SWA_KV_HEADS * HEAD_DIM
IN_COLS = 3 * SB_WIDTH + SWA_WIDTH + 2 * SWA_KV_WIDTH
BLOCK = 128
REL_BUCKETS = 32
REL_MAX_DIST = 128
D_FF = 2816
FF_CHUNK = D_FF // 2
ALPHA = 2.0 ** 0.25
LN_EPS = 1e-5
RMS_EPS = 1e-6
SCALE = HEAD_DIM ** -0.5
SB_TILE = 256
SB_GROUP_FWD = 8
SB_GROUP_BWD = 4
SB_FORWARD_LEAD = 8
SB_DEAD = -105.0
SWA_SUB = 8

ADAM_LR = 0.001
ADAM_B1 = 0.9
ADAM_B2 = 0.999
ADAM_EPS = 1e-08
ADAM_WD = 0.01
ADAM_STEP = 10

N_CHIPS = 4
SMALL_ROWS = 16

MESH = pl.DeviceIdType.MESH


def _sds(shape, dtype):
    return jax.ShapeDtypeStruct(shape, dtype)


def _cp(sem=None, vmem_mb=48):
    kw = dict(vmem_limit_bytes=vmem_mb * 1024 * 1024)
    if sem is not None:
        kw["dimension_semantics"] = sem
    return pltpu.CompilerParams(**kw)


def _dot(a, b):
    return jnp.dot(a, b, preferred_element_type=F32)


def _dot_nt(a, b):
    return lax.dot_general(a, b, (((1,), (1,)), ((), ())), preferred_element_type=F32)


def _dot_tn(a, b):
    return lax.dot_general(a, b, (((0,), (0,)), ((), ())), preferred_element_type=F32)


def _ln_hat(x):
    mu = jnp.mean(x, axis=-1, keepdims=True)
    xc = x - mu
    var = jnp.mean(xc * xc, axis=-1, keepdims=True)
    rstd = lax.rsqrt(var + LN_EPS)
    return xc * rstd, rstd


def _ln_bwd(xhat, rstd, dy, g):
    dxh = dy * g
    m1 = jnp.mean(dxh, axis=-1, keepdims=True)
    m2 = jnp.mean(dxh * xhat, axis=-1, keepdims=True)
    return rstd * (dxh - m1 - xhat * m2)


def _colsum(x):
    return jnp.sum(x, axis=0, keepdims=True)


def _split2(x):
    hi = x.astype(_MXU)
    lo = (x - hi.astype(F32)).astype(_MXU)
    return hi, lo


def _rows(tm, n):
    return pl.BlockSpec((tm, n), lambda i: (i, 0))


def _fixed(*shape):
    nd = len(shape)
    return pl.BlockSpec(shape, lambda i: (0,) * nd)


IN_SECTIONS = (SB_WIDTH, SB_WIDTH, SB_WIDTH, SWA_WIDTH, 2 * SWA_KV_WIDTH)


def _ln_in_proj(x, g, b, w):
    S = x.shape[0]
    tm = min(S, SB_TILE)
    offs = np.cumsum((0,) + IN_SECTIONS)
    swa = (4,)

    def body(x_ref, g_ref, b_ref, w_ref, h_ref, hb_ref, *o_refs):
        p_refs, (qT_ref, kT_ref, vT_ref, kr_ref, vr_ref, qw_ref) = o_refs[:len(swa)], o_refs[len(swa):]
        xhat, _ = _ln_hat(x_ref[...])
        h = xhat * g_ref[...] + b_ref[...]
        h_ref[...] = h
        hb = h.astype(_MXU)
        hb_ref[...] = hb
        proj = _dot(hb, w_ref[...])
        for k, p_ref in zip(swa, p_refs):
            p_ref[...] = proj[:, offs[k]:offs[k + 1]].astype(p_ref.dtype)
        heads = lambda k: proj[:, offs[k]:offs[k + 1]].T.astype(_MXU).reshape(SB_HEADS, HEAD_DIM, tm)
        qT_ref[...] = heads(0)
        kT_ref[:, 0] = heads(1)
        vT_ref[:, 0] = heads(2)
        for hd in range(SB_HEADS):
            cols = slice(hd * HEAD_DIM, (hd + 1) * HEAD_DIM)
            kr_ref[hd, 0] = proj[:, offs[1]:offs[2]][:, cols].astype(_MXU)
            vr_ref[hd, 0] = proj[:, offs[2]:offs[3]][:, cols].astype(_MXU)
            qw_ref[hd] = proj[:, offs[3]:offs[4]][:, cols].astype(_MXU)

    blocked = pl.BlockSpec((SB_HEADS, 1, HEAD_DIM, tm), lambda i: (0, i, 0, 0))
    blocked_rows = pl.BlockSpec((SB_HEADS, 1, tm, HEAD_DIM), lambda i: (0, i, 0, 0))
    return pl.pallas_call(
        body, name="ln_in_proj", grid=(S // tm,),
        in_specs=[_rows(tm, D_MODEL), _fixed(1, D_MODEL), _fixed(1, D_MODEL), _fixed(D_MODEL, IN_COLS)],
        out_specs=[_rows(tm, D_MODEL), _rows(tm, D_MODEL)] + [_rows(tm, IN_SECTIONS[k]) for k in swa]
                  + [pl.BlockSpec((SB_HEADS, HEAD_DIM, tm), lambda i: (0, 0, i)), blocked, blocked, blocked_rows,
                     blocked_rows, pl.BlockSpec((SWA_HEADS, tm, HEAD_DIM), lambda i: (0, i, 0))],
        out_shape=[_sds((S, D_MODEL), F32), _sds((S, D_MODEL), _MXU)] + [_sds((S, IN_SECTIONS[k]), _MXU) for k in swa]
                  + [_sds((SB_HEADS, HEAD_DIM, S), _MXU), _sds((SB_HEADS, S // tm, HEAD_DIM, tm), _MXU),
                     _sds((SB_HEADS, S // tm, HEAD_DIM, tm), _MXU), _sds((SB_HEADS, S // tm, tm, HEAD_DIM), _MXU),
                     _sds((SB_HEADS, S // tm, tm, HEAD_DIM), _MXU), _sds((SWA_HEADS, S, HEAD_DIM), _MXU)],
        compiler_params=_cp(("parallel",)),
    )(x, g, b, w)


def _rms(x, g):
    r = lax.rsqrt(jnp.mean(x * x, axis=-1, keepdims=True) + RMS_EPS)
    return x * r * g, r


def _mix_out(sb, sw, gsb, gsw, w_out, h0, g1, b1):
    S = sb.shape[0]
    tm = min(S, 512)

    def body(sb_ref, sw_ref, gsb_ref, gsw_ref, w_ref, h0_ref, g1_ref, b1_ref, pre_ref, mg_ref, h1_ref):
        ysb, _ = _rms(sb_ref[...], gsb_ref[...])
        ysw, _ = _rms(sw_ref[...], gsw_ref[...])
        ysb = ysb.astype(_MXU)
        ysw = ysw.astype(_MXU)
        mg_ref[:, :SB_WIDTH] = ysb
        mg_ref[:, SB_WIDTH:] = ysw
        mix = _dot(ysb, w_ref[:SB_WIDTH, :]) + _dot(ysw, w_ref[SB_WIDTH:, :])
        pre1 = ALPHA * h0_ref[...] + mix
        pre_ref[...] = pre1
        xhat, _ = _ln_hat(pre1)
        h1_ref[...] = (xhat * g1_ref[...] + b1_ref[...]).astype(h1_ref.dtype)

    vec = _fixed(1, D_MODEL)
    return pl.pallas_call(
        body, name="mix_out", grid=(S // tm,),
        in_specs=[_rows(tm, SB_WIDTH), _rows(tm, SWA_WIDTH), _fixed(1, SB_WIDTH), _fixed(1, SWA_WIDTH),
                  _fixed(D_MODEL, D_MODEL), _rows(tm, D_MODEL), vec, vec],
        out_specs=[_rows(tm, D_MODEL), _rows(tm, D_MODEL), _rows(tm, D_MODEL)],
        out_shape=[_sds((S, D_MODEL), F32), _sds((S, D_MODEL), _MXU), _sds((S, D_MODEL), _MXU)],
        compiler_params=_cp(("parallel",)),
    )(sb, sw, gsb, gsw, w_out, h0, g1, b1)


def _sigmoid(x):
    return 1.0 / (1.0 + jnp.exp(-x))


def _ffn_up(h1b, wgu):
    S = h1b.shape[0]
    tm = min(S, 1024)

    def body(h_ref, wg_ref, wu_ref, a_ref, s1_ref, s2_ref):
        h1 = h_ref[...]
        gate = _dot(h1, wg_ref[0])
        up = _dot(h1, wu_ref[0])
        sg = _sigmoid(gate)
        silu = gate * sg
        a_ref[...] = (silu * up).astype(a_ref.dtype)
        s1_ref[...] = silu.astype(s1_ref.dtype)
        s2_ref[...] = (up * (sg * (1.0 + gate * (1.0 - sg)))).astype(s2_ref.dtype)

    chunk = pl.BlockSpec((tm, FF_CHUNK), lambda j, i: (i, j))
    return pl.pallas_call(
        body, name="ffn_up", grid=(2, S // tm),
        in_specs=[pl.BlockSpec((tm, D_MODEL), lambda j, i: (i, 0)),
                  pl.BlockSpec((1, D_MODEL, FF_CHUNK), lambda j, i: (j, 0, 0)),
                  pl.BlockSpec((1, D_MODEL, FF_CHUNK), lambda j, i: (j + 2, 0, 0))],
        out_specs=[chunk, chunk, chunk],
        out_shape=[_sds((S, D_FF), _MXU)] * 3,
        compiler_params=_cp(("arbitrary", "arbitrary"), vmem_mb=56),
    )(h1b, wgu, wgu)


def _ffn_down_loss(a, w_down, pre1, g1, b1, g2, b2, tgt):
    S = a.shape[0]
    tm = min(S, 512)

    def body(a_ref, w_ref, p_ref, g1_ref, b1_ref, g2_ref, b2_ref, t_ref, d_ref, db_ref, dg2_ref, db2_ref, err_ref):
        @pl.when(pl.program_id(0) == 0)
        def _():
            dg2_ref[...] = jnp.zeros_like(dg2_ref)
            db2_ref[...] = jnp.zeros_like(db2_ref)
            err_ref[...] = jnp.zeros_like(err_ref)

        xhat1, _ = _ln_hat(p_ref[...])
        h1 = xhat1 * g1_ref[...] + b1_ref[...]
        pre2 = ALPHA * h1 + _dot(a_ref[...], w_ref[...])
        xhat2, rstd2 = _ln_hat(pre2)
        err = xhat2 * g2_ref[...] + b2_ref[...] - t_ref[...]
        dh2 = err * (1.0 / D_MODEL)
        dp2 = _ln_bwd(xhat2, rstd2, dh2, g2_ref[...])
        d_ref[...] = dp2
        db_ref[...] = dp2.astype(db_ref.dtype)
        dg2_ref[...] += _colsum(dh2 * xhat2)
        db2_ref[...] += _colsum(dh2)
        err_ref[...] += _colsum(err * err)

    vec = _fixed(1, D_MODEL)
    return pl.pallas_call(
        body, name="ffn_down_loss", grid=(S // tm,),
        in_specs=[_rows(tm, D_FF), _fixed(D_FF, D_MODEL), _rows(tm, D_MODEL), vec, vec, vec, vec, _rows(tm, D_MODEL)],
        out_specs=[_rows(tm, D_MODEL), _rows(tm, D_MODEL), vec, vec, vec],
        out_shape=[_sds((S, D_MODEL), F32), _sds((S, D_MODEL), _MXU), _sds((1, D_MODEL), F32), _sds((1, D_MODEL), F32),
                   _sds((1, D_MODEL), F32)],
        compiler_params=_cp(("arbitrary",)),
    )(a, w_down, pre1, g1, b1, g2, b2, tgt)


def _ffn_down_bwd(dp2b, w_down, s1, s2, act):
    S = dp2b.shape[0]
    tm = min(S, 512)

    def body(d_ref, w_ref, s1_ref, s2_ref, a_ref, dg_ref, du_ref, gw_ref):
        @pl.when(pl.program_id(1) == 0)
        def _():
            gw_ref[...] = jnp.zeros_like(gw_ref)

        d = d_ref[...]
        da = _dot_nt(d, w_ref[...])
        gw_ref[...] += _dot_tn(a_ref[...], d)
        du_ref[...] = (da * s1_ref[...].astype(F32)).astype(du_ref.dtype)
        dg_ref[...] = (da * s2_ref[...].astype(F32)).astype(dg_ref.dtype)

    chunk = pl.BlockSpec((tm, FF_CHUNK), lambda j, i: (i, j))
    return pl.pallas_call(
        body, name="ffn_down_bwd", grid=(2, S // tm),
        in_specs=[pl.BlockSpec((tm, D_MODEL), lambda j, i: (i, 0)),
                  pl.BlockSpec((FF_CHUNK, D_MODEL), lambda j, i: (j, 0)), chunk, chunk, chunk],
        out_specs=[chunk, chunk, pl.BlockSpec((FF_CHUNK, D_MODEL), lambda j, i: (j, 0))],
        out_shape=[_sds((S, D_FF), _MXU), _sds((S, D_FF), _MXU), _sds((D_FF, D_MODEL), F32)],
        compiler_params=_cp(("arbitrary", "arbitrary")),
    )(dp2b, w_down, s1, s2, act)


def _ffn_up_bwd(dgate, dup, wgu, dp2, pre1, g1):
    S = dgate.shape[0]
    tm = min(S, 256)

    def body(dg_ref, du_ref, w_ref, d2_ref, p_ref, g_ref, d1_ref, d1b_ref, dg1_ref, db1_ref):
        @pl.when(pl.program_id(0) == 0)
        def _():
            dg1_ref[...] = jnp.zeros_like(dg1_ref)
            db1_ref[...] = jnp.zeros_like(db1_ref)

        dh1 = ALPHA * d2_ref[...]
        for j in range(2):
            cols = slice(j * FF_CHUNK, (j + 1) * FF_CHUNK)
            dh1 += _dot_nt(dg_ref[:, cols], w_ref[j])
            dh1 += _dot_nt(du_ref[:, cols], w_ref[j + 2])
        xhat, rstd = _ln_hat(p_ref[...])
        dp1 = _ln_bwd(xhat, rstd, dh1, g_ref[...])
        d1_ref[...] = dp1
        d1b_ref[...] = dp1.astype(d1b_ref.dtype)
        dg1_ref[...] += _colsum(dh1 * xhat)
        db1_ref[...] += _colsum(dh1)

    vec = _fixed(1, D_MODEL)
    return pl.pallas_call(
        body, name="ffn_up_bwd", grid=(S // tm,),
        in_specs=[_rows(tm, D_FF), _rows(tm, D_FF), _fixed(4, D_MODEL, FF_CHUNK), _rows(tm, D_MODEL),
                  _rows(tm, D_MODEL), vec],
        out_specs=[_rows(tm, D_MODEL), _rows(tm, D_MODEL), vec, vec],
        out_shape=[_sds((S, D_MODEL), F32), _sds((S, D_MODEL), _MXU), _sds((1, D_MODEL), F32), _sds((1, D_MODEL), F32)],
        compiler_params=_cp(("arbitrary",), vmem_mb=56),
    )(dgate, dup, wgu, dp2, pre1, g1)


def _rms_bwd(x, g, dy):
    n = x.shape[-1]
    r = lax.rsqrt(jnp.mean(x * x, axis=-1, keepdims=True) + RMS_EPS)
    u = dy * g
    dx = r * u - x * (r * r * r) * (jnp.sum(u * x, axis=-1, keepdims=True) * (1.0 / n))
    return dx, _colsum(dy * x * r)


def _mix_bwd(dp1b, w_out, sb, sw, gsb, gsw, merged):
    S = sb.shape[0]
    tm = min(S, 512)

    def body(d_ref, w_ref, sb_ref, sw_ref, gsb_ref, gsw_ref, m_ref, dsb_ref, dsw_ref, dgsb_ref, dgsw_ref, gw_ref):
        @pl.when(pl.program_id(0) == 0)
        def _():
            dgsb_ref[...] = jnp.zeros_like(dgsb_ref)
            dgsw_ref[...] = jnp.zeros_like(dgsw_ref)
            gw_ref[...] = jnp.zeros_like(gw_ref)

        d = d_ref[...]
        gw_ref[...] += _dot_tn(m_ref[...], d)
        dm = _dot_nt(d, w_ref[...])
        dsb, dgsb = _rms_bwd(sb_ref[...], gsb_ref[...], dm[:, :SB_WIDTH])
        dsw, dgsw = _rms_bwd(sw_ref[...], gsw_ref[...], dm[:, SB_WIDTH:])
        dsb_ref[...] = dsb.T.astype(dsb_ref.dtype).reshape(dsb_ref.shape)
        for hd in range(SWA_HEADS):
            dsw_ref[hd] = dsw[:, hd * HEAD_DIM:(hd + 1) * HEAD_DIM].astype(dsw_ref.dtype)
        dgsb_ref[...] += dgsb
        dgsw_ref[...] += dgsw

    return pl.pallas_call(
        body, name="mix_bwd", grid=(S // tm,),
        in_specs=[_rows(tm, D_MODEL), _fixed(D_MODEL, D_MODEL), _rows(tm, SB_WIDTH), _rows(tm, SWA_WIDTH),
                  _fixed(1, SB_WIDTH), _fixed(1, SWA_WIDTH), _rows(tm, D_MODEL)],
        out_specs=[pl.BlockSpec((SB_HEADS, HEAD_DIM, tm), lambda i: (0, 0, i)),
                   pl.BlockSpec((SWA_HEADS, tm, HEAD_DIM), lambda i: (0, i, 0)), _fixed(1, SB_WIDTH), _fixed(1, SWA_WIDTH),
                   _fixed(D_MODEL, D_MODEL)],
        out_shape=[_sds((SB_HEADS, HEAD_DIM, S), _MXU), _sds((SWA_HEADS, S, HEAD_DIM), _MXU), _sds((1, SB_WIDTH), F32),
                   _sds((1, SWA_WIDTH), F32), _sds((D_MODEL, D_MODEL), F32)],
        compiler_params=_cp(("arbitrary",)),
    )(dp1b, w_out, sb, sw, gsb, gsw, merged)


def _in_proj_bwd(dproj, w_in, dp1, x, g, parts):
    S = x.shape[0]
    tm = min(S, 512)
    nw = len(parts)
    ns = len(IN_SECTIONS)
    offs = np.cumsum((0,) + IN_SECTIONS)
    s_ins, s_outs, s_sems = _scatter_io(parts)

    def body(*refs):
        dpj_refs = refs[:ns]
        w_ref, d1_ref, x_ref, g_ref = refs[ns:ns + 4]
        rest = refs[ns + 4:]
        gx_ref, dg_ref, db_ref = rest[nw:nw + 3]
        scatter = _Scatter(rest[:nw], rest[nw + 3:2 * nw + 3], *rest[2 * nw + 3:])

        @pl.when(pl.program_id(0) == 0)
        def _():
            scatter.start()
            dg_ref[...] = jnp.zeros_like(dg_ref)
            db_ref[...] = jnp.zeros_like(db_ref)

        dh0 = ALPHA * d1_ref[...]
        for k in range(ns):
            dh0 += _dot_nt(dpj_refs[k][...], w_ref[:, offs[k]:offs[k + 1]])
        xhat, rstd = _ln_hat(x_ref[...])
        gx_ref[...] = _ln_bwd(xhat, rstd, dh0, g_ref[...])
        dg_ref[...] += _colsum(dh0 * xhat)
        db_ref[...] += _colsum(dh0)

        @pl.when(pl.program_id(0) == pl.num_programs(0) - 1)
        def _():
            scatter.finish()

    vec = _fixed(1, D_MODEL)
    any_spec = pl.BlockSpec(memory_space=pl.ANY)
    res = pl.pallas_call(
        body, name="in_proj_bwd", grid=(S // tm,),
        in_specs=[_rows(tm, n) for n in IN_SECTIONS]
                 + [_fixed(D_MODEL, IN_COLS), _rows(tm, D_MODEL), _rows(tm, D_MODEL), vec] + [any_spec] * nw,
        out_specs=[_rows(tm, D_MODEL), vec, vec] + [any_spec] * nw,
        out_shape=[_sds((S, D_MODEL), F32), _sds((1, D_MODEL), F32), _sds((1, D_MODEL), F32)] + s_outs,
        scratch_shapes=s_sems,
        compiler_params=_cp(("arbitrary",)),
    )(*dproj, w_in, dp1, x, g, *s_ins)
    return res[0], res[1], res[2], list(res[3:])


def _matmul_tn(a, b, name, tk, tn):
    T, K = a.shape
    N = b.shape[1]
    tt = min(T, 1024)

    def body(a_ref, b_ref, o_ref):
        @pl.when(pl.program_id(2) == 0)
        def _():
            o_ref[...] = jnp.zeros_like(o_ref)

        o_ref[...] += _dot_tn(a_ref[...], b_ref[...])

    return pl.pallas_call(
        body, name=name, grid=(K // tk, N // tn, T // tt),
        in_specs=[pl.BlockSpec((tt, tk), lambda k, n, t: (t, k)), pl.BlockSpec((tt, tn), lambda k, n, t: (t, n))],
        out_specs=pl.BlockSpec((tk, tn), lambda k, n, t: (k, n)),
        out_shape=_sds((K, N), F32),
        compiler_params=_cp(("parallel", "parallel", "arbitrary")),
    )(a, b)


def _place():
    x, y, c = lax.axis_index("x"), lax.axis_index("y"), lax.axis_index("c")
    chips = [(1 - x, y), (x, 1 - y), (1 - x, 1 - y)]
    return x, y, c, chips


class _Gather:
    def __init__(self, in_refs, out_refs, send_sems, recv_sems):
        self.in_refs, self.out_refs, self.send_sems, self.recv_sems = in_refs, out_refs, send_sems, recv_sems
        self.x, self.y, self.c, self.chips = _place()

    def _copy(self, w, k, chip, hc, to, src=None):
        part = self.out_refs[w].at[2 * chip[0] + chip[1], hc]
        return pltpu.make_async_remote_copy(
            src_ref=part if src is None else src, dst_ref=part, send_sem=self.send_sems.at[w, k],
            recv_sem=self.recv_sems.at[w, k], device_id=to, device_id_type=MESH)

    def _first(self):
        x, y, c = self.x, self.y, self.c
        return [self._copy(w, j, (x, y), c, (*chip, c), src=self.in_refs[w].at[c])
                for w in range(len(self.in_refs)) for j, chip in enumerate(self.chips)]

    def start(self):
        for cp in self._first():
            cp.start()

    def _passed(self):
        sibling = (self.x, self.y, 1 - self.c)
        return [self._copy(w, 3 + j, chip, self.c, sibling)
                for w in range(len(self.in_refs)) for j, chip in enumerate(self.chips)]

    def forward(self):
        me = (self.x, self.y, self.c)
        passed = self._passed()
        for w in range(len(self.in_refs)):
            for j, chip in enumerate(self.chips):
                self._copy(w, j, chip, self.c, me).wait_recv()
                passed[3 * w + j].start()

    def finish(self):
        me = (self.x, self.y, self.c)
        for w in range(len(self.in_refs)):
            for j, chip in enumerate(self.chips):
                self._copy(w, 3 + j, chip, 1 - self.c, me).wait_recv()
        for cp in self._first() + self._passed():
            cp.wait_send()


def _gather_io(shards):
    halves = [(s.shape[0] // 2, s.shape[1]) for s in shards]
    ins = [s.reshape(2, h, cols) for s, (h, cols) in zip(shards, halves)]
    outs = [_sds((N_CHIPS, 2, h, cols), s.dtype) for s, (h, cols) in zip(shards, halves)]
    sems = [pltpu.SemaphoreType.DMA((len(shards), 6)), pltpu.SemaphoreType.DMA((len(shards), 6))]
    return ins, outs, sems


def _gather_assemble(outs, shards):
    me = 2 * lax.axis_index("x") + lax.axis_index("y")
    return [lax.dynamic_update_slice_in_dim(o.reshape((N_CHIPS,) + s.shape), s[None], me, axis=0)
            for o, s in zip(outs, shards)]


class _Scatter:
    def __init__(self, p_refs, out_refs, send_sems, recv_sems):
        self.p_refs, self.out_refs, self.send_sems, self.recv_sems = p_refs, out_refs, send_sems, recv_sems
        self.x, self.y, self.c, self.chips = _place()
        self.me = 2 * self.x + self.y

    def _copy(self, w, j, chip, src_chip, dst_chip):
        return pltpu.make_async_remote_copy(
            src_ref=self.p_refs[w].at[src_chip], dst_ref=self.out_refs[w].at[dst_chip], send_sem=self.send_sems.at[w, j],
            recv_sem=self.recv_sems.at[w, j], device_id=(*chip, self.c), device_id_type=MESH)

    def _sends(self):
        return [self._copy(w, j, chip, 2 * chip[0] + chip[1], self.me)
                for w in range(len(self.p_refs)) for j, chip in enumerate(self.chips)]

    def start(self):
        for cp in self._sends():
            cp.start()

    def finish(self):
        for w in range(len(self.p_refs)):
            for j, chip in enumerate(self.chips):
                self._copy(w, j, chip, self.me, 2 * chip[0] + chip[1]).wait_recv()
        for cp in self._sends():
            cp.wait_send()


def _scatter_io(parts):
    sems = [pltpu.SemaphoreType.DMA((len(parts), 3)), pltpu.SemaphoreType.DMA((len(parts), 3))]
    return list(parts), [_sds(p.shape, p.dtype) for p in parts], sems


class _Swap:
    def __init__(self, g_refs, out_refs, send_sems, recv_sems):
        x, y, c, _ = _place()
        self.copies = []
        for w in range(len(g_refs)):
            half = out_refs[w].shape[1]
            theirs = g_refs[w].at[:, pl.ds(pl.multiple_of((1 - c) * half, 8), half), :]
            self.copies.append(pltpu.make_async_remote_copy(
                src_ref=theirs, dst_ref=out_refs[w], send_sem=send_sems.at[w], recv_sem=recv_sems.at[w],
                device_id=(x, y, 1 - c), device_id_type=MESH))

    def start(self):
        for cp in self.copies:
            cp.start()

    def finish(self):
        for cp in self.copies:
            cp.wait()


def _swap_io(grads):
    outs = [_sds((g.shape[0], g.shape[1] // 2, g.shape[2]), g.dtype) for g in grads]
    return list(grads), outs, [pltpu.SemaphoreType.DMA((len(grads),)), pltpu.SemaphoreType.DMA((len(grads),))]


def _matmul_tn_pair(a, b0, b1, name):
    T, K = a.shape
    tt = min(T, 1024)

    def body(a_ref, b0_ref, b1_ref, o_ref):
        n = pl.program_id(0)

        @pl.when(pl.program_id(1) == 0)
        def _():
            o_ref[...] = jnp.zeros_like(o_ref)

        @pl.when(n < 2)
        def _():
            o_ref[0] += _dot_tn(a_ref[...], b0_ref[...])

        @pl.when(n >= 2)
        def _():
            o_ref[0] += _dot_tn(a_ref[...], b1_ref[...])

    return pl.pallas_call(
        body, name=name, grid=(4, T // tt),
        in_specs=[pl.BlockSpec((tt, K), lambda n, t: (t, 0)),
                  pl.BlockSpec((tt, FF_CHUNK), lambda n, t: (t, jnp.minimum(n, 1))),
                  pl.BlockSpec((tt, FF_CHUNK), lambda n, t: (t, jnp.maximum(n - 2, 0)))],
        out_specs=pl.BlockSpec((1, K, FF_CHUNK), lambda n, t: (n, 0, 0)),
        out_shape=_sds((4, K, FF_CHUNK), F32),
        compiler_params=_cp(("parallel", "arbitrary")),
    )(a, b0, b1)


def _sb_logs(zt, causal):
    e = jnp.exp(-jnp.abs(zt))
    lb = jnp.minimum(zt, 0.0) - jnp.log(1.0 + e)
    l1m = lb - zt
    if causal is not None:
        l1m = jnp.where(causal, l1m, 0.0)
    return lb, l1m


def _sb_weights(lb, suf, causal):
    a = jnp.exp(lb + suf)
    if causal is not None:
        a = jnp.where(causal, a, 0.0)
    return a


def _tri_masks(t):
    r = lax.broadcasted_iota(jnp.int32, (t, t), 0)
    c = lax.broadcasted_iota(jnp.int32, (t, t), 1)
    return r, c


def _sb_fwd(qT, kb, vTb, shards):
    Hh, _, S = qT.shape
    nk, T = kb.shape[1], kb.shape[2]
    nq = S // T
    G = SB_GROUP_FWD
    nw = len(shards)
    g_ins, g_outs, g_sems = _gather_io(shards)
    forward_step = max(nq - 1 - SB_FORWARD_LEAD, 0)

    def body(qT_ref, k_ref, vT_ref, *rest):
        o_ref, rs_ref, first_ref = rest[nw:nw + 3]
        gather = _Gather(rest[:nw], rest[nw + 3:2 * nw + 3], *rest[2 * nw + 3:])
        i = pl.program_id(1)
        first_step = jnp.logical_and(pl.program_id(0) == 0, i == 0)
        last_step = jnp.logical_and(pl.program_id(0) == pl.num_programs(0) - 1, i == pl.num_programs(1) - 1)

        @pl.when(first_step)
        def _():
            gather.start()

        qts = [(qT_ref[g].astype(F32) * SCALE).astype(_MXU) for g in range(G)]
        r, c = _tri_masks(T)
        upper = (c > r).astype(_MXU)
        causal = r < c

        def blk(j, carry, mask):
            hs = range(G)
            for g in hs:
                rs_ref[g, 0, j] = jnp.broadcast_to(carry[g][0], (8, T))
            zs = [_dot(k_ref[g, j], qts[g]) for g in hs]
            lbs, l1ms = zip(*[_sb_logs(zs[g], mask) for g in hs])
            splits = [_split2(l1ms[g]) for g in hs]
            cums = [_dot(upper, splits[g][0]) + _dot(upper, splits[g][1]) for g in hs]
            avs = [_sb_weights(lbs[g], carry[g][0] + cums[g], mask).astype(_MXU) for g in hs]
            accs = [carry[g][1] + _dot(vT_ref[g, j], avs[g]) for g in hs]
            return tuple((carry[g][0] + _colsum(l1ms[g]), accs[g]) for g in hs)

        def go_on(j, carry):
            top = carry[0][0]
            for g in range(1, G):
                top = jnp.maximum(top, carry[g][0])
            return jnp.logical_and(j >= 0, jnp.max(top) >= SB_DEAD)

        init = tuple((jnp.zeros((1, T), F32), jnp.zeros((HEAD_DIM, T), F32)) for _ in range(G))
        carry = blk(i, init, causal)
        j, carry = lax.while_loop(lambda st: go_on(*st), lambda st: (st[0] - 1, blk(st[0], st[1], None)),
                                  (i - 1, carry))

        first_ref[...] = jnp.broadcast_to((j + 1).astype(F32), first_ref.shape)

        o_ref[...] = jnp.concatenate([carry[g][1] for g in range(G)], axis=0).T

        @pl.when(jnp.logical_and(pl.program_id(0) == pl.num_programs(0) - 1, i == forward_step))
        def _():
            gather.forward()

        @pl.when(last_step)
        def _():
            gather.finish()

    any_spec = pl.BlockSpec(memory_space=pl.ANY)
    res = pl.pallas_call(
        body, name="sb_fwd", grid=(Hh // G, nq),
        in_specs=[pl.BlockSpec((G, HEAD_DIM, T), lambda h, i: (h, 0, i)),
                  pl.BlockSpec((G, nk, T, HEAD_DIM), lambda h, i: (h, 0, 0, 0), pipeline_mode=pl.Buffered(1)),
                  pl.BlockSpec((G, nk, HEAD_DIM, T), lambda h, i: (h, 0, 0, 0), pipeline_mode=pl.Buffered(1))]
                 + [any_spec] * nw,
        out_specs=[pl.BlockSpec((T, G * HEAD_DIM), lambda h, i: (i, h)),
                   pl.BlockSpec((G, 1, nk, 8, T), lambda h, i: (h, i, 0, 0, 0)),
                   pl.BlockSpec((1, 1, 8, 128), lambda h, i: (h, i, 0, 0))] + [any_spec] * nw,
        out_shape=[_sds((S, Hh * HEAD_DIM), F32), _sds((Hh, nq, nk, 8, T), F32), _sds((Hh // G, nq, 8, 128), F32)]
                  + g_outs,
        scratch_shapes=g_sems,
        compiler_params=_cp(("arbitrary", "arbitrary")),
    )(qT, kb, vTb, *g_ins)
    return res[0], res[1], res[2], _gather_assemble(res[3:], shards)


def _sb_bwd(qT, kb, kTb, vb, doT, rsave, first, parts):
    Hh, _, S = qT.shape
    nk, T = kb.shape[1], kb.shape[2]
    nq = S // T
    G = SB_GROUP_BWD
    nw = len(parts)
    s_ins, s_outs, s_sems = _scatter_io(parts)

    def body(qT_ref, k_ref, kT_ref, v_ref, doT_ref, rs_ref, first_ref, *rest):
        dq_ref, dk_out_ref, dv_out_ref = rest[nw:nw + 3]
        dk_ref, dv_ref = rest[2 * nw + 3:2 * nw + 5]
        scatter = _Scatter(rest[:nw], rest[nw + 3:2 * nw + 3], *rest[2 * nw + 5:])
        i = pl.program_id(1)
        first_step = jnp.logical_and(pl.program_id(0) == 0, i == 0)
        last_step = jnp.logical_and(pl.program_id(0) == pl.num_programs(0) - 1, i == pl.num_programs(1) - 1)

        @pl.when(first_step)
        def _():
            scatter.start()

        @pl.when(i == 0)
        def _():
            dk_ref[...] = jnp.zeros_like(dk_ref)
            dv_ref[...] = jnp.zeros_like(dv_ref)

        qts = [(qT_ref[g].astype(F32) * SCALE).astype(_MXU) for g in range(G)]
        douts = [doT_ref[g] for g in range(G)]
        r, c = _tri_masks(T)
        upper = (c > r).astype(_MXU)
        lower = (c < r).astype(_MXU)
        causal = r < c

        def blk(j, carry, mask):
            hs = range(G)
            zs = [_dot(k_ref[g, j], qts[g]) for g in hs]
            das = [_dot(v_ref[g, j], douts[g]) for g in hs]
            lbs, l1ms = zip(*[_sb_logs(zs[g], mask) for g in hs])
            splits = [_split2(l1ms[g]) for g in hs]
            cums = [_dot(upper, splits[g][0]) + _dot(upper, splits[g][1]) for g in hs]
            avs = [_sb_weights(lbs[g], rs_ref[g, 0, j][0:1, :] + cums[g], mask) for g in hs]
            ets = [das[g] * avs[g] for g in hs]
            esplits = [_split2(ets[g]) for g in hs]
            ecums = [_dot(lower, esplits[g][0]) + _dot(lower, esplits[g][1]) for g in hs]
            dzs = []
            for g in hs:
                sig = jnp.exp(lbs[g])
                dz = ets[g] * (1.0 - sig) - (carry[g][0] + ecums[g]) * sig
                if mask is not None:
                    dz = jnp.where(mask, dz, 0.0)
                dzs.append(dz.astype(_MXU))
            dqs = [carry[g][1] + _dot(kT_ref[g, j], dzs[g]) for g in hs]
            for g in hs:
                dk_ref[j, g * HEAD_DIM:(g + 1) * HEAD_DIM, :] += _dot_nt(qts[g], dzs[g])
            for g in hs:
                dv_ref[j, g * HEAD_DIM:(g + 1) * HEAD_DIM, :] += _dot_nt(douts[g], avs[g].astype(_MXU))
            return tuple((carry[g][0] + _colsum(ets[g]), dqs[g]) for g in hs)

        first = jnp.clip(jnp.max(first_ref[0, 0][0:1, 0:1]).astype(jnp.int32), 0, i)
        carry = tuple((jnp.zeros((1, T), F32), jnp.zeros((HEAD_DIM, T), F32)) for _ in range(G))
        carry = lax.fori_loop(first, i, lambda s, cr: blk(s, cr, None), carry)
        carry = blk(i, carry, causal)
        dq_ref[...] = (jnp.concatenate([carry[g][1] for g in range(G)], axis=0) * SCALE).T.astype(dq_ref.dtype)

        @pl.when(i == pl.num_programs(1) - 1)
        def _():
            def flush(j, _):
                rows = pl.ds(pl.multiple_of(j * T, T), T)
                dk_out_ref[rows, :] = dk_ref[j].T.astype(dk_out_ref.dtype)
                dv_out_ref[rows, :] = dv_ref[j].T.astype(dv_out_ref.dtype)
                return 0
            lax.fori_loop(0, nk, flush, 0)

        @pl.when(last_step)
        def _():
            scatter.finish()

    colblk = pl.BlockSpec((G, HEAD_DIM, T), lambda h, i: (h, 0, i))
    once = pl.Buffered(1)
    kblk = pl.BlockSpec((G, nk, T, HEAD_DIM), lambda h, i: (h, 0, 0, 0), pipeline_mode=once)
    kTblk = pl.BlockSpec((G, nk, HEAD_DIM, T), lambda h, i: (h, 0, 0, 0), pipeline_mode=once)
    any_spec = pl.BlockSpec(memory_space=pl.ANY)
    res = pl.pallas_call(
        body, name="sb_bwd", grid=(Hh // G, nq),
        in_specs=[colblk, kblk, kTblk, kblk, colblk,
                  pl.BlockSpec((G, 1, nk, 8, T), lambda h, i: (h, i, 0, 0, 0)),
                  pl.BlockSpec((1, 1, 8, 128), lambda h, i: ((h * G) // SB_GROUP_FWD, i, 0, 0))] + [any_spec] * nw,
        out_specs=[pl.BlockSpec((T, G * HEAD_DIM), lambda h, i: (i, h)),
                   pl.BlockSpec((S, G * HEAD_DIM), lambda h, i: (0, h), pipeline_mode=once),
                   pl.BlockSpec((S, G * HEAD_DIM), lambda h, i: (0, h), pipeline_mode=once)] + [any_spec] * nw,
        out_shape=[_sds((S, Hh * HEAD_DIM), _MXU)] * 3 + s_outs,
        scratch_shapes=[pltpu.VMEM((nk, G * HEAD_DIM, T), F32), pltpu.VMEM((nk, G * HEAD_DIM, T), F32)] + s_sems,
        compiler_params=_cp(("arbitrary", "arbitrary"), vmem_mb=60),
    )(qT, kb, kTb, vb, doT, rsave, first, *s_ins)
    return res[0], res[1], res[2], list(res[3:])


def _bucket_table():
    qi = np.arange(BLOCK)[:, None]
    cj = np.arange(2 * BLOCK)[None, :]
    dist = qi + BLOCK - cj
    exact = REL_BUCKETS // 2
    d = np.maximum(dist, 0)
    d_f = np.maximum(d, 1).astype(np.float32)
    large = exact + (np.log(d_f / np.float32(exact)) / np.float32(math.log(REL_MAX_DIST / exact))
                     * np.float32(REL_BUCKETS - exact)).astype(np.int32)
    large = np.minimum(large, REL_BUCKETS - 1)
    return np.where(d < exact, d, large).astype(np.int32)


def _swa_bias(rel_bias, bucket):
    def body(rb_ref, bk_ref, o_ref):
        bk = bk_ref[...]
        for h in range(SWA_HEADS):
            t = jnp.zeros((2 * BLOCK, BLOCK), F32)
            for b in range(REL_BUCKETS):
                t = jnp.where(bk == b, rb_ref[b, h], t)
            o_ref[h] = t

    return pl.pallas_call(
        body, name="swa_bias",
        in_specs=[pl.BlockSpec(memory_space=pltpu.SMEM), pl.BlockSpec(memory_space=pltpu.VMEM)],
        out_specs=pl.BlockSpec(memory_space=pltpu.VMEM),
        out_shape=_sds((SWA_HEADS, 2 * BLOCK, BLOCK), F32),
    )(rel_bias, bucket)


def _swa_logits(q, kp, kc):
    qs = (q.astype(F32) * SCALE).astype(_MXU)
    return qs, _dot_nt(kp, qs), _dot_nt(kc, qs)


def _swa_softmax(lp, lc, bias, sink, live_prev):
    r, c = _tri_masks(BLOCK)
    in_window = r > c if live_prev is None else jnp.logical_and(r > c, live_prev)
    lp = jnp.where(in_window, lp + bias[:BLOCK, :], -jnp.inf)
    lc = jnp.where(r <= c, lc + bias[BLOCK:, :], -jnp.inf)
    m = jnp.maximum(jnp.maximum(jnp.max(lp, axis=0, keepdims=True), jnp.max(lc, axis=0, keepdims=True)), sink)
    pp = jnp.exp(lp - m)
    pc = jnp.exp(lc - m)
    ps = jnp.exp(sink - m)
    denom = _colsum(pp) + _colsum(pc) + ps
    return pp / denom, pc / denom, ps / denom


def _swa_sub(nb):
    return min(SWA_SUB, nb)


def _swa_keys(b, prev_ref, cur_ref, i):
    cur = cur_ref[0, b * BLOCK:(b + 1) * BLOCK, :]
    if b == 0:
        return prev_ref[0], cur, i > 0
    return cur_ref[0, (b - 1) * BLOCK:b * BLOCK, :], cur, None


def _swa_keys_t(b, prev_ref, cur_ref):
    cur = cur_ref[0, :, b * BLOCK:(b + 1) * BLOCK]
    return (prev_ref[0] if b == 0 else cur_ref[0, :, (b - 1) * BLOCK:b * BLOCK]), cur


SWA_PAIR = 4


def _swa_fwd(q, k, vT, bias, sink):
    S = q.shape[1]
    nb = S // BLOCK
    ns = _swa_sub(nb)
    R = ns * BLOCK
    P = SWA_PAIR

    def body(q_ref, kp_ref, kc_ref, vp_ref, vc_ref, bias_ref, sink_ref, o_ref):
        i = pl.program_id(1)
        units = [(hh, b) for hh in range(P) for b in range(ns)]
        keys = [_swa_keys(b, kp_ref, kc_ref, i) for b in range(ns)]
        vals = [_swa_keys_t(b, vp_ref, vc_ref) for b in range(ns)]
        logits = {u: _swa_logits(q_ref[u[0], u[1] * BLOCK:(u[1] + 1) * BLOCK, :], keys[u[1]][0], keys[u[1]][1])
                  for u in units}
        ws = {u: _swa_softmax(logits[u][1], logits[u][2], bias_ref[u[0]], sink_ref[u[0]][:, :1], keys[u[1]][2])
              for u in units}
        outs = {u: _dot(vals[u[1]][0], ws[u][0].astype(_MXU)) + _dot(vals[u[1]][1], ws[u][1].astype(_MXU))
                for u in units}
        for b in range(ns):
            o_ref[b * BLOCK:(b + 1) * BLOCK, :] = jnp.concatenate([outs[(hh, b)] for hh in range(P)], axis=0).T

    kvh = lambda p: (p * P) // SWA_GROUP
    prev = pl.BlockSpec((1, BLOCK, HEAD_DIM), lambda p, i: (kvh(p), jnp.maximum(i * ns - 1, 0), 0))
    cur = pl.BlockSpec((1, R, HEAD_DIM), lambda p, i: (kvh(p), i, 0))
    prev_t = pl.BlockSpec((1, HEAD_DIM, BLOCK), lambda p, i: (kvh(p), 0, jnp.maximum(i * ns - 1, 0)))
    cur_t = pl.BlockSpec((1, HEAD_DIM, R), lambda p, i: (kvh(p), 0, i))
    return pl.pallas_call(
        body, name="swa_fwd", grid=(SWA_HEADS // P, nb // ns),
        in_specs=[pl.BlockSpec((P, R, HEAD_DIM), lambda p, i: (p, i, 0)), prev, cur, prev_t, cur_t,
                  pl.BlockSpec((P, 2 * BLOCK, BLOCK), lambda p, i: (p, 0, 0)),
                  pl.BlockSpec((P, 1, BLOCK), lambda p, i: (p, 0, 0))],
        out_specs=pl.BlockSpec((R, P * HEAD_DIM), lambda p, i: (i, p)),
        out_shape=_sds((S, SWA_HEADS * HEAD_DIM), F32),
        compiler_params=_cp(("parallel", "parallel")),
    )(q, k, k, vT, vT, bias, sink)


def _swa_bwd(q, k, kT, v, bias, sink, do, grads):
    S = q.shape[1]
    nb = S // BLOCK
    ns = _swa_sub(nb)
    R = ns * BLOCK
    P = SWA_PAIR
    nw = len(grads)
    x_ins, x_outs, x_sems = _swap_io(grads)

    def body(q_ref, kp_ref, kc_ref, ktp_ref, ktc_ref, vp_ref, vc_ref, bias_ref, sink_ref, do_ref, *rest):
        dq_ref, dk_ref, dv_ref, dbias_ref, dsink_ref = rest[nw:nw + 5]
        swap = _Swap(rest[:nw], rest[nw + 5:2 * nw + 5], *rest[2 * nw + 5:])
        g = pl.program_id(1)
        i = pl.program_id(2)
        first_step = jnp.logical_and(pl.program_id(0) == 0, jnp.logical_and(g == 0, i == 0))
        last_step = jnp.logical_and(pl.program_id(0) == pl.num_programs(0) - 1,
                                    jnp.logical_and(g == pl.num_programs(1) - 1, i == pl.num_programs(2) - 1))

        @pl.when(first_step)
        def _():
            swap.start()

        @pl.when(jnp.logical_and(g == 0, i == 0))
        def _():
            dk_ref[...] = jnp.zeros_like(dk_ref)
            dv_ref[...] = jnp.zeros_like(dv_ref)

        @pl.when(i == 0)
        def _():
            dbias_ref[...] = jnp.zeros_like(dbias_ref)
            dsink_ref[...] = jnp.zeros_like(dsink_ref)

        subs = range(ns)
        units = [(hh, b) for hh in range(P) for b in subs]
        rows = [slice(b * BLOCK, (b + 1) * BLOCK) for b in subs]
        keys = [_swa_keys(b, kp_ref, kc_ref, i) for b in subs]
        keys_t = [_swa_keys_t(b, ktp_ref, ktc_ref) for b in subs]
        vals = [_swa_keys(b, vp_ref, vc_ref, i) for b in subs]
        douts = {u: do_ref[u[0], rows[u[1]], :] for u in units}
        logits = {u: _swa_logits(q_ref[u[0], rows[u[1]], :], keys[u[1]][0], keys[u[1]][1]) for u in units}
        dws = {u: (_dot_nt(vals[u[1]][0], douts[u]), _dot_nt(vals[u[1]][1], douts[u])) for u in units}
        wts, dls = {}, {}
        for hh in range(P):
            dbp = jnp.zeros((BLOCK, BLOCK), F32)
            dbc = jnp.zeros((BLOCK, BLOCK), F32)
            dsk = jnp.zeros((1, BLOCK), F32)
            for b in subs:
                u = (hh, b)
                wp, wc, ws = _swa_softmax(logits[u][1], logits[u][2], bias_ref[hh], sink_ref[hh][:, :1], keys[b][2])
                dwp, dwc = dws[u]
                delta = _colsum(wp * dwp) + _colsum(wc * dwc)
                dlp = wp * (dwp - delta)
                dlc = wc * (dwc - delta)
                dbp += dlp
                dbc += dlc
                dsk -= ws * delta
                wts[u] = (wp.astype(_MXU), wc.astype(_MXU))
                dls[u] = (dlp.astype(_MXU), dlc.astype(_MXU))
            dbias_ref[hh, :BLOCK, :] += dbp
            dbias_ref[hh, BLOCK:, :] += dbc
            dsink_ref[hh] += jnp.broadcast_to(dsk, (8, BLOCK))
        dqs = {u: (_dot(keys_t[u[1]][0], dls[u][0]) + _dot(keys_t[u[1]][1], dls[u][1])) * SCALE for u in units}
        for b in subs:
            dq_ref[rows[b], :] = jnp.concatenate([dqs[(hh, b)] for hh in range(P)], axis=0).T.astype(dq_ref.dtype)
        dk_cur = [sum(_dot(dls[(hh, b)][1], logits[(hh, b)][0]) for hh in range(P)) for b in subs]
        dv_cur = [sum(_dot(wts[(hh, b)][1], douts[(hh, b)]) for hh in range(P)) for b in subs]
        dk_prev = [sum(_dot(dls[(hh, b)][0], logits[(hh, b)][0]) for hh in range(P)) for b in subs]
        dv_prev = [sum(_dot(wts[(hh, b)][0], douts[(hh, b)]) for hh in range(P)) for b in subs]
        for b in subs:
            last = b + 1 == ns
            dk_ref[0, i * ns + b] += dk_cur[b] if last else dk_cur[b] + dk_prev[b + 1]
            dv_ref[0, i * ns + b] += dv_cur[b] if last else dv_cur[b] + dv_prev[b + 1]

        @pl.when(i > 0)
        def _():
            dk_ref[0, i * ns - 1] += dk_prev[0]
            dv_ref[0, i * ns - 1] += dv_prev[0]

        @pl.when(last_step)
        def _():
            swap.finish()

    G2 = SWA_GROUP // P
    hp = lambda kv, g, i: kv * G2 + g
    prev = pl.BlockSpec((1, BLOCK, HEAD_DIM), lambda kv, g, i: (kv, jnp.maximum(i * ns - 1, 0), 0))
    cur = pl.BlockSpec((1, R, HEAD_DIM), lambda kv, g, i: (kv, i, 0))
    prev_t = pl.BlockSpec((1, HEAD_DIM, BLOCK), lambda kv, g, i: (kv, 0, jnp.maximum(i * ns - 1, 0)))
    cur_t = pl.BlockSpec((1, HEAD_DIM, R), lambda kv, g, i: (kv, 0, i))
    qblk = pl.BlockSpec((P, R, HEAD_DIM), lambda kv, g, i: (hp(kv, g, i), i, 0))
    kvacc = pl.BlockSpec((1, nb, BLOCK, HEAD_DIM), lambda kv, g, i: (kv, 0, 0, 0))
    any_spec = pl.BlockSpec(memory_space=pl.ANY)
    res = pl.pallas_call(
        body, name="swa_bwd", grid=(SWA_KV_HEADS, G2, nb // ns),
        in_specs=[qblk, prev, cur, prev_t, cur_t, prev, cur,
                  pl.BlockSpec((P, 2 * BLOCK, BLOCK), lambda kv, g, i: (hp(kv, g, i), 0, 0)),
                  pl.BlockSpec((P, 1, BLOCK), lambda kv, g, i: (hp(kv, g, i), 0, 0)), qblk] + [any_spec] * nw,
        out_specs=[pl.BlockSpec((R, P * HEAD_DIM), lambda kv, g, i: (i, hp(kv, g, i))), kvacc, kvacc,
                   pl.BlockSpec((P, 2 * BLOCK, BLOCK), lambda kv, g, i: (hp(kv, g, i), 0, 0)),
                   pl.BlockSpec((P, 8, BLOCK), lambda kv, g, i: (hp(kv, g, i), 0, 0))] + [any_spec] * nw,
        out_shape=[_sds((S, SWA_HEADS * HEAD_DIM), _MXU), _sds((SWA_KV_HEADS, nb, BLOCK, HEAD_DIM), F32),
                   _sds((SWA_KV_HEADS, nb, BLOCK, HEAD_DIM), F32), _sds((SWA_HEADS, 2 * BLOCK, BLOCK), F32),
                   _sds((SWA_HEADS, 8, BLOCK), F32)] + x_outs,
        scratch_shapes=x_sems,
        compiler_params=_cp(("arbitrary", "arbitrary", "arbitrary")),
    )(q, k, k, kT, kT, v, v, bias, sink, do, *x_ins)
    return res[0], res[1], res[2], res[3], res[4], list(res[5:])


def _swa_small_grads(dbias, dsink, bucket):
    rows = REL_BUCKETS + 8

    def total(x):
        return jnp.sum(jnp.sum(x, axis=1, keepdims=True), axis=0, keepdims=True)

    def body(db_ref, ds_ref, bk_ref, o_ref):
        bk = bk_ref[...]
        r = lax.broadcasted_iota(jnp.int32, (rows, BLOCK), 0)
        c = lax.broadcasted_iota(jnp.int32, (rows, BLOCK), 1)
        out = jnp.zeros((rows, BLOCK), F32)
        for h in range(SWA_HEADS):
            db = db_ref[h]
            for b in range(REL_BUCKETS):
                s = total(jnp.where(bk == b, db, 0.0))
                out = jnp.where(jnp.logical_and(r == b, c == h), s, out)
            s = jnp.sum(ds_ref[h][0:1, :], axis=1, keepdims=True)
            out = jnp.where(jnp.logical_and(r == REL_BUCKETS, c == h), s, out)
        o_ref[...] = out

    vm = pl.BlockSpec(memory_space=pltpu.VMEM)
    return pl.pallas_call(body, name="swa_small_grads", in_specs=[vm, vm, vm], out_specs=vm,
                          out_shape=_sds((rows, BLOCK), F32))(dbias, dsink, bucket)


def _tile_rows(n):
    for t in (512, 352, 256, 176, 128, 64, 32, 16, 8):
        if n % t == 0:
            return t
    return n


def _cast_rows(x, dtype, name):
    R, C = x.shape
    tr = _tile_rows(R)

    def body(x_ref, o_ref):
        o_ref[...] = x_ref[...].astype(o_ref.dtype)

    return pl.pallas_call(body, name=name, grid=(R // tr,), in_specs=[_rows(tr, C)], out_specs=_rows(tr, C),
                          out_shape=_sds((R, C), dtype), compiler_params=_cp(("parallel",)))(x)


def _pair_sum(g, recv, c, name):
    n, half, C = recv.shape
    tr = _tile_rows(half)

    def body(c_ref, a_ref, b_ref, o_ref):
        o_ref[...] = (a_ref[0] + b_ref[...]).astype(o_ref.dtype)

    return pl.pallas_call(
        body, name=name,
        grid_spec=pltpu.PrefetchScalarGridSpec(
            num_scalar_prefetch=1, grid=(n, half // tr),
            in_specs=[pl.BlockSpec((1, 1, tr, C), lambda j, i, c_ref: (j, c_ref[0], i, 0)),
                      pl.BlockSpec((1, tr, C), lambda j, i, c_ref: (j, i, 0))],
            out_specs=pl.BlockSpec((1, tr, C), lambda j, i, c_ref: (j, i, 0))),
        out_shape=_sds((n, half, C), _MXU),
        compiler_params=_cp(("parallel", "parallel")))(c.reshape(1), g.reshape(n, 2, half, C), recv)


def _chip_sum(own, recv, me, name):
    n, R, C = recv.shape
    tr = _tile_rows(R)

    def body(me_ref, own_ref, recv_ref, o_ref):
        acc = None
        for j in range(n):
            term = jnp.where(me_ref[0] == j, own_ref[0], recv_ref[j]).astype(F32)
            acc = term if acc is None else acc + term
        o_ref[...] = acc

    return pl.pallas_call(
        body, name=name,
        grid_spec=pltpu.PrefetchScalarGridSpec(
            num_scalar_prefetch=1, grid=(R // tr,),
            in_specs=[pl.BlockSpec((1, tr, C), lambda i, me_ref: (me_ref[0], i, 0)),
                      pl.BlockSpec((n, tr, C), lambda i, me_ref: (0, i, 0))],
            out_specs=pl.BlockSpec((tr, C), lambda i, me_ref: (i, 0))),
        out_shape=_sds((R, C), F32), compiler_params=_cp(("parallel",)))(me.reshape(1), own, recv)


def _adamw_math(w, g, m, v):
    m = ADAM_B1 * m + (1.0 - ADAM_B1) * g
    v = ADAM_B2 * v + (1.0 - ADAM_B2) * (g * g)
    m_hat = m / (1.0 - ADAM_B1 ** ADAM_STEP)
    v_hat = v / (1.0 - ADAM_B2 ** ADAM_STEP)
    delta = -ADAM_LR * (m_hat / (jnp.sqrt(v_hat) + ADAM_EPS) + ADAM_WD * w)
    return delta, m, v


def _adamw(w, g, m, v, name):
    R, C = w.shape
    tr = _tile_rows(R)

    def body(w_ref, g_ref, m_ref, v_ref, d_ref, nm_ref, nv_ref):
        d, nm, nv = _adamw_math(w_ref[...], g_ref[...], m_ref[...], v_ref[...])
        d_ref[...] = d
        nm_ref[...] = nm
        nv_ref[...] = nv

    blk = _rows(tr, C)
    return pl.pallas_call(body, name=name, grid=(R // tr,), in_specs=[blk] * 4, out_specs=[blk] * 3,
                          out_shape=[_sds((R, C), F32)] * 3, compiler_params=_cp(("parallel",)))(w, g, m, v)


def _gather_weights(shards):
    nw = len(shards)
    ins, outs, sems = _gather_io(shards)

    def body(*refs):
        ex = _Gather(refs[:nw], refs[nw:2 * nw], *refs[2 * nw:])
        ex.start()
        ex.forward()
        ex.finish()

    any_spec = pl.BlockSpec(memory_space=pl.ANY)
    got = pl.pallas_call(body, name="gather_weights", in_specs=[any_spec] * nw, out_specs=[any_spec] * nw,
                         out_shape=outs, scratch_shapes=sems)(*ins)
    return _gather_assemble(got, shards)


def _swap_halves(grads, name):
    nw = len(grads)
    ins, outs, sems = _swap_io(grads)

    def body(*refs):
        ex = _Swap(refs[:nw], refs[nw:2 * nw], *refs[2 * nw:])
        ex.start()
        ex.finish()

    any_spec = pl.BlockSpec(memory_space=pl.ANY)
    return pl.pallas_call(body, name=name, in_specs=[any_spec] * nw, out_specs=[any_spec] * nw,
                          out_shape=outs, scratch_shapes=sems)(*ins)


def _join_halves(sums):
    nw = len(sums)

    def body(*refs):
        f_refs, out_refs = refs[:nw], refs[nw:2 * nw]
        send_sems, recv_sems = refs[2 * nw:]
        x, y, c, _ = _place()
        ws = range(nw)

        def copy(w, half_index):
            return pltpu.make_async_remote_copy(
                src_ref=f_refs[w], dst_ref=out_refs[w].at[half_index], send_sem=send_sems.at[w],
                recv_sem=recv_sems.at[w], device_id=(x, y, 1 - c), device_id_type=MESH)

        sends = [copy(w, c) for w in ws]
        for cp in sends:
            cp.start()
        for w in ws:
            copy(w, 1 - c).wait_recv()
        for cp in sends:
            cp.wait_send()

    any_spec = pl.BlockSpec(memory_space=pl.ANY)
    outs = pl.pallas_call(
        body, name="join_halves", in_specs=[any_spec] * nw, out_specs=[any_spec] * nw,
        out_shape=[_sds((2,) + f.shape, f.dtype) for f in sums],
        scratch_shapes=[pltpu.SemaphoreType.DMA((nw,)), pltpu.SemaphoreType.DMA((nw,))],
    )(*sums)
    c = lax.axis_index("c")
    return [lax.dynamic_update_slice_in_dim(o, f[None], c, axis=0).reshape(2 * f.shape[0], f.shape[1])
            for o, f in zip(outs, sums)]


def _allreduce_small(block):
    m_per, n = block.shape

    def body(x_ref, sum_ref, loss_ref, all_ref, send_sems, recv_sems, local_sem):
        x, y, c, chips = _place()
        me, sibling = (x, y, c), (x, y, 1 - c)

        def rows(px, py, pc):
            return all_ref.at[pl.ds(pl.multiple_of((4 * px + 2 * py + pc) * m_per, 8), m_per), :]

        def copy(k, blk, to, src=None):
            return pltpu.make_async_remote_copy(
                src_ref=rows(*blk) if src is None else src, dst_ref=rows(*blk), send_sem=send_sems.at[k],
                recv_sem=recv_sems.at[k], device_id=to, device_id_type=MESH)

        mine = pltpu.make_async_copy(x_ref, rows(*me), local_sem)
        mine.start()
        first = [copy(0, me, sibling, src=x_ref)]
        first += [copy(1 + j, me, (*chip, c), src=x_ref) for j, chip in enumerate(chips)]
        for cp in first:
            cp.start()
        passed = [copy(4 + j, (*chip, c), sibling) for j, chip in enumerate(chips)]
        for j, chip in enumerate(chips):
            copy(1 + j, (*chip, c), me).wait_recv()
            passed[j].start()
        copy(0, sibling, me).wait_recv()
        for j, chip in enumerate(chips):
            copy(4 + j, (*chip, 1 - c), me).wait_recv()
        for cp in first + passed:
            cp.wait_send()
        mine.wait()

        acc = all_ref[0:m_per, :]
        for d in range(1, 8):
            acc = acc + all_ref[d * m_per:(d + 1) * m_per, :]
        sum_ref[...] = acc
        tot = jnp.sum(acc[8:9, :], axis=1, keepdims=True) * (0.5 / D_MODEL)
        loss_ref[...] = jnp.broadcast_to(tot, loss_ref.shape)

    vm = pl.BlockSpec(memory_space=pltpu.VMEM)
    return pl.pallas_call(
        body, name="allreduce_small", in_specs=[vm], out_specs=[vm, vm],
        out_shape=[_sds((m_per, n), F32), _sds((8, 128), F32)],
        scratch_shapes=[pltpu.VMEM((8 * m_per, n), F32), pltpu.SemaphoreType.DMA((7,)), pltpu.SemaphoreType.DMA((7,)),
                        pltpu.SemaphoreType.DMA],
    )(block)


def _heads_rows(x, nh):
    S = x.shape[0]
    return x.reshape(S, nh, HEAD_DIM).transpose(1, 0, 2)


def _heads_cols(x, nh):
    S = x.shape[0]
    return x.reshape(S, nh, HEAD_DIM).transpose(1, 2, 0)


def _pad_row(v):
    v = v.reshape(1, -1)
    return jnp.pad(v, ((0, 0), (0, D_MODEL - v.shape[1])))


def _pack_small(ln_in_g, ln_in_b, sb_g, swa_g, sinks, rel_bias, ln1_g, ln1_b, ln2_g, ln2_b, extra):
    rows = [_pad_row(ln_in_g), _pad_row(ln_in_b), jnp.concatenate([sb_g.reshape(1, -1), swa_g.reshape(1, -1)], axis=1),
            _pad_row(jnp.concatenate([rel_bias.reshape(1, -1), sinks.reshape(1, -1)], axis=1)),
            _pad_row(ln1_g), _pad_row(ln1_b), _pad_row(ln2_g), _pad_row(ln2_b), _pad_row(extra)]
    rows.append(jnp.zeros((SMALL_ROWS - len(rows), D_MODEL), F32))
    return jnp.concatenate(rows, axis=0)


def _unpack_small(blk):
    nrb = REL_BUCKETS * SWA_HEADS
    return (blk[0], blk[1], blk[2:3, :SB_WIDTH], blk[2:3, SB_WIDTH:], blk[3:4, nrb:nrb + SWA_HEADS],
            blk[3, :nrb].reshape(REL_BUCKETS, SWA_HEADS), blk[4:5], blk[5:6], blk[6:7], blk[7:8])


def kernel(x, ln_in_g, ln_in_b, w_in, sb_norm_g, swa_norm_g, sinks, rel_bias, w_out, ln1_g, ln1_b, w_gate_up, w_down, ln2_g, ln2_b, loss_target, m_ln_in_g, m_ln_in_b, m_w_in, m_sb_norm_g, m_swa_norm_g, m_sinks, m_rel_bias, m_w_out, m_ln1_g, m_ln1_b, m_w_gate_up, m_w_down, m_ln2_g, m_ln2_b, v_ln_in_g, v_ln_in_b, v_w_in, v_sb_norm_g, v_swa_norm_g, v_sinks, v_rel_bias, v_w_out, v_ln1_g, v_ln1_b, v_w_gate_up, v_w_down, v_ln2_g, v_ln2_b):
    S = x.shape[1]
    x2 = x.reshape(S, D_MODEL)
    tgt = loss_target.reshape(S, D_MODEL)
    T = min(S, SB_TILE)
    bucket = jnp.asarray(_bucket_table().T)
    row = lambda v: v.reshape(1, -1)

    shards = [_cast_rows(w[0], _MXU, "cast_" + n) for n, w in (("w_in", w_in), ("w_out", w_out), ("w_gate_up", w_gate_up), ("w_down", w_down))]
    (w_in_sh,) = _gather_weights(shards[:1])
    w_in_f = jnp.concatenate([w_in_sh[j] for j in range(N_CHIPS)], axis=1)

    h0, h0b, kv_sw, qT_sb, kTb_sb, vTb_sb, kb_sb, vb_sb, qh_sw = _ln_in_proj(x2, row(ln_in_g), row(ln_in_b), w_in_f)
    k_sw, v_sw = kv_sw[:, :SWA_KV_WIDTH], kv_sw[:, SWA_KV_WIDTH:]
    sb_out, rsave, sb_first, (w_out_sh, w_gu_sh, w_down_sh) = _sb_fwd(qT_sb, kb_sb, vTb_sb, shards[1:])
    w_out_f = w_out_sh.reshape(D_MODEL, D_MODEL)
    w_down_f = w_down_sh.reshape(D_FF, D_MODEL)

    bias = _swa_bias(rel_bias, bucket)
    sink_rows = jnp.broadcast_to(sinks.reshape(SWA_HEADS, 1, 1), (SWA_HEADS, 1, BLOCK))
    kh_sw, vh_sw = _heads_rows(k_sw, SWA_KV_HEADS), _heads_rows(v_sw, SWA_KV_HEADS)
    swa_out = _swa_fwd(qh_sw, kh_sw, _heads_cols(v_sw, SWA_KV_HEADS), bias, sink_rows)

    pre1, merged, h1b = _mix_out(sb_out, swa_out, sb_norm_g, swa_norm_g, w_out_f, h0, ln1_g, ln1_b)
    act, silu, dsilu_up = _ffn_up(h1b, w_gu_sh)
    dp2, dp2b, dg2, db2, errsum = _ffn_down_loss(act, w_down_f, pre1, ln1_g, ln1_b, ln2_g, ln2_b, tgt)

    dgate, dup, g_w_down = _ffn_down_bwd(dp2b, w_down_f, silu, dsilu_up, act)
    g_w_gu = _matmul_tn_pair(h1b, dgate, dup, "grad_w_gate_up")
    dp1, dp1b, dg1, db1 = _ffn_up_bwd(dgate, dup, w_gu_sh, dp2, pre1, ln1_g)
    doT_sb, doh_sw, dgsb, dgsw, g_w_out = _mix_bwd(dp1b, w_out_f, sb_out, swa_out, sb_norm_g, swa_norm_g, merged)

    c = lax.axis_index("c").astype(jnp.int32)
    me = (2 * lax.axis_index("x") + lax.axis_index("y")).astype(jnp.int32)
    grads_a = [g_w_out.reshape(N_CHIPS, D_MODEL // N_CHIPS, D_MODEL), g_w_gu, g_w_down.reshape(N_CHIPS, D_FF // N_CHIPS, D_MODEL)]
    names_a = ("w_out", "w_gate_up", "w_down")
    dq_sw, dkh_sw, dvh_sw, dbias, dsink, swapped_a = _swa_bwd(qh_sw, kh_sw, _heads_cols(k_sw, SWA_KV_HEADS), vh_sw, bias,
                                                               sink_rows, doh_sw, grads_a)
    swa_small = _swa_small_grads(dbias, dsink, bucket)
    partials_a = [_pair_sum(g, r, c, "pair_sum_" + n) for g, r, n in zip(grads_a, swapped_a, names_a)]
    dq_sb, dk_sb, dv_sb, recv_a = _sb_bwd(qT_sb, kb_sb, kTb_sb, vb_sb,
                                             doT_sb, rsave, sb_first, partials_a)
    tok = lambda t, nh: t.reshape(nh, S, HEAD_DIM).transpose(1, 0, 2).reshape(S, nh * HEAD_DIM)
    dproj = [dq_sb, dk_sb, dv_sb, dq_sw,
             jnp.concatenate([tok(dkh_sw, SWA_KV_HEADS), tok(dvh_sw, SWA_KV_HEADS)], axis=1).astype(_MXU)]
    g_w_in = jnp.concatenate([_matmul_tn(h0b, d, "grad_w_in_%d" % k, D_MODEL, d.shape[1]) for k, d in enumerate(dproj)],
                             axis=1)

    cin = IN_COLS // N_CHIPS
    grads_b = [jnp.stack([g_w_in[:, j * cin:(j + 1) * cin] for j in range(N_CHIPS)])]
    partials_b = [_pair_sum(grads_b[0], _swap_halves(grads_b, "swap_halves_in")[0], c, "pair_sum_w_in")]
    grad_x, dg_in, db_in, recv_b = _in_proj_bwd(dproj, w_in_f, dp1, x2, row(ln_in_g), partials_b)
    names = ("w_in",) + names_a
    sums = [_chip_sum(p, r, me, "chip_sum_" + n) for p, r, n in zip(partials_b + partials_a, list(recv_b) + list(recv_a), names)]
    gs_in, gs_out, gs_gu, gs_down = _join_halves(sums)

    nrb = REL_BUCKETS * SWA_HEADS
    small = _pack_small(dg_in, db_in, dgsb, dgsw, swa_small[REL_BUCKETS, :SWA_HEADS],
                        swa_small[:REL_BUCKETS, :SWA_HEADS], dg1, db1, dg2, db2, errsum)
    g_small, loss_tile = _allreduce_small(small)
    loss = loss_tile[0, 0]

    big = []
    for name, w, g, m, v in (("adamw_w_in", w_in, gs_in, m_w_in, v_w_in), ("adamw_w_out", w_out, gs_out, m_w_out, v_w_out),
                             ("adamw_w_gate_up", w_gate_up, gs_gu, m_w_gate_up, v_w_gate_up),
                             ("adamw_w_down", w_down, gs_down, m_w_down, v_w_down)):
        d, nm, nv = _adamw(w[0], g, m[0], v[0], name)
        big.append((g[None], d[None], nm[None], nv[None]))
    zero = jnp.zeros((1,), F32)
    w_small = _pack_small(ln_in_g, ln_in_b, sb_norm_g, swa_norm_g, sinks, rel_bias, ln1_g, ln1_b, ln2_g, ln2_b, zero)
    m_small = _pack_small(m_ln_in_g, m_ln_in_b, m_sb_norm_g, m_swa_norm_g, m_sinks, m_rel_bias, m_ln1_g, m_ln1_b,
                          m_ln2_g, m_ln2_b, zero)
    v_small = _pack_small(v_ln_in_g, v_ln_in_b, v_sb_norm_g, v_swa_norm_g, v_sinks, v_rel_bias, v_ln1_g, v_ln1_b,
                          v_ln2_g, v_ln2_b, zero)
    small_out = [_unpack_small(t) for t in (g_small,) + tuple(_adamw(w_small, g_small, m_small, v_small, "adamw_small"))]

    def kind(k):
        s = small_out[k]
        return [s[0], s[1], big[0][k], s[2], s[3], s[4], s[5], big[1][k], s[6], s[7], big[2][k], big[3][k], s[8], s[9]]

    return (loss, grad_x.reshape(1, S, D_MODEL), *kind(0), *kind(1), *kind(2), *kind(3))
```
